```python
import jax, jax.numpy as jnp
from jax import lax
import numpy as np

D_MODEL = 1024
BATCH = 8
SEQ = 4096
DEPTH = 4

N_HEADS = 8
QK_NOPE = 64
QK_ROPE = 32
QK_HEAD = QK_NOPE + QK_ROPE
V_HEAD = 64
Q_LORA = 384
KV_LORA = 256
ATTN_WIDTH = N_HEADS * V_HEAD
CONV_WIDTH = D_MODEL - ATTN_WIDTH
CONV_TAPS = 3
IN_COLS = Q_LORA + KV_LORA + QK_ROPE + 3 * CONV_WIDTH
D_FF = 4 * D_MODEL
PLE_DIM = 256
ROPE_THETA = 10000.0
Q_BLOCK = 128
EPS = 1e-6
MAX_POS_OFFSET = 1024

kernel_name = "hybrid_mla_shortconv_trunk"


def rmsnorm(x, g):
    xf = x.astype(jnp.float32)
    y = xf * lax.rsqrt(jnp.mean(xf * xf, axis=-1, keepdims=True) + EPS)
    return (y * g.astype(jnp.float32)).astype(x.dtype)


def rope_tables(positions):
    inv_freq = 1.0 / (ROPE_THETA ** (jnp.arange(0, QK_ROPE, 2, dtype=jnp.float32) / QK_ROPE))
    ang = positions.astype(jnp.float32)[..., None] * inv_freq
    return jnp.cos(ang)[:, :, None, :], jnp.sin(ang)[:, :, None, :]


def apply_rope(x, cos, sin):
    half = QK_ROPE // 2
    x1 = x[..., :half].astype(jnp.float32)
    x2 = x[..., half:].astype(jnp.float32)
    return jnp.concatenate([x1 * cos - x2 * sin, x2 * cos + x1 * sin], axis=-1).astype(x.dtype)


def causal_block_attention(q, k, v):
    b, s = q.shape[0], q.shape[1]
    scale = QK_HEAD ** -0.5
    n_blocks = s // Q_BLOCK
    k_idx = jnp.arange(s)

    def one_block(i):
        start = i * Q_BLOCK
        qb = lax.dynamic_slice_in_dim(q, start, Q_BLOCK, axis=1)
        sc = jnp.einsum('bqhd,bkhd->bhqk', qb, k, preferred_element_type=jnp.float32) * scale
        q_idx = start + jnp.arange(Q_BLOCK)
        sc = jnp.where(k_idx[None, :] <= q_idx[:, None], sc, -jnp.inf)
        pr = jax.nn.softmax(sc, axis=-1).astype(v.dtype)
        return jnp.einsum('bhqk,bkhd->bqhd', pr, v)

    out = lax.map(one_block, jnp.arange(n_blocks))
    return jnp.moveaxis(out, 0, 1).reshape(b, s, N_HEADS * V_HEAD)


def mla_group(q_lat, kv_lat, k_pe, cos, sin, g_q_lat, w_uq, g_kv_lat, w_ukv,
              g_qn_nope, g_qn_rope, g_kn_nope, g_kn_rope):
    b, s = q_lat.shape[0], q_lat.shape[1]
    q = (rmsnorm(q_lat, g_q_lat) @ w_uq).reshape(b, s, N_HEADS, QK_HEAD)
    kv = (rmsnorm(kv_lat, g_kv_lat) @ w_ukv).reshape(b, s, N_HEADS, QK_NOPE + V_HEAD)
    k_nope, v = kv[..., :QK_NOPE], kv[..., QK_NOPE:]
    q_nope = rmsnorm(q[..., :QK_NOPE], g_qn_nope)
    q_pe = apply_rope(rmsnorm(q[..., QK_NOPE:], g_qn_rope), cos, sin)
    k_nope = rmsnorm(k_nope, g_kn_nope)
    k_pe = apply_rope(rmsnorm(k_pe.reshape(b, s, 1, QK_ROPE), g_kn_rope), cos, sin)
    qf = jnp.concatenate([q_nope, q_pe], axis=-1)
    kf = jnp.concatenate([k_nope, jnp.broadcast_to(k_pe, (b, s, N_HEADS, QK_ROPE))], axis=-1)
    return causal_block_attention(qf, kf, v)


def short_conv_group(gate_b, gate_c, x_in, conv_w):
    u = gate_c * x_in
    s = u.shape[1]
    up = jnp.pad(u, ((0, 0), (CONV_TAPS - 1, 0), (0, 0)))
    y = sum(conv_w[j] * up[:, CONV_TAPS - 1 - j: CONV_TAPS - 1 - j + s] for j in range(CONV_TAPS))
    return gate_b * y


def _fwd_setup_inputs(seed: int = 0) -> dict:
    key = jax.random.key(seed)
    ks = jax.random.split(key, 24)

    def w(k, shape, fan_in):
        return jax.random.normal(k, (DEPTH,) + shape, jnp.float32) * fan_in ** -0.5

    def gain(k, n):
        return 1.0 + 0.02 * jax.random.normal(k, (DEPTH, n), jnp.float32)

    x = jax.random.normal(ks[0], (BATCH, SEQ, D_MODEL), jnp.float32)
    p = jax.random.normal(ks[1], (DEPTH, BATCH, SEQ, PLE_DIM), jnp.float32)
    offs = jax.random.randint(ks[2], (BATCH, 1), 0, MAX_POS_OFFSET, dtype=jnp.int32)
    positions = (offs + jnp.arange(SEQ, dtype=jnp.int32)[None, :]).astype(jnp.int32)
    return {
        "x": x,
        "p": p,
        "positions": positions,
        "g_mix": gain(ks[3], D_MODEL),
        "w_in": w(ks[4], (D_MODEL, IN_COLS), D_MODEL),
        "g_q_lat": gain(ks[5], Q_LORA),
        "w_uq": w(ks[6], (Q_LORA, N_HEADS * QK_HEAD), Q_LORA),
        "g_kv_lat": gain(ks[7], KV_LORA),
        "w_ukv": w(ks[8], (KV_LORA, N_HEADS * (QK_NOPE + V_HEAD)), KV_LORA),
        "g_qn_nope": gain(ks[9], QK_NOPE),
        "g_qn_rope": gain(ks[10], QK_ROPE),
        "g_kn_nope": gain(ks[11], QK_NOPE),
        "g_kn_rope": gain(ks[12], QK_ROPE),
        "conv_w": w(ks[13], (CONV_TAPS, CONV_WIDTH), CONV_TAPS),
        "g_out_attn": gain(ks[14], ATTN_WIDTH),
        "g_out_conv": gain(ks[15], CONV_WIDTH),
        "w_o": w(ks[16], (D_MODEL, D_MODEL), D_MODEL),
        "g_mlp": gain(ks[17], D_MODEL),
        "w_up": w(ks[18], (D_MODEL, D_FF), D_MODEL),
        "w_down": w(ks[19], (D_FF, D_MODEL), D_FF),
        "g_ple": gain(ks[20], D_MODEL),
        "w_ple_gate": w(ks[21], (D_MODEL, D_MODEL), D_MODEL),
        "w_ple": w(ks[22], (PLE_DIM, D_MODEL), PLE_DIM),
    }


def _fwd_reference(x, p, positions, g_mix, w_in, g_q_lat, w_uq, g_kv_lat, w_ukv,
              g_qn_nope, g_qn_rope, g_kn_nope, g_kn_rope, conv_w, g_out_attn,
              g_out_conv, w_o, g_mlp, w_up, w_down, g_ple, w_ple_gate, w_ple):
    cos, sin = rope_tables(positions)
    o1 = Q_LORA
    o2 = o1 + KV_LORA
    o3 = o2 + QK_ROPE
    o4 = o3 + CONV_WIDTH
    o5 = o4 + CONV_WIDTH
    for i in range(DEPTH):
        h = rmsnorm(x, g_mix[i])
        z = h @ w_in[i]
        attn = mla_group(z[..., :o1], z[..., o1:o2], z[..., o2:o3], cos, sin,
                         g_q_lat[i], w_uq[i], g_kv_lat[i], w_ukv[i],
                         g_qn_nope[i], g_qn_rope[i], g_kn_nope[i], g_kn_rope[i])
        conv = short_conv_group(z[..., o3:o4], z[..., o4:o5], z[..., o5:], conv_w[i])
        mixed = jnp.concatenate([rmsnorm(attn, g_out_attn[i]), rmsnorm(conv, g_out_conv[i])], axis=-1)
        x = x + mixed @ w_o[i]
        h2 = rmsnorm(x, g_mlp[i])
        x = x + jnp.square(jax.nn.relu(h2 @ w_up[i])) @ w_down[i]
        gate = jax.nn.sigmoid(rmsnorm(x, g_ple[i]) @ w_ple_gate[i])
        x = x + gate * (p[i] @ w_ple[i])
    return x


import jax as _jax
import jax.numpy as _jnp

TWIN_FORMAT = 'train_step'
FWD_PARAMS = ['x', 'p', 'positions', 'g_mix', 'w_in', 'g_q_lat', 'w_uq', 'g_kv_lat', 'w_ukv', 'g_qn_nope', 'g_qn_rope', 'g_kn_nope', 'g_kn_rope', 'conv_w', 'g_out_attn', 'g_out_conv', 'w_o', 'g_mlp', 'w_up', 'w_down', 'g_ple', 'w_ple_gate', 'w_ple']
TWIN_WEIGHTS = ['g_mix', 'w_in', 'g_q_lat', 'w_uq', 'g_kv_lat', 'w_ukv', 'g_qn_nope', 'g_qn_rope', 'g_kn_nope', 'g_kn_rope', 'conv_w', 'g_out_attn', 'g_out_conv', 'w_o', 'g_mlp', 'w_up', 'w_down', 'g_ple', 'w_ple_gate', 'w_ple']
TWIN_DIFF_INPUT = 'x'
TWIN_INPUTS = ['x', 'p', 'positions', 'g_mix', 'w_in', 'g_q_lat', 'w_uq', 'g_kv_lat', 'w_ukv', 'g_qn_nope', 'g_qn_rope', 'g_kn_nope', 'g_kn_rope', 'conv_w', 'g_out_attn', 'g_out_conv', 'w_o', 'g_mlp', 'w_up', 'w_down', 'g_ple', 'w_ple_gate', 'w_ple', 'loss_target', 'm_g_mix', 'm_w_in', 'm_g_q_lat', 'm_w_uq', 'm_g_kv_lat', 'm_w_ukv', 'm_g_qn_nope', 'm_g_qn_rope', 'm_g_kn_nope', 'm_g_kn_rope', 'm_conv_w', 'm_g_out_attn', 'm_g_out_conv', 'm_w_o', 'm_g_mlp', 'm_w_up', 'm_w_down', 'm_g_ple', 'm_w_ple_gate', 'm_w_ple', 'v_g_mix', 'v_w_in', 'v_g_q_lat', 'v_w_uq', 'v_g_kv_lat', 'v_w_ukv', 'v_g_qn_nope', 'v_g_qn_rope', 'v_g_kn_nope', 'v_g_kn_rope', 'v_conv_w', 'v_g_out_attn', 'v_g_out_conv', 'v_w_o', 'v_g_mlp', 'v_w_up', 'v_w_down', 'v_g_ple', 'v_w_ple_gate', 'v_w_ple']
TWIN_OUTPUTS = ['loss', 'grad_x', 'grad_g_mix', 'grad_w_in', 'grad_g_q_lat', 'grad_w_uq', 'grad_g_kv_lat', 'grad_w_ukv', 'grad_g_qn_nope', 'grad_g_qn_rope', 'grad_g_kn_nope', 'grad_g_kn_rope', 'grad_conv_w', 'grad_g_out_attn', 'grad_g_out_conv', 'grad_w_o', 'grad_g_mlp', 'grad_w_up', 'grad_w_down', 'grad_g_ple', 'grad_w_ple_gate', 'grad_w_ple', 'delta_g_mix', 'delta_w_in', 'delta_g_q_lat', 'delta_w_uq', 'delta_g_kv_lat', 'delta_w_ukv', 'delta_g_qn_nope', 'delta_g_qn_rope', 'delta_g_kn_nope', 'delta_g_kn_rope', 'delta_conv_w', 'delta_g_out_attn', 'delta_g_out_conv', 'delta_w_o', 'delta_g_mlp', 'delta_w_up', 'delta_w_down', 'delta_g_ple', 'delta_w_ple_gate', 'delta_w_ple', 'new_m_g_mix', 'new_m_w_in', 'new_m_g_q_lat', 'new_m_w_uq', 'new_m_g_kv_lat', 'new_m_w_ukv', 'new_m_g_qn_nope', 'new_m_g_qn_rope', 'new_m_g_kn_nope', 'new_m_g_kn_rope', 'new_m_conv_w', 'new_m_g_out_attn', 'new_m_g_out_conv', 'new_m_w_o', 'new_m_g_mlp', 'new_m_w_up', 'new_m_w_down', 'new_m_g_ple', 'new_m_w_ple_gate', 'new_m_w_ple', 'new_v_g_mix', 'new_v_w_in', 'new_v_g_q_lat', 'new_v_w_uq', 'new_v_g_kv_lat', 'new_v_w_ukv', 'new_v_g_qn_nope', 'new_v_g_qn_rope', 'new_v_g_kn_nope', 'new_v_g_kn_rope', 'new_v_conv_w', 'new_v_g_out_attn', 'new_v_g_out_conv', 'new_v_w_o', 'new_v_g_mlp', 'new_v_w_up', 'new_v_w_down', 'new_v_g_ple', 'new_v_w_ple_gate', 'new_v_w_ple']
TWIN_LEAF_KINDS = {'loss': 'loss', 'grad_x': 'grad_x', 'grad_g_mix': 'grad_w', 'grad_w_in': 'grad_w', 'grad_g_q_lat': 'grad_w', 'grad_w_uq': 'grad_w', 'grad_g_kv_lat': 'grad_w', 'grad_w_ukv': 'grad_w', 'grad_g_qn_nope': 'grad_w', 'grad_g_qn_rope': 'grad_w', 'grad_g_kn_nope': 'grad_w', 'grad_g_kn_rope': 'grad_w', 'grad_conv_w': 'grad_w', 'grad_g_out_attn': 'grad_w', 'grad_g_out_conv': 'grad_w', 'grad_w_o': 'grad_w', 'grad_g_mlp': 'grad_w', 'grad_w_up': 'grad_w', 'grad_w_down': 'grad_w', 'grad_g_ple': 'grad_w', 'grad_w_ple_gate': 'grad_w', 'grad_w_ple': 'grad_w', 'delta_g_mix': 'delta_w', 'delta_w_in': 'delta_w', 'delta_g_q_lat': 'delta_w', 'delta_w_uq': 'delta_w', 'delta_g_kv_lat': 'delta_w', 'delta_w_ukv': 'delta_w', 'delta_g_qn_nope': 'delta_w', 'delta_g_qn_rope': 'delta_w', 'delta_g_kn_nope': 'delta_w', 'delta_g_kn_rope': 'delta_w', 'delta_conv_w': 'delta_w', 'delta_g_out_attn': 'delta_w', 'delta_g_out_conv': 'delta_w', 'delta_w_o': 'delta_w', 'delta_g_mlp': 'delta_w', 'delta_w_up': 'delta_w', 'delta_w_down': 'delta_w', 'delta_g_ple': 'delta_w', 'delta_w_ple_gate': 'delta_w', 'delta_w_ple': 'delta_w', 'new_m_g_mix': 'new_m', 'new_m_w_in': 'new_m', 'new_m_g_q_lat': 'new_m', 'new_m_w_uq': 'new_m', 'new_m_g_kv_lat': 'new_m', 'new_m_w_ukv': 'new_m', 'new_m_g_qn_nope': 'new_m', 'new_m_g_qn_rope': 'new_m', 'new_m_g_kn_nope': 'new_m', 'new_m_g_kn_rope': 'new_m', 'new_m_conv_w': 'new_m', 'new_m_g_out_attn': 'new_m', 'new_m_g_out_conv': 'new_m', 'new_m_w_o': 'new_m', 'new_m_g_mlp': 'new_m', 'new_m_w_up': 'new_m', 'new_m_w_down': 'new_m', 'new_m_g_ple': 'new_m', 'new_m_w_ple_gate': 'new_m', 'new_m_w_ple': 'new_m', 'new_v_g_mix': 'new_v', 'new_v_w_in': 'new_v', 'new_v_g_q_lat': 'new_v', 'new_v_w_uq': 'new_v', 'new_v_g_kv_lat': 'new_v', 'new_v_w_ukv': 'new_v', 'new_v_g_qn_nope': 'new_v', 'new_v_g_qn_rope': 'new_v', 'new_v_g_kn_nope': 'new_v', 'new_v_g_kn_rope': 'new_v', 'new_v_conv_w': 'new_v', 'new_v_g_out_attn': 'new_v', 'new_v_g_out_conv': 'new_v', 'new_v_w_o': 'new_v', 'new_v_g_mlp': 'new_v', 'new_v_w_up': 'new_v', 'new_v_w_down': 'new_v', 'new_v_g_ple': 'new_v', 'new_v_w_ple_gate': 'new_v', 'new_v_w_ple': 'new_v'}


def _forward(args):
    return _fwd_reference(*[args[k] for k in FWD_PARAMS])


def _output_shape():
    out = _jax.eval_shape(lambda: _forward(_fwd_setup_inputs(0)))
    return out.shape, out.dtype

N_MICROBATCH = 1
ADAM_LR = 0.001
ADAM_B1 = 0.9
ADAM_B2 = 0.999
ADAM_EPS = 1e-08
ADAM_WD = 0.01
ADAM_STEP = 10
PER_EXAMPLE_BATCH_AXIS = {'x': 0, 'p': 1, 'positions': 0, 'loss_target': 0}
SHARED_INPUTS = []
_WEIGHT_DTYPES = {'g_mix': _jnp.float32, 'w_in': _jnp.float32, 'g_q_lat': _jnp.float32, 'w_uq': _jnp.float32, 'g_kv_lat': _jnp.float32, 'w_ukv': _jnp.float32, 'g_qn_nope': _jnp.float32, 'g_qn_rope': _jnp.float32, 'g_kn_nope': _jnp.float32, 'g_kn_rope': _jnp.float32, 'conv_w': _jnp.float32, 'g_out_attn': _jnp.float32, 'g_out_conv': _jnp.float32, 'w_o': _jnp.float32, 'g_mlp': _jnp.float32, 'w_up': _jnp.float32, 'w_down': _jnp.float32, 'g_ple': _jnp.float32, 'w_ple_gate': _jnp.float32, 'w_ple': _jnp.float32}
MOMENT_SCALE = {'g_mix': 3.303652e+01, 'w_in': 2.285469e+01, 'g_q_lat': 2.538581e+00, 'w_uq': 1.823193e+00, 'g_kv_lat': 6.241906e+01, 'w_ukv': 3.257452e+01, 'g_qn_nope': 7.327508e+00, 'g_qn_rope': 3.490857e+00, 'g_kn_nope': 7.318968e+00, 'g_kn_rope': 6.731848e+00, 'conv_w': 2.484591e+00, 'g_out_attn': 6.085636e+01, 'g_out_conv': 4.189395e+01, 'w_o': 3.318091e+01, 'g_mlp': 1.007053e+02, 'w_up': 1.111999e+01, 'w_down': 4.009632e+01, 'g_ple': 1.374755e+00, 'w_ple_gate': 1.017262e+00, 'w_ple': 7.851031e-01}


def _to_microbatches(a, axis):
    t = _jnp.moveaxis(a, axis, 0)
    t = t.reshape((N_MICROBATCH, t.shape[0] // N_MICROBATCH) + t.shape[1:])
    return _jnp.moveaxis(t, 1, axis + 1)


def setup_inputs(seed: int = 0) -> dict:
    inp = _fwd_setup_inputs(seed)
    key = _jax.random.fold_in(_jax.random.key(seed), 7919)
    shape, _ = _output_shape()
    out = dict(inp)
    out["loss_target"] = _jax.random.normal(_jax.random.fold_in(key, 0), shape, _jnp.float32)
    for i, name in enumerate(TWIN_WEIGHTS):
        w = inp[name].astype(_jnp.float32)
        if MOMENT_SCALE is None:
            s = _jnp.sqrt(_jnp.mean(_jnp.square(w)) + 1e-30)
        else:
            s = MOMENT_SCALE[name]
        km, kv = _jax.random.split(_jax.random.fold_in(key, i + 1))
        out[name] = w
        out["m_" + name] = s * _jax.random.normal(km, w.shape, _jnp.float32)
        out["v_" + name] = (s * s) * _jax.random.uniform(kv, w.shape, _jnp.float32, 0.5, 1.5)
    if N_MICROBATCH > 1:
        for name, axis in PER_EXAMPLE_BATCH_AXIS.items():
            out[name] = _to_microbatches(out[name], axis)
    return {'x': out['x'], 'p': out['p'], 'positions': out['positions'], 'g_mix': out['g_mix'], 'w_in': out['w_in'], 'g_q_lat': out['g_q_lat'], 'w_uq': out['w_uq'], 'g_kv_lat': out['g_kv_lat'], 'w_ukv': out['w_ukv'], 'g_qn_nope': out['g_qn_nope'], 'g_qn_rope': out['g_qn_rope'], 'g_kn_nope': out['g_kn_nope'], 'g_kn_rope': out['g_kn_rope'], 'conv_w': out['conv_w'], 'g_out_attn': out['g_out_attn'], 'g_out_conv': out['g_out_conv'], 'w_o': out['w_o'], 'g_mlp': out['g_mlp'], 'w_up': out['w_up'], 'w_down': out['w_down'], 'g_ple': out['g_ple'], 'w_ple_gate': out['w_ple_gate'], 'w_ple': out['w_ple'], 'loss_target': out['loss_target'], 'm_g_mix': out['m_g_mix'], 'm_w_in': out['m_w_in'], 'm_g_q_lat': out['m_g_q_lat'], 'm_w_uq': out['m_w_uq'], 'm_g_kv_lat': out['m_g_kv_lat'], 'm_w_ukv': out['m_w_ukv'], 'm_g_qn_nope': out['m_g_qn_nope'], 'm_g_qn_rope': out['m_g_qn_rope'], 'm_g_kn_nope': out['m_g_kn_nope'], 'm_g_kn_rope': out['m_g_kn_rope'], 'm_conv_w': out['m_conv_w'], 'm_g_out_attn': out['m_g_out_attn'], 'm_g_out_conv': out['m_g_out_conv'], 'm_w_o': out['m_w_o'], 'm_g_mlp': out['m_g_mlp'], 'm_w_up': out['m_w_up'], 'm_w_down': out['m_w_down'], 'm_g_ple': out['m_g_ple'], 'm_w_ple_gate': out['m_w_ple_gate'], 'm_w_ple': out['m_w_ple'], 'v_g_mix': out['v_g_mix'], 'v_w_in': out['v_w_in'], 'v_g_q_lat': out['v_g_q_lat'], 'v_w_uq': out['v_w_uq'], 'v_g_kv_lat': out['v_g_kv_lat'], 'v_w_ukv': out['v_w_ukv'], 'v_g_qn_nope': out['v_g_qn_nope'], 'v_g_qn_rope': out['v_g_qn_rope'], 'v_g_kn_nope': out['v_g_kn_nope'], 'v_g_kn_rope': out['v_g_kn_rope'], 'v_conv_w': out['v_conv_w'], 'v_g_out_attn': out['v_g_out_attn'], 'v_g_out_conv': out['v_g_out_conv'], 'v_w_o': out['v_w_o'], 'v_g_mlp': out['v_g_mlp'], 'v_w_up': out['v_w_up'], 'v_w_down': out['v_w_down'], 'v_g_ple': out['v_g_ple'], 'v_w_ple_gate': out['v_w_ple_gate'], 'v_w_ple': out['v_w_ple']}


def _loss(weights, diff, rest, loss_target):
    with _jax.named_scope("forward"):
        args = {**rest, TWIN_DIFF_INPUT: diff, **{k: w.astype(_WEIGHT_DTYPES[k]) for k, w in weights.items()}}
        y = _forward(args)
    with _jax.named_scope("loss_head"):
        err = _jnp.square(y.astype(_jnp.float32) - loss_target)
        return 0.5 * _jnp.sum(_jnp.mean(err, axis=-1)) if err.ndim else 0.5 * err


def _adamw(w, g, m, v):
    m = ADAM_B1 * m + (1.0 - ADAM_B1) * g
    v = ADAM_B2 * v + (1.0 - ADAM_B2) * _jnp.square(g)
    m_hat = m / (1.0 - ADAM_B1 ** ADAM_STEP)
    v_hat = v / (1.0 - ADAM_B2 ** ADAM_STEP)
    delta = -ADAM_LR * (m_hat / (_jnp.sqrt(v_hat) + ADAM_EPS) + ADAM_WD * w)
    return delta, m, v


def reference(x, p, positions, g_mix, w_in, g_q_lat, w_uq, g_kv_lat, w_ukv, g_qn_nope, g_qn_rope, g_kn_nope, g_kn_rope, conv_w, g_out_attn, g_out_conv, w_o, g_mlp, w_up, w_down, g_ple, w_ple_gate, w_ple, loss_target, m_g_mix, m_w_in, m_g_q_lat, m_w_uq, m_g_kv_lat, m_w_ukv, m_g_qn_nope, m_g_qn_rope, m_g_kn_nope, m_g_kn_rope, m_conv_w, m_g_out_attn, m_g_out_conv, m_w_o, m_g_mlp, m_w_up, m_w_down, m_g_ple, m_w_ple_gate, m_w_ple, v_g_mix, v_w_in, v_g_q_lat, v_w_uq, v_g_kv_lat, v_w_ukv, v_g_qn_nope, v_g_qn_rope, v_g_kn_nope, v_g_kn_rope, v_conv_w, v_g_out_attn, v_g_out_conv, v_w_o, v_g_mlp, v_w_up, v_w_down, v_g_ple, v_w_ple_gate, v_w_ple):
    given = dict(x=x, p=p, positions=positions, g_mix=g_mix, w_in=w_in, g_q_lat=g_q_lat, w_uq=w_uq, g_kv_lat=g_kv_lat, w_ukv=w_ukv, g_qn_nope=g_qn_nope, g_qn_rope=g_qn_rope, g_kn_nope=g_kn_nope, g_kn_rope=g_kn_rope, conv_w=conv_w, g_out_attn=g_out_attn, g_out_conv=g_out_conv, w_o=w_o, g_mlp=g_mlp, w_up=w_up, w_down=w_down, g_ple=g_ple, w_ple_gate=w_ple_gate, w_ple=w_ple, loss_target=loss_target, m_g_mix=m_g_mix, m_w_in=m_w_in, m_g_q_lat=m_g_q_lat, m_w_uq=m_w_uq, m_g_kv_lat=m_g_kv_lat, m_w_ukv=m_w_ukv, m_g_qn_nope=m_g_qn_nope, m_g_qn_rope=m_g_qn_rope, m_g_kn_nope=m_g_kn_nope, m_g_kn_rope=m_g_kn_rope, m_conv_w=m_conv_w, m_g_out_attn=m_g_out_attn, m_g_out_conv=m_g_out_conv, m_w_o=m_w_o, m_g_mlp=m_g_mlp, m_w_up=m_w_up, m_w_down=m_w_down, m_g_ple=m_g_ple, m_w_ple_gate=m_w_ple_gate, m_w_ple=m_w_ple, v_g_mix=v_g_mix, v_w_in=v_w_in, v_g_q_lat=v_g_q_lat, v_w_uq=v_w_uq, v_g_kv_lat=v_g_kv_lat, v_w_ukv=v_w_ukv, v_g_qn_nope=v_g_qn_nope, v_g_qn_rope=v_g_qn_rope, v_g_kn_nope=v_g_kn_nope, v_g_kn_rope=v_g_kn_rope, v_conv_w=v_conv_w, v_g_out_attn=v_g_out_attn, v_g_out_conv=v_g_out_conv, v_w_o=v_w_o, v_g_mlp=v_g_mlp, v_w_up=v_w_up, v_w_down=v_w_down, v_g_ple=v_g_ple, v_w_ple_gate=v_w_ple_gate, v_w_ple=v_w_ple)
    weights = {n: given[n] for n in TWIN_WEIGHTS}
    shared = {n: given[n] for n in SHARED_INPUTS}
    per_example = {n: given[n] for n in ['x', 'p', 'positions']}
    grad_fn = _jax.value_and_grad(_loss, argnums=(0, 1))

    def one_microbatch(ex, loss_target):
        ex = dict(ex)
        diff = ex.pop(TWIN_DIFF_INPUT)
        return grad_fn(weights, diff, {**shared, **ex}, loss_target)

    if N_MICROBATCH == 1:
        loss, (grad_w, grad_x) = one_microbatch(per_example, given["loss_target"])
    else:
        def body(carry, xs):
            loss_sum, grad_sum = carry
            l_k, (gw_k, gx_k) = one_microbatch(xs[0], xs[1])
            with _jax.named_scope("update"):
                return (loss_sum + l_k, _jax.tree.map(_jnp.add, grad_sum, gw_k)), gx_k

        init = (_jnp.zeros((), _jnp.float32), _jax.tree.map(_jnp.zeros_like, weights))
        (loss, grad_w), grad_x = _jax.lax.scan(body, init, (per_example, given["loss_target"]))
    with _jax.named_scope("update"):
        delta_w, new_m, new_v = {}, {}, {}
        for n in TWIN_WEIGHTS:
            delta_w[n], new_m[n], new_v[n] = _adamw(weights[n], grad_w[n], given["m_" + n], given["v_" + n])
    return (loss, grad_x, *[grad_w[n] for n in TWIN_WEIGHTS], *[delta_w[n] for n in TWIN_WEIGHTS],
            *[new_m[n] for n in TWIN_WEIGHTS], *[new_v[n] for n in TWIN_WEIGHTS])
```

```python
import functools

import jax
import jax.numpy as jnp
from jax import lax
from jax.experimental import pallas as pl
from jax.experimental.pallas import tpu as pltpu

F32 = jnp.float32
BF16 = jnp.bfloat16
MESH = pl.DeviceIdType.MESH

D_MODEL = 1024
N_HEADS = 8
QK_NOPE = 64
QK_ROPE = 32
QK_HEAD = QK_NOPE + QK_ROPE
V_HEAD = 64
Q_LORA = 384
KV_LORA = 256
ATTN_WIDTH = N_HEADS * V_HEAD
CONV_WIDTH = 512
CONV_TAPS = 3
D_FF = 4096
PLE_DIM = 256
ROPE_THETA = 10000.0
EPS = 1e-6
ATT_SCALE = QK_HEAD ** -0.5

ADAM_LR = 0.001
ADAM_B1 = 0.9
ADAM_B2 = 0.999
ADAM_EPS = 1e-08
ADAM_WD = 0.01
ADAM_STEP = 10

LANES = 128
SUBLANES = 8
HEAD_PAD = LANES
HP = N_HEADS * HEAD_PAD
ROPE_LO = QK_NOPE
ROPE_MID = QK_NOPE + QK_ROPE // 2
ROPE_HI = QK_NOPE + QK_ROPE
VMEM_LIMIT = 56 * 1024 * 1024

Z_Q, Z_KV, Z_KPE, Z_GB, Z_GC, Z_XIN = 0, 384, 640, 768, 1280, 1792
Z_COLS = 2304
Z_LAT = Z_KPE

ROW_TILE = 512
ATT_TILE = 256
MM_TM, MM_TN, MM_TK = 1024, 1024, 512

N_CHIPS = 4
N_DEV = 8
FLAT_COLS = 1024

BIG = (
    ("w_in", 1024, 2208, 1), ("w_uq", 384, 768, 1), ("w_ukv", 256, 1024, 1), ("w_o", 1024, 1024, 0),
    ("w_up", 1024, 4096, 1), ("w_down", 4096, 1024, 0), ("w_ple_gate", 1024, 1024, 0), ("w_ple", 256, 1024, 1),
)
SMALL = (
    ("g_mix", 1024), ("g_q_lat", 384), ("g_kv_lat", 256), ("g_qn_nope", 64), ("g_qn_rope", 32), ("g_kn_nope", 64),
    ("g_kn_rope", 32), ("g_out_attn", 512), ("g_out_conv", 512), ("g_mlp", 1024), ("g_ple", 1024),
)
WEIGHT_ORDER = ("g_mix", "w_in", "g_q_lat", "w_uq", "g_kv_lat", "w_ukv", "g_qn_nope", "g_qn_rope", "g_kn_nope",
                "g_kn_rope", "conv_w", "g_out_attn", "g_out_conv", "w_o", "g_mlp", "w_up", "w_down", "g_ple",
                "w_ple_gate", "w_ple")


def _pallas(body, **kw):
    return pl.pallas_call(body, **kw)


def _cparams(*sem):
    return pltpu.CompilerParams(dimension_semantics=sem, vmem_limit_bytes=VMEM_LIMIT)


def _tile(dim, pref, unit=LANES):
    if dim <= pref:
        return dim
    t = (pref // unit) * unit
    while t > unit and dim % t:
        t -= unit
    assert dim % t == 0, (dim, pref)
    return t


def _pad128(n):
    return -(-n // LANES) * LANES


_DIMS = {"nn": (((1,), (0,)), ((), ())), "nt": (((1,), (1,)), ((), ())), "tn": (((0,), (0,)), ((), ()))}


def _mm(a, b, *, mode="nn", res=None, aux=None, epi=None, out_dtype=F32, name):
    if mode == "nn":
        (M, K), (K2, N) = a.shape, b.shape
    elif mode == "nt":
        (M, K), (N, K2) = a.shape, b.shape
    else:
        (K, M), (K2, N) = a.shape, b.shape
    assert K == K2, (a.shape, b.shape, mode)
    tm, tn, tk = _tile(M, MM_TM), _tile(N, MM_TN), _tile(K, MM_TK)
    nk = K // tk
    a_spec = pl.BlockSpec((tk, tm), lambda i, j, k: (k, i)) if mode == "tn" else pl.BlockSpec((tm, tk), lambda i, j, k: (i, k))
    b_spec = pl.BlockSpec((tn, tk), lambda i, j, k: (j, k)) if mode == "nt" else pl.BlockSpec((tk, tn), lambda i, j, k: (k, j))
    mn_spec = pl.BlockSpec((tm, tn), lambda i, j, k: (i, j))
    extra = [t for t in (res, aux) if t is not None]
    dims = _DIMS[mode]

    def body(*refs):
        a_ref, b_ref = refs[0], refs[1]
        extra_refs = refs[2:2 + len(extra)]
        out_refs = refs[2 + len(extra):-1]
        acc = refs[-1]
        k = pl.program_id(2)

        @pl.when(k == 0)
        def _():
            acc[...] = jnp.zeros_like(acc)

        acc[...] += lax.dot_general(a_ref[...].astype(BF16), b_ref[...].astype(BF16), dims, preferred_element_type=F32)

        @pl.when(k == nk - 1)
        def _():
            r = acc[...]
            if res is not None:
                r = r + extra_refs[0][...]
            if epi == "relu2":
                out_refs[0][...] = r
                t = jnp.maximum(r, 0.0)
                out_refs[1][...] = (t * t).astype(BF16)
            elif epi == "drelu2":
                out_refs[0][...] = (r * (2.0 * jnp.maximum(extra_refs[-1][...], 0.0))).astype(out_dtype)
            else:
                out_refs[0][...] = r.astype(out_dtype)

    if epi == "relu2":
        out_shape = (jax.ShapeDtypeStruct((M, N), F32), jax.ShapeDtypeStruct((M, N), BF16))
        out_specs = (mn_spec, mn_spec)
    else:
        out_shape = jax.ShapeDtypeStruct((M, N), out_dtype)
        out_specs = mn_spec
    return _pallas(
        body, name=name, grid=(M // tm, N // tn, nk),
        in_specs=[a_spec, b_spec] + [mn_spec] * len(extra), out_specs=out_specs, out_shape=out_shape,
        scratch_shapes=[pltpu.VMEM((tm, tn), F32)],
        compiler_params=_cparams("parallel", "parallel", "arbitrary"),
    )(a, b, *extra)


def _rows(ts, d, col=0):
    return pl.BlockSpec((ts, d), lambda i: (i, col))


def _gain(d):
    return pl.BlockSpec((1, d), lambda i: (0, 0))


def _accum(d):
    return pl.BlockSpec((SUBLANES, d), lambda i: (0, 0))


def _accumulate(ref, val):
    i = pl.program_id(0)

    @pl.when(i == 0)
    def _():
        ref[...] = jnp.zeros_like(ref)

    ref[...] += jnp.broadcast_to(jnp.sum(val, axis=0, keepdims=True), ref.shape)


def _rinv(x, n):
    return lax.rsqrt(jnp.sum(x * x, axis=-1, keepdims=True) / n + EPS)


def _norm_bwd(x, dy, g, n):
    r = _rinv(x, n)
    xhat = x * r
    dyg = dy * g
    dx = r * (dyg - xhat * (jnp.sum(dyg * xhat, axis=-1, keepdims=True) / n))
    return dx, dy * xhat


def _rms_fwd(x, g, name):
    S, D = x.shape
    ts = _tile(S, ROW_TILE, SUBLANES)

    def body(x_ref, g_ref, h_ref):
        xv = x_ref[...]
        h_ref[...] = (xv * _rinv(xv, D) * g_ref[...]).astype(BF16)

    return _pallas(body, name=name, grid=(S // ts,), in_specs=[_rows(ts, D), _gain(D)], out_specs=_rows(ts, D),
                   out_shape=jax.ShapeDtypeStruct((S, D), BF16), compiler_params=_cparams("parallel"))(x, g)


def _rms_bwd(dy, x, g, dres, name, want_bf16=True):
    S, D = x.shape
    ts = _tile(S, ROW_TILE, SUBLANES)

    def body(dy_ref, x_ref, g_ref, dres_ref, *outs):
        dx, dgc = _norm_bwd(x_ref[...], dy_ref[...], g_ref[...], D)
        dx = dx + dres_ref[...]
        outs[0][...] = dx
        if want_bf16:
            outs[1][...] = dx.astype(BF16)
        _accumulate(outs[-1], dgc)

    out_shape = [jax.ShapeDtypeStruct((S, D), F32)] + ([jax.ShapeDtypeStruct((S, D), BF16)] if want_bf16 else []) + [
        jax.ShapeDtypeStruct((SUBLANES, D), F32)]
    out_specs = [_rows(ts, D)] * (2 if want_bf16 else 1) + [_accum(D)]
    return _pallas(body, name=name, grid=(S // ts,), in_specs=[_rows(ts, D), _rows(ts, D), _gain(D), _rows(ts, D)],
                   out_specs=out_specs, out_shape=out_shape, compiler_params=_cparams("arbitrary"))(dy, x, g, dres)


def _lat_fwd(z, gq, gkv):
    S = z.shape[0]
    ts = _tile(S, ROW_TILE, SUBLANES)

    def body(z_ref, gq_ref, gkv_ref, q_ref, kv_ref):
        zq = z_ref[:, Z_Q:Z_KV]
        zkv = z_ref[:, Z_KV:Z_KPE]
        q_ref[...] = (zq * _rinv(zq, Q_LORA) * gq_ref[...]).astype(BF16)
        kv_ref[...] = (zkv * _rinv(zkv, KV_LORA) * gkv_ref[...]).astype(BF16)

    return _pallas(body, name="lat_fwd", grid=(S // ts,), in_specs=[_rows(ts, Z_LAT), _gain(Q_LORA), _gain(KV_LORA)],
                   out_specs=[_rows(ts, Q_LORA), _rows(ts, KV_LORA)],
                   out_shape=[jax.ShapeDtypeStruct((S, Q_LORA), BF16), jax.ShapeDtypeStruct((S, KV_LORA), BF16)],
                   compiler_params=_cparams("parallel"))(z, gq, gkv)


def _lat_bwd(dq, dkv, z, gq, gkv):
    S = z.shape[0]
    ts = _tile(S, ROW_TILE, SUBLANES)

    def body(dq_ref, dkv_ref, z_ref, gq_ref, gkv_ref, dlat_ref, dgq_ref, dgkv_ref):
        dxq, cq = _norm_bwd(z_ref[:, Z_Q:Z_KV], dq_ref[...], gq_ref[...], Q_LORA)
        dxkv, ckv = _norm_bwd(z_ref[:, Z_KV:Z_KPE], dkv_ref[...], gkv_ref[...], KV_LORA)
        dlat_ref[:, Z_Q:Z_KV] = dxq.astype(BF16)
        dlat_ref[:, Z_KV:Z_KPE] = dxkv.astype(BF16)
        _accumulate(dgq_ref, cq)
        _accumulate(dgkv_ref, ckv)

    return _pallas(body, name="lat_bwd", grid=(S // ts,),
                   in_specs=[_rows(ts, Q_LORA), _rows(ts, KV_LORA), _rows(ts, Z_LAT), _gain(Q_LORA), _gain(KV_LORA)],
                   out_specs=[_rows(ts, Z_LAT), _accum(Q_LORA), _accum(KV_LORA)],
                   out_shape=[jax.ShapeDtypeStruct((S, Z_LAT), BF16), jax.ShapeDtypeStruct((SUBLANES, Q_LORA), F32),
                              jax.ShapeDtypeStruct((SUBLANES, KV_LORA), F32)],
                   compiler_params=_cparams("arbitrary"))(dq, dkv, z, gq, gkv)


def _head_masks():
    lane = lax.broadcasted_iota(jnp.int32, (1, HEAD_PAD), 1)
    return lane < ROPE_LO, (lane >= ROPE_LO) & (lane < ROPE_HI), lane < ROPE_MID


def _seg_sum(t, m_n, m_r):
    sn = jnp.sum(jnp.where(m_n, t, 0.0), axis=-1, keepdims=True) / QK_NOPE
    sr = jnp.sum(jnp.where(m_r, t, 0.0), axis=-1, keepdims=True) / QK_ROPE
    return jnp.where(m_n, sn, sr)


def _rot_half(y, lo):
    half = QK_ROPE // 2
    return jnp.where(lo, -pltpu.roll(y, HEAD_PAD - half, 1), pltpu.roll(y, half, 1))


def _qk_fwd(q_raw, kv_raw, z, cs, sn, gq, gk):
    S = q_raw.shape[0]
    ts = _tile(S, ROW_TILE, SUBLANES)

    def body(q_ref, k_ref, v_ref, kpe_ref, cs_ref, sn_ref, gq_ref, gk_ref, qf_ref, kf_ref, vb_ref):
        m_n, m_r, lo = _head_masks()
        cos, sin, gqv, gkv = cs_ref[...], sn_ref[...], gq_ref[...], gk_ref[...]

        def norm(x, g):
            return x * lax.rsqrt(_seg_sum(x * x, m_n, m_r) + EPS) * g

        def rope(y):
            return y * cos + _rot_half(y, lo) * sin

        kr = rope(norm(kpe_ref[...], gkv))
        for h in range(N_HEADS):
            sl = slice(h * HEAD_PAD, (h + 1) * HEAD_PAD)
            qf_ref[:, sl] = rope(norm(q_ref[:, sl], gqv)).astype(BF16)
            kf_ref[:, sl] = (norm(k_ref[:, sl], gkv) + kr).astype(BF16)
            vb_ref[:, sl] = v_ref[:, sl].astype(BF16)

    hd = jax.ShapeDtypeStruct((S, HP), BF16)
    return _pallas(body, name="qk_fwd", grid=(S // ts,),
                   in_specs=[_rows(ts, HP), _rows(ts, HP, 0), _rows(ts, HP, 1), _rows(ts, HEAD_PAD, Z_KPE // HEAD_PAD),
                             _rows(ts, HEAD_PAD), _rows(ts, HEAD_PAD), _gain(HEAD_PAD), _gain(HEAD_PAD)],
                   out_specs=[_rows(ts, HP)] * 3, out_shape=[hd, hd, hd],
                   compiler_params=_cparams("parallel"))(q_raw, kv_raw, kv_raw, z, cs, sn, gq, gk)


def _qk_bwd(dqf, dkf, dv, q_raw, kv_raw, z, cs, sn, gq, gk):
    S = q_raw.shape[0]
    ts = _tile(S, ROW_TILE, SUBLANES)

    def body(dqf_ref, dkf_ref, dv_ref, q_ref, k_ref, kpe_ref, cs_ref, sn_ref, gq_ref, gk_ref,
             dq_ref, dkv_ref, dkpe_ref, dgq_ref, dgk_ref):
        m_n, m_r, lo = _head_masks()
        cos, sin, gqv, gkv = cs_ref[...], sn_ref[...], gq_ref[...], gk_ref[...]

        def rope_t(w):
            return w * cos - jnp.where(m_r, _rot_half(w * sin, lo), 0.0)

        def norm_bwd(x, dy, g):
            r = lax.rsqrt(_seg_sum(x * x, m_n, m_r) + EPS)
            xhat = x * r
            dyg = dy * g
            return r * (dyg - xhat * _seg_sum(dyg * xhat, m_n, m_r)), dy * xhat

        accq = jnp.zeros((ts, HEAD_PAD), F32)
        acck = jnp.zeros((ts, HEAD_PAD), F32)
        dkr = jnp.zeros((ts, HEAD_PAD), F32)
        for h in range(N_HEADS):
            sl = slice(h * HEAD_PAD, (h + 1) * HEAD_PAD)
            dx, c = norm_bwd(q_ref[:, sl], rope_t(dqf_ref[:, sl]), gqv)
            dq_ref[:, sl] = dx.astype(BF16)
            accq = accq + c
            dk = dkf_ref[:, sl]
            dkr = dkr + jnp.where(m_r, dk, 0.0)
            dx, c = norm_bwd(k_ref[:, sl], jnp.where(m_n, dk, 0.0), gkv)
            dkv_ref[:, sl] = dx.astype(BF16)
            acck = acck + c
            dkv_ref[:, HP + h * HEAD_PAD:HP + (h + 1) * HEAD_PAD] = dv_ref[:, sl].astype(BF16)
        dx, c = norm_bwd(kpe_ref[...], rope_t(dkr), gkv)
        dkpe_ref[...] = dx.astype(BF16)
        _accumulate(dgq_ref, accq)
        _accumulate(dgk_ref, acck + c)

    return _pallas(body, name="qk_bwd", grid=(S // ts,),
                   in_specs=[_rows(ts, HP)] * 4 + [_rows(ts, HP, 0), _rows(ts, HEAD_PAD, Z_KPE // HEAD_PAD),
                                                   _rows(ts, HEAD_PAD), _rows(ts, HEAD_PAD), _gain(HEAD_PAD), _gain(HEAD_PAD)],
                   out_specs=[_rows(ts, HP), _rows(ts, 2 * HP), _rows(ts, HEAD_PAD), _accum(HEAD_PAD), _accum(HEAD_PAD)],
                   out_shape=[jax.ShapeDtypeStruct((S, HP), BF16), jax.ShapeDtypeStruct((S, 2 * HP), BF16),
                              jax.ShapeDtypeStruct((S, HEAD_PAD), BF16), jax.ShapeDtypeStruct((SUBLANES, HEAD_PAD), F32),
                              jax.ShapeDtypeStruct((SUBLANES, HEAD_PAD), F32)],
                   compiler_params=_cparams("arbitrary"))(dqf, dkf, dv, q_raw, kv_raw, z, cs, sn, gq, gk)


def _causal(t):
    row = lax.broadcasted_iota(jnp.int32, (t, t), 0)
    col = lax.broadcasted_iota(jnp.int32, (t, t), 1)
    return col <= row


def _attn_fwd(qf, kf, vb):
    S = qf.shape[0]
    t = _tile(S, ATT_TILE, SUBLANES)

    def body(q_ref, k_ref, v_ref, o_ref, lse_ref):
        qi = pl.program_id(1)
        q = q_ref[...]

        def step(kj, carry, masked):
            m, l, acc = carry
            rows = pl.ds(pl.multiple_of(kj * t, t), t)
            s = lax.dot_general(q, k_ref[rows, :], _DIMS["nt"], preferred_element_type=F32) * ATT_SCALE
            if masked:
                s = jnp.where(_causal(t), s, -jnp.inf)
            m_new = jnp.maximum(m, jnp.max(s, axis=-1, keepdims=True))
            alpha = jnp.exp(m - m_new)
            p = jnp.exp(s - m_new)
            l = alpha * l + jnp.sum(p, axis=-1, keepdims=True)
            acc = alpha * acc + jnp.dot(p.astype(BF16), v_ref[rows, :], preferred_element_type=F32)
            return m_new, l, acc

        init = (jnp.full((t, 1), -jnp.inf, F32), jnp.zeros((t, 1), F32), jnp.zeros((t, HEAD_PAD), F32))
        carry = lax.fori_loop(0, qi, lambda kj, c: step(kj, c, False), init)
        m, l, acc = step(qi, carry, True)
        o_ref[...] = acc / l
        lse_ref[...] = jnp.broadcast_to(m + jnp.log(l), (t, HEAD_PAD))

    tile = pl.BlockSpec((t, HEAD_PAD), lambda h, i: (i, h))
    full = pl.BlockSpec((S, HEAD_PAD), lambda h, i: (0, h))
    return _pallas(body, name="attn_fwd", grid=(N_HEADS, S // t), in_specs=[tile, full, full],
                   out_specs=[tile, pl.BlockSpec((None, t, HEAD_PAD), lambda h, i: (h, i, 0))],
                   out_shape=[jax.ShapeDtypeStruct((S, HP), F32), jax.ShapeDtypeStruct((N_HEADS, S, HEAD_PAD), F32)],
                   compiler_params=_cparams("parallel", "arbitrary"))(qf, kf, vb)


def _attn_bwd(qf, kf, vb, o, do, lse):
    S = qf.shape[0]
    t = _tile(S, ATT_TILE, SUBLANES)
    nq = S // t

    def body(q_ref, k_ref, v_ref, o_ref, do_ref, lse_ref, dq_ref, dk_ref, dv_ref):
        kj = pl.program_id(1)

        @pl.when(kj == 0)
        def _():
            dq_ref[...] = jnp.zeros_like(dq_ref)

        dk_ref[...] = jnp.zeros_like(dk_ref)
        dv_ref[...] = jnp.zeros_like(dv_ref)
        k = k_ref[...]
        v = v_ref[...]

        def step(qi, masked):
            rows = pl.ds(pl.multiple_of(qi * t, t), t)
            q = q_ref[rows, :]
            dof = do_ref[rows, :]
            dob = dof.astype(BF16)
            s = lax.dot_general(q, k, _DIMS["nt"], preferred_element_type=F32) * ATT_SCALE
            p = jnp.exp(s - lse_ref[rows, :][:, :1])
            if masked:
                p = jnp.where(_causal(t), p, 0.0)
            dv_ref[...] += lax.dot_general(p.astype(BF16), dob, _DIMS["tn"], preferred_element_type=F32)
            dp = lax.dot_general(dob, v, _DIMS["nt"], preferred_element_type=F32)
            delta = jnp.sum(dof * o_ref[rows, :], axis=-1, keepdims=True)
            ds = (p * (dp - delta) * ATT_SCALE).astype(BF16)
            dk_ref[...] += lax.dot_general(ds, q, _DIMS["tn"], preferred_element_type=F32)
            dq_ref[rows, :] += jnp.dot(ds, k, preferred_element_type=F32)

        step(kj, True)

        def rest(qi, carry):
            step(qi, False)
            return carry

        lax.fori_loop(kj + 1, nq, rest, 0)

    tile = pl.BlockSpec((t, HEAD_PAD), lambda h, j: (j, h))
    full = pl.BlockSpec((S, HEAD_PAD), lambda h, j: (0, h))
    hd = jax.ShapeDtypeStruct((S, HP), F32)
    return _pallas(body, name="attn_bwd", grid=(N_HEADS, nq),
                   in_specs=[full, tile, tile, full, full, pl.BlockSpec((None, S, HEAD_PAD), lambda h, j: (h, 0, 0))],
                   out_specs=[full, tile, tile], out_shape=[hd, hd, hd],
                   compiler_params=_cparams("arbitrary", "arbitrary"))(qf, kf, vb, o, do, lse)


def _shift_down(u, j, row):
    return jnp.where(row >= j, pltpu.roll(u, j, 0), 0.0)


def _shift_up(u, j, row, s):
    return jnp.where(row < s - j, pltpu.roll(u, s - j, 0), 0.0)


def _conv_cols(s, first_tile):
    return pl.BlockSpec((s, LANES), lambda cb: (0, first_tile + cb))


def _conv_fwd(z, cw8):
    S = z.shape[0]

    def body(gb_ref, gc_ref, xin_ref, w_ref, out_ref):
        row = lax.broadcasted_iota(jnp.int32, (S, LANES), 0)
        u = gc_ref[...] * xin_ref[...]
        y = w_ref[0:1, :] * u
        for j in range(1, CONV_TAPS):
            y = y + w_ref[j:j + 1, :] * _shift_down(u, j, row)
        out_ref[...] = gb_ref[...] * y

    return _pallas(body, name="conv_fwd", grid=(CONV_WIDTH // LANES,),
                   in_specs=[_conv_cols(S, Z_GB // LANES), _conv_cols(S, Z_GC // LANES), _conv_cols(S, Z_XIN // LANES),
                             pl.BlockSpec((SUBLANES, LANES), lambda cb: (0, cb))],
                   out_specs=_conv_cols(S, 0), out_shape=jax.ShapeDtypeStruct((S, CONV_WIDTH), F32),
                   compiler_params=_cparams("parallel"))(z, z, z, cw8)


def _conv_bwd(dconv, z, cw8):
    S = z.shape[0]

    def body(d_ref, gb_ref, gc_ref, xin_ref, w_ref, dgb_ref, dgc_ref, dxin_ref, dw_ref):
        row = lax.broadcasted_iota(jnp.int32, (S, LANES), 0)
        gc, xin, d = gc_ref[...], xin_ref[...], d_ref[...]
        u = gc * xin
        dy = d * gb_ref[...]
        y = w_ref[0:1, :] * u
        du = w_ref[0:1, :] * dy
        dw = [jnp.sum(dy * u, axis=0, keepdims=True)]
        for j in range(1, CONV_TAPS):
            uj = _shift_down(u, j, row)
            y = y + w_ref[j:j + 1, :] * uj
            du = du + w_ref[j:j + 1, :] * _shift_up(dy, j, row, S)
            dw.append(jnp.sum(dy * uj, axis=0, keepdims=True))
        dgb_ref[...] = (d * y).astype(BF16)
        dgc_ref[...] = (du * xin).astype(BF16)
        dxin_ref[...] = (du * gc).astype(BF16)
        tap = lax.broadcasted_iota(jnp.int32, (SUBLANES, LANES), 0)
        dw_ref[...] = sum(jnp.where(tap == j, dw[j], 0.0) for j in range(CONV_TAPS))

    col = _conv_cols(S, 0)
    sd = jax.ShapeDtypeStruct((S, CONV_WIDTH), BF16)
    return _pallas(body, name="conv_bwd", grid=(CONV_WIDTH // LANES,),
                   in_specs=[col, _conv_cols(S, Z_GB // LANES), _conv_cols(S, Z_GC // LANES), _conv_cols(S, Z_XIN // LANES),
                             pl.BlockSpec((SUBLANES, LANES), lambda cb: (0, cb))],
                   out_specs=[col, col, col, pl.BlockSpec((SUBLANES, LANES), lambda cb: (0, cb))],
                   out_shape=[sd, sd, sd, jax.ShapeDtypeStruct((SUBLANES, CONV_WIDTH), F32)],
                   compiler_params=_cparams("parallel"))(dconv, z, z, z, cw8)


def _mix_fwd(o, conv, ga, gc):
    S = o.shape[0]
    ts = _tile(S, ROW_TILE, SUBLANES)

    def body(o_ref, c_ref, ga_ref, gc_ref, out_ref):
        ov, cv = o_ref[...], c_ref[...]
        out_ref[:, :HP] = (ov * _rinv(ov, ATTN_WIDTH) * ga_ref[...]).astype(BF16)
        out_ref[:, HP:] = (cv * _rinv(cv, CONV_WIDTH) * gc_ref[...]).astype(BF16)

    return _pallas(body, name="mix_fwd", grid=(S // ts,),
                   in_specs=[_rows(ts, HP), _rows(ts, CONV_WIDTH), _gain(HP), _gain(CONV_WIDTH)],
                   out_specs=_rows(ts, HP + CONV_WIDTH), out_shape=jax.ShapeDtypeStruct((S, HP + CONV_WIDTH), BF16),
                   compiler_params=_cparams("parallel"))(o, conv, ga, gc)


def _mix_bwd(dmixed, o, conv, ga, gc):
    S = o.shape[0]
    ts = _tile(S, ROW_TILE, SUBLANES)

    def body(d_ref, o_ref, c_ref, ga_ref, gc_ref, do_ref, dc_ref, dga_ref, dgc_ref):
        dx, ca = _norm_bwd(o_ref[...], d_ref[:, :HP], ga_ref[...], ATTN_WIDTH)
        do_ref[...] = dx
        dx, cc = _norm_bwd(c_ref[...], d_ref[:, HP:], gc_ref[...], CONV_WIDTH)
        dc_ref[...] = dx
        _accumulate(dga_ref, ca)
        _accumulate(dgc_ref, cc)

    return _pallas(body, name="mix_bwd", grid=(S // ts,),
                   in_specs=[_rows(ts, HP + CONV_WIDTH), _rows(ts, HP), _rows(ts, CONV_WIDTH), _gain(HP), _gain(CONV_WIDTH)],
                   out_specs=[_rows(ts, HP), _rows(ts, CONV_WIDTH), _accum(HP), _accum(CONV_WIDTH)],
                   out_shape=[jax.ShapeDtypeStruct((S, HP), F32), jax.ShapeDtypeStruct((S, CONV_WIDTH), F32),
                              jax.ShapeDtypeStruct((SUBLANES, HP), F32), jax.ShapeDtypeStruct((SUBLANES, CONV_WIDTH), F32)],
                   compiler_params=_cparams("arbitrary"))(dmixed, o, conv, ga, gc)


def _ple_fwd(x, gl, pe):
    S, D = x.shape
    ts = _tile(S, ROW_TILE, SUBLANES)

    def body(x_ref, gl_ref, pe_ref, out_ref):
        out_ref[...] = x_ref[...] + jax.nn.sigmoid(gl_ref[...]) * pe_ref[...]

    return _pallas(body, name="ple_fwd", grid=(S // ts,), in_specs=[_rows(ts, D)] * 3, out_specs=_rows(ts, D),
                   out_shape=jax.ShapeDtypeStruct((S, D), F32), compiler_params=_cparams("parallel"))(x, gl, pe)


def _ple_bwd(dx, gl, pe):
    S, D = dx.shape
    ts = _tile(S, ROW_TILE, SUBLANES)

    def body(dx_ref, gl_ref, pe_ref, dpe_ref, dgl_ref):
        d = dx_ref[...]
        gate = jax.nn.sigmoid(gl_ref[...])
        dpe_ref[...] = (d * gate).astype(BF16)
        dgl_ref[...] = (d * pe_ref[...] * (gate * (1.0 - gate))).astype(BF16)

    sd = jax.ShapeDtypeStruct((S, D), BF16)
    return _pallas(body, name="ple_bwd", grid=(S // ts,), in_specs=[_rows(ts, D)] * 3, out_specs=[_rows(ts, D)] * 2,
                   out_shape=[sd, sd], compiler_params=_cparams("parallel"))(dx, gl, pe)


def _loss_grad(y, target):
    S, D = y.shape
    ts = _tile(S, ROW_TILE, SUBLANES)

    def body(y_ref, t_ref, dy_ref, sq_ref):
        e = y_ref[...] - t_ref[...]
        dy_ref[...] = e / D

        @pl.when(pl.program_id(0) == 0)
        def _():
            sq_ref[...] = jnp.zeros_like(sq_ref)

        sq_ref[...] += jnp.broadcast_to(jnp.sum(jnp.sum(e * e, axis=1, keepdims=True), axis=0, keepdims=True), sq_ref.shape)

    return _pallas(body, name="loss_grad", grid=(S // ts,), in_specs=[_rows(ts, D)] * 2,
                   out_specs=[_rows(ts, D), _accum(LANES)],
                   out_shape=[jax.ShapeDtypeStruct((S, D), F32), jax.ShapeDtypeStruct((SUBLANES, LANES), F32)],
                   compiler_params=_cparams("arbitrary"))(y, target)


def _adamw(w, g, m, v, name):
    R, C = w.shape
    tr = _tile(R, ROW_TILE, SUBLANES)

    def body(w_ref, g_ref, m_ref, v_ref, d_ref, m2_ref, v2_ref):
        gv = g_ref[...]
        m2 = ADAM_B1 * m_ref[...] + (1.0 - ADAM_B1) * gv
        v2 = ADAM_B2 * v_ref[...] + (1.0 - ADAM_B2) * (gv * gv)
        m_hat = m2 / (1.0 - ADAM_B1 ** ADAM_STEP)
        v_hat = v2 / (1.0 - ADAM_B2 ** ADAM_STEP)
        d_ref[...] = -ADAM_LR * (m_hat / (jnp.sqrt(v_hat) + ADAM_EPS) + ADAM_WD * w_ref[...])
        m2_ref[...] = m2
        v2_ref[...] = v2

    sd = jax.ShapeDtypeStruct((R, C), F32)
    spec = pl.BlockSpec((tr, C), lambda i: (i, 0))
    return _pallas(body, name=name, grid=(R // tr,), in_specs=[spec] * 4, out_specs=[spec] * 3, out_shape=[sd, sd, sd],
                   compiler_params=_cparams("parallel"))(w, g, m, v)


def _place():
    return lax.axis_index("x"), lax.axis_index("y"), lax.axis_index("c")


def _other_chips(x, y):
    return [(1 - x, y), (x, 1 - y), (1 - x, 1 - y)]


HBM = pl.BlockSpec(memory_space=pl.ANY)
VMEM_SPEC = pl.BlockSpec(memory_space=pltpu.VMEM)


def _allgather_weights(wflat):
    R, C = wflat.shape
    H = R // 2

    def body(w_ref, out_ref, send_sems, recv_sems, local_sem):
        x, y, c = _place()
        me, sibling = (x, y, c), (x, y, 1 - c)
        chips = _other_chips(x, y)
        mine, theirs = pl.ds(c * H, H), pl.ds((1 - c) * H, H)

        def copy(k, src, dst, to):
            return pltpu.make_async_remote_copy(src_ref=src, dst_ref=dst, send_sem=send_sems.at[k], recv_sem=recv_sems.at[k],
                                                device_id=to, device_id_type=MESH)

        own = pltpu.make_async_copy(w_ref, out_ref.at[2 * x + y], local_sem)
        own.start()
        first = [copy(k, w_ref.at[mine], out_ref.at[2 * x + y, mine], (cx, cy, c)) for k, (cx, cy) in enumerate(chips)]
        for cp in first:
            cp.start()
        passed = []
        for k, (cx, cy) in enumerate(chips):
            blk = out_ref.at[2 * cx + cy, mine]
            copy(k, blk, blk, me).wait_recv()
            passed.append(copy(3 + k, blk, blk, sibling))
            passed[-1].start()
        for k, (cx, cy) in enumerate(chips):
            blk = out_ref.at[2 * cx + cy, theirs]
            copy(3 + k, blk, blk, me).wait_recv()
        for cp in first + passed:
            cp.wait_send()
        own.wait()

    return _pallas(body, name="allgather_weights", in_specs=[HBM], out_specs=HBM,
                   out_shape=jax.ShapeDtypeStruct((N_CHIPS, R, C), wflat.dtype),
                   scratch_shapes=[pltpu.SemaphoreType.DMA((6,)), pltpu.SemaphoreType.DMA((6,)), pltpu.SemaphoreType.DMA],
                   compiler_params=pltpu.CompilerParams(has_side_effects=True))(wflat)


def _swap_halves(g):
    n, R, C = g.shape
    H = R // 2

    def body(g_ref, out_ref, send_sem, recv_sem):
        x, y, c = _place()
        cp = pltpu.make_async_remote_copy(src_ref=g_ref.at[:, pl.ds((1 - c) * H, H)], dst_ref=out_ref, send_sem=send_sem,
                                          recv_sem=recv_sem, device_id=(x, y, 1 - c), device_id_type=MESH)
        cp.start()
        cp.wait()

    return _pallas(body, name="grad_swap_halves", in_specs=[HBM], out_specs=HBM,
                   out_shape=jax.ShapeDtypeStruct((n, H, C), g.dtype),
                   scratch_shapes=[pltpu.SemaphoreType.DMA, pltpu.SemaphoreType.DMA],
                   compiler_params=pltpu.CompilerParams(has_side_effects=True))(g)


def _add_half(g, r1, c_arr):
    n, R, C = g.shape
    H = R // 2
    th = _tile(H, 736, 16)
    nb = H // th

    def body(c_ref, g_ref, r_ref, out_ref):
        out_ref[...] = (g_ref[...].astype(F32) + r_ref[...].astype(F32)).astype(BF16)

    grid_spec = pltpu.PrefetchScalarGridSpec(
        num_scalar_prefetch=1, grid=(n, nb),
        in_specs=[pl.BlockSpec((1, th, C), lambda s, i, c_ref: (s, c_ref[0] * nb + i, 0)),
                  pl.BlockSpec((1, th, C), lambda s, i, c_ref: (s, i, 0))],
        out_specs=pl.BlockSpec((1, th, C), lambda s, i, c_ref: (s, i, 0)))
    return _pallas(body, name="grad_add_half", grid_spec=grid_spec, out_shape=jax.ShapeDtypeStruct((n, H, C), BF16),
                   compiler_params=_cparams("parallel", "parallel"))(c_arr, g, r1)


def _exchange_chips(p):
    n, H, C = p.shape

    def body(p_ref, out_ref, send_sems, recv_sems):
        x, y, c = _place()
        cps = [pltpu.make_async_remote_copy(src_ref=p_ref.at[2 * cx + cy], dst_ref=out_ref.at[k], send_sem=send_sems.at[k],
                                            recv_sem=recv_sems.at[k], device_id=(cx, cy, c), device_id_type=MESH)
               for k, (cx, cy) in enumerate(_other_chips(x, y))]
        for cp in cps:
            cp.start()
        for cp in cps:
            cp.wait()

    return _pallas(body, name="grad_exchange_chips", in_specs=[HBM], out_specs=HBM,
                   out_shape=jax.ShapeDtypeStruct((n - 1, H, C), p.dtype),
                   scratch_shapes=[pltpu.SemaphoreType.DMA((3,)), pltpu.SemaphoreType.DMA((3,))],
                   compiler_params=pltpu.CompilerParams(has_side_effects=True))(p)


def _add_chips(p, r2, chip_arr):
    n, H, C = p.shape
    th = _tile(H, 736, 16)

    def body(j_ref, p_ref, r_ref, out_ref):
        s = p_ref[0].astype(F32)
        for k in range(N_CHIPS - 1):
            s = s + r_ref[k].astype(F32)
        out_ref[...] = s

    grid_spec = pltpu.PrefetchScalarGridSpec(
        num_scalar_prefetch=1, grid=(H // th,),
        in_specs=[pl.BlockSpec((1, th, C), lambda i, j_ref: (j_ref[0], i, 0)),
                  pl.BlockSpec((N_CHIPS - 1, th, C), lambda i, j_ref: (0, i, 0))],
        out_specs=pl.BlockSpec((th, C), lambda i, j_ref: (i, 0)))
    return _pallas(body, name="grad_add_chips", grid_spec=grid_spec, out_shape=jax.ShapeDtypeStruct((H, C), F32),
                   compiler_params=_cparams("parallel"))(chip_arr, p, r2)


def _join_halves(fh):
    H, C = fh.shape

    def body(f_ref, out_ref, send_sem, recv_sem, local_sem):
        x, y, c = _place()
        mine = out_ref.at[pl.ds(c * H, H)]
        own = pltpu.make_async_copy(f_ref, mine, local_sem)
        own.start()
        cp = pltpu.make_async_remote_copy(src_ref=f_ref, dst_ref=mine, send_sem=send_sem, recv_sem=recv_sem,
                                          device_id=(x, y, 1 - c), device_id_type=MESH)
        cp.start()
        cp.wait()
        own.wait()

    return _pallas(body, name="grad_join_halves", in_specs=[HBM], out_specs=HBM,
                   out_shape=jax.ShapeDtypeStruct((2 * H, C), fh.dtype),
                   scratch_shapes=[pltpu.SemaphoreType.DMA, pltpu.SemaphoreType.DMA, pltpu.SemaphoreType.DMA],
                   compiler_params=pltpu.CompilerParams(has_side_effects=True))(fh)


def _allgather_small(v, name):
    R, C = v.shape

    def body(v_ref, out_ref, send_sems, recv_sems):
        x, y, c = _place()
        me = 4 * x + 2 * y + c
        out_ref[me] = v_ref[...]
        cps = []
        for k in range(1, N_DEV):
            peer = (1 - x if k & 4 else x, 1 - y if k & 2 else y, 1 - c if k & 1 else c)
            cps.append(pltpu.make_async_remote_copy(src_ref=v_ref, dst_ref=out_ref.at[me], send_sem=send_sems.at[k - 1],
                                                    recv_sem=recv_sems.at[k - 1], device_id=peer, device_id_type=MESH))
        for cp in cps:
            cp.start()
        for cp in cps:
            cp.wait()

    return _pallas(body, name=name, in_specs=[VMEM_SPEC], out_specs=VMEM_SPEC,
                   out_shape=jax.ShapeDtypeStruct((N_DEV, R, C), v.dtype),
                   scratch_shapes=[pltpu.SemaphoreType.DMA((N_DEV - 1,)), pltpu.SemaphoreType.DMA((N_DEV - 1,))],
                   compiler_params=pltpu.CompilerParams(has_side_effects=True))(v)


def _sum_devices(g):
    n, R, C = g.shape

    def body(g_ref, out_ref):
        s = g_ref[0]
        for d in range(1, n):
            s = s + g_ref[d]
        out_ref[...] = s

    return _pallas(body, name="sum_devices", in_specs=[VMEM_SPEC], out_specs=VMEM_SPEC,
                   out_shape=jax.ShapeDtypeStruct((R, C), g.dtype))(g)


def _pad_heads(a, width):
    lead = a.shape[:-1]
    a = a.reshape(lead + (N_HEADS, width))
    a = jnp.pad(a, [(0, 0)] * len(lead) + [(0, 0), (0, HEAD_PAD - width)])
    return a.reshape(lead + (HP,))


def _unpad_heads(a, width):
    lead = a.shape[:-1]
    return a.reshape(lead + (N_HEADS, HEAD_PAD))[..., :width].reshape(lead + (N_HEADS * width,))


def _layer_weights(full, small, l):
    w_in = full["w_in"][l]
    kpe = jnp.pad(w_in[:, 640:672], ((0, 0), (ROPE_LO, HEAD_PAD - ROPE_HI)))
    w_ukv = full["w_ukv"][l].reshape(KV_LORA, N_HEADS, QK_NOPE + V_HEAD)
    w_o = full["w_o"][l]
    row = lambda name: small[name][l].reshape(1, -1)
    headrow = lambda a, b: jnp.pad(jnp.concatenate([small[a][l], small[b][l]]), (0, HEAD_PAD - QK_HEAD)).reshape(1, HEAD_PAD)
    return dict(
        w_in=jnp.concatenate([w_in[:, :640], kpe, w_in[:, 672:]], axis=1),
        w_uq=_pad_heads(full["w_uq"][l], QK_HEAD),
        w_ukv=jnp.concatenate([_pad_heads(w_ukv[..., :QK_NOPE].reshape(KV_LORA, -1), QK_NOPE),
                               _pad_heads(w_ukv[..., QK_NOPE:].reshape(KV_LORA, -1), V_HEAD)], axis=1),
        w_o=jnp.concatenate([_pad_heads(w_o[:ATTN_WIDTH].T, V_HEAD).T, w_o[ATTN_WIDTH:]], axis=0),
        w_up=full["w_up"][l], w_down=full["w_down"][l], w_ple_gate=full["w_ple_gate"][l], w_ple=full["w_ple"][l],
        g_mix=row("g_mix"), g_q_lat=row("g_q_lat"), g_kv_lat=row("g_kv_lat"), g_mlp=row("g_mlp"), g_ple=row("g_ple"),
        g_out_conv=row("g_out_conv"), g_out_attn=_pad_heads(small["g_out_attn"][l], V_HEAD).reshape(1, HP),
        gq=headrow("g_qn_nope", "g_qn_rope"), gk=headrow("g_kn_nope", "g_kn_rope"),
        cw8=jnp.pad(small["conv_w"][l], ((0, SUBLANES - CONV_TAPS), (0, 0))),
    )


def _unpad_grads(gp):
    dw_in = gp["w_in"]
    dw_ukv = gp["w_ukv"]
    k_part = dw_ukv[:, :HP].reshape(KV_LORA, N_HEADS, HEAD_PAD)[..., :QK_NOPE]
    v_part = dw_ukv[:, HP:].reshape(KV_LORA, N_HEADS, HEAD_PAD)[..., :V_HEAD]
    dw_o = gp["w_o"]
    first = lambda name: gp[name][0]
    return dict(
        w_in=jnp.concatenate([dw_in[:, :640], dw_in[:, Z_KPE + ROPE_LO:Z_KPE + ROPE_HI], dw_in[:, Z_GB:]], axis=1),
        w_uq=_unpad_heads(gp["w_uq"], QK_HEAD),
        w_ukv=jnp.concatenate([k_part, v_part], axis=-1).reshape(KV_LORA, -1),
        w_o=jnp.concatenate([_unpad_heads(dw_o[:HP].T, V_HEAD).T, dw_o[HP:]], axis=0),
        w_up=gp["w_up"], w_down=gp["w_down"], w_ple_gate=gp["w_ple_gate"], w_ple=gp["w_ple"],
        g_mix=first("g_mix"), g_q_lat=first("g_q_lat"), g_kv_lat=first("g_kv_lat"), g_mlp=first("g_mlp"),
        g_ple=first("g_ple"), g_out_conv=first("g_out_conv"), g_out_attn=_unpad_heads(first("g_out_attn"), V_HEAD),
        g_qn_nope=gp["gq"][0, :QK_NOPE], g_qn_rope=gp["gq"][0, QK_NOPE:QK_HEAD],
        g_kn_nope=gp["gk"][0, :QK_NOPE], g_kn_rope=gp["gk"][0, QK_NOPE:QK_HEAD],
        conv_w=gp["cw8"][:CONV_TAPS],
    )


def _rope_tables(positions):
    inv_freq = 1.0 / (ROPE_THETA ** (jnp.arange(0, QK_ROPE, 2, dtype=F32) / QK_ROPE))
    ang = positions.astype(F32)[:, None] * inv_freq
    cos, sin = jnp.cos(ang), jnp.sin(ang)
    pad = lambda t, v: jnp.pad(jnp.concatenate([t, t], axis=1), ((0, 0), (ROPE_LO, HEAD_PAD - ROPE_HI)), constant_values=v)
    return pad(cos, 1.0), pad(sin, 0.0)


def _layer_fwd(x0, p_l, W, cs, sn):
    h = _rms_fwd(x0, W["g_mix"], "rms_mix")
    z = _mm(h, W["w_in"], name="mm_in")
    qln, kvln = _lat_fwd(z, W["g_q_lat"], W["g_kv_lat"])
    q_raw = _mm(qln, W["w_uq"], name="mm_uq")
    kv_raw = _mm(kvln, W["w_ukv"], name="mm_ukv")
    qf, kf, vb = _qk_fwd(q_raw, kv_raw, z, cs, sn, W["gq"], W["gk"])
    o, lse = _attn_fwd(qf, kf, vb)
    conv = _conv_fwd(z, W["cw8"])
    mixed = _mix_fwd(o, conv, W["g_out_attn"], W["g_out_conv"])
    x1 = _mm(mixed, W["w_o"], res=x0, name="mm_o")
    h2 = _rms_fwd(x1, W["g_mlp"], "rms_mlp")
    a, f = _mm(h2, W["w_up"], epi="relu2", name="mm_up")
    x2 = _mm(f, W["w_down"], res=x1, name="mm_down")
    h3 = _rms_fwd(x2, W["g_ple"], "rms_ple")
    gl = _mm(h3, W["w_ple_gate"], name="mm_ple_gate")
    pe = _mm(p_l, W["w_ple"], name="mm_ple")
    x3 = _ple_fwd(x2, gl, pe)
    saved = dict(x0=x0, h=h, z=z, qln=qln, kvln=kvln, q_raw=q_raw, kv_raw=kv_raw, qf=qf, kf=kf, vb=vb, o=o, lse=lse,
                 conv=conv, mixed=mixed, x1=x1, h2=h2, a=a, f=f, x2=x2, h3=h3, gl=gl, pe=pe)
    return x3, saved


def _layer_bwd(dx3, p_l, W, cs, sn, sv):
    g = {}
    dpe, dgl = _ple_bwd(dx3, sv["gl"], sv["pe"])
    g["w_ple"] = _mm(p_l, dpe, mode="tn", name="mm_dw_ple")
    g["w_ple_gate"] = _mm(sv["h3"], dgl, mode="tn", name="mm_dw_ple_gate")
    dh3 = _mm(dgl, W["w_ple_gate"], mode="nt", name="mm_dh3")
    dx2, dx2b, g["g_ple"] = _rms_bwd(dh3, sv["x2"], W["g_ple"], dx3, "rms_ple_bwd")
    da = _mm(dx2b, W["w_down"], mode="nt", aux=sv["a"], epi="drelu2", out_dtype=BF16, name="mm_da")
    g["w_down"] = _mm(sv["f"], dx2b, mode="tn", name="mm_dw_down")
    g["w_up"] = _mm(sv["h2"], da, mode="tn", name="mm_dw_up")
    dh2 = _mm(da, W["w_up"], mode="nt", name="mm_dh2")
    dx1, dx1b, g["g_mlp"] = _rms_bwd(dh2, sv["x1"], W["g_mlp"], dx2, "rms_mlp_bwd")
    dmixed = _mm(dx1b, W["w_o"], mode="nt", name="mm_dmixed")
    g["w_o"] = _mm(sv["mixed"], dx1b, mode="tn", name="mm_dw_o")
    do, dconv, g["g_out_attn"], g["g_out_conv"] = _mix_bwd(dmixed, sv["o"], sv["conv"], W["g_out_attn"], W["g_out_conv"])
    dgb, dgc, dxin, g["cw8"] = _conv_bwd(dconv, sv["z"], W["cw8"])
    dqf, dkf, dv = _attn_bwd(sv["qf"], sv["kf"], sv["vb"], sv["o"], do, sv["lse"])
    dq_raw, dkv_raw, dkpe, g["gq"], g["gk"] = _qk_bwd(dqf, dkf, dv, sv["q_raw"], sv["kv_raw"], sv["z"], cs, sn, W["gq"], W["gk"])
    g["w_uq"] = _mm(sv["qln"], dq_raw, mode="tn", name="mm_dw_uq")
    dqln = _mm(dq_raw, W["w_uq"], mode="nt", name="mm_dqln")
    g["w_ukv"] = _mm(sv["kvln"], dkv_raw, mode="tn", name="mm_dw_ukv")
    dkvln = _mm(dkv_raw, W["w_ukv"], mode="nt", name="mm_dkvln")
    dlat, g["g_q_lat"], g["g_kv_lat"] = _lat_bwd(dqln, dkvln, sv["z"], W["g_q_lat"], W["g_kv_lat"])
    dz = jnp.concatenate([dlat, dkpe, dgb, dgc, dxin], axis=1)
    g["w_in"] = _mm(sv["h"], dz, mode="tn", name="mm_dw_in")
    dh = _mm(dz, W["w_in"], mode="nt", name="mm_dh")
    dx0, g["g_mix"] = _rms_bwd(dh, sv["x0"], W["g_mix"], dx1, "rms_mix_bwd", want_bf16=False)
    return dx0, g


def _local_step(x, p, positions, target, full, small):
    depth = p.shape[0]
    cs, sn = _rope_tables(positions)
    Ws = [_layer_weights(full, small, l) for l in range(depth)]
    saved = []
    for l in range(depth):
        x, sv = _layer_fwd(x, p[l], Ws[l], cs, sn)
        saved.append(sv)
    dx, sq = _loss_grad(x, target)
    grads = [None] * depth
    for l in reversed(range(depth)):
        dx, gp = _layer_bwd(dx, p[l], Ws[l], cs, sn, saved[l])
        grads[l] = _unpad_grads(gp)
    return sq, dx, grads


def _shard_major(a, axis):
    K, N = a.shape
    if axis == 1:
        a = a.reshape(K, N_CHIPS, N // N_CHIPS).transpose(1, 0, 2)
    return a.reshape(N_CHIPS, -1)


def _from_shards(a, K, N, axis):
    if axis == 1:
        return a.reshape(N_CHIPS, K, N // N_CHIPS).transpose(1, 0, 2).reshape(K, N)
    return a.reshape(K, N)


def _shard_shape(K, N, axis):
    return (K, N // N_CHIPS) if axis == 1 else (K // N_CHIPS, N)


LAYER_ELEMS = sum(K * N // N_CHIPS for _, K, N, _ in BIG)
assert LAYER_ELEMS % FLAT_COLS == 0


def _small_rows(extra):
    return sum(_pad128(n) for _, n in SMALL + extra) // LANES


def _pack_small(vals, depth, extra, tail_rows=0):
    parts = []
    for l in range(depth):
        for name, n in SMALL + extra:
            parts.append(jnp.pad(vals[name][l].reshape(-1), (0, _pad128(n) - n)))
    flat = jnp.concatenate(parts).reshape(-1, LANES)
    rows = flat.shape[0] + tail_rows
    return jnp.pad(flat, ((0, -(-rows // SUBLANES) * SUBLANES - flat.shape[0]), (0, 0)))


def _unpack_small(flat, depth, extra):
    per_layer = _small_rows(extra) * LANES
    body = flat.reshape(-1)[:depth * per_layer].reshape(depth, per_layer)
    out, off = {}, 0
    for name, n in SMALL + extra:
        out[name] = body[:, off:off + n]
        off += _pad128(n)
    return out


def kernel(x, p, positions, g_mix, w_in, g_q_lat, w_uq, g_kv_lat, w_ukv, g_qn_nope, g_qn_rope, g_kn_nope, g_kn_rope, conv_w, g_out_attn, g_out_conv, w_o, g_mlp, w_up, w_down, g_ple, w_ple_gate, w_ple, loss_target, m_g_mix, m_w_in, m_g_q_lat, m_w_uq, m_g_kv_lat, m_w_ukv, m_g_qn_nope, m_g_qn_rope, m_g_kn_nope, m_g_kn_rope, m_conv_w, m_g_out_attn, m_g_out_conv, m_w_o, m_g_mlp, m_w_up, m_w_down, m_g_ple, m_w_ple_gate, m_w_ple, v_g_mix, v_w_in, v_g_q_lat, v_w_uq, v_g_kv_lat, v_w_ukv, v_g_qn_nope, v_g_qn_rope, v_g_kn_nope, v_g_kn_rope, v_conv_w, v_g_out_attn, v_g_out_conv, v_w_o, v_g_mlp, v_w_up, v_w_down, v_g_ple, v_w_ple_gate, v_w_ple):
    w = dict(g_mix=g_mix, w_in=w_in, g_q_lat=g_q_lat, w_uq=w_uq, g_kv_lat=g_kv_lat, w_ukv=w_ukv, g_qn_nope=g_qn_nope,
             g_qn_rope=g_qn_rope, g_kn_nope=g_kn_nope, g_kn_rope=g_kn_rope, conv_w=conv_w, g_out_attn=g_out_attn,
             g_out_conv=g_out_conv, w_o=w_o, g_mlp=g_mlp, w_up=w_up, w_down=w_down, g_ple=g_ple, w_ple_gate=w_ple_gate,
             w_ple=w_ple)
    m = dict(g_mix=m_g_mix, w_in=m_w_in, g_q_lat=m_g_q_lat, w_uq=m_w_uq, g_kv_lat=m_g_kv_lat, w_ukv=m_w_ukv,
             g_qn_nope=m_g_qn_nope, g_qn_rope=m_g_qn_rope, g_kn_nope=m_g_kn_nope, g_kn_rope=m_g_kn_rope, conv_w=m_conv_w,
             g_out_attn=m_g_out_attn, g_out_conv=m_g_out_conv, w_o=m_w_o, g_mlp=m_g_mlp, w_up=m_w_up, w_down=m_w_down,
             g_ple=m_g_ple, w_ple_gate=m_w_ple_gate, w_ple=m_w_ple)
    v = dict(g_mix=v_g_mix, w_in=v_w_in, g_q_lat=v_g_q_lat, w_uq=v_w_uq, g_kv_lat=v_g_kv_lat, w_ukv=v_w_ukv,
             g_qn_nope=v_g_qn_nope, g_qn_rope=v_g_qn_rope, g_kn_nope=v_g_kn_nope, g_kn_rope=v_g_kn_rope, conv_w=v_conv_w,
             g_out_attn=v_g_out_attn, g_out_conv=v_g_out_conv, w_o=v_w_o, g_mlp=v_g_mlp, w_up=v_w_up, w_down=v_w_down,
             g_ple=v_g_ple, w_ple_gate=v_w_ple_gate, w_ple=v_w_ple)
    depth = p.shape[0]
    ax, ay, ac = _place()
    chip = 2 * ax + ay
    c_arr = jnp.reshape(ac, (1,)).astype(jnp.int32)
    chip_arr = jnp.reshape(chip, (1,)).astype(jnp.int32)
    conv_shard = ("conv_w", CONV_TAPS * CONV_WIDTH // N_CHIPS)
    conv_full = ("conv_w", CONV_TAPS * CONV_WIDTH)

    wflat = jnp.concatenate([w[name][l].reshape(-1) for l in range(depth) for name, _, _, _ in BIG]).astype(BF16)
    gathered = _allgather_weights(wflat.reshape(-1, FLAT_COLS)).reshape(N_CHIPS, depth, LAYER_ELEMS)
    full, off = {}, 0
    for name, K, N, axis in BIG:
        n = K * N // N_CHIPS
        full[name] = [_from_shards(gathered[:, l, off:off + n], K, N, axis) for l in range(depth)]
        off += n
    conv_rows = -(-depth * CONV_TAPS // SUBLANES) * SUBLANES
    conv_all = _allgather_small(jnp.pad(conv_w.reshape(depth * CONV_TAPS, LANES), ((0, conv_rows - depth * CONV_TAPS), (0, 0))),
                                "allgather_conv_w")
    conv_cat = jnp.concatenate([conv_all[2 * j, :depth * CONV_TAPS] for j in range(N_CHIPS)], axis=1)
    small = {name: w[name] for name, _ in SMALL}
    small["conv_w"] = conv_cat.reshape(depth, CONV_TAPS, CONV_WIDTH)

    sq, grad_x, grads = _local_step(x[0], p[:, 0], positions[0], loss_target[0], full, small)

    gflat = jnp.concatenate([_shard_major(grads[l][name], axis) for l in range(depth) for name, _, _, axis in BIG], axis=1)
    gflat = gflat.astype(BF16).reshape(N_CHIPS, -1, FLAT_COLS)
    part = _add_half(gflat, _swap_halves(gflat), c_arr)
    half = _add_chips(part, _exchange_chips(part), chip_arr)
    reduced = _join_halves(half).reshape(depth, LAYER_ELEMS)
    g_out, off = {}, 0
    for name, K, N, axis in BIG:
        n = K * N // N_CHIPS
        g_out[name] = reduced[:, off:off + n].reshape((depth,) + _shard_shape(K, N, axis))
        off += n

    stacked = {name: jnp.stack([grads[l][name] for l in range(depth)]) for name, _ in SMALL + (conv_full,)}
    loss_row = jnp.pad(sq[:1] * (0.5 / D_MODEL), ((0, 0), (0, 0)))
    packed = _pack_small(stacked, depth, (conv_full,), tail_rows=1)
    n_rows = depth * _small_rows((conv_full,))
    packed = packed.at[n_rows].set(loss_row[0])
    summed = _sum_devices(_allgather_small(packed, "allgather_small_grads"))
    loss = summed[n_rows, 0]
    g_small = _unpack_small(summed, depth, (conv_full,))
    for name, n in SMALL:
        g_out[name] = g_small[name]
    g_conv = g_small["conv_w"].reshape(depth, CONV_TAPS, CONV_WIDTH)
    g_out["conv_w"] = lax.dynamic_slice_in_dim(g_conv, chip * LANES, LANES, axis=2)

    delta, new_m, new_v = {}, {}, {}
    for name, K, N, axis in BIG:
        shape = w[name].shape
        two_d = lambda a: a.reshape(-1, shape[-1])
        d, m2, v2 = _adamw(two_d(w[name]), two_d(g_out[name]), two_d(m[name]), two_d(v[name]), "adamw_" + name)
        delta[name], new_m[name], new_v[name] = d.reshape(shape), m2.reshape(shape), v2.reshape(shape)
    pack = lambda vals: _pack_small({k: a.reshape(depth, -1) for k, a in vals.items()}, depth, (conv_shard,))
    with_conv = lambda vals: {**{name: vals[name] for name, _ in SMALL}, "conv_w": vals["conv_w"]}
    d, m2, v2 = _adamw(pack(with_conv(w)), pack(with_conv(g_out)), pack(with_conv(m)), pack(with_conv(v)), "adamw_small")
    for res, packed_res in ((delta, d), (new_m, m2), (new_v, v2)):
        un = _unpack_small(packed_res, depth, (conv_shard,))
        for name, _ in SMALL + (conv_shard,):
            res[name] = un[name].reshape(w[name].shape)

    return (loss, grad_x[None], *[g_out[n] for n in WEIGHT_ORDER], *[delta[n] for n in WEIGHT_ORDER],
            *[new_m[n] for n in WEIGHT_ORDER], *[new_v[n] for n in WEIGHT_ORDER])
```

```python
import functools

import jax
import jax.numpy as jnp
from jax import lax
from jax.experimental import pallas as pl
from jax.experimental.pallas import tpu as pltpu

F32 = jnp.float32
BF16 = jnp.bfloat16
MESH = pl.DeviceIdType.MESH

D_MODEL = 1024
N_HEADS = 8
QK_NOPE = 64
QK_ROPE = 32
QK_HEAD = QK_NOPE + QK_ROPE
V_HEAD = 64
Q_LORA = 384
KV_LORA = 256
ATTN_WIDTH = N_HEADS * V_HEAD
CONV_WIDTH = 512
CONV_TAPS = 3
D_FF = 4096
PLE_DIM = 256
ROPE_THETA = 10000.0
EPS = 1e-6
ATT_SCALE = QK_HEAD ** -0.5
LOG2E = 1.4426950408889634
ATT_SCALE_LOG2 = ATT_SCALE * LOG2E

ADAM_LR = 0.001
ADAM_B1 = 0.9
ADAM_B2 = 0.999
ADAM_EPS = 1e-08
ADAM_WD = 0.01
ADAM_STEP = 10

LANES = 128
SUBLANES = 8
HEAD_PAD = LANES
HP = N_HEADS * HEAD_PAD
ROPE_LO = QK_NOPE
ROPE_MID = QK_NOPE + QK_ROPE // 2
ROPE_HI = QK_NOPE + QK_ROPE
VMEM_LIMIT = 56 * 1024 * 1024

Z_Q, Z_KV, Z_KPE, Z_GB, Z_GC, Z_XIN = 0, 384, 640, 768, 1280, 1792
Z_COLS = 2304
Z_LAT = Z_KPE

ROW_TILE = 512
ATT_TILE = 256
ATT_CHAINS = 4
MM_TM, MM_TN, MM_TK = 1024, 1024, 512

N_CHIPS = 4
N_DEV = 8
FLAT_COLS = 1024

BIG = (
    ("w_in", 1024, 2208, 1), ("w_uq", 384, 768, 1), ("w_ukv", 256, 1024, 1), ("w_o", 1024, 1024, 0),
    ("w_up", 1024, 4096, 1), ("w_down", 4096, 1024, 0), ("w_ple_gate", 1024, 1024, 0), ("w_ple", 256, 1024, 1),
)
SMALL = (
    ("g_mix", 1024), ("g_q_lat", 384), ("g_kv_lat", 256), ("g_qn_nope", 64), ("g_qn_rope", 32), ("g_kn_nope", 64),
    ("g_kn_rope", 32), ("g_out_attn", 512), ("g_out_conv", 512), ("g_mlp", 1024), ("g_ple", 1024),
)
WEIGHT_ORDER = ("g_mix", "w_in", "g_q_lat", "w_uq", "g_kv_lat", "w_ukv", "g_qn_nope", "g_qn_rope", "g_kn_nope",
                "g_kn_rope", "conv_w", "g_out_attn", "g_out_conv", "w_o", "g_mlp", "w_up", "w_down", "g_ple",
                "w_ple_gate", "w_ple")


def _pallas(body, **kw):
    return pl.pallas_call(body, **kw)


def _cparams(*sem):
    return pltpu.CompilerParams(dimension_semantics=sem, vmem_limit_bytes=VMEM_LIMIT)


def _tile(dim, pref, unit=LANES):
    if dim <= pref:
        return dim
    t = (pref // unit) * unit
    while t > unit and dim % t:
        t -= unit
    assert dim % t == 0, (dim, pref)
    return t


def _pad128(n):
    return -(-n // LANES) * LANES


_DIMS = {"nn": (((1,), (0,)), ((), ())), "nt": (((1,), (1,)), ((), ())), "tn": (((0,), (0,)), ((), ()))}


class Stacked:
    def __init__(self, arr, l):
        self.arr, self.l = arr, l
        self.shape = arr.shape[1:]
        self.row_limit = self.col_limit = None

    def spec(self, tr, tc, rc):
        l = self.l
        return pl.BlockSpec((None, tr, tc), lambda i, j, k: (l,) + rc(i, j, k))


class Gathered:
    def __init__(self, arr, l, axis):
        self.arr, self.l, self.axis = arr, l, axis
        _, _, ks, ns = arr.shape
        self.shape = (ks * N_CHIPS, ns) if axis == 0 else (ks, ns * N_CHIPS)
        self.row_limit = ks if axis == 0 else None
        self.col_limit = ns if axis == 1 else None

    def spec(self, tr, tc, rc):
        l, axis = self.l, self.axis
        _, _, ks, ns = self.arr.shape
        per = (ks // tr) if axis == 0 else (ns // tc)

        def index(i, j, k):
            r, c = rc(i, j, k)
            return (r // per, l, r % per, c) if axis == 0 else (c // per, l, r, c % per)

        return pl.BlockSpec((None, None, tr, tc), index)


def _mm(a, b, *, mode="nn", res=None, aux=None, epi=None, out_dtype=F32, shard_out=None, name):
    if mode == "nn":
        (M, K), (K2, N) = a.shape, b.shape
    elif mode == "nt":
        (M, K), (N, K2) = a.shape, b.shape
    else:
        (K, M), (K2, N) = a.shape, b.shape
    assert K == K2, (a.shape, b.shape, mode)
    b_rows, b_cols = getattr(b, "row_limit", None), getattr(b, "col_limit", None)
    n_lim, k_lim = (b_rows, b_cols) if mode == "nt" else (b_cols, b_rows)
    m_lim = M // N_CHIPS if shard_out == 0 else None
    if shard_out == 1:
        n_lim = N // N_CHIPS
    tm, tn, tk = _tile(m_lim or M, MM_TM), _tile(n_lim or N, MM_TN), _tile(k_lim or K, MM_TK)
    nk = K // tk
    a_spec = pl.BlockSpec((tk, tm), lambda i, j, k: (k, i)) if mode == "tn" else pl.BlockSpec((tm, tk), lambda i, j, k: (i, k))
    if mode == "nt":
        b_block, b_rc = (tn, tk), (lambda i, j, k: (j, k))
    else:
        b_block, b_rc = (tk, tn), (lambda i, j, k: (k, j))
    if isinstance(b, (Stacked, Gathered)):
        b_spec, b = b.spec(*b_block, b_rc), b.arr
    else:
        b_spec = pl.BlockSpec(b_block, b_rc)
    mn_spec = pl.BlockSpec((tm, tn), lambda i, j, k: (i, j))
    extra = [t for t in (res, aux) if t is not None]
    dims = _DIMS[mode]

    def body(*refs):
        a_ref, b_ref = refs[0], refs[1]
        extra_refs = refs[2:2 + len(extra)]
        out_refs = refs[2 + len(extra):-1]
        acc = refs[-1]
        k = pl.program_id(2)

        @pl.when(k == 0)
        def _():
            acc[...] = jnp.zeros_like(acc)

        acc[...] += lax.dot_general(a_ref[...].astype(BF16), b_ref[...].astype(BF16), dims, preferred_element_type=F32)

        @pl.when(k == nk - 1)
        def _():
            r = acc[...]
            if res is not None:
                r = r + extra_refs[0][...]
            if epi == "relu2":
                out_refs[0][...] = r
                t = jnp.maximum(r, 0.0)
                out_refs[1][...] = (t * t).astype(BF16)
            elif epi == "drelu2":
                out_refs[0][...] = (r * (2.0 * jnp.maximum(extra_refs[-1][...], 0.0))).astype(out_dtype)
            else:
                out_refs[0][...] = r.astype(out_dtype)

    if epi == "relu2":
        out_shape = (jax.ShapeDtypeStruct((M, N), F32), jax.ShapeDtypeStruct((M, N), BF16))
        out_specs = (mn_spec, mn_spec)
    elif shard_out == 0:
        per = (M // N_CHIPS) // tm
        out_shape = jax.ShapeDtypeStruct((N_CHIPS, M // N_CHIPS, N), out_dtype)
        out_specs = pl.BlockSpec((None, tm, tn), lambda i, j, k: (i // per, i % per, j))
    elif shard_out == 1:
        per = (N // N_CHIPS) // tn
        out_shape = jax.ShapeDtypeStruct((N_CHIPS, M, N // N_CHIPS), out_dtype)
        out_specs = pl.BlockSpec((None, tm, tn), lambda i, j, k: (j // per, i, j % per))
    else:
        out_shape = jax.ShapeDtypeStruct((M, N), out_dtype)
        out_specs = mn_spec
    return _pallas(
        body, name=name, grid=(M // tm, N // tn, nk),
        in_specs=[a_spec, b_spec] + [mn_spec] * len(extra), out_specs=out_specs, out_shape=out_shape,
        scratch_shapes=[pltpu.VMEM((tm, tn), F32)],
        compiler_params=_cparams("parallel", "parallel", "arbitrary"),
    )(a, b, *extra)


def _rows(ts, d, col=0):
    return pl.BlockSpec((ts, d), lambda i: (i, col))


def _gain(d):
    return pl.BlockSpec((1, d), lambda i: (0, 0))


def _accum(d):
    return pl.BlockSpec((SUBLANES, d), lambda i: (0, 0))


def _accumulate(ref, val):
    i = pl.program_id(0)

    @pl.when(i == 0)
    def _():
        ref[...] = jnp.zeros_like(ref)

    ref[...] += jnp.broadcast_to(jnp.sum(val, axis=0, keepdims=True), ref.shape)


def _rinv(x, n):
    return lax.rsqrt(jnp.sum(x * x, axis=-1, keepdims=True) / n + EPS)


def _norm_bwd(x, dy, g, n):
    r = _rinv(x, n)
    xhat = x * r
    dyg = dy * g
    dx = r * (dyg - xhat * (jnp.sum(dyg * xhat, axis=-1, keepdims=True) / n))
    return dx, dy * xhat


def _rms_fwd(x, g, name):
    S, D = x.shape
    ts = _tile(S, ROW_TILE, SUBLANES)

    def body(x_ref, g_ref, h_ref):
        xv = x_ref[...]
        h_ref[...] = (xv * _rinv(xv, D) * g_ref[...]).astype(BF16)

    return _pallas(body, name=name, grid=(S // ts,), in_specs=[_rows(ts, D), _gain(D)], out_specs=_rows(ts, D),
                   out_shape=jax.ShapeDtypeStruct((S, D), BF16), compiler_params=_cparams("parallel"))(x, g)


def _rms_bwd(dy, x, g, dres, name, want_bf16=True):
    S, D = x.shape
    ts = _tile(S, ROW_TILE, SUBLANES)

    def body(dy_ref, x_ref, g_ref, dres_ref, *outs):
        dx, dgc = _norm_bwd(x_ref[...], dy_ref[...], g_ref[...], D)
        dx = dx + dres_ref[...]
        outs[0][...] = dx
        if want_bf16:
            outs[1][...] = dx.astype(BF16)
        _accumulate(outs[-1], dgc)

    out_shape = [jax.ShapeDtypeStruct((S, D), F32)] + ([jax.ShapeDtypeStruct((S, D), BF16)] if want_bf16 else []) + [
        jax.ShapeDtypeStruct((SUBLANES, D), F32)]
    out_specs = [_rows(ts, D)] * (2 if want_bf16 else 1) + [_accum(D)]
    return _pallas(body, name=name, grid=(S // ts,), in_specs=[_rows(ts, D), _rows(ts, D), _gain(D), _rows(ts, D)],
                   out_specs=out_specs, out_shape=out_shape, compiler_params=_cparams("arbitrary"))(dy, x, g, dres)


def _lat_fwd(z, gq, gkv):
    S = z.shape[0]
    ts = _tile(S, ROW_TILE, SUBLANES)

    def body(z_ref, gq_ref, gkv_ref, q_ref, kv_ref):
        zq = z_ref[:, Z_Q:Z_KV]
        zkv = z_ref[:, Z_KV:Z_KPE]
        q_ref[...] = (zq * _rinv(zq, Q_LORA) * gq_ref[...]).astype(BF16)
        kv_ref[...] = (zkv * _rinv(zkv, KV_LORA) * gkv_ref[...]).astype(BF16)

    return _pallas(body, name="lat_fwd", grid=(S // ts,), in_specs=[_rows(ts, Z_LAT), _gain(Q_LORA), _gain(KV_LORA)],
                   out_specs=[_rows(ts, Q_LORA), _rows(ts, KV_LORA)],
                   out_shape=[jax.ShapeDtypeStruct((S, Q_LORA), BF16), jax.ShapeDtypeStruct((S, KV_LORA), BF16)],
                   compiler_params=_cparams("parallel"))(z, gq, gkv)


def _lat_bwd(dq, dkv, z, gq, gkv):
    S = z.shape[0]
    ts = _tile(S, ROW_TILE, SUBLANES)

    def body(dq_ref, dkv_ref, z_ref, gq_ref, gkv_ref, dlat_ref, dgq_ref, dgkv_ref):
        dxq, cq = _norm_bwd(z_ref[:, Z_Q:Z_KV], dq_ref[...], gq_ref[...], Q_LORA)
        dxkv, ckv = _norm_bwd(z_ref[:, Z_KV:Z_KPE], dkv_ref[...], gkv_ref[...], KV_LORA)
        dlat_ref[:, Z_Q:Z_KV] = dxq.astype(BF16)
        dlat_ref[:, Z_KV:Z_KPE] = dxkv.astype(BF16)
        _accumulate(dgq_ref, cq)
        _accumulate(dgkv_ref, ckv)

    return _pallas(body, name="lat_bwd", grid=(S // ts,),
                   in_specs=[_rows(ts, Q_LORA), _rows(ts, KV_LORA), _rows(ts, Z_LAT), _gain(Q_LORA), _gain(KV_LORA)],
                   out_specs=[_rows(ts, Z_LAT), _accum(Q_LORA), _accum(KV_LORA)],
                   out_shape=[jax.ShapeDtypeStruct((S, Z_LAT), BF16), jax.ShapeDtypeStruct((SUBLANES, Q_LORA), F32),
                              jax.ShapeDtypeStruct((SUBLANES, KV_LORA), F32)],
                   compiler_params=_cparams("arbitrary"))(dq, dkv, z, gq, gkv)


def _head_masks():
    lane = lax.broadcasted_iota(jnp.int32, (1, HEAD_PAD), 1)
    return lane < ROPE_LO, (lane >= ROPE_LO) & (lane < ROPE_HI), lane < ROPE_MID


def _seg_sum(t, m_n, m_r):
    sn = jnp.sum(jnp.where(m_n, t, 0.0), axis=-1, keepdims=True) / QK_NOPE
    sr = jnp.sum(jnp.where(m_r, t, 0.0), axis=-1, keepdims=True) / QK_ROPE
    return jnp.where(m_n, sn, sr)


def _rot_half(y, lo):
    half = QK_ROPE // 2
    return jnp.where(lo, -pltpu.roll(y, HEAD_PAD - half, 1), pltpu.roll(y, half, 1))


def _qk_fwd(q_raw, kv_raw, z, cs, sn, gq, gk):
    S = q_raw.shape[0]
    ts = _tile(S, ROW_TILE, SUBLANES)

    def body(q_ref, k_ref, v_ref, kpe_ref, cs_ref, sn_ref, gq_ref, gk_ref, qf_ref, kf_ref, vb_ref):
        m_n, m_r, lo = _head_masks()
        cos, sin, gqv, gkv = cs_ref[...], sn_ref[...], gq_ref[...], gk_ref[...]

        def norm(x, g):
            return x * lax.rsqrt(_seg_sum(x * x, m_n, m_r) + EPS) * g

        def rope(y):
            return y * cos + _rot_half(y, lo) * sin

        kr = rope(norm(kpe_ref[...], gkv))
        lane = lax.broadcasted_iota(jnp.int32, (1, HEAD_PAD), 1)
        for h in range(N_HEADS):
            sl = slice(h * HEAD_PAD, (h + 1) * HEAD_PAD)
            qf_ref[:, sl] = rope(norm(q_ref[:, sl], gqv)).astype(BF16)
            kf_ref[:, sl] = (norm(k_ref[:, sl], gkv) + kr).astype(BF16)
            vb_ref[:, sl] = jnp.where(lane == V_HEAD, 1.0, v_ref[:, sl]).astype(BF16)

    hd = jax.ShapeDtypeStruct((S, HP), BF16)
    return _pallas(body, name="qk_fwd", grid=(S // ts,),
                   in_specs=[_rows(ts, HP), _rows(ts, HP, 0), _rows(ts, HP, 1), _rows(ts, HEAD_PAD, Z_KPE // HEAD_PAD),
                             _rows(ts, HEAD_PAD), _rows(ts, HEAD_PAD), _gain(HEAD_PAD), _gain(HEAD_PAD)],
                   out_specs=[_rows(ts, HP)] * 3, out_shape=[hd, hd, hd],
                   compiler_params=_cparams("parallel"))(q_raw, kv_raw, kv_raw, z, cs, sn, gq, gk)


def _qk_bwd(dqf, dkf, dv, q_raw, kv_raw, z, cs, sn, gq, gk):
    S = q_raw.shape[0]
    ts = _tile(S, ROW_TILE, SUBLANES)

    def body(dqf_ref, dkf_ref, dv_ref, q_ref, k_ref, kpe_ref, cs_ref, sn_ref, gq_ref, gk_ref,
             dq_ref, dkv_ref, dkpe_ref, dgq_ref, dgk_ref):
        m_n, m_r, lo = _head_masks()
        cos, sin, gqv, gkv = cs_ref[...], sn_ref[...], gq_ref[...], gk_ref[...]

        def rope_t(w):
            return w * cos - jnp.where(m_r, _rot_half(w * sin, lo), 0.0)

        def norm_bwd(x, dy, g):
            r = lax.rsqrt(_seg_sum(x * x, m_n, m_r) + EPS)
            xhat = x * r
            dyg = dy * g
            return r * (dyg - xhat * _seg_sum(dyg * xhat, m_n, m_r)), dy * xhat

        accq = jnp.zeros((ts, HEAD_PAD), F32)
        acck = jnp.zeros((ts, HEAD_PAD), F32)
        dkr = jnp.zeros((ts, HEAD_PAD), F32)
        for h in range(N_HEADS):
            sl = slice(h * HEAD_PAD, (h + 1) * HEAD_PAD)
            dx, c = norm_bwd(q_ref[:, sl], rope_t(dqf_ref[:, sl]), gqv)
            dq_ref[:, sl] = dx.astype(BF16)
            accq = accq + c
            dk = dkf_ref[:, sl]
            dkr = dkr + jnp.where(m_r, dk, 0.0)
            dx, c = norm_bwd(k_ref[:, sl], jnp.where(m_n, dk, 0.0), gkv)
            dkv_ref[:, sl] = dx.astype(BF16)
            acck = acck + c
            dkv_ref[:, HP + h * HEAD_PAD:HP + (h + 1) * HEAD_PAD] = dv_ref[:, sl].astype(BF16)
        dx, c = norm_bwd(kpe_ref[...], rope_t(dkr), gkv)
        dkpe_ref[...] = dx.astype(BF16)
        _accumulate(dgq_ref, accq)
        _accumulate(dgk_ref, acck + c)

    return _pallas(body, name="qk_bwd", grid=(S // ts,),
                   in_specs=[_rows(ts, HP)] * 4 + [_rows(ts, HP, 0), _rows(ts, HEAD_PAD, Z_KPE // HEAD_PAD),
                                                   _rows(ts, HEAD_PAD), _rows(ts, HEAD_PAD), _gain(HEAD_PAD), _gain(HEAD_PAD)],
                   out_specs=[_rows(ts, HP), _rows(ts, 2 * HP), _rows(ts, HEAD_PAD), _accum(HEAD_PAD), _accum(HEAD_PAD)],
                   out_shape=[jax.ShapeDtypeStruct((S, HP), BF16), jax.ShapeDtypeStruct((S, 2 * HP), BF16),
                              jax.ShapeDtypeStruct((S, HEAD_PAD), BF16), jax.ShapeDtypeStruct((SUBLANES, HEAD_PAD), F32),
                              jax.ShapeDtypeStruct((SUBLANES, HEAD_PAD), F32)],
                   compiler_params=_cparams("arbitrary"))(dqf, dkf, dv, q_raw, kv_raw, z, cs, sn, gq, gk)


def _causal(t):
    row = lax.broadcasted_iota(jnp.int32, (t, t), 0)
    col = lax.broadcasted_iota(jnp.int32, (t, t), 1)
    return col <= row


def _attn_tiles(S):
    t = _tile(S, ATT_TILE, SUBLANES)
    return t, min(ATT_CHAINS, S // t)


def _attn_fwd(qf, kf, vb):
    S = qf.shape[0]
    t, nc = _attn_tiles(S)
    tq = nc * t

    def body(q_ref, k_ref, v_ref, o_ref, lse_ref, m_ref, acc_ref):
        qt = pl.program_id(1)
        m_ref[...] = jnp.full_like(m_ref, -jnp.inf)
        acc_ref[...] = jnp.zeros_like(acc_ref)

        def chain(a, kb, vb_, masked):
            sub = slice(a * t, (a + 1) * t)
            s = lax.dot_general(q_ref[sub, :], kb, _DIMS["nt"], preferred_element_type=F32) * ATT_SCALE_LOG2
            if masked:
                s = jnp.where(_causal(t), s, -jnp.inf)
            m_old = m_ref[sub, :]
            m_new = jnp.maximum(m_old, jnp.max(s, axis=-1, keepdims=True))
            p = jnp.exp2(s - jnp.concatenate([m_new] * (t // HEAD_PAD), axis=1))
            acc_ref[sub, :] = jnp.exp2(m_old - m_new) * acc_ref[sub, :] + jnp.dot(p.astype(BF16), vb_, preferred_element_type=F32)
            m_ref[sub, :] = m_new

        def trip(j, carry):
            rows = pl.ds(pl.multiple_of(j * t, t), t)
            kb, vb_ = k_ref[rows, :], v_ref[rows, :]
            for a in range(nc):
                chain(a, kb, vb_, False)
            return carry

        lax.fori_loop(0, nc * qt, trip, 0)
        for d in range(nc):
            rows = pl.ds(pl.multiple_of((nc * qt + d) * t, t), t)
            kb, vb_ = k_ref[rows, :], v_ref[rows, :]
            for a in range(d, nc):
                chain(a, kb, vb_, a == d)
        acc = acc_ref[...]
        l = acc[:, V_HEAD:V_HEAD + 1]
        lane = lax.broadcasted_iota(jnp.int32, (1, HEAD_PAD), 1)
        o_ref[...] = jnp.where(lane < V_HEAD, acc / l, 0.0)
        lse_ref[...] = m_ref[...] + jnp.log(l) * LOG2E

    tile = pl.BlockSpec((tq, HEAD_PAD), lambda h, i: (i, h))
    full = pl.BlockSpec((S, HEAD_PAD), lambda h, i: (0, h))
    return _pallas(body, name="attn_fwd", grid=(N_HEADS, S // tq), in_specs=[tile, full, full],
                   out_specs=[tile, pl.BlockSpec((None, tq, HEAD_PAD), lambda h, i: (h, i, 0))],
                   out_shape=[jax.ShapeDtypeStruct((S, HP), F32), jax.ShapeDtypeStruct((N_HEADS, S, HEAD_PAD), F32)],
                   scratch_shapes=[pltpu.VMEM((tq, HEAD_PAD), F32), pltpu.VMEM((tq, HEAD_PAD), F32)],
                   compiler_params=_cparams("parallel", "arbitrary"))(qf, kf, vb)


def _attn_bwd(qf, kf, vb, o, do, lse):
    S = qf.shape[0]
    t, nc = _attn_tiles(S)
    tkv = nc * t
    nq = S // t

    def body(q_ref, k_ref, v_ref, o_ref, do_ref, lse_ref, dq_ref, dk_ref, dv_ref):
        kt = pl.program_id(1)

        @pl.when(kt == 0)
        def _():
            dq_ref[...] = jnp.zeros_like(dq_ref)

        dk_ref[...] = jnp.zeros_like(dk_ref)
        dv_ref[...] = jnp.zeros_like(dv_ref)

        def q_block(rows):
            dof = do_ref[rows, :]
            delta = jnp.sum(dof * o_ref[rows, :], axis=-1, keepdims=True)
            return q_ref[rows, :], dof.astype(BF16), lse_ref[rows, :][:, :1], delta

        def chain(b, qv, masked):
            q, dob, lse, delta = qv
            sub = slice(b * t, (b + 1) * t)
            kb, vb_ = k_ref[sub, :], v_ref[sub, :]
            s = lax.dot_general(q, kb, _DIMS["nt"], preferred_element_type=F32) * ATT_SCALE_LOG2
            p = jnp.exp2(s - lse)
            if masked:
                p = jnp.where(_causal(t), p, 0.0)
            dv_ref[sub, :] += lax.dot_general(p.astype(BF16), dob, _DIMS["tn"], preferred_element_type=F32)
            dp = lax.dot_general(dob, vb_, _DIMS["nt"], preferred_element_type=F32)
            ds = (p * (dp - delta) * ATT_SCALE).astype(BF16)
            dk_ref[sub, :] += lax.dot_general(ds, q, _DIMS["tn"], preferred_element_type=F32)
            return jnp.dot(ds, kb, preferred_element_type=F32)

        for a in range(nc):
            rows = pl.ds(pl.multiple_of((nc * kt + a) * t, t), t)
            qv = q_block(rows)
            dq_ref[rows, :] += sum(chain(b, qv, b == a) for b in range(a + 1))

        def trip(i, carry):
            rows = pl.ds(pl.multiple_of(i * t, t), t)
            qv = q_block(rows)
            dq_ref[rows, :] += sum(chain(b, qv, False) for b in range(nc))
            return carry

        lax.fori_loop(nc * (kt + 1), nq, trip, 0)

    tile = pl.BlockSpec((tkv, HEAD_PAD), lambda h, j: (j, h))
    full = pl.BlockSpec((S, HEAD_PAD), lambda h, j: (0, h))
    hd = jax.ShapeDtypeStruct((S, HP), F32)
    return _pallas(body, name="attn_bwd", grid=(N_HEADS, S // tkv),
                   in_specs=[full, tile, tile, full, full, pl.BlockSpec((None, S, HEAD_PAD), lambda h, j: (h, 0, 0))],
                   out_specs=[full, tile, tile], out_shape=[hd, hd, hd],
                   compiler_params=_cparams("arbitrary", "arbitrary"))(qf, kf, vb, o, do, lse)


def _shift_down(u, j, row):
    return jnp.where(row >= j, pltpu.roll(u, j, 0), 0.0)


def _shift_up(u, j, row, s):
    return jnp.where(row < s - j, pltpu.roll(u, s - j, 0), 0.0)


def _conv_cols(s, first_tile):
    return pl.BlockSpec((s, LANES), lambda cb: (0, first_tile + cb))


def _conv_fwd(z, cw8):
    S = z.shape[0]

    def body(gb_ref, gc_ref, xin_ref, w_ref, out_ref):
        row = lax.broadcasted_iota(jnp.int32, (S, LANES), 0)
        u = gc_ref[...] * xin_ref[...]
        y = w_ref[0:1, :] * u
        for j in range(1, CONV_TAPS):
            y = y + w_ref[j:j + 1, :] * _shift_down(u, j, row)
        out_ref[...] = gb_ref[...] * y

    return _pallas(body, name="conv_fwd", grid=(CONV_WIDTH // LANES,),
                   in_specs=[_conv_cols(S, Z_GB // LANES), _conv_cols(S, Z_GC // LANES), _conv_cols(S, Z_XIN // LANES),
                             pl.BlockSpec((SUBLANES, LANES), lambda cb: (0, cb))],
                   out_specs=_conv_cols(S, 0), out_shape=jax.ShapeDtypeStruct((S, CONV_WIDTH), F32),
                   compiler_params=_cparams("parallel"))(z, z, z, cw8)


def _conv_bwd(dconv, z, cw8):
    S = z.shape[0]

    def body(d_ref, gb_ref, gc_ref, xin_ref, w_ref, dgb_ref, dgc_ref, dxin_ref, dw_ref):
        row = lax.broadcasted_iota(jnp.int32, (S, LANES), 0)
        gc, xin, d = gc_ref[...], xin_ref[...], d_ref[...]
        u = gc * xin
        dy = d * gb_ref[...]
        y = w_ref[0:1, :] * u
        du = w_ref[0:1, :] * dy
        dw = [jnp.sum(dy * u, axis=0, keepdims=True)]
        for j in range(1, CONV_TAPS):
            uj = _shift_down(u, j, row)
            y = y + w_ref[j:j + 1, :] * uj
            du = du + w_ref[j:j + 1, :] * _shift_up(dy, j, row, S)
            dw.append(jnp.sum(dy * uj, axis=0, keepdims=True))
        dgb_ref[...] = (d * y).astype(BF16)
        dgc_ref[...] = (du * xin).astype(BF16)
        dxin_ref[...] = (du * gc).astype(BF16)
        tap = lax.broadcasted_iota(jnp.int32, (SUBLANES, LANES), 0)
        dw_ref[...] = sum(jnp.where(tap == j, dw[j], 0.0) for j in range(CONV_TAPS))

    col = _conv_cols(S, 0)
    sd = jax.ShapeDtypeStruct((S, CONV_WIDTH), BF16)
    return _pallas(body, name="conv_bwd", grid=(CONV_WIDTH // LANES,),
                   in_specs=[col, _conv_cols(S, Z_GB // LANES), _conv_cols(S, Z_GC // LANES), _conv_cols(S, Z_XIN // LANES),
                             pl.BlockSpec((SUBLANES, LANES), lambda cb: (0, cb))],
                   out_specs=[col, col, col, pl.BlockSpec((SUBLANES, LANES), lambda cb: (0, cb))],
                   out_shape=[sd, sd, sd, jax.ShapeDtypeStruct((SUBLANES, CONV_WIDTH), F32)],
                   compiler_params=_cparams("parallel"))(dconv, z, z, z, cw8)


def _mix_fwd(o, conv, ga, gc):
    S = o.shape[0]
    ts = _tile(S, ROW_TILE, SUBLANES)

    def body(o_ref, c_ref, ga_ref, gc_ref, out_ref):
        ov, cv = o_ref[...], c_ref[...]
        out_ref[:, :HP] = (ov * _rinv(ov, ATTN_WIDTH) * ga_ref[...]).astype(BF16)
        out_ref[:, HP:] = (cv * _rinv(cv, CONV_WIDTH) * gc_ref[...]).astype(BF16)

    return _pallas(body, name="mix_fwd", grid=(S // ts,),
                   in_specs=[_rows(ts, HP), _rows(ts, CONV_WIDTH), _gain(HP), _gain(CONV_WIDTH)],
                   out_specs=_rows(ts, HP + CONV_WIDTH), out_shape=jax.ShapeDtypeStruct((S, HP + CONV_WIDTH), BF16),
                   compiler_params=_cparams("parallel"))(o, conv, ga, gc)


def _mix_bwd(dmixed, o, conv, ga, gc):
    S = o.shape[0]
    ts = _tile(S, ROW_TILE, SUBLANES)

    def body(d_ref, o_ref, c_ref, ga_ref, gc_ref, do_ref, dc_ref, dga_ref, dgc_ref):
        dx, ca = _norm_bwd(o_ref[...], d_ref[:, :HP], ga_ref[...], ATTN_WIDTH)
        do_ref[...] = dx
        dx, cc = _norm_bwd(c_ref[...], d_ref[:, HP:], gc_ref[...], CONV_WIDTH)
        dc_ref[...] = dx
        _accumulate(dga_ref, ca)
        _accumulate(dgc_ref, cc)

    return _pallas(body, name="mix_bwd", grid=(S // ts,),
                   in_specs=[_rows(ts, HP + CONV_WIDTH), _rows(ts, HP), _rows(ts, CONV_WIDTH), _gain(HP), _gain(CONV_WIDTH)],
                   out_specs=[_rows(ts, HP), _rows(ts, CONV_WIDTH), _accum(HP), _accum(CONV_WIDTH)],
                   out_shape=[jax.ShapeDtypeStruct((S, HP), F32), jax.ShapeDtypeStruct((S, CONV_WIDTH), F32),
                              jax.ShapeDtypeStruct((SUBLANES, HP), F32), jax.ShapeDtypeStruct((SUBLANES, CONV_WIDTH), F32)],
                   compiler_params=_cparams("arbitrary"))(dmixed, o, conv, ga, gc)


def _ple_fwd(x, gl, pe):
    S, D = x.shape
    ts = _tile(S, ROW_TILE, SUBLANES)

    def body(x_ref, gl_ref, pe_ref, out_ref):
        out_ref[...] = x_ref[...] + jax.nn.sigmoid(gl_ref[...]) * pe_ref[...]

    return _pallas(body, name="ple_fwd", grid=(S // ts,), in_specs=[_rows(ts, D)] * 3, out_specs=_rows(ts, D),
                   out_shape=jax.ShapeDtypeStruct((S, D), F32), compiler_params=_cparams("parallel"))(x, gl, pe)


def _ple_bwd(dx, gl, pe):
    S, D = dx.shape
    ts = _tile(S, ROW_TILE, SUBLANES)

    def body(dx_ref, gl_ref, pe_ref, dpe_ref, dgl_ref):
        d = dx_ref[...]
        gate = jax.nn.sigmoid(gl_ref[...])
        dpe_ref[...] = (d * gate).astype(BF16)
        dgl_ref[...] = (d * pe_ref[...] * (gate * (1.0 - gate))).astype(BF16)

    sd = jax.ShapeDtypeStruct((S, D), BF16)
    return _pallas(body, name="ple_bwd", grid=(S // ts,), in_specs=[_rows(ts, D)] * 3, out_specs=[_rows(ts, D)] * 2,
                   out_shape=[sd, sd], compiler_params=_cparams("parallel"))(dx, gl, pe)


def _loss_grad(y, target):
    S, D = y.shape
    ts = _tile(S, ROW_TILE, SUBLANES)

    def body(y_ref, t_ref, dy_ref, sq_ref):
        e = y_ref[...] - t_ref[...]
        dy_ref[...] = e / D

        @pl.when(pl.program_id(0) == 0)
        def _():
            sq_ref[...] = jnp.zeros_like(sq_ref)

        sq_ref[...] += jnp.broadcast_to(jnp.sum(jnp.sum(e * e, axis=1, keepdims=True), axis=0, keepdims=True), sq_ref.shape)

    return _pallas(body, name="loss_grad", grid=(S // ts,), in_specs=[_rows(ts, D)] * 2,
                   out_specs=[_rows(ts, D), _accum(LANES)],
                   out_shape=[jax.ShapeDtypeStruct((S, D), F32), jax.ShapeDtypeStruct((SUBLANES, LANES), F32)],
                   compiler_params=_cparams("arbitrary"))(y, target)


def _adamw(w, g, m, v, name):
    L, R, C = w.shape
    tr = _tile(R, ROW_TILE, SUBLANES)

    def body(w_ref, g_ref, m_ref, v_ref, d_ref, m2_ref, v2_ref):
        gv = g_ref[...]
        m2 = ADAM_B1 * m_ref[...] + (1.0 - ADAM_B1) * gv
        v2 = ADAM_B2 * v_ref[...] + (1.0 - ADAM_B2) * (gv * gv)
        m_hat = m2 / (1.0 - ADAM_B1 ** ADAM_STEP)
        v_hat = v2 / (1.0 - ADAM_B2 ** ADAM_STEP)
        d_ref[...] = -ADAM_LR * (m_hat / (jnp.sqrt(v_hat) + ADAM_EPS) + ADAM_WD * w_ref[...])
        m2_ref[...] = m2
        v2_ref[...] = v2

    sd = jax.ShapeDtypeStruct((L, R, C), F32)
    spec = pl.BlockSpec((None, tr, C), lambda l, i: (l, i, 0))
    return _pallas(body, name=name, grid=(L, R // tr), in_specs=[spec] * 4, out_specs=[spec] * 3, out_shape=[sd, sd, sd],
                   compiler_params=_cparams("parallel", "parallel"))(w, g, m, v)


def _place():
    return lax.axis_index("x"), lax.axis_index("y"), lax.axis_index("c")


def _other_chips(x, y):
    return [(1 - x, y), (x, 1 - y), (1 - x, 1 - y)]


HBM = pl.BlockSpec(memory_space=pl.ANY)
VMEM_SPEC = pl.BlockSpec(memory_space=pltpu.VMEM)


def _remote_copy(send_sems, recv_sems, k, src, dst, to):
    return pltpu.make_async_remote_copy(src_ref=src, dst_ref=dst, send_sem=send_sems.at[k], recv_sem=recv_sems.at[k],
                                        device_id=to, device_id_type=MESH)


def _comm_call(body, name, arrays, out_shapes, n_remote, n_local=0):
    scratch = [pltpu.SemaphoreType.DMA((n_remote,)), pltpu.SemaphoreType.DMA((n_remote,))]
    if n_local:
        scratch.append(pltpu.SemaphoreType.DMA((n_local,)))
    return _pallas(body, name=name, in_specs=[HBM] * len(arrays), out_specs=[HBM] * len(out_shapes), out_shape=out_shapes,
                   scratch_shapes=scratch, compiler_params=pltpu.CompilerParams(has_side_effects=True))(*arrays)


def _allgather_weights(shards):
    n = len(shards)

    def body(*refs):
        w_refs, out_refs = refs[:n], refs[n:2 * n]
        send_sems, recv_sems, local_sems = refs[2 * n:]
        x, y, c = _place()
        me, sibling = (x, y, c), (x, y, 1 - c)
        chips = _other_chips(x, y)
        copy = functools.partial(_remote_copy, send_sems, recv_sems)

        def half(ref, which):
            h = ref.shape[-2] // 2
            return pl.ds(which * h, h)

        own, first, passed = [], [], []
        for i, (w_ref, out_ref) in enumerate(zip(w_refs, out_refs)):
            own.append(pltpu.make_async_copy(w_ref, out_ref.at[2 * x + y], local_sems.at[i]))
            own[-1].start()
            for k, (cx, cy) in enumerate(chips):
                first.append(copy(6 * i + k, w_ref.at[:, half(w_ref, c)], out_ref.at[2 * x + y, :, half(w_ref, c)], (cx, cy, c)))
                first[-1].start()
        for i, out_ref in enumerate(out_refs):
            for k, (cx, cy) in enumerate(chips):
                blk = out_ref.at[2 * cx + cy, :, half(out_ref, c)]
                copy(6 * i + k, blk, blk, me).wait_recv()
                passed.append(copy(6 * i + 3 + k, blk, blk, sibling))
                passed[-1].start()
        for i, out_ref in enumerate(out_refs):
            for k, (cx, cy) in enumerate(chips):
                blk = out_ref.at[2 * cx + cy, :, half(out_ref, 1 - c)]
                copy(6 * i + 3 + k, blk, blk, me).wait_recv()
        for cp in first + passed:
            cp.wait_send()
        for cp in own:
            cp.wait()

    return _comm_call(body, "allgather_weights", shards,
                      [jax.ShapeDtypeStruct((N_CHIPS,) + s.shape, s.dtype) for s in shards], 6 * n, n)


def _swap_halves(gs):
    n = len(gs)

    def body(*refs):
        g_refs, out_refs, (send_sems, recv_sems) = refs[:n], refs[n:2 * n], refs[2 * n:]
        x, y, c = _place()
        cps = []
        for i, (g_ref, out_ref) in enumerate(zip(g_refs, out_refs)):
            h = g_ref.shape[2] // 2
            cps.append(_remote_copy(send_sems, recv_sems, i, g_ref.at[:, :, pl.ds((1 - c) * h, h)], out_ref, (x, y, 1 - c)))
            cps[-1].start()
        for cp in cps:
            cp.wait()

    return _comm_call(body, "grad_swap_halves", gs,
                      [jax.ShapeDtypeStruct(g.shape[:2] + (g.shape[2] // 2, g.shape[3]), g.dtype) for g in gs], n)


def _add_half(g, r1, c_arr, name):
    L, n, R, C = g.shape
    H = R // 2
    th = _tile(H, ROW_TILE, 16)
    nb = H // th

    def body(c_ref, g_ref, r_ref, out_ref):
        out_ref[...] = (g_ref[...].astype(F32) + r_ref[...].astype(F32)).astype(BF16)

    blk = (None, None, th, C)
    grid_spec = pltpu.PrefetchScalarGridSpec(
        num_scalar_prefetch=1, grid=(L, n, nb),
        in_specs=[pl.BlockSpec(blk, lambda l, s, i, c_ref: (l, s, c_ref[0] * nb + i, 0)),
                  pl.BlockSpec(blk, lambda l, s, i, c_ref: (l, s, i, 0))],
        out_specs=pl.BlockSpec(blk, lambda l, s, i, c_ref: (l, s, i, 0)))
    return _pallas(body, name=name, grid_spec=grid_spec, out_shape=jax.ShapeDtypeStruct((L, n, H, C), BF16),
                   compiler_params=_cparams("parallel", "parallel", "parallel"))(c_arr, g, r1)


def _exchange_chips(ps):
    n = len(ps)

    def body(*refs):
        p_refs, out_refs, (send_sems, recv_sems) = refs[:n], refs[n:2 * n], refs[2 * n:]
        x, y, c = _place()
        cps = []
        for i, (p_ref, out_ref) in enumerate(zip(p_refs, out_refs)):
            for k, (cx, cy) in enumerate(_other_chips(x, y)):
                cps.append(_remote_copy(send_sems, recv_sems, 3 * i + k, p_ref.at[:, 2 * cx + cy], out_ref.at[k], (cx, cy, c)))
                cps[-1].start()
        for cp in cps:
            cp.wait()

    return _comm_call(body, "grad_exchange_chips", ps,
                      [jax.ShapeDtypeStruct((N_CHIPS - 1, p.shape[0]) + p.shape[2:], p.dtype) for p in ps], 3 * n)


def _add_chips(p, r2, chip_arr, name):
    L, n, H, C = p.shape
    th = _tile(H, ROW_TILE, 16)

    def body(j_ref, p_ref, r_ref, out_ref):
        s = p_ref[...].astype(F32)
        for k in range(N_CHIPS - 1):
            s = s + r_ref[k].astype(F32)
        out_ref[...] = s

    grid_spec = pltpu.PrefetchScalarGridSpec(
        num_scalar_prefetch=1, grid=(L, H // th),
        in_specs=[pl.BlockSpec((None, None, th, C), lambda l, i, j_ref: (l, j_ref[0], i, 0)),
                  pl.BlockSpec((N_CHIPS - 1, None, th, C), lambda l, i, j_ref: (0, l, i, 0))],
        out_specs=pl.BlockSpec((None, th, C), lambda l, i, j_ref: (l, i, 0)))
    return _pallas(body, name=name, grid_spec=grid_spec, out_shape=jax.ShapeDtypeStruct((L, H, C), F32),
                   compiler_params=_cparams("parallel", "parallel"))(chip_arr, p, r2)


def _join_halves(fhs):
    n = len(fhs)

    def body(*refs):
        f_refs, out_refs = refs[:n], refs[n:2 * n]
        send_sems, recv_sems, local_sems = refs[2 * n:]
        x, y, c = _place()
        own, cps = [], []
        for i, (f_ref, out_ref) in enumerate(zip(f_refs, out_refs)):
            h = f_ref.shape[1]
            mine = out_ref.at[:, pl.ds(c * h, h)]
            own.append(pltpu.make_async_copy(f_ref, mine, local_sems.at[i]))
            own[-1].start()
            cps.append(_remote_copy(send_sems, recv_sems, i, f_ref, mine, (x, y, 1 - c)))
            cps[-1].start()
        for cp in cps + own:
            cp.wait()

    return _comm_call(body, "grad_join_halves", fhs,
                      [jax.ShapeDtypeStruct((f.shape[0], 2 * f.shape[1], f.shape[2]), f.dtype) for f in fhs], n, n)


def _allgather_small(v, name):
    R, C = v.shape

    def body(v_ref, out_ref, send_sems, recv_sems):
        x, y, c = _place()
        me = 4 * x + 2 * y + c
        out_ref[me] = v_ref[...]
        cps = []
        for k in range(1, N_DEV):
            peer = (1 - x if k & 4 else x, 1 - y if k & 2 else y, 1 - c if k & 1 else c)
            cps.append(pltpu.make_async_remote_copy(src_ref=v_ref, dst_ref=out_ref.at[me], send_sem=send_sems.at[k - 1],
                                                    recv_sem=recv_sems.at[k - 1], device_id=peer, device_id_type=MESH))
        for cp in cps:
            cp.start()
        for cp in cps:
            cp.wait()

    return _pallas(body, name=name, in_specs=[VMEM_SPEC], out_specs=VMEM_SPEC,
                   out_shape=jax.ShapeDtypeStruct((N_DEV, R, C), v.dtype),
                   scratch_shapes=[pltpu.SemaphoreType.DMA((N_DEV - 1,)), pltpu.SemaphoreType.DMA((N_DEV - 1,))],
                   compiler_params=pltpu.CompilerParams(has_side_effects=True))(v)


def _sum_devices(g):
    n, R, C = g.shape

    def body(g_ref, out_ref):
        s = g_ref[0]
        for d in range(1, n):
            s = s + g_ref[d]
        out_ref[...] = s

    return _pallas(body, name="sum_devices", in_specs=[VMEM_SPEC], out_specs=VMEM_SPEC,
                   out_shape=jax.ShapeDtypeStruct((R, C), g.dtype))(g)


def _pad_heads(a, width):
    lead = a.shape[:-1]
    a = a.reshape(lead + (N_HEADS, width))
    a = jnp.pad(a, [(0, 0)] * len(lead) + [(0, 0), (0, HEAD_PAD - width)])
    return a.reshape(lead + (HP,))


def _unpad_heads(a, width):
    lead = a.shape[:-1]
    return a.reshape(lead + (N_HEADS, HEAD_PAD))[..., :width].reshape(lead + (N_HEADS * width,))


def _prepare_weights(gathered):
    L = gathered["w_in"].shape[1]
    cols = lambda g: g.transpose(1, 2, 0, 3).reshape(L, g.shape[2], -1)
    rows = lambda g: g.transpose(1, 0, 2, 3).reshape(L, -1, g.shape[3])
    w_in = cols(gathered["w_in"])
    kpe = jnp.pad(w_in[:, :, 640:672], ((0, 0), (0, 0), (ROPE_LO, HEAD_PAD - ROPE_HI)))
    w_ukv = cols(gathered["w_ukv"]).reshape(L, KV_LORA, N_HEADS, QK_NOPE + V_HEAD)
    w_o = rows(gathered["w_o"])
    attn_rows = _pad_heads(w_o[:, :ATTN_WIDTH].transpose(0, 2, 1), V_HEAD).transpose(0, 2, 1)
    return dict(
        gathered,
        w_in=jnp.concatenate([w_in[:, :, :640], kpe, w_in[:, :, 672:]], axis=2),
        w_uq=_pad_heads(cols(gathered["w_uq"]), QK_HEAD),
        w_ukv=jnp.concatenate([_pad_heads(w_ukv[..., :QK_NOPE].reshape(L, KV_LORA, -1), QK_NOPE),
                               _pad_heads(w_ukv[..., QK_NOPE:].reshape(L, KV_LORA, -1), V_HEAD)], axis=2),
        w_o=jnp.concatenate([attn_rows, w_o[:, ATTN_WIDTH:]], axis=1),
    )


def _layer_weights(prep, small, l):
    row = lambda name: small[name][l].reshape(1, -1)
    headrow = lambda a, b: jnp.pad(jnp.concatenate([small[a][l], small[b][l]]), (0, HEAD_PAD - QK_HEAD)).reshape(1, HEAD_PAD)
    return dict(
        w_in=Stacked(prep["w_in"], l), w_uq=Stacked(prep["w_uq"], l), w_ukv=Stacked(prep["w_ukv"], l),
        w_o=Stacked(prep["w_o"], l), w_up=Gathered(prep["w_up"], l, 1), w_down=Gathered(prep["w_down"], l, 0),
        w_ple_gate=Gathered(prep["w_ple_gate"], l, 0), w_ple=Gathered(prep["w_ple"], l, 1),
        g_mix=row("g_mix"), g_q_lat=row("g_q_lat"), g_kv_lat=row("g_kv_lat"), g_mlp=row("g_mlp"), g_ple=row("g_ple"),
        g_out_conv=row("g_out_conv"), g_out_attn=_pad_heads(small["g_out_attn"][l], V_HEAD).reshape(1, HP),
        gq=headrow("g_qn_nope", "g_qn_rope"), gk=headrow("g_kn_nope", "g_kn_rope"),
        cw8=jnp.pad(small["conv_w"][l], ((0, SUBLANES - CONV_TAPS), (0, 0))),
    )


def _unpad_grads(gp):
    dw_in = gp["w_in"]
    dw_ukv = gp["w_ukv"]
    k_part = dw_ukv[:, :HP].reshape(KV_LORA, N_HEADS, HEAD_PAD)[..., :QK_NOPE]
    v_part = dw_ukv[:, HP:].reshape(KV_LORA, N_HEADS, HEAD_PAD)[..., :V_HEAD]
    dw_o = gp["w_o"]
    first = lambda name: gp[name][0]
    col_shards = lambda a: a.reshape(a.shape[0], N_CHIPS, -1).transpose(1, 0, 2)
    return dict(
        w_in=col_shards(jnp.concatenate([dw_in[:, :640], dw_in[:, Z_KPE + ROPE_LO:Z_KPE + ROPE_HI], dw_in[:, Z_GB:]], axis=1)),
        w_uq=col_shards(_unpad_heads(gp["w_uq"], QK_HEAD)),
        w_ukv=col_shards(jnp.concatenate([k_part, v_part], axis=-1).reshape(KV_LORA, -1)),
        w_o=jnp.concatenate([_unpad_heads(dw_o[:HP].T, V_HEAD).T, dw_o[HP:]], axis=0).reshape(N_CHIPS, -1, D_MODEL),
        w_up=gp["w_up"], w_down=gp["w_down"], w_ple_gate=gp["w_ple_gate"], w_ple=gp["w_ple"],
        g_mix=first("g_mix"), g_q_lat=first("g_q_lat"), g_kv_lat=first("g_kv_lat"), g_mlp=first("g_mlp"),
        g_ple=first("g_ple"), g_out_conv=first("g_out_conv"), g_out_attn=_unpad_heads(first("g_out_attn"), V_HEAD),
        g_qn_nope=gp["gq"][0, :QK_NOPE], g_qn_rope=gp["gq"][0, QK_NOPE:QK_HEAD],
        g_kn_nope=gp["gk"][0, :QK_NOPE], g_kn_rope=gp["gk"][0, QK_NOPE:QK_HEAD],
        conv_w=gp["cw8"][:CONV_TAPS],
    )


def _rope_tables(positions):
    inv_freq = 1.0 / (ROPE_THETA ** (jnp.arange(0, QK_ROPE, 2, dtype=F32) / QK_ROPE))
    ang = positions.astype(F32)[:, None] * inv_freq
    cos, sin = jnp.cos(ang), jnp.sin(ang)
    pad = lambda t, v: jnp.pad(jnp.concatenate([t, t], axis=1), ((0, 0), (ROPE_LO, HEAD_PAD - ROPE_HI)), constant_values=v)
    return pad(cos, 1.0), pad(sin, 0.0)


def _layer_fwd(x0, p_l, W, cs, sn):
    h = _rms_fwd(x0, W["g_mix"], "rms_mix")
    z = _mm(h, W["w_in"], name="mm_in")
    qln, kvln = _lat_fwd(z, W["g_q_lat"], W["g_kv_lat"])
    q_raw = _mm(qln, W["w_uq"], name="mm_uq")
    kv_raw = _mm(kvln, W["w_ukv"], name="mm_ukv")
    qf, kf, vb = _qk_fwd(q_raw, kv_raw, z, cs, sn, W["gq"], W["gk"])
    o, lse = _attn_fwd(qf, kf, vb)
    conv = _conv_fwd(z, W["cw8"])
    mixed = _mix_fwd(o, conv, W["g_out_attn"], W["g_out_conv"])
    x1 = _mm(mixed, W["w_o"], res=x0, name="mm_o")
    h2 = _rms_fwd(x1, W["g_mlp"], "rms_mlp")
    a, f = _mm(h2, W["w_up"], epi="relu2", name="mm_up")
    x2 = _mm(f, W["w_down"], res=x1, name="mm_down")
    h3 = _rms_fwd(x2, W["g_ple"], "rms_ple")
    gl = _mm(h3, W["w_ple_gate"], name="mm_ple_gate")
    pe = _mm(p_l, W["w_ple"], name="mm_ple")
    x3 = _ple_fwd(x2, gl, pe)
    saved = dict(x0=x0, h=h, z=z, qln=qln, kvln=kvln, q_raw=q_raw, kv_raw=kv_raw, qf=qf, kf=kf, vb=vb, o=o, lse=lse,
                 conv=conv, mixed=mixed, x1=x1, h2=h2, a=a, f=f, x2=x2, h3=h3, gl=gl, pe=pe)
    return x3, saved


def _layer_bwd(dx3, p_l, W, cs, sn, sv):
    g = {}
    dpe, dgl = _ple_bwd(dx3, sv["gl"], sv["pe"])
    g["w_ple"] = _mm(p_l, dpe, mode="tn", out_dtype=BF16, shard_out=1, name="mm_dw_ple")
    g["w_ple_gate"] = _mm(sv["h3"], dgl, mode="tn", out_dtype=BF16, shard_out=0, name="mm_dw_ple_gate")
    dh3 = _mm(dgl, W["w_ple_gate"], mode="nt", name="mm_dh3")
    dx2, dx2b, g["g_ple"] = _rms_bwd(dh3, sv["x2"], W["g_ple"], dx3, "rms_ple_bwd")
    da = _mm(dx2b, W["w_down"], mode="nt", aux=sv["a"], epi="drelu2", out_dtype=BF16, name="mm_da")
    g["w_down"] = _mm(sv["f"], dx2b, mode="tn", out_dtype=BF16, shard_out=0, name="mm_dw_down")
    g["w_up"] = _mm(sv["h2"], da, mode="tn", out_dtype=BF16, shard_out=1, name="mm_dw_up")
    dh2 = _mm(da, W["w_up"], mode="nt", name="mm_dh2")
    dx1, dx1b, g["g_mlp"] = _rms_bwd(dh2, sv["x1"], W["g_mlp"], dx2, "rms_mlp_bwd")
    dmixed = _mm(dx1b, W["w_o"], mode="nt", name="mm_dmixed")
    g["w_o"] = _mm(sv["mixed"], dx1b, mode="tn", out_dtype=BF16, name="mm_dw_o")
    do, dconv, g["g_out_attn"], g["g_out_conv"] = _mix_bwd(dmixed, sv["o"], sv["conv"], W["g_out_attn"], W["g_out_conv"])
    dgb, dgc, dxin, g["cw8"] = _conv_bwd(dconv, sv["z"], W["cw8"])
    dqf, dkf, dv = _attn_bwd(sv["qf"], sv["kf"], sv["vb"], sv["o"], do, sv["lse"])
    dq_raw, dkv_raw, dkpe, g["gq"], g["gk"] = _qk_bwd(dqf, dkf, dv, sv["q_raw"], sv["kv_raw"], sv["z"], cs, sn, W["gq"], W["gk"])
    g["w_uq"] = _mm(sv["qln"], dq_raw, mode="tn", out_dtype=BF16, name="mm_dw_uq")
    dqln = _mm(dq_raw, W["w_uq"], mode="nt", name="mm_dqln")
    g["w_ukv"] = _mm(sv["kvln"], dkv_raw, mode="tn", out_dtype=BF16, name="mm_dw_ukv")
    dkvln = _mm(dkv_raw, W["w_ukv"], mode="nt", name="mm_dkvln")
    dlat, g["g_q_lat"], g["g_kv_lat"] = _lat_bwd(dqln, dkvln, sv["z"], W["g_q_lat"], W["g_kv_lat"])
    dz = jnp.concatenate([dlat, dkpe, dgb, dgc, dxin], axis=1)
    g["w_in"] = _mm(sv["h"], dz, mode="tn", out_dtype=BF16, name="mm_dw_in")
    dh = _mm(dz, W["w_in"], mode="nt", name="mm_dh")
    dx0, g["g_mix"] = _rms_bwd(dh, sv["x0"], W["g_mix"], dx1, "rms_mix_bwd", want_bf16=False)
    return dx0, g


def _local_step(x, p, positions, target, gathered, small):
    depth = p.shape[0]
    cs, sn = _rope_tables(positions)
    prep = _prepare_weights(gathered)
    Ws = [_layer_weights(prep, small, l) for l in range(depth)]
    saved = []
    for l in range(depth):
        x, sv = _layer_fwd(x, p[l], Ws[l], cs, sn)
        saved.append(sv)
    dx, sq = _loss_grad(x, target)
    grads = [None] * depth
    for l in reversed(range(depth)):
        dx, gp = _layer_bwd(dx, p[l], Ws[l], cs, sn, saved[l])
        grads[l] = _unpad_grads(gp)
    return sq, dx, grads


def _small_rows(extra):
    return sum(_pad128(n) for _, n in SMALL + extra) // LANES


def _pack_small(vals, depth, extra, tail_rows=0):
    parts = []
    for l in range(depth):
        for name, n in SMALL + extra:
            parts.append(jnp.pad(vals[name][l].reshape(-1), (0, _pad128(n) - n)))
    flat = jnp.concatenate(parts).reshape(-1, LANES)
    rows = flat.shape[0] + tail_rows
    return jnp.pad(flat, ((0, -(-rows // SUBLANES) * SUBLANES - flat.shape[0]), (0, 0)))


def _unpack_small(flat, depth, extra):
    per_layer = _small_rows(extra) * LANES
    body = flat.reshape(-1)[:depth * per_layer].reshape(depth, per_layer)
    out, off = {}, 0
    for name, n in SMALL + extra:
        out[name] = body[:, off:off + n]
        off += _pad128(n)
    return out


def kernel(x, p, positions, g_mix, w_in, g_q_lat, w_uq, g_kv_lat, w_ukv, g_qn_nope, g_qn_rope, g_kn_nope, g_kn_rope, conv_w, g_out_attn, g_out_conv, w_o, g_mlp, w_up, w_down, g_ple, w_ple_gate, w_ple, loss_target, m_g_mix, m_w_in, m_g_q_lat, m_w_uq, m_g_kv_lat, m_w_ukv, m_g_qn_nope, m_g_qn_rope, m_g_kn_nope, m_g_kn_rope, m_conv_w, m_g_out_attn, m_g_out_conv, m_w_o, m_g_mlp, m_w_up, m_w_down, m_g_ple, m_w_ple_gate, m_w_ple, v_g_mix, v_w_in, v_g_q_lat, v_w_uq, v_g_kv_lat, v_w_ukv, v_g_qn_nope, v_g_qn_rope, v_g_kn_nope, v_g_kn_rope, v_conv_w, v_g_out_attn, v_g_out_conv, v_w_o, v_g_mlp, v_w_up, v_w_down, v_g_ple, v_w_ple_gate, v_w_ple):
    w = dict(g_mix=g_mix, w_in=w_in, g_q_lat=g_q_lat, w_uq=w_uq, g_kv_lat=g_kv_lat, w_ukv=w_ukv, g_qn_nope=g_qn_nope,
             g_qn_rope=g_qn_rope, g_kn_nope=g_kn_nope, g_kn_rope=g_kn_rope, conv_w=conv_w, g_out_attn=g_out_attn,
             g_out_conv=g_out_conv, w_o=w_o, g_mlp=g_mlp, w_up=w_up, w_down=w_down, g_ple=g_ple, w_ple_gate=w_ple_gate,
             w_ple=w_ple)
    m = dict(g_mix=m_g_mix, w_in=m_w_in, g_q_lat=m_g_q_lat, w_uq=m_w_uq, g_kv_lat=m_g_kv_lat, w_ukv=m_w_ukv,
             g_qn_nope=m_g_qn_nope, g_qn_rope=m_g_qn_rope, g_kn_nope=m_g_kn_nope, g_kn_rope=m_g_kn_rope, conv_w=m_conv_w,
             g_out_attn=m_g_out_attn, g_out_conv=m_g_out_conv, w_o=m_w_o, g_mlp=m_g_mlp, w_up=m_w_up, w_down=m_w_down,
             g_ple=m_g_ple, w_ple_gate=m_w_ple_gate, w_ple=m_w_ple)
    v = dict(g_mix=v_g_mix, w_in=v_w_in, g_q_lat=v_g_q_lat, w_uq=v_w_uq, g_kv_lat=v_g_kv_lat, w_ukv=v_w_ukv,
             g_qn_nope=v_g_qn_nope, g_qn_rope=v_g_qn_rope, g_kn_nope=v_g_kn_nope, g_kn_rope=v_g_kn_rope, conv_w=v_conv_w,
             g_out_attn=v_g_out_attn, g_out_conv=v_g_out_conv, w_o=v_w_o, g_mlp=v_g_mlp, w_up=v_w_up, w_down=v_w_down,
             g_ple=v_g_ple, w_ple_gate=v_w_ple_gate, w_ple=v_w_ple)
    depth = p.shape[0]
    ax, ay, ac = _place()
    chip = 2 * ax + ay
    c_arr = jnp.reshape(ac, (1,)).astype(jnp.int32)
    chip_arr = jnp.reshape(chip, (1,)).astype(jnp.int32)
    conv_shard = ("conv_w", CONV_TAPS * CONV_WIDTH // N_CHIPS)
    conv_full = ("conv_w", CONV_TAPS * CONV_WIDTH)

    names = [name for name, _, _, _ in BIG]
    gathered = dict(zip(names, _allgather_weights([w[name].astype(BF16) for name in names])))
    conv_rows = -(-depth * CONV_TAPS // SUBLANES) * SUBLANES
    conv_all = _allgather_small(jnp.pad(conv_w.reshape(depth * CONV_TAPS, LANES), ((0, conv_rows - depth * CONV_TAPS), (0, 0))),
                                "allgather_conv_w")
    conv_cat = jnp.concatenate([conv_all[2 * j, :depth * CONV_TAPS] for j in range(N_CHIPS)], axis=1)
    small = {name: w[name] for name, _ in SMALL}
    small["conv_w"] = conv_cat.reshape(depth, CONV_TAPS, CONV_WIDTH)

    sq, grad_x, grads = _local_step(x[0], p[:, 0], positions[0], loss_target[0], gathered, small)

    gs = [jnp.stack([grads[l][name] for l in range(depth)]) for name in names]
    parts = [_add_half(g, r, c_arr, "grad_add_half_" + name) for name, g, r in zip(names, gs, _swap_halves(gs))]
    halves = [_add_chips(pt, r, chip_arr, "grad_add_chips_" + name) for name, pt, r in zip(names, parts, _exchange_chips(parts))]
    g_out = dict(zip(names, _join_halves(halves)))

    stacked = {name: jnp.stack([grads[l][name] for l in range(depth)]) for name, _ in SMALL + (conv_full,)}
    loss_row = jnp.pad(sq[:1] * (0.5 / D_MODEL), ((0, 0), (0, 0)))
    packed = _pack_small(stacked, depth, (conv_full,), tail_rows=1)
    n_rows = depth * _small_rows((conv_full,))
    packed = packed.at[n_rows].set(loss_row[0])
    summed = _sum_devices(_allgather_small(packed, "allgather_small_grads"))
    loss = summed[n_rows, 0]
    g_small = _unpack_small(summed, depth, (conv_full,))
    for name, n in SMALL:
        g_out[name] = g_small[name]
    g_conv = g_small["conv_w"].reshape(depth, CONV_TAPS, CONV_WIDTH)
    g_out["conv_w"] = lax.dynamic_slice_in_dim(g_conv, chip * LANES, LANES, axis=2)

    delta, new_m, new_v = {}, {}, {}
    for name in names:
        delta[name], new_m[name], new_v[name] = _adamw(w[name], g_out[name], m[name], v[name], "adamw_" + name)
    pack = lambda vals: _pack_small({k: a.reshape(depth, -1) for k, a in vals.items()}, depth, (conv_shard,))[None]
    with_conv = lambda vals: {**{name: vals[name] for name, _ in SMALL}, "conv_w": vals["conv_w"]}
    d, m2, v2 = _adamw(pack(with_conv(w)), pack(with_conv(g_out)), pack(with_conv(m)), pack(with_conv(v)), "adamw_small")
    for res, packed_res in ((delta, d), (new_m, m2), (new_v, v2)):
        un = _unpack_small(packed_res[0], depth, (conv_shard,))
        for name, _ in SMALL + (conv_shard,):
            res[name] = un[name].reshape(w[name].shape)

    return (loss, grad_x[None], *[g_out[n] for n in WEIGHT_ORDER], *[delta[n] for n in WEIGHT_ORDER],
            *[new_m[n] for n in WEIGHT_ORDER], *[new_v[n] for n in WEIGHT_ORDER])
```

```python
import functools

import jax
import jax.numpy as jnp
from jax import lax
from jax.experimental import pallas as pl
from jax.experimental.pallas import tpu as pltpu

F32 = jnp.float32
BF16 = jnp.bfloat16
MESH = pl.DeviceIdType.MESH

D_MODEL = 1024
N_HEADS = 8
QK_NOPE = 64
QK_ROPE = 32
QK_HEAD = QK_NOPE + QK_ROPE
V_HEAD = 64
Q_LORA = 384
KV_LORA = 256
ATTN_WIDTH = N_HEADS * V_HEAD
CONV_WIDTH = 512
CONV_TAPS = 3
D_FF = 4096
PLE_DIM = 256
ROPE_THETA = 10000.0
EPS = 1e-6
ATT_SCALE = QK_HEAD ** -0.5
LOG2E = 1.4426950408889634
ATT_SCALE_LOG2 = ATT_SCALE * LOG2E

ADAM_LR = 0.001
ADAM_B1 = 0.9
ADAM_B2 = 0.999
ADAM_EPS = 1e-08
ADAM_WD = 0.01
ADAM_STEP = 10

LANES = 128
SUBLANES = 8
HEAD_PAD = LANES
HP = N_HEADS * HEAD_PAD
ROPE_LO = QK_NOPE
ROPE_MID = QK_NOPE + QK_ROPE // 2
ROPE_HI = QK_NOPE + QK_ROPE
VMEM_LIMIT = 56 * 1024 * 1024

Z_Q, Z_KV, Z_KPE, Z_GB, Z_GC, Z_XIN = 0, 384, 640, 768, 1280, 1792
Z_COLS = 2304
Z_LAT = Z_KPE

ROW_TILE = 512
ATT_TILE = 256
ATT_CHAINS = 4
MM_TM, MM_TN, MM_TK = 1024, 1024, 4096
MM_TM_MIN, MM_TK_MIN = 256, 512
MM_VMEM_BUDGET = 40 * 1024 * 1024

N_CHIPS = 4
N_DEV = 8
FLAT_COLS = 1024

BIG = (
    ("w_in", 1024, 2208, 1), ("w_uq", 384, 768, 1), ("w_ukv", 256, 1024, 1), ("w_o", 1024, 1024, 0),
    ("w_up", 1024, 4096, 1), ("w_down", 4096, 1024, 0), ("w_ple_gate", 1024, 1024, 0), ("w_ple", 256, 1024, 1),
)
SMALL = (
    ("g_mix", 1024), ("g_q_lat", 384), ("g_kv_lat", 256), ("g_qn_nope", 64), ("g_qn_rope", 32), ("g_kn_nope", 64),
    ("g_kn_rope", 32), ("g_out_attn", 512), ("g_out_conv", 512), ("g_mlp", 1024), ("g_ple", 1024),
)
WEIGHT_ORDER = ("g_mix", "w_in", "g_q_lat", "w_uq", "g_kv_lat", "w_ukv", "g_qn_nope", "g_qn_rope", "g_kn_nope",
                "g_kn_rope", "conv_w", "g_out_attn", "g_out_conv", "w_o", "g_mlp", "w_up", "w_down", "g_ple",
                "w_ple_gate", "w_ple")


def _pallas(body, **kw):
    return pl.pallas_call(body, **kw)


def _cparams(*sem):
    return pltpu.CompilerParams(dimension_semantics=sem, vmem_limit_bytes=VMEM_LIMIT)


def _tile(dim, pref, unit=LANES):
    if dim <= pref:
        return dim
    t = (pref // unit) * unit
    while t > unit and dim % t:
        t -= unit
    assert dim % t == 0, (dim, pref)
    return t


def _pad128(n):
    return -(-n // LANES) * LANES


_DIMS = {"nn": (((1,), (0,)), ((), ())), "nt": (((1,), (1,)), ((), ())), "tn": (((0,), (0,)), ((), ()))}


class Stacked:
    def __init__(self, arr, l):
        self.arr, self.l = arr, l
        self.shape = arr.shape[1:]
        self.row_limit = self.col_limit = None

    def spec(self, tr, tc, rc):
        l = self.l
        return pl.BlockSpec((None, tr, tc), lambda i, j, k: (l,) + rc(i, j, k))


class Gathered:
    def __init__(self, arr, l, axis):
        self.arr, self.l, self.axis = arr, l, axis
        _, _, ks, ns = arr.shape
        self.shape = (ks * N_CHIPS, ns) if axis == 0 else (ks, ns * N_CHIPS)
        self.row_limit = ks if axis == 0 else None
        self.col_limit = ns if axis == 1 else None

    def spec(self, tr, tc, rc):
        l, axis = self.l, self.axis
        _, _, ks, ns = self.arr.shape
        per = (ks // tr) if axis == 0 else (ns // tc)

        def index(i, j, k):
            r, c = rc(i, j, k)
            return (r // per, l, r % per, c) if axis == 0 else (c // per, l, r, c % per)

        return pl.BlockSpec((None, None, tr, tc), index)


def _mm(a, b, *, mode="nn", res=None, aux=None, epi=None, out_dtype=F32, shard_out=None, name):
    if mode == "nn":
        (M, K), (K2, N) = a.shape, b.shape
    elif mode == "nt":
        (M, K), (N, K2) = a.shape, b.shape
    else:
        (K, M), (K2, N) = a.shape, b.shape
    assert K == K2, (a.shape, b.shape, mode)
    b_rows, b_cols = getattr(b, "row_limit", None), getattr(b, "col_limit", None)
    n_lim, k_lim = (b_rows, b_cols) if mode == "nt" else (b_cols, b_rows)
    m_lim = M // N_CHIPS if shard_out == 0 else None
    if shard_out == 1:
        n_lim = N // N_CHIPS
    extra = [t for t in (res, aux) if t is not None]
    out_bytes = 6 if epi == "relu2" else jnp.dtype(out_dtype).itemsize

    def footprint(tm, tn, tk):
        blocks = tm * tk * a.dtype.itemsize + tk * tn * 2 + tm * tn * (out_bytes + 4 * len(extra))
        return 2 * blocks + (tm * tn * 4 if tk < K else 0)

    tm, tn, tk = _tile(m_lim or M, MM_TM), _tile(n_lim or N, MM_TN), _tile(k_lim or K, MM_TK)
    while footprint(tm, tn, tk) > MM_VMEM_BUDGET and tk > MM_TK_MIN:
        tk = _tile(K, tk // 2)
    while footprint(tm, tn, tk) > MM_VMEM_BUDGET and tm > MM_TM_MIN:
        tm = _tile(M, tm // 2)
    nk = K // tk
    a_spec = pl.BlockSpec((tk, tm), lambda i, j, k: (k, i)) if mode == "tn" else pl.BlockSpec((tm, tk), lambda i, j, k: (i, k))
    if mode == "nt":
        b_block, b_rc = (tn, tk), (lambda i, j, k: (j, k))
    else:
        b_block, b_rc = (tk, tn), (lambda i, j, k: (k, j))
    if isinstance(b, (Stacked, Gathered)):
        b_spec, b = b.spec(*b_block, b_rc), b.arr
    else:
        b_spec = pl.BlockSpec(b_block, b_rc)
    mn_spec = pl.BlockSpec((tm, tn), lambda i, j, k: (i, j))
    dims = _DIMS[mode]

    def body(*refs):
        a_ref, b_ref = refs[0], refs[1]
        extra_refs = refs[2:2 + len(extra)]
        out_refs = refs[2 + len(extra):2 + len(extra) + (2 if epi == "relu2" else 1)]
        prod = lax.dot_general(a_ref[...].astype(BF16), b_ref[...].astype(BF16), dims, preferred_element_type=F32)

        def finish(r):
            if res is not None:
                r = r + extra_refs[0][...]
            if epi == "relu2":
                out_refs[0][...] = r
                t = jnp.maximum(r, 0.0)
                out_refs[1][...] = (t * t).astype(BF16)
            elif epi == "drelu2":
                out_refs[0][...] = (r * (2.0 * jnp.maximum(extra_refs[-1][...], 0.0))).astype(out_dtype)
            else:
                out_refs[0][...] = r.astype(out_dtype)

        if nk == 1:
            finish(prod)
        else:
            acc = refs[-1]
            k = pl.program_id(2)

            @pl.when(k == 0)
            def _():
                acc[...] = prod

            @pl.when(k > 0)
            def _():
                acc[...] += prod

            @pl.when(k == nk - 1)
            def _():
                finish(acc[...])

    if epi == "relu2":
        out_shape = (jax.ShapeDtypeStruct((M, N), F32), jax.ShapeDtypeStruct((M, N), BF16))
        out_specs = (mn_spec, mn_spec)
    elif shard_out == 0:
        per = (M // N_CHIPS) // tm
        out_shape = jax.ShapeDtypeStruct((N_CHIPS, M // N_CHIPS, N), out_dtype)
        out_specs = pl.BlockSpec((None, tm, tn), lambda i, j, k: (i // per, i % per, j))
    elif shard_out == 1:
        per = (N // N_CHIPS) // tn
        out_shape = jax.ShapeDtypeStruct((N_CHIPS, M, N // N_CHIPS), out_dtype)
        out_specs = pl.BlockSpec((None, tm, tn), lambda i, j, k: (j // per, i, j % per))
    else:
        out_shape = jax.ShapeDtypeStruct((M, N), out_dtype)
        out_specs = mn_spec
    return _pallas(
        body, name=name, grid=(M // tm, N // tn, nk),
        in_specs=[a_spec, b_spec] + [mn_spec] * len(extra), out_specs=out_specs, out_shape=out_shape,
        scratch_shapes=[pltpu.VMEM((tm, tn), F32)] if nk > 1 else [],
        compiler_params=_cparams("parallel", "parallel", "arbitrary"),
    )(a, b, *extra)


def _rows(ts, d, col=0):
    return pl.BlockSpec((ts, d), lambda i: (i, col))


def _gain(d):
    return pl.BlockSpec((1, d), lambda i: (0, 0))


def _accum(d):
    return pl.BlockSpec((SUBLANES, d), lambda i: (0, 0))


def _accumulate(ref, val):
    i = pl.program_id(0)

    @pl.when(i == 0)
    def _():
        ref[...] = jnp.zeros_like(ref)

    ref[...] += jnp.broadcast_to(jnp.sum(val, axis=0, keepdims=True), ref.shape)


def _rinv(x, n):
    return lax.rsqrt(jnp.sum(x * x, axis=-1, keepdims=True) / n + EPS)


def _norm_bwd(x, dy, g, n):
    r = _rinv(x, n)
    xhat = x * r
    dyg = dy * g
    dx = r * (dyg - xhat * (jnp.sum(dyg * xhat, axis=-1, keepdims=True) / n))
    return dx, dy * xhat


def _rms_fwd(x, g, name):
    S, D = x.shape
    ts = _tile(S, ROW_TILE, SUBLANES)

    def body(x_ref, g_ref, h_ref):
        xv = x_ref[...]
        h_ref[...] = (xv * _rinv(xv, D) * g_ref[...]).astype(BF16)

    return _pallas(body, name=name, grid=(S // ts,), in_specs=[_rows(ts, D), _gain(D)], out_specs=_rows(ts, D),
                   out_shape=jax.ShapeDtypeStruct((S, D), BF16), compiler_params=_cparams("parallel"))(x, g)


def _rms_bwd(dy, x, g, dres, name, want_bf16=True):
    S, D = x.shape
    ts = _tile(S, ROW_TILE, SUBLANES)

    def body(dy_ref, x_ref, g_ref, dres_ref, *outs):
        dx, dgc = _norm_bwd(x_ref[...], dy_ref[...], g_ref[...], D)
        dx = dx + dres_ref[...]
        outs[0][...] = dx
        if want_bf16:
            outs[1][...] = dx.astype(BF16)
        _accumulate(outs[-1], dgc)

    out_shape = [jax.ShapeDtypeStruct((S, D), F32)] + ([jax.ShapeDtypeStruct((S, D), BF16)] if want_bf16 else []) + [
        jax.ShapeDtypeStruct((SUBLANES, D), F32)]
    out_specs = [_rows(ts, D)] * (2 if want_bf16 else 1) + [_accum(D)]
    return _pallas(body, name=name, grid=(S // ts,), in_specs=[_rows(ts, D), _rows(ts, D), _gain(D), _rows(ts, D)],
                   out_specs=out_specs, out_shape=out_shape, compiler_params=_cparams("arbitrary"))(dy, x, g, dres)


def _lat_fwd(z, gq, gkv):
    S = z.shape[0]
    ts = _tile(S, ROW_TILE, SUBLANES)

    def body(z_ref, gq_ref, gkv_ref, q_ref, kv_ref):
        zq = z_ref[:, Z_Q:Z_KV]
        zkv = z_ref[:, Z_KV:Z_KPE]
        q_ref[...] = (zq * _rinv(zq, Q_LORA) * gq_ref[...]).astype(BF16)
        kv_ref[...] = (zkv * _rinv(zkv, KV_LORA) * gkv_ref[...]).astype(BF16)

    return _pallas(body, name="lat_fwd", grid=(S // ts,), in_specs=[_rows(ts, Z_LAT), _gain(Q_LORA), _gain(KV_LORA)],
                   out_specs=[_rows(ts, Q_LORA), _rows(ts, KV_LORA)],
                   out_shape=[jax.ShapeDtypeStruct((S, Q_LORA), BF16), jax.ShapeDtypeStruct((S, KV_LORA), BF16)],
                   compiler_params=_cparams("parallel"))(z, gq, gkv)


def _lat_bwd(dq, dkv, z, gq, gkv):
    S = z.shape[0]
    ts = _tile(S, ROW_TILE, SUBLANES)

    def body(dq_ref, dkv_ref, z_ref, gq_ref, gkv_ref, dlat_ref, dgq_ref, dgkv_ref):
        dxq, cq = _norm_bwd(z_ref[:, Z_Q:Z_KV], dq_ref[...], gq_ref[...], Q_LORA)
        dxkv, ckv = _norm_bwd(z_ref[:, Z_KV:Z_KPE], dkv_ref[...], gkv_ref[...], KV_LORA)
        dlat_ref[:, Z_Q:Z_KV] = dxq.astype(BF16)
        dlat_ref[:, Z_KV:Z_KPE] = dxkv.astype(BF16)
        _accumulate(dgq_ref, cq)
        _accumulate(dgkv_ref, ckv)

    return _pallas(body, name="lat_bwd", grid=(S // ts,),
                   in_specs=[_rows(ts, Q_LORA), _rows(ts, KV_LORA), _rows(ts, Z_LAT), _gain(Q_LORA), _gain(KV_LORA)],
                   out_specs=[_rows(ts, Z_LAT), _accum(Q_LORA), _accum(KV_LORA)],
                   out_shape=[jax.ShapeDtypeStruct((S, Z_LAT), BF16), jax.ShapeDtypeStruct((SUBLANES, Q_LORA), F32),
                              jax.ShapeDtypeStruct((SUBLANES, KV_LORA), F32)],
                   compiler_params=_cparams("arbitrary"))(dq, dkv, z, gq, gkv)


def _head_masks():
    lane = lax.broadcasted_iota(jnp.int32, (1, HEAD_PAD), 1)
    return lane < ROPE_LO, (lane >= ROPE_LO) & (lane < ROPE_HI), lane < ROPE_MID


def _seg_sum(t, m_n, m_r):
    sn = jnp.sum(jnp.where(m_n, t, 0.0), axis=-1, keepdims=True) / QK_NOPE
    sr = jnp.sum(jnp.where(m_r, t, 0.0), axis=-1, keepdims=True) / QK_ROPE
    return jnp.where(m_n, sn, sr)


def _rot_half(y, lo):
    half = QK_ROPE // 2
    return jnp.where(lo, -pltpu.roll(y, HEAD_PAD - half, 1), pltpu.roll(y, half, 1))


def _qk_fwd(q_raw, kv_raw, z, cs, sn, gq, gk):
    S = q_raw.shape[0]
    ts = _tile(S, ROW_TILE, SUBLANES)

    def body(q_ref, k_ref, v_ref, kpe_ref, cs_ref, sn_ref, gq_ref, gk_ref, qf_ref, kf_ref, vb_ref):
        m_n, m_r, lo = _head_masks()
        cos, sin, gqv, gkv = cs_ref[...], sn_ref[...], gq_ref[...], gk_ref[...]

        def norm(x, g):
            return x * lax.rsqrt(_seg_sum(x * x, m_n, m_r) + EPS) * g

        def rope(y):
            return y * cos + _rot_half(y, lo) * sin

        kr = rope(norm(kpe_ref[...], gkv))
        lane = lax.broadcasted_iota(jnp.int32, (1, HEAD_PAD), 1)
        for h in range(N_HEADS):
            sl = slice(h * HEAD_PAD, (h + 1) * HEAD_PAD)
            qf_ref[:, sl] = rope(norm(q_ref[:, sl], gqv)).astype(BF16)
            kf_ref[:, sl] = (norm(k_ref[:, sl], gkv) + kr).astype(BF16)
            vb_ref[:, sl] = jnp.where(lane == V_HEAD, 1.0, v_ref[:, sl]).astype(BF16)

    hd = jax.ShapeDtypeStruct((S, HP), BF16)
    return _pallas(body, name="qk_fwd", grid=(S // ts,),
                   in_specs=[_rows(ts, HP), _rows(ts, HP, 0), _rows(ts, HP, 1), _rows(ts, HEAD_PAD, Z_KPE // HEAD_PAD),
                             _rows(ts, HEAD_PAD), _rows(ts, HEAD_PAD), _gain(HEAD_PAD), _gain(HEAD_PAD)],
                   out_specs=[_rows(ts, HP)] * 3, out_shape=[hd, hd, hd],
                   compiler_params=_cparams("parallel"))(q_raw, kv_raw, kv_raw, z, cs, sn, gq, gk)


def _qk_bwd(dqf, dkf, dv, q_raw, kv_raw, z, cs, sn, gq, gk):
    S = q_raw.shape[0]
    ts = _tile(S, ROW_TILE, SUBLANES)

    def body(dqf_ref, dkf_ref, dv_ref, q_ref, k_ref, kpe_ref, cs_ref, sn_ref, gq_ref, gk_ref,
             dq_ref, dkv_ref, dkpe_ref, dgq_ref, dgk_ref):
        m_n, m_r, lo = _head_masks()
        cos, sin, gqv, gkv = cs_ref[...], sn_ref[...], gq_ref[...], gk_ref[...]

        def rope_t(w):
            return w * cos - jnp.where(m_r, _rot_half(w * sin, lo), 0.0)

        def norm_bwd(x, dy, g):
            r = lax.rsqrt(_seg_sum(x * x, m_n, m_r) + EPS)
            xhat = x * r
            dyg = dy * g
            return r * (dyg - xhat * _seg_sum(dyg * xhat, m_n, m_r)), dy * xhat

        accq = jnp.zeros((ts, HEAD_PAD), F32)
        acck = jnp.zeros((ts, HEAD_PAD), F32)
        dkr = jnp.zeros((ts, HEAD_PAD), F32)
        for h in range(N_HEADS):
            sl = slice(h * HEAD_PAD, (h + 1) * HEAD_PAD)
            dx, c = norm_bwd(q_ref[:, sl], rope_t(dqf_ref[:, sl]), gqv)
            dq_ref[:, sl] = dx.astype(BF16)
            accq = accq + c
            dk = dkf_ref[:, sl]
            dkr = dkr + jnp.where(m_r, dk, 0.0)
            dx, c = norm_bwd(k_ref[:, sl], jnp.where(m_n, dk, 0.0), gkv)
            dkv_ref[:, sl] = dx.astype(BF16)
            acck = acck + c
            dkv_ref[:, HP + h * HEAD_PAD:HP + (h + 1) * HEAD_PAD] = dv_ref[:, sl].astype(BF16)
        dx, c = norm_bwd(kpe_ref[...], rope_t(dkr), gkv)
        dkpe_ref[...] = dx.astype(BF16)
        _accumulate(dgq_ref, accq)
        _accumulate(dgk_ref, acck + c)

    return _pallas(body, name="qk_bwd", grid=(S // ts,),
                   in_specs=[_rows(ts, HP)] * 4 + [_rows(ts, HP, 0), _rows(ts, HEAD_PAD, Z_KPE // HEAD_PAD),
                                                   _rows(ts, HEAD_PAD), _rows(ts, HEAD_PAD), _gain(HEAD_PAD), _gain(HEAD_PAD)],
                   out_specs=[_rows(ts, HP), _rows(ts, 2 * HP), _rows(ts, HEAD_PAD), _accum(HEAD_PAD), _accum(HEAD_PAD)],
                   out_shape=[jax.ShapeDtypeStruct((S, HP), BF16), jax.ShapeDtypeStruct((S, 2 * HP), BF16),
                              jax.ShapeDtypeStruct((S, HEAD_PAD), BF16), jax.ShapeDtypeStruct((SUBLANES, HEAD_PAD), F32),
                              jax.ShapeDtypeStruct((SUBLANES, HEAD_PAD), F32)],
                   compiler_params=_cparams("arbitrary"))(dqf, dkf, dv, q_raw, kv_raw, z, cs, sn, gq, gk)


def _causal(t):
    row = lax.broadcasted_iota(jnp.int32, (t, t), 0)
    col = lax.broadcasted_iota(jnp.int32, (t, t), 1)
    return col <= row


def _attn_tiles(S):
    t = _tile(S, ATT_TILE, SUBLANES)
    return t, min(ATT_CHAINS, S // t)


def _attn_fwd(qf, kf, vb):
    S = qf.shape[0]
    t, nc = _attn_tiles(S)
    tq = nc * t

    def body(q_ref, k_ref, v_ref, o_ref, lse_ref, m_ref, acc_ref):
        qt = pl.program_id(1)
        m_ref[...] = jnp.full_like(m_ref, -jnp.inf)
        acc_ref[...] = jnp.zeros_like(acc_ref)

        def chain(a, kb, vb_, masked):
            sub = slice(a * t, (a + 1) * t)
            s = lax.dot_general(q_ref[sub, :], kb, _DIMS["nt"], preferred_element_type=F32) * ATT_SCALE_LOG2
            if masked:
                s = jnp.where(_causal(t), s, -jnp.inf)
            m_old = m_ref[sub, :]
            m_new = jnp.maximum(m_old, jnp.max(s, axis=-1, keepdims=True))
            p = jnp.exp2(s - jnp.concatenate([m_new] * (t // HEAD_PAD), axis=1))
            acc_ref[sub, :] = jnp.exp2(m_old - m_new) * acc_ref[sub, :] + jnp.dot(p.astype(BF16), vb_, preferred_element_type=F32)
            m_ref[sub, :] = m_new

        def trip(j, carry):
            rows = pl.ds(pl.multiple_of(j * t, t), t)
            kb, vb_ = k_ref[rows, :], v_ref[rows, :]
            for a in range(nc):
                chain(a, kb, vb_, False)
            return carry

        lax.fori_loop(0, nc * qt, trip, 0)
        for d in range(nc):
            rows = pl.ds(pl.multiple_of((nc * qt + d) * t, t), t)
            kb, vb_ = k_ref[rows, :], v_ref[rows, :]
            for a in range(d, nc):
                chain(a, kb, vb_, a == d)
        acc = acc_ref[...]
        l = acc[:, V_HEAD:V_HEAD + 1]
        lane = lax.broadcasted_iota(jnp.int32, (1, HEAD_PAD), 1)
        o_ref[...] = jnp.where(lane < V_HEAD, acc / l, 0.0)
        lse_ref[...] = m_ref[...] + jnp.log(l) * LOG2E

    tile = pl.BlockSpec((tq, HEAD_PAD), lambda h, i: (i, h))
    full = pl.BlockSpec((S, HEAD_PAD), lambda h, i: (0, h))
    return _pallas(body, name="attn_fwd", grid=(N_HEADS, S // tq), in_specs=[tile, full, full],
                   out_specs=[tile, pl.BlockSpec((None, tq, HEAD_PAD), lambda h, i: (h, i, 0))],
                   out_shape=[jax.ShapeDtypeStruct((S, HP), F32), jax.ShapeDtypeStruct((N_HEADS, S, HEAD_PAD), F32)],
                   scratch_shapes=[pltpu.VMEM((tq, HEAD_PAD), F32), pltpu.VMEM((tq, HEAD_PAD), F32)],
                   compiler_params=_cparams("parallel", "arbitrary"))(qf, kf, vb)


def _attn_bwd(qf, kf, vb, o, do, lse):
    S = qf.shape[0]
    t, nc = _attn_tiles(S)
    tkv = nc * t
    nq = S // t

    def body(q_ref, k_ref, v_ref, o_ref, do_ref, lse_ref, dq_ref, dk_ref, dv_ref):
        kt = pl.program_id(1)

        @pl.when(kt == 0)
        def _():
            dq_ref[...] = jnp.zeros_like(dq_ref)

        dk_ref[...] = jnp.zeros_like(dk_ref)
        dv_ref[...] = jnp.zeros_like(dv_ref)

        def q_block(rows):
            dof = do_ref[rows, :]
            delta = jnp.sum(dof * o_ref[rows, :], axis=-1, keepdims=True)
            return q_ref[rows, :], dof.astype(BF16), lse_ref[rows, :][:, :1], delta

        def chain(b, qv, masked):
            q, dob, lse, delta = qv
            sub = slice(b * t, (b + 1) * t)
            kb, vb_ = k_ref[sub, :], v_ref[sub, :]
            s = lax.dot_general(q, kb, _DIMS["nt"], preferred_element_type=F32) * ATT_SCALE_LOG2
            p = jnp.exp2(s - lse)
            if masked:
                p = jnp.where(_causal(t), p, 0.0)
            dv_ref[sub, :] += lax.dot_general(p.astype(BF16), dob, _DIMS["tn"], preferred_element_type=F32)
            dp = lax.dot_general(dob, vb_, _DIMS["nt"], preferred_element_type=F32)
            ds = (p * (dp - delta) * ATT_SCALE).astype(BF16)
            dk_ref[sub, :] += lax.dot_general(ds, q, _DIMS["tn"], preferred_element_type=F32)
            return jnp.dot(ds, kb, preferred_element_type=F32)

        for a in range(nc):
            rows = pl.ds(pl.multiple_of((nc * kt + a) * t, t), t)
            qv = q_block(rows)
            dq_ref[rows, :] += sum(chain(b, qv, b == a) for b in range(a + 1))

        def trip(i, carry):
            rows = pl.ds(pl.multiple_of(i * t, t), t)
            qv = q_block(rows)
            dq_ref[rows, :] += sum(chain(b, qv, False) for b in range(nc))
            return carry

        lax.fori_loop(nc * (kt + 1), nq, trip, 0)

    tile = pl.BlockSpec((tkv, HEAD_PAD), lambda h, j: (j, h))
    full = pl.BlockSpec((S, HEAD_PAD), lambda h, j: (0, h))
    hd = jax.ShapeDtypeStruct((S, HP), F32)
    return _pallas(body, name="attn_bwd", grid=(N_HEADS, S // tkv),
                   in_specs=[full, tile, tile, full, full, pl.BlockSpec((None, S, HEAD_PAD), lambda h, j: (h, 0, 0))],
                   out_specs=[full, tile, tile], out_shape=[hd, hd, hd],
                   compiler_params=_cparams("arbitrary", "arbitrary"))(qf, kf, vb, o, do, lse)


def _shift_down(u, j, row):
    return jnp.where(row >= j, pltpu.roll(u, j, 0), 0.0)


def _shift_up(u, j, row, s):
    return jnp.where(row < s - j, pltpu.roll(u, s - j, 0), 0.0)


def _conv_cols(s, first_tile):
    return pl.BlockSpec((s, LANES), lambda cb: (0, first_tile + cb))


def _conv_fwd(z, cw8):
    S = z.shape[0]

    def body(gb_ref, gc_ref, xin_ref, w_ref, out_ref):
        row = lax.broadcasted_iota(jnp.int32, (S, LANES), 0)
        u = gc_ref[...] * xin_ref[...]
        y = w_ref[0:1, :] * u
        for j in range(1, CONV_TAPS):
            y = y + w_ref[j:j + 1, :] * _shift_down(u, j, row)
        out_ref[...] = gb_ref[...] * y

    return _pallas(body, name="conv_fwd", grid=(CONV_WIDTH // LANES,),
                   in_specs=[_conv_cols(S, Z_GB // LANES), _conv_cols(S, Z_GC // LANES), _conv_cols(S, Z_XIN // LANES),
                             pl.BlockSpec((SUBLANES, LANES), lambda cb: (0, cb))],
                   out_specs=_conv_cols(S, 0), out_shape=jax.ShapeDtypeStruct((S, CONV_WIDTH), F32),
                   compiler_params=_cparams("parallel"))(z, z, z, cw8)


def _conv_bwd(dconv, z, cw8):
    S = z.shape[0]

    def body(d_ref, gb_ref, gc_ref, xin_ref, w_ref, dgb_ref, dgc_ref, dxin_ref, dw_ref):
        row = lax.broadcasted_iota(jnp.int32, (S, LANES), 0)
        gc, xin, d = gc_ref[...], xin_ref[...], d_ref[...]
        u = gc * xin
        dy = d * gb_ref[...]
        y = w_ref[0:1, :] * u
        du = w_ref[0:1, :] * dy
        dw = [jnp.sum(dy * u, axis=0, keepdims=True)]
        for j in range(1, CONV_TAPS):
            uj = _shift_down(u, j, row)
            y = y + w_ref[j:j + 1, :] * uj
            du = du + w_ref[j:j + 1, :] * _shift_up(dy, j, row, S)
            dw.append(jnp.sum(dy * uj, axis=0, keepdims=True))
        dgb_ref[...] = (d * y).astype(BF16)
        dgc_ref[...] = (du * xin).astype(BF16)
        dxin_ref[...] = (du * gc).astype(BF16)
        tap = lax.broadcasted_iota(jnp.int32, (SUBLANES, LANES), 0)
        dw_ref[...] = sum(jnp.where(tap == j, dw[j], 0.0) for j in range(CONV_TAPS))

    col = _conv_cols(S, 0)
    sd = jax.ShapeDtypeStruct((S, CONV_WIDTH), BF16)
    return _pallas(body, name="conv_bwd", grid=(CONV_WIDTH // LANES,),
                   in_specs=[col, _conv_cols(S, Z_GB // LANES), _conv_cols(S, Z_GC // LANES), _conv_cols(S, Z_XIN // LANES),
                             pl.BlockSpec((SUBLANES, LANES), lambda cb: (0, cb))],
                   out_specs=[col, col, col, pl.BlockSpec((SUBLANES, LANES), lambda cb: (0, cb))],
                   out_shape=[sd, sd, sd, jax.ShapeDtypeStruct((SUBLANES, CONV_WIDTH), F32)],
                   compiler_params=_cparams("parallel"))(dconv, z, z, z, cw8)


def _mix_fwd(o, conv, ga, gc):
    S = o.shape[0]
    ts = _tile(S, ROW_TILE, SUBLANES)

    def body(o_ref, c_ref, ga_ref, gc_ref, out_ref):
        ov, cv = o_ref[...], c_ref[...]
        out_ref[:, :HP] = (ov * _rinv(ov, ATTN_WIDTH) * ga_ref[...]).astype(BF16)
        out_ref[:, HP:] = (cv * _rinv(cv, CONV_WIDTH) * gc_ref[...]).astype(BF16)

    return _pallas(body, name="mix_fwd", grid=(S // ts,),
                   in_specs=[_rows(ts, HP), _rows(ts, CONV_WIDTH), _gain(HP), _gain(CONV_WIDTH)],
                   out_specs=_rows(ts, HP + CONV_WIDTH), out_shape=jax.ShapeDtypeStruct((S, HP + CONV_WIDTH), BF16),
                   compiler_params=_cparams("parallel"))(o, conv, ga, gc)


def _mix_bwd(dmixed, o, conv, ga, gc):
    S = o.shape[0]
    ts = _tile(S, ROW_TILE, SUBLANES)

    def body(d_ref, o_ref, c_ref, ga_ref, gc_ref, do_ref, dc_ref, dga_ref, dgc_ref):
        dx, ca = _norm_bwd(o_ref[...], d_ref[:, :HP], ga_ref[...], ATTN_WIDTH)
        do_ref[...] = dx
        dx, cc = _norm_bwd(c_ref[...], d_ref[:, HP:], gc_ref[...], CONV_WIDTH)
        dc_ref[...] = dx
        _accumulate(dga_ref, ca)
        _accumulate(dgc_ref, cc)

    return _pallas(body, name="mix_bwd", grid=(S // ts,),
                   in_specs=[_rows(ts, HP + CONV_WIDTH), _rows(ts, HP), _rows(ts, CONV_WIDTH), _gain(HP), _gain(CONV_WIDTH)],
                   out_specs=[_rows(ts, HP), _rows(ts, CONV_WIDTH), _accum(HP), _accum(CONV_WIDTH)],
                   out_shape=[jax.ShapeDtypeStruct((S, HP), F32), jax.ShapeDtypeStruct((S, CONV_WIDTH), F32),
                              jax.ShapeDtypeStruct((SUBLANES, HP), F32), jax.ShapeDtypeStruct((SUBLANES, CONV_WIDTH), F32)],
                   compiler_params=_cparams("arbitrary"))(dmixed, o, conv, ga, gc)


def _ple_fwd(x, gl, pe):
    S, D = x.shape
    ts = _tile(S, ROW_TILE, SUBLANES)

    def body(x_ref, gl_ref, pe_ref, out_ref):
        out_ref[...] = x_ref[...] + jax.nn.sigmoid(gl_ref[...]) * pe_ref[...]

    return _pallas(body, name="ple_fwd", grid=(S // ts,), in_specs=[_rows(ts, D)] * 3, out_specs=_rows(ts, D),
                   out_shape=jax.ShapeDtypeStruct((S, D), F32), compiler_params=_cparams("parallel"))(x, gl, pe)


def _ple_bwd(dx, gl, pe):
    S, D = dx.shape
    ts = _tile(S, ROW_TILE, SUBLANES)

    def body(dx_ref, gl_ref, pe_ref, dpe_ref, dgl_ref):
        d = dx_ref[...]
        gate = jax.nn.sigmoid(gl_ref[...])
        dpe_ref[...] = (d * gate).astype(BF16)
        dgl_ref[...] = (d * pe_ref[...] * (gate * (1.0 - gate))).astype(BF16)

    sd = jax.ShapeDtypeStruct((S, D), BF16)
    return _pallas(body, name="ple_bwd", grid=(S // ts,), in_specs=[_rows(ts, D)] * 3, out_specs=[_rows(ts, D)] * 2,
                   out_shape=[sd, sd], compiler_params=_cparams("parallel"))(dx, gl, pe)


def _loss_grad(y, target):
    S, D = y.shape
    ts = _tile(S, ROW_TILE, SUBLANES)

    def body(y_ref, t_ref, dy_ref, sq_ref):
        e = y_ref[...] - t_ref[...]
        dy_ref[...] = e / D

        @pl.when(pl.program_id(0) == 0)
        def _():
            sq_ref[...] = jnp.zeros_like(sq_ref)

        sq_ref[...] += jnp.broadcast_to(jnp.sum(jnp.sum(e * e, axis=1, keepdims=True), axis=0, keepdims=True), sq_ref.shape)

    return _pallas(body, name="loss_grad", grid=(S // ts,), in_specs=[_rows(ts, D)] * 2,
                   out_specs=[_rows(ts, D), _accum(LANES)],
                   out_shape=[jax.ShapeDtypeStruct((S, D), F32), jax.ShapeDtypeStruct((SUBLANES, LANES), F32)],
                   compiler_params=_cparams("arbitrary"))(y, target)


def _adamw(w, g, m, v, name):
    L, R, C = w.shape
    tr = _tile(R, ROW_TILE, SUBLANES)

    def body(w_ref, g_ref, m_ref, v_ref, d_ref, m2_ref, v2_ref):
        gv = g_ref[...]
        m2 = ADAM_B1 * m_ref[...] + (1.0 - ADAM_B1) * gv
        v2 = ADAM_B2 * v_ref[...] + (1.0 - ADAM_B2) * (gv * gv)
        m_hat = m2 / (1.0 - ADAM_B1 ** ADAM_STEP)
        v_hat = v2 / (1.0 - ADAM_B2 ** ADAM_STEP)
        d_ref[...] = -ADAM_LR * (m_hat / (jnp.sqrt(v_hat) + ADAM_EPS) + ADAM_WD * w_ref[...])
        m2_ref[...] = m2
        v2_ref[...] = v2

    sd = jax.ShapeDtypeStruct((L, R, C), F32)
    spec = pl.BlockSpec((None, tr, C), lambda l, i: (l, i, 0))
    return _pallas(body, name=name, grid=(L, R // tr), in_specs=[spec] * 4, out_specs=[spec] * 3, out_shape=[sd, sd, sd],
                   compiler_params=_cparams("parallel", "parallel"))(w, g, m, v)


def _place():
    return lax.axis_index("x"), lax.axis_index("y"), lax.axis_index("c")


def _other_chips(x, y):
    return [(1 - x, y), (x, 1 - y), (1 - x, 1 - y)]


HBM = pl.BlockSpec(memory_space=pl.ANY)
VMEM_SPEC = pl.BlockSpec(memory_space=pltpu.VMEM)


def _remote_copy(send_sems, recv_sems, k, src, dst, to):
    return pltpu.make_async_remote_copy(src_ref=src, dst_ref=dst, send_sem=send_sems.at[k], recv_sem=recv_sems.at[k],
                                        device_id=to, device_id_type=MESH)


def _comm_call(body, name, arrays, out_shapes, n_remote, in_place=False):
    scratch = [pltpu.SemaphoreType.DMA((n_remote,)), pltpu.SemaphoreType.DMA((n_remote,))]
    aliases = {i: i for i in range(len(arrays))} if in_place else {}
    return _pallas(body, name=name, in_specs=[HBM] * len(arrays), out_specs=[HBM] * len(out_shapes), out_shape=out_shapes,
                   scratch_shapes=scratch, input_output_aliases=aliases,
                   compiler_params=pltpu.CompilerParams(has_side_effects=True))(*arrays)


def _shard_slab(w, chip_arr, name):
    L, R, C = w.shape
    tr = _tile(R, ROW_TILE, 16)

    def body(j_ref, w_ref, out_ref):
        out_ref[...] = w_ref[...].astype(BF16)

    grid_spec = pltpu.PrefetchScalarGridSpec(
        num_scalar_prefetch=1, grid=(L, R // tr),
        in_specs=[pl.BlockSpec((None, tr, C), lambda l, i, j_ref: (l, i, 0))],
        out_specs=pl.BlockSpec((None, None, tr, C), lambda l, i, j_ref: (j_ref[0], l, i, 0)))
    return _pallas(body, name=name, grid_spec=grid_spec, out_shape=jax.ShapeDtypeStruct((N_CHIPS, L, R, C), BF16),
                   compiler_params=_cparams("parallel", "parallel"))(chip_arr, w)


def _allgather_weights(slabs):
    n = len(slabs)

    def body(*refs):
        out_refs = refs[n:2 * n]
        send_sems, recv_sems = refs[2 * n:]
        x, y, c = _place()
        me, sibling = (x, y, c), (x, y, 1 - c)
        chips = _other_chips(x, y)
        copy = functools.partial(_remote_copy, send_sems, recv_sems)

        def half(ref, which):
            h = ref.shape[-2] // 2
            return pl.ds(which * h, h)

        first, passed = [], []
        for i, out_ref in enumerate(out_refs):
            blk = out_ref.at[2 * x + y, :, half(out_ref, c)]
            for k, (cx, cy) in enumerate(chips):
                first.append(copy(6 * i + k, blk, blk, (cx, cy, c)))
                first[-1].start()
        for i, out_ref in enumerate(out_refs):
            for k, (cx, cy) in enumerate(chips):
                blk = out_ref.at[2 * cx + cy, :, half(out_ref, c)]
                copy(6 * i + k, blk, blk, me).wait_recv()
                passed.append(copy(6 * i + 3 + k, blk, blk, sibling))
                passed[-1].start()
        for i, out_ref in enumerate(out_refs):
            for k, (cx, cy) in enumerate(chips):
                blk = out_ref.at[2 * cx + cy, :, half(out_ref, 1 - c)]
                copy(6 * i + 3 + k, blk, blk, me).wait_recv()
        for cp in first + passed:
            cp.wait_send()

    return _comm_call(body, "allgather_weights", slabs, [jax.ShapeDtypeStruct(s.shape, s.dtype) for s in slabs], 6 * n,
                      in_place=True)


def _swap_halves(gs):
    n = len(gs)

    def body(*refs):
        g_refs, out_refs, (send_sems, recv_sems) = refs[:n], refs[n:2 * n], refs[2 * n:]
        x, y, c = _place()
        cps = []
        for i, (g_ref, out_ref) in enumerate(zip(g_refs, out_refs)):
            h = g_ref.shape[2] // 2
            cps.append(_remote_copy(send_sems, recv_sems, i, g_ref.at[:, :, pl.ds((1 - c) * h, h)], out_ref, (x, y, 1 - c)))
            cps[-1].start()
        for cp in cps:
            cp.wait()

    return _comm_call(body, "grad_swap_halves", gs,
                      [jax.ShapeDtypeStruct(g.shape[:2] + (g.shape[2] // 2, g.shape[3]), g.dtype) for g in gs], n)


def _add_half(g, r1, c_arr, name):
    L, n, R, C = g.shape
    H = R // 2
    th = _tile(H, ROW_TILE, 16)
    nb = H // th

    def body(c_ref, g_ref, r_ref, out_ref):
        out_ref[...] = (g_ref[...].astype(F32) + r_ref[...].astype(F32)).astype(BF16)

    blk = (None, None, th, C)
    grid_spec = pltpu.PrefetchScalarGridSpec(
        num_scalar_prefetch=1, grid=(L, n, nb),
        in_specs=[pl.BlockSpec(blk, lambda l, s, i, c_ref: (l, s, c_ref[0] * nb + i, 0)),
                  pl.BlockSpec(blk, lambda l, s, i, c_ref: (l, s, i, 0))],
        out_specs=pl.BlockSpec(blk, lambda l, s, i, c_ref: (l, s, i, 0)))
    return _pallas(body, name=name, grid_spec=grid_spec, out_shape=jax.ShapeDtypeStruct((L, n, H, C), BF16),
                   compiler_params=_cparams("parallel", "parallel", "parallel"))(c_arr, g, r1)


def _exchange_chips(ps):
    n = len(ps)

    def body(*refs):
        p_refs, out_refs, (send_sems, recv_sems) = refs[:n], refs[n:2 * n], refs[2 * n:]
        x, y, c = _place()
        cps = []
        for i, (p_ref, out_ref) in enumerate(zip(p_refs, out_refs)):
            for k, (cx, cy) in enumerate(_other_chips(x, y)):
                cps.append(_remote_copy(send_sems, recv_sems, 3 * i + k, p_ref.at[:, 2 * cx + cy], out_ref.at[k], (cx, cy, c)))
                cps[-1].start()
        for cp in cps:
            cp.wait()

    return _comm_call(body, "grad_exchange_chips", ps,
                      [jax.ShapeDtypeStruct((N_CHIPS - 1, p.shape[0]) + p.shape[2:], p.dtype) for p in ps], 3 * n)


def _add_chips(p, r2, chip_arr, c_arr, name):
    L, n, H, C = p.shape
    th = _tile(H, ROW_TILE, 16)
    nb = H // th

    def body(j_ref, c_ref, p_ref, r_ref, out_ref):
        s = p_ref[...].astype(F32)
        for k in range(N_CHIPS - 1):
            s = s + r_ref[k].astype(F32)
        out_ref[...] = s

    grid_spec = pltpu.PrefetchScalarGridSpec(
        num_scalar_prefetch=2, grid=(L, nb),
        in_specs=[pl.BlockSpec((None, None, th, C), lambda l, i, j_ref, c_ref: (l, j_ref[0], i, 0)),
                  pl.BlockSpec((N_CHIPS - 1, None, th, C), lambda l, i, j_ref, c_ref: (0, l, i, 0))],
        out_specs=pl.BlockSpec((None, th, C), lambda l, i, j_ref, c_ref: (l, c_ref[0] * nb + i, 0)))
    return _pallas(body, name=name, grid_spec=grid_spec, out_shape=jax.ShapeDtypeStruct((L, 2 * H, C), F32),
                   compiler_params=_cparams("parallel", "parallel"))(chip_arr, c_arr, p, r2)


def _join_halves(fs):
    n = len(fs)

    def body(*refs):
        out_refs, (send_sems, recv_sems) = refs[n:2 * n], refs[2 * n:]
        x, y, c = _place()
        cps = []
        for i, out_ref in enumerate(out_refs):
            h = out_ref.shape[1] // 2
            mine = out_ref.at[:, pl.ds(c * h, h)]
            cps.append(_remote_copy(send_sems, recv_sems, i, mine, mine, (x, y, 1 - c)))
            cps[-1].start()
        for cp in cps:
            cp.wait()

    return _comm_call(body, "grad_join_halves", fs, [jax.ShapeDtypeStruct(f.shape, f.dtype) for f in fs], n, in_place=True)


def _allgather_small(v, name):
    R, C = v.shape

    def body(v_ref, out_ref, send_sems, recv_sems):
        x, y, c = _place()
        me = 4 * x + 2 * y + c
        out_ref[me] = v_ref[...]
        cps = []
        for k in range(1, N_DEV):
            peer = (1 - x if k & 4 else x, 1 - y if k & 2 else y, 1 - c if k & 1 else c)
            cps.append(pltpu.make_async_remote_copy(src_ref=v_ref, dst_ref=out_ref.at[me], send_sem=send_sems.at[k - 1],
                                                    recv_sem=recv_sems.at[k - 1], device_id=peer, device_id_type=MESH))
        for cp in cps:
            cp.start()
        for cp in cps:
            cp.wait()

    return _pallas(body, name=name, in_specs=[VMEM_SPEC], out_specs=VMEM_SPEC,
                   out_shape=jax.ShapeDtypeStruct((N_DEV, R, C), v.dtype),
                   scratch_shapes=[pltpu.SemaphoreType.DMA((N_DEV - 1,)), pltpu.SemaphoreType.DMA((N_DEV - 1,))],
                   compiler_params=pltpu.CompilerParams(has_side_effects=True))(v)


def _sum_devices(g):
    n, R, C = g.shape

    def body(g_ref, out_ref):
        s = g_ref[0]
        for d in range(1, n):
            s = s + g_ref[d]
        out_ref[...] = s

    return _pallas(body, name="sum_devices", in_specs=[VMEM_SPEC], out_specs=VMEM_SPEC,
                   out_shape=jax.ShapeDtypeStruct((R, C), g.dtype))(g)


def _pad_heads(a, width):
    lead = a.shape[:-1]
    a = a.reshape(lead + (N_HEADS, width))
    a = jnp.pad(a, [(0, 0)] * len(lead) + [(0, 0), (0, HEAD_PAD - width)])
    return a.reshape(lead + (HP,))


def _unpad_heads(a, width):
    lead = a.shape[:-1]
    return a.reshape(lead + (N_HEADS, HEAD_PAD))[..., :width].reshape(lead + (N_HEADS * width,))


def _prepare_weights(gathered):
    L = gathered["w_in"].shape[1]
    cols = lambda g: g.transpose(1, 2, 0, 3).reshape(L, g.shape[2], -1)
    rows = lambda g: g.transpose(1, 0, 2, 3).reshape(L, -1, g.shape[3])
    w_in = cols(gathered["w_in"])
    kpe = jnp.pad(w_in[:, :, 640:672], ((0, 0), (0, 0), (ROPE_LO, HEAD_PAD - ROPE_HI)))
    w_ukv = cols(gathered["w_ukv"]).reshape(L, KV_LORA, N_HEADS, QK_NOPE + V_HEAD)
    w_o = rows(gathered["w_o"])
    attn_rows = _pad_heads(w_o[:, :ATTN_WIDTH].transpose(0, 2, 1), V_HEAD).transpose(0, 2, 1)
    return dict(
        gathered,
        w_in=jnp.concatenate([w_in[:, :, :640], kpe, w_in[:, :, 672:]], axis=2),
        w_uq=_pad_heads(cols(gathered["w_uq"]), QK_HEAD),
        w_ukv=jnp.concatenate([_pad_heads(w_ukv[..., :QK_NOPE].reshape(L, KV_LORA, -1), QK_NOPE),
                               _pad_heads(w_ukv[..., QK_NOPE:].reshape(L, KV_LORA, -1), V_HEAD)], axis=2),
        w_o=jnp.concatenate([attn_rows, w_o[:, ATTN_WIDTH:]], axis=1),
    )


def _layer_weights(prep, small, l):
    row = lambda name: small[name][l].reshape(1, -1)
    headrow = lambda a, b: jnp.pad(jnp.concatenate([small[a][l], small[b][l]]), (0, HEAD_PAD - QK_HEAD)).reshape(1, HEAD_PAD)
    return dict(
        w_in=Stacked(prep["w_in"], l), w_uq=Stacked(prep["w_uq"], l), w_ukv=Stacked(prep["w_ukv"], l),
        w_o=Stacked(prep["w_o"], l), w_up=Gathered(prep["w_up"], l, 1), w_down=Gathered(prep["w_down"], l, 0),
        w_ple_gate=Gathered(prep["w_ple_gate"], l, 0), w_ple=Gathered(prep["w_ple"], l, 1),
        g_mix=row("g_mix"), g_q_lat=row("g_q_lat"), g_kv_lat=row("g_kv_lat"), g_mlp=row("g_mlp"), g_ple=row("g_ple"),
        g_out_conv=row("g_out_conv"), g_out_attn=_pad_heads(small["g_out_attn"][l], V_HEAD).reshape(1, HP),
        gq=headrow("g_qn_nope", "g_qn_rope"), gk=headrow("g_kn_nope", "g_kn_rope"),
        cw8=jnp.pad(small["conv_w"][l], ((0, SUBLANES - CONV_TAPS), (0, 0))),
    )


def _unpad_grads(gp):
    dw_in = gp["w_in"]
    dw_ukv = gp["w_ukv"]
    k_part = dw_ukv[:, :HP].reshape(KV_LORA, N_HEADS, HEAD_PAD)[..., :QK_NOPE]
    v_part = dw_ukv[:, HP:].reshape(KV_LORA, N_HEADS, HEAD_PAD)[..., :V_HEAD]
    dw_o = gp["w_o"]
    first = lambda name: gp[name][0]
    col_shards = lambda a: a.reshape(a.shape[0], N_CHIPS, -1).transpose(1, 0, 2)
    return dict(
        w_in=col_shards(jnp.concatenate([dw_in[:, :640], dw_in[:, Z_KPE + ROPE_LO:Z_KPE + ROPE_HI], dw_in[:, Z_GB:]], axis=1)),
        w_uq=col_shards(_unpad_heads(gp["w_uq"], QK_HEAD)),
        w_ukv=col_shards(jnp.concatenate([k_part, v_part], axis=-1).reshape(KV_LORA, -1)),
        w_o=jnp.concatenate([_unpad_heads(dw_o[:HP].T, V_HEAD).T, dw_o[HP:]], axis=0).reshape(N_CHIPS, -1, D_MODEL),
        w_up=gp["w_up"], w_down=gp["w_down"], w_ple_gate=gp["w_ple_gate"], w_ple=gp["w_ple"],
        g_mix=first("g_mix"), g_q_lat=first("g_q_lat"), g_kv_lat=first("g_kv_lat"), g_mlp=first("g_mlp"),
        g_ple=first("g_ple"), g_out_conv=first("g_out_conv"), g_out_attn=_unpad_heads(first("g_out_attn"), V_HEAD),
        g_qn_nope=gp["gq"][0, :QK_NOPE], g_qn_rope=gp["gq"][0, QK_NOPE:QK_HEAD],
        g_kn_nope=gp["gk"][0, :QK_NOPE], g_kn_rope=gp["gk"][0, QK_NOPE:QK_HEAD],
        conv_w=gp["cw8"][:CONV_TAPS],
    )


def _rope_tables(positions):
    inv_freq = 1.0 / (ROPE_THETA ** (jnp.arange(0, QK_ROPE, 2, dtype=F32) / QK_ROPE))
    ang = positions.astype(F32)[:, None] * inv_freq
    cos, sin = jnp.cos(ang), jnp.sin(ang)
    pad = lambda t, v: jnp.pad(jnp.concatenate([t, t], axis=1), ((0, 0), (ROPE_LO, HEAD_PAD - ROPE_HI)), constant_values=v)
    return pad(cos, 1.0), pad(sin, 0.0)


def _layer_fwd(x0, p_l, W, cs, sn):
    h = _rms_fwd(x0, W["g_mix"], "rms_mix")
    z = _mm(h, W["w_in"], name="mm_in")
    qln, kvln = _lat_fwd(z, W["g_q_lat"], W["g_kv_lat"])
    q_raw = _mm(qln, W["w_uq"], name="mm_uq")
    kv_raw = _mm(kvln, W["w_ukv"], name="mm_ukv")
    qf, kf, vb = _qk_fwd(q_raw, kv_raw, z, cs, sn, W["gq"], W["gk"])
    o, lse = _attn_fwd(qf, kf, vb)
    conv = _conv_fwd(z, W["cw8"])
    mixed = _mix_fwd(o, conv, W["g_out_attn"], W["g_out_conv"])
    x1 = _mm(mixed, W["w_o"], res=x0, name="mm_o")
    h2 = _rms_fwd(x1, W["g_mlp"], "rms_mlp")
    a, f = _mm(h2, W["w_up"], epi="relu2", name="mm_up")
    x2 = _mm(f, W["w_down"], res=x1, name="mm_down")
    h3 = _rms_fwd(x2, W["g_ple"], "rms_ple")
    gl = _mm(h3, W["w_ple_gate"], name="mm_ple_gate")
    pe = _mm(p_l, W["w_ple"], name="mm_ple")
    x3 = _ple_fwd(x2, gl, pe)
    saved = dict(x0=x0, h=h, z=z, qln=qln, kvln=kvln, q_raw=q_raw, kv_raw=kv_raw, qf=qf, kf=kf, vb=vb, o=o, lse=lse,
                 conv=conv, mixed=mixed, x1=x1, h2=h2, a=a, f=f, x2=x2, h3=h3, gl=gl, pe=pe)
    return x3, saved


def _layer_bwd(dx3, p_l, W, cs, sn, sv):
    g = {}
    dpe, dgl = _ple_bwd(dx3, sv["gl"], sv["pe"])
    g["w_ple"] = _mm(p_l, dpe, mode="tn", out_dtype=BF16, shard_out=1, name="mm_dw_ple")
    g["w_ple_gate"] = _mm(sv["h3"], dgl, mode="tn", out_dtype=BF16, shard_out=0, name="mm_dw_ple_gate")
    dh3 = _mm(dgl, W["w_ple_gate"], mode="nt", name="mm_dh3")
    dx2, dx2b, g["g_ple"] = _rms_bwd(dh3, sv["x2"], W["g_ple"], dx3, "rms_ple_bwd")
    da = _mm(dx2b, W["w_down"], mode="nt", aux=sv["a"], epi="drelu2", out_dtype=BF16, name="mm_da")
    g["w_down"] = _mm(sv["f"], dx2b, mode="tn", out_dtype=BF16, shard_out=0, name="mm_dw_down")
    g["w_up"] = _mm(sv["h2"], da, mode="tn", out_dtype=BF16, shard_out=1, name="mm_dw_up")
    dh2 = _mm(da, W["w_up"], mode="nt", name="mm_dh2")
    dx1, dx1b, g["g_mlp"] = _rms_bwd(dh2, sv["x1"], W["g_mlp"], dx2, "rms_mlp_bwd")
    dmixed = _mm(dx1b, W["w_o"], mode="nt", name="mm_dmixed")
    g["w_o"] = _mm(sv["mixed"], dx1b, mode="tn", out_dtype=BF16, name="mm_dw_o")
    do, dconv, g["g_out_attn"], g["g_out_conv"] = _mix_bwd(dmixed, sv["o"], sv["conv"], W["g_out_attn"], W["g_out_conv"])
    dgb, dgc, dxin, g["cw8"] = _conv_bwd(dconv, sv["z"], W["cw8"])
    dqf, dkf, dv = _attn_bwd(sv["qf"], sv["kf"], sv["vb"], sv["o"], do, sv["lse"])
    dq_raw, dkv_raw, dkpe, g["gq"], g["gk"] = _qk_bwd(dqf, dkf, dv, sv["q_raw"], sv["kv_raw"], sv["z"], cs, sn, W["gq"], W["gk"])
    g["w_uq"] = _mm(sv["qln"], dq_raw, mode="tn", out_dtype=BF16, name="mm_dw_uq")
    dqln = _mm(dq_raw, W["w_uq"], mode="nt", name="mm_dqln")
    g["w_ukv"] = _mm(sv["kvln"], dkv_raw, mode="tn", out_dtype=BF16, name="mm_dw_ukv")
    dkvln = _mm(dkv_raw, W["w_ukv"], mode="nt", name="mm_dkvln")
    dlat, g["g_q_lat"], g["g_kv_lat"] = _lat_bwd(dqln, dkvln, sv["z"], W["g_q_lat"], W["g_kv_lat"])
    dz = jnp.concatenate([dlat, dkpe, dgb, dgc, dxin], axis=1)
    g["w_in"] = _mm(sv["h"], dz, mode="tn", out_dtype=BF16, name="mm_dw_in")
    dh = _mm(dz, W["w_in"], mode="nt", name="mm_dh")
    dx0, g["g_mix"] = _rms_bwd(dh, sv["x0"], W["g_mix"], dx1, "rms_mix_bwd", want_bf16=False)
    return dx0, g


def _local_step(x, p, positions, target, gathered, small):
    depth = p.shape[0]
    cs, sn = _rope_tables(positions)
    prep = _prepare_weights(gathered)
    Ws = [_layer_weights(prep, small, l) for l in range(depth)]
    saved = []
    for l in range(depth):
        x, sv = _layer_fwd(x, p[l], Ws[l], cs, sn)
        saved.append(sv)
    dx, sq = _loss_grad(x, target)
    grads = [None] * depth
    for l in reversed(range(depth)):
        dx, gp = _layer_bwd(dx, p[l], Ws[l], cs, sn, saved[l])
        grads[l] = _unpad_grads(gp)
    return sq, dx, grads


def _small_rows(extra):
    return sum(_pad128(n) for _, n in SMALL + extra) // LANES


def _pack_small(vals, depth, extra, tail_rows=0):
    parts = []
    for l in range(depth):
        for name, n in SMALL + extra:
            parts.append(jnp.pad(vals[name][l].reshape(-1), (0, _pad128(n) - n)))
    flat = jnp.concatenate(parts).reshape(-1, LANES)
    rows = flat.shape[0] + tail_rows
    return jnp.pad(flat, ((0, -(-rows // SUBLANES) * SUBLANES - flat.shape[0]), (0, 0)))


def _unpack_small(flat, depth, extra):
    per_layer = _small_rows(extra) * LANES
    body = flat.reshape(-1)[:depth * per_layer].reshape(depth, per_layer)
    out, off = {}, 0
    for name, n in SMALL + extra:
        out[name] = body[:, off:off + n]
        off += _pad128(n)
    return out


def kernel(x, p, positions, g_mix, w_in, g_q_lat, w_uq, g_kv_lat, w_ukv, g_qn_nope, g_qn_rope, g_kn_nope, g_kn_rope, conv_w, g_out_attn, g_out_conv, w_o, g_mlp, w_up, w_down, g_ple, w_ple_gate, w_ple, loss_target, m_g_mix, m_w_in, m_g_q_lat, m_w_uq, m_g_kv_lat, m_w_ukv, m_g_qn_nope, m_g_qn_rope, m_g_kn_nope, m_g_kn_rope, m_conv_w, m_g_out_attn, m_g_out_conv, m_w_o, m_g_mlp, m_w_up, m_w_down, m_g_ple, m_w_ple_gate, m_w_ple, v_g_mix, v_w_in, v_g_q_lat, v_w_uq, v_g_kv_lat, v_w_ukv, v_g_qn_nope, v_g_qn_rope, v_g_kn_nope, v_g_kn_rope, v_conv_w, v_g_out_attn, v_g_out_conv, v_w_o, v_g_mlp, v_w_up, v_w_down, v_g_ple, v_w_ple_gate, v_w_ple):
    w = dict(g_mix=g_mix, w_in=w_in, g_q_lat=g_q_lat, w_uq=w_uq, g_kv_lat=g_kv_lat, w_ukv=w_ukv, g_qn_nope=g_qn_nope,
             g_qn_rope=g_qn_rope, g_kn_nope=g_kn_nope, g_kn_rope=g_kn_rope, conv_w=conv_w, g_out_attn=g_out_attn,
             g_out_conv=g_out_conv, w_o=w_o, g_mlp=g_mlp, w_up=w_up, w_down=w_down, g_ple=g_ple, w_ple_gate=w_ple_gate,
             w_ple=w_ple)
    m = dict(g_mix=m_g_mix, w_in=m_w_in, g_q_lat=m_g_q_lat, w_uq=m_w_uq, g_kv_lat=m_g_kv_lat, w_ukv=m_w_ukv,
             g_qn_nope=m_g_qn_nope, g_qn_rope=m_g_qn_rope, g_kn_nope=m_g_kn_nope, g_kn_rope=m_g_kn_rope, conv_w=m_conv_w,
             g_out_attn=m_g_out_attn, g_out_conv=m_g_out_conv, w_o=m_w_o, g_mlp=m_g_mlp, w_up=m_w_up, w_down=m_w_down,
             g_ple=m_g_ple, w_ple_gate=m_w_ple_gate, w_ple=m_w_ple)
    v = dict(g_mix=v_g_mix, w_in=v_w_in, g_q_lat=v_g_q_lat, w_uq=v_w_uq, g_kv_lat=v_g_kv_lat, w_ukv=v_w_ukv,
             g_qn_nope=v_g_qn_nope, g_qn_rope=v_g_qn_rope, g_kn_nope=v_g_kn_nope, g_kn_rope=v_g_kn_rope, conv_w=v_conv_w,
             g_out_attn=v_g_out_attn, g_out_conv=v_g_out_conv, w_o=v_w_o, g_mlp=v_g_mlp, w_up=v_w_up, w_down=v_w_down,
             g_ple=v_g_ple, w_ple_gate=v_w_ple_gate, w_ple=v_w_ple)
    depth = p.shape[0]
    ax, ay, ac = _place()
    chip = 2 * ax + ay
    c_arr = jnp.reshape(ac, (1,)).astype(jnp.int32)
    chip_arr = jnp.reshape(chip, (1,)).astype(jnp.int32)
    conv_shard =("conv_w", CONV_TAPS * CONV_WIDTH // N_CHIPS)
    conv_full = ("conv_w", CONV_TAPS * CONV_WIDTH)

    names = [name for name, _, _, _ in BIG]
    gathered = dict(zip(names, _allgather_weights([_shard_slab(w[name], chip_arr, "shard_slab_" + name) for name in names])))
    conv_rows = -(-depth * CONV_TAPS // SUBLANES) * SUBLANES
    conv_all = _allgather_small(jnp.pad(conv_w.reshape(depth * CONV_TAPS, LANES), ((0, conv_rows - depth * CONV_TAPS), (0, 0))),
                                "allgather_conv_w")
    conv_cat = jnp.concatenate([conv_all[2 * j, :depth * CONV_TAPS] for j in range(N_CHIPS)], axis=1)
    small = {name: w[name] for name, _ in SMALL}
    small["conv_w"] = conv_cat.reshape(depth, CONV_TAPS, CONV_WIDTH)

    sq, grad_x, grads = _local_step(x[0], p[:, 0], positions[0], loss_target[0], gathered, small)

    gs = [jnp.stack([grads[l][name] for l in range(depth)]) for name in names]
    parts = [_add_half(g, r, c_arr, "grad_add_half_" + name) for name, g, r in zip(names, gs, _swap_halves(gs))]
    halves = [_add_chips(pt, r, chip_arr, c_arr, "grad_add_chips_" + name)
              for name, pt, r in zip(names, parts, _exchange_chips(parts))]
    g_out = dict(zip(names, _join_halves(halves)))

    stacked = {name: jnp.stack([grads[l][name] for l in range(depth)]) for name, _ in SMALL + (conv_full,)}
    loss_row = jnp.pad(sq[:1] * (0.5 / D_MODEL), ((0, 0), (0, 0)))
    packed = _pack_small(stacked, depth, (conv_full,), tail_rows=1)
    n_rows = depth * _small_rows((conv_full,))
    packed = packed.at[n_rows].set(loss_row[0])
    summed = _sum_devices(_allgather_small(packed, "allgather_small_grads"))
    loss = summed[n_rows, 0]
    g_small = _unpack_small(summed, depth, (conv_full,))
    for name, n in SMALL:
        g_out[name] = g_small[name]
    g_conv = g_small["conv_w"].reshape(depth, CONV_TAPS, CONV_WIDTH)
    g_out["conv_w"] = lax.dynamic_slice_in_dim(g_conv, chip * LANES, LANES, axis=2)

    delta, new_m, new_v = {}, {}, {}
    for name in names:
        delta[name], new_m[name], new_v[name] = _adamw(w[name], g_out[name], m[name], v[name], "adamw_" + name)
    pack = lambda vals: _pack_small({k: a.reshape(depth, -1) for k, a in vals.items()}, depth, (conv_shard,))[None]
    with_conv = lambda vals: {**{name: vals[name] for name, _ in SMALL}, "conv_w": vals["conv_w"]}
    d, m2, v2 = _adamw(pack(with_conv(w)), pack(with_conv(g_out)), pack(with_conv(m)), pack(with_conv(v)), "adamw_small")
    for res, packed_res in ((delta, d), (new_m, m2), (new_v, v2)):
        un = _unpack_small(packed_res[0], depth, (conv_shard,))
        for name, _ in SMALL + (conv_shard,):
            res[name] = un[name].reshape(w[name].shape)

    return (loss, grad_x[None], *[g_out[n] for n in WEIGHT_ORDER], *[delta[n] for n in WEIGHT_ORDER],
            *[new_m[n] for n in WEIGHT_ORDER], *[new_v[n] for n in WEIGHT_ORDER])
```

```python
import functools

import jax
import jax.numpy as jnp
from jax import lax
from jax.experimental import pallas as pl
from jax.experimental.pallas import tpu as pltpu

F32 = jnp.float32
BF16 = jnp.bfloat16
MESH = pl.DeviceIdType.MESH

D_MODEL = 1024
N_HEADS = 8
QK_NOPE = 64
QK_ROPE = 32
QK_HEAD = QK_NOPE + QK_ROPE
V_HEAD = 64
Q_LORA = 384
KV_LORA = 256
ATTN_WIDTH = N_HEADS * V_HEAD
CONV_WIDTH = 512
CONV_TAPS = 3
D_FF = 4096
PLE_DIM = 256
ROPE_THETA = 10000.0
EPS = 1e-6
ATT_SCALE = QK_HEAD ** -0.5
LOG2E = 1.4426950408889634
ATT_SCALE_LOG2 = ATT_SCALE * LOG2E

ADAM_LR = 0.001
ADAM_B1 = 0.9
ADAM_B2 = 0.999
ADAM_EPS = 1e-08
ADAM_WD = 0.01
ADAM_STEP = 10

LANES = 128
SUBLANES = 8
HEAD_PAD = LANES
HP = N_HEADS * HEAD_PAD
ROPE_LO = QK_NOPE
ROPE_MID = QK_NOPE + QK_ROPE // 2
ROPE_HI = QK_NOPE + QK_ROPE
VMEM_LIMIT = 56 * 1024 * 1024

Z_Q, Z_KV, Z_KPE, Z_GB, Z_GC, Z_XIN = 0, 384, 640, 768, 1280, 1792
Z_COLS = 2304
Z_LAT = Z_KPE

ROW_TILE = 512
ATT_TILE = 256
ATT_CHAINS = 4
MM_TM, MM_TN, MM_TK = 1024, 1024, 4096
MM_TM_MIN, MM_TK_MIN = 256, 512
MM_VMEM_BUDGET = 40 * 1024 * 1024

N_CHIPS = 4
N_DEV = 8

PRE = ("w_in", "w_uq", "w_ukv")
POST = ("w_o", "w_up", "w_down", "w_ple_gate", "w_ple")
SMALL = (
    ("g_mix", 1024), ("g_q_lat", 384), ("g_kv_lat", 256), ("g_qn_nope", 64), ("g_qn_rope", 32), ("g_kn_nope", 64),
    ("g_kn_rope", 32), ("g_out_attn", 512), ("g_out_conv", 512), ("g_mlp", 1024), ("g_ple", 1024),
)
WEIGHT_ORDER = ("g_mix", "w_in", "g_q_lat", "w_uq", "g_kv_lat", "w_ukv", "g_qn_nope", "g_qn_rope", "g_kn_nope",
                "g_kn_rope", "conv_w", "g_out_attn", "g_out_conv", "w_o", "g_mlp", "w_up", "w_down", "g_ple",
                "w_ple_gate", "w_ple")


def _pallas(body, **kw):
    return pl.pallas_call(body, **kw)


def _cparams(*sem):
    return pltpu.CompilerParams(dimension_semantics=sem, vmem_limit_bytes=VMEM_LIMIT)


def _tile(dim, pref, unit=LANES):
    if dim <= pref:
        return dim
    t = (pref // unit) * unit
    while t > unit and dim % t:
        t -= unit
    assert dim % t == 0, (dim, pref)
    return t


def _pad128(n):
    return -(-n // LANES) * LANES


_DIMS = {"nn": (((1,), (0,)), ((), ())), "nt": (((1,), (1,)), ((), ())), "tn": (((0,), (0,)), ((), ()))}


class Gathered:
    def __init__(self, arr, l, axis):
        self.arr, self.l, self.axis = arr, l, axis
        _, _, ks, ns = arr.shape
        self.shape = (ks * N_CHIPS, ns) if axis == 0 else (ks, ns * N_CHIPS)
        self.row_limit = ks if axis == 0 else None
        self.col_limit = ns if axis == 1 else None

    def spec(self, tr, tc, rc):
        l, axis = self.l, self.axis
        _, _, ks, ns = self.arr.shape
        per = (ks // tr) if axis == 0 else (ns // tc)

        def index(i, j, k):
            r, c = rc(i, j, k)
            return (r // per, l, r % per, c) if axis == 0 else (c // per, l, r, c % per)

        return pl.BlockSpec((None, None, tr, tc), index)


def _mm(a, b, *, mode="nn", res=None, aux=None, epi=None, out_dtype=F32, shard_out=None, name):
    if mode == "nn":
        (M, K), (K2, N) = a.shape, b.shape
    elif mode == "nt":
        (M, K), (N, K2) = a.shape, b.shape
    else:
        (K, M), (K2, N) = a.shape, b.shape
    assert K == K2, (a.shape, b.shape, mode)
    b_rows, b_cols = getattr(b, "row_limit", None), getattr(b, "col_limit", None)
    n_lim, k_lim = (b_rows, b_cols) if mode == "nt" else (b_cols, b_rows)
    m_lim = M // N_CHIPS if shard_out == 0 else None
    if shard_out == 1:
        n_lim = N // N_CHIPS
    extra = [t for t in (res, aux) if t is not None]
    out_bytes = 6 if epi == "relu2" else jnp.dtype(out_dtype).itemsize

    def footprint(tm, tn, tk):
        blocks = tm * tk * a.dtype.itemsize + tk * tn * 2 + tm * tn * (out_bytes + 4 * len(extra))
        return 2 * blocks + (tm * tn * 4 if tk < K else 0)

    tm, tn, tk = _tile(m_lim or M, MM_TM), _tile(n_lim or N, MM_TN), _tile(k_lim or K, MM_TK)
    while footprint(tm, tn, tk) > MM_VMEM_BUDGET and tk > MM_TK_MIN:
        tk = _tile(K, tk // 2)
    while footprint(tm, tn, tk) > MM_VMEM_BUDGET and tm > MM_TM_MIN:
        tm = _tile(M, tm // 2)
    nk = K // tk
    a_spec = pl.BlockSpec((tk, tm), lambda i, j, k: (k, i)) if mode == "tn" else pl.BlockSpec((tm, tk), lambda i, j, k: (i, k))
    if mode == "nt":
        b_block, b_rc = (tn, tk), (lambda i, j, k: (j, k))
    else:
        b_block, b_rc = (tk, tn), (lambda i, j, k: (k, j))
    if isinstance(b, Gathered):
        b_spec, b = b.spec(*b_block, b_rc), b.arr
    else:
        b_spec = pl.BlockSpec(b_block, b_rc)
    mn_spec = pl.BlockSpec((tm, tn), lambda i, j, k: (i, j))
    dims = _DIMS[mode]

    def body(*refs):
        a_ref, b_ref = refs[0], refs[1]
        extra_refs = refs[2:2 + len(extra)]
        out_refs = refs[2 + len(extra):2 + len(extra) + (2 if epi == "relu2" else 1)]
        prod = lax.dot_general(a_ref[...].astype(BF16), b_ref[...].astype(BF16), dims, preferred_element_type=F32)

        def finish(r):
            if res is not None:
                r = r + extra_refs[0][...]
            if epi == "relu2":
                out_refs[0][...] = r
                t = jnp.maximum(r, 0.0)
                out_refs[1][...] = (t * t).astype(BF16)
            elif epi == "drelu2":
                out_refs[0][...] = (r * (2.0 * jnp.maximum(extra_refs[-1][...], 0.0))).astype(out_dtype)
            else:
                out_refs[0][...] = r.astype(out_dtype)

        if nk == 1:
            finish(prod)
        else:
            acc = refs[-1]
            k = pl.program_id(2)

            @pl.when(k == 0)
            def _():
                acc[...] = prod

            @pl.when(k > 0)
            def _():
                acc[...] += prod

            @pl.when(k == nk - 1)
            def _():
                finish(acc[...])

    if epi == "relu2":
        out_shape = (jax.ShapeDtypeStruct((M, N), F32), jax.ShapeDtypeStruct((M, N), BF16))
        out_specs = (mn_spec, mn_spec)
    elif shard_out == 0:
        per = (M // N_CHIPS) // tm
        out_shape = jax.ShapeDtypeStruct((N_CHIPS, M // N_CHIPS, N), out_dtype)
        out_specs = pl.BlockSpec((None, tm, tn), lambda i, j, k: (i // per, i % per, j))
    elif shard_out == 1:
        per = (N // N_CHIPS) // tn
        out_shape = jax.ShapeDtypeStruct((N_CHIPS, M, N // N_CHIPS), out_dtype)
        out_specs = pl.BlockSpec((None, tm, tn), lambda i, j, k: (j // per, i, j % per))
    else:
        out_shape = jax.ShapeDtypeStruct((M, N), out_dtype)
        out_specs = mn_spec
    return _pallas(
        body, name=name, grid=(M // tm, N // tn, nk),
        in_specs=[a_spec, b_spec] + [mn_spec] * len(extra), out_specs=out_specs, out_shape=out_shape,
        scratch_shapes=[pltpu.VMEM((tm, tn), F32)] if nk > 1 else [],
        compiler_params=_cparams("parallel", "parallel", "arbitrary"),
    )(a, b, *extra)


def _rows(ts, d, col=0):
    return pl.BlockSpec((ts, d), lambda i: (i, col))


def _gain(d):
    return pl.BlockSpec((1, d), lambda i: (0, 0))


def _accum(d):
    return pl.BlockSpec((SUBLANES, d), lambda i: (0, 0))


def _accumulate(ref, val):
    i = pl.program_id(0)

    @pl.when(i == 0)
    def _():
        ref[...] = jnp.zeros_like(ref)

    ref[...] += jnp.broadcast_to(jnp.sum(val, axis=0, keepdims=True), ref.shape)


def _rinv(x, n):
    return lax.rsqrt(jnp.sum(x * x, axis=-1, keepdims=True) / n + EPS)


def _norm_bwd(x, dy, g, n):
    r = _rinv(x, n)
    xhat = x * r
    dyg = dy * g
    dx = r * (dyg - xhat * (jnp.sum(dyg * xhat, axis=-1, keepdims=True) / n))
    return dx, dy * xhat


def _rms_fwd(x, g, name):
    S, D = x.shape
    ts = _tile(S, ROW_TILE, SUBLANES)

    def body(x_ref, g_ref, h_ref):
        xv = x_ref[...]
        h_ref[...] = (xv * _rinv(xv, D) * g_ref[...]).astype(BF16)

    return _pallas(body, name=name, grid=(S // ts,), in_specs=[_rows(ts, D), _gain(D)], out_specs=_rows(ts, D),
                   out_shape=jax.ShapeDtypeStruct((S, D), BF16), compiler_params=_cparams("parallel"))(x, g)


def _rms_bwd(dy, x, g, dres, name, want_bf16=True):
    S, D = x.shape
    ts = _tile(S, ROW_TILE, SUBLANES)

    def body(dy_ref, x_ref, g_ref, dres_ref, *outs):
        dx, dgc = _norm_bwd(x_ref[...], dy_ref[...], g_ref[...], D)
        dx = dx + dres_ref[...]
        outs[0][...] = dx
        if want_bf16:
            outs[1][...] = dx.astype(BF16)
        _accumulate(outs[-1], dgc)

    out_shape = [jax.ShapeDtypeStruct((S, D), F32)] + ([jax.ShapeDtypeStruct((S, D), BF16)] if want_bf16 else []) + [
        jax.ShapeDtypeStruct((SUBLANES, D), F32)]
    out_specs = [_rows(ts, D)] * (2 if want_bf16 else 1) + [_accum(D)]
    return _pallas(body, name=name, grid=(S // ts,), in_specs=[_rows(ts, D), _rows(ts, D), _gain(D), _rows(ts, D)],
                   out_specs=out_specs, out_shape=out_shape, compiler_params=_cparams("arbitrary"))(dy, x, g, dres)


def _lat_fwd(z, gq, gkv):
    S = z.shape[0]
    ts = _tile(S, ROW_TILE, SUBLANES)

    def body(z_ref, gq_ref, gkv_ref, q_ref, kv_ref):
        zq = z_ref[:, Z_Q:Z_KV]
        zkv = z_ref[:, Z_KV:Z_KPE]
        q_ref[...] = (zq * _rinv(zq, Q_LORA) * gq_ref[...]).astype(BF16)
        kv_ref[...] = (zkv * _rinv(zkv, KV_LORA) * gkv_ref[...]).astype(BF16)

    return _pallas(body, name="lat_fwd", grid=(S // ts,), in_specs=[_rows(ts, Z_LAT), _gain(Q_LORA), _gain(KV_LORA)],
                   out_specs=[_rows(ts, Q_LORA), _rows(ts, KV_LORA)],
                   out_shape=[jax.ShapeDtypeStruct((S, Q_LORA), BF16), jax.ShapeDtypeStruct((S, KV_LORA), BF16)],
                   compiler_params=_cparams("parallel"))(z, gq, gkv)


def _lat_bwd(dq, dkv, z, gq, gkv):
    S = z.shape[0]
    ts = _tile(S, ROW_TILE, SUBLANES)

    def body(dq_ref, dkv_ref, z_ref, gq_ref, gkv_ref, dlat_ref, dgq_ref, dgkv_ref):
        dxq, cq = _norm_bwd(z_ref[:, Z_Q:Z_KV], dq_ref[...], gq_ref[...], Q_LORA)
        dxkv, ckv = _norm_bwd(z_ref[:, Z_KV:Z_KPE], dkv_ref[...], gkv_ref[...], KV_LORA)
        dlat_ref[:, Z_Q:Z_KV] = dxq.astype(BF16)
        dlat_ref[:, Z_KV:Z_KPE] = dxkv.astype(BF16)
        _accumulate(dgq_ref, cq)
        _accumulate(dgkv_ref, ckv)

    return _pallas(body, name="lat_bwd", grid=(S // ts,),
                   in_specs=[_rows(ts, Q_LORA), _rows(ts, KV_LORA), _rows(ts, Z_LAT), _gain(Q_LORA), _gain(KV_LORA)],
                   out_specs=[_rows(ts, Z_LAT), _accum(Q_LORA), _accum(KV_LORA)],
                   out_shape=[jax.ShapeDtypeStruct((S, Z_LAT), BF16), jax.ShapeDtypeStruct((SUBLANES, Q_LORA), F32),
                              jax.ShapeDtypeStruct((SUBLANES, KV_LORA), F32)],
                   compiler_params=_cparams("arbitrary"))(dq, dkv, z, gq, gkv)


def _head_masks():
    lane = lax.broadcasted_iota(jnp.int32, (1, HEAD_PAD), 1)
    return lane < ROPE_LO, (lane >= ROPE_LO) & (lane < ROPE_HI), lane < ROPE_MID


def _seg_sum(t, m_n, m_r):
    sn = jnp.sum(jnp.where(m_n, t, 0.0), axis=-1, keepdims=True) / QK_NOPE
    sr = jnp.sum(jnp.where(m_r, t, 0.0), axis=-1, keepdims=True) / QK_ROPE
    return jnp.where(m_n, sn, sr)


def _rot_half(y, lo):
    half = QK_ROPE // 2
    return jnp.where(lo, -pltpu.roll(y, HEAD_PAD - half, 1), pltpu.roll(y, half, 1))


def _qk_fwd(q_raw, kv_raw, z, cs, sn, gq, gk):
    S = q_raw.shape[0]
    ts = _tile(S, ROW_TILE, SUBLANES)

    def body(q_ref, k_ref, v_ref, kpe_ref, cs_ref, sn_ref, gq_ref, gk_ref, qf_ref, kf_ref, vb_ref):
        m_n, m_r, lo = _head_masks()
        cos, sin, gqv, gkv = cs_ref[...], sn_ref[...], gq_ref[...], gk_ref[...]

        def norm(x, g):
            return x * lax.rsqrt(_seg_sum(x * x, m_n, m_r) + EPS) * g

        def rope(y):
            return y * cos + _rot_half(y, lo) * sin

        kr = rope(norm(kpe_ref[...], gkv))
        lane = lax.broadcasted_iota(jnp.int32, (1, HEAD_PAD), 1)
        for h in range(N_HEADS):
            sl = slice(h * HEAD_PAD, (h + 1) * HEAD_PAD)
            qf_ref[:, sl] = rope(norm(q_ref[:, sl], gqv)).astype(BF16)
            kf_ref[:, sl] = (norm(k_ref[:, sl], gkv) + kr).astype(BF16)
            vb_ref[:, sl] = jnp.where(lane == V_HEAD, 1.0, v_ref[:, sl]).astype(BF16)

    hd = jax.ShapeDtypeStruct((S, HP), BF16)
    return _pallas(body, name="qk_fwd", grid=(S // ts,),
                   in_specs=[_rows(ts, HP), _rows(ts, HP, 0), _rows(ts, HP, 1), _rows(ts, HEAD_PAD, Z_KPE // HEAD_PAD),
                             _rows(ts, HEAD_PAD), _rows(ts, HEAD_PAD), _gain(HEAD_PAD), _gain(HEAD_PAD)],
                   out_specs=[_rows(ts, HP)] * 3, out_shape=[hd, hd, hd],
                   compiler_params=_cparams("parallel"))(q_raw, kv_raw, kv_raw, z, cs, sn, gq, gk)


def _qk_bwd(dqf, dkf, dv, q_raw, kv_raw, z, cs, sn, gq, gk):
    S = q_raw.shape[0]
    ts = _tile(S, ROW_TILE, SUBLANES)

    def body(dqf_ref, dkf_ref, dv_ref, q_ref, k_ref, kpe_ref, cs_ref, sn_ref, gq_ref, gk_ref,
             dq_ref, dkv_ref, dkpe_ref, dgq_ref, dgk_ref):
        m_n, m_r, lo = _head_masks()
        cos, sin, gqv, gkv = cs_ref[...], sn_ref[...], gq_ref[...], gk_ref[...]

        def rope_t(w):
            return w * cos - jnp.where(m_r, _rot_half(w * sin, lo), 0.0)

        def norm_bwd(x, dy, g):
            r = lax.rsqrt(_seg_sum(x * x, m_n, m_r) + EPS)
            xhat = x * r
            dyg = dy * g
            return r * (dyg - xhat * _seg_sum(dyg * xhat, m_n, m_r)), dy * xhat

        accq = jnp.zeros((ts, HEAD_PAD), F32)
        acck = jnp.zeros((ts, HEAD_PAD), F32)
        dkr = jnp.zeros((ts, HEAD_PAD), F32)
        for h in range(N_HEADS):
            sl = slice(h * HEAD_PAD, (h + 1) * HEAD_PAD)
            dx, c = norm_bwd(q_ref[:, sl], rope_t(dqf_ref[:, sl]), gqv)
            dq_ref[:, sl] = dx.astype(BF16)
            accq = accq + c
            dk = dkf_ref[:, sl]
            dkr = dkr + jnp.where(m_r, dk, 0.0)
            dx, c = norm_bwd(k_ref[:, sl], jnp.where(m_n, dk, 0.0), gkv)
            dkv_ref[:, sl] = dx.astype(BF16)
            acck = acck + c
            dkv_ref[:, HP + h * HEAD_PAD:HP + (h + 1) * HEAD_PAD] = dv_ref[:, sl].astype(BF16)
        dx, c = norm_bwd(kpe_ref[...], rope_t(dkr), gkv)
        dkpe_ref[...] = dx.astype(BF16)
        _accumulate(dgq_ref, accq)
        _accumulate(dgk_ref, acck + c)

    return _pallas(body, name="qk_bwd", grid=(S // ts,),
                   in_specs=[_rows(ts, HP)] * 4 + [_rows(ts, HP, 0), _rows(ts, HEAD_PAD, Z_KPE // HEAD_PAD),
                                                   _rows(ts, HEAD_PAD), _rows(ts, HEAD_PAD), _gain(HEAD_PAD), _gain(HEAD_PAD)],
                   out_specs=[_rows(ts, HP), _rows(ts, 2 * HP), _rows(ts, HEAD_PAD), _accum(HEAD_PAD), _accum(HEAD_PAD)],
                   out_shape=[jax.ShapeDtypeStruct((S, HP), BF16), jax.ShapeDtypeStruct((S, 2 * HP), BF16),
                              jax.ShapeDtypeStruct((S, HEAD_PAD), BF16), jax.ShapeDtypeStruct((SUBLANES, HEAD_PAD), F32),
                              jax.ShapeDtypeStruct((SUBLANES, HEAD_PAD), F32)],
                   compiler_params=_cparams("arbitrary"))(dqf, dkf, dv, q_raw, kv_raw, z, cs, sn, gq, gk)


def _causal(t):
    row = lax.broadcasted_iota(jnp.int32, (t, t), 0)
    col = lax.broadcasted_iota(jnp.int32, (t, t), 1)
    return col <= row


def _attn_tiles(S):
    t = _tile(S, ATT_TILE, SUBLANES)
    return t, min(ATT_CHAINS, S // t)


def _attn_fwd(qf, kf, vb, rider):
    S = qf.shape[0]
    t, nc = _attn_tiles(S)
    tq = nc * t
    nqt = S // tq

    def body(q_ref, k_ref, v_ref, *rest):
        (o_ref, lse_ref, m_ref, acc_ref), ride = rider.split(rest, 2, 2)
        qt = pl.program_id(1)
        ride(pl.program_id(0) * nqt + qt, N_HEADS * nqt)
        m_ref[...] = jnp.full_like(m_ref, -jnp.inf)
        acc_ref[...] = jnp.zeros_like(acc_ref)

        def chain(a, kb, vb_, masked):
            sub = slice(a * t, (a + 1) * t)
            s = lax.dot_general(q_ref[sub, :], kb, _DIMS["nt"], preferred_element_type=F32) * ATT_SCALE_LOG2
            if masked:
                s = jnp.where(_causal(t), s, -jnp.inf)
            m_old = m_ref[sub, :]
            m_new = jnp.maximum(m_old, jnp.max(s, axis=-1, keepdims=True))
            p = jnp.exp2(s - jnp.concatenate([m_new] * (t // HEAD_PAD), axis=1))
            acc_ref[sub, :] = jnp.exp2(m_old - m_new) * acc_ref[sub, :] + jnp.dot(p.astype(BF16), vb_, preferred_element_type=F32)
            m_ref[sub, :] = m_new

        def trip(j, carry):
            rows = pl.ds(pl.multiple_of(j * t, t), t)
            kb, vb_ = k_ref[rows, :], v_ref[rows, :]
            for a in range(nc):
                chain(a, kb, vb_, False)
            return carry

        lax.fori_loop(0, nc * qt, trip, 0)
        for d in range(nc):
            rows = pl.ds(pl.multiple_of((nc * qt + d) * t, t), t)
            kb, vb_ = k_ref[rows, :], v_ref[rows, :]
            for a in range(d, nc):
                chain(a, kb, vb_, a == d)
        acc = acc_ref[...]
        l = acc[:, V_HEAD:V_HEAD + 1]
        lane = lax.broadcasted_iota(jnp.int32, (1, HEAD_PAD), 1)
        o_ref[...] = jnp.where(lane < V_HEAD, acc / l, 0.0)
        lse_ref[...] = m_ref[...] + jnp.log(l) * LOG2E

    tile = pl.BlockSpec((tq, HEAD_PAD), lambda h, i: (i, h))
    full = pl.BlockSpec((S, HEAD_PAD), lambda h, i: (0, h))
    outs = _pallas(body, name="attn_fwd", grid=(N_HEADS, nqt), in_specs=[tile, full, full] + rider.in_specs(),
                   out_specs=[tile, pl.BlockSpec((None, tq, HEAD_PAD), lambda h, i: (h, i, 0))] + rider.out_specs(),
                   out_shape=[jax.ShapeDtypeStruct((S, HP), F32), jax.ShapeDtypeStruct((N_HEADS, S, HEAD_PAD), F32)]
                   + rider.out_shape(),
                   scratch_shapes=[pltpu.VMEM((tq, HEAD_PAD), F32), pltpu.VMEM((tq, HEAD_PAD), F32)] + rider.scratch(),
                   input_output_aliases=rider.aliases(3, 2),
                   compiler_params=_cparams("arbitrary", "arbitrary"))(qf, kf, vb, *rider.args())
    return outs[0], outs[1], outs[2:]


def _attn_bwd(qf, kf, vb, o, do, lse, rider):
    S = qf.shape[0]
    t, nc = _attn_tiles(S)
    tkv = nc * t
    nq = S // t
    nkt = S // tkv

    def body(q_ref, k_ref, v_ref, o_ref, do_ref, lse_ref, *rest):
        (dq_ref, dk_ref, dv_ref), ride = rider.split(rest, 3, 0)
        kt = pl.program_id(1)
        ride(pl.program_id(0) * nkt + kt, N_HEADS * nkt)

        @pl.when(kt == 0)
        def _():
            dq_ref[...] = jnp.zeros_like(dq_ref)

        dk_ref[...] = jnp.zeros_like(dk_ref)
        dv_ref[...] = jnp.zeros_like(dv_ref)

        def q_block(rows):
            dof = do_ref[rows, :]
            delta = jnp.sum(dof * o_ref[rows, :], axis=-1, keepdims=True)
            return q_ref[rows, :], dof.astype(BF16), lse_ref[rows, :][:, :1], delta

        def chain(b, qv, masked):
            q, dob, lse, delta = qv
            sub = slice(b * t, (b + 1) * t)
            kb, vb_ = k_ref[sub, :], v_ref[sub, :]
            s = lax.dot_general(q, kb, _DIMS["nt"], preferred_element_type=F32) * ATT_SCALE_LOG2
            p = jnp.exp2(s - lse)
            if masked:
                p = jnp.where(_causal(t), p, 0.0)
            dv_ref[sub, :] += lax.dot_general(p.astype(BF16), dob, _DIMS["tn"], preferred_element_type=F32)
            dp = lax.dot_general(dob, vb_, _DIMS["nt"], preferred_element_type=F32)
            ds = (p * (dp - delta) * ATT_SCALE).astype(BF16)
            dk_ref[sub, :] += lax.dot_general(ds, q, _DIMS["tn"], preferred_element_type=F32)
            return jnp.dot(ds, kb, preferred_element_type=F32)

        for a in range(nc):
            rows = pl.ds(pl.multiple_of((nc * kt + a) * t, t), t)
            qv = q_block(rows)
            dq_ref[rows, :] += sum(chain(b, qv, b == a) for b in range(a + 1))

        def trip(i, carry):
            rows = pl.ds(pl.multiple_of(i * t, t), t)
            qv = q_block(rows)
            dq_ref[rows, :] += sum(chain(b, qv, False) for b in range(nc))
            return carry

        lax.fori_loop(nc * (kt + 1), nq, trip, 0)

    tile = pl.BlockSpec((tkv, HEAD_PAD), lambda h, j: (j, h))
    full = pl.BlockSpec((S, HEAD_PAD), lambda h, j: (0, h))
    hd = jax.ShapeDtypeStruct((S, HP), F32)
    outs = _pallas(body, name="attn_bwd", grid=(N_HEADS, nkt),
                   in_specs=[full, tile, tile, full, full, pl.BlockSpec((None, S, HEAD_PAD), lambda h, j: (h, 0, 0))]
                   + rider.in_specs(),
                   out_specs=[full, tile, tile] + rider.out_specs(), out_shape=[hd, hd, hd] + rider.out_shape(),
                   scratch_shapes=rider.scratch(), input_output_aliases=rider.aliases(6, 3),
                   compiler_params=_cparams("arbitrary", "arbitrary"))(qf, kf, vb, o, do, lse, *rider.args())
    return outs[0], outs[1], outs[2], outs[3:]


def _shift_down(u, j, row):
    return jnp.where(row >= j, pltpu.roll(u, j, 0), 0.0)


def _shift_up(u, j, row, s):
    return jnp.where(row < s - j, pltpu.roll(u, s - j, 0), 0.0)


def _conv_cols(s, first_tile):
    return pl.BlockSpec((s, LANES), lambda cb: (0, first_tile + cb))


def _conv_fwd(z, cw8):
    S = z.shape[0]

    def body(gb_ref, gc_ref, xin_ref, w_ref, out_ref):
        row = lax.broadcasted_iota(jnp.int32, (S, LANES), 0)
        u = gc_ref[...] * xin_ref[...]
        y = w_ref[0:1, :] * u
        for j in range(1, CONV_TAPS):
            y = y + w_ref[j:j + 1, :] * _shift_down(u, j, row)
        out_ref[...] = gb_ref[...] * y

    return _pallas(body, name="conv_fwd", grid=(CONV_WIDTH // LANES,),
                   in_specs=[_conv_cols(S, Z_GB // LANES), _conv_cols(S, Z_GC // LANES), _conv_cols(S, Z_XIN // LANES),
                             pl.BlockSpec((SUBLANES, LANES), lambda cb: (0, cb))],
                   out_specs=_conv_cols(S, 0), out_shape=jax.ShapeDtypeStruct((S, CONV_WIDTH), F32),
                   compiler_params=_cparams("parallel"))(z, z, z, cw8)


def _conv_bwd(dconv, z, cw8):
    S = z.shape[0]

    def body(d_ref, gb_ref, gc_ref, xin_ref, w_ref, dgb_ref, dgc_ref, dxin_ref, dw_ref):
        row = lax.broadcasted_iota(jnp.int32, (S, LANES), 0)
        gc, xin, d = gc_ref[...], xin_ref[...], d_ref[...]
        u = gc * xin
        dy = d * gb_ref[...]
        y = w_ref[0:1, :] * u
        du = w_ref[0:1, :] * dy
        dw = [jnp.sum(dy * u, axis=0, keepdims=True)]
        for j in range(1, CONV_TAPS):
            uj = _shift_down(u, j, row)
            y = y + w_ref[j:j + 1, :] * uj
            du = du + w_ref[j:j + 1, :] * _shift_up(dy, j, row, S)
            dw.append(jnp.sum(dy * uj, axis=0, keepdims=True))
        dgb_ref[...] = (d * y).astype(BF16)
        dgc_ref[...] = (du * xin).astype(BF16)
        dxin_ref[...] = (du * gc).astype(BF16)
        tap = lax.broadcasted_iota(jnp.int32, (SUBLANES, LANES), 0)
        dw_ref[...] = sum(jnp.where(tap == j, dw[j], 0.0) for j in range(CONV_TAPS))

    col = _conv_cols(S, 0)
    sd = jax.ShapeDtypeStruct((S, CONV_WIDTH), BF16)
    return _pallas(body, name="conv_bwd", grid=(CONV_WIDTH // LANES,),
                   in_specs=[col, _conv_cols(S, Z_GB // LANES), _conv_cols(S, Z_GC // LANES), _conv_cols(S, Z_XIN // LANES),
                             pl.BlockSpec((SUBLANES, LANES), lambda cb: (0, cb))],
                   out_specs=[col, col, col, pl.BlockSpec((SUBLANES, LANES), lambda cb: (0, cb))],
                   out_shape=[sd, sd, sd, jax.ShapeDtypeStruct((SUBLANES, CONV_WIDTH), F32)],
                   compiler_params=_cparams("parallel"))(dconv, z, z, z, cw8)


def _mix_fwd(o, conv, ga, gc):
    S = o.shape[0]
    ts = _tile(S, ROW_TILE, SUBLANES)

    def body(o_ref, c_ref, ga_ref, gc_ref, out_ref):
        ov, cv = o_ref[...], c_ref[...]
        out_ref[:, :HP] = (ov * _rinv(ov, ATTN_WIDTH) * ga_ref[...]).astype(BF16)
        out_ref[:, HP:] = (cv * _rinv(cv, CONV_WIDTH) * gc_ref[...]).astype(BF16)

    return _pallas(body, name="mix_fwd", grid=(S // ts,),
                   in_specs=[_rows(ts, HP), _rows(ts, CONV_WIDTH), _gain(HP), _gain(CONV_WIDTH)],
                   out_specs=_rows(ts, HP + CONV_WIDTH), out_shape=jax.ShapeDtypeStruct((S, HP + CONV_WIDTH), BF16),
                   compiler_params=_cparams("parallel"))(o, conv, ga, gc)


def _mix_bwd(dmixed, o, conv, ga, gc):
    S = o.shape[0]
    ts = _tile(S, ROW_TILE, SUBLANES)

    def body(d_ref, o_ref, c_ref, ga_ref, gc_ref, do_ref, dc_ref, dga_ref, dgc_ref):
        dx, ca = _norm_bwd(o_ref[...], d_ref[:, :HP], ga_ref[...], ATTN_WIDTH)
        do_ref[...] = dx
        dx, cc = _norm_bwd(c_ref[...], d_ref[:, HP:], gc_ref[...], CONV_WIDTH)
        dc_ref[...] = dx
        _accumulate(dga_ref, ca)
        _accumulate(dgc_ref, cc)

    return _pallas(body, name="mix_bwd", grid=(S // ts,),
                   in_specs=[_rows(ts, HP + CONV_WIDTH), _rows(ts, HP), _rows(ts, CONV_WIDTH), _gain(HP), _gain(CONV_WIDTH)],
                   out_specs=[_rows(ts, HP), _rows(ts, CONV_WIDTH), _accum(HP), _accum(CONV_WIDTH)],
                   out_shape=[jax.ShapeDtypeStruct((S, HP), F32), jax.ShapeDtypeStruct((S, CONV_WIDTH), F32),
                              jax.ShapeDtypeStruct((SUBLANES, HP), F32), jax.ShapeDtypeStruct((SUBLANES, CONV_WIDTH), F32)],
                   compiler_params=_cparams("arbitrary"))(dmixed, o, conv, ga, gc)


def _ple_fwd(x, gl, pe):
    S, D = x.shape
    ts = _tile(S, ROW_TILE, SUBLANES)

    def body(x_ref, gl_ref, pe_ref, out_ref):
        out_ref[...] = x_ref[...] + jax.nn.sigmoid(gl_ref[...]) * pe_ref[...]

    return _pallas(body, name="ple_fwd", grid=(S // ts,), in_specs=[_rows(ts, D)] * 3, out_specs=_rows(ts, D),
                   out_shape=jax.ShapeDtypeStruct((S, D), F32), compiler_params=_cparams("parallel"))(x, gl, pe)


def _ple_bwd(dx, gl, pe):
    S, D = dx.shape
    ts = _tile(S, ROW_TILE, SUBLANES)

    def body(dx_ref, gl_ref, pe_ref, dpe_ref, dgl_ref):
        d = dx_ref[...]
        gate = jax.nn.sigmoid(gl_ref[...])
        dpe_ref[...] = (d * gate).astype(BF16)
        dgl_ref[...] = (d * pe_ref[...] * (gate * (1.0 - gate))).astype(BF16)

    sd = jax.ShapeDtypeStruct((S, D), BF16)
    return _pallas(body, name="ple_bwd", grid=(S // ts,), in_specs=[_rows(ts, D)] * 3, out_specs=[_rows(ts, D)] * 2,
                   out_shape=[sd, sd], compiler_params=_cparams("parallel"))(dx, gl, pe)


def _loss_grad(y, target):
    S, D = y.shape
    ts = _tile(S, ROW_TILE, SUBLANES)

    def body(y_ref, t_ref, dy_ref, sq_ref):
        e = y_ref[...] - t_ref[...]
        dy_ref[...] = e / D

        @pl.when(pl.program_id(0) == 0)
        def _():
            sq_ref[...] = jnp.zeros_like(sq_ref)

        sq_ref[...] += jnp.broadcast_to(jnp.sum(jnp.sum(e * e, axis=1, keepdims=True), axis=0, keepdims=True), sq_ref.shape)

    return _pallas(body, name="loss_grad", grid=(S // ts,), in_specs=[_rows(ts, D)] * 2,
                   out_specs=[_rows(ts, D), _accum(LANES)],
                   out_shape=[jax.ShapeDtypeStruct((S, D), F32), jax.ShapeDtypeStruct((SUBLANES, LANES), F32)],
                   compiler_params=_cparams("arbitrary"))(y, target)


def _adamw(w, g, m, v, name):
    L, R, C = w.shape
    tr = _tile(R, ROW_TILE, SUBLANES)

    def body(w_ref, g_ref, m_ref, v_ref, d_ref, m2_ref, v2_ref):
        gv = g_ref[...]
        m2 = ADAM_B1 * m_ref[...] + (1.0 - ADAM_B1) * gv
        v2 = ADAM_B2 * v_ref[...] + (1.0 - ADAM_B2) * (gv * gv)
        m_hat = m2 / (1.0 - ADAM_B1 ** ADAM_STEP)
        v_hat = v2 / (1.0 - ADAM_B2 ** ADAM_STEP)
        d_ref[...] = -ADAM_LR * (m_hat / (jnp.sqrt(v_hat) + ADAM_EPS) + ADAM_WD * w_ref[...])
        m2_ref[...] = m2
        v2_ref[...] = v2

    sd = jax.ShapeDtypeStruct((L, R, C), F32)
    spec = pl.BlockSpec((None, tr, C), lambda l, i: (l, i, 0))
    return _pallas(body, name=name, grid=(L, R // tr), in_specs=[spec] * 4, out_specs=[spec] * 3, out_shape=[sd, sd, sd],
                   compiler_params=_cparams("parallel", "parallel"))(w, g, m, v)


def _place():
    return lax.axis_index("x"), lax.axis_index("y"), lax.axis_index("c")


def _other_chips(x, y):
    return [(1 - x, y), (x, 1 - y), (1 - x, 1 - y)]


HBM = pl.BlockSpec(memory_space=pl.ANY)
VMEM_SPEC = pl.BlockSpec(memory_space=pltpu.VMEM)


def _remote_copy(send_sems, recv_sems, k, src, dst, to):
    return pltpu.make_async_remote_copy(src_ref=src, dst_ref=dst, send_sem=send_sems.at[k], recv_sem=recv_sems.at[k],
                                        device_id=to, device_id_type=MESH)


def _comm_call(body, name, arrays, out_shapes, n_remote, in_place=False):
    scratch = [pltpu.SemaphoreType.DMA((n_remote,)), pltpu.SemaphoreType.DMA((n_remote,))]
    aliases = {i: i for i in range(len(arrays))} if in_place else {}
    return _pallas(body, name=name, in_specs=[HBM] * len(arrays), out_specs=[HBM] * len(out_shapes), out_shape=out_shapes,
                   scratch_shapes=scratch, input_output_aliases=aliases,
                   compiler_params=pltpu.CompilerParams(has_side_effects=True))(*arrays)


def _shard_slab(w, chip_arr, name):
    L, R, C = w.shape
    tr = _tile(R, ROW_TILE, 16)

    def body(j_ref, w_ref, out_ref):
        out_ref[...] = w_ref[...].astype(BF16)

    grid_spec = pltpu.PrefetchScalarGridSpec(
        num_scalar_prefetch=1, grid=(L, R // tr),
        in_specs=[pl.BlockSpec((None, tr, C), lambda l, i, j_ref: (l, i, 0))],
        out_specs=pl.BlockSpec((None, None, tr, C), lambda l, i, j_ref: (j_ref[0], l, i, 0)))
    return _pallas(body, name=name, grid_spec=grid_spec, out_shape=jax.ShapeDtypeStruct((N_CHIPS, L, R, C), BF16),
                   compiler_params=_cparams("parallel", "parallel"))(chip_arr, w)


class Rider:
    def __init__(self, reads, inplace, n_sems, stages):
        self.reads, self.inplace, self.n_sems, self.stages = list(reads), list(inplace), n_sems, stages

    def args(self):
        return self.reads + self.inplace

    def in_specs(self):
        return [HBM] * len(self.args())

    def out_specs(self):
        return [HBM] * len(self.inplace)

    def out_shape(self):
        return [jax.ShapeDtypeStruct(a.shape, a.dtype) for a in self.inplace]

    def scratch(self):
        return [pltpu.SemaphoreType.DMA((self.n_sems,)), pltpu.SemaphoreType.DMA((self.n_sems,))]

    def aliases(self, n_host_in, n_host_out):
        return {n_host_in + len(self.reads) + j: n_host_out + j for j in range(len(self.inplace))}

    def split(self, rest, n_host_out, n_host_scratch):
        n_r, n_io = len(self.reads), len(self.inplace)
        reads = rest[:n_r]
        host_out = rest[n_r + n_io:n_r + n_io + n_host_out]
        outs = rest[n_r + n_io + n_host_out:n_r + 2 * n_io + n_host_out]
        host_scratch = rest[n_r + 2 * n_io + n_host_out:n_r + 2 * n_io + n_host_out + n_host_scratch]
        sems = rest[n_r + 2 * n_io + n_host_out + n_host_scratch:]

        def ride(step, n_steps):
            at = [0, n_steps - 1] if len(self.stages) == 2 else [0, (3 * n_steps) // 4, n_steps - 1]
            for s, stage in zip(at, self.stages):
                pl.when(step == s)(functools.partial(stage, reads, outs, *sems))

        return tuple(host_out) + tuple(host_scratch), ride

    def run(self, name):
        n_r, n_io = len(self.reads), len(self.inplace)

        def body(*refs):
            for stage in self.stages:
                stage(refs[:n_r], refs[n_r + n_io:n_r + 2 * n_io], *refs[n_r + 2 * n_io:])

        return _pallas(body, name=name, in_specs=self.in_specs(), out_specs=self.out_specs(), out_shape=self.out_shape(),
                       scratch_shapes=self.scratch(), input_output_aliases=self.aliases(0, 0),
                       compiler_params=pltpu.CompilerParams(has_side_effects=True))(*self.args())


def _allgather_rider(slabs, items):
    def stage(which, reads, outs, send_sems, recv_sems):
        x, y, c = _place()
        me, sibling = (x, y, c), (x, y, 1 - c)
        copy = functools.partial(_remote_copy, send_sems, recv_sems)
        for n, (j, l) in enumerate(items):
            ref = outs[j]
            h = ref.shape[2] // 2
            mine, theirs = pl.ds(c * h, h), pl.ds((1 - c) * h, h)
            own = ref.at[2 * x + y, l, mine]
            for k, (cx, cy) in enumerate(_other_chips(x, y)):
                arrived = ref.at[2 * cx + cy, l, mine]
                if which == 0:
                    copy(6 * n + k, own, own, (cx, cy, c)).start()
                elif which == 1:
                    copy(6 * n + k, arrived, arrived, me).wait_recv()
                    copy(6 * n + 3 + k, arrived, arrived, sibling).start()
                else:
                    passed = ref.at[2 * cx + cy, l, theirs]
                    copy(6 * n + 3 + k, passed, passed, me).wait_recv()
                    copy(6 * n + k, own, own, me).wait_send()
                    copy(6 * n + 3 + k, arrived, arrived, me).wait_send()

    return Rider([], slabs, 6 * len(items), [functools.partial(stage, w) for w in range(3)])


def _exchange_rider(grads, landing, items):
    def stage(start, reads, outs, send_sems, recv_sems):
        x, y, c = _place()
        for n, (i, j, l) in enumerate(items):
            g_ref, r_ref = reads[i], outs[j]
            h = g_ref.shape[1] // 2
            for k in range(1, N_DEV):
                px, py, pc = (1 - x if k & 4 else x, 1 - y if k & 2 else y, 1 - c if k & 1 else c)
                cp = _remote_copy(send_sems, recv_sems, 7 * n + k - 1, g_ref.at[2 * px + py, pl.ds(pc * h, h)],
                                  r_ref.at[l, k - 1], (px, py, pc))
                if start:
                    cp.start()
                else:
                    cp.wait()

    return Rider(grads, landing, 7 * len(items), [functools.partial(stage, True), functools.partial(stage, False)])


def _reduce_partials(g, r, chip_arr, c_arr, name):
    L, n, R, C = g.shape
    H = R // 2
    th = _tile(H, ROW_TILE, 16)
    nb = H // th

    def body(j_ref, c_ref, g_ref, r_ref, out_ref):
        s = g_ref[...].astype(F32)
        for k in range(N_DEV - 1):
            s = s + r_ref[k].astype(F32)
        out_ref[...] = s

    grid_spec = pltpu.PrefetchScalarGridSpec(
        num_scalar_prefetch=2, grid=(L, nb),
        in_specs=[pl.BlockSpec((None, None, th, C), lambda l, i, j_ref, c_ref: (l, j_ref[0], c_ref[0] * nb + i, 0)),
                  pl.BlockSpec((None, N_DEV - 1, th, C), lambda l, i, j_ref, c_ref: (l, 0, i, 0))],
        out_specs=pl.BlockSpec((None, th, C), lambda l, i, j_ref, c_ref: (l, c_ref[0] * nb + i, 0)))
    return _pallas(body, name=name, grid_spec=grid_spec, out_shape=jax.ShapeDtypeStruct((L, R, C), F32),
                   compiler_params=_cparams("parallel", "parallel"))(chip_arr, c_arr, g, r)


def _join_halves(fs):
    n = len(fs)

    def body(*refs):
        out_refs, (send_sems, recv_sems) = refs[n:2 * n], refs[2 * n:]
        x, y, c = _place()
        cps = []
        for i, out_ref in enumerate(out_refs):
            h = out_ref.shape[1] // 2
            mine = out_ref.at[:, pl.ds(c * h, h)]
            cps.append(_remote_copy(send_sems, recv_sems, i, mine, mine, (x, y, 1 - c)))
            cps[-1].start()
        for cp in cps:
            cp.wait()

    return _comm_call(body, "grad_join_halves", fs, [jax.ShapeDtypeStruct(f.shape, f.dtype) for f in fs], n, in_place=True)


def _allgather_small(v, name):
    R, C = v.shape

    def body(v_ref, out_ref, send_sems, recv_sems):
        x, y, c = _place()
        me = 4 * x + 2 * y + c
        out_ref[me] = v_ref[...]
        cps = []
        for k in range(1, N_DEV):
            peer = (1 - x if k & 4 else x, 1 - y if k & 2 else y, 1 - c if k & 1 else c)
            cps.append(pltpu.make_async_remote_copy(src_ref=v_ref, dst_ref=out_ref.at[me], send_sem=send_sems.at[k - 1],
                                                    recv_sem=recv_sems.at[k - 1], device_id=peer, device_id_type=MESH))
        for cp in cps:
            cp.start()
        for cp in cps:
            cp.wait()

    return _pallas(body, name=name, in_specs=[VMEM_SPEC], out_specs=VMEM_SPEC,
                   out_shape=jax.ShapeDtypeStruct((N_DEV, R, C), v.dtype),
                   scratch_shapes=[pltpu.SemaphoreType.DMA((N_DEV - 1,)), pltpu.SemaphoreType.DMA((N_DEV - 1,))],
                   compiler_params=pltpu.CompilerParams(has_side_effects=True))(v)


def _sum_devices(g):
    n, R, C = g.shape

    def body(g_ref, out_ref):
        s = g_ref[0]
        for d in range(1, n):
            s = s + g_ref[d]
        out_ref[...] = s

    return _pallas(body, name="sum_devices", in_specs=[VMEM_SPEC], out_specs=VMEM_SPEC,
                   out_shape=jax.ShapeDtypeStruct((R, C), g.dtype))(g)


def _pad_heads(a, width):
    lead = a.shape[:-1]
    a = a.reshape(lead + (N_HEADS, width))
    a = jnp.pad(a, [(0, 0)] * len(lead) + [(0, 0), (0, HEAD_PAD - width)])
    return a.reshape(lead + (HP,))


def _unpad_heads(a, width):
    lead = a.shape[:-1]
    return a.reshape(lead + (N_HEADS, HEAD_PAD))[..., :width].reshape(lead + (N_HEADS * width,))


def _pre_weights(slabs, l):
    cols = lambda name: slabs[name][:, l].transpose(1, 0, 2).reshape(slabs[name].shape[2], -1)
    w_in = cols("w_in")
    kpe = jnp.pad(w_in[:, 640:672], ((0, 0), (ROPE_LO, HEAD_PAD - ROPE_HI)))
    w_ukv = cols("w_ukv").reshape(KV_LORA, N_HEADS, QK_NOPE + V_HEAD)
    return dict(
        w_in=jnp.concatenate([w_in[:, :640], kpe, w_in[:, 672:]], axis=1),
        w_uq=_pad_heads(cols("w_uq"), QK_HEAD),
        w_ukv=jnp.concatenate([_pad_heads(w_ukv[..., :QK_NOPE].reshape(KV_LORA, -1), QK_NOPE),
                               _pad_heads(w_ukv[..., QK_NOPE:].reshape(KV_LORA, -1), V_HEAD)], axis=1),
    )


def _post_weights(slabs, l, w_o_padded=None):
    if w_o_padded is None:
        w_o = slabs["w_o"][:, l].reshape(D_MODEL, D_MODEL)
        w_o_padded = jnp.concatenate([_pad_heads(w_o[:ATTN_WIDTH].T, V_HEAD).T, w_o[ATTN_WIDTH:]], axis=0)
    return dict(
        w_o=w_o_padded,
        w_up=Gathered(slabs["w_up"], l, 1), w_down=Gathered(slabs["w_down"], l, 0),
        w_ple_gate=Gathered(slabs["w_ple_gate"], l, 0), w_ple=Gathered(slabs["w_ple"], l, 1),
    )


def _gains(small, l):
    row = lambda name: small[name][l].reshape(1, -1)
    headrow = lambda a, b: jnp.pad(jnp.concatenate([small[a][l], small[b][l]]), (0, HEAD_PAD - QK_HEAD)).reshape(1, HEAD_PAD)
    return dict(
        g_mix=row("g_mix"), g_q_lat=row("g_q_lat"), g_kv_lat=row("g_kv_lat"), g_mlp=row("g_mlp"), g_ple=row("g_ple"),
        g_out_conv=row("g_out_conv"), g_out_attn=_pad_heads(small["g_out_attn"][l], V_HEAD).reshape(1, HP),
        gq=headrow("g_qn_nope", "g_qn_rope"), gk=headrow("g_kn_nope", "g_kn_rope"),
        cw8=jnp.pad(small["conv_w"][l], ((0, SUBLANES - CONV_TAPS), (0, 0))),
    )


def _w_o_shards(dw_o):
    return jnp.concatenate([_unpad_heads(dw_o[:HP].T, V_HEAD).T, dw_o[HP:]], axis=0).reshape(N_CHIPS, -1, D_MODEL)


def _unpad_grads(gp):
    dw_in = gp["w_in"]
    dw_ukv = gp["w_ukv"]
    k_part = dw_ukv[:, :HP].reshape(KV_LORA, N_HEADS, HEAD_PAD)[..., :QK_NOPE]
    v_part = dw_ukv[:, HP:].reshape(KV_LORA, N_HEADS, HEAD_PAD)[..., :V_HEAD]
    first = lambda name: gp[name][0]
    col_shards = lambda a: a.reshape(a.shape[0], N_CHIPS, -1).transpose(1, 0, 2)
    return dict(
        w_in=col_shards(jnp.concatenate([dw_in[:, :640], dw_in[:, Z_KPE + ROPE_LO:Z_KPE + ROPE_HI], dw_in[:, Z_GB:]], axis=1)),
        w_uq=col_shards(_unpad_heads(gp["w_uq"], QK_HEAD)),
        w_ukv=col_shards(jnp.concatenate([k_part, v_part], axis=-1).reshape(KV_LORA, -1)),
        w_o=gp["w_o"], w_up=gp["w_up"], w_down=gp["w_down"], w_ple_gate=gp["w_ple_gate"], w_ple=gp["w_ple"],
        g_mix=first("g_mix"), g_q_lat=first("g_q_lat"), g_kv_lat=first("g_kv_lat"), g_mlp=first("g_mlp"),
        g_ple=first("g_ple"), g_out_conv=first("g_out_conv"), g_out_attn=_unpad_heads(first("g_out_attn"), V_HEAD),
        g_qn_nope=gp["gq"][0, :QK_NOPE], g_qn_rope=gp["gq"][0, QK_NOPE:QK_HEAD],
        g_kn_nope=gp["gk"][0, :QK_NOPE], g_kn_rope=gp["gk"][0, QK_NOPE:QK_HEAD],
        conv_w=gp["cw8"][:CONV_TAPS],
    )


def _rope_tables(positions):
    inv_freq = 1.0 / (ROPE_THETA ** (jnp.arange(0, QK_ROPE, 2, dtype=F32) / QK_ROPE))
    ang = positions.astype(F32)[:, None] * inv_freq
    cos, sin = jnp.cos(ang), jnp.sin(ang)
    pad = lambda t, v: jnp.pad(jnp.concatenate([t, t], axis=1), ((0, 0), (ROPE_LO, HEAD_PAD - ROPE_HI)), constant_values=v)
    return pad(cos, 1.0), pad(sin, 0.0)


def _layer_fwd(x0, p_l, W, cs, sn, attend):
    h = _rms_fwd(x0, W["g_mix"], "rms_mix")
    z = _mm(h, W["w_in"], name="mm_in")
    qln, kvln = _lat_fwd(z, W["g_q_lat"], W["g_kv_lat"])
    q_raw = _mm(qln, W["w_uq"], name="mm_uq")
    kv_raw = _mm(kvln, W["w_ukv"], name="mm_ukv")
    qf, kf, vb = _qk_fwd(q_raw, kv_raw, z, cs, sn, W["gq"], W["gk"])
    o, lse = attend(qf, kf, vb)
    conv = _conv_fwd(z, W["cw8"])
    mixed = _mix_fwd(o, conv, W["g_out_attn"], W["g_out_conv"])
    x1 = _mm(mixed, W["w_o"], res=x0, name="mm_o")
    h2 = _rms_fwd(x1, W["g_mlp"], "rms_mlp")
    a, f = _mm(h2, W["w_up"], epi="relu2", name="mm_up")
    x2 = _mm(f, W["w_down"], res=x1, name="mm_down")
    h3 = _rms_fwd(x2, W["g_ple"], "rms_ple")
    gl = _mm(h3, W["w_ple_gate"], name="mm_ple_gate")
    pe = _mm(p_l, W["w_ple"], name="mm_ple")
    x3 = _ple_fwd(x2, gl, pe)
    saved = dict(x0=x0, h=h, z=z, qln=qln, kvln=kvln, q_raw=q_raw, kv_raw=kv_raw, qf=qf, kf=kf, vb=vb, o=o, lse=lse,
                 conv=conv, mixed=mixed, x1=x1, h2=h2, a=a, f=f, x2=x2, h3=h3, gl=gl, pe=pe)
    return x3, saved


def _layer_bwd(dx3, p_l, W, cs, sn, sv, attend_bwd):
    g = {}
    dpe, dgl = _ple_bwd(dx3, sv["gl"], sv["pe"])
    g["w_ple"] = _mm(p_l, dpe, mode="tn", out_dtype=BF16, shard_out=1, name="mm_dw_ple")
    g["w_ple_gate"] = _mm(sv["h3"], dgl, mode="tn", out_dtype=BF16, shard_out=0, name="mm_dw_ple_gate")
    dh3 = _mm(dgl, W["w_ple_gate"], mode="nt", name="mm_dh3")
    dx2, dx2b, g["g_ple"] = _rms_bwd(dh3, sv["x2"], W["g_ple"], dx3, "rms_ple_bwd")
    da = _mm(dx2b, W["w_down"], mode="nt", aux=sv["a"], epi="drelu2", out_dtype=BF16, name="mm_da")
    g["w_down"] = _mm(sv["f"], dx2b, mode="tn", out_dtype=BF16, shard_out=0, name="mm_dw_down")
    g["w_up"] = _mm(sv["h2"], da, mode="tn", out_dtype=BF16, shard_out=1, name="mm_dw_up")
    dh2 = _mm(da, W["w_up"], mode="nt", name="mm_dh2")
    dx1, dx1b, g["g_mlp"] = _rms_bwd(dh2, sv["x1"], W["g_mlp"], dx2, "rms_mlp_bwd")
    dmixed = _mm(dx1b, W["w_o"], mode="nt", name="mm_dmixed")
    g["w_o"] = _mm(sv["mixed"], dx1b, mode="tn", out_dtype=BF16, name="mm_dw_o")
    do, dconv, g["g_out_attn"], g["g_out_conv"] = _mix_bwd(dmixed, sv["o"], sv["conv"], W["g_out_attn"], W["g_out_conv"])
    dgb, dgc, dxin, g["cw8"] = _conv_bwd(dconv, sv["z"], W["cw8"])
    dqf, dkf, dv = attend_bwd(g, sv["qf"], sv["kf"], sv["vb"], sv["o"], do, sv["lse"])
    dq_raw, dkv_raw, dkpe, g["gq"], g["gk"] = _qk_bwd(dqf, dkf, dv, sv["q_raw"], sv["kv_raw"], sv["z"], cs, sn, W["gq"], W["gk"])
    g["w_uq"] = _mm(sv["qln"], dq_raw, mode="tn", out_dtype=BF16, name="mm_dw_uq")
    dqln = _mm(dq_raw, W["w_uq"], mode="nt", name="mm_dqln")
    g["w_ukv"] = _mm(sv["kvln"], dkv_raw, mode="tn", out_dtype=BF16, name="mm_dw_ukv")
    dkvln = _mm(dkv_raw, W["w_ukv"], mode="nt", name="mm_dkvln")
    dlat, g["g_q_lat"], g["g_kv_lat"] = _lat_bwd(dqln, dkvln, sv["z"], W["g_q_lat"], W["g_kv_lat"])
    dz = jnp.concatenate([dlat, dkpe, dgb, dgc, dxin], axis=1)
    g["w_in"] = _mm(sv["h"], dz, mode="tn", out_dtype=BF16, name="mm_dw_in")
    dh = _mm(dz, W["w_in"], mode="nt", name="mm_dh")
    dx0, g["g_mix"] = _rms_bwd(dh, sv["x0"], W["g_mix"], dx1, "rms_mix_bwd", want_bf16=False)
    return dx0, g


def _local_step(x, p, positions, target, slabs, small):
    depth = p.shape[0]
    cs, sn = _rope_tables(positions)
    slabs = dict(slabs)

    def gather(items, host):
        touched = [n for n in PRE + POST if any(n == name for name, _ in items)]
        rider = _allgather_rider([slabs[n] for n in touched], [(touched.index(name), l) for name, l in items])
        if host is None:
            out, new = (), rider.run("allgather_first")
        else:
            *out, new = host(rider)
        slabs.update(zip(touched, new))
        return out

    gather([(name, 0) for name in PRE], None)
    Ws, saved = [], []
    for l in range(depth):
        W = dict(_gains(small, l), **_pre_weights(slabs, l))

        def attend(qf, kf, vb, W=W, l=l):
            items = [(name, l) for name in POST] + ([(name, l + 1) for name in PRE] if l + 1 < depth else [])
            o, lse = gather(items, functools.partial(_attn_fwd, qf, kf, vb))
            W.update(_post_weights(slabs, l))
            return o, lse

        x, sv = _layer_fwd(x, p[l], W, cs, sn, attend)
        Ws.append(W)
        saved.append(sv)
    dx, sq = _loss_grad(x, target)

    landing = {name: lax.empty((depth, N_DEV - 1, slabs[name].shape[2] // 2, slabs[name].shape[3]), BF16) for name in PRE + POST}

    def exchange(sends, host):
        touched = [n for n in PRE + POST if any(n == name for name, _, _ in sends)]
        rider = _exchange_rider([g for _, _, g in sends], [landing[n] for n in touched],
                                [(i, touched.index(name), l) for i, (name, l, _) in enumerate(sends)])
        if host is None:
            out, new = (), rider.run("grad_exchange_last")
        else:
            *out, new = host(rider)
        landing.update(zip(touched, new))
        return out

    grads = [None] * depth
    for l in reversed(range(depth)):
        W = dict(Ws[l], **_post_weights(slabs, l, Ws[l]["w_o"]))

        def attend_bwd(g, qf, kf, vb, o, do, lse, l=l):
            g["w_o"] = _w_o_shards(g["w_o"])
            sends = [(name, l, g[name]) for name in POST]
            if l + 1 < depth:
                sends += [(name, l + 1, grads[l + 1][name]) for name in PRE]
            return exchange(sends, functools.partial(_attn_bwd, qf, kf, vb, o, do, lse))

        dx, gp = _layer_bwd(dx, p[l], W, cs, sn, saved[l], attend_bwd)
        grads[l] = _unpad_grads(gp)
    exchange([(name, 0, grads[0][name]) for name in PRE], None)
    return sq, dx, grads, landing


def _small_rows(extra):
    return sum(_pad128(n) for _, n in SMALL + extra) // LANES


def _pack_small(vals, depth, extra, tail_rows=0):
    parts = []
    for l in range(depth):
        for name, n in SMALL + extra:
            parts.append(jnp.pad(vals[name][l].reshape(-1), (0, _pad128(n) - n)))
    flat = jnp.concatenate(parts).reshape(-1, LANES)
    rows = flat.shape[0] + tail_rows
    return jnp.pad(flat, ((0, -(-rows // SUBLANES) * SUBLANES - flat.shape[0]), (0, 0)))


def _unpack_small(flat, depth, extra):
    per_layer = _small_rows(extra) * LANES
    body = flat.reshape(-1)[:depth * per_layer].reshape(depth, per_layer)
    out, off = {}, 0
    for name, n in SMALL + extra:
        out[name] = body[:, off:off + n]
        off += _pad128(n)
    return out


def kernel(x, p, positions, g_mix, w_in, g_q_lat, w_uq, g_kv_lat, w_ukv, g_qn_nope, g_qn_rope, g_kn_nope, g_kn_rope, conv_w, g_out_attn, g_out_conv, w_o, g_mlp, w_up, w_down, g_ple, w_ple_gate, w_ple, loss_target, m_g_mix, m_w_in, m_g_q_lat, m_w_uq, m_g_kv_lat, m_w_ukv, m_g_qn_nope, m_g_qn_rope, m_g_kn_nope, m_g_kn_rope, m_conv_w, m_g_out_attn, m_g_out_conv, m_w_o, m_g_mlp, m_w_up, m_w_down, m_g_ple, m_w_ple_gate, m_w_ple, v_g_mix, v_w_in, v_g_q_lat, v_w_uq, v_g_kv_lat, v_w_ukv, v_g_qn_nope, v_g_qn_rope, v_g_kn_nope, v_g_kn_rope, v_conv_w, v_g_out_attn, v_g_out_conv, v_w_o, v_g_mlp, v_w_up, v_w_down, v_g_ple, v_w_ple_gate, v_w_ple):
    w = dict(g_mix=g_mix, w_in=w_in, g_q_lat=g_q_lat, w_uq=w_uq, g_kv_lat=g_kv_lat, w_ukv=w_ukv, g_qn_nope=g_qn_nope,
             g_qn_rope=g_qn_rope, g_kn_nope=g_kn_nope, g_kn_rope=g_kn_rope, conv_w=conv_w, g_out_attn=g_out_attn,
             g_out_conv=g_out_conv, w_o=w_o, g_mlp=g_mlp, w_up=w_up, w_down=w_down, g_ple=g_ple, w_ple_gate=w_ple_gate,
             w_ple=w_ple)
    m = dict(g_mix=m_g_mix, w_in=m_w_in, g_q_lat=m_g_q_lat, w_uq=m_w_uq, g_kv_lat=m_g_kv_lat, w_ukv=m_w_ukv,
             g_qn_nope=m_g_qn_nope, g_qn_rope=m_g_qn_rope, g_kn_nope=m_g_kn_nope, g_kn_rope=m_g_kn_rope, conv_w=m_conv_w,
             g_out_attn=m_g_out_attn, g_out_conv=m_g_out_conv, w_o=m_w_o, g_mlp=m_g_mlp, w_up=m_w_up, w_down=m_w_down,
             g_ple=m_g_ple, w_ple_gate=m_w_ple_gate, w_ple=m_w_ple)
    v = dict(g_mix=v_g_mix, w_in=v_w_in, g_q_lat=v_g_q_lat, w_uq=v_w_uq, g_kv_lat=v_g_kv_lat, w_ukv=v_w_ukv,
             g_qn_nope=v_g_qn_nope, g_qn_rope=v_g_qn_rope, g_kn_nope=v_g_kn_nope, g_kn_rope=v_g_kn_rope, conv_w=v_conv_w,
             g_out_attn=v_g_out_attn, g_out_conv=v_g_out_conv, w_o=v_w_o, g_mlp=v_g_mlp, w_up=v_w_up, w_down=v_w_down,
             g_ple=v_g_ple, w_ple_gate=v_w_ple_gate, w_ple=v_w_ple)
    depth = p.shape[0]
    ax, ay, ac = _place()
    chip = 2 * ax + ay
    c_arr = jnp.reshape(ac, (1,)).astype(jnp.int32)
    chip_arr = jnp.reshape(chip, (1,)).astype(jnp.int32)
    conv_shard =("conv_w", CONV_TAPS * CONV_WIDTH // N_CHIPS)
    conv_full = ("conv_w", CONV_TAPS * CONV_WIDTH)

    names = PRE + POST
    slabs = {name: _shard_slab(w[name], chip_arr, "shard_slab_" + name) for name in names}
    conv_rows = -(-depth * CONV_TAPS // SUBLANES) * SUBLANES
    conv_all = _allgather_small(jnp.pad(conv_w.reshape(depth * CONV_TAPS, LANES), ((0, conv_rows - depth * CONV_TAPS), (0, 0))),
                                "allgather_conv_w")
    conv_cat = jnp.concatenate([conv_all[2 * j, :depth * CONV_TAPS] for j in range(N_CHIPS)], axis=1)
    small = {name: w[name] for name, _ in SMALL}
    small["conv_w"] = conv_cat.reshape(depth, CONV_TAPS, CONV_WIDTH)

    sq, grad_x, grads, landing = _local_step(x[0], p[:, 0], positions[0], loss_target[0], slabs, small)

    halves = [_reduce_partials(jnp.stack([grads[l][name] for l in range(depth)]), landing[name], chip_arr, c_arr,
                               "grad_reduce_" + name) for name in names]
    g_out = dict(zip(names, _join_halves(halves)))

    stacked = {name: jnp.stack([grads[l][name] for l in range(depth)]) for name, _ in SMALL + (conv_full,)}
    loss_row = jnp.pad(sq[:1] * (0.5 / D_MODEL), ((0, 0), (0, 0)))
    packed = _pack_small(stacked, depth, (conv_full,), tail_rows=1)
    n_rows = depth * _small_rows((conv_full,))
    packed = packed.at[n_rows].set(loss_row[0])
    summed = _sum_devices(_allgather_small(packed, "allgather_small_grads"))
    loss = summed[n_rows, 0]
    g_small = _unpack_small(summed, depth, (conv_full,))
    for name, n in SMALL:
        g_out[name] = g_small[name]
    g_conv = g_small["conv_w"].reshape(depth, CONV_TAPS, CONV_WIDTH)
    g_out["conv_w"] = lax.dynamic_slice_in_dim(g_conv, chip * LANES, LANES, axis=2)

    delta, new_m, new_v = {}, {}, {}
    for name in names:
        delta[name], new_m[name], new_v[name] = _adamw(w[name], g_out[name], m[name], v[name], "adamw_" + name)
    pack = lambda vals: _pack_small({k: a.reshape(depth, -1) for k, a in vals.items()}, depth, (conv_shard,))[None]
    with_conv = lambda vals: {**{name: vals[name] for name, _ in SMALL}, "conv_w": vals["conv_w"]}
    d, m2, v2 = _adamw(pack(with_conv(w)), pack(with_conv(g_out)), pack(with_conv(m)), pack(with_conv(v)), "adamw_small")
    for res, packed_res in ((delta, d), (new_m, m2), (new_v, v2)):
        un = _unpack_small(packed_res[0], depth, (conv_shard,))
        for name, _ in SMALL + (conv_shard,):
            res[name] = un[name].reshape(w[name].shape)

    return (loss, grad_x[None], *[g_out[n] for n in WEIGHT_ORDER], *[delta[n] for n in WEIGHT_ORDER],
            *[new_m[n] for n in WEIGHT_ORDER], *[new_v[n] for n in WEIGHT_ORDER])
```

```python
import functools

import jax
import jax.numpy as jnp
from jax import lax
from jax.experimental import pallas as pl
from jax.experimental.pallas import tpu as pltpu

F32 = jnp.float32
BF16 = jnp.bfloat16
MESH = pl.DeviceIdType.MESH

D_MODEL = 1024
N_HEADS = 8
QK_NOPE = 64
QK_ROPE = 32
QK_HEAD = QK_NOPE + QK_ROPE
V_HEAD = 64
Q_LORA = 384
KV_LORA = 256
ATTN_WIDTH = N_HEADS * V_HEAD
CONV_WIDTH = 512
CONV_TAPS = 3
D_FF = 4096
PLE_DIM = 256
ROPE_THETA = 10000.0
EPS = 1e-6
ATT_SCALE = QK_HEAD ** -0.5
LOG2E = 1.4426950408889634
ATT_SCALE_LOG2 = ATT_SCALE * LOG2E

ADAM_LR = 0.001
ADAM_B1 = 0.9
ADAM_B2 = 0.999
ADAM_EPS = 1e-08
ADAM_WD = 0.01
ADAM_STEP = 10

LANES = 128
SUBLANES = 8
HEAD_PAD = LANES
HP = N_HEADS * HEAD_PAD
ROPE_LO = QK_NOPE
ROPE_MID = QK_NOPE + QK_ROPE // 2
ROPE_HI = QK_NOPE + QK_ROPE
VMEM_LIMIT = 56 * 1024 * 1024

Z_Q, Z_KV, Z_KPE, Z_GB, Z_GC, Z_XIN = 0, 384, 640, 768, 1280, 1792
Z_COLS = 2304
Z_LAT = Z_KPE

ROW_TILE = 512
ATT_TILE = 256
ATT_CHAINS_FWD, ATT_CHAINS_BWD = 8, 4
MM_TM, MM_TN, MM_TK = 1024, 1024, 4096
MM_TM_MIN, MM_TK_MIN = 256, 512
MM_VMEM_BUDGET = 40 * 1024 * 1024

N_CHIPS = 4
N_DEV = 8

PRE = ("w_in", "w_uq", "w_ukv")
POST = ("w_o", "w_up", "w_down", "w_ple_gate", "w_ple")
SMALL = (
    ("g_mix", 1024), ("g_q_lat", 384), ("g_kv_lat", 256), ("g_qn_nope", 64), ("g_qn_rope", 32), ("g_kn_nope", 64),
    ("g_kn_rope", 32), ("g_out_attn", 512), ("g_out_conv", 512), ("g_mlp", 1024), ("g_ple", 1024),
)
WEIGHT_ORDER = ("g_mix", "w_in", "g_q_lat", "w_uq", "g_kv_lat", "w_ukv", "g_qn_nope", "g_qn_rope", "g_kn_nope",
                "g_kn_rope", "conv_w", "g_out_attn", "g_out_conv", "w_o", "g_mlp", "w_up", "w_down", "g_ple",
                "w_ple_gate", "w_ple")


def _pallas(body, **kw):
    return pl.pallas_call(body, **kw)


def _cparams(*sem):
    return pltpu.CompilerParams(dimension_semantics=sem, vmem_limit_bytes=VMEM_LIMIT)


def _tile(dim, pref, unit=LANES):
    if dim <= pref:
        return dim
    t = (pref // unit) * unit
    while t > unit and dim % t:
        t -= unit
    assert dim % t == 0, (dim, pref)
    return t


def _pad128(n):
    return -(-n // LANES) * LANES


_DIMS = {"nn": (((1,), (0,)), ((), ())), "nt": (((1,), (1,)), ((), ())), "tn": (((0,), (0,)), ((), ()))}


class Gathered:
    def __init__(self, arr, l, axis):
        self.arr, self.l, self.axis = arr, l, axis
        _, _, ks, ns = arr.shape
        self.shape = (ks * N_CHIPS, ns) if axis == 0 else (ks, ns * N_CHIPS)
        self.row_limit = ks if axis == 0 else None
        self.col_limit = ns if axis == 1 else None

    def spec(self, tr, tc, rc):
        l, axis = self.l, self.axis
        _, _, ks, ns = self.arr.shape
        per = (ks // tr) if axis == 0 else (ns // tc)

        def index(i, j, k):
            r, c = rc(i, j, k)
            return (r // per, l, r % per, c) if axis == 0 else (c // per, l, r, c % per)

        return pl.BlockSpec((None, None, tr, tc), index)


def _mm(a, b, *, mode="nn", res=None, aux=None, gain=None, epi=None, out_dtype=F32, shard_out=None, name):
    whole_rows = epi in ("rms", "rms_bwd")
    if mode == "nn":
        (M, K), (K2, N) = a.shape, b.shape
    elif mode == "nt":
        (M, K), (N, K2) = a.shape, b.shape
    else:
        (K, M), (K2, N) = a.shape, b.shape
    assert K == K2, (a.shape, b.shape, mode)
    b_rows, b_cols = getattr(b, "row_limit", None), getattr(b, "col_limit", None)
    n_lim, k_lim = (b_rows, b_cols) if mode == "nt" else (b_cols, b_rows)
    m_lim = M // N_CHIPS if shard_out == 0 else None
    if shard_out == 1:
        n_lim = N // N_CHIPS
    extra = [t for t in (res, aux) if t is not None]
    out_bytes = 6 if epi in ("relu2", "rms", "rms_bwd") else jnp.dtype(out_dtype).itemsize
    if whole_rows:
        assert (n_lim or N) == N and N <= MM_TN and shard_out is None and gain is not None
        m_lim = min(M, MM_TM // 2)

    def footprint(tm, tn, tk):
        blocks = tm * tk * a.dtype.itemsize + tk * tn * 2 + tm * tn * (out_bytes + 4 * len(extra))
        return 2 * blocks + (tm * tn * 4 if tk < K else 0)

    tm, tn, tk = _tile(m_lim or M, MM_TM), _tile(n_lim or N, MM_TN), _tile(k_lim or K, MM_TK)
    while footprint(tm, tn, tk) > MM_VMEM_BUDGET and tk > MM_TK_MIN:
        tk = _tile(K, tk // 2)
    while footprint(tm, tn, tk) > MM_VMEM_BUDGET and tm > MM_TM_MIN:
        tm = _tile(M, tm // 2)
    nk = K // tk
    a_spec = pl.BlockSpec((tk, tm), lambda i, j, k: (k, i)) if mode == "tn" else pl.BlockSpec((tm, tk), lambda i, j, k: (i, k))
    if mode == "nt":
        b_block, b_rc = (tn, tk), (lambda i, j, k: (j, k))
    else:
        b_block, b_rc = (tk, tn), (lambda i, j, k: (k, j))
    if isinstance(b, Gathered):
        b_spec, b = b.spec(*b_block, b_rc), b.arr
    else:
        b_spec = pl.BlockSpec(b_block, b_rc)
    mn_spec = pl.BlockSpec((tm, tn), lambda i, j, k: (i, j))
    dims = _DIMS[mode]
    n_out = {"relu2": 2, "rms": 2, "rms_bwd": 3}.get(epi, 1)

    def body(*refs):
        a_ref, b_ref = refs[0], refs[1]
        extra_refs = refs[2:2 + len(extra)]
        gain_ref = refs[2 + len(extra)] if whole_rows else None
        out_refs = refs[2 + len(extra) + whole_rows:][:n_out]
        prod = lax.dot_general(a_ref[...].astype(BF16), b_ref[...].astype(BF16), dims, preferred_element_type=F32)

        def finish(r):
            if epi == "rms_bwd":
                dx, dgc = _norm_bwd(extra_refs[1][...], r, gain_ref[...], N)
                dx = dx + extra_refs[0][...]
                out_refs[0][...] = dx
                out_refs[1][...] = dx.astype(BF16)
                _accumulate(out_refs[2], dgc)
                return
            if res is not None:
                r = r + extra_refs[0][...]
            if epi == "rms":
                out_refs[0][...] = r
                out_refs[1][...] = (r * _rinv(r, N) * gain_ref[...]).astype(BF16)
            elif epi == "relu2":
                out_refs[0][...] = r
                t = jnp.maximum(r, 0.0)
                out_refs[1][...] = (t * t).astype(BF16)
            elif epi == "drelu2":
                out_refs[0][...] = (r * (2.0 * jnp.maximum(extra_refs[-1][...], 0.0))).astype(out_dtype)
            else:
                out_refs[0][...] = r.astype(out_dtype)

        if nk == 1:
            finish(prod)
        else:
            acc = refs[-1]
            k = pl.program_id(2)

            @pl.when(k == 0)
            def _():
                acc[...] = prod

            @pl.when(k > 0)
            def _():
                acc[...] += prod

            @pl.when(k == nk - 1)
            def _():
                finish(acc[...])

    if epi in ("relu2", "rms", "rms_bwd"):
        out_shape = [jax.ShapeDtypeStruct((M, N), F32), jax.ShapeDtypeStruct((M, N), BF16)]
        out_specs = [mn_spec, mn_spec]
        if epi == "rms_bwd":
            out_shape.append(jax.ShapeDtypeStruct((SUBLANES, N), F32))
            out_specs.append(pl.BlockSpec((SUBLANES, tn), lambda i, j, k: (0, j)))
    elif shard_out == 0:
        per = (M // N_CHIPS) // tm
        out_shape = jax.ShapeDtypeStruct((N_CHIPS, M // N_CHIPS, N), out_dtype)
        out_specs = pl.BlockSpec((None, tm, tn), lambda i, j, k: (i // per, i % per, j))
    elif shard_out == 1:
        per = (N // N_CHIPS) // tn
        out_shape = jax.ShapeDtypeStruct((N_CHIPS, M, N // N_CHIPS), out_dtype)
        out_specs = pl.BlockSpec((None, tm, tn), lambda i, j, k: (j // per, i, j % per))
    else:
        out_shape = jax.ShapeDtypeStruct((M, N), out_dtype)
        out_specs = mn_spec
    gains = [gain] if whole_rows else []
    return _pallas(
        body, name=name, grid=(M // tm, N // tn, nk),
        in_specs=[a_spec, b_spec] + [mn_spec] * len(extra) + [pl.BlockSpec((1, tn), lambda i, j, k: (0, j))] * len(gains),
        out_specs=out_specs, out_shape=out_shape,
        scratch_shapes=[pltpu.VMEM((tm, tn), F32)] if nk > 1 else [],
        compiler_params=_cparams(*(["arbitrary"] * 3 if epi == "rms_bwd" else ["parallel", "parallel", "arbitrary"])),
    )(a, b, *extra, *gains)


def _rows(ts, d, col=0):
    return pl.BlockSpec((ts, d), lambda i: (i, col))


def _gain(d):
    return pl.BlockSpec((1, d), lambda i: (0, 0))


def _accum(d):
    return pl.BlockSpec((SUBLANES, d), lambda i: (0, 0))


def _accumulate(ref, val):
    i = pl.program_id(0)

    @pl.when(i == 0)
    def _():
        ref[...] = jnp.zeros_like(ref)

    ref[...] += jnp.broadcast_to(jnp.sum(val, axis=0, keepdims=True), ref.shape)


def _rinv(x, n):
    return lax.rsqrt(jnp.sum(x * x, axis=-1, keepdims=True) / n + EPS)


def _norm_bwd(x, dy, g, n):
    r = _rinv(x, n)
    xhat = x * r
    dyg = dy * g
    dx = r * (dyg - xhat * (jnp.sum(dyg * xhat, axis=-1, keepdims=True) / n))
    return dx, dy * xhat


def _rms_fwd(x, g, name):
    S, D = x.shape
    ts = _tile(S, ROW_TILE, SUBLANES)

    def body(x_ref, g_ref, h_ref):
        xv = x_ref[...]
        h_ref[...] = (xv * _rinv(xv, D) * g_ref[...]).astype(BF16)

    return _pallas(body, name=name, grid=(S // ts,), in_specs=[_rows(ts, D), _gain(D)], out_specs=_rows(ts, D),
                   out_shape=jax.ShapeDtypeStruct((S, D), BF16), compiler_params=_cparams("parallel"))(x, g)


def _lat_fwd(z, gq, gkv):
    S = z.shape[0]
    ts = _tile(S, ROW_TILE, SUBLANES)

    def body(z_ref, gq_ref, gkv_ref, q_ref, kv_ref):
        zq = z_ref[:, Z_Q:Z_KV]
        zkv = z_ref[:, Z_KV:Z_KPE]
        q_ref[...] = (zq * _rinv(zq, Q_LORA) * gq_ref[...]).astype(BF16)
        kv_ref[...] = (zkv * _rinv(zkv, KV_LORA) * gkv_ref[...]).astype(BF16)

    return _pallas(body, name="lat_fwd", grid=(S // ts,), in_specs=[_rows(ts, Z_LAT), _gain(Q_LORA), _gain(KV_LORA)],
                   out_specs=[_rows(ts, Q_LORA), _rows(ts, KV_LORA)],
                   out_shape=[jax.ShapeDtypeStruct((S, Q_LORA), BF16), jax.ShapeDtypeStruct((S, KV_LORA), BF16)],
                   compiler_params=_cparams("parallel"))(z, gq, gkv)


def _lat_bwd(dq, dkv, z, gq, gkv):
    S = z.shape[0]
    ts = _tile(S, ROW_TILE, SUBLANES)

    def body(dq_ref, dkv_ref, z_ref, gq_ref, gkv_ref, dlat_ref, dgq_ref, dgkv_ref):
        dxq, cq = _norm_bwd(z_ref[:, Z_Q:Z_KV], dq_ref[...], gq_ref[...], Q_LORA)
        dxkv, ckv = _norm_bwd(z_ref[:, Z_KV:Z_KPE], dkv_ref[...], gkv_ref[...], KV_LORA)
        dlat_ref[:, Z_Q:Z_KV] = dxq.astype(BF16)
        dlat_ref[:, Z_KV:Z_KPE] = dxkv.astype(BF16)
        _accumulate(dgq_ref, cq)
        _accumulate(dgkv_ref, ckv)

    return _pallas(body, name="lat_bwd", grid=(S // ts,),
                   in_specs=[_rows(ts, Q_LORA), _rows(ts, KV_LORA), _rows(ts, Z_LAT), _gain(Q_LORA), _gain(KV_LORA)],
                   out_specs=[_rows(ts, Z_LAT), _accum(Q_LORA), _accum(KV_LORA)],
                   out_shape=[jax.ShapeDtypeStruct((S, Z_LAT), BF16), jax.ShapeDtypeStruct((SUBLANES, Q_LORA), F32),
                              jax.ShapeDtypeStruct((SUBLANES, KV_LORA), F32)],
                   compiler_params=_cparams("arbitrary"))(dq, dkv, z, gq, gkv)


def _head_masks():
    lane = lax.broadcasted_iota(jnp.int32, (1, HEAD_PAD), 1)
    return lane < ROPE_LO, (lane >= ROPE_LO) & (lane < ROPE_HI), lane < ROPE_MID


def _seg_sum(t, m_n, m_r):
    sn = jnp.sum(jnp.where(m_n, t, 0.0), axis=-1, keepdims=True) / QK_NOPE
    sr = jnp.sum(jnp.where(m_r, t, 0.0), axis=-1, keepdims=True) / QK_ROPE
    return jnp.where(m_n, sn, sr)


def _rot_half(y, lo):
    half = QK_ROPE // 2
    return jnp.where(lo, -pltpu.roll(y, HEAD_PAD - half, 1), pltpu.roll(y, half, 1))


def _qk_fwd(q_raw, kv_raw, z, cs, sn, gq, gk):
    S = q_raw.shape[0]
    ts = _tile(S, ROW_TILE, SUBLANES)

    def body(q_ref, k_ref, v_ref, kpe_ref, cs_ref, sn_ref, gq_ref, gk_ref, qf_ref, kf_ref, vb_ref):
        m_n, m_r, lo = _head_masks()
        cos, sin, gqv, gkv = cs_ref[...], sn_ref[...], gq_ref[...], gk_ref[...]

        def norm(x, g):
            return x * lax.rsqrt(_seg_sum(x * x, m_n, m_r) + EPS) * g

        def rope(y):
            return y * cos + _rot_half(y, lo) * sin

        kr = rope(norm(kpe_ref[...], gkv))
        lane = lax.broadcasted_iota(jnp.int32, (1, HEAD_PAD), 1)
        for h in range(N_HEADS):
            sl = slice(h * HEAD_PAD, (h + 1) * HEAD_PAD)
            qf_ref[:, sl] = rope(norm(q_ref[:, sl], gqv)).astype(BF16)
            kf_ref[:, sl] = (norm(k_ref[:, sl], gkv) + kr).astype(BF16)
            vb_ref[:, sl] = jnp.where(lane == V_HEAD, 1.0, v_ref[:, sl]).astype(BF16)

    hd = jax.ShapeDtypeStruct((S, HP), BF16)
    return _pallas(body, name="qk_fwd", grid=(S // ts,),
                   in_specs=[_rows(ts, HP), _rows(ts, HP, 0), _rows(ts, HP, 1), _rows(ts, HEAD_PAD, Z_KPE // HEAD_PAD),
                             _rows(ts, HEAD_PAD), _rows(ts, HEAD_PAD), _gain(HEAD_PAD), _gain(HEAD_PAD)],
                   out_specs=[_rows(ts, HP)] * 3, out_shape=[hd, hd, hd],
                   compiler_params=_cparams("parallel"))(q_raw, kv_raw, kv_raw, z, cs, sn, gq, gk)


def _qk_bwd(dqf, dkf, dv, q_raw, kv_raw, z, cs, sn, gq, gk):
    S = q_raw.shape[0]
    ts = _tile(S, ROW_TILE, SUBLANES)

    def body(dqf_ref, dkf_ref, dv_ref, q_ref, k_ref, kpe_ref, cs_ref, sn_ref, gq_ref, gk_ref,
             dq_ref, dkv_ref, dkpe_ref, dgq_ref, dgk_ref):
        m_n, m_r, lo = _head_masks()
        cos, sin, gqv, gkv = cs_ref[...], sn_ref[...], gq_ref[...], gk_ref[...]

        def rope_t(w):
            return w * cos - jnp.where(m_r, _rot_half(w * sin, lo), 0.0)

        def norm_bwd(x, dy, g):
            r = lax.rsqrt(_seg_sum(x * x, m_n, m_r) + EPS)
            xhat = x * r
            dyg = dy * g
            return r * (dyg - xhat * _seg_sum(dyg * xhat, m_n, m_r)), dy * xhat

        accq = jnp.zeros((ts, HEAD_PAD), F32)
        acck = jnp.zeros((ts, HEAD_PAD), F32)
        dkr = jnp.zeros((ts, HEAD_PAD), F32)
        for h in range(N_HEADS):
            sl = slice(h * HEAD_PAD, (h + 1) * HEAD_PAD)
            dx, c = norm_bwd(q_ref[:, sl], rope_t(dqf_ref[:, sl]), gqv)
            dq_ref[:, sl] = dx.astype(BF16)
            accq = accq + c
            dk = dkf_ref[:, sl]
            dkr = dkr + jnp.where(m_r, dk, 0.0)
            dx, c = norm_bwd(k_ref[:, sl], jnp.where(m_n, dk, 0.0), gkv)
            dkv_ref[:, sl] = dx.astype(BF16)
            acck = acck + c
            dkv_ref[:, HP + h * HEAD_PAD:HP + (h + 1) * HEAD_PAD] = dv_ref[:, sl].astype(BF16)
        dx, c = norm_bwd(kpe_ref[...], rope_t(dkr), gkv)
        dkpe_ref[...] = dx.astype(BF16)
        _accumulate(dgq_ref, accq)
        _accumulate(dgk_ref, acck + c)

    return _pallas(body, name="qk_bwd", grid=(S // ts,),
                   in_specs=[_rows(ts, HP)] * 4 + [_rows(ts, HP, 0), _rows(ts, HEAD_PAD, Z_KPE // HEAD_PAD),
                                                   _rows(ts, HEAD_PAD), _rows(ts, HEAD_PAD), _gain(HEAD_PAD), _gain(HEAD_PAD)],
                   out_specs=[_rows(ts, HP), _rows(ts, 2 * HP), _rows(ts, HEAD_PAD), _accum(HEAD_PAD), _accum(HEAD_PAD)],
                   out_shape=[jax.ShapeDtypeStruct((S, HP), BF16), jax.ShapeDtypeStruct((S, 2 * HP), BF16),
                              jax.ShapeDtypeStruct((S, HEAD_PAD), BF16), jax.ShapeDtypeStruct((SUBLANES, HEAD_PAD), F32),
                              jax.ShapeDtypeStruct((SUBLANES, HEAD_PAD), F32)],
                   compiler_params=_cparams("arbitrary"))(dqf, dkf, dv, q_raw, kv_raw, z, cs, sn, gq, gk)


def _causal(t):
    row = lax.broadcasted_iota(jnp.int32, (t, t), 0)
    col = lax.broadcasted_iota(jnp.int32, (t, t), 1)
    return col <= row


def _attn_tiles(S, chains):
    t = _tile(S, ATT_TILE, SUBLANES)
    return t, min(chains, S // t)


def _attn_fwd(qf, kf, vb, rider):
    S = qf.shape[0]
    t, nc = _attn_tiles(S, ATT_CHAINS_FWD)
    tq = nc * t
    nqt = S // tq

    def body(q_ref, k_ref, v_ref, *rest):
        (o_ref, lse_ref, m_ref, acc_ref), ride = rider.split(rest, 2, 2)
        qt = pl.program_id(1)
        ride(pl.program_id(0) * nqt + qt, N_HEADS * nqt)
        m_ref[...] = jnp.full_like(m_ref, -jnp.inf)
        acc_ref[...] = jnp.zeros_like(acc_ref)

        def chain(a, kb, vb_, masked):
            sub = slice(a * t, (a + 1) * t)
            s = lax.dot_general(q_ref[sub, :], kb, _DIMS["nt"], preferred_element_type=F32) * ATT_SCALE_LOG2
            if masked:
                s = jnp.where(_causal(t), s, -jnp.inf)
            m_old = m_ref[sub, :]
            m_new = jnp.maximum(m_old, jnp.max(s, axis=-1, keepdims=True))
            p = jnp.exp2(s - jnp.concatenate([m_new] * (t // HEAD_PAD), axis=1))
            acc_ref[sub, :] = jnp.exp2(m_old - m_new) * acc_ref[sub, :] + jnp.dot(p.astype(BF16), vb_, preferred_element_type=F32)
            m_ref[sub, :] = m_new

        def trip(j, carry):
            rows = pl.ds(pl.multiple_of(j * t, t), t)
            kb, vb_ = k_ref[rows, :], v_ref[rows, :]
            for a in range(nc):
                chain(a, kb, vb_, False)
            return carry

        lax.fori_loop(0, nc * qt, trip, 0)
        for d in range(nc):
            rows = pl.ds(pl.multiple_of((nc * qt + d) * t, t), t)
            kb, vb_ = k_ref[rows, :], v_ref[rows, :]
            for a in range(d, nc):
                chain(a, kb, vb_, a == d)
        acc = acc_ref[...]
        l = acc[:, V_HEAD:V_HEAD + 1]
        lane = lax.broadcasted_iota(jnp.int32, (1, HEAD_PAD), 1)
        o_ref[...] = jnp.where(lane < V_HEAD, acc / l, 0.0)
        lse_ref[...] = m_ref[...] + jnp.log(l) * LOG2E

    tile = pl.BlockSpec((tq, HEAD_PAD), lambda h, i: (i, h))
    full = pl.BlockSpec((S, HEAD_PAD), lambda h, i: (0, h))
    outs = _pallas(body, name="attn_fwd", grid=(N_HEADS, nqt), in_specs=[tile, full, full] + rider.in_specs(),
                   out_specs=[tile, pl.BlockSpec((None, tq, HEAD_PAD), lambda h, i: (h, i, 0))] + rider.out_specs(),
                   out_shape=[jax.ShapeDtypeStruct((S, HP), F32), jax.ShapeDtypeStruct((N_HEADS, S, HEAD_PAD), F32)]
                   + rider.out_shape(),
                   scratch_shapes=[pltpu.VMEM((tq, HEAD_PAD), F32), pltpu.VMEM((tq, HEAD_PAD), F32)] + rider.scratch(),
                   input_output_aliases=rider.aliases(3, 2),
                   compiler_params=_cparams("arbitrary", "arbitrary"))(qf, kf, vb, *rider.args())
    return outs[0], outs[1], outs[2:]


def _attn_bwd(qf, kf, vb, o, do, lse, rider):
    S = qf.shape[0]
    t, nc = _attn_tiles(S, ATT_CHAINS_BWD)
    tkv = nc * t
    nq = S // t
    nkt = S // tkv

    def body(q_ref, k_ref, v_ref, o_ref, do_ref, lse_ref, *rest):
        (dq_ref, dk_ref, dv_ref), ride = rider.split(rest, 3, 0)
        kt = pl.program_id(1)
        ride(pl.program_id(0) * nkt + kt, N_HEADS * nkt)

        @pl.when(kt == 0)
        def _():
            dq_ref[...] = jnp.zeros_like(dq_ref)

        dk_ref[...] = jnp.zeros_like(dk_ref)
        dv_ref[...] = jnp.zeros_like(dv_ref)

        def q_block(rows):
            dof = do_ref[rows, :]
            delta = jnp.sum(dof * o_ref[rows, :], axis=-1, keepdims=True)
            return q_ref[rows, :], dof.astype(BF16), lse_ref[rows, :][:, :1], delta

        def chain(b, qv, masked):
            q, dob, lse, delta = qv
            sub = slice(b * t, (b + 1) * t)
            kb, vb_ = k_ref[sub, :], v_ref[sub, :]
            s = lax.dot_general(q, kb, _DIMS["nt"], preferred_element_type=F32) * ATT_SCALE_LOG2
            p = jnp.exp2(s - lse)
            if masked:
                p = jnp.where(_causal(t), p, 0.0)
            dv_ref[sub, :] += lax.dot_general(p.astype(BF16), dob, _DIMS["tn"], preferred_element_type=F32)
            dp = lax.dot_general(dob, vb_, _DIMS["nt"], preferred_element_type=F32)
            ds = (p * (dp - delta) * ATT_SCALE).astype(BF16)
            dk_ref[sub, :] += lax.dot_general(ds, q, _DIMS["tn"], preferred_element_type=F32)
            return jnp.dot(ds, kb, preferred_element_type=F32)

        for a in range(nc):
            rows = pl.ds(pl.multiple_of((nc * kt + a) * t, t), t)
            qv = q_block(rows)
            dq_ref[rows, :] += sum(chain(b, qv, b == a) for b in range(a + 1))

        def trip(i, carry):
            rows = pl.ds(pl.multiple_of(i * t, t), t)
            qv = q_block(rows)
            dq_ref[rows, :] += sum(chain(b, qv, False) for b in range(nc))
            return carry

        lax.fori_loop(nc * (kt + 1), nq, trip, 0)

    tile = pl.BlockSpec((tkv, HEAD_PAD), lambda h, j: (j, h))
    full = pl.BlockSpec((S, HEAD_PAD), lambda h, j: (0, h))
    hd = jax.ShapeDtypeStruct((S, HP), F32)
    outs = _pallas(body, name="attn_bwd", grid=(N_HEADS, nkt),
                   in_specs=[full, tile, tile, full, full, pl.BlockSpec((None, S, HEAD_PAD), lambda h, j: (h, 0, 0))]
                   + rider.in_specs(),
                   out_specs=[full, tile, tile] + rider.out_specs(), out_shape=[hd, hd, hd] + rider.out_shape(),
                   scratch_shapes=rider.scratch(), input_output_aliases=rider.aliases(6, 3),
                   compiler_params=_cparams("arbitrary", "arbitrary"))(qf, kf, vb, o, do, lse, *rider.args())
    return outs[0], outs[1], outs[2], outs[3:]


def _shift_down(u, j, row):
    return jnp.where(row >= j, pltpu.roll(u, j, 0), 0.0)


def _shift_up(u, j, row, s):
    return jnp.where(row < s - j, pltpu.roll(u, s - j, 0), 0.0)


def _conv_cols(s, first_tile):
    return pl.BlockSpec((s, LANES), lambda cb: (0, first_tile + cb))


def _conv_fwd(z, cw8):
    S = z.shape[0]

    def body(gb_ref, gc_ref, xin_ref, w_ref, out_ref):
        row = lax.broadcasted_iota(jnp.int32, (S, LANES), 0)
        u = gc_ref[...] * xin_ref[...]
        y = w_ref[0:1, :] * u
        for j in range(1, CONV_TAPS):
            y = y + w_ref[j:j + 1, :] * _shift_down(u, j, row)
        out_ref[...] = gb_ref[...] * y

    return _pallas(body, name="conv_fwd", grid=(CONV_WIDTH // LANES,),
                   in_specs=[_conv_cols(S, Z_GB // LANES), _conv_cols(S, Z_GC // LANES), _conv_cols(S, Z_XIN // LANES),
                             pl.BlockSpec((SUBLANES, LANES), lambda cb: (0, cb))],
                   out_specs=_conv_cols(S, 0), out_shape=jax.ShapeDtypeStruct((S, CONV_WIDTH), F32),
                   compiler_params=_cparams("parallel"))(z, z, z, cw8)


def _conv_bwd(dconv, z, cw8):
    S = z.shape[0]

    def body(d_ref, gb_ref, gc_ref, xin_ref, w_ref, dgb_ref, dgc_ref, dxin_ref, dw_ref):
        row = lax.broadcasted_iota(jnp.int32, (S, LANES), 0)
        gc, xin, d = gc_ref[...], xin_ref[...], d_ref[...]
        u = gc * xin
        dy = d * gb_ref[...]
        y = w_ref[0:1, :] * u
        du = w_ref[0:1, :] * dy
        dw = [jnp.sum(dy * u, axis=0, keepdims=True)]
        for j in range(1, CONV_TAPS):
            uj = _shift_down(u, j, row)
            y = y + w_ref[j:j + 1, :] * uj
            du = du + w_ref[j:j + 1, :] * _shift_up(dy, j, row, S)
            dw.append(jnp.sum(dy * uj, axis=0, keepdims=True))
        dgb_ref[...] = (d * y).astype(BF16)
        dgc_ref[...] = (du * xin).astype(BF16)
        dxin_ref[...] = (du * gc).astype(BF16)
        tap = lax.broadcasted_iota(jnp.int32, (SUBLANES, LANES), 0)
        dw_ref[...] = sum(jnp.where(tap == j, dw[j], 0.0) for j in range(CONV_TAPS))

    col = _conv_cols(S, 0)
    sd = jax.ShapeDtypeStruct((S, CONV_WIDTH), BF16)
    return _pallas(body, name="conv_bwd", grid=(CONV_WIDTH // LANES,),
                   in_specs=[col, _conv_cols(S, Z_GB // LANES), _conv_cols(S, Z_GC // LANES), _conv_cols(S, Z_XIN // LANES),
                             pl.BlockSpec((SUBLANES, LANES), lambda cb: (0, cb))],
                   out_specs=[col, col, col, pl.BlockSpec((SUBLANES, LANES), lambda cb: (0, cb))],
                   out_shape=[sd, sd, sd, jax.ShapeDtypeStruct((SUBLANES, CONV_WIDTH), F32)],
                   compiler_params=_cparams("parallel"))(dconv, z, z, z, cw8)


def _mix_fwd(o, conv, ga, gc):
    S = o.shape[0]
    ts = _tile(S, ROW_TILE, SUBLANES)

    def body(o_ref, c_ref, ga_ref, gc_ref, out_ref):
        ov, cv = o_ref[...], c_ref[...]
        out_ref[:, :HP] = (ov * _rinv(ov, ATTN_WIDTH) * ga_ref[...]).astype(BF16)
        out_ref[:, HP:] = (cv * _rinv(cv, CONV_WIDTH) * gc_ref[...]).astype(BF16)

    return _pallas(body, name="mix_fwd", grid=(S // ts,),
                   in_specs=[_rows(ts, HP), _rows(ts, CONV_WIDTH), _gain(HP), _gain(CONV_WIDTH)],
                   out_specs=_rows(ts, HP + CONV_WIDTH), out_shape=jax.ShapeDtypeStruct((S, HP + CONV_WIDTH), BF16),
                   compiler_params=_cparams("parallel"))(o, conv, ga, gc)


def _mix_bwd(dmixed, o, conv, ga, gc):
    S = o.shape[0]
    ts = _tile(S, ROW_TILE, SUBLANES)

    def body(d_ref, o_ref, c_ref, ga_ref, gc_ref, do_ref, dc_ref, dga_ref, dgc_ref):
        dx, ca = _norm_bwd(o_ref[...], d_ref[:, :HP], ga_ref[...], ATTN_WIDTH)
        do_ref[...] = dx
        dx, cc = _norm_bwd(c_ref[...], d_ref[:, HP:], gc_ref[...], CONV_WIDTH)
        dc_ref[...] = dx
        _accumulate(dga_ref, ca)
        _accumulate(dgc_ref, cc)

    return _pallas(body, name="mix_bwd", grid=(S // ts,),
                   in_specs=[_rows(ts, HP + CONV_WIDTH), _rows(ts, HP), _rows(ts, CONV_WIDTH), _gain(HP), _gain(CONV_WIDTH)],
                   out_specs=[_rows(ts, HP), _rows(ts, CONV_WIDTH), _accum(HP), _accum(CONV_WIDTH)],
                   out_shape=[jax.ShapeDtypeStruct((S, HP), F32), jax.ShapeDtypeStruct((S, CONV_WIDTH), F32),
                              jax.ShapeDtypeStruct((SUBLANES, HP), F32), jax.ShapeDtypeStruct((SUBLANES, CONV_WIDTH), F32)],
                   compiler_params=_cparams("arbitrary"))(dmixed, o, conv, ga, gc)


def _ple_fwd(x, gl, pe, g_next):
    S, D = x.shape
    ts = _tile(S, ROW_TILE, SUBLANES)

    def body(x_ref, gl_ref, pe_ref, g_ref, out_ref, h_ref):
        y = x_ref[...] + jax.nn.sigmoid(gl_ref[...]) * pe_ref[...]
        out_ref[...] = y
        h_ref[...] = (y * _rinv(y, D) * g_ref[...]).astype(BF16)

    return _pallas(body, name="ple_fwd", grid=(S // ts,), in_specs=[_rows(ts, D)] * 3 + [_gain(D)], out_specs=[_rows(ts, D)] * 2,
                   out_shape=[jax.ShapeDtypeStruct((S, D), F32), jax.ShapeDtypeStruct((S, D), BF16)],
                   compiler_params=_cparams("parallel"))(x, gl, pe, g_next)


def _ple_bwd(dx, gl, pe):
    S, D = dx.shape
    ts = _tile(S, ROW_TILE, SUBLANES)

    def body(dx_ref, gl_ref, pe_ref, dpe_ref, dgl_ref):
        d = dx_ref[...]
        gate = jax.nn.sigmoid(gl_ref[...])
        dpe_ref[...] = (d * gate).astype(BF16)
        dgl_ref[...] = (d * pe_ref[...] * (gate * (1.0 - gate))).astype(BF16)

    sd = jax.ShapeDtypeStruct((S, D), BF16)
    return _pallas(body, name="ple_bwd", grid=(S // ts,), in_specs=[_rows(ts, D)] * 3, out_specs=[_rows(ts, D)] * 2,
                   out_shape=[sd, sd], compiler_params=_cparams("parallel"))(dx, gl, pe)


def _loss_grad(y, target):
    S, D = y.shape
    ts = _tile(S, ROW_TILE, SUBLANES)

    def body(y_ref, t_ref, dy_ref, sq_ref):
        e = y_ref[...] - t_ref[...]
        dy_ref[...] = e / D

        @pl.when(pl.program_id(0) == 0)
        def _():
            sq_ref[...] = jnp.zeros_like(sq_ref)

        sq_ref[...] += jnp.broadcast_to(jnp.sum(jnp.sum(e * e, axis=1, keepdims=True), axis=0, keepdims=True), sq_ref.shape)

    return _pallas(body, name="loss_grad", grid=(S // ts,), in_specs=[_rows(ts, D)] * 2,
                   out_specs=[_rows(ts, D), _accum(LANES)],
                   out_shape=[jax.ShapeDtypeStruct((S, D), F32), jax.ShapeDtypeStruct((SUBLANES, LANES), F32)],
                   compiler_params=_cparams("arbitrary"))(y, target)


def _adamw(w, g, m, v, name):
    L, R, C = w.shape
    tr = _tile(R, ROW_TILE, SUBLANES)

    def body(w_ref, g_ref, m_ref, v_ref, d_ref, m2_ref, v2_ref):
        gv = g_ref[...]
        m2 = ADAM_B1 * m_ref[...] + (1.0 - ADAM_B1) * gv
        v2 = ADAM_B2 * v_ref[...] + (1.0 - ADAM_B2) * (gv * gv)
        m_hat = m2 / (1.0 - ADAM_B1 ** ADAM_STEP)
        v_hat = v2 / (1.0 - ADAM_B2 ** ADAM_STEP)
        d_ref[...] = -ADAM_LR * (m_hat / (jnp.sqrt(v_hat) + ADAM_EPS) + ADAM_WD * w_ref[...])
        m2_ref[...] = m2
        v2_ref[...] = v2

    sd = jax.ShapeDtypeStruct((L, R, C), F32)
    spec = pl.BlockSpec((None, tr, C), lambda l, i: (l, i, 0))
    return _pallas(body, name=name, grid=(L, R // tr), in_specs=[spec] * 4, out_specs=[spec] * 3, out_shape=[sd, sd, sd],
                   compiler_params=_cparams("parallel", "parallel"))(w, g, m, v)


def _place():
    return lax.axis_index("x"), lax.axis_index("y"), lax.axis_index("c")


def _other_chips(x, y):
    return [(1 - x, y), (x, 1 - y), (1 - x, 1 - y)]


HBM = pl.BlockSpec(memory_space=pl.ANY)
VMEM_SPEC = pl.BlockSpec(memory_space=pltpu.VMEM)


def _remote_copy(send_sems, recv_sems, k, src, dst, to):
    return pltpu.make_async_remote_copy(src_ref=src, dst_ref=dst, send_sem=send_sems.at[k], recv_sem=recv_sems.at[k],
                                        device_id=to, device_id_type=MESH)


def _comm_call(body, name, arrays, out_shapes, n_remote, in_place=False):
    scratch = [pltpu.SemaphoreType.DMA((n_remote,)), pltpu.SemaphoreType.DMA((n_remote,))]
    aliases = {i: i for i in range(len(arrays))} if in_place else {}
    return _pallas(body, name=name, in_specs=[HBM] * len(arrays), out_specs=[HBM] * len(out_shapes), out_shape=out_shapes,
                   scratch_shapes=scratch, input_output_aliases=aliases,
                   compiler_params=pltpu.CompilerParams(has_side_effects=True))(*arrays)


def _shard_slab(w, chip_arr, name):
    L, R, C = w.shape
    tr = _tile(R, ROW_TILE, 16)

    def body(j_ref, w_ref, out_ref):
        out_ref[...] = w_ref[...].astype(BF16)

    grid_spec = pltpu.PrefetchScalarGridSpec(
        num_scalar_prefetch=1, grid=(L, R // tr),
        in_specs=[pl.BlockSpec((None, tr, C), lambda l, i, j_ref: (l, i, 0))],
        out_specs=pl.BlockSpec((None, None, tr, C), lambda l, i, j_ref: (j_ref[0], l, i, 0)))
    return _pallas(body, name=name, grid_spec=grid_spec, out_shape=jax.ShapeDtypeStruct((N_CHIPS, L, R, C), BF16),
                   compiler_params=_cparams("parallel", "parallel"))(chip_arr, w)


class Rider:
    def __init__(self, reads, inplace, n_sems, stages):
        self.reads, self.inplace, self.n_sems, self.stages = list(reads), list(inplace), n_sems, stages

    def args(self):
        return self.reads + self.inplace

    def in_specs(self):
        return [HBM] * len(self.args())

    def out_specs(self):
        return [HBM] * len(self.inplace)

    def out_shape(self):
        return [jax.ShapeDtypeStruct(a.shape, a.dtype) for a in self.inplace]

    def scratch(self):
        return [pltpu.SemaphoreType.DMA((self.n_sems,)), pltpu.SemaphoreType.DMA((self.n_sems,))]

    def aliases(self, n_host_in, n_host_out):
        return {n_host_in + len(self.reads) + j: n_host_out + j for j in range(len(self.inplace))}

    def split(self, rest, n_host_out, n_host_scratch):
        n_r, n_io = len(self.reads), len(self.inplace)
        reads = rest[:n_r]
        host_out = rest[n_r + n_io:n_r + n_io + n_host_out]
        outs = rest[n_r + n_io + n_host_out:n_r + 2 * n_io + n_host_out]
        host_scratch = rest[n_r + 2 * n_io + n_host_out:n_r + 2 * n_io + n_host_out + n_host_scratch]
        sems = rest[n_r + 2 * n_io + n_host_out + n_host_scratch:]

        def ride(step, n_steps):
            at = [0, n_steps - 1] if len(self.stages) == 2 else [0, (3 * n_steps) // 4, n_steps - 1]
            for s, stage in zip(at, self.stages):
                pl.when(step == s)(functools.partial(stage, reads, outs, *sems))

        return tuple(host_out) + tuple(host_scratch), ride

    def run(self, name):
        n_r, n_io = len(self.reads), len(self.inplace)

        def body(*refs):
            for stage in self.stages:
                stage(refs[:n_r], refs[n_r + n_io:n_r + 2 * n_io], *refs[n_r + 2 * n_io:])

        return _pallas(body, name=name, in_specs=self.in_specs(), out_specs=self.out_specs(), out_shape=self.out_shape(),
                       scratch_shapes=self.scratch(), input_output_aliases=self.aliases(0, 0),
                       compiler_params=pltpu.CompilerParams(has_side_effects=True))(*self.args())


def _allgather_rider(slabs, items):
    def stage(which, reads, outs, send_sems, recv_sems):
        x, y, c = _place()
        me, sibling = (x, y, c), (x, y, 1 - c)
        copy = functools.partial(_remote_copy, send_sems, recv_sems)
        for n, (j, l) in enumerate(items):
            ref = outs[j]
            h = ref.shape[2] // 2
            mine, theirs = pl.ds(c * h, h), pl.ds((1 - c) * h, h)
            own = ref.at[2 * x + y, l, mine]
            for k, (cx, cy) in enumerate(_other_chips(x, y)):
                arrived = ref.at[2 * cx + cy, l, mine]
                if which == 0:
                    copy(6 * n + k, own, own, (cx, cy, c)).start()
                elif which == 1:
                    copy(6 * n + k, arrived, arrived, me).wait_recv()
                    copy(6 * n + 3 + k, arrived, arrived, sibling).start()
                else:
                    passed = ref.at[2 * cx + cy, l, theirs]
                    copy(6 * n + 3 + k, passed, passed, me).wait_recv()
                    copy(6 * n + k, own, own, me).wait_send()
                    copy(6 * n + 3 + k, arrived, arrived, me).wait_send()

    return Rider([], slabs, 6 * len(items), [functools.partial(stage, w) for w in range(3)])


def _exchange_rider(grads, landing, items):
    def stage(start, reads, outs, send_sems, recv_sems):
        x, y, c = _place()
        for n, (i, j, l) in enumerate(items):
            g_ref, r_ref = reads[i], outs[j]
            h = g_ref.shape[1] // 2
            for k in range(1, N_DEV):
                px, py, pc = (1 - x if k & 4 else x, 1 - y if k & 2 else y, 1 - c if k & 1 else c)
                cp = _remote_copy(send_sems, recv_sems, 7 * n + k - 1, g_ref.at[2 * px + py, pl.ds(pc * h, h)],
                                  r_ref.at[l, k - 1], (px, py, pc))
                if start:
                    cp.start()
                else:
                    cp.wait()

    return Rider(grads, landing, 7 * len(items), [functools.partial(stage, True), functools.partial(stage, False)])


def _reduce_partials(g, r, chip_arr, c_arr, name):
    L, n, R, C = g.shape
    H = R // 2
    th = _tile(H, ROW_TILE, 16)
    nb = H // th

    def body(j_ref, c_ref, g_ref, r_ref, out_ref):
        s = g_ref[...].astype(F32)
        for k in range(N_DEV - 1):
            s = s + r_ref[k].astype(F32)
        out_ref[...] = s

    grid_spec = pltpu.PrefetchScalarGridSpec(
        num_scalar_prefetch=2, grid=(L, nb),
        in_specs=[pl.BlockSpec((None, None, th, C), lambda l, i, j_ref, c_ref: (l, j_ref[0], c_ref[0] * nb + i, 0)),
                  pl.BlockSpec((None, N_DEV - 1, th, C), lambda l, i, j_ref, c_ref: (l, 0, i, 0))],
        out_specs=pl.BlockSpec((None, th, C), lambda l, i, j_ref, c_ref: (l, c_ref[0] * nb + i, 0)))
    return _pallas(body, name=name, grid_spec=grid_spec, out_shape=jax.ShapeDtypeStruct((L, R, C), F32),
                   compiler_params=_cparams("parallel", "parallel"))(chip_arr, c_arr, g, r)


def _join_halves(fs):
    n = len(fs)

    def body(*refs):
        out_refs, (send_sems, recv_sems) = refs[n:2 * n], refs[2 * n:]
        x, y, c = _place()
        cps = []
        for i, out_ref in enumerate(out_refs):
            h = out_ref.shape[1] // 2
            mine = out_ref.at[:, pl.ds(c * h, h)]
            cps.append(_remote_copy(send_sems, recv_sems, i, mine, mine, (x, y, 1 - c)))
            cps[-1].start()
        for cp in cps:
            cp.wait()

    return _comm_call(body, "grad_join_halves", fs, [jax.ShapeDtypeStruct(f.shape, f.dtype) for f in fs], n, in_place=True)


def _allgather_small(v, name):
    R, C = v.shape

    def body(v_ref, out_ref, send_sems, recv_sems):
        x, y, c = _place()
        me = 4 * x + 2 * y + c
        out_ref[me] = v_ref[...]
        cps = []
        for k in range(1, N_DEV):
            peer = (1 - x if k & 4 else x, 1 - y if k & 2 else y, 1 - c if k & 1 else c)
            cps.append(pltpu.make_async_remote_copy(src_ref=v_ref, dst_ref=out_ref.at[me], send_sem=send_sems.at[k - 1],
                                                    recv_sem=recv_sems.at[k - 1], device_id=peer, device_id_type=MESH))
        for cp in cps:
            cp.start()
        for cp in cps:
            cp.wait()

    return _pallas(body, name=name, in_specs=[VMEM_SPEC], out_specs=VMEM_SPEC,
                   out_shape=jax.ShapeDtypeStruct((N_DEV, R, C), v.dtype),
                   scratch_shapes=[pltpu.SemaphoreType.DMA((N_DEV - 1,)), pltpu.SemaphoreType.DMA((N_DEV - 1,))],
                   compiler_params=pltpu.CompilerParams(has_side_effects=True))(v)


def _sum_devices(g):
    n, R, C = g.shape

    def body(g_ref, out_ref):
        s = g_ref[0]
        for d in range(1, n):
            s = s + g_ref[d]
        out_ref[...] = s

    return _pallas(body, name="sum_devices", in_specs=[VMEM_SPEC], out_specs=VMEM_SPEC,
                   out_shape=jax.ShapeDtypeStruct((R, C), g.dtype))(g)


def _pad_heads(a, width):
    lead = a.shape[:-1]
    a = a.reshape(lead + (N_HEADS, width))
    a = jnp.pad(a, [(0, 0)] * len(lead) + [(0, 0), (0, HEAD_PAD - width)])
    return a.reshape(lead + (HP,))


def _unpad_heads(a, width):
    lead = a.shape[:-1]
    return a.reshape(lead + (N_HEADS, HEAD_PAD))[..., :width].reshape(lead + (N_HEADS * width,))


def _pre_weights(slabs, l):
    cols = lambda name: slabs[name][:, l].transpose(1, 0, 2).reshape(slabs[name].shape[2], -1)
    w_in = cols("w_in")
    kpe = jnp.pad(w_in[:, 640:672], ((0, 0), (ROPE_LO, HEAD_PAD - ROPE_HI)))
    w_ukv = cols("w_ukv").reshape(KV_LORA, N_HEADS, QK_NOPE + V_HEAD)
    return dict(
        w_in=jnp.concatenate([w_in[:, :640], kpe, w_in[:, 672:]], axis=1),
        w_uq=_pad_heads(cols("w_uq"), QK_HEAD),
        w_ukv=jnp.concatenate([_pad_heads(w_ukv[..., :QK_NOPE].reshape(KV_LORA, -1), QK_NOPE),
                               _pad_heads(w_ukv[..., QK_NOPE:].reshape(KV_LORA, -1), V_HEAD)], axis=1),
    )


def _post_weights(slabs, l, w_o_padded=None):
    if w_o_padded is None:
        w_o = slabs["w_o"][:, l].reshape(D_MODEL, D_MODEL)
        w_o_padded = jnp.concatenate([_pad_heads(w_o[:ATTN_WIDTH].T, V_HEAD).T, w_o[ATTN_WIDTH:]], axis=0)
    return dict(
        w_o=w_o_padded,
        w_up=Gathered(slabs["w_up"], l, 1), w_down=Gathered(slabs["w_down"], l, 0),
        w_ple_gate=slabs["w_ple_gate"][:, l].reshape(D_MODEL, D_MODEL), w_ple=Gathered(slabs["w_ple"], l, 1),
    )


def _gains(small, l):
    row = lambda name: small[name][l].reshape(1, -1)
    headrow = lambda a, b: jnp.pad(jnp.concatenate([small[a][l], small[b][l]]), (0, HEAD_PAD - QK_HEAD)).reshape(1, HEAD_PAD)
    return dict(
        g_mix=row("g_mix"), g_q_lat=row("g_q_lat"), g_kv_lat=row("g_kv_lat"), g_mlp=row("g_mlp"), g_ple=row("g_ple"),
        g_out_conv=row("g_out_conv"), g_out_attn=_pad_heads(small["g_out_attn"][l], V_HEAD).reshape(1, HP),
        gq=headrow("g_qn_nope", "g_qn_rope"), gk=headrow("g_kn_nope", "g_kn_rope"),
        cw8=jnp.pad(small["conv_w"][l], ((0, SUBLANES - CONV_TAPS), (0, 0))),
    )


def _w_o_shards(dw_o):
    return jnp.concatenate([_unpad_heads(dw_o[:HP].T, V_HEAD).T, dw_o[HP:]], axis=0).reshape(N_CHIPS, -1, D_MODEL)


def _unpad_grads(gp):
    dw_in = gp["w_in"]
    dw_ukv = gp["w_ukv"]
    k_part = dw_ukv[:, :HP].reshape(KV_LORA, N_HEADS, HEAD_PAD)[..., :QK_NOPE]
    v_part = dw_ukv[:, HP:].reshape(KV_LORA, N_HEADS, HEAD_PAD)[..., :V_HEAD]
    first = lambda name: gp[name][0]
    col_shards = lambda a: a.reshape(a.shape[0], N_CHIPS, -1).transpose(1, 0, 2)
    return dict(
        w_in=col_shards(jnp.concatenate([dw_in[:, :640], dw_in[:, Z_KPE + ROPE_LO:Z_KPE + ROPE_HI], dw_in[:, Z_GB:]], axis=1)),
        w_uq=col_shards(_unpad_heads(gp["w_uq"], QK_HEAD)),
        w_ukv=col_shards(jnp.concatenate([k_part, v_part], axis=-1).reshape(KV_LORA, -1)),
        w_o=gp["w_o"], w_up=gp["w_up"], w_down=gp["w_down"], w_ple_gate=gp["w_ple_gate"], w_ple=gp["w_ple"],
        g_mix=first("g_mix"), g_q_lat=first("g_q_lat"), g_kv_lat=first("g_kv_lat"), g_mlp=first("g_mlp"),
        g_ple=first("g_ple"), g_out_conv=first("g_out_conv"), g_out_attn=_unpad_heads(first("g_out_attn"), V_HEAD),
        g_qn_nope=gp["gq"][0, :QK_NOPE], g_qn_rope=gp["gq"][0, QK_NOPE:QK_HEAD],
        g_kn_nope=gp["gk"][0, :QK_NOPE], g_kn_rope=gp["gk"][0, QK_NOPE:QK_HEAD],
        conv_w=gp["cw8"][:CONV_TAPS],
    )


def _rope_tables(positions):
    inv_freq = 1.0 / (ROPE_THETA ** (jnp.arange(0, QK_ROPE, 2, dtype=F32) / QK_ROPE))
    ang = positions.astype(F32)[:, None] * inv_freq
    cos, sin = jnp.cos(ang), jnp.sin(ang)
    pad = lambda t, v: jnp.pad(jnp.concatenate([t, t], axis=1), ((0, 0), (ROPE_LO, HEAD_PAD - ROPE_HI)), constant_values=v)
    return pad(cos, 1.0), pad(sin, 0.0)


def _layer_fwd(x0, h, p_l, W, cs, sn, attend, g_next):
    z = _mm(h, W["w_in"], name="mm_in")
    qln, kvln = _lat_fwd(z, W["g_q_lat"], W["g_kv_lat"])
    q_raw = _mm(qln, W["w_uq"], name="mm_uq")
    kv_raw = _mm(kvln, W["w_ukv"], name="mm_ukv")
    qf, kf, vb = _qk_fwd(q_raw, kv_raw, z, cs, sn, W["gq"], W["gk"])
    o, lse = attend(qf, kf, vb)
    conv = _conv_fwd(z, W["cw8"])
    mixed = _mix_fwd(o, conv, W["g_out_attn"], W["g_out_conv"])
    x1, h2 = _mm(mixed, W["w_o"], res=x0, gain=W["g_mlp"], epi="rms", name="mm_o")
    a, f = _mm(h2, W["w_up"], epi="relu2", name="mm_up")
    x2, h3 = _mm(f, W["w_down"], res=x1, gain=W["g_ple"], epi="rms", name="mm_down")
    gl = _mm(h3, W["w_ple_gate"], name="mm_ple_gate")
    pe = _mm(p_l, W["w_ple"], name="mm_ple")
    x3, h_next = _ple_fwd(x2, gl, pe, g_next)
    saved = dict(x0=x0, h=h, z=z, qln=qln, kvln=kvln, q_raw=q_raw, kv_raw=kv_raw, qf=qf, kf=kf, vb=vb, o=o, lse=lse,
                 conv=conv, mixed=mixed, x1=x1, h2=h2, a=a, f=f, x2=x2, h3=h3, gl=gl, pe=pe)
    return x3, h_next, saved


def _layer_bwd(dx3, p_l, W, cs, sn, sv, attend_bwd):
    g = {}
    dpe, dgl = _ple_bwd(dx3, sv["gl"], sv["pe"])
    g["w_ple"] = _mm(p_l, dpe, mode="tn", out_dtype=BF16, shard_out=1, name="mm_dw_ple")
    g["w_ple_gate"] = _mm(sv["h3"], dgl, mode="tn", out_dtype=BF16, shard_out=0, name="mm_dw_ple_gate")
    dx2, dx2b, g["g_ple"] = _mm(dgl, W["w_ple_gate"], mode="nt", res=dx3, aux=sv["x2"], gain=W["g_ple"], epi="rms_bwd",
                                name="mm_dh3")
    da = _mm(dx2b, W["w_down"], mode="nt", aux=sv["a"], epi="drelu2", out_dtype=BF16, name="mm_da")
    g["w_down"] = _mm(sv["f"], dx2b, mode="tn", out_dtype=BF16, shard_out=0, name="mm_dw_down")
    g["w_up"] = _mm(sv["h2"], da, mode="tn", out_dtype=BF16, shard_out=1, name="mm_dw_up")
    dx1, dx1b, g["g_mlp"] = _mm(da, W["w_up"], mode="nt", res=dx2, aux=sv["x1"], gain=W["g_mlp"], epi="rms_bwd",
                                name="mm_dh2")
    dmixed = _mm(dx1b, W["w_o"], mode="nt", name="mm_dmixed")
    g["w_o"] = _mm(sv["mixed"], dx1b, mode="tn", out_dtype=BF16, name="mm_dw_o")
    do, dconv, g["g_out_attn"], g["g_out_conv"] = _mix_bwd(dmixed, sv["o"], sv["conv"], W["g_out_attn"], W["g_out_conv"])
    dgb, dgc, dxin, g["cw8"] = _conv_bwd(dconv, sv["z"], W["cw8"])
    dqf, dkf, dv = attend_bwd(g, sv["qf"], sv["kf"], sv["vb"], sv["o"], do, sv["lse"])
    dq_raw, dkv_raw, dkpe, g["gq"], g["gk"] = _qk_bwd(dqf, dkf, dv, sv["q_raw"], sv["kv_raw"], sv["z"], cs, sn, W["gq"], W["gk"])
    g["w_uq"] = _mm(sv["qln"], dq_raw, mode="tn", out_dtype=BF16, name="mm_dw_uq")
    dqln = _mm(dq_raw, W["w_uq"], mode="nt", name="mm_dqln")
    g["w_ukv"] = _mm(sv["kvln"], dkv_raw, mode="tn", out_dtype=BF16, name="mm_dw_ukv")
    dkvln = _mm(dkv_raw, W["w_ukv"], mode="nt", name="mm_dkvln")
    dlat, g["g_q_lat"], g["g_kv_lat"] = _lat_bwd(dqln, dkvln, sv["z"], W["g_q_lat"], W["g_kv_lat"])
    dz = jnp.concatenate([dlat, dkpe, dgb, dgc, dxin], axis=1)
    g["w_in"] = _mm(sv["h"], dz, mode="tn", out_dtype=BF16, name="mm_dw_in")
    dx0, _, g["g_mix"] = _mm(dz, W["w_in"], mode="nt", res=dx1, aux=sv["x0"], gain=W["g_mix"], epi="rms_bwd", name="mm_dh")
    return dx0, g


def _local_step(x, p, positions, target, slabs, small):
    depth = p.shape[0]
    cs, sn = _rope_tables(positions)
    slabs = dict(slabs)

    def gather(items, host):
        touched = [n for n in PRE + POST if any(n == name for name, _ in items)]
        rider = _allgather_rider([slabs[n] for n in touched], [(touched.index(name), l) for name, l in items])
        if host is None:
            out, new = (), rider.run("allgather_first")
        else:
            *out, new = host(rider)
        slabs.update(zip(touched, new))
        return out

    gather([(name, 0) for name in PRE], None)
    Ws, saved = [], []
    gains = [_gains(small, l) for l in range(depth)]
    h = _rms_fwd(x, gains[0]["g_mix"], "rms_mix")
    for l in range(depth):
        W = dict(gains[l], **_pre_weights(slabs, l))

        def attend(qf, kf, vb, W=W, l=l):
            items = [(name, l) for name in POST] + ([(name, l + 1) for name in PRE] if l + 1 < depth else [])
            o, lse = gather(items, functools.partial(_attn_fwd, qf, kf, vb))
            W.update(_post_weights(slabs, l))
            return o, lse

        g_next = gains[l + 1]["g_mix"] if l + 1 < depth else jnp.ones_like(gains[l]["g_mix"])
        x, h, sv = _layer_fwd(x, h, p[l], W, cs, sn, attend, g_next)
        Ws.append(W)
        saved.append(sv)
    dx, sq = _loss_grad(x, target)

    landing = {name: lax.empty((depth, N_DEV - 1, slabs[name].shape[2] // 2, slabs[name].shape[3]), BF16) for name in PRE + POST}

    def exchange(sends, host):
        touched = [n for n in PRE + POST if any(n == name for name, _, _ in sends)]
        rider = _exchange_rider([g for _, _, g in sends], [landing[n] for n in touched],
                                [(i, touched.index(name), l) for i, (name, l, _) in enumerate(sends)])
        if host is None:
            out, new = (), rider.run("grad_exchange_last")
        else:
            *out, new = host(rider)
        landing.update(zip(touched, new))
        return out

    grads = [None] * depth
    for l in reversed(range(depth)):
        W = dict(Ws[l], **_post_weights(slabs, l, Ws[l]["w_o"]))

        def attend_bwd(g, qf, kf, vb, o, do, lse, l=l):
            g["w_o"] = _w_o_shards(g["w_o"])
            sends = [(name, l, g[name]) for name in POST]
            if l + 1 < depth:
                sends += [(name, l + 1, grads[l + 1][name]) for name in PRE]
            return exchange(sends, functools.partial(_attn_bwd, qf, kf, vb, o, do, lse))

        dx, gp = _layer_bwd(dx, p[l], W, cs, sn, saved[l], attend_bwd)
        grads[l] = _unpad_grads(gp)
    exchange([(name, 0, grads[0][name]) for name in PRE], None)
    return sq, dx, grads, landing


def _small_rows(extra):
    return sum(_pad128(n) for _, n in SMALL + extra) // LANES


def _pack_small(vals, depth, extra, tail_rows=0):
    parts = []
    for l in range(depth):
        for name, n in SMALL + extra:
            parts.append(jnp.pad(vals[name][l].reshape(-1), (0, _pad128(n) - n)))
    flat = jnp.concatenate(parts).reshape(-1, LANES)
    rows = flat.shape[0] + tail_rows
    return jnp.pad(flat, ((0, -(-rows // SUBLANES) * SUBLANES - flat.shape[0]), (0, 0)))


def _unpack_small(flat, depth, extra):
    per_layer = _small_rows(extra) * LANES
    body = flat.reshape(-1)[:depth * per_layer].reshape(depth, per_layer)
    out, off = {}, 0
    for name, n in SMALL + extra:
        out[name] = body[:, off:off + n]
        off += _pad128(n)
    return out


def kernel(x, p, positions, g_mix, w_in, g_q_lat, w_uq, g_kv_lat, w_ukv, g_qn_nope, g_qn_rope, g_kn_nope, g_kn_rope, conv_w, g_out_attn, g_out_conv, w_o, g_mlp, w_up, w_down, g_ple, w_ple_gate, w_ple, loss_target, m_g_mix, m_w_in, m_g_q_lat, m_w_uq, m_g_kv_lat, m_w_ukv, m_g_qn_nope, m_g_qn_rope, m_g_kn_nope, m_g_kn_rope, m_conv_w, m_g_out_attn, m_g_out_conv, m_w_o, m_g_mlp, m_w_up, m_w_down, m_g_ple, m_w_ple_gate, m_w_ple, v_g_mix, v_w_in, v_g_q_lat, v_w_uq, v_g_kv_lat, v_w_ukv, v_g_qn_nope, v_g_qn_rope, v_g_kn_nope, v_g_kn_rope, v_conv_w, v_g_out_attn, v_g_out_conv, v_w_o, v_g_mlp, v_w_up, v_w_down, v_g_ple, v_w_ple_gate, v_w_ple):
    w = dict(g_mix=g_mix, w_in=w_in, g_q_lat=g_q_lat, w_uq=w_uq, g_kv_lat=g_kv_lat, w_ukv=w_ukv, g_qn_nope=g_qn_nope,
             g_qn_rope=g_qn_rope, g_kn_nope=g_kn_nope, g_kn_rope=g_kn_rope, conv_w=conv_w, g_out_attn=g_out_attn,
             g_out_conv=g_out_conv, w_o=w_o, g_mlp=g_mlp, w_up=w_up, w_down=w_down, g_ple=g_ple, w_ple_gate=w_ple_gate,
             w_ple=w_ple)
    m = dict(g_mix=m_g_mix, w_in=m_w_in, g_q_lat=m_g_q_lat, w_uq=m_w_uq, g_kv_lat=m_g_kv_lat, w_ukv=m_w_ukv,
             g_qn_nope=m_g_qn_nope, g_qn_rope=m_g_qn_rope, g_kn_nope=m_g_kn_nope, g_kn_rope=m_g_kn_rope, conv_w=m_conv_w,
             g_out_attn=m_g_out_attn, g_out_conv=m_g_out_conv, w_o=m_w_o, g_mlp=m_g_mlp, w_up=m_w_up, w_down=m_w_down,
             g_ple=m_g_ple, w_ple_gate=m_w_ple_gate, w_ple=m_w_ple)
    v = dict(g_mix=v_g_mix, w_in=v_w_in, g_q_lat=v_g_q_lat, w_uq=v_w_uq, g_kv_lat=v_g_kv_lat, w_ukv=v_w_ukv,
             g_qn_nope=v_g_qn_nope, g_qn_rope=v_g_qn_rope, g_kn_nope=v_g_kn_nope, g_kn_rope=v_g_kn_rope, conv_w=v_conv_w,
             g_out_attn=v_g_out_attn, g_out_conv=v_g_out_conv, w_o=v_w_o, g_mlp=v_g_mlp, w_up=v_w_up, w_down=v_w_down,
             g_ple=v_g_ple, w_ple_gate=v_w_ple_gate, w_ple=v_w_ple)
    depth = p.shape[0]
    ax, ay, ac = _place()
    chip = 2 * ax + ay
    c_arr = jnp.reshape(ac, (1,)).astype(jnp.int32)
    chip_arr = jnp.reshape(chip, (1,)).astype(jnp.int32)
    conv_shard =("conv_w", CONV_TAPS * CONV_WIDTH // N_CHIPS)
    conv_full = ("conv_w", CONV_TAPS * CONV_WIDTH)

    names = PRE + POST
    slabs = {name: _shard_slab(w[name], chip_arr, "shard_slab_" + name) for name in names}
    conv_rows = -(-depth * CONV_TAPS // SUBLANES) * SUBLANES
    conv_all = _allgather_small(jnp.pad(conv_w.reshape(depth * CONV_TAPS, LANES), ((0, conv_rows - depth * CONV_TAPS), (0, 0))),
                                "allgather_conv_w")
    conv_cat = jnp.concatenate([conv_all[2 * j, :depth * CONV_TAPS] for j in range(N_CHIPS)], axis=1)
    small = {name: w[name] for name, _ in SMALL}
    small["conv_w"] = conv_cat.reshape(depth, CONV_TAPS, CONV_WIDTH)

    sq, grad_x, grads, landing = _local_step(x[0], p[:, 0], positions[0], loss_target[0], slabs, small)

    halves = [_reduce_partials(jnp.stack([grads[l][name] for l in range(depth)]), landing[name], chip_arr, c_arr,
                               "grad_reduce_" + name) for name in names]
    g_out = dict(zip(names, _join_halves(halves)))

    stacked = {name: jnp.stack([grads[l][name] for l in range(depth)]) for name, _ in SMALL + (conv_full,)}
    loss_row = jnp.pad(sq[:1] * (0.5 / D_MODEL), ((0, 0), (0, 0)))
    packed = _pack_small(stacked, depth, (conv_full,), tail_rows=1)
    n_rows = depth * _small_rows((conv_full,))
    packed = packed.at[n_rows].set(loss_row[0])
    summed = _sum_devices(_allgather_small(packed, "allgather_small_grads"))
    loss = summed[n_rows, 0]
    g_small = _unpack_small(summed, depth, (conv_full,))
    for name, n in SMALL:
        g_out[name] = g_small[name]
    g_conv = g_small["conv_w"].reshape(depth, CONV_TAPS, CONV_WIDTH)
    g_out["conv_w"] = lax.dynamic_slice_in_dim(g_conv, chip * LANES, LANES, axis=2)

    delta, new_m, new_v = {}, {}, {}
    for name in names:
        delta[name], new_m[name], new_v[name] = _adamw(w[name], g_out[name], m[name], v[name], "adamw_" + name)
    pack = lambda vals: _pack_small({k: a.reshape(depth, -1) for k, a in vals.items()}, depth, (conv_shard,))[None]
    with_conv = lambda vals: {**{name: vals[name] for name, _ in SMALL}, "conv_w": vals["conv_w"]}
    d, m2, v2 = _adamw(pack(with_conv(w)), pack(with_conv(g_out)), pack(with_conv(m)), pack(with_conv(v)), "adamw_small")
    for res, packed_res in ((delta, d), (new_m, m2), (new_v, v2)):
        un = _unpack_small(packed_res[0], depth, (conv_shard,))
        for name, _ in SMALL + (conv_shard,):
            res[name] = un[name].reshape(w[name].shape)

    return (loss, grad_x[None], *[g_out[n] for n in WEIGHT_ORDER], *[delta[n] for n in WEIGHT_ORDER],
            *[new_m[n] for n in WEIGHT_ORDER], *[new_v[n] for n in WEIGHT_ORDER])
```

```python
import functools

import jax
import jax.numpy as jnp
from jax import lax
from jax.experimental import pallas as pl
from jax.experimental.pallas import tpu as pltpu

F32 = jnp.float32
BF16 = jnp.bfloat16
MESH = pl.DeviceIdType.MESH

D_MODEL = 1024
N_HEADS = 8
QK_NOPE = 64
QK_ROPE = 32
QK_HEAD = QK_NOPE + QK_ROPE
V_HEAD = 64
Q_LORA = 384
KV_LORA = 256
ATTN_WIDTH = N_HEADS * V_HEAD
CONV_WIDTH = 512
CONV_TAPS = 3
D_FF = 4096
PLE_DIM = 256
ROPE_THETA = 10000.0
EPS = 1e-6
ATT_SCALE = QK_HEAD ** -0.5
LOG2E = 1.4426950408889634
ATT_SCALE_LOG2 = ATT_SCALE * LOG2E

ADAM_LR = 0.001
ADAM_B1 = 0.9
ADAM_B2 = 0.999
ADAM_EPS = 1e-08
ADAM_WD = 0.01
ADAM_STEP = 10

LANES = 128
SUBLANES = 8
HEAD_PAD = LANES
HP = N_HEADS * HEAD_PAD
ROPE_LO = QK_NOPE
ROPE_MID = QK_NOPE + QK_ROPE // 2
ROPE_HI = QK_NOPE + QK_ROPE
VMEM_LIMIT = 56 * 1024 * 1024

Z_Q, Z_KV, Z_KPE, Z_GB, Z_GC, Z_XIN = 0, 384, 640, 768, 1280, 1792
Z_COLS = 2304
Z_LAT = Z_KPE

ROW_TILE = 512
ATT_TILE = 256
ATT_CHAINS_FWD, ATT_CHAINS_BWD = 8, 4
MM_TM, MM_TN, MM_TK = 1024, 1024, 4096
MM_TM_MIN, MM_TK_MIN = 256, 512
MM_VMEM_BUDGET = 40 * 1024 * 1024

N_CHIPS = 4
N_DEV = 8

PRE = ("w_in", "w_uq", "w_ukv")
POST = ("w_o", "w_up", "w_down", "w_ple_gate", "w_ple")
SMALL = (
    ("g_mix", 1024), ("g_q_lat", 384), ("g_kv_lat", 256), ("g_qn_nope", 64), ("g_qn_rope", 32), ("g_kn_nope", 64),
    ("g_kn_rope", 32), ("g_out_attn", 512), ("g_out_conv", 512), ("g_mlp", 1024), ("g_ple", 1024),
)
WEIGHT_ORDER = ("g_mix", "w_in", "g_q_lat", "w_uq", "g_kv_lat", "w_ukv", "g_qn_nope", "g_qn_rope", "g_kn_nope",
                "g_kn_rope", "conv_w", "g_out_attn", "g_out_conv", "w_o", "g_mlp", "w_up", "w_down", "g_ple",
                "w_ple_gate", "w_ple")


def _pallas(body, **kw):
    return pl.pallas_call(body, **kw)


def _cparams(*sem):
    return pltpu.CompilerParams(dimension_semantics=sem, vmem_limit_bytes=VMEM_LIMIT)


def _tile(dim, pref, unit=LANES):
    if dim <= pref:
        return dim
    t = (pref // unit) * unit
    while t > unit and dim % t:
        t -= unit
    assert dim % t == 0, (dim, pref)
    return t


def _pad128(n):
    return -(-n // LANES) * LANES


_DIMS = {"nn": (((1,), (0,)), ((), ())), "nt": (((1,), (1,)), ((), ())), "tn": (((0,), (0,)), ((), ()))}


def _mm(a, b, *, mode="nn", res=None, aux=None, gain=None, epi=None, out_dtype=F32, shard_out=None, name):
    whole_rows = epi in ("rms", "rms_bwd")
    if mode == "nn":
        (M, K), (K2, N) = a.shape, b.shape
    elif mode == "nt":
        (M, K), (N, K2) = a.shape, b.shape
    else:
        (K, M), (K2, N) = a.shape, b.shape
    assert K == K2, (a.shape, b.shape, mode)
    n_lim = k_lim = None
    m_lim = M // N_CHIPS if shard_out == 0 else None
    if shard_out == 1:
        n_lim = N // N_CHIPS
    extra = [t for t in (res, aux) if t is not None]
    out_bytes = 6 if epi in ("relu2", "rms", "rms_bwd") else jnp.dtype(out_dtype).itemsize
    if whole_rows:
        assert (n_lim or N) == N and N <= MM_TN and shard_out is None and gain is not None
        m_lim = min(M, MM_TM // 2)

    def footprint(tm, tn, tk):
        blocks = tm * tk * a.dtype.itemsize + tk * tn * 2 + tm * tn * (out_bytes + 4 * len(extra))
        return 2 * blocks + (tm * tn * 4 if tk < K else 0)

    tm, tn, tk = _tile(m_lim or M, MM_TM), _tile(n_lim or N, MM_TN), _tile(k_lim or K, MM_TK)
    while footprint(tm, tn, tk) > MM_VMEM_BUDGET and tk > MM_TK_MIN:
        tk = _tile(K, tk // 2)
    while footprint(tm, tn, tk) > MM_VMEM_BUDGET and tm > MM_TM_MIN:
        tm = _tile(M, tm // 2)
    nk = K // tk
    a_spec = pl.BlockSpec((tk, tm), lambda i, j, k: (k, i)) if mode == "tn" else pl.BlockSpec((tm, tk), lambda i, j, k: (i, k))
    if mode == "nt":
        b_block, b_rc = (tn, tk), (lambda i, j, k: (j, k))
    else:
        b_block, b_rc = (tk, tn), (lambda i, j, k: (k, j))
    b_spec = pl.BlockSpec(b_block, b_rc)
    mn_spec = pl.BlockSpec((tm, tn), lambda i, j, k: (i, j))
    dims = _DIMS[mode]
    n_out = {"relu2": 2, "rms": 2, "rms_bwd": 3}.get(epi, 1)

    def body(*refs):
        a_ref, b_ref = refs[0], refs[1]
        extra_refs = refs[2:2 + len(extra)]
        gain_ref = refs[2 + len(extra)] if whole_rows else None
        out_refs = refs[2 + len(extra) + whole_rows:][:n_out]
        prod = lax.dot_general(a_ref[...].astype(BF16), b_ref[...].astype(BF16), dims, preferred_element_type=F32)

        def finish(r):
            if epi == "rms_bwd":
                dx, dgc = _norm_bwd(extra_refs[1][...], r, gain_ref[...], N)
                dx = dx + extra_refs[0][...]
                out_refs[0][...] = dx
                out_refs[1][...] = dx.astype(BF16)
                _accumulate(out_refs[2], dgc)
                return
            if res is not None:
                r = r + extra_refs[0][...]
            if epi == "rms":
                out_refs[0][...] = r
                out_refs[1][...] = (r * _rinv(r, N) * gain_ref[...]).astype(BF16)
            elif epi == "relu2":
                out_refs[0][...] = r
                t = jnp.maximum(r, 0.0)
                out_refs[1][...] = (t * t).astype(BF16)
            elif epi == "drelu2":
                out_refs[0][...] = (r * (2.0 * jnp.maximum(extra_refs[-1][...], 0.0))).astype(out_dtype)
            else:
                out_refs[0][...] = r.astype(out_dtype)

        if nk == 1:
            finish(prod)
        else:
            acc = refs[-1]
            k = pl.program_id(2)

            @pl.when(k == 0)
            def _():
                acc[...] = prod

            @pl.when(k > 0)
            def _():
                acc[...] += prod

            @pl.when(k == nk - 1)
            def _():
                finish(acc[...])

    if epi in ("relu2", "rms", "rms_bwd"):
        out_shape = [jax.ShapeDtypeStruct((M, N), F32), jax.ShapeDtypeStruct((M, N), BF16)]
        out_specs = [mn_spec, mn_spec]
        if epi == "rms_bwd":
            out_shape.append(jax.ShapeDtypeStruct((SUBLANES, N), F32))
            out_specs.append(pl.BlockSpec((SUBLANES, tn), lambda i, j, k: (0, j)))
    elif shard_out == 0:
        per = (M // N_CHIPS) // tm
        out_shape = jax.ShapeDtypeStruct((N_CHIPS, M // N_CHIPS, N), out_dtype)
        out_specs = pl.BlockSpec((None, tm, tn), lambda i, j, k: (i // per, i % per, j))
    elif shard_out == 1:
        per = (N // N_CHIPS) // tn
        out_shape = jax.ShapeDtypeStruct((N_CHIPS, M, N // N_CHIPS), out_dtype)
        out_specs = pl.BlockSpec((None, tm, tn), lambda i, j, k: (j // per, i, j % per))
    else:
        out_shape = jax.ShapeDtypeStruct((M, N), out_dtype)
        out_specs = mn_spec
    gains = [gain] if whole_rows else []
    return _pallas(
        body, name=name, grid=(M // tm, N // tn, nk),
        in_specs=[a_spec, b_spec] + [mn_spec] * len(extra) + [pl.BlockSpec((1, tn), lambda i, j, k: (0, j))] * len(gains),
        out_specs=out_specs, out_shape=out_shape,
        scratch_shapes=[pltpu.VMEM((tm, tn), F32)] if nk > 1 else [],
        compiler_params=_cparams(*(["arbitrary"] * 3 if epi == "rms_bwd" else ["parallel", "parallel", "arbitrary"])),
    )(a, b, *extra, *gains)


def _rows(ts, d, col=0):
    return pl.BlockSpec((ts, d), lambda i: (i, col))


def _gain(d):
    return pl.BlockSpec((1, d), lambda i: (0, 0))


def _accum(d):
    return pl.BlockSpec((SUBLANES, d), lambda i: (0, 0))


def _accumulate(ref, val):
    i = pl.program_id(0)

    @pl.when(i == 0)
    def _():
        ref[...] = jnp.zeros_like(ref)

    ref[...] += jnp.broadcast_to(jnp.sum(val, axis=0, keepdims=True), ref.shape)


def _rinv(x, n):
    return lax.rsqrt(jnp.sum(x * x, axis=-1, keepdims=True) / n + EPS)


def _norm_bwd(x, dy, g, n):
    r = _rinv(x, n)
    xhat = x * r
    dyg = dy * g
    dx = r * (dyg - xhat * (jnp.sum(dyg * xhat, axis=-1, keepdims=True) / n))
    return dx, dy * xhat


def _rms_fwd(x, g, name):
    S, D = x.shape
    ts = _tile(S, ROW_TILE, SUBLANES)

    def body(x_ref, g_ref, h_ref):
        xv = x_ref[...]
        h_ref[...] = (xv * _rinv(xv, D) * g_ref[...]).astype(BF16)

    return _pallas(body, name=name, grid=(S // ts,), in_specs=[_rows(ts, D), _gain(D)], out_specs=_rows(ts, D),
                   out_shape=jax.ShapeDtypeStruct((S, D), BF16), compiler_params=_cparams("parallel"))(x, g)


def _lat_fwd(z, gq, gkv):
    S = z.shape[0]
    ts = _tile(S, ROW_TILE, SUBLANES)

    def body(z_ref, gq_ref, gkv_ref, q_ref, kv_ref):
        zq = z_ref[:, Z_Q:Z_KV]
        zkv = z_ref[:, Z_KV:Z_KPE]
        q_ref[...] = (zq * _rinv(zq, Q_LORA) * gq_ref[...]).astype(BF16)
        kv_ref[...] = (zkv * _rinv(zkv, KV_LORA) * gkv_ref[...]).astype(BF16)

    return _pallas(body, name="lat_fwd", grid=(S // ts,), in_specs=[_rows(ts, Z_LAT), _gain(Q_LORA), _gain(KV_LORA)],
                   out_specs=[_rows(ts, Q_LORA), _rows(ts, KV_LORA)],
                   out_shape=[jax.ShapeDtypeStruct((S, Q_LORA), BF16), jax.ShapeDtypeStruct((S, KV_LORA), BF16)],
                   compiler_params=_cparams("parallel"))(z, gq, gkv)


def _lat_bwd(dq, dkv, z, gq, gkv):
    S = z.shape[0]
    ts = _tile(S, ROW_TILE, SUBLANES)

    def body(dq_ref, dkv_ref, z_ref, gq_ref, gkv_ref, dlat_ref, dgq_ref, dgkv_ref):
        dxq, cq = _norm_bwd(z_ref[:, Z_Q:Z_KV], dq_ref[...], gq_ref[...], Q_LORA)
        dxkv, ckv = _norm_bwd(z_ref[:, Z_KV:Z_KPE], dkv_ref[...], gkv_ref[...], KV_LORA)
        dlat_ref[:, Z_Q:Z_KV] = dxq.astype(BF16)
        dlat_ref[:, Z_KV:Z_KPE] = dxkv.astype(BF16)
        _accumulate(dgq_ref, cq)
        _accumulate(dgkv_ref, ckv)

    return _pallas(body, name="lat_bwd", grid=(S // ts,),
                   in_specs=[_rows(ts, Q_LORA), _rows(ts, KV_LORA), _rows(ts, Z_LAT), _gain(Q_LORA), _gain(KV_LORA)],
                   out_specs=[_rows(ts, Z_LAT), _accum(Q_LORA), _accum(KV_LORA)],
                   out_shape=[jax.ShapeDtypeStruct((S, Z_LAT), BF16), jax.ShapeDtypeStruct((SUBLANES, Q_LORA), F32),
                              jax.ShapeDtypeStruct((SUBLANES, KV_LORA), F32)],
                   compiler_params=_cparams("arbitrary"))(dq, dkv, z, gq, gkv)


def _head_masks():
    lane = lax.broadcasted_iota(jnp.int32, (1, HEAD_PAD), 1)
    return lane < ROPE_LO, (lane >= ROPE_LO) & (lane < ROPE_HI), lane < ROPE_MID


def _seg_sum(t, m_n, m_r):
    sn = jnp.sum(jnp.where(m_n, t, 0.0), axis=-1, keepdims=True) / QK_NOPE
    sr = jnp.sum(jnp.where(m_r, t, 0.0), axis=-1, keepdims=True) / QK_ROPE
    return jnp.where(m_n, sn, sr)


def _rot_half(y, lo):
    half = QK_ROPE // 2
    return jnp.where(lo, -pltpu.roll(y, HEAD_PAD - half, 1), pltpu.roll(y, half, 1))


def _qk_fwd(q_raw, kv_raw, z, cs, sn, gq, gk):
    S = q_raw.shape[0]
    ts = _tile(S, ROW_TILE, SUBLANES)

    def body(q_ref, k_ref, v_ref, kpe_ref, cs_ref, sn_ref, gq_ref, gk_ref, qf_ref, kf_ref, vb_ref):
        m_n, m_r, lo = _head_masks()
        cos, sin, gqv, gkv = cs_ref[...], sn_ref[...], gq_ref[...], gk_ref[...]

        def norm(x, g):
            return x * lax.rsqrt(_seg_sum(x * x, m_n, m_r) + EPS) * g

        def rope(y):
            return y * cos + _rot_half(y, lo) * sin

        kr = rope(norm(kpe_ref[...], gkv))
        lane = lax.broadcasted_iota(jnp.int32, (1, HEAD_PAD), 1)
        for h in range(N_HEADS):
            sl = slice(h * HEAD_PAD, (h + 1) * HEAD_PAD)
            qf_ref[:, sl] = rope(norm(q_ref[:, sl], gqv)).astype(BF16)
            kf_ref[:, sl] = (norm(k_ref[:, sl], gkv) + kr).astype(BF16)
            vb_ref[:, sl] = jnp.where(lane == V_HEAD, 1.0, v_ref[:, sl]).astype(BF16)

    hd = jax.ShapeDtypeStruct((S, HP), BF16)
    return _pallas(body, name="qk_fwd", grid=(S // ts,),
                   in_specs=[_rows(ts, HP), _rows(ts, HP, 0), _rows(ts, HP, 1), _rows(ts, HEAD_PAD, Z_KPE // HEAD_PAD),
                             _rows(ts, HEAD_PAD), _rows(ts, HEAD_PAD), _gain(HEAD_PAD), _gain(HEAD_PAD)],
                   out_specs=[_rows(ts, HP)] * 3, out_shape=[hd, hd, hd],
                   compiler_params=_cparams("parallel"))(q_raw, kv_raw, kv_raw, z, cs, sn, gq, gk)


def _qk_bwd(dqf, dkf, dv, q_raw, kv_raw, z, cs, sn, gq, gk):
    S = q_raw.shape[0]
    ts = _tile(S, ROW_TILE, SUBLANES)

    def body(dqf_ref, dkf_ref, dv_ref, q_ref, k_ref, kpe_ref, cs_ref, sn_ref, gq_ref, gk_ref,
             dq_ref, dkv_ref, dkpe_ref, dgq_ref, dgk_ref):
        m_n, m_r, lo = _head_masks()
        cos, sin, gqv, gkv = cs_ref[...], sn_ref[...], gq_ref[...], gk_ref[...]

        def rope_t(w):
            return w * cos - jnp.where(m_r, _rot_half(w * sin, lo), 0.0)

        def norm_bwd(x, dy, g):
            r = lax.rsqrt(_seg_sum(x * x, m_n, m_r) + EPS)
            xhat = x * r
            dyg = dy * g
            return r * (dyg - xhat * _seg_sum(dyg * xhat, m_n, m_r)), dy * xhat

        accq = jnp.zeros((ts, HEAD_PAD), F32)
        acck = jnp.zeros((ts, HEAD_PAD), F32)
        dkr = jnp.zeros((ts, HEAD_PAD), F32)
        for h in range(N_HEADS):
            sl = slice(h * HEAD_PAD, (h + 1) * HEAD_PAD)
            dx, c = norm_bwd(q_ref[:, sl], rope_t(dqf_ref[:, sl]), gqv)
            dq_ref[:, sl] = dx.astype(BF16)
            accq = accq + c
            dk = dkf_ref[:, sl]
            dkr = dkr + jnp.where(m_r, dk, 0.0)
            dx, c = norm_bwd(k_ref[:, sl], jnp.where(m_n, dk, 0.0), gkv)
            dkv_ref[:, sl] = dx.astype(BF16)
            acck = acck + c
            dkv_ref[:, HP + h * HEAD_PAD:HP + (h + 1) * HEAD_PAD] = dv_ref[:, sl].astype(BF16)
        dx, c = norm_bwd(kpe_ref[...], rope_t(dkr), gkv)
        dkpe_ref[...] = dx.astype(BF16)
        _accumulate(dgq_ref, accq)
        _accumulate(dgk_ref, acck + c)

    return _pallas(body, name="qk_bwd", grid=(S // ts,),
                   in_specs=[_rows(ts, HP)] * 4 + [_rows(ts, HP, 0), _rows(ts, HEAD_PAD, Z_KPE // HEAD_PAD),
                                                   _rows(ts, HEAD_PAD), _rows(ts, HEAD_PAD), _gain(HEAD_PAD), _gain(HEAD_PAD)],
                   out_specs=[_rows(ts, HP), _rows(ts, 2 * HP), _rows(ts, HEAD_PAD), _accum(HEAD_PAD), _accum(HEAD_PAD)],
                   out_shape=[jax.ShapeDtypeStruct((S, HP), BF16), jax.ShapeDtypeStruct((S, 2 * HP), BF16),
                              jax.ShapeDtypeStruct((S, HEAD_PAD), BF16), jax.ShapeDtypeStruct((SUBLANES, HEAD_PAD), F32),
                              jax.ShapeDtypeStruct((SUBLANES, HEAD_PAD), F32)],
                   compiler_params=_cparams("arbitrary"))(dqf, dkf, dv, q_raw, kv_raw, z, cs, sn, gq, gk)


def _causal(t):
    row = lax.broadcasted_iota(jnp.int32, (t, t), 0)
    col = lax.broadcasted_iota(jnp.int32, (t, t), 1)
    return col <= row


def _attn_tiles(S, chains):
    t = _tile(S, ATT_TILE, SUBLANES)
    return t, min(chains, S // t)


def _attn_fwd(qf, kf, vb, rider):
    S = qf.shape[0]
    t, nc = _attn_tiles(S, ATT_CHAINS_FWD)
    tq = nc * t
    nqt = S // tq

    def body(q_ref, k_ref, v_ref, *rest):
        (o_ref, lse_ref, m_ref, acc_ref), ride = rider.split(rest, 2, 2)
        qt = pl.program_id(1)
        ride(pl.program_id(0) * nqt + qt, N_HEADS * nqt)
        m_ref[...] = jnp.full_like(m_ref, -jnp.inf)
        acc_ref[...] = jnp.zeros_like(acc_ref)

        def chain(a, kb, vb_, masked):
            sub = slice(a * t, (a + 1) * t)
            s = lax.dot_general(q_ref[sub, :], kb, _DIMS["nt"], preferred_element_type=F32) * ATT_SCALE_LOG2
            if masked:
                s = jnp.where(_causal(t), s, -jnp.inf)
            m_old = m_ref[sub, :]
            m_new = jnp.maximum(m_old, jnp.max(s, axis=-1, keepdims=True))
            p = jnp.exp2(s - jnp.concatenate([m_new] * (t // HEAD_PAD), axis=1))
            acc_ref[sub, :] = jnp.exp2(m_old - m_new) * acc_ref[sub, :] + jnp.dot(p.astype(BF16), vb_, preferred_element_type=F32)
            m_ref[sub, :] = m_new

        def trip(j, carry):
            rows = pl.ds(pl.multiple_of(j * t, t), t)
            kb, vb_ = k_ref[rows, :], v_ref[rows, :]
            for a in range(nc):
                chain(a, kb, vb_, False)
            return carry

        lax.fori_loop(0, nc * qt, trip, 0)
        for d in range(nc):
            rows = pl.ds(pl.multiple_of((nc * qt + d) * t, t), t)
            kb, vb_ = k_ref[rows, :], v_ref[rows, :]
            for a in range(d, nc):
                chain(a, kb, vb_, a == d)
        acc = acc_ref[...]
        l = acc[:, V_HEAD:V_HEAD + 1]
        lane = lax.broadcasted_iota(jnp.int32, (1, HEAD_PAD), 1)
        o_ref[...] = jnp.where(lane < V_HEAD, acc / l, 0.0)
        lse_ref[...] = m_ref[...] + jnp.log(l) * LOG2E

    tile = pl.BlockSpec((tq, HEAD_PAD), lambda h, i: (i, h))
    full = pl.BlockSpec((S, HEAD_PAD), lambda h, i: (0, h))
    outs = _pallas(body, name="attn_fwd", grid=(N_HEADS, nqt), in_specs=[tile, full, full] + rider.in_specs(),
                   out_specs=[tile, pl.BlockSpec((None, tq, HEAD_PAD), lambda h, i: (h, i, 0))] + rider.out_specs(),
                   out_shape=[jax.ShapeDtypeStruct((S, HP), F32), jax.ShapeDtypeStruct((N_HEADS, S, HEAD_PAD), F32)]
                   + rider.out_shape(),
                   scratch_shapes=[pltpu.VMEM((tq, HEAD_PAD), F32), pltpu.VMEM((tq, HEAD_PAD), F32)] + rider.scratch(),
                   input_output_aliases=rider.aliases(3, 2),
                   compiler_params=_cparams("arbitrary", "arbitrary"))(qf, kf, vb, *rider.args())
    return outs[0], outs[1], outs[2:]


def _attn_bwd(qf, kf, vb, o, do, lse, rider):
    S = qf.shape[0]
    t, nc = _attn_tiles(S, ATT_CHAINS_BWD)
    tkv = nc * t
    nq = S // t
    nkt = S // tkv

    def body(q_ref, k_ref, v_ref, o_ref, do_ref, lse_ref, *rest):
        (dq_ref, dk_ref, dv_ref), ride = rider.split(rest, 3, 0)
        kt = pl.program_id(1)
        ride(pl.program_id(0) * nkt + kt, N_HEADS * nkt)

        @pl.when(kt == 0)
        def _():
            dq_ref[...] = jnp.zeros_like(dq_ref)

        dk_ref[...] = jnp.zeros_like(dk_ref)
        dv_ref[...] = jnp.zeros_like(dv_ref)

        def q_block(rows):
            dof = do_ref[rows, :]
            delta = jnp.sum(dof * o_ref[rows, :], axis=-1, keepdims=True)
            return q_ref[rows, :], dof.astype(BF16), lse_ref[rows, :][:, :1], delta

        def chain(b, qv, masked):
            q, dob, lse, delta = qv
            sub = slice(b * t, (b + 1) * t)
            kb, vb_ = k_ref[sub, :], v_ref[sub, :]
            s = lax.dot_general(q, kb, _DIMS["nt"], preferred_element_type=F32) * ATT_SCALE_LOG2
            p = jnp.exp2(s - lse)
            if masked:
                p = jnp.where(_causal(t), p, 0.0)
            dv_ref[sub, :] += lax.dot_general(p.astype(BF16), dob, _DIMS["tn"], preferred_element_type=F32)
            dp = lax.dot_general(dob, vb_, _DIMS["nt"], preferred_element_type=F32)
            ds = (p * (dp - delta) * ATT_SCALE).astype(BF16)
            dk_ref[sub, :] += lax.dot_general(ds, q, _DIMS["tn"], preferred_element_type=F32)
            return jnp.dot(ds, kb, preferred_element_type=F32)

        for a in range(nc):
            rows = pl.ds(pl.multiple_of((nc * kt + a) * t, t), t)
            qv = q_block(rows)
            dq_ref[rows, :] += sum(chain(b, qv, b == a) for b in range(a + 1))

        def trip(i, carry):
            rows = pl.ds(pl.multiple_of(i * t, t), t)
            qv = q_block(rows)
            dq_ref[rows, :] += sum(chain(b, qv, False) for b in range(nc))
            return carry

        lax.fori_loop(nc * (kt + 1), nq, trip, 0)

    tile = pl.BlockSpec((tkv, HEAD_PAD), lambda h, j: (j, h))
    full = pl.BlockSpec((S, HEAD_PAD), lambda h, j: (0, h))
    hd = jax.ShapeDtypeStruct((S, HP), F32)
    outs = _pallas(body, name="attn_bwd", grid=(N_HEADS, nkt),
                   in_specs=[full, tile, tile, full, full, pl.BlockSpec((None, S, HEAD_PAD), lambda h, j: (h, 0, 0))]
                   + rider.in_specs(),
                   out_specs=[full, tile, tile] + rider.out_specs(), out_shape=[hd, hd, hd] + rider.out_shape(),
                   scratch_shapes=rider.scratch(), input_output_aliases=rider.aliases(6, 3),
                   compiler_params=_cparams("arbitrary", "arbitrary"))(qf, kf, vb, o, do, lse, *rider.args())
    return outs[0], outs[1], outs[2], outs[3:]


def _shift_down(u, j, row):
    return jnp.where(row >= j, pltpu.roll(u, j, 0), 0.0)


def _shift_up(u, j, row, s):
    return jnp.where(row < s - j, pltpu.roll(u, s - j, 0), 0.0)


def _conv_cols(s, first_tile):
    return pl.BlockSpec((s, LANES), lambda cb: (0, first_tile + cb))


def _conv_fwd(z, cw8):
    S = z.shape[0]

    def body(gb_ref, gc_ref, xin_ref, w_ref, out_ref):
        row = lax.broadcasted_iota(jnp.int32, (S, LANES), 0)
        u = gc_ref[...] * xin_ref[...]
        y = w_ref[0:1, :] * u
        for j in range(1, CONV_TAPS):
            y = y + w_ref[j:j + 1, :] * _shift_down(u, j, row)
        out_ref[...] = gb_ref[...] * y

    return _pallas(body, name="conv_fwd", grid=(CONV_WIDTH // LANES,),
                   in_specs=[_conv_cols(S, Z_GB // LANES), _conv_cols(S, Z_GC // LANES), _conv_cols(S, Z_XIN // LANES),
                             pl.BlockSpec((SUBLANES, LANES), lambda cb: (0, cb))],
                   out_specs=_conv_cols(S, 0), out_shape=jax.ShapeDtypeStruct((S, CONV_WIDTH), F32),
                   compiler_params=_cparams("parallel"))(z, z, z, cw8)


def _conv_bwd(dconv, z, cw8):
    S = z.shape[0]

    def body(d_ref, gb_ref, gc_ref, xin_ref, w_ref, dgb_ref, dgc_ref, dxin_ref, dw_ref):
        row = lax.broadcasted_iota(jnp.int32, (S, LANES), 0)
        gc, xin, d = gc_ref[...], xin_ref[...], d_ref[...]
        u = gc * xin
        dy = d * gb_ref[...]
        y = w_ref[0:1, :] * u
        du = w_ref[0:1, :] * dy
        dw = [jnp.sum(dy * u, axis=0, keepdims=True)]
        for j in range(1, CONV_TAPS):
            uj = _shift_down(u, j, row)
            y = y + w_ref[j:j + 1, :] * uj
            du = du + w_ref[j:j + 1, :] * _shift_up(dy, j, row, S)
            dw.append(jnp.sum(dy * uj, axis=0, keepdims=True))
        dgb_ref[...] = (d * y).astype(BF16)
        dgc_ref[...] = (du * xin).astype(BF16)
        dxin_ref[...] = (du * gc).astype(BF16)
        tap = lax.broadcasted_iota(jnp.int32, (SUBLANES, LANES), 0)
        dw_ref[...] = sum(jnp.where(tap == j, dw[j], 0.0) for j in range(CONV_TAPS))

    col = _conv_cols(S, 0)
    sd = jax.ShapeDtypeStruct((S, CONV_WIDTH), BF16)
    return _pallas(body, name="conv_bwd", grid=(CONV_WIDTH // LANES,),
                   in_specs=[col, _conv_cols(S, Z_GB // LANES), _conv_cols(S, Z_GC // LANES), _conv_cols(S, Z_XIN // LANES),
                             pl.BlockSpec((SUBLANES, LANES), lambda cb: (0, cb))],
                   out_specs=[col, col, col, pl.BlockSpec((SUBLANES, LANES), lambda cb: (0, cb))],
                   out_shape=[sd, sd, sd, jax.ShapeDtypeStruct((SUBLANES, CONV_WIDTH), F32)],
                   compiler_params=_cparams("parallel"))(dconv, z, z, z, cw8)


def _mix_fwd(o, conv, ga, gc):
    S = o.shape[0]
    ts = _tile(S, ROW_TILE, SUBLANES)

    def body(o_ref, c_ref, ga_ref, gc_ref, out_ref):
        ov, cv = o_ref[...], c_ref[...]
        out_ref[:, :HP] = (ov * _rinv(ov, ATTN_WIDTH) * ga_ref[...]).astype(BF16)
        out_ref[:, HP:] = (cv * _rinv(cv, CONV_WIDTH) * gc_ref[...]).astype(BF16)

    return _pallas(body, name="mix_fwd", grid=(S // ts,),
                   in_specs=[_rows(ts, HP), _rows(ts, CONV_WIDTH), _gain(HP), _gain(CONV_WIDTH)],
                   out_specs=_rows(ts, HP + CONV_WIDTH), out_shape=jax.ShapeDtypeStruct((S, HP + CONV_WIDTH), BF16),
                   compiler_params=_cparams("parallel"))(o, conv, ga, gc)


def _mix_bwd(dmixed, o, conv, ga, gc):
    S = o.shape[0]
    ts = _tile(S, ROW_TILE, SUBLANES)

    def body(d_ref, o_ref, c_ref, ga_ref, gc_ref, do_ref, dc_ref, dga_ref, dgc_ref):
        dx, ca = _norm_bwd(o_ref[...], d_ref[:, :HP], ga_ref[...], ATTN_WIDTH)
        do_ref[...] = dx
        dx, cc = _norm_bwd(c_ref[...], d_ref[:, HP:], gc_ref[...], CONV_WIDTH)
        dc_ref[...] = dx
        _accumulate(dga_ref, ca)
        _accumulate(dgc_ref, cc)

    return _pallas(body, name="mix_bwd", grid=(S // ts,),
                   in_specs=[_rows(ts, HP + CONV_WIDTH), _rows(ts, HP), _rows(ts, CONV_WIDTH), _gain(HP), _gain(CONV_WIDTH)],
                   out_specs=[_rows(ts, HP), _rows(ts, CONV_WIDTH), _accum(HP), _accum(CONV_WIDTH)],
                   out_shape=[jax.ShapeDtypeStruct((S, HP), F32), jax.ShapeDtypeStruct((S, CONV_WIDTH), F32),
                              jax.ShapeDtypeStruct((SUBLANES, HP), F32), jax.ShapeDtypeStruct((SUBLANES, CONV_WIDTH), F32)],
                   compiler_params=_cparams("arbitrary"))(dmixed, o, conv, ga, gc)


def _ple_fwd(x, gl, pe, g_next):
    S, D = x.shape
    ts = _tile(S, ROW_TILE, SUBLANES)

    def body(x_ref, gl_ref, pe_ref, g_ref, out_ref, h_ref):
        y = x_ref[...] + jax.nn.sigmoid(gl_ref[...]) * pe_ref[...]
        out_ref[...] = y
        h_ref[...] = (y * _rinv(y, D) * g_ref[...]).astype(BF16)

    return _pallas(body, name="ple_fwd", grid=(S // ts,), in_specs=[_rows(ts, D)] * 3 + [_gain(D)], out_specs=[_rows(ts, D)] * 2,
                   out_shape=[jax.ShapeDtypeStruct((S, D), F32), jax.ShapeDtypeStruct((S, D), BF16)],
                   compiler_params=_cparams("parallel"))(x, gl, pe, g_next)


def _ple_bwd(dx, gl, pe):
    S, D = dx.shape
    ts = _tile(S, ROW_TILE, SUBLANES)

    def body(dx_ref, gl_ref, pe_ref, dpe_ref, dgl_ref):
        d = dx_ref[...]
        gate = jax.nn.sigmoid(gl_ref[...])
        dpe_ref[...] = (d * gate).astype(BF16)
        dgl_ref[...] = (d * pe_ref[...] * (gate * (1.0 - gate))).astype(BF16)

    sd = jax.ShapeDtypeStruct((S, D), BF16)
    return _pallas(body, name="ple_bwd", grid=(S // ts,), in_specs=[_rows(ts, D)] * 3, out_specs=[_rows(ts, D)] * 2,
                   out_shape=[sd, sd], compiler_params=_cparams("parallel"))(dx, gl, pe)


def _loss_grad(y, target):
    S, D = y.shape
    ts = _tile(S, ROW_TILE, SUBLANES)

    def body(y_ref, t_ref, dy_ref, sq_ref):
        e = y_ref[...] - t_ref[...]
        dy_ref[...] = e / D

        @pl.when(pl.program_id(0) == 0)
        def _():
            sq_ref[...] = jnp.zeros_like(sq_ref)

        sq_ref[...] += jnp.broadcast_to(jnp.sum(jnp.sum(e * e, axis=1, keepdims=True), axis=0, keepdims=True), sq_ref.shape)

    return _pallas(body, name="loss_grad", grid=(S // ts,), in_specs=[_rows(ts, D)] * 2,
                   out_specs=[_rows(ts, D), _accum(LANES)],
                   out_shape=[jax.ShapeDtypeStruct((S, D), F32), jax.ShapeDtypeStruct((SUBLANES, LANES), F32)],
                   compiler_params=_cparams("arbitrary"))(y, target)


def _adamw(w, g, m, v, name):
    L, R, C = w.shape
    tr = _tile(R, ROW_TILE, SUBLANES)

    def body(w_ref, g_ref, m_ref, v_ref, d_ref, m2_ref, v2_ref):
        gv = g_ref[...]
        m2 = ADAM_B1 * m_ref[...] + (1.0 - ADAM_B1) * gv
        v2 = ADAM_B2 * v_ref[...] + (1.0 - ADAM_B2) * (gv * gv)
        m_hat = m2 / (1.0 - ADAM_B1 ** ADAM_STEP)
        v_hat = v2 / (1.0 - ADAM_B2 ** ADAM_STEP)
        d_ref[...] = -ADAM_LR * (m_hat / (jnp.sqrt(v_hat) + ADAM_EPS) + ADAM_WD * w_ref[...])
        m2_ref[...] = m2
        v2_ref[...] = v2

    sd = jax.ShapeDtypeStruct((L, R, C), F32)
    spec = pl.BlockSpec((None, tr, C), lambda l, i: (l, i, 0))
    return _pallas(body, name=name, grid=(L, R // tr), in_specs=[spec] * 4, out_specs=[spec] * 3, out_shape=[sd, sd, sd],
                   compiler_params=_cparams("parallel", "parallel"))(w, g, m, v)


def _place():
    return lax.axis_index("x"), lax.axis_index("y"), lax.axis_index("c")


def _other_chips(x, y):
    return [(1 - x, y), (x, 1 - y), (1 - x, 1 - y)]


HBM = pl.BlockSpec(memory_space=pl.ANY)
VMEM_SPEC = pl.BlockSpec(memory_space=pltpu.VMEM)


def _remote_copy(send_sems, recv_sems, k, src, dst, to):
    return pltpu.make_async_remote_copy(src_ref=src, dst_ref=dst, send_sem=send_sems.at[k], recv_sem=recv_sems.at[k],
                                        device_id=to, device_id_type=MESH)


def _comm_call(body, name, arrays, out_shapes, n_remote, in_place=False):
    scratch = [pltpu.SemaphoreType.DMA((n_remote,)), pltpu.SemaphoreType.DMA((n_remote,))]
    aliases = {i: i for i in range(len(arrays))} if in_place else {}
    return _pallas(body, name=name, in_specs=[HBM] * len(arrays), out_specs=[HBM] * len(out_shapes), out_shape=out_shapes,
                   scratch_shapes=scratch, input_output_aliases=aliases,
                   compiler_params=pltpu.CompilerParams(has_side_effects=True))(*arrays)


def _shard_slab(w, chip_arr, name):
    L, R, C = w.shape
    tr = _tile(R, ROW_TILE, 16)

    def body(j_ref, w_ref, out_ref):
        out_ref[...] = w_ref[...].astype(BF16)

    grid_spec = pltpu.PrefetchScalarGridSpec(
        num_scalar_prefetch=1, grid=(L, R // tr),
        in_specs=[pl.BlockSpec((None, tr, C), lambda l, i, j_ref: (l, i, 0))],
        out_specs=pl.BlockSpec((None, None, tr, C), lambda l, i, j_ref: (j_ref[0], l, i, 0)))
    return _pallas(body, name=name, grid_spec=grid_spec, out_shape=jax.ShapeDtypeStruct((N_CHIPS, L, R, C), BF16),
                   compiler_params=_cparams("parallel", "parallel"))(chip_arr, w)


class Rider:
    def __init__(self, reads, inplace, n_sems, stages):
        self.reads, self.inplace, self.n_sems, self.stages = list(reads), list(inplace), n_sems, stages

    def args(self):
        return self.reads + self.inplace

    def in_specs(self):
        return [HBM] * len(self.args())

    def out_specs(self):
        return [HBM] * len(self.inplace)

    def out_shape(self):
        return [jax.ShapeDtypeStruct(a.shape, a.dtype) for a in self.inplace]

    def scratch(self):
        return [pltpu.SemaphoreType.DMA((self.n_sems,)), pltpu.SemaphoreType.DMA((self.n_sems,))]

    def aliases(self, n_host_in, n_host_out):
        return {n_host_in + len(self.reads) + j: n_host_out + j for j in range(len(self.inplace))}

    def split(self, rest, n_host_out, n_host_scratch):
        n_r, n_io = len(self.reads), len(self.inplace)
        reads = rest[:n_r]
        host_out = rest[n_r + n_io:n_r + n_io + n_host_out]
        outs = rest[n_r + n_io + n_host_out:n_r + 2 * n_io + n_host_out]
        host_scratch = rest[n_r + 2 * n_io + n_host_out:n_r + 2 * n_io + n_host_out + n_host_scratch]
        sems = rest[n_r + 2 * n_io + n_host_out + n_host_scratch:]

        def ride(step, n_steps):
            at = [0, n_steps - 1] if len(self.stages) == 2 else [0, (3 * n_steps) // 4, n_steps - 1]
            for s, stage in zip(at, self.stages):
                pl.when(step == s)(functools.partial(stage, reads, outs, *sems))

        return tuple(host_out) + tuple(host_scratch), ride

    def run(self, name):
        n_r, n_io = len(self.reads), len(self.inplace)

        def body(*refs):
            for stage in self.stages:
                stage(refs[:n_r], refs[n_r + n_io:n_r + 2 * n_io], *refs[n_r + 2 * n_io:])

        return _pallas(body, name=name, in_specs=self.in_specs(), out_specs=self.out_specs(), out_shape=self.out_shape(),
                       scratch_shapes=self.scratch(), input_output_aliases=self.aliases(0, 0),
                       compiler_params=pltpu.CompilerParams(has_side_effects=True))(*self.args())


def _allgather_rider(slabs, items):
    def stage(which, reads, outs, send_sems, recv_sems):
        x, y, c = _place()
        me, sibling = (x, y, c), (x, y, 1 - c)
        copy = functools.partial(_remote_copy, send_sems, recv_sems)
        for n, (j, l) in enumerate(items):
            ref = outs[j]
            h = ref.shape[2] // 2
            mine, theirs = pl.ds(c * h, h), pl.ds((1 - c) * h, h)
            own = ref.at[2 * x + y, l, mine]
            for k, (cx, cy) in enumerate(_other_chips(x, y)):
                arrived = ref.at[2 * cx + cy, l, mine]
                if which == 0:
                    copy(6 * n + k, own, own, (cx, cy, c)).start()
                elif which == 1:
                    copy(6 * n + k, arrived, arrived, me).wait_recv()
                    copy(6 * n + 3 + k, arrived, arrived, sibling).start()
                else:
                    passed = ref.at[2 * cx + cy, l, theirs]
                    copy(6 * n + 3 + k, passed, passed, me).wait_recv()
                    copy(6 * n + k, own, own, me).wait_send()
                    copy(6 * n + 3 + k, arrived, arrived, me).wait_send()

    return Rider([], slabs, 6 * len(items), [functools.partial(stage, w) for w in range(3)])


def _exchange_rider(grads, landing, items):
    def stage(start, reads, outs, send_sems, recv_sems):
        x, y, c = _place()
        for n, (i, j, l) in enumerate(items):
            g_ref, r_ref = reads[i], outs[j]
            h = g_ref.shape[1] // 2
            for k in range(1, N_DEV):
                px, py, pc = (1 - x if k & 4 else x, 1 - y if k & 2 else y, 1 - c if k & 1 else c)
                cp = _remote_copy(send_sems, recv_sems, 7 * n + k - 1, g_ref.at[2 * px + py, pl.ds(pc * h, h)],
                                  r_ref.at[l, k - 1], (px, py, pc))
                if start:
                    cp.start()
                else:
                    cp.wait()

    return Rider(grads, landing, 7 * len(items), [functools.partial(stage, True), functools.partial(stage, False)])


def _reduce_partials(g, r, chip_arr, c_arr, name):
    L, n, R, C = g.shape
    H = R // 2
    th = _tile(H, ROW_TILE, 16)
    nb = H // th

    def body(j_ref, c_ref, g_ref, r_ref, out_ref):
        s = g_ref[...].astype(F32)
        for k in range(N_DEV - 1):
            s = s + r_ref[k].astype(F32)
        out_ref[...] = s

    grid_spec = pltpu.PrefetchScalarGridSpec(
        num_scalar_prefetch=2, grid=(L, nb),
        in_specs=[pl.BlockSpec((None, None, th, C), lambda l, i, j_ref, c_ref: (l, j_ref[0], c_ref[0] * nb + i, 0)),
                  pl.BlockSpec((None, N_DEV - 1, th, C), lambda l, i, j_ref, c_ref: (l, 0, i, 0))],
        out_specs=pl.BlockSpec((None, th, C), lambda l, i, j_ref, c_ref: (l, c_ref[0] * nb + i, 0)))
    return _pallas(body, name=name, grid_spec=grid_spec, out_shape=jax.ShapeDtypeStruct((L, R, C), F32),
                   compiler_params=_cparams("parallel", "parallel"))(chip_arr, c_arr, g, r)


def _join_halves(fs):
    n = len(fs)

    def body(*refs):
        out_refs, (send_sems, recv_sems) = refs[n:2 * n], refs[2 * n:]
        x, y, c = _place()
        cps = []
        for i, out_ref in enumerate(out_refs):
            h = out_ref.shape[1] // 2
            mine = out_ref.at[:, pl.ds(c * h, h)]
            cps.append(_remote_copy(send_sems, recv_sems, i, mine, mine, (x, y, 1 - c)))
            cps[-1].start()
        for cp in cps:
            cp.wait()

    return _comm_call(body, "grad_join_halves", fs, [jax.ShapeDtypeStruct(f.shape, f.dtype) for f in fs], n, in_place=True)


def _allgather_small(v, name):
    R, C = v.shape

    def body(v_ref, out_ref, send_sems, recv_sems):
        x, y, c = _place()
        me = 4 * x + 2 * y + c
        out_ref[me] = v_ref[...]
        cps = []
        for k in range(1, N_DEV):
            peer = (1 - x if k & 4 else x, 1 - y if k & 2 else y, 1 - c if k & 1 else c)
            cps.append(pltpu.make_async_remote_copy(src_ref=v_ref, dst_ref=out_ref.at[me], send_sem=send_sems.at[k - 1],
                                                    recv_sem=recv_sems.at[k - 1], device_id=peer, device_id_type=MESH))
        for cp in cps:
            cp.start()
        for cp in cps:
            cp.wait()

    return _pallas(body, name=name, in_specs=[VMEM_SPEC], out_specs=VMEM_SPEC,
                   out_shape=jax.ShapeDtypeStruct((N_DEV, R, C), v.dtype),
                   scratch_shapes=[pltpu.SemaphoreType.DMA((N_DEV - 1,)), pltpu.SemaphoreType.DMA((N_DEV - 1,))],
                   compiler_params=pltpu.CompilerParams(has_side_effects=True))(v)


def _sum_devices(g):
    n, R, C = g.shape

    def body(g_ref, out_ref):
        s = g_ref[0]
        for d in range(1, n):
            s = s + g_ref[d]
        out_ref[...] = s

    return _pallas(body, name="sum_devices", in_specs=[VMEM_SPEC], out_specs=VMEM_SPEC,
                   out_shape=jax.ShapeDtypeStruct((R, C), g.dtype))(g)


def _pad_heads(a, width):
    lead = a.shape[:-1]
    a = a.reshape(lead + (N_HEADS, width))
    a = jnp.pad(a, [(0, 0)] * len(lead) + [(0, 0), (0, HEAD_PAD - width)])
    return a.reshape(lead + (HP,))


def _unpad_heads(a, width):
    lead = a.shape[:-1]
    return a.reshape(lead + (N_HEADS, HEAD_PAD))[..., :width].reshape(lead + (N_HEADS * width,))


def _pre_weights(slabs, l):
    cols = lambda name: slabs[name][:, l].transpose(1, 0, 2).reshape(slabs[name].shape[2], -1)
    w_in = cols("w_in")
    kpe = jnp.pad(w_in[:, 640:672], ((0, 0), (ROPE_LO, HEAD_PAD - ROPE_HI)))
    w_ukv = cols("w_ukv").reshape(KV_LORA, N_HEADS, QK_NOPE + V_HEAD)
    return dict(
        w_in=jnp.concatenate([w_in[:, :640], kpe, w_in[:, 672:]], axis=1),
        w_uq=_pad_heads(cols("w_uq"), QK_HEAD),
        w_ukv=jnp.concatenate([_pad_heads(w_ukv[..., :QK_NOPE].reshape(KV_LORA, -1), QK_NOPE),
                               _pad_heads(w_ukv[..., QK_NOPE:].reshape(KV_LORA, -1), V_HEAD)], axis=1),
    )


def _post_weights(slabs, l):
    rows = lambda name: slabs[name][:, l].reshape(-1, slabs[name].shape[3])
    cols = lambda name: slabs[name][:, l].transpose(1, 0, 2).reshape(slabs[name].shape[2], -1)
    w_o = rows("w_o")
    return dict(
        w_o=jnp.concatenate([_pad_heads(w_o[:ATTN_WIDTH].T, V_HEAD).T, w_o[ATTN_WIDTH:]], axis=0),
        w_up=cols("w_up"), w_down=rows("w_down"), w_ple_gate=rows("w_ple_gate"), w_ple=cols("w_ple"),
    )


def _gains(small, l):
    row = lambda name: small[name][l].reshape(1, -1)
    headrow = lambda a, b: jnp.pad(jnp.concatenate([small[a][l], small[b][l]]), (0, HEAD_PAD - QK_HEAD)).reshape(1, HEAD_PAD)
    return dict(
        g_mix=row("g_mix"), g_q_lat=row("g_q_lat"), g_kv_lat=row("g_kv_lat"), g_mlp=row("g_mlp"), g_ple=row("g_ple"),
        g_out_conv=row("g_out_conv"), g_out_attn=_pad_heads(small["g_out_attn"][l], V_HEAD).reshape(1, HP),
        gq=headrow("g_qn_nope", "g_qn_rope"), gk=headrow("g_kn_nope", "g_kn_rope"),
        cw8=jnp.pad(small["conv_w"][l], ((0, SUBLANES - CONV_TAPS), (0, 0))),
    )


def _w_o_shards(dw_o):
    return jnp.concatenate([_unpad_heads(dw_o[:HP].T, V_HEAD).T, dw_o[HP:]], axis=0).reshape(N_CHIPS, -1, D_MODEL)


def _unpad_grads(gp):
    dw_in = gp["w_in"]
    dw_ukv = gp["w_ukv"]
    k_part = dw_ukv[:, :HP].reshape(KV_LORA, N_HEADS, HEAD_PAD)[..., :QK_NOPE]
    v_part = dw_ukv[:, HP:].reshape(KV_LORA, N_HEADS, HEAD_PAD)[..., :V_HEAD]
    first = lambda name: gp[name][0]
    col_shards = lambda a: a.reshape(a.shape[0], N_CHIPS, -1).transpose(1, 0, 2)
    return dict(
        w_in=col_shards(jnp.concatenate([dw_in[:, :640], dw_in[:, Z_KPE + ROPE_LO:Z_KPE + ROPE_HI], dw_in[:, Z_GB:]], axis=1)),
        w_uq=col_shards(_unpad_heads(gp["w_uq"], QK_HEAD)),
        w_ukv=col_shards(jnp.concatenate([k_part, v_part], axis=-1).reshape(KV_LORA, -1)),
        w_o=gp["w_o"], w_up=gp["w_up"], w_down=gp["w_down"], w_ple_gate=gp["w_ple_gate"], w_ple=gp["w_ple"],
        g_mix=first("g_mix"), g_q_lat=first("g_q_lat"), g_kv_lat=first("g_kv_lat"), g_mlp=first("g_mlp"),
        g_ple=first("g_ple"), g_out_conv=first("g_out_conv"), g_out_attn=_unpad_heads(first("g_out_attn"), V_HEAD),
        g_qn_nope=gp["gq"][0, :QK_NOPE], g_qn_rope=gp["gq"][0, QK_NOPE:QK_HEAD],
        g_kn_nope=gp["gk"][0, :QK_NOPE], g_kn_rope=gp["gk"][0, QK_NOPE:QK_HEAD],
        conv_w=gp["cw8"][:CONV_TAPS],
    )


def _rope_tables(positions):
    inv_freq = 1.0 / (ROPE_THETA ** (jnp.arange(0, QK_ROPE, 2, dtype=F32) / QK_ROPE))
    ang = positions.astype(F32)[:, None] * inv_freq
    cos, sin = jnp.cos(ang), jnp.sin(ang)
    pad = lambda t, v: jnp.pad(jnp.concatenate([t, t], axis=1), ((0, 0), (ROPE_LO, HEAD_PAD - ROPE_HI)), constant_values=v)
    return pad(cos, 1.0), pad(sin, 0.0)


def _layer_fwd(x0, h, p_l, W, cs, sn, attend, g_next):
    z = _mm(h, W["w_in"], name="mm_in")
    qln, kvln = _lat_fwd(z, W["g_q_lat"], W["g_kv_lat"])
    q_raw = _mm(qln, W["w_uq"], name="mm_uq")
    kv_raw = _mm(kvln, W["w_ukv"], name="mm_ukv")
    qf, kf, vb = _qk_fwd(q_raw, kv_raw, z, cs, sn, W["gq"], W["gk"])
    o, lse = attend(qf, kf, vb)
    conv = _conv_fwd(z, W["cw8"])
    mixed = _mix_fwd(o, conv, W["g_out_attn"], W["g_out_conv"])
    x1, h2 = _mm(mixed, W["w_o"], res=x0, gain=W["g_mlp"], epi="rms", name="mm_o")
    a, f = _mm(h2, W["w_up"], epi="relu2", name="mm_up")
    x2, h3 = _mm(f, W["w_down"], res=x1, gain=W["g_ple"], epi="rms", name="mm_down")
    gl = _mm(h3, W["w_ple_gate"], name="mm_ple_gate")
    pe = _mm(p_l, W["w_ple"], name="mm_ple")
    x3, h_next = _ple_fwd(x2, gl, pe, g_next)
    saved = dict(x0=x0, h=h, z=z, qln=qln, kvln=kvln, q_raw=q_raw, kv_raw=kv_raw, qf=qf, kf=kf, vb=vb, o=o, lse=lse,
                 conv=conv, mixed=mixed, x1=x1, h2=h2, a=a, f=f, x2=x2, h3=h3, gl=gl, pe=pe)
    return x3, h_next, saved


def _layer_bwd(dx3, p_l, W, cs, sn, sv, attend_bwd):
    g = {}
    dpe, dgl = _ple_bwd(dx3, sv["gl"], sv["pe"])
    g["w_ple"] = _mm(p_l, dpe, mode="tn", out_dtype=BF16, shard_out=1, name="mm_dw_ple")
    g["w_ple_gate"] = _mm(sv["h3"], dgl, mode="tn", out_dtype=BF16, shard_out=0, name="mm_dw_ple_gate")
    dx2, dx2b, g["g_ple"] = _mm(dgl, W["w_ple_gate"], mode="nt", res=dx3, aux=sv["x2"], gain=W["g_ple"], epi="rms_bwd",
                                name="mm_dh3")
    da = _mm(dx2b, W["w_down"], mode="nt", aux=sv["a"], epi="drelu2", out_dtype=BF16, name="mm_da")
    g["w_down"] = _mm(sv["f"], dx2b, mode="tn", out_dtype=BF16, shard_out=0, name="mm_dw_down")
    g["w_up"] = _mm(sv["h2"], da, mode="tn", out_dtype=BF16, shard_out=1, name="mm_dw_up")
    dx1, dx1b, g["g_mlp"] = _mm(da, W["w_up"], mode="nt", res=dx2, aux=sv["x1"], gain=W["g_mlp"], epi="rms_bwd",
                                name="mm_dh2")
    dmixed = _mm(dx1b, W["w_o"], mode="nt", name="mm_dmixed")
    g["w_o"] = _mm(sv["mixed"], dx1b, mode="tn", out_dtype=BF16, name="mm_dw_o")
    do, dconv, g["g_out_attn"], g["g_out_conv"] = _mix_bwd(dmixed, sv["o"], sv["conv"], W["g_out_attn"], W["g_out_conv"])
    dgb, dgc, dxin, g["cw8"] = _conv_bwd(dconv, sv["z"], W["cw8"])
    dqf, dkf, dv = attend_bwd(g, sv["qf"], sv["kf"], sv["vb"], sv["o"], do, sv["lse"])
    dq_raw, dkv_raw, dkpe, g["gq"], g["gk"] = _qk_bwd(dqf, dkf, dv, sv["q_raw"], sv["kv_raw"], sv["z"], cs, sn, W["gq"], W["gk"])
    g["w_uq"] = _mm(sv["qln"], dq_raw, mode="tn", out_dtype=BF16, name="mm_dw_uq")
    dqln = _mm(dq_raw, W["w_uq"], mode="nt", name="mm_dqln")
    g["w_ukv"] = _mm(sv["kvln"], dkv_raw, mode="tn", out_dtype=BF16, name="mm_dw_ukv")
    dkvln = _mm(dkv_raw, W["w_ukv"], mode="nt", name="mm_dkvln")
    dlat, g["g_q_lat"], g["g_kv_lat"] = _lat_bwd(dqln, dkvln, sv["z"], W["g_q_lat"], W["g_kv_lat"])
    dz = jnp.concatenate([dlat, dkpe, dgb, dgc, dxin], axis=1)
    g["w_in"] = _mm(sv["h"], dz, mode="tn", out_dtype=BF16, name="mm_dw_in")
    dx0, _, g["g_mix"] = _mm(dz, W["w_in"], mode="nt", res=dx1, aux=sv["x0"], gain=W["g_mix"], epi="rms_bwd", name="mm_dh")
    return dx0, g


def _local_step(x, p, positions, target, slabs, small):
    depth = p.shape[0]
    cs, sn = _rope_tables(positions)
    slabs = dict(slabs)

    def gather(items, host):
        touched = [n for n in PRE + POST if any(n == name for name, _ in items)]
        rider = _allgather_rider([slabs[n] for n in touched], [(touched.index(name), l) for name, l in items])
        if host is None:
            out, new = (), rider.run("allgather_first")
        else:
            *out, new = host(rider)
        slabs.update(zip(touched, new))
        return out

    gather([(name, 0) for name in PRE], None)
    Ws, saved = [], []
    gains = [_gains(small, l) for l in range(depth)]
    h = _rms_fwd(x, gains[0]["g_mix"], "rms_mix")
    for l in range(depth):
        W = dict(gains[l], **_pre_weights(slabs, l))

        def attend(qf, kf, vb, W=W, l=l):
            items = [(name, l) for name in POST] + ([(name, l + 1) for name in PRE] if l + 1 < depth else [])
            o, lse = gather(items, functools.partial(_attn_fwd, qf, kf, vb))
            W.update(_post_weights(slabs, l))
            return o, lse

        g_next = gains[l + 1]["g_mix"] if l + 1 < depth else jnp.ones_like(gains[l]["g_mix"])
        x, h, sv = _layer_fwd(x, h, p[l], W, cs, sn, attend, g_next)
        Ws.append(W)
        saved.append(sv)
    dx, sq = _loss_grad(x, target)

    landing = {name: lax.empty((depth, N_DEV - 1, slabs[name].shape[2] // 2, slabs[name].shape[3]), BF16) for name in PRE + POST}

    def exchange(sends, host):
        touched = [n for n in PRE + POST if any(n == name for name, _, _ in sends)]
        rider = _exchange_rider([g for _, _, g in sends], [landing[n] for n in touched],
                                [(i, touched.index(name), l) for i, (name, l, _) in enumerate(sends)])
        if host is None:
            out, new = (), rider.run("grad_exchange_last")
        else:
            *out, new = host(rider)
        landing.update(zip(touched, new))
        return out

    grads = [None] * depth
    for l in reversed(range(depth)):
        W = Ws[l]

        def attend_bwd(g, qf, kf, vb, o, do, lse, l=l):
            g["w_o"] = _w_o_shards(g["w_o"])
            sends = [(name, l, g[name]) for name in POST]
            if l + 1 < depth:
                sends += [(name, l + 1, grads[l + 1][name]) for name in PRE]
            return exchange(sends, functools.partial(_attn_bwd, qf, kf, vb, o, do, lse))

        dx, gp = _layer_bwd(dx, p[l], W, cs, sn, saved[l], attend_bwd)
        grads[l] = _unpad_grads(gp)
    exchange([(name, 0, grads[0][name]) for name in PRE], None)
    return sq, dx, grads, landing


def _pack_rows(vals, entries):
    depth = vals[entries[0][0]].shape[0]
    return jnp.concatenate([jnp.pad(vals[name].reshape(depth, -1), ((0, 0), (0, _pad128(n) - n))) for name, n in entries], axis=1)


def _unpack_rows(packed, entries):
    out, off = {}, 0
    for name, n in entries:
        out[name] = packed[:, off:off + n]
        off += _pad128(n)
    return out


def kernel(x, p, positions, g_mix, w_in, g_q_lat, w_uq, g_kv_lat, w_ukv, g_qn_nope, g_qn_rope, g_kn_nope, g_kn_rope, conv_w, g_out_attn, g_out_conv, w_o, g_mlp, w_up, w_down, g_ple, w_ple_gate, w_ple, loss_target, m_g_mix, m_w_in, m_g_q_lat, m_w_uq, m_g_kv_lat, m_w_ukv, m_g_qn_nope, m_g_qn_rope, m_g_kn_nope, m_g_kn_rope, m_conv_w, m_g_out_attn, m_g_out_conv, m_w_o, m_g_mlp, m_w_up, m_w_down, m_g_ple, m_w_ple_gate, m_w_ple, v_g_mix, v_w_in, v_g_q_lat, v_w_uq, v_g_kv_lat, v_w_ukv, v_g_qn_nope, v_g_qn_rope, v_g_kn_nope, v_g_kn_rope, v_conv_w, v_g_out_attn, v_g_out_conv, v_w_o, v_g_mlp, v_w_up, v_w_down, v_g_ple, v_w_ple_gate, v_w_ple):
    w = dict(g_mix=g_mix, w_in=w_in, g_q_lat=g_q_lat, w_uq=w_uq, g_kv_lat=g_kv_lat, w_ukv=w_ukv, g_qn_nope=g_qn_nope,
             g_qn_rope=g_qn_rope, g_kn_nope=g_kn_nope, g_kn_rope=g_kn_rope, conv_w=conv_w, g_out_attn=g_out_attn,
             g_out_conv=g_out_conv, w_o=w_o, g_mlp=g_mlp, w_up=w_up, w_down=w_down, g_ple=g_ple, w_ple_gate=w_ple_gate,
             w_ple=w_ple)
    m = dict(g_mix=m_g_mix, w_in=m_w_in, g_q_lat=m_g_q_lat, w_uq=m_w_uq, g_kv_lat=m_g_kv_lat, w_ukv=m_w_ukv,
             g_qn_nope=m_g_qn_nope, g_qn_rope=m_g_qn_rope, g_kn_nope=m_g_kn_nope, g_kn_rope=m_g_kn_rope, conv_w=m_conv_w,
             g_out_attn=m_g_out_attn, g_out_conv=m_g_out_conv, w_o=m_w_o, g_mlp=m_g_mlp, w_up=m_w_up, w_down=m_w_down,
             g_ple=m_g_ple, w_ple_gate=m_w_ple_gate, w_ple=m_w_ple)
    v = dict(g_mix=v_g_mix, w_in=v_w_in, g_q_lat=v_g_q_lat, w_uq=v_w_uq, g_kv_lat=v_g_kv_lat, w_ukv=v_w_ukv,
             g_qn_nope=v_g_qn_nope, g_qn_rope=v_g_qn_rope, g_kn_nope=v_g_kn_nope, g_kn_rope=v_g_kn_rope, conv_w=v_conv_w,
             g_out_attn=v_g_out_attn, g_out_conv=v_g_out_conv, w_o=v_w_o, g_mlp=v_g_mlp, w_up=v_w_up, w_down=v_w_down,
             g_ple=v_g_ple, w_ple_gate=v_w_ple_gate, w_ple=v_w_ple)
    depth = p.shape[0]
    ax, ay, ac = _place()
    chip = 2 * ax + ay
    c_arr = jnp.reshape(ac, (1,)).astype(jnp.int32)
    chip_arr = jnp.reshape(chip, (1,)).astype(jnp.int32)
    conv_shard =("conv_w", CONV_TAPS * CONV_WIDTH // N_CHIPS)
    conv_full = ("conv_w", CONV_TAPS * CONV_WIDTH)

    names = PRE + POST
    slabs = {name: _shard_slab(w[name], chip_arr, "shard_slab_" + name) for name in names}
    conv_rows = -(-depth * CONV_TAPS // SUBLANES) * SUBLANES
    conv_all = _allgather_small(jnp.pad(conv_w.reshape(depth * CONV_TAPS, LANES), ((0, conv_rows - depth * CONV_TAPS), (0, 0))),
                                "allgather_conv_w")
    conv_cat = jnp.concatenate([conv_all[2 * j, :depth * CONV_TAPS] for j in range(N_CHIPS)], axis=1)
    small = {name: w[name] for name, _ in SMALL}
    small["conv_w"] = conv_cat.reshape(depth, CONV_TAPS, CONV_WIDTH)

    sq, grad_x, grads, landing = _local_step(x[0], p[:, 0], positions[0], loss_target[0], slabs, small)

    halves = [_reduce_partials(jnp.stack([grads[l][name] for l in range(depth)]), landing[name], chip_arr, c_arr,
                               "grad_reduce_" + name) for name in names]
    g_out = dict(zip(names, _join_halves(halves)))

    reduced = SMALL + (conv_full, ("loss", LANES))
    stacked = {name: jnp.stack([grads[l][name] for l in range(depth)]) for name, _ in SMALL + (conv_full,)}
    stacked["loss"] = jnp.broadcast_to(sq[:1] * (0.5 / D_MODEL), (depth, LANES))
    g_small = _unpack_rows(_sum_devices(_allgather_small(_pack_rows(stacked, reduced), "allgather_small_grads")), reduced)
    loss = g_small["loss"][0, 0]
    for name, n in SMALL:
        g_out[name] = g_small[name]
    g_conv = g_small["conv_w"].reshape(depth, CONV_TAPS, CONV_WIDTH)
    g_out["conv_w"] = lax.dynamic_slice_in_dim(g_conv, chip * LANES, LANES, axis=2)

    delta, new_m, new_v = {}, {}, {}
    for name in names:
        delta[name], new_m[name], new_v[name] = _adamw(w[name], g_out[name], m[name], v[name], "adamw_" + name)
    local = SMALL + (conv_shard,)
    pack = lambda vals: _pack_rows(vals, local)[None]
    d, m2, v2 = _adamw(pack(w), pack(g_out), pack(m), pack(v), "adamw_small")
    for res, packed_res in ((delta, d), (new_m, m2), (new_v, v2)):
        un = _unpack_rows(packed_res[0], local)
        for name, _ in local:
            res[name] = un[name].reshape(w[name].shape)

    return (loss, grad_x[None], *[g_out[n] for n in WEIGHT_ORDER], *[delta[n] for n in WEIGHT_ORDER],
            *[new_m[n] for n in WEIGHT_ORDER], *[new_v[n] for n in WEIGHT_ORDER])
```

```python
import functools

import jax
import jax.numpy as jnp
from jax import lax
from jax.experimental import pallas as pl
from jax.experimental.pallas import tpu as pltpu

F32 = jnp.float32
BF16 = jnp.bfloat16
MESH = pl.DeviceIdType.MESH

D_MODEL = 1024
N_HEADS = 8
QK_NOPE = 64
QK_ROPE = 32
QK_HEAD = QK_NOPE + QK_ROPE
V_HEAD = 64
Q_LORA = 384
KV_LORA = 256
ATTN_WIDTH = N_HEADS * V_HEAD
CONV_WIDTH = 512
CONV_TAPS = 3
D_FF = 4096
PLE_DIM = 256
ROPE_THETA = 10000.0
EPS = 1e-6
ATT_SCALE = QK_HEAD ** -0.5
LOG2E = 1.4426950408889634
ATT_SCALE_LOG2 = ATT_SCALE * LOG2E

ADAM_LR = 0.001
ADAM_B1 = 0.9
ADAM_B2 = 0.999
ADAM_EPS = 1e-08
ADAM_WD = 0.01
ADAM_STEP = 10

LANES = 128
SUBLANES = 8
HEAD_PAD = LANES
HP = N_HEADS * HEAD_PAD
ROPE_LO = QK_NOPE
ROPE_MID = QK_NOPE + QK_ROPE // 2
ROPE_HI = QK_NOPE + QK_ROPE
VMEM_LIMIT = 56 * 1024 * 1024

Z_Q, Z_KV, Z_KPE, Z_GB, Z_GC, Z_XIN = 0, 384, 640, 768, 1280, 1792
Z_COLS = 2304
Z_LAT = Z_KPE

ROW_TILE = 512
ATT_TILE = 256
ATT_CHAINS_FWD, ATT_CHAINS_BWD = 8, 4
MM_TM, MM_TN, MM_TK = 1024, 1024, 4096
MM_TM_MIN, MM_TK_MIN = 256, 512
MM_VMEM_BUDGET = 40 * 1024 * 1024

N_CHIPS = 4
N_DEV = 8

PRE = ("w_in", "w_uq", "w_ukv")
POST = ("w_o", "w_up", "w_down", "w_ple_gate", "w_ple")
SMALL = (
    ("g_mix", 1024), ("g_q_lat", 384), ("g_kv_lat", 256), ("g_qn_nope", 64), ("g_qn_rope", 32), ("g_kn_nope", 64),
    ("g_kn_rope", 32), ("g_out_attn", 512), ("g_out_conv", 512), ("g_mlp", 1024), ("g_ple", 1024),
)
WEIGHT_ORDER = ("g_mix", "w_in", "g_q_lat", "w_uq", "g_kv_lat", "w_ukv", "g_qn_nope", "g_qn_rope", "g_kn_nope",
                "g_kn_rope", "conv_w", "g_out_attn", "g_out_conv", "w_o", "g_mlp", "w_up", "w_down", "g_ple",
                "w_ple_gate", "w_ple")


def _pallas(body, **kw):
    return pl.pallas_call(body, **kw)


def _cparams(*sem):
    return pltpu.CompilerParams(dimension_semantics=sem, vmem_limit_bytes=VMEM_LIMIT)


def _tile(dim, pref, unit=LANES):
    if dim <= pref:
        return dim
    t = (pref // unit) * unit
    while t > unit and dim % t:
        t -= unit
    assert dim % t == 0, (dim, pref)
    return t


def _pad128(n):
    return -(-n // LANES) * LANES


_DIMS = {"nn": (((1,), (0,)), ((), ())), "nt": (((1,), (1,)), ((), ())), "tn": (((0,), (0,)), ((), ()))}


def _mm(a, b, *, mode="nn", res=None, aux=None, gain=None, epi=None, out_dtype=F32, shard_out=None, name):
    whole_rows = epi in ("rms", "rms_bwd")
    if mode == "nn":
        (M, K), (K2, N) = a.shape, b.shape
    elif mode == "nt":
        (M, K), (N, K2) = a.shape, b.shape
    else:
        (K, M), (K2, N) = a.shape, b.shape
    assert K == K2, (a.shape, b.shape, mode)
    n_lim = k_lim = None
    m_lim = M // N_CHIPS if shard_out == 0 else None
    if shard_out == 1:
        n_lim = N // N_CHIPS
    extra = [t for t in (res, aux) if t is not None]
    out_bytes = 6 if epi in ("relu2", "rms", "rms_bwd") else jnp.dtype(out_dtype).itemsize
    if whole_rows:
        assert (n_lim or N) == N and N <= MM_TN and shard_out is None and gain is not None
        m_lim = min(M, MM_TM // 2)

    def footprint(tm, tn, tk):
        blocks = tm * tk * a.dtype.itemsize + tk * tn * 2 + tm * tn * (out_bytes + 4 * len(extra))
        return 2 * blocks + (tm * tn * 4 if tk < K else 0)

    tm, tn, tk = _tile(m_lim or M, MM_TM), _tile(n_lim or N, MM_TN), _tile(k_lim or K, MM_TK)
    while footprint(tm, tn, tk) > MM_VMEM_BUDGET and tk > MM_TK_MIN:
        tk = _tile(K, tk // 2)
    while footprint(tm, tn, tk) > MM_VMEM_BUDGET and tm > MM_TM_MIN:
        tm = _tile(M, tm // 2)
    nk = K // tk
    a_spec = pl.BlockSpec((tk, tm), lambda i, j, k: (k, i)) if mode == "tn" else pl.BlockSpec((tm, tk), lambda i, j, k: (i, k))
    if mode == "nt":
        b_block, b_rc = (tn, tk), (lambda i, j, k: (j, k))
    else:
        b_block, b_rc = (tk, tn), (lambda i, j, k: (k, j))
    b_spec = pl.BlockSpec(b_block, b_rc)
    mn_spec = pl.BlockSpec((tm, tn), lambda i, j, k: (i, j))
    dims = _DIMS[mode]
    n_out = {"relu2": 2, "rms": 2, "rms_bwd": 3}.get(epi, 1)

    def body(*refs):
        a_ref, b_ref = refs[0], refs[1]
        extra_refs = refs[2:2 + len(extra)]
        gain_ref = refs[2 + len(extra)] if whole_rows else None
        out_refs = refs[2 + len(extra) + whole_rows:][:n_out]
        prod = lax.dot_general(a_ref[...].astype(BF16), b_ref[...].astype(BF16), dims, preferred_element_type=F32)

        def finish(r):
            if epi == "rms_bwd":
                dx, dgc = _norm_bwd(extra_refs[1][...], r, gain_ref[...], N)
                dx = dx + extra_refs[0][...]
                out_refs[0][...] = dx
                out_refs[1][...] = dx.astype(BF16)
                _accumulate(out_refs[2], dgc)
                return
            if res is not None:
                r = r + extra_refs[0][...]
            if epi == "rms":
                out_refs[0][...] = r
                out_refs[1][...] = (r * _rinv(r, N) * gain_ref[...]).astype(BF16)
            elif epi == "relu2":
                out_refs[0][...] = r
                t = jnp.maximum(r, 0.0)
                out_refs[1][...] = (t * t).astype(BF16)
            elif epi == "drelu2":
                out_refs[0][...] = (r * (2.0 * jnp.maximum(extra_refs[-1][...], 0.0))).astype(out_dtype)
            else:
                out_refs[0][...] = r.astype(out_dtype)

        if nk == 1:
            finish(prod)
        else:
            acc = refs[-1]
            k = pl.program_id(2)

            @pl.when(k == 0)
            def _():
                acc[...] = prod

            @pl.when(k > 0)
            def _():
                acc[...] += prod

            @pl.when(k == nk - 1)
            def _():
                finish(acc[...])

    if epi in ("relu2", "rms", "rms_bwd"):
        out_shape = [jax.ShapeDtypeStruct((M, N), F32), jax.ShapeDtypeStruct((M, N), BF16)]
        out_specs = [mn_spec, mn_spec]
        if epi == "rms_bwd":
            out_shape.append(jax.ShapeDtypeStruct((SUBLANES, N), F32))
            out_specs.append(pl.BlockSpec((SUBLANES, tn), lambda i, j, k: (0, j)))
    elif shard_out == 0:
        per = (M // N_CHIPS) // tm
        out_shape = jax.ShapeDtypeStruct((N_CHIPS, M // N_CHIPS, N), out_dtype)
        out_specs = pl.BlockSpec((None, tm, tn), lambda i, j, k: (i // per, i % per, j))
    elif shard_out == 1:
        per = (N // N_CHIPS) // tn
        out_shape = jax.ShapeDtypeStruct((N_CHIPS, M, N // N_CHIPS), out_dtype)
        out_specs = pl.BlockSpec((None, tm, tn), lambda i, j, k: (j // per, i, j % per))
    else:
        out_shape = jax.ShapeDtypeStruct((M, N), out_dtype)
        out_specs = mn_spec
    gains = [gain] if whole_rows else []
    return _pallas(
        body, name=name, grid=(M // tm, N // tn, nk),
        in_specs=[a_spec, b_spec] + [mn_spec] * len(extra) + [pl.BlockSpec((1, tn), lambda i, j, k: (0, j))] * len(gains),
        out_specs=out_specs, out_shape=out_shape,
        scratch_shapes=[pltpu.VMEM((tm, tn), F32)] if nk > 1 else [],
        compiler_params=_cparams(*(["arbitrary"] * 3 if epi == "rms_bwd" else ["parallel", "parallel", "arbitrary"])),
    )(a, b, *extra, *gains)


def _rows(ts, d, col=0):
    return pl.BlockSpec((ts, d), lambda i: (i, col))


def _gain(d):
    return pl.BlockSpec((1, d), lambda i: (0, 0))


def _accum(d):
    return pl.BlockSpec((SUBLANES, d), lambda i: (0, 0))


def _accumulate(ref, val):
    i = pl.program_id(0)

    @pl.when(i == 0)
    def _():
        ref[...] = jnp.zeros_like(ref)

    ref[...] += jnp.broadcast_to(jnp.sum(val, axis=0, keepdims=True), ref.shape)


def _rinv(x, n):
    return lax.rsqrt(jnp.sum(x * x, axis=-1, keepdims=True) / n + EPS)


def _norm_bwd(x, dy, g, n):
    r = _rinv(x, n)
    xhat = x * r
    dyg = dy * g
    dx = r * (dyg - xhat * (jnp.sum(dyg * xhat, axis=-1, keepdims=True) / n))
    return dx, dy * xhat


def _rms_fwd(x, g, name):
    S, D = x.shape
    ts = _tile(S, ROW_TILE, SUBLANES)

    def body(x_ref, g_ref, h_ref):
        xv = x_ref[...]
        h_ref[...] = (xv * _rinv(xv, D) * g_ref[...]).astype(BF16)

    return _pallas(body, name=name, grid=(S // ts,), in_specs=[_rows(ts, D), _gain(D)], out_specs=_rows(ts, D),
                   out_shape=jax.ShapeDtypeStruct((S, D), BF16), compiler_params=_cparams("parallel"))(x, g)


def _lat_fwd(z, gq, gkv):
    S = z.shape[0]
    ts = _tile(S, ROW_TILE, SUBLANES)

    def body(z_ref, gq_ref, gkv_ref, q_ref, kv_ref):
        zq = z_ref[:, Z_Q:Z_KV]
        zkv = z_ref[:, Z_KV:Z_KPE]
        q_ref[...] = (zq * _rinv(zq, Q_LORA) * gq_ref[...]).astype(BF16)
        kv_ref[...] = (zkv * _rinv(zkv, KV_LORA) * gkv_ref[...]).astype(BF16)

    return _pallas(body, name="lat_fwd", grid=(S // ts,), in_specs=[_rows(ts, Z_LAT), _gain(Q_LORA), _gain(KV_LORA)],
                   out_specs=[_rows(ts, Q_LORA), _rows(ts, KV_LORA)],
                   out_shape=[jax.ShapeDtypeStruct((S, Q_LORA), BF16), jax.ShapeDtypeStruct((S, KV_LORA), BF16)],
                   compiler_params=_cparams("parallel"))(z, gq, gkv)


def _lat_bwd(dq, dkv, z, gq, gkv):
    S = z.shape[0]
    ts = _tile(S, ROW_TILE, SUBLANES)

    def body(dq_ref, dkv_ref, z_ref, gq_ref, gkv_ref, dlat_ref, dgq_ref, dgkv_ref):
        dxq, cq = _norm_bwd(z_ref[:, Z_Q:Z_KV], dq_ref[...], gq_ref[...], Q_LORA)
        dxkv, ckv = _norm_bwd(z_ref[:, Z_KV:Z_KPE], dkv_ref[...], gkv_ref[...], KV_LORA)
        dlat_ref[:, Z_Q:Z_KV] = dxq.astype(BF16)
        dlat_ref[:, Z_KV:Z_KPE] = dxkv.astype(BF16)
        _accumulate(dgq_ref, cq)
        _accumulate(dgkv_ref, ckv)

    return _pallas(body, name="lat_bwd", grid=(S // ts,),
                   in_specs=[_rows(ts, Q_LORA), _rows(ts, KV_LORA), _rows(ts, Z_LAT), _gain(Q_LORA), _gain(KV_LORA)],
                   out_specs=[_rows(ts, Z_LAT), _accum(Q_LORA), _accum(KV_LORA)],
                   out_shape=[jax.ShapeDtypeStruct((S, Z_LAT), BF16), jax.ShapeDtypeStruct((SUBLANES, Q_LORA), F32),
                              jax.ShapeDtypeStruct((SUBLANES, KV_LORA), F32)],
                   compiler_params=_cparams("arbitrary"))(dq, dkv, z, gq, gkv)


def _head_masks():
    lane = lax.broadcasted_iota(jnp.int32, (1, HEAD_PAD), 1)
    return lane < ROPE_LO, (lane >= ROPE_LO) & (lane < ROPE_HI), lane < ROPE_MID


def _seg_sum(t, m_n, m_r):
    sn = jnp.sum(jnp.where(m_n, t, 0.0), axis=-1, keepdims=True) / QK_NOPE
    sr = jnp.sum(jnp.where(m_r, t, 0.0), axis=-1, keepdims=True) / QK_ROPE
    return jnp.where(m_n, sn, sr)


def _rot_half(y, lo):
    half = QK_ROPE // 2
    return jnp.where(lo, -pltpu.roll(y, HEAD_PAD - half, 1), pltpu.roll(y, half, 1))


def _qk_fwd(q_raw, kv_raw, z, cs, sn, gq, gk):
    S = q_raw.shape[0]
    ts = _tile(S, ROW_TILE, SUBLANES)

    def body(q_ref, k_ref, v_ref, kpe_ref, cs_ref, sn_ref, gq_ref, gk_ref, qf_ref, kf_ref, vb_ref):
        m_n, m_r, lo = _head_masks()
        cos, sin, gqv, gkv = cs_ref[...], sn_ref[...], gq_ref[...], gk_ref[...]

        def norm(x, g):
            return x * lax.rsqrt(_seg_sum(x * x, m_n, m_r) + EPS) * g

        def rope(y):
            return y * cos + _rot_half(y, lo) * sin

        kr = rope(norm(kpe_ref[...], gkv))
        lane = lax.broadcasted_iota(jnp.int32, (1, HEAD_PAD), 1)
        for h in range(N_HEADS):
            sl = slice(h * HEAD_PAD, (h + 1) * HEAD_PAD)
            qf_ref[:, sl] = rope(norm(q_ref[:, sl], gqv)).astype(BF16)
            kf_ref[:, sl] = (norm(k_ref[:, sl], gkv) + kr).astype(BF16)
            vb_ref[:, sl] = jnp.where(lane == V_HEAD, 1.0, v_ref[:, sl]).astype(BF16)

    hd = jax.ShapeDtypeStruct((S, HP), BF16)
    return _pallas(body, name="qk_fwd", grid=(S // ts,),
                   in_specs=[_rows(ts, HP), _rows(ts, HP, 0), _rows(ts, HP, 1), _rows(ts, HEAD_PAD, Z_KPE // HEAD_PAD),
                             _rows(ts, HEAD_PAD), _rows(ts, HEAD_PAD), _gain(HEAD_PAD), _gain(HEAD_PAD)],
                   out_specs=[_rows(ts, HP)] * 3, out_shape=[hd, hd, hd],
                   compiler_params=_cparams("parallel"))(q_raw, kv_raw, kv_raw, z, cs, sn, gq, gk)


def _qk_bwd(dqf, dkf, dv, q_raw, kv_raw, z, cs, sn, gq, gk):
    S = q_raw.shape[0]
    ts = _tile(S, ROW_TILE, SUBLANES)

    def body(dqf_ref, dkf_ref, dv_ref, q_ref, k_ref, kpe_ref, cs_ref, sn_ref, gq_ref, gk_ref,
             dq_ref, dkv_ref, dkpe_ref, dgq_ref, dgk_ref):
        m_n, m_r, lo = _head_masks()
        cos, sin, gqv, gkv = cs_ref[...], sn_ref[...], gq_ref[...], gk_ref[...]

        def rope_t(w):
            return w * cos - jnp.where(m_r, _rot_half(w * sin, lo), 0.0)

        def norm_bwd(x, dy, g):
            r = lax.rsqrt(_seg_sum(x * x, m_n, m_r) + EPS)
            xhat = x * r
            dyg = dy * g
            return r * (dyg - xhat * _seg_sum(dyg * xhat, m_n, m_r)), dy * xhat

        accq = jnp.zeros((ts, HEAD_PAD), F32)
        acck = jnp.zeros((ts, HEAD_PAD), F32)
        dkr = jnp.zeros((ts, HEAD_PAD), F32)
        for h in range(N_HEADS):
            sl = slice(h * HEAD_PAD, (h + 1) * HEAD_PAD)
            dx, c = norm_bwd(q_ref[:, sl], rope_t(dqf_ref[:, sl]), gqv)
            dq_ref[:, sl] = dx.astype(BF16)
            accq = accq + c
            dk = dkf_ref[:, sl]
            dkr = dkr + jnp.where(m_r, dk, 0.0)
            dx, c = norm_bwd(k_ref[:, sl], jnp.where(m_n, dk, 0.0), gkv)
            dkv_ref[:, sl] = dx.astype(BF16)
            acck = acck + c
            dkv_ref[:, HP + h * HEAD_PAD:HP + (h + 1) * HEAD_PAD] = dv_ref[:, sl].astype(BF16)
        dx, c = norm_bwd(kpe_ref[...], rope_t(dkr), gkv)
        dkpe_ref[...] = dx.astype(BF16)
        _accumulate(dgq_ref, accq)
        _accumulate(dgk_ref, acck + c)

    return _pallas(body, name="qk_bwd", grid=(S // ts,),
                   in_specs=[_rows(ts, HP)] * 4 + [_rows(ts, HP, 0), _rows(ts, HEAD_PAD, Z_KPE // HEAD_PAD),
                                                   _rows(ts, HEAD_PAD), _rows(ts, HEAD_PAD), _gain(HEAD_PAD), _gain(HEAD_PAD)],
                   out_specs=[_rows(ts, HP), _rows(ts, 2 * HP), _rows(ts, HEAD_PAD), _accum(HEAD_PAD), _accum(HEAD_PAD)],
                   out_shape=[jax.ShapeDtypeStruct((S, HP), BF16), jax.ShapeDtypeStruct((S, 2 * HP), BF16),
                              jax.ShapeDtypeStruct((S, HEAD_PAD), BF16), jax.ShapeDtypeStruct((SUBLANES, HEAD_PAD), F32),
                              jax.ShapeDtypeStruct((SUBLANES, HEAD_PAD), F32)],
                   compiler_params=_cparams("arbitrary"))(dqf, dkf, dv, q_raw, kv_raw, z, cs, sn, gq, gk)


def _causal(t):
    row = lax.broadcasted_iota(jnp.int32, (t, t), 0)
    col = lax.broadcasted_iota(jnp.int32, (t, t), 1)
    return col <= row


def _attn_tiles(S, chains):
    t = _tile(S, ATT_TILE, SUBLANES)
    return t, min(chains, S // t)


def _attn_fwd(qf, kf, vb, rider):
    S = qf.shape[0]
    t, nc = _attn_tiles(S, ATT_CHAINS_FWD)
    tq = nc * t
    nqt = S // tq

    def body(q_ref, k_ref, v_ref, *rest):
        (o_ref, lse_ref, m_ref, acc_ref), ride = rider.split(rest, 2, 2)
        qt = pl.program_id(1)
        ride(pl.program_id(0) * nqt + qt, N_HEADS * nqt)
        m_ref[...] = jnp.full_like(m_ref, -jnp.inf)
        acc_ref[...] = jnp.zeros_like(acc_ref)

        def against(rows, first, diagonal):
            kb, vb_ = k_ref[rows, :], v_ref[rows, :]
            subs = [slice(a * t, (a + 1) * t) for a in range(first, nc)]
            s = [lax.dot_general(q_ref[sub, :], kb, _DIMS["nt"], preferred_element_type=F32) * ATT_SCALE_LOG2 for sub in subs]
            if diagonal:
                s[0] = jnp.where(_causal(t), s[0], -jnp.inf)
            m_old = [m_ref[sub, :] for sub in subs]
            m_new = [jnp.maximum(mo, jnp.max(sa, axis=-1, keepdims=True)) for mo, sa in zip(m_old, s)]
            p = [jnp.exp2(sa - jnp.concatenate([mn] * (t // HEAD_PAD), axis=1)).astype(BF16) for sa, mn in zip(s, m_new)]
            for sub, mo, mn, pa in zip(subs, m_old, m_new, p):
                acc_ref[sub, :] = jnp.exp2(mo - mn) * acc_ref[sub, :] + jnp.dot(pa, vb_, preferred_element_type=F32)
                m_ref[sub, :] = mn

        def trip(j, carry):
            against(pl.ds(pl.multiple_of(j * t, t), t), 0, False)
            return carry

        lax.fori_loop(0, nc * qt, trip, 0)
        for d in range(nc):
            against(pl.ds(pl.multiple_of((nc * qt + d) * t, t), t), d, True)
        acc = acc_ref[...]
        l = acc[:, V_HEAD:V_HEAD + 1]
        lane = lax.broadcasted_iota(jnp.int32, (1, HEAD_PAD), 1)
        o_ref[...] = jnp.where(lane < V_HEAD, acc / l, 0.0)
        lse_ref[...] = m_ref[...] + jnp.log(l) * LOG2E

    tile = pl.BlockSpec((tq, HEAD_PAD), lambda h, i: (i, h))
    full = pl.BlockSpec((S, HEAD_PAD), lambda h, i: (0, h))
    outs = _pallas(body, name="attn_fwd", grid=(N_HEADS, nqt), in_specs=[tile, full, full] + rider.in_specs(),
                   out_specs=[tile, pl.BlockSpec((None, tq, HEAD_PAD), lambda h, i: (h, i, 0))] + rider.out_specs(),
                   out_shape=[jax.ShapeDtypeStruct((S, HP), F32), jax.ShapeDtypeStruct((N_HEADS, S, HEAD_PAD), F32)]
                   + rider.out_shape(),
                   scratch_shapes=[pltpu.VMEM((tq, HEAD_PAD), F32), pltpu.VMEM((tq, HEAD_PAD), F32)] + rider.scratch(),
                   input_output_aliases=rider.aliases(3, 2),
                   compiler_params=_cparams("arbitrary", "arbitrary"))(qf, kf, vb, *rider.args())
    return outs[0], outs[1], outs[2:]


def _attn_bwd(qf, kf, vb, o, do, lse, rider):
    S = qf.shape[0]
    t, nc = _attn_tiles(S, ATT_CHAINS_BWD)
    tkv = nc * t
    nq = S // t
    nkt = S // tkv

    def body(q_ref, k_ref, v_ref, o_ref, do_ref, lse_ref, *rest):
        (dq_ref, dk_ref, dv_ref), ride = rider.split(rest, 3, 0)
        kt = pl.program_id(1)
        ride(pl.program_id(0) * nkt + kt, N_HEADS * nkt)

        @pl.when(kt == 0)
        def _():
            dq_ref[...] = jnp.zeros_like(dq_ref)

        dk_ref[...] = jnp.zeros_like(dk_ref)
        dv_ref[...] = jnp.zeros_like(dv_ref)

        def q_block(rows):
            dof = do_ref[rows, :]
            delta = jnp.sum(dof * o_ref[rows, :], axis=-1, keepdims=True)
            return q_ref[rows, :], dof.astype(BF16), lse_ref[rows, :][:, :1], delta

        def against(rows, n_sub, diagonal):
            q, dob, lse, delta = q_block(rows)
            subs = [slice(b * t, (b + 1) * t) for b in range(n_sub)]
            kbs = [k_ref[sub, :] for sub in subs]
            s = [lax.dot_general(q, kb, _DIMS["nt"], preferred_element_type=F32) for kb in kbs]
            dp = [lax.dot_general(dob, v_ref[sub, :], _DIMS["nt"], preferred_element_type=F32) for sub in subs]
            p = [jnp.exp2(sb * ATT_SCALE_LOG2 - lse) for sb in s]
            if diagonal:
                p[-1] = jnp.where(_causal(t), p[-1], 0.0)
            ds = [(pb * (dpb - delta) * ATT_SCALE).astype(BF16) for pb, dpb in zip(p, dp)]
            for sub, pb in zip(subs, p):
                dv_ref[sub, :] += lax.dot_general(pb.astype(BF16), dob, _DIMS["tn"], preferred_element_type=F32)
            for sub, dsb in zip(subs, ds):
                dk_ref[sub, :] += lax.dot_general(dsb, q, _DIMS["tn"], preferred_element_type=F32)
            dq_ref[rows, :] += sum(jnp.dot(dsb, kb, preferred_element_type=F32) for dsb, kb in zip(ds, kbs))

        for a in range(nc):
            against(pl.ds(pl.multiple_of((nc * kt + a) * t, t), t), a + 1, True)

        def trip(i, carry):
            against(pl.ds(pl.multiple_of(i * t, t), t), nc, False)
            return carry

        lax.fori_loop(nc * (kt + 1), nq, trip, 0)

    tile = pl.BlockSpec((tkv, HEAD_PAD), lambda h, j: (j, h))
    full = pl.BlockSpec((S, HEAD_PAD), lambda h, j: (0, h))
    hd = jax.ShapeDtypeStruct((S, HP), F32)
    outs = _pallas(body, name="attn_bwd", grid=(N_HEADS, nkt),
                   in_specs=[full, tile, tile, full, full, pl.BlockSpec((None, S, HEAD_PAD), lambda h, j: (h, 0, 0))]
                   + rider.in_specs(),
                   out_specs=[full, tile, tile] + rider.out_specs(), out_shape=[hd, hd, hd] + rider.out_shape(),
                   scratch_shapes=rider.scratch(), input_output_aliases=rider.aliases(6, 3),
                   compiler_params=_cparams("arbitrary", "arbitrary"))(qf, kf, vb, o, do, lse, *rider.args())
    return outs[0], outs[1], outs[2], outs[3:]


def _shift_down(u, j, row):
    return jnp.where(row >= j, pltpu.roll(u, j, 0), 0.0)


def _shift_up(u, j, row, s):
    return jnp.where(row < s - j, pltpu.roll(u, s - j, 0), 0.0)


def _conv_cols(s, first_tile):
    return pl.BlockSpec((s, LANES), lambda cb: (0, first_tile + cb))


def _conv_fwd(z, cw8):
    S = z.shape[0]

    def body(gb_ref, gc_ref, xin_ref, w_ref, out_ref):
        row = lax.broadcasted_iota(jnp.int32, (S, LANES), 0)
        u = gc_ref[...] * xin_ref[...]
        y = w_ref[0:1, :] * u
        for j in range(1, CONV_TAPS):
            y = y + w_ref[j:j + 1, :] * _shift_down(u, j, row)
        out_ref[...] = gb_ref[...] * y

    return _pallas(body, name="conv_fwd", grid=(CONV_WIDTH // LANES,),
                   in_specs=[_conv_cols(S, Z_GB // LANES), _conv_cols(S, Z_GC // LANES), _conv_cols(S, Z_XIN // LANES),
                             pl.BlockSpec((SUBLANES, LANES), lambda cb: (0, cb))],
                   out_specs=_conv_cols(S, 0), out_shape=jax.ShapeDtypeStruct((S, CONV_WIDTH), F32),
                   compiler_params=_cparams("parallel"))(z, z, z, cw8)


def _conv_bwd(dconv, z, cw8):
    S = z.shape[0]

    def body(d_ref, gb_ref, gc_ref, xin_ref, w_ref, dgb_ref, dgc_ref, dxin_ref, dw_ref):
        row = lax.broadcasted_iota(jnp.int32, (S, LANES), 0)
        gc, xin, d = gc_ref[...], xin_ref[...], d_ref[...]
        u = gc * xin
        dy = d * gb_ref[...]
        y = w_ref[0:1, :] * u
        du = w_ref[0:1, :] * dy
        dw = [jnp.sum(dy * u, axis=0, keepdims=True)]
        for j in range(1, CONV_TAPS):
            uj = _shift_down(u, j, row)
            y = y + w_ref[j:j + 1, :] * uj
            du = du + w_ref[j:j + 1, :] * _shift_up(dy, j, row, S)
            dw.append(jnp.sum(dy * uj, axis=0, keepdims=True))
        dgb_ref[...] = (d * y).astype(BF16)
        dgc_ref[...] = (du * xin).astype(BF16)
        dxin_ref[...] = (du * gc).astype(BF16)
        tap = lax.broadcasted_iota(jnp.int32, (SUBLANES, LANES), 0)
        dw_ref[...] = sum(jnp.where(tap == j, dw[j], 0.0) for j in range(CONV_TAPS))

    col = _conv_cols(S, 0)
    sd = jax.ShapeDtypeStruct((S, CONV_WIDTH), BF16)
    return _pallas(body, name="conv_bwd", grid=(CONV_WIDTH // LANES,),
                   in_specs=[col, _conv_cols(S, Z_GB // LANES), _conv_cols(S, Z_GC // LANES), _conv_cols(S, Z_XIN // LANES),
                             pl.BlockSpec((SUBLANES, LANES), lambda cb: (0, cb))],
                   out_specs=[col, col, col, pl.BlockSpec((SUBLANES, LANES), lambda cb: (0, cb))],
                   out_shape=[sd, sd, sd, jax.ShapeDtypeStruct((SUBLANES, CONV_WIDTH), F32)],
                   compiler_params=_cparams("parallel"))(dconv, z, z, z, cw8)


def _mix_fwd(o, conv, ga, gc):
    S = o.shape[0]
    ts = _tile(S, ROW_TILE, SUBLANES)

    def body(o_ref, c_ref, ga_ref, gc_ref, out_ref):
        ov, cv = o_ref[...], c_ref[...]
        out_ref[:, :HP] = (ov * _rinv(ov, ATTN_WIDTH) * ga_ref[...]).astype(BF16)
        out_ref[:, HP:] = (cv * _rinv(cv, CONV_WIDTH) * gc_ref[...]).astype(BF16)

    return _pallas(body, name="mix_fwd", grid=(S // ts,),
                   in_specs=[_rows(ts, HP), _rows(ts, CONV_WIDTH), _gain(HP), _gain(CONV_WIDTH)],
                   out_specs=_rows(ts, HP + CONV_WIDTH), out_shape=jax.ShapeDtypeStruct((S, HP + CONV_WIDTH), BF16),
                   compiler_params=_cparams("parallel"))(o, conv, ga, gc)


def _mix_bwd(dmixed, o, conv, ga, gc):
    S = o.shape[0]
    ts = _tile(S, ROW_TILE, SUBLANES)

    def body(d_ref, o_ref, c_ref, ga_ref, gc_ref, do_ref, dc_ref, dga_ref, dgc_ref):
        dx, ca = _norm_bwd(o_ref[...], d_ref[:, :HP], ga_ref[...], ATTN_WIDTH)
        do_ref[...] = dx
        dx, cc = _norm_bwd(c_ref[...], d_ref[:, HP:], gc_ref[...], CONV_WIDTH)
        dc_ref[...] = dx
        _accumulate(dga_ref, ca)
        _accumulate(dgc_ref, cc)

    return _pallas(body, name="mix_bwd", grid=(S // ts,),
                   in_specs=[_rows(ts, HP + CONV_WIDTH), _rows(ts, HP), _rows(ts, CONV_WIDTH), _gain(HP), _gain(CONV_WIDTH)],
                   out_specs=[_rows(ts, HP), _rows(ts, CONV_WIDTH), _accum(HP), _accum(CONV_WIDTH)],
                   out_shape=[jax.ShapeDtypeStruct((S, HP), F32), jax.ShapeDtypeStruct((S, CONV_WIDTH), F32),
                              jax.ShapeDtypeStruct((SUBLANES, HP), F32), jax.ShapeDtypeStruct((SUBLANES, CONV_WIDTH), F32)],
                   compiler_params=_cparams("arbitrary"))(dmixed, o, conv, ga, gc)


def _ple_fwd(x, gl, pe, g_next):
    S, D = x.shape
    ts = _tile(S, ROW_TILE, SUBLANES)

    def body(x_ref, gl_ref, pe_ref, g_ref, out_ref, h_ref):
        y = x_ref[...] + jax.nn.sigmoid(gl_ref[...]) * pe_ref[...]
        out_ref[...] = y
        h_ref[...] = (y * _rinv(y, D) * g_ref[...]).astype(BF16)

    return _pallas(body, name="ple_fwd", grid=(S // ts,), in_specs=[_rows(ts, D)] * 3 + [_gain(D)], out_specs=[_rows(ts, D)] * 2,
                   out_shape=[jax.ShapeDtypeStruct((S, D), F32), jax.ShapeDtypeStruct((S, D), BF16)],
                   compiler_params=_cparams("parallel"))(x, gl, pe, g_next)


def _ple_bwd(dx, gl, pe):
    S, D = dx.shape
    ts = _tile(S, ROW_TILE, SUBLANES)

    def body(dx_ref, gl_ref, pe_ref, dpe_ref, dgl_ref):
        d = dx_ref[...]
        gate = jax.nn.sigmoid(gl_ref[...])
        dpe_ref[...] = (d * gate).astype(BF16)
        dgl_ref[...] = (d * pe_ref[...] * (gate * (1.0 - gate))).astype(BF16)

    sd = jax.ShapeDtypeStruct((S, D), BF16)
    return _pallas(body, name="ple_bwd", grid=(S // ts,), in_specs=[_rows(ts, D)] * 3, out_specs=[_rows(ts, D)] * 2,
                   out_shape=[sd, sd], compiler_params=_cparams("parallel"))(dx, gl, pe)


def _loss_grad(y, target):
    S, D = y.shape
    ts = _tile(S, ROW_TILE, SUBLANES)

    def body(y_ref, t_ref, dy_ref, sq_ref):
        e = y_ref[...] - t_ref[...]
        dy_ref[...] = e / D

        @pl.when(pl.program_id(0) == 0)
        def _():
            sq_ref[...] = jnp.zeros_like(sq_ref)

        sq_ref[...] += jnp.broadcast_to(jnp.sum(jnp.sum(e * e, axis=1, keepdims=True), axis=0, keepdims=True), sq_ref.shape)

    return _pallas(body, name="loss_grad", grid=(S // ts,), in_specs=[_rows(ts, D)] * 2,
                   out_specs=[_rows(ts, D), _accum(LANES)],
                   out_shape=[jax.ShapeDtypeStruct((S, D), F32), jax.ShapeDtypeStruct((SUBLANES, LANES), F32)],
                   compiler_params=_cparams("arbitrary"))(y, target)


def _adamw(w, g, m, v, name):
    L, R, C = w.shape
    tr = _tile(R, ROW_TILE, SUBLANES)

    def body(w_ref, g_ref, m_ref, v_ref, d_ref, m2_ref, v2_ref):
        gv = g_ref[...]
        m2 = ADAM_B1 * m_ref[...] + (1.0 - ADAM_B1) * gv
        v2 = ADAM_B2 * v_ref[...] + (1.0 - ADAM_B2) * (gv * gv)
        m_hat = m2 / (1.0 - ADAM_B1 ** ADAM_STEP)
        v_hat = v2 / (1.0 - ADAM_B2 ** ADAM_STEP)
        d_ref[...] = -ADAM_LR * (m_hat / (jnp.sqrt(v_hat) + ADAM_EPS) + ADAM_WD * w_ref[...])
        m2_ref[...] = m2
        v2_ref[...] = v2

    sd = jax.ShapeDtypeStruct((L, R, C), F32)
    spec = pl.BlockSpec((None, tr, C), lambda l, i: (l, i, 0))
    return _pallas(body, name=name, grid=(L, R // tr), in_specs=[spec] * 4, out_specs=[spec] * 3, out_shape=[sd, sd, sd],
                   compiler_params=_cparams("parallel", "parallel"))(w, g, m, v)


def _place():
    return lax.axis_index("x"), lax.axis_index("y"), lax.axis_index("c")


def _other_chips(x, y):
    return [(1 - x, y), (x, 1 - y), (1 - x, 1 - y)]


HBM = pl.BlockSpec(memory_space=pl.ANY)
VMEM_SPEC = pl.BlockSpec(memory_space=pltpu.VMEM)


def _remote_copy(send_sems, recv_sems, k, src, dst, to):
    return pltpu.make_async_remote_copy(src_ref=src, dst_ref=dst, send_sem=send_sems.at[k], recv_sem=recv_sems.at[k],
                                        device_id=to, device_id_type=MESH)


def _comm_call(body, name, arrays, out_shapes, n_remote, in_place=False):
    scratch = [pltpu.SemaphoreType.DMA((n_remote,)), pltpu.SemaphoreType.DMA((n_remote,))]
    aliases = {i: i for i in range(len(arrays))} if in_place else {}
    return _pallas(body, name=name, in_specs=[HBM] * len(arrays), out_specs=[HBM] * len(out_shapes), out_shape=out_shapes,
                   scratch_shapes=scratch, input_output_aliases=aliases,
                   compiler_params=pltpu.CompilerParams(has_side_effects=True))(*arrays)


def _shard_slab(w, chip_arr, name):
    L, R, C = w.shape
    tr = _tile(R, ROW_TILE, 16)

    def body(j_ref, w_ref, out_ref):
        out_ref[...] = w_ref[...].astype(BF16)

    grid_spec = pltpu.PrefetchScalarGridSpec(
        num_scalar_prefetch=1, grid=(L, R // tr),
        in_specs=[pl.BlockSpec((None, tr, C), lambda l, i, j_ref: (l, i, 0))],
        out_specs=pl.BlockSpec((None, None, tr, C), lambda l, i, j_ref: (j_ref[0], l, i, 0)))
    return _pallas(body, name=name, grid_spec=grid_spec, out_shape=jax.ShapeDtypeStruct((N_CHIPS, L, R, C), BF16),
                   compiler_params=_cparams("parallel", "parallel"))(chip_arr, w)


class Rider:
    def __init__(self, reads, inplace, n_sems, stages):
        self.reads, self.inplace, self.n_sems, self.stages = list(reads), list(inplace), n_sems, stages

    def args(self):
        return self.reads + self.inplace

    def in_specs(self):
        return [HBM] * len(self.args())

    def out_specs(self):
        return [HBM] * len(self.inplace)

    def out_shape(self):
        return [jax.ShapeDtypeStruct(a.shape, a.dtype) for a in self.inplace]

    def scratch(self):
        return [pltpu.SemaphoreType.DMA((self.n_sems,)), pltpu.SemaphoreType.DMA((self.n_sems,))]

    def aliases(self, n_host_in, n_host_out):
        return {n_host_in + len(self.reads) + j: n_host_out + j for j in range(len(self.inplace))}

    def split(self, rest, n_host_out, n_host_scratch):
        n_r, n_io = len(self.reads), len(self.inplace)
        reads = rest[:n_r]
        host_out = rest[n_r + n_io:n_r + n_io + n_host_out]
        outs = rest[n_r + n_io + n_host_out:n_r + 2 * n_io + n_host_out]
        host_scratch = rest[n_r + 2 * n_io + n_host_out:n_r + 2 * n_io + n_host_out + n_host_scratch]
        sems = rest[n_r + 2 * n_io + n_host_out + n_host_scratch:]

        def ride(step, n_steps):
            at = [0, n_steps - 1] if len(self.stages) == 2 else [0, (3 * n_steps) // 4, n_steps - 1]
            for s, stage in zip(at, self.stages):
                pl.when(step == s)(functools.partial(stage, reads, outs, *sems))

        return tuple(host_out) + tuple(host_scratch), ride

    def run(self, name):
        n_r, n_io = len(self.reads), len(self.inplace)

        def body(*refs):
            for stage in self.stages:
                stage(refs[:n_r], refs[n_r + n_io:n_r + 2 * n_io], *refs[n_r + 2 * n_io:])

        return _pallas(body, name=name, in_specs=self.in_specs(), out_specs=self.out_specs(), out_shape=self.out_shape(),
                       scratch_shapes=self.scratch(), input_output_aliases=self.aliases(0, 0),
                       compiler_params=pltpu.CompilerParams(has_side_effects=True))(*self.args())


def _allgather_rider(slabs, items):
    def stage(which, reads, outs, send_sems, recv_sems):
        x, y, c = _place()
        me, sibling = (x, y, c), (x, y, 1 - c)
        copy = functools.partial(_remote_copy, send_sems, recv_sems)
        for n, (j, l) in enumerate(items):
            ref = outs[j]
            h = ref.shape[2] // 2
            mine, theirs = pl.ds(c * h, h), pl.ds((1 - c) * h, h)
            own = ref.at[2 * x + y, l, mine]
            for k, (cx, cy) in enumerate(_other_chips(x, y)):
                arrived = ref.at[2 * cx + cy, l, mine]
                if which == 0:
                    copy(6 * n + k, own, own, (cx, cy, c)).start()
                elif which == 1:
                    copy(6 * n + k, arrived, arrived, me).wait_recv()
                    copy(6 * n + 3 + k, arrived, arrived, sibling).start()
                else:
                    passed = ref.at[2 * cx + cy, l, theirs]
                    copy(6 * n + 3 + k, passed, passed, me).wait_recv()
                    copy(6 * n + k, own, own, me).wait_send()
                    copy(6 * n + 3 + k, arrived, arrived, me).wait_send()

    return Rider([], slabs, 6 * len(items), [functools.partial(stage, w) for w in range(3)])


def _exchange_rider(grads, landing, items):
    def stage(start, reads, outs, send_sems, recv_sems):
        x, y, c = _place()
        for n, (i, j, l) in enumerate(items):
            g_ref, r_ref = reads[i], outs[j]
            h = g_ref.shape[1] // 2
            for k in range(1, N_DEV):
                px, py, pc = (1 - x if k & 4 else x, 1 - y if k & 2 else y, 1 - c if k & 1 else c)
                cp = _remote_copy(send_sems, recv_sems, 7 * n + k - 1, g_ref.at[2 * px + py, pl.ds(pc * h, h)],
                                  r_ref.at[l, k - 1], (px, py, pc))
                if start:
                    cp.start()
                else:
                    cp.wait()

    return Rider(grads, landing, 7 * len(items), [functools.partial(stage, True), functools.partial(stage, False)])


def _reduce_partials(g, r, chip_arr, c_arr, name):
    L, n, R, C = g.shape
    H = R // 2
    th = _tile(H, ROW_TILE, 16)
    nb = H // th

    def body(j_ref, c_ref, g_ref, r_ref, out_ref):
        s = g_ref[...].astype(F32)
        for k in range(N_DEV - 1):
            s = s + r_ref[k].astype(F32)
        out_ref[...] = s

    grid_spec = pltpu.PrefetchScalarGridSpec(
        num_scalar_prefetch=2, grid=(L, nb),
        in_specs=[pl.BlockSpec((None, None, th, C), lambda l, i, j_ref, c_ref: (l, j_ref[0], c_ref[0] * nb + i, 0)),
                  pl.BlockSpec((None, N_DEV - 1, th, C), lambda l, i, j_ref, c_ref: (l, 0, i, 0))],
        out_specs=pl.BlockSpec((None, th, C), lambda l, i, j_ref, c_ref: (l, c_ref[0] * nb + i, 0)))
    return _pallas(body, name=name, grid_spec=grid_spec, out_shape=jax.ShapeDtypeStruct((L, R, C), F32),
                   compiler_params=_cparams("parallel", "parallel"))(chip_arr, c_arr, g, r)


def _join_halves(fs):
    n = len(fs)

    def body(*refs):
        out_refs, (send_sems, recv_sems) = refs[n:2 * n], refs[2 * n:]
        x, y, c = _place()
        cps = []
        for i, out_ref in enumerate(out_refs):
            h = out_ref.shape[1] // 2
            mine = out_ref.at[:, pl.ds(c * h, h)]
            cps.append(_remote_copy(send_sems, recv_sems, i, mine, mine, (x, y, 1 - c)))
            cps[-1].start()
        for cp in cps:
            cp.wait()

    return _comm_call(body, "grad_join_halves", fs, [jax.ShapeDtypeStruct(f.shape, f.dtype) for f in fs], n, in_place=True)


def _allgather_small(v, name):
    R, C = v.shape

    def body(v_ref, out_ref, send_sems, recv_sems):
        x, y, c = _place()
        me = 4 * x + 2 * y + c
        out_ref[me] = v_ref[...]
        cps = []
        for k in range(1, N_DEV):
            peer = (1 - x if k & 4 else x, 1 - y if k & 2 else y, 1 - c if k & 1 else c)
            cps.append(pltpu.make_async_remote_copy(src_ref=v_ref, dst_ref=out_ref.at[me], send_sem=send_sems.at[k - 1],
                                                    recv_sem=recv_sems.at[k - 1], device_id=peer, device_id_type=MESH))
        for cp in cps:
            cp.start()
        for cp in cps:
            cp.wait()

    return _pallas(body, name=name, in_specs=[VMEM_SPEC], out_specs=VMEM_SPEC,
                   out_shape=jax.ShapeDtypeStruct((N_DEV, R, C), v.dtype),
                   scratch_shapes=[pltpu.SemaphoreType.DMA((N_DEV - 1,)), pltpu.SemaphoreType.DMA((N_DEV - 1,))],
                   compiler_params=pltpu.CompilerParams(has_side_effects=True))(v)


def _sum_devices(g):
    n, R, C = g.shape

    def body(g_ref, out_ref):
        s = g_ref[0]
        for d in range(1, n):
            s = s + g_ref[d]
        out_ref[...] = s

    return _pallas(body, name="sum_devices", in_specs=[VMEM_SPEC], out_specs=VMEM_SPEC,
                   out_shape=jax.ShapeDtypeStruct((R, C), g.dtype))(g)


def _pad_heads(a, width):
    lead = a.shape[:-1]
    a = a.reshape(lead + (N_HEADS, width))
    a = jnp.pad(a, [(0, 0)] * len(lead) + [(0, 0), (0, HEAD_PAD - width)])
    return a.reshape(lead + (HP,))


def _unpad_heads(a, width):
    lead = a.shape[:-1]
    return a.reshape(lead + (N_HEADS, HEAD_PAD))[..., :width].reshape(lead + (N_HEADS * width,))


def _pre_weights(slabs, l):
    cols = lambda name: slabs[name][:, l].transpose(1, 0, 2).reshape(slabs[name].shape[2], -1)
    w_in = cols("w_in")
    kpe = jnp.pad(w_in[:, 640:672], ((0, 0), (ROPE_LO, HEAD_PAD - ROPE_HI)))
    w_ukv = cols("w_ukv").reshape(KV_LORA, N_HEADS, QK_NOPE + V_HEAD)
    return dict(
        w_in=jnp.concatenate([w_in[:, :640], kpe, w_in[:, 672:]], axis=1),
        w_uq=_pad_heads(cols("w_uq"), QK_HEAD),
        w_ukv=jnp.concatenate([_pad_heads(w_ukv[..., :QK_NOPE].reshape(KV_LORA, -1), QK_NOPE),
                               _pad_heads(w_ukv[..., QK_NOPE:].reshape(KV_LORA, -1), V_HEAD)], axis=1),
    )


def _post_weights(slabs, l):
    rows = lambda name: slabs[name][:, l].reshape(-1, slabs[name].shape[3])
    cols = lambda name: slabs[name][:, l].transpose(1, 0, 2).reshape(slabs[name].shape[2], -1)
    w_o = rows("w_o")
    return dict(
        w_o=jnp.concatenate([_pad_heads(w_o[:ATTN_WIDTH].T, V_HEAD).T, w_o[ATTN_WIDTH:]], axis=0),
        w_up=cols("w_up"), w_down=rows("w_down"), w_ple_gate=rows("w_ple_gate"), w_ple=cols("w_ple"),
    )


def _gains(small, l):
    row = lambda name: small[name][l].reshape(1, -1)
    headrow = lambda a, b: jnp.pad(jnp.concatenate([small[a][l], small[b][l]]), (0, HEAD_PAD - QK_HEAD)).reshape(1, HEAD_PAD)
    return dict(
        g_mix=row("g_mix"), g_q_lat=row("g_q_lat"), g_kv_lat=row("g_kv_lat"), g_mlp=row("g_mlp"), g_ple=row("g_ple"),
        g_out_conv=row("g_out_conv"), g_out_attn=_pad_heads(small["g_out_attn"][l], V_HEAD).reshape(1, HP),
        gq=headrow("g_qn_nope", "g_qn_rope"), gk=headrow("g_kn_nope", "g_kn_rope"),
        cw8=jnp.pad(small["conv_w"][l], ((0, SUBLANES - CONV_TAPS), (0, 0))),
    )


def _w_o_shards(dw_o):
    return jnp.concatenate([_unpad_heads(dw_o[:HP].T, V_HEAD).T, dw_o[HP:]], axis=0).reshape(N_CHIPS, -1, D_MODEL)


def _unpad_grads(gp):
    dw_in = gp["w_in"]
    dw_ukv = gp["w_ukv"]
    k_part = dw_ukv[:, :HP].reshape(KV_LORA, N_HEADS, HEAD_PAD)[..., :QK_NOPE]
    v_part = dw_ukv[:, HP:].reshape(KV_LORA, N_HEADS, HEAD_PAD)[..., :V_HEAD]
    first = lambda name: gp[name][0]
    col_shards = lambda a: a.reshape(a.shape[0], N_CHIPS, -1).transpose(1, 0, 2)
    return dict(
        w_in=col_shards(jnp.concatenate([dw_in[:, :640], dw_in[:, Z_KPE + ROPE_LO:Z_KPE + ROPE_HI], dw_in[:, Z_GB:]], axis=1)),
        w_uq=col_shards(_unpad_heads(gp["w_uq"], QK_HEAD)),
        w_ukv=col_shards(jnp.concatenate([k_part, v_part], axis=-1).reshape(KV_LORA, -1)),
        w_o=gp["w_o"], w_up=gp["w_up"], w_down=gp["w_down"], w_ple_gate=gp["w_ple_gate"], w_ple=gp["w_ple"],
        g_mix=first("g_mix"), g_q_lat=first("g_q_lat"), g_kv_lat=first("g_kv_lat"), g_mlp=first("g_mlp"),
        g_ple=first("g_ple"), g_out_conv=first("g_out_conv"), g_out_attn=_unpad_heads(first("g_out_attn"), V_HEAD),
        g_qn_nope=gp["gq"][0, :QK_NOPE], g_qn_rope=gp["gq"][0, QK_NOPE:QK_HEAD],
        g_kn_nope=gp["gk"][0, :QK_NOPE], g_kn_rope=gp["gk"][0, QK_NOPE:QK_HEAD],
        conv_w=gp["cw8"][:CONV_TAPS],
    )


def _rope_tables(positions):
    inv_freq = 1.0 / (ROPE_THETA ** (jnp.arange(0, QK_ROPE, 2, dtype=F32) / QK_ROPE))
    ang = positions.astype(F32)[:, None] * inv_freq
    cos, sin = jnp.cos(ang), jnp.sin(ang)
    pad = lambda t, v: jnp.pad(jnp.concatenate([t, t], axis=1), ((0, 0), (ROPE_LO, HEAD_PAD - ROPE_HI)), constant_values=v)
    return pad(cos, 1.0), pad(sin, 0.0)


def _layer_fwd(x0, h, p_l, W, cs, sn, attend, g_next):
    z = _mm(h, W["w_in"], name="mm_in")
    qln, kvln = _lat_fwd(z, W["g_q_lat"], W["g_kv_lat"])
    q_raw = _mm(qln, W["w_uq"], name="mm_uq")
    kv_raw = _mm(kvln, W["w_ukv"], name="mm_ukv")
    qf, kf, vb = _qk_fwd(q_raw, kv_raw, z, cs, sn, W["gq"], W["gk"])
    o, lse = attend(qf, kf, vb)
    conv = _conv_fwd(z, W["cw8"])
    mixed = _mix_fwd(o, conv, W["g_out_attn"], W["g_out_conv"])
    x1, h2 = _mm(mixed, W["w_o"], res=x0, gain=W["g_mlp"], epi="rms", name="mm_o")
    a, f = _mm(h2, W["w_up"], epi="relu2", name="mm_up")
    x2, h3 = _mm(f, W["w_down"], res=x1, gain=W["g_ple"], epi="rms", name="mm_down")
    gl = _mm(h3, W["w_ple_gate"], name="mm_ple_gate")
    pe = _mm(p_l, W["w_ple"], name="mm_ple")
    x3, h_next = _ple_fwd(x2, gl, pe, g_next)
    saved = dict(x0=x0, h=h, z=z, qln=qln, kvln=kvln, q_raw=q_raw, kv_raw=kv_raw, qf=qf, kf=kf, vb=vb, o=o, lse=lse,
                 conv=conv, mixed=mixed, x1=x1, h2=h2, a=a, f=f, x2=x2, h3=h3, gl=gl, pe=pe)
    return x3, h_next, saved


def _layer_bwd(dx3, p_l, W, cs, sn, sv, attend_bwd):
    g = {}
    dpe, dgl = _ple_bwd(dx3, sv["gl"], sv["pe"])
    g["w_ple"] = _mm(p_l, dpe, mode="tn", out_dtype=BF16, shard_out=1, name="mm_dw_ple")
    g["w_ple_gate"] = _mm(sv["h3"], dgl, mode="tn", out_dtype=BF16, shard_out=0, name="mm_dw_ple_gate")
    dx2, dx2b, g["g_ple"] = _mm(dgl, W["w_ple_gate"], mode="nt", res=dx3, aux=sv["x2"], gain=W["g_ple"], epi="rms_bwd",
                                name="mm_dh3")
    da = _mm(dx2b, W["w_down"], mode="nt", aux=sv["a"], epi="drelu2", out_dtype=BF16, name="mm_da")
    g["w_down"] = _mm(sv["f"], dx2b, mode="tn", out_dtype=BF16, shard_out=0, name="mm_dw_down")
    g["w_up"] = _mm(sv["h2"], da, mode="tn", out_dtype=BF16, shard_out=1, name="mm_dw_up")
    dx1, dx1b, g["g_mlp"] = _mm(da, W["w_up"], mode="nt", res=dx2, aux=sv["x1"], gain=W["g_mlp"], epi="rms_bwd",
                                name="mm_dh2")
    dmixed = _mm(dx1b, W["w_o"], mode="nt", name="mm_dmixed")
    g["w_o"] = _mm(sv["mixed"], dx1b, mode="tn", out_dtype=BF16, name="mm_dw_o")
    do, dconv, g["g_out_attn"], g["g_out_conv"] = _mix_bwd(dmixed, sv["o"], sv["conv"], W["g_out_attn"], W["g_out_conv"])
    dgb, dgc, dxin, g["cw8"] = _conv_bwd(dconv, sv["z"], W["cw8"])
    dqf, dkf, dv = attend_bwd(g, sv["qf"], sv["kf"], sv["vb"], sv["o"], do, sv["lse"])
    dq_raw, dkv_raw, dkpe, g["gq"], g["gk"] = _qk_bwd(dqf, dkf, dv, sv["q_raw"], sv["kv_raw"], sv["z"], cs, sn, W["gq"], W["gk"])
    g["w_uq"] = _mm(sv["qln"], dq_raw, mode="tn", out_dtype=BF16, name="mm_dw_uq")
    dqln = _mm(dq_raw, W["w_uq"], mode="nt", name="mm_dqln")
    g["w_ukv"] = _mm(sv["kvln"], dkv_raw, mode="tn", out_dtype=BF16, name="mm_dw_ukv")
    dkvln = _mm(dkv_raw, W["w_ukv"], mode="nt", name="mm_dkvln")
    dlat, g["g_q_lat"], g["g_kv_lat"] = _lat_bwd(dqln, dkvln, sv["z"], W["g_q_lat"], W["g_kv_lat"])
    dz = jnp.concatenate([dlat, dkpe, dgb, dgc, dxin], axis=1)
    g["w_in"] = _mm(sv["h"], dz, mode="tn", out_dtype=BF16, name="mm_dw_in")
    dx0, _, g["g_mix"] = _mm(dz, W["w_in"], mode="nt", res=dx1, aux=sv["x0"], gain=W["g_mix"], epi="rms_bwd", name="mm_dh")
    return dx0, g


def _local_step(x, p, positions, target, slabs, small):
    depth = p.shape[0]
    cs, sn = _rope_tables(positions)
    slabs = dict(slabs)

    def gather(items, host):
        touched = [n for n in PRE + POST if any(n == name for name, _ in items)]
        rider = _allgather_rider([slabs[n] for n in touched], [(touched.index(name), l) for name, l in items])
        if host is None:
            out, new = (), rider.run("allgather_first")
        else:
            *out, new = host(rider)
        slabs.update(zip(touched, new))
        return out

    gather([(name, 0) for name in PRE], None)
    Ws, saved = [], []
    gains = [_gains(small, l) for l in range(depth)]
    h = _rms_fwd(x, gains[0]["g_mix"], "rms_mix")
    for l in range(depth):
        W = dict(gains[l], **_pre_weights(slabs, l))

        def attend(qf, kf, vb, W=W, l=l):
            items = [(name, l) for name in POST] + ([(name, l + 1) for name in PRE] if l + 1 < depth else [])
            o, lse = gather(items, functools.partial(_attn_fwd, qf, kf, vb))
            W.update(_post_weights(slabs, l))
            return o, lse

        g_next = gains[l + 1]["g_mix"] if l + 1 < depth else jnp.ones_like(gains[l]["g_mix"])
        x, h, sv = _layer_fwd(x, h, p[l], W, cs, sn, attend, g_next)
        Ws.append(W)
        saved.append(sv)
    dx, sq = _loss_grad(x, target)

    landing = {name: lax.empty((depth, N_DEV - 1, slabs[name].shape[2] // 2, slabs[name].shape[3]), BF16) for name in PRE + POST}

    def exchange(sends, host):
        touched = [n for n in PRE + POST if any(n == name for name, _, _ in sends)]
        rider = _exchange_rider([g for _, _, g in sends], [landing[n] for n in touched],
                                [(i, touched.index(name), l) for i, (name, l, _) in enumerate(sends)])
        if host is None:
            out, new = (), rider.run("grad_exchange_last")
        else:
            *out, new = host(rider)
        landing.update(zip(touched, new))
        return out

    grads = [None] * depth
    for l in reversed(range(depth)):
        W = Ws[l]

        def attend_bwd(g, qf, kf, vb, o, do, lse, l=l):
            g["w_o"] = _w_o_shards(g["w_o"])
            sends = [(name, l, g[name]) for name in POST]
            if l + 1 < depth:
                sends += [(name, l + 1, grads[l + 1][name]) for name in PRE]
            return exchange(sends, functools.partial(_attn_bwd, qf, kf, vb, o, do, lse))

        dx, gp = _layer_bwd(dx, p[l], W, cs, sn, saved[l], attend_bwd)
        grads[l] = _unpad_grads(gp)
    exchange([(name, 0, grads[0][name]) for name in PRE], None)
    return sq, dx, grads, landing


def _pack_rows(vals, entries):
    depth = vals[entries[0][0]].shape[0]
    return jnp.concatenate([jnp.pad(vals[name].reshape(depth, -1), ((0, 0), (0, _pad128(n) - n))) for name, n in entries], axis=1)


def _unpack_rows(packed, entries):
    out, off = {}, 0
    for name, n in entries:
        out[name] = packed[:, off:off + n]
        off += _pad128(n)
    return out


def kernel(x, p, positions, g_mix, w_in, g_q_lat, w_uq, g_kv_lat, w_ukv, g_qn_nope, g_qn_rope, g_kn_nope, g_kn_rope, conv_w, g_out_attn, g_out_conv, w_o, g_mlp, w_up, w_down, g_ple, w_ple_gate, w_ple, loss_target, m_g_mix, m_w_in, m_g_q_lat, m_w_uq, m_g_kv_lat, m_w_ukv, m_g_qn_nope, m_g_qn_rope, m_g_kn_nope, m_g_kn_rope, m_conv_w, m_g_out_attn, m_g_out_conv, m_w_o, m_g_mlp, m_w_up, m_w_down, m_g_ple, m_w_ple_gate, m_w_ple, v_g_mix, v_w_in, v_g_q_lat, v_w_uq, v_g_kv_lat, v_w_ukv, v_g_qn_nope, v_g_qn_rope, v_g_kn_nope, v_g_kn_rope, v_conv_w, v_g_out_attn, v_g_out_conv, v_w_o, v_g_mlp, v_w_up, v_w_down, v_g_ple, v_w_ple_gate, v_w_ple):
    w = dict(g_mix=g_mix, w_in=w_in, g_q_lat=g_q_lat, w_uq=w_uq, g_kv_lat=g_kv_lat, w_ukv=w_ukv, g_qn_nope=g_qn_nope,
             g_qn_rope=g_qn_rope, g_kn_nope=g_kn_nope, g_kn_rope=g_kn_rope, conv_w=conv_w, g_out_attn=g_out_attn,
             g_out_conv=g_out_conv, w_o=w_o, g_mlp=g_mlp, w_up=w_up, w_down=w_down, g_ple=g_ple, w_ple_gate=w_ple_gate,
             w_ple=w_ple)
    m = dict(g_mix=m_g_mix, w_in=m_w_in, g_q_lat=m_g_q_lat, w_uq=m_w_uq, g_kv_lat=m_g_kv_lat, w_ukv=m_w_ukv,
             g_qn_nope=m_g_qn_nope, g_qn_rope=m_g_qn_rope, g_kn_nope=m_g_kn_nope, g_kn_rope=m_g_kn_rope, conv_w=m_conv_w,
             g_out_attn=m_g_out_attn, g_out_conv=m_g_out_conv, w_o=m_w_o, g_mlp=m_g_mlp, w_up=m_w_up, w_down=m_w_down,
             g_ple=m_g_ple, w_ple_gate=m_w_ple_gate, w_ple=m_w_ple)
    v = dict(g_mix=v_g_mix, w_in=v_w_in, g_q_lat=v_g_q_lat, w_uq=v_w_uq, g_kv_lat=v_g_kv_lat, w_ukv=v_w_ukv,
             g_qn_nope=v_g_qn_nope, g_qn_rope=v_g_qn_rope, g_kn_nope=v_g_kn_nope, g_kn_rope=v_g_kn_rope, conv_w=v_conv_w,
             g_out_attn=v_g_out_attn, g_out_conv=v_g_out_conv, w_o=v_w_o, g_mlp=v_g_mlp, w_up=v_w_up, w_down=v_w_down,
             g_ple=v_g_ple, w_ple_gate=v_w_ple_gate, w_ple=v_w_ple)
    depth = p.shape[0]
    ax, ay, ac = _place()
    chip = 2 * ax + ay
    c_arr = jnp.reshape(ac, (1,)).astype(jnp.int32)
    chip_arr = jnp.reshape(chip, (1,)).astype(jnp.int32)
    conv_shard =("conv_w", CONV_TAPS * CONV_WIDTH // N_CHIPS)
    conv_full = ("conv_w", CONV_TAPS * CONV_WIDTH)

    names = PRE + POST
    slabs = {name: _shard_slab(w[name], chip_arr, "shard_slab_" + name) for name in names}
    conv_rows = -(-depth * CONV_TAPS // SUBLANES) * SUBLANES
    conv_all = _allgather_small(jnp.pad(conv_w.reshape(depth * CONV_TAPS, LANES), ((0, conv_rows - depth * CONV_TAPS), (0, 0))),
                                "allgather_conv_w")
    conv_cat = jnp.concatenate([conv_all[2 * j, :depth * CONV_TAPS] for j in range(N_CHIPS)], axis=1)
    small = {name: w[name] for name, _ in SMALL}
    small["conv_w"] = conv_cat.reshape(depth, CONV_TAPS, CONV_WIDTH)

    sq, grad_x, grads, landing = _local_step(x[0], p[:, 0], positions[0], loss_target[0], slabs, small)

    halves = [_reduce_partials(jnp.stack([grads[l][name] for l in range(depth)]), landing[name], chip_arr, c_arr,
                               "grad_reduce_" + name) for name in names]
    g_out = dict(zip(names, _join_halves(halves)))

    reduced = SMALL + (conv_full, ("loss", LANES))
    stacked = {name: jnp.stack([grads[l][name] for l in range(depth)]) for name, _ in SMALL + (conv_full,)}
    stacked["loss"] = jnp.broadcast_to(sq[:1] * (0.5 / D_MODEL), (depth, LANES))
    g_small = _unpack_rows(_sum_devices(_allgather_small(_pack_rows(stacked, reduced), "allgather_small_grads")), reduced)
    loss = g_small["loss"][0, 0]
    for name, n in SMALL:
        g_out[name] = g_small[name]
    g_conv = g_small["conv_w"].reshape(depth, CONV_TAPS, CONV_WIDTH)
    g_out["conv_w"] = lax.dynamic_slice_in_dim(g_conv, chip * LANES, LANES, axis=2)

    delta, new_m, new_v = {}, {}, {}
    for name in names:
        delta[name], new_m[name], new_v[name] = _adamw(w[name], g_out[name], m[name], v[name], "adamw_" + name)
    local = SMALL + (conv_shard,)
    pack = lambda vals: _pack_rows(vals, local)[None]
    d, m2, v2 = _adamw(pack(w), pack(g_out), pack(m), pack(v), "adamw_small")
    for res, packed_res in ((delta, d), (new_m, m2), (new_v, v2)):
        un = _unpack_rows(packed_res[0], local)
        for name, _ in local:
            res[name] = un[name].reshape(w[name].shape)

    return (loss, grad_x[None], *[g_out[n] for n in WEIGHT_ORDER], *[delta[n] for n in WEIGHT_ORDER],
            *[new_m[n] for n in WEIGHT_ORDER], *[new_v[n] for n in WEIGHT_ORDER])
```

```python
import functools

import jax
import jax.numpy as jnp
from jax import lax
from jax.experimental import pallas as pl
from jax.experimental.pallas import tpu as pltpu

F32 = jnp.float32
BF16 = jnp.bfloat16
MESH = pl.DeviceIdType.MESH

D_MODEL = 1024
N_HEADS = 8
QK_NOPE = 64
QK_ROPE = 32
QK_HEAD = QK_NOPE + QK_ROPE
V_HEAD = 64
Q_LORA = 384
KV_LORA = 256
ATTN_WIDTH = N_HEADS * V_HEAD
CONV_WIDTH = 512
CONV_TAPS = 3
D_FF = 4096
PLE_DIM = 256
ROPE_THETA = 10000.0
EPS = 1e-6
ATT_SCALE = QK_HEAD ** -0.5
LOG2E = 1.4426950408889634
ATT_SCALE_LOG2 = ATT_SCALE * LOG2E

ADAM_LR = 0.001
ADAM_B1 = 0.9
ADAM_B2 = 0.999
ADAM_EPS = 1e-08
ADAM_WD = 0.01
ADAM_STEP = 10

LANES = 128
SUBLANES = 8
HEAD_PAD = LANES
HP = N_HEADS * HEAD_PAD
ROPE_LO = QK_NOPE
ROPE_MID = QK_NOPE + QK_ROPE // 2
ROPE_HI = QK_NOPE + QK_ROPE
VMEM_LIMIT = 56 * 1024 * 1024

Z_Q, Z_KV, Z_KPE, Z_GB, Z_GC, Z_XIN = 0, 384, 640, 768, 1280, 1792
Z_COLS = 2304
Z_LAT = Z_KPE

ROW_TILE = 512
ATT_TILE = 256
ATT_CHAINS_FWD, ATT_CHAINS_BWD = 8, 4
MM_TM, MM_TN, MM_TK = 1024, 1024, 4096
MM_TM_MIN, MM_TK_MIN = 256, 512
MM_VMEM_BUDGET = 40 * 1024 * 1024

N_CHIPS = 4
N_DEV = 8

PRE = ("w_in", "w_uq", "w_ukv")
POST = ("w_o", "w_up", "w_down", "w_ple_gate", "w_ple")
SMALL = (
    ("g_mix", 1024), ("g_q_lat", 384), ("g_kv_lat", 256), ("g_qn_nope", 64), ("g_qn_rope", 32), ("g_kn_nope", 64),
    ("g_kn_rope", 32), ("g_out_attn", 512), ("g_out_conv", 512), ("g_mlp", 1024), ("g_ple", 1024),
)
WEIGHT_ORDER = ("g_mix", "w_in", "g_q_lat", "w_uq", "g_kv_lat", "w_ukv", "g_qn_nope", "g_qn_rope", "g_kn_nope",
                "g_kn_rope", "conv_w", "g_out_attn", "g_out_conv", "w_o", "g_mlp", "w_up", "w_down", "g_ple",
                "w_ple_gate", "w_ple")


def _pallas(body, **kw):
    return pl.pallas_call(body, **kw)


def _cparams(*sem):
    return pltpu.CompilerParams(dimension_semantics=sem, vmem_limit_bytes=VMEM_LIMIT)


def _tile(dim, pref, unit=LANES):
    if dim <= pref:
        return dim
    t = (pref // unit) * unit
    while t > unit and dim % t:
        t -= unit
    assert dim % t == 0, (dim, pref)
    return t


def _pad128(n):
    return -(-n // LANES) * LANES


_DIMS = {"nn": (((1,), (0,)), ((), ())), "nt": (((1,), (1,)), ((), ())), "tn": (((0,), (0,)), ((), ()))}


def _mm(a, b, *, mode="nn", res=None, aux=None, gain=None, epi=None, out_dtype=F32, shard_out=None, name):
    whole_rows = epi in ("rms", "rms_bwd")
    if mode == "nn":
        (M, K), (K2, N) = a.shape, b.shape
    elif mode == "nt":
        (M, K), (N, K2) = a.shape, b.shape
    else:
        (K, M), (K2, N) = a.shape, b.shape
    assert K == K2, (a.shape, b.shape, mode)
    n_lim = k_lim = None
    m_lim = M // N_CHIPS if shard_out == 0 else None
    if shard_out == 1:
        n_lim = N // N_CHIPS
    extra = [t for t in (res, aux) if t is not None]
    out_bytes = 6 if epi in ("relu2", "rms", "rms_bwd") else jnp.dtype(out_dtype).itemsize
    if whole_rows:
        assert (n_lim or N) == N and N <= MM_TN and shard_out is None and gain is not None
        m_lim = min(M, MM_TM // 2)

    def footprint(tm, tn, tk):
        blocks = tm * tk * a.dtype.itemsize + tk * tn * 2 + tm * tn * (out_bytes + 4 * len(extra))
        return 2 * blocks + (tm * tn * 4 if tk < K else 0)

    tm, tn, tk = _tile(m_lim or M, MM_TM), _tile(n_lim or N, MM_TN), _tile(k_lim or K, MM_TK)
    while footprint(tm, tn, tk) > MM_VMEM_BUDGET and tk > MM_TK_MIN:
        tk = _tile(K, tk // 2)
    while footprint(tm, tn, tk) > MM_VMEM_BUDGET and tm > MM_TM_MIN:
        tm = _tile(M, tm // 2)
    nk = K // tk
    a_spec = pl.BlockSpec((tk, tm), lambda i, j, k: (k, i)) if mode == "tn" else pl.BlockSpec((tm, tk), lambda i, j, k: (i, k))
    if mode == "nt":
        b_block, b_rc = (tn, tk), (lambda i, j, k: (j, k))
    else:
        b_block, b_rc = (tk, tn), (lambda i, j, k: (k, j))
    b_spec = pl.BlockSpec(b_block, b_rc)
    mn_spec = pl.BlockSpec((tm, tn), lambda i, j, k: (i, j))
    dims = _DIMS[mode]
    n_out = {"relu2": 2, "rms": 2, "rms_bwd": 3}.get(epi, 1)

    def body(*refs):
        a_ref, b_ref = refs[0], refs[1]
        extra_refs = refs[2:2 + len(extra)]
        gain_ref = refs[2 + len(extra)] if whole_rows else None
        out_refs = refs[2 + len(extra) + whole_rows:][:n_out]
        prod = lax.dot_general(a_ref[...].astype(BF16), b_ref[...].astype(BF16), dims, preferred_element_type=F32)

        def finish(r):
            if epi == "rms_bwd":
                dx, dgc = _norm_bwd(extra_refs[1][...], r, gain_ref[...], N)
                dx = dx + extra_refs[0][...]
                out_refs[0][...] = dx
                out_refs[1][...] = dx.astype(BF16)
                _accumulate(out_refs[2], dgc)
                return
            if res is not None:
                r = r + extra_refs[0][...]
            if epi == "rms":
                out_refs[0][...] = r
                out_refs[1][...] = (r * _rinv(r, N) * gain_ref[...]).astype(BF16)
            elif epi == "relu2":
                out_refs[0][...] = r
                t = jnp.maximum(r, 0.0)
                out_refs[1][...] = (t * t).astype(BF16)
            elif epi == "drelu2":
                out_refs[0][...] = (r * (2.0 * jnp.maximum(extra_refs[-1][...], 0.0))).astype(out_dtype)
            else:
                out_refs[0][...] = r.astype(out_dtype)

        if nk == 1:
            finish(prod)
        else:
            acc = refs[-1]
            k = pl.program_id(2)

            @pl.when(k == 0)
            def _():
                acc[...] = prod

            @pl.when(k > 0)
            def _():
                acc[...] += prod

            @pl.when(k == nk - 1)
            def _():
                finish(acc[...])

    if epi in ("relu2", "rms", "rms_bwd"):
        out_shape = [jax.ShapeDtypeStruct((M, N), F32), jax.ShapeDtypeStruct((M, N), BF16)]
        out_specs = [mn_spec, mn_spec]
        if epi == "rms_bwd":
            out_shape.append(jax.ShapeDtypeStruct((SUBLANES, N), F32))
            out_specs.append(pl.BlockSpec((SUBLANES, tn), lambda i, j, k: (0, j)))
    elif shard_out == 0:
        per = (M // N_CHIPS) // tm
        out_shape = jax.ShapeDtypeStruct((N_CHIPS, M // N_CHIPS, N), out_dtype)
        out_specs = pl.BlockSpec((None, tm, tn), lambda i, j, k: (i // per, i % per, j))
    elif shard_out == 1:
        per = (N // N_CHIPS) // tn
        out_shape = jax.ShapeDtypeStruct((N_CHIPS, M, N // N_CHIPS), out_dtype)
        out_specs = pl.BlockSpec((None, tm, tn), lambda i, j, k: (j // per, i, j % per))
    else:
        out_shape = jax.ShapeDtypeStruct((M, N), out_dtype)
        out_specs = mn_spec
    gains = [gain] if whole_rows else []
    return _pallas(
        body, name=name, grid=(M // tm, N // tn, nk),
        in_specs=[a_spec, b_spec] + [mn_spec] * len(extra) + [pl.BlockSpec((1, tn), lambda i, j, k: (0, j))] * len(gains),
        out_specs=out_specs, out_shape=out_shape,
        scratch_shapes=[pltpu.VMEM((tm, tn), F32)] if nk > 1 else [],
        compiler_params=_cparams(*(["arbitrary"] * 3 if epi == "rms_bwd" else ["parallel", "parallel", "arbitrary"])),
    )(a, b, *extra, *gains)


def _rows(ts, d, col=0):
    return pl.BlockSpec((ts, d), lambda i: (i, col))


def _gain(d):
    return pl.BlockSpec((1, d), lambda i: (0, 0))


def _accum(d):
    return pl.BlockSpec((SUBLANES, d), lambda i: (0, 0))


def _accumulate(ref, val):
    i = pl.program_id(0)

    @pl.when(i == 0)
    def _():
        ref[...] = jnp.zeros_like(ref)

    ref[...] += jnp.broadcast_to(jnp.sum(val, axis=0, keepdims=True), ref.shape)


def _rinv(x, n):
    return lax.rsqrt(jnp.sum(x * x, axis=-1, keepdims=True) / n + EPS)


def _norm_bwd(x, dy, g, n):
    r = _rinv(x, n)
    xhat = x * r
    dyg = dy * g
    dx = r * (dyg - xhat * (jnp.sum(dyg * xhat, axis=-1, keepdims=True) / n))
    return dx, dy * xhat


def _rms_fwd(x, g, name):
    S, D = x.shape
    ts = _tile(S, ROW_TILE, SUBLANES)

    def body(x_ref, g_ref, h_ref):
        xv = x_ref[...]
        h_ref[...] = (xv * _rinv(xv, D) * g_ref[...]).astype(BF16)

    return _pallas(body, name=name, grid=(S // ts,), in_specs=[_rows(ts, D), _gain(D)], out_specs=_rows(ts, D),
                   out_shape=jax.ShapeDtypeStruct((S, D), BF16), compiler_params=_cparams("parallel"))(x, g)


def _lat_fwd(z, gq, gkv):
    S = z.shape[0]
    ts = _tile(S, ROW_TILE, SUBLANES)

    def body(z_ref, gq_ref, gkv_ref, q_ref, kv_ref):
        zq = z_ref[:, Z_Q:Z_KV]
        zkv = z_ref[:, Z_KV:Z_KPE]
        q_ref[...] = (zq * _rinv(zq, Q_LORA) * gq_ref[...]).astype(BF16)
        kv_ref[...] = (zkv * _rinv(zkv, KV_LORA) * gkv_ref[...]).astype(BF16)

    return _pallas(body, name="lat_fwd", grid=(S // ts,), in_specs=[_rows(ts, Z_LAT), _gain(Q_LORA), _gain(KV_LORA)],
                   out_specs=[_rows(ts, Q_LORA), _rows(ts, KV_LORA)],
                   out_shape=[jax.ShapeDtypeStruct((S, Q_LORA), BF16), jax.ShapeDtypeStruct((S, KV_LORA), BF16)],
                   compiler_params=_cparams("parallel"))(z, gq, gkv)


def _lat_bwd(dq, dkv, z, gq, gkv):
    S = z.shape[0]
    ts = _tile(S, ROW_TILE, SUBLANES)

    def body(dq_ref, dkv_ref, z_ref, gq_ref, gkv_ref, dlat_ref, dgq_ref, dgkv_ref):
        dxq, cq = _norm_bwd(z_ref[:, Z_Q:Z_KV], dq_ref[...], gq_ref[...], Q_LORA)
        dxkv, ckv = _norm_bwd(z_ref[:, Z_KV:Z_KPE], dkv_ref[...], gkv_ref[...], KV_LORA)
        dlat_ref[:, Z_Q:Z_KV] = dxq.astype(BF16)
        dlat_ref[:, Z_KV:Z_KPE] = dxkv.astype(BF16)
        _accumulate(dgq_ref, cq)
        _accumulate(dgkv_ref, ckv)

    return _pallas(body, name="lat_bwd", grid=(S // ts,),
                   in_specs=[_rows(ts, Q_LORA), _rows(ts, KV_LORA), _rows(ts, Z_LAT), _gain(Q_LORA), _gain(KV_LORA)],
                   out_specs=[_rows(ts, Z_LAT), _accum(Q_LORA), _accum(KV_LORA)],
                   out_shape=[jax.ShapeDtypeStruct((S, Z_LAT), BF16), jax.ShapeDtypeStruct((SUBLANES, Q_LORA), F32),
                              jax.ShapeDtypeStruct((SUBLANES, KV_LORA), F32)],
                   compiler_params=_cparams("arbitrary"))(dq, dkv, z, gq, gkv)


def _head_masks():
    lane = lax.broadcasted_iota(jnp.int32, (1, HEAD_PAD), 1)
    return lane < ROPE_LO, (lane >= ROPE_LO) & (lane < ROPE_HI)


def _head_matrices():
    k = jnp.arange(HEAD_PAD)[:, None]
    j = jnp.arange(HEAD_PAD)[None, :]
    nope = (k < ROPE_LO) & (j < ROPE_LO)
    rope = (k >= ROPE_LO) & (k < ROPE_HI) & (j >= ROPE_LO) & (j < ROPE_HI)
    seg = jnp.where(nope, 1.0 / QK_NOPE, jnp.where(rope, 1.0 / QK_ROPE, 0.0))
    half = QK_ROPE // 2
    rot = jnp.where((j >= ROPE_LO) & (j < ROPE_MID) & (k == j + half), -1.0,
                    jnp.where((j >= ROPE_MID) & (j < ROPE_HI) & (k == j - half), 1.0, 0.0))
    return seg.astype(BF16), rot.astype(BF16)


_HEAD_MAT = pl.BlockSpec((HEAD_PAD, HEAD_PAD), lambda i: (0, 0))


def _split_dot(t, mat):
    hi = t.astype(BF16)
    lo = (t - hi.astype(F32)).astype(BF16)
    return jnp.dot(hi, mat, preferred_element_type=F32) + jnp.dot(lo, mat, preferred_element_type=F32)


def _qk_fwd(q_raw, kv_raw, z, cs, sn, gq, gk, seg, rot):
    S = q_raw.shape[0]
    ts = _tile(S, ROW_TILE, SUBLANES)

    def body(q_ref, k_ref, v_ref, kpe_ref, cs_ref, sn_ref, gq_ref, gk_ref, seg_ref, rot_ref, qf_ref, kf_ref, vb_ref):
        cos, sin, gqv, gkv = cs_ref[...], sn_ref[...], gq_ref[...], gk_ref[...]
        seg_m, rot_m = seg_ref[...], rot_ref[...]

        def norm(x, g):
            return x * lax.rsqrt(_split_dot(x * x, seg_m) + EPS) * g

        def rope(y):
            return y * cos + _split_dot(y, rot_m) * sin

        kr = rope(norm(kpe_ref[...], gkv))
        lane = lax.broadcasted_iota(jnp.int32, (1, HEAD_PAD), 1)
        for h in range(N_HEADS):
            sl = slice(h * HEAD_PAD, (h + 1) * HEAD_PAD)
            qf_ref[:, sl] = rope(norm(q_ref[:, sl], gqv)).astype(BF16)
            kf_ref[:, sl] = (norm(k_ref[:, sl], gkv) + kr).astype(BF16)
            vb_ref[:, sl] = jnp.where(lane == V_HEAD, 1.0, v_ref[:, sl]).astype(BF16)

    hd = jax.ShapeDtypeStruct((S, HP), BF16)
    return _pallas(body, name="qk_fwd", grid=(S // ts,),
                   in_specs=[_rows(ts, HP), _rows(ts, HP, 0), _rows(ts, HP, 1), _rows(ts, HEAD_PAD, Z_KPE // HEAD_PAD),
                             _rows(ts, HEAD_PAD), _rows(ts, HEAD_PAD), _gain(HEAD_PAD), _gain(HEAD_PAD), _HEAD_MAT, _HEAD_MAT],
                   out_specs=[_rows(ts, HP)] * 3, out_shape=[hd, hd, hd],
                   compiler_params=_cparams("parallel"))(q_raw, kv_raw, kv_raw, z, cs, sn, gq, gk, seg, rot)


def _qk_bwd(dqf, dkf, dv, q_raw, kv_raw, z, cs, sn, gq, gk, seg, rot):
    S = q_raw.shape[0]
    ts = _tile(S, ROW_TILE, SUBLANES)

    def body(dqf_ref, dkf_ref, dv_ref, q_ref, k_ref, kpe_ref, cs_ref, sn_ref, gq_ref, gk_ref, seg_ref, rot_ref,
             dq_ref, dkv_ref, dkpe_ref, dgq_ref, dgk_ref):
        m_n, m_r = _head_masks()
        cos, sin, gqv, gkv = cs_ref[...], sn_ref[...], gq_ref[...], gk_ref[...]
        seg_m, rot_m = seg_ref[...], rot_ref[...]

        def rope_t(w):
            return w * cos - _split_dot(w * sin, rot_m)

        def norm_bwd(x, dy, g):
            r = lax.rsqrt(_split_dot(x * x, seg_m) + EPS)
            xhat = x * r
            dyg = dy * g
            return r * (dyg - xhat * _split_dot(dyg * xhat, seg_m)), dy * xhat

        accq = jnp.zeros((ts, HEAD_PAD), F32)
        acck = jnp.zeros((ts, HEAD_PAD), F32)
        dkr = jnp.zeros((ts, HEAD_PAD), F32)
        for h in range(N_HEADS):
            sl = slice(h * HEAD_PAD, (h + 1) * HEAD_PAD)
            dx, c = norm_bwd(q_ref[:, sl], rope_t(dqf_ref[:, sl]), gqv)
            dq_ref[:, sl] = dx.astype(BF16)
            accq = accq + c
            dk = dkf_ref[:, sl]
            dkr = dkr + jnp.where(m_r, dk, 0.0)
            dx, c = norm_bwd(k_ref[:, sl], jnp.where(m_n, dk, 0.0), gkv)
            dkv_ref[:, sl] = dx.astype(BF16)
            acck = acck + c
            dkv_ref[:, HP + h * HEAD_PAD:HP + (h + 1) * HEAD_PAD] = dv_ref[:, sl].astype(BF16)
        dx, c = norm_bwd(kpe_ref[...], rope_t(dkr), gkv)
        dkpe_ref[...] = dx.astype(BF16)
        _accumulate(dgq_ref, accq)
        _accumulate(dgk_ref, acck + c)

    return _pallas(body, name="qk_bwd", grid=(S // ts,),
                   in_specs=[_rows(ts, HP)] * 4 + [_rows(ts, HP, 0), _rows(ts, HEAD_PAD, Z_KPE // HEAD_PAD),
                                                   _rows(ts, HEAD_PAD), _rows(ts, HEAD_PAD), _gain(HEAD_PAD), _gain(HEAD_PAD),
                                                   _HEAD_MAT, _HEAD_MAT],
                   out_specs=[_rows(ts, HP), _rows(ts, 2 * HP), _rows(ts, HEAD_PAD), _accum(HEAD_PAD), _accum(HEAD_PAD)],
                   out_shape=[jax.ShapeDtypeStruct((S, HP), BF16), jax.ShapeDtypeStruct((S, 2 * HP), BF16),
                              jax.ShapeDtypeStruct((S, HEAD_PAD), BF16), jax.ShapeDtypeStruct((SUBLANES, HEAD_PAD), F32),
                              jax.ShapeDtypeStruct((SUBLANES, HEAD_PAD), F32)],
                   compiler_params=_cparams("arbitrary"))(dqf, dkf, dv, q_raw, kv_raw, z, cs, sn, gq, gk, seg, rot)


def _causal(t):
    row = lax.broadcasted_iota(jnp.int32, (t, t), 0)
    col = lax.broadcasted_iota(jnp.int32, (t, t), 1)
    return col <= row


def _attn_tiles(S, chains):
    t = _tile(S, ATT_TILE, SUBLANES)
    return t, min(chains, S // t)


def _attn_fwd(qf, kf, vb, rider):
    S = qf.shape[0]
    t, nc = _attn_tiles(S, ATT_CHAINS_FWD)
    tq = nc * t
    nqt = S // tq

    def body(q_ref, k_ref, v_ref, *rest):
        (o_ref, lse_ref, m_ref, acc_ref), ride = rider.split(rest, 2, 2)
        qt = pl.program_id(1)
        ride(pl.program_id(0) * nqt + qt, N_HEADS * nqt)
        m_ref[...] = jnp.full_like(m_ref, -jnp.inf)
        acc_ref[...] = jnp.zeros_like(acc_ref)

        def against(rows, first, diagonal):
            kb, vb_ = k_ref[rows, :], v_ref[rows, :]
            subs = [slice(a * t, (a + 1) * t) for a in range(first, nc)]
            s = [lax.dot_general(q_ref[sub, :], kb, _DIMS["nt"], preferred_element_type=F32) * ATT_SCALE_LOG2 for sub in subs]
            if diagonal:
                s[0] = jnp.where(_causal(t), s[0], -jnp.inf)
            m_old = [m_ref[sub, :] for sub in subs]
            m_new = [jnp.maximum(mo, jnp.max(sa, axis=-1, keepdims=True)) for mo, sa in zip(m_old, s)]
            p = [jnp.exp2(sa - jnp.concatenate([mn] * (t // HEAD_PAD), axis=1)).astype(BF16) for sa, mn in zip(s, m_new)]
            for sub, mo, mn, pa in zip(subs, m_old, m_new, p):
                acc_ref[sub, :] = jnp.exp2(mo - mn) * acc_ref[sub, :] + jnp.dot(pa, vb_, preferred_element_type=F32)
                m_ref[sub, :] = mn

        def trip(j, carry):
            against(pl.ds(pl.multiple_of(j * t, t), t), 0, False)
            return carry

        lax.fori_loop(0, nc * qt, trip, 0)
        for d in range(nc):
            against(pl.ds(pl.multiple_of((nc * qt + d) * t, t), t), d, True)
        acc = acc_ref[...]
        l = acc[:, V_HEAD:V_HEAD + 1]
        lane = lax.broadcasted_iota(jnp.int32, (1, HEAD_PAD), 1)
        o_ref[...] = jnp.where(lane < V_HEAD, acc / l, 0.0)
        lse_ref[...] = m_ref[...] + jnp.log(l) * LOG2E

    tile = pl.BlockSpec((tq, HEAD_PAD), lambda h, i: (i, h))
    full = pl.BlockSpec((S, HEAD_PAD), lambda h, i: (0, h))
    outs = _pallas(body, name="attn_fwd", grid=(N_HEADS, nqt), in_specs=[tile, full, full] + rider.in_specs(),
                   out_specs=[tile, pl.BlockSpec((None, tq, HEAD_PAD), lambda h, i: (h, i, 0))] + rider.out_specs(),
                   out_shape=[jax.ShapeDtypeStruct((S, HP), F32), jax.ShapeDtypeStruct((N_HEADS, S, HEAD_PAD), F32)]
                   + rider.out_shape(),
                   scratch_shapes=[pltpu.VMEM((tq, HEAD_PAD), F32), pltpu.VMEM((tq, HEAD_PAD), F32)] + rider.scratch(),
                   input_output_aliases=rider.aliases(3, 2),
                   compiler_params=_cparams("arbitrary", "arbitrary"))(qf, kf, vb, *rider.args())
    return outs[0], outs[1], outs[2:]


def _attn_bwd(qf, kf, vb, o, do, lse, rider):
    S = qf.shape[0]
    t, nc = _attn_tiles(S, ATT_CHAINS_BWD)
    tkv = nc * t
    nq = S // t
    nkt = S // tkv

    def body(q_ref, k_ref, v_ref, o_ref, do_ref, lse_ref, *rest):
        (dq_ref, dk_ref, dv_ref), ride = rider.split(rest, 3, 0)
        kt = pl.program_id(1)
        ride(pl.program_id(0) * nkt + kt, N_HEADS * nkt)

        @pl.when(kt == 0)
        def _():
            dq_ref[...] = jnp.zeros_like(dq_ref)

        dk_ref[...] = jnp.zeros_like(dk_ref)
        dv_ref[...] = jnp.zeros_like(dv_ref)

        def q_block(rows):
            dof = do_ref[rows, :]
            delta = jnp.sum(dof * o_ref[rows, :], axis=-1, keepdims=True)
            return q_ref[rows, :], dof.astype(BF16), lse_ref[rows, :][:, :1], delta

        def against(rows, n_sub, diagonal):
            q, dob, lse, delta = q_block(rows)
            subs = [slice(b * t, (b + 1) * t) for b in range(n_sub)]
            kbs = [k_ref[sub, :] for sub in subs]
            s = [lax.dot_general(q, kb, _DIMS["nt"], preferred_element_type=F32) for kb in kbs]
            dp = [lax.dot_general(dob, v_ref[sub, :], _DIMS["nt"], preferred_element_type=F32) for sub in subs]
            p = [jnp.exp2(sb * ATT_SCALE_LOG2 - lse) for sb in s]
            if diagonal:
                p[-1] = jnp.where(_causal(t), p[-1], 0.0)
            ds = [(pb * (dpb - delta) * ATT_SCALE).astype(BF16) for pb, dpb in zip(p, dp)]
            for sub, pb in zip(subs, p):
                dv_ref[sub, :] += lax.dot_general(pb.astype(BF16), dob, _DIMS["tn"], preferred_element_type=F32)
            for sub, dsb in zip(subs, ds):
                dk_ref[sub, :] += lax.dot_general(dsb, q, _DIMS["tn"], preferred_element_type=F32)
            dq_ref[rows, :] += sum(jnp.dot(dsb, kb, preferred_element_type=F32) for dsb, kb in zip(ds, kbs))

        for a in range(nc):
            against(pl.ds(pl.multiple_of((nc * kt + a) * t, t), t), a + 1, True)

        def trip(i, carry):
            against(pl.ds(pl.multiple_of(i * t, t), t), nc, False)
            return carry

        lax.fori_loop(nc * (kt + 1), nq, trip, 0)

    tile = pl.BlockSpec((tkv, HEAD_PAD), lambda h, j: (j, h))
    full = pl.BlockSpec((S, HEAD_PAD), lambda h, j: (0, h))
    hd = jax.ShapeDtypeStruct((S, HP), F32)
    outs = _pallas(body, name="attn_bwd", grid=(N_HEADS, nkt),
                   in_specs=[full, tile, tile, full, full, pl.BlockSpec((None, S, HEAD_PAD), lambda h, j: (h, 0, 0))]
                   + rider.in_specs(),
                   out_specs=[full, tile, tile] + rider.out_specs(), out_shape=[hd, hd, hd] + rider.out_shape(),
                   scratch_shapes=rider.scratch(), input_output_aliases=rider.aliases(6, 3),
                   compiler_params=_cparams("arbitrary", "arbitrary"))(qf, kf, vb, o, do, lse, *rider.args())
    return outs[0], outs[1], outs[2], outs[3:]


def _shift_down(u, j, row):
    return jnp.where(row >= j, pltpu.roll(u, j, 0), 0.0)


def _shift_up(u, j, row, s):
    return jnp.where(row < s - j, pltpu.roll(u, s - j, 0), 0.0)


def _conv_cols(s, first_tile):
    return pl.BlockSpec((s, LANES), lambda cb: (0, first_tile + cb))


def _conv_fwd(z, cw8):
    S = z.shape[0]

    def body(gb_ref, gc_ref, xin_ref, w_ref, out_ref):
        row = lax.broadcasted_iota(jnp.int32, (S, LANES), 0)
        u = gc_ref[...] * xin_ref[...]
        y = w_ref[0:1, :] * u
        for j in range(1, CONV_TAPS):
            y = y + w_ref[j:j + 1, :] * _shift_down(u, j, row)
        out_ref[...] = gb_ref[...] * y

    return _pallas(body, name="conv_fwd", grid=(CONV_WIDTH // LANES,),
                   in_specs=[_conv_cols(S, Z_GB // LANES), _conv_cols(S, Z_GC // LANES), _conv_cols(S, Z_XIN // LANES),
                             pl.BlockSpec((SUBLANES, LANES), lambda cb: (0, cb))],
                   out_specs=_conv_cols(S, 0), out_shape=jax.ShapeDtypeStruct((S, CONV_WIDTH), F32),
                   compiler_params=_cparams("parallel"))(z, z, z, cw8)


def _conv_bwd(dconv, z, cw8):
    S = z.shape[0]

    def body(d_ref, gb_ref, gc_ref, xin_ref, w_ref, dgb_ref, dgc_ref, dxin_ref, dw_ref):
        row = lax.broadcasted_iota(jnp.int32, (S, LANES), 0)
        gc, xin, d = gc_ref[...], xin_ref[...], d_ref[...]
        u = gc * xin
        dy = d * gb_ref[...]
        y = w_ref[0:1, :] * u
        du = w_ref[0:1, :] * dy
        dw = [jnp.sum(dy * u, axis=0, keepdims=True)]
        for j in range(1, CONV_TAPS):
            uj = _shift_down(u, j, row)
            y = y + w_ref[j:j + 1, :] * uj
            du = du + w_ref[j:j + 1, :] * _shift_up(dy, j, row, S)
            dw.append(jnp.sum(dy * uj, axis=0, keepdims=True))
        dgb_ref[...] = (d * y).astype(BF16)
        dgc_ref[...] = (du * xin).astype(BF16)
        dxin_ref[...] = (du * gc).astype(BF16)
        tap = lax.broadcasted_iota(jnp.int32, (SUBLANES, LANES), 0)
        dw_ref[...] = sum(jnp.where(tap == j, dw[j], 0.0) for j in range(CONV_TAPS))

    col = _conv_cols(S, 0)
    sd = jax.ShapeDtypeStruct((S, CONV_WIDTH), BF16)
    return _pallas(body, name="conv_bwd", grid=(CONV_WIDTH // LANES,),
                   in_specs=[col, _conv_cols(S, Z_GB // LANES), _conv_cols(S, Z_GC // LANES), _conv_cols(S, Z_XIN // LANES),
                             pl.BlockSpec((SUBLANES, LANES), lambda cb: (0, cb))],
                   out_specs=[col, col, col, pl.BlockSpec((SUBLANES, LANES), lambda cb: (0, cb))],
                   out_shape=[sd, sd, sd, jax.ShapeDtypeStruct((SUBLANES, CONV_WIDTH), F32)],
                   compiler_params=_cparams("parallel"))(dconv, z, z, z, cw8)


def _mix_fwd(o, conv, ga, gc):
    S = o.shape[0]
    ts = _tile(S, ROW_TILE, SUBLANES)

    def body(o_ref, c_ref, ga_ref, gc_ref, out_ref):
        ov, cv = o_ref[...], c_ref[...]
        out_ref[:, :HP] = (ov * _rinv(ov, ATTN_WIDTH) * ga_ref[...]).astype(BF16)
        out_ref[:, HP:] = (cv * _rinv(cv, CONV_WIDTH) * gc_ref[...]).astype(BF16)

    return _pallas(body, name="mix_fwd", grid=(S // ts,),
                   in_specs=[_rows(ts, HP), _rows(ts, CONV_WIDTH), _gain(HP), _gain(CONV_WIDTH)],
                   out_specs=_rows(ts, HP + CONV_WIDTH), out_shape=jax.ShapeDtypeStruct((S, HP + CONV_WIDTH), BF16),
                   compiler_params=_cparams("parallel"))(o, conv, ga, gc)


def _mix_bwd(dmixed, o, conv, ga, gc):
    S = o.shape[0]
    ts = _tile(S, ROW_TILE, SUBLANES)

    def body(d_ref, o_ref, c_ref, ga_ref, gc_ref, do_ref, dc_ref, dga_ref, dgc_ref):
        dx, ca = _norm_bwd(o_ref[...], d_ref[:, :HP], ga_ref[...], ATTN_WIDTH)
        do_ref[...] = dx
        dx, cc = _norm_bwd(c_ref[...], d_ref[:, HP:], gc_ref[...], CONV_WIDTH)
        dc_ref[...] = dx
        _accumulate(dga_ref, ca)
        _accumulate(dgc_ref, cc)

    return _pallas(body, name="mix_bwd", grid=(S // ts,),
                   in_specs=[_rows(ts, HP + CONV_WIDTH), _rows(ts, HP), _rows(ts, CONV_WIDTH), _gain(HP), _gain(CONV_WIDTH)],
                   out_specs=[_rows(ts, HP), _rows(ts, CONV_WIDTH), _accum(HP), _accum(CONV_WIDTH)],
                   out_shape=[jax.ShapeDtypeStruct((S, HP), F32), jax.ShapeDtypeStruct((S, CONV_WIDTH), F32),
                              jax.ShapeDtypeStruct((SUBLANES, HP), F32), jax.ShapeDtypeStruct((SUBLANES, CONV_WIDTH), F32)],
                   compiler_params=_cparams("arbitrary"))(dmixed, o, conv, ga, gc)


def _ple_fwd(x, gl, pe, g_next):
    S, D = x.shape
    ts = _tile(S, ROW_TILE, SUBLANES)

    def body(x_ref, gl_ref, pe_ref, g_ref, out_ref, h_ref):
        y = x_ref[...] + jax.nn.sigmoid(gl_ref[...]) * pe_ref[...]
        out_ref[...] = y
        h_ref[...] = (y * _rinv(y, D) * g_ref[...]).astype(BF16)

    return _pallas(body, name="ple_fwd", grid=(S // ts,), in_specs=[_rows(ts, D)] * 3 + [_gain(D)], out_specs=[_rows(ts, D)] * 2,
                   out_shape=[jax.ShapeDtypeStruct((S, D), F32), jax.ShapeDtypeStruct((S, D), BF16)],
                   compiler_params=_cparams("parallel"))(x, gl, pe, g_next)


def _ple_bwd(dx, gl, pe):
    S, D = dx.shape
    ts = _tile(S, ROW_TILE, SUBLANES)

    def body(dx_ref, gl_ref, pe_ref, dpe_ref, dgl_ref):
        d = dx_ref[...]
        gate = jax.nn.sigmoid(gl_ref[...])
        dpe_ref[...] = (d * gate).astype(BF16)
        dgl_ref[...] = (d * pe_ref[...] * (gate * (1.0 - gate))).astype(BF16)

    sd = jax.ShapeDtypeStruct((S, D), BF16)
    return _pallas(body, name="ple_bwd", grid=(S // ts,), in_specs=[_rows(ts, D)] * 3, out_specs=[_rows(ts, D)] * 2,
                   out_shape=[sd, sd], compiler_params=_cparams("parallel"))(dx, gl, pe)


def _loss_grad(y, target):
    S, D = y.shape
    ts = _tile(S, ROW_TILE, SUBLANES)

    def body(y_ref, t_ref, dy_ref, sq_ref):
        e = y_ref[...] - t_ref[...]
        dy_ref[...] = e / D

        @pl.when(pl.program_id(0) == 0)
        def _():
            sq_ref[...] = jnp.zeros_like(sq_ref)

        sq_ref[...] += jnp.broadcast_to(jnp.sum(jnp.sum(e * e, axis=1, keepdims=True), axis=0, keepdims=True), sq_ref.shape)

    return _pallas(body, name="loss_grad", grid=(S // ts,), in_specs=[_rows(ts, D)] * 2,
                   out_specs=[_rows(ts, D), _accum(LANES)],
                   out_shape=[jax.ShapeDtypeStruct((S, D), F32), jax.ShapeDtypeStruct((SUBLANES, LANES), F32)],
                   compiler_params=_cparams("arbitrary"))(y, target)


def _adamw(w, g, m, v, name):
    L, R, C = w.shape
    tr = _tile(R, ROW_TILE, SUBLANES)

    def body(w_ref, g_ref, m_ref, v_ref, d_ref, m2_ref, v2_ref):
        gv = g_ref[...]
        m2 = ADAM_B1 * m_ref[...] + (1.0 - ADAM_B1) * gv
        v2 = ADAM_B2 * v_ref[...] + (1.0 - ADAM_B2) * (gv * gv)
        m_hat = m2 / (1.0 - ADAM_B1 ** ADAM_STEP)
        v_hat = v2 / (1.0 - ADAM_B2 ** ADAM_STEP)
        d_ref[...] = -ADAM_LR * (m_hat / (jnp.sqrt(v_hat) + ADAM_EPS) + ADAM_WD * w_ref[...])
        m2_ref[...] = m2
        v2_ref[...] = v2

    sd = jax.ShapeDtypeStruct((L, R, C), F32)
    spec = pl.BlockSpec((None, tr, C), lambda l, i: (l, i, 0))
    return _pallas(body, name=name, grid=(L, R // tr), in_specs=[spec] * 4, out_specs=[spec] * 3, out_shape=[sd, sd, sd],
                   compiler_params=_cparams("parallel", "parallel"))(w, g, m, v)


def _place():
    return lax.axis_index("x"), lax.axis_index("y"), lax.axis_index("c")


def _other_chips(x, y):
    return [(1 - x, y), (x, 1 - y), (1 - x, 1 - y)]


HBM = pl.BlockSpec(memory_space=pl.ANY)
VMEM_SPEC = pl.BlockSpec(memory_space=pltpu.VMEM)


def _remote_copy(send_sems, recv_sems, k, src, dst, to):
    return pltpu.make_async_remote_copy(src_ref=src, dst_ref=dst, send_sem=send_sems.at[k], recv_sem=recv_sems.at[k],
                                        device_id=to, device_id_type=MESH)


def _comm_call(body, name, arrays, out_shapes, n_remote, in_place=False):
    scratch = [pltpu.SemaphoreType.DMA((n_remote,)), pltpu.SemaphoreType.DMA((n_remote,))]
    aliases = {i: i for i in range(len(arrays))} if in_place else {}
    return _pallas(body, name=name, in_specs=[HBM] * len(arrays), out_specs=[HBM] * len(out_shapes), out_shape=out_shapes,
                   scratch_shapes=scratch, input_output_aliases=aliases,
                   compiler_params=pltpu.CompilerParams(has_side_effects=True))(*arrays)


def _shard_slab(w, chip_arr, name):
    L, R, C = w.shape
    tr = _tile(R, ROW_TILE, 16)

    def body(j_ref, w_ref, out_ref):
        out_ref[...] = w_ref[...].astype(BF16)

    grid_spec = pltpu.PrefetchScalarGridSpec(
        num_scalar_prefetch=1, grid=(L, R // tr),
        in_specs=[pl.BlockSpec((None, tr, C), lambda l, i, j_ref: (l, i, 0))],
        out_specs=pl.BlockSpec((None, None, tr, C), lambda l, i, j_ref: (j_ref[0], l, i, 0)))
    return _pallas(body, name=name, grid_spec=grid_spec, out_shape=jax.ShapeDtypeStruct((N_CHIPS, L, R, C), BF16),
                   compiler_params=_cparams("parallel", "parallel"))(chip_arr, w)


class Rider:
    def __init__(self, reads, inplace, n_sems, stages):
        self.reads, self.inplace, self.n_sems, self.stages = list(reads), list(inplace), n_sems, stages

    def args(self):
        return self.reads + self.inplace

    def in_specs(self):
        return [HBM] * len(self.args())

    def out_specs(self):
        return [HBM] * len(self.inplace)

    def out_shape(self):
        return [jax.ShapeDtypeStruct(a.shape, a.dtype) for a in self.inplace]

    def scratch(self):
        return [pltpu.SemaphoreType.DMA((self.n_sems,)), pltpu.SemaphoreType.DMA((self.n_sems,))]

    def aliases(self, n_host_in, n_host_out):
        return {n_host_in + len(self.reads) + j: n_host_out + j for j in range(len(self.inplace))}

    def split(self, rest, n_host_out, n_host_scratch):
        n_r, n_io = len(self.reads), len(self.inplace)
        reads = rest[:n_r]
        host_out = rest[n_r + n_io:n_r + n_io + n_host_out]
        outs = rest[n_r + n_io + n_host_out:n_r + 2 * n_io + n_host_out]
        host_scratch = rest[n_r + 2 * n_io + n_host_out:n_r + 2 * n_io + n_host_out + n_host_scratch]
        sems = rest[n_r + 2 * n_io + n_host_out + n_host_scratch:]

        def ride(step, n_steps):
            at = [0, n_steps - 1] if len(self.stages) == 2 else [0, (3 * n_steps) // 4, n_steps - 1]
            for s, stage in zip(at, self.stages):
                pl.when(step == s)(functools.partial(stage, reads, outs, *sems))

        return tuple(host_out) + tuple(host_scratch), ride

    def run(self, name):
        n_r, n_io = len(self.reads), len(self.inplace)

        def body(*refs):
            for stage in self.stages:
                stage(refs[:n_r], refs[n_r + n_io:n_r + 2 * n_io], *refs[n_r + 2 * n_io:])

        return _pallas(body, name=name, in_specs=self.in_specs(), out_specs=self.out_specs(), out_shape=self.out_shape(),
                       scratch_shapes=self.scratch(), input_output_aliases=self.aliases(0, 0),
                       compiler_params=pltpu.CompilerParams(has_side_effects=True))(*self.args())


def _allgather_rider(slabs, items):
    def stage(which, reads, outs, send_sems, recv_sems):
        x, y, c = _place()
        me, sibling = (x, y, c), (x, y, 1 - c)
        copy = functools.partial(_remote_copy, send_sems, recv_sems)
        for n, (j, l) in enumerate(items):
            ref = outs[j]
            h = ref.shape[2] // 2
            mine, theirs = pl.ds(c * h, h), pl.ds((1 - c) * h, h)
            own = ref.at[2 * x + y, l, mine]
            for k, (cx, cy) in enumerate(_other_chips(x, y)):
                arrived = ref.at[2 * cx + cy, l, mine]
                if which == 0:
                    copy(6 * n + k, own, own, (cx, cy, c)).start()
                elif which == 1:
                    copy(6 * n + k, arrived, arrived, me).wait_recv()
                    copy(6 * n + 3 + k, arrived, arrived, sibling).start()
                else:
                    passed = ref.at[2 * cx + cy, l, theirs]
                    copy(6 * n + 3 + k, passed, passed, me).wait_recv()
                    copy(6 * n + k, own, own, me).wait_send()
                    copy(6 * n + 3 + k, arrived, arrived, me).wait_send()

    return Rider([], slabs, 6 * len(items), [functools.partial(stage, w) for w in range(3)])


def _exchange_rider(grads, landing, items):
    def stage(start, reads, outs, send_sems, recv_sems):
        x, y, c = _place()
        for n, (i, j, l) in enumerate(items):
            g_ref, r_ref = reads[i], outs[j]
            h = g_ref.shape[1] // 2
            for k in range(1, N_DEV):
                px, py, pc = (1 - x if k & 4 else x, 1 - y if k & 2 else y, 1 - c if k & 1 else c)
                cp = _remote_copy(send_sems, recv_sems, 7 * n + k - 1, g_ref.at[2 * px + py, pl.ds(pc * h, h)],
                                  r_ref.at[l, k - 1], (px, py, pc))
                if start:
                    cp.start()
                else:
                    cp.wait()

    return Rider(grads, landing, 7 * len(items), [functools.partial(stage, True), functools.partial(stage, False)])


def _reduce_partials(g, r, chip_arr, c_arr, name):
    L, n, R, C = g.shape
    H = R // 2
    th = _tile(H, ROW_TILE, 16)
    nb = H // th

    def body(j_ref, c_ref, g_ref, r_ref, out_ref):
        s = g_ref[...].astype(F32)
        for k in range(N_DEV - 1):
            s = s + r_ref[k].astype(F32)
        out_ref[...] = s

    grid_spec = pltpu.PrefetchScalarGridSpec(
        num_scalar_prefetch=2, grid=(L, nb),
        in_specs=[pl.BlockSpec((None, None, th, C), lambda l, i, j_ref, c_ref: (l, j_ref[0], c_ref[0] * nb + i, 0)),
                  pl.BlockSpec((None, N_DEV - 1, th, C), lambda l, i, j_ref, c_ref: (l, 0, i, 0))],
        out_specs=pl.BlockSpec((None, th, C), lambda l, i, j_ref, c_ref: (l, c_ref[0] * nb + i, 0)))
    return _pallas(body, name=name, grid_spec=grid_spec, out_shape=jax.ShapeDtypeStruct((L, R, C), F32),
                   compiler_params=_cparams("parallel", "parallel"))(chip_arr, c_arr, g, r)


def _join_halves(fs):
    n = len(fs)

    def body(*refs):
        out_refs, (send_sems, recv_sems) = refs[n:2 * n], refs[2 * n:]
        x, y, c = _place()
        cps = []
        for i, out_ref in enumerate(out_refs):
            h = out_ref.shape[1] // 2
            mine = out_ref.at[:, pl.ds(c * h, h)]
            cps.append(_remote_copy(send_sems, recv_sems, i, mine, mine, (x, y, 1 - c)))
            cps[-1].start()
        for cp in cps:
            cp.wait()

    return _comm_call(body, "grad_join_halves", fs, [jax.ShapeDtypeStruct(f.shape, f.dtype) for f in fs], n, in_place=True)


def _allgather_small(v, name):
    R, C = v.shape

    def body(v_ref, out_ref, send_sems, recv_sems):
        x, y, c = _place()
        me = 4 * x + 2 * y + c
        out_ref[me] = v_ref[...]
        cps = []
        for k in range(1, N_DEV):
            peer = (1 - x if k & 4 else x, 1 - y if k & 2 else y, 1 - c if k & 1 else c)
            cps.append(pltpu.make_async_remote_copy(src_ref=v_ref, dst_ref=out_ref.at[me], send_sem=send_sems.at[k - 1],
                                                    recv_sem=recv_sems.at[k - 1], device_id=peer, device_id_type=MESH))
        for cp in cps:
            cp.start()
        for cp in cps:
            cp.wait()

    return _pallas(body, name=name, in_specs=[VMEM_SPEC], out_specs=VMEM_SPEC,
                   out_shape=jax.ShapeDtypeStruct((N_DEV, R, C), v.dtype),
                   scratch_shapes=[pltpu.SemaphoreType.DMA((N_DEV - 1,)), pltpu.SemaphoreType.DMA((N_DEV - 1,))],
                   compiler_params=pltpu.CompilerParams(has_side_effects=True))(v)


def _sum_devices(g):
    n, R, C = g.shape

    def body(g_ref, out_ref):
        s = g_ref[0]
        for d in range(1, n):
            s = s + g_ref[d]
        out_ref[...] = s

    return _pallas(body, name="sum_devices", in_specs=[VMEM_SPEC], out_specs=VMEM_SPEC,
                   out_shape=jax.ShapeDtypeStruct((R, C), g.dtype))(g)


def _pad_heads(a, width):
    lead = a.shape[:-1]
    a = a.reshape(lead + (N_HEADS, width))
    a = jnp.pad(a, [(0, 0)] * len(lead) + [(0, 0), (0, HEAD_PAD - width)])
    return a.reshape(lead + (HP,))


def _unpad_heads(a, width):
    lead = a.shape[:-1]
    return a.reshape(lead + (N_HEADS, HEAD_PAD))[..., :width].reshape(lead + (N_HEADS * width,))


def _pre_weights(slabs, l):
    cols = lambda name: slabs[name][:, l].transpose(1, 0, 2).reshape(slabs[name].shape[2], -1)
    w_in = cols("w_in")
    kpe = jnp.pad(w_in[:, 640:672], ((0, 0), (ROPE_LO, HEAD_PAD - ROPE_HI)))
    w_ukv = cols("w_ukv").reshape(KV_LORA, N_HEADS, QK_NOPE + V_HEAD)
    return dict(
        w_in=jnp.concatenate([w_in[:, :640], kpe, w_in[:, 672:]], axis=1),
        w_uq=_pad_heads(cols("w_uq"), QK_HEAD),
        w_ukv=jnp.concatenate([_pad_heads(w_ukv[..., :QK_NOPE].reshape(KV_LORA, -1), QK_NOPE),
                               _pad_heads(w_ukv[..., QK_NOPE:].reshape(KV_LORA, -1), V_HEAD)], axis=1),
    )


def _post_weights(slabs, l):
    rows = lambda name: slabs[name][:, l].reshape(-1, slabs[name].shape[3])
    cols = lambda name: slabs[name][:, l].transpose(1, 0, 2).reshape(slabs[name].shape[2], -1)
    w_o = rows("w_o")
    return dict(
        w_o=jnp.concatenate([_pad_heads(w_o[:ATTN_WIDTH].T, V_HEAD).T, w_o[ATTN_WIDTH:]], axis=0),
        w_up=cols("w_up"), w_down=rows("w_down"), w_ple_gate=rows("w_ple_gate"), w_ple=cols("w_ple"),
    )


def _gains(small, l):
    row = lambda name: small[name][l].reshape(1, -1)
    headrow = lambda a, b: jnp.pad(jnp.concatenate([small[a][l], small[b][l]]), (0, HEAD_PAD - QK_HEAD)).reshape(1, HEAD_PAD)
    return dict(
        g_mix=row("g_mix"), g_q_lat=row("g_q_lat"), g_kv_lat=row("g_kv_lat"), g_mlp=row("g_mlp"), g_ple=row("g_ple"),
        g_out_conv=row("g_out_conv"), g_out_attn=_pad_heads(small["g_out_attn"][l], V_HEAD).reshape(1, HP),
        gq=headrow("g_qn_nope", "g_qn_rope"), gk=headrow("g_kn_nope", "g_kn_rope"),
        cw8=jnp.pad(small["conv_w"][l], ((0, SUBLANES - CONV_TAPS), (0, 0))),
        head_mats=_head_matrices(),
    )


def _w_o_shards(dw_o):
    return jnp.concatenate([_unpad_heads(dw_o[:HP].T, V_HEAD).T, dw_o[HP:]], axis=0).reshape(N_CHIPS, -1, D_MODEL)


def _unpad_grads(gp):
    dw_in = gp["w_in"]
    dw_ukv = gp["w_ukv"]
    k_part = dw_ukv[:, :HP].reshape(KV_LORA, N_HEADS, HEAD_PAD)[..., :QK_NOPE]
    v_part = dw_ukv[:, HP:].reshape(KV_LORA, N_HEADS, HEAD_PAD)[..., :V_HEAD]
    first = lambda name: gp[name][0]
    col_shards = lambda a: a.reshape(a.shape[0], N_CHIPS, -1).transpose(1, 0, 2)
    return dict(
        w_in=col_shards(jnp.concatenate([dw_in[:, :640], dw_in[:, Z_KPE + ROPE_LO:Z_KPE + ROPE_HI], dw_in[:, Z_GB:]], axis=1)),
        w_uq=col_shards(_unpad_heads(gp["w_uq"], QK_HEAD)),
        w_ukv=col_shards(jnp.concatenate([k_part, v_part], axis=-1).reshape(KV_LORA, -1)),
        w_o=gp["w_o"], w_up=gp["w_up"], w_down=gp["w_down"], w_ple_gate=gp["w_ple_gate"], w_ple=gp["w_ple"],
        g_mix=first("g_mix"), g_q_lat=first("g_q_lat"), g_kv_lat=first("g_kv_lat"), g_mlp=first("g_mlp"),
        g_ple=first("g_ple"), g_out_conv=first("g_out_conv"), g_out_attn=_unpad_heads(first("g_out_attn"), V_HEAD),
        g_qn_nope=gp["gq"][0, :QK_NOPE], g_qn_rope=gp["gq"][0, QK_NOPE:QK_HEAD],
        g_kn_nope=gp["gk"][0, :QK_NOPE], g_kn_rope=gp["gk"][0, QK_NOPE:QK_HEAD],
        conv_w=gp["cw8"][:CONV_TAPS],
    )


def _rope_tables(positions):
    inv_freq = 1.0 / (ROPE_THETA ** (jnp.arange(0, QK_ROPE, 2, dtype=F32) / QK_ROPE))
    ang = positions.astype(F32)[:, None] * inv_freq
    cos, sin = jnp.cos(ang), jnp.sin(ang)
    pad = lambda t, v: jnp.pad(jnp.concatenate([t, t], axis=1), ((0, 0), (ROPE_LO, HEAD_PAD - ROPE_HI)), constant_values=v)
    return pad(cos, 1.0), pad(sin, 0.0)


def _layer_fwd(x0, h, p_l, W, cs, sn, attend, g_next):
    z = _mm(h, W["w_in"], name="mm_in")
    qln, kvln = _lat_fwd(z, W["g_q_lat"], W["g_kv_lat"])
    q_raw = _mm(qln, W["w_uq"], name="mm_uq")
    kv_raw = _mm(kvln, W["w_ukv"], name="mm_ukv")
    qf, kf, vb = _qk_fwd(q_raw, kv_raw, z, cs, sn, W["gq"], W["gk"], *W["head_mats"])
    o, lse = attend(qf, kf, vb)
    conv = _conv_fwd(z, W["cw8"])
    mixed = _mix_fwd(o, conv, W["g_out_attn"], W["g_out_conv"])
    x1, h2 = _mm(mixed, W["w_o"], res=x0, gain=W["g_mlp"], epi="rms", name="mm_o")
    a, f = _mm(h2, W["w_up"], epi="relu2", name="mm_up")
    x2, h3 = _mm(f, W["w_down"], res=x1, gain=W["g_ple"], epi="rms", name="mm_down")
    gl = _mm(h3, W["w_ple_gate"], name="mm_ple_gate")
    pe = _mm(p_l, W["w_ple"], name="mm_ple")
    x3, h_next = _ple_fwd(x2, gl, pe, g_next)
    saved = dict(x0=x0, h=h, z=z, qln=qln, kvln=kvln, q_raw=q_raw, kv_raw=kv_raw, qf=qf, kf=kf, vb=vb, o=o, lse=lse,
                 conv=conv, mixed=mixed, x1=x1, h2=h2, a=a, f=f, x2=x2, h3=h3, gl=gl, pe=pe)
    return x3, h_next, saved


def _layer_bwd(dx3, p_l, W, cs, sn, sv, attend_bwd):
    g = {}
    dpe, dgl = _ple_bwd(dx3, sv["gl"], sv["pe"])
    g["w_ple"] = _mm(p_l, dpe, mode="tn", out_dtype=BF16, shard_out=1, name="mm_dw_ple")
    g["w_ple_gate"] = _mm(sv["h3"], dgl, mode="tn", out_dtype=BF16, shard_out=0, name="mm_dw_ple_gate")
    dx2, dx2b, g["g_ple"] = _mm(dgl, W["w_ple_gate"], mode="nt", res=dx3, aux=sv["x2"], gain=W["g_ple"], epi="rms_bwd",
                                name="mm_dh3")
    da = _mm(dx2b, W["w_down"], mode="nt", aux=sv["a"], epi="drelu2", out_dtype=BF16, name="mm_da")
    g["w_down"] = _mm(sv["f"], dx2b, mode="tn", out_dtype=BF16, shard_out=0, name="mm_dw_down")
    g["w_up"] = _mm(sv["h2"], da, mode="tn", out_dtype=BF16, shard_out=1, name="mm_dw_up")
    dx1, dx1b, g["g_mlp"] = _mm(da, W["w_up"], mode="nt", res=dx2, aux=sv["x1"], gain=W["g_mlp"], epi="rms_bwd",
                                name="mm_dh2")
    dmixed = _mm(dx1b, W["w_o"], mode="nt", name="mm_dmixed")
    g["w_o"] = _mm(sv["mixed"], dx1b, mode="tn", out_dtype=BF16, name="mm_dw_o")
    do, dconv, g["g_out_attn"], g["g_out_conv"] = _mix_bwd(dmixed, sv["o"], sv["conv"], W["g_out_attn"], W["g_out_conv"])
    dgb, dgc, dxin, g["cw8"] = _conv_bwd(dconv, sv["z"], W["cw8"])
    dqf, dkf, dv = attend_bwd(g, sv["qf"], sv["kf"], sv["vb"], sv["o"], do, sv["lse"])
    dq_raw, dkv_raw, dkpe, g["gq"], g["gk"] = _qk_bwd(dqf, dkf, dv, sv["q_raw"], sv["kv_raw"], sv["z"], cs, sn, W["gq"], W["gk"],
                                                      *W["head_mats"])
    g["w_uq"] = _mm(sv["qln"], dq_raw, mode="tn", out_dtype=BF16, name="mm_dw_uq")
    dqln = _mm(dq_raw, W["w_uq"], mode="nt", name="mm_dqln")
    g["w_ukv"] = _mm(sv["kvln"], dkv_raw, mode="tn", out_dtype=BF16, name="mm_dw_ukv")
    dkvln = _mm(dkv_raw, W["w_ukv"], mode="nt", name="mm_dkvln")
    dlat, g["g_q_lat"], g["g_kv_lat"] = _lat_bwd(dqln, dkvln, sv["z"], W["g_q_lat"], W["g_kv_lat"])
    dz = jnp.concatenate([dlat, dkpe, dgb, dgc, dxin], axis=1)
    g["w_in"] = _mm(sv["h"], dz, mode="tn", out_dtype=BF16, name="mm_dw_in")
    dx0, _, g["g_mix"] = _mm(dz, W["w_in"], mode="nt", res=dx1, aux=sv["x0"], gain=W["g_mix"], epi="rms_bwd", name="mm_dh")
    return dx0, g


def _local_step(x, p, positions, target, slabs, small):
    depth = p.shape[0]
    cs, sn = _rope_tables(positions)
    slabs = dict(slabs)

    def gather(items, host):
        touched = [n for n in PRE + POST if any(n == name for name, _ in items)]
        rider = _allgather_rider([slabs[n] for n in touched], [(touched.index(name), l) for name, l in items])
        if host is None:
            out, new = (), rider.run("allgather_first")
        else:
            *out, new = host(rider)
        slabs.update(zip(touched, new))
        return out

    gather([(name, 0) for name in PRE], None)
    Ws, saved = [], []
    gains = [_gains(small, l) for l in range(depth)]
    h = _rms_fwd(x, gains[0]["g_mix"], "rms_mix")
    for l in range(depth):
        W = dict(gains[l], **_pre_weights(slabs, l))

        def attend(qf, kf, vb, W=W, l=l):
            items = [(name, l) for name in POST] + ([(name, l + 1) for name in PRE] if l + 1 < depth else [])
            o, lse = gather(items, functools.partial(_attn_fwd, qf, kf, vb))
            W.update(_post_weights(slabs, l))
            return o, lse

        g_next = gains[l + 1]["g_mix"] if l + 1 < depth else jnp.ones_like(gains[l]["g_mix"])
        x, h, sv = _layer_fwd(x, h, p[l], W, cs, sn, attend, g_next)
        Ws.append(W)
        saved.append(sv)
    dx, sq = _loss_grad(x, target)

    landing = {name: lax.empty((depth, N_DEV - 1, slabs[name].shape[2] // 2, slabs[name].shape[3]), BF16) for name in PRE + POST}

    def exchange(sends, host):
        touched = [n for n in PRE + POST if any(n == name for name, _, _ in sends)]
        rider = _exchange_rider([g for _, _, g in sends], [landing[n] for n in touched],
                                [(i, touched.index(name), l) for i, (name, l, _) in enumerate(sends)])
        if host is None:
            out, new = (), rider.run("grad_exchange_last")
        else:
            *out, new = host(rider)
        landing.update(zip(touched, new))
        return out

    grads = [None] * depth
    for l in reversed(range(depth)):
        W = Ws[l]

        def attend_bwd(g, qf, kf, vb, o, do, lse, l=l):
            g["w_o"] = _w_o_shards(g["w_o"])
            sends = [(name, l, g[name]) for name in POST]
            if l + 1 < depth:
                sends += [(name, l + 1, grads[l + 1][name]) for name in PRE]
            return exchange(sends, functools.partial(_attn_bwd, qf, kf, vb, o, do, lse))

        dx, gp = _layer_bwd(dx, p[l], W, cs, sn, saved[l], attend_bwd)
        grads[l] = _unpad_grads(gp)
    exchange([(name, 0, grads[0][name]) for name in PRE], None)
    return sq, dx, grads, landing


def _pack_rows(vals, entries):
    depth = vals[entries[0][0]].shape[0]
    return jnp.concatenate([jnp.pad(vals[name].reshape(depth, -1), ((0, 0), (0, _pad128(n) - n))) for name, n in entries], axis=1)


def _unpack_rows(packed, entries):
    out, off = {}, 0
    for name, n in entries:
        out[name] = packed[:, off:off + n]
        off += _pad128(n)
    return out


def kernel(x, p, positions, g_mix, w_in, g_q_lat, w_uq, g_kv_lat, w_ukv, g_qn_nope, g_qn_rope, g_kn_nope, g_kn_rope, conv_w, g_out_attn, g_out_conv, w_o, g_mlp, w_up, w_down, g_ple, w_ple_gate, w_ple, loss_target, m_g_mix, m_w_in, m_g_q_lat, m_w_uq, m_g_kv_lat, m_w_ukv, m_g_qn_nope, m_g_qn_rope, m_g_kn_nope, m_g_kn_rope, m_conv_w, m_g_out_attn, m_g_out_conv, m_w_o, m_g_mlp, m_w_up, m_w_down, m_g_ple, m_w_ple_gate, m_w_ple, v_g_mix, v_w_in, v_g_q_lat, v_w_uq, v_g_kv_lat, v_w_ukv, v_g_qn_nope, v_g_qn_rope, v_g_kn_nope, v_g_kn_rope, v_conv_w, v_g_out_attn, v_g_out_conv, v_w_o, v_g_mlp, v_w_up, v_w_down, v_g_ple, v_w_ple_gate, v_w_ple):
    w = dict(g_mix=g_mix, w_in=w_in, g_q_lat=g_q_lat, w_uq=w_uq, g_kv_lat=g_kv_lat, w_ukv=w_ukv, g_qn_nope=g_qn_nope,
             g_qn_rope=g_qn_rope, g_kn_nope=g_kn_nope, g_kn_rope=g_kn_rope, conv_w=conv_w, g_out_attn=g_out_attn,
             g_out_conv=g_out_conv, w_o=w_o, g_mlp=g_mlp, w_up=w_up, w_down=w_down, g_ple=g_ple, w_ple_gate=w_ple_gate,
             w_ple=w_ple)
    m = dict(g_mix=m_g_mix, w_in=m_w_in, g_q_lat=m_g_q_lat, w_uq=m_w_uq, g_kv_lat=m_g_kv_lat, w_ukv=m_w_ukv,
             g_qn_nope=m_g_qn_nope, g_qn_rope=m_g_qn_rope, g_kn_nope=m_g_kn_nope, g_kn_rope=m_g_kn_rope, conv_w=m_conv_w,
             g_out_attn=m_g_out_attn, g_out_conv=m_g_out_conv, w_o=m_w_o, g_mlp=m_g_mlp, w_up=m_w_up, w_down=m_w_down,
             g_ple=m_g_ple, w_ple_gate=m_w_ple_gate, w_ple=m_w_ple)
    v = dict(g_mix=v_g_mix, w_in=v_w_in, g_q_lat=v_g_q_lat, w_uq=v_w_uq, g_kv_lat=v_g_kv_lat, w_ukv=v_w_ukv,
             g_qn_nope=v_g_qn_nope, g_qn_rope=v_g_qn_rope, g_kn_nope=v_g_kn_nope, g_kn_rope=v_g_kn_rope, conv_w=v_conv_w,
             g_out_attn=v_g_out_attn, g_out_conv=v_g_out_conv, w_o=v_w_o, g_mlp=v_g_mlp, w_up=v_w_up, w_down=v_w_down,
             g_ple=v_g_ple, w_ple_gate=v_w_ple_gate, w_ple=v_w_ple)
    depth = p.shape[0]
    ax, ay, ac = _place()
    chip = 2 * ax + ay
    c_arr = jnp.reshape(ac, (1,)).astype(jnp.int32)
    chip_arr = jnp.reshape(chip, (1,)).astype(jnp.int32)
    conv_shard =("conv_w", CONV_TAPS * CONV_WIDTH // N_CHIPS)
    conv_full = ("conv_w", CONV_TAPS * CONV_WIDTH)

    names = PRE + POST
    slabs = {name: _shard_slab(w[name], chip_arr, "shard_slab_" + name) for name in names}
    conv_rows = -(-depth * CONV_TAPS // SUBLANES) * SUBLANES
    conv_all = _allgather_small(jnp.pad(conv_w.reshape(depth * CONV_TAPS, LANES), ((0, conv_rows - depth * CONV_TAPS), (0, 0))),
                                "allgather_conv_w")
    conv_cat = jnp.concatenate([conv_all[2 * j, :depth * CONV_TAPS] for j in range(N_CHIPS)], axis=1)
    small = {name: w[name] for name, _ in SMALL}
    small["conv_w"] = conv_cat.reshape(depth, CONV_TAPS, CONV_WIDTH)

    sq, grad_x, grads, landing = _local_step(x[0], p[:, 0], positions[0], loss_target[0], slabs, small)

    halves = [_reduce_partials(jnp.stack([grads[l][name] for l in range(depth)]), landing[name], chip_arr, c_arr,
                               "grad_reduce_" + name) for name in names]
    g_out = dict(zip(names, _join_halves(halves)))

    reduced = SMALL + (conv_full, ("loss", LANES))
    stacked = {name: jnp.stack([grads[l][name] for l in range(depth)]) for name, _ in SMALL + (conv_full,)}
    stacked["loss"] = jnp.broadcast_to(sq[:1] * (0.5 / D_MODEL), (depth, LANES))
    g_small = _unpack_rows(_sum_devices(_allgather_small(_pack_rows(stacked, reduced), "allgather_small_grads")), reduced)
    loss = g_small["loss"][0, 0]
    for name, n in SMALL:
        g_out[name] = g_small[name]
    g_conv = g_small["conv_w"].reshape(depth, CONV_TAPS, CONV_WIDTH)
    g_out["conv_w"] = lax.dynamic_slice_in_dim(g_conv, chip * LANES, LANES, axis=2)

    delta, new_m, new_v = {}, {}, {}
    for name in names:
        delta[name], new_m[name], new_v[name] = _adamw(w[name], g_out[name], m[name], v[name], "adamw_" + name)
    local = SMALL + (conv_shard,)
    pack = lambda vals: _pack_rows(vals, local)[None]
    d, m2, v2 = _adamw(pack(w), pack(g_out), pack(m), pack(v), "adamw_small")
    for res, packed_res in ((delta, d), (new_m, m2), (new_v, v2)):
        un = _unpack_rows(packed_res[0], local)
        for name, _ in local:
            res[name] = un[name].reshape(w[name].shape)

    return (loss, grad_x[None], *[g_out[n] for n in WEIGHT_ORDER], *[delta[n] for n in WEIGHT_ORDER],
            *[new_m[n] for n in WEIGHT_ORDER], *[new_v[n] for n in WEIGHT_ORDER])
```

```python
import functools

import jax
import jax.numpy as jnp
from jax import lax
from jax.experimental import pallas as pl
from jax.experimental.pallas import tpu as pltpu

F32 = jnp.float32
BF16 = jnp.bfloat16
MESH = pl.DeviceIdType.MESH

D_MODEL = 1024
N_HEADS = 8
QK_NOPE = 64
QK_ROPE = 32
QK_HEAD = QK_NOPE + QK_ROPE
V_HEAD = 64
Q_LORA = 384
KV_LORA = 256
ATTN_WIDTH = N_HEADS * V_HEAD
CONV_WIDTH = 512
CONV_TAPS = 3
D_FF = 4096
PLE_DIM = 256
ROPE_THETA = 10000.0
EPS = 1e-6
ATT_SCALE = QK_HEAD ** -0.5
LOG2E = 1.4426950408889634
ATT_SCALE_LOG2 = ATT_SCALE * LOG2E

ADAM_LR = 0.001
ADAM_B1 = 0.9
ADAM_B2 = 0.999
ADAM_EPS = 1e-08
ADAM_WD = 0.01
ADAM_STEP = 10

LANES = 128
SUBLANES = 8
HEAD_PAD = LANES
HP = N_HEADS * HEAD_PAD
ROPE_LO = QK_NOPE
ROPE_MID = QK_NOPE + QK_ROPE // 2
ROPE_HI = QK_NOPE + QK_ROPE
VMEM_LIMIT = 56 * 1024 * 1024

Z_Q, Z_KV, Z_KPE, Z_GB, Z_GC, Z_XIN = 0, 384, 640, 768, 1280, 1792
Z_COLS = 2304
Z_LAT = Z_KPE

ROW_TILE = 512
ATT_TILE = 256
ATT_CHAINS_FWD, ATT_CHAINS_BWD = 8, 4
MM_TM, MM_TN, MM_TK = 1024, 1024, 4096
MM_TM_MIN, MM_TK_MIN = 256, 512
MM_VMEM_BUDGET = 40 * 1024 * 1024

N_CHIPS = 4
N_DEV = 8

PRE = ("w_in", "w_uq", "w_ukv")
POST = ("w_o", "w_up", "w_down", "w_ple_gate", "w_ple")
ROW_SHARDED = ("w_o", "w_down", "w_ple_gate")
SMALL = (
    ("g_mix", 1024), ("g_q_lat", 384), ("g_kv_lat", 256), ("g_qn_nope", 64), ("g_qn_rope", 32), ("g_kn_nope", 64),
    ("g_kn_rope", 32), ("g_out_attn", 512), ("g_out_conv", 512), ("g_mlp", 1024), ("g_ple", 1024),
)
WEIGHT_ORDER = ("g_mix", "w_in", "g_q_lat", "w_uq", "g_kv_lat", "w_ukv", "g_qn_nope", "g_qn_rope", "g_kn_nope",
                "g_kn_rope", "conv_w", "g_out_attn", "g_out_conv", "w_o", "g_mlp", "w_up", "w_down", "g_ple",
                "w_ple_gate", "w_ple")


def _pallas(body, **kw):
    return pl.pallas_call(body, **kw)


def _cparams(*sem):
    return pltpu.CompilerParams(dimension_semantics=sem, vmem_limit_bytes=VMEM_LIMIT)


def _tile(dim, pref, unit=LANES):
    if dim <= pref:
        return dim
    t = (pref // unit) * unit
    while t > unit and dim % t:
        t -= unit
    assert dim % t == 0, (dim, pref)
    return t


def _pad128(n):
    return -(-n // LANES) * LANES


_DIMS = {"nn": (((1,), (0,)), ((), ())), "nt": (((1,), (1,)), ((), ())), "tn": (((0,), (0,)), ((), ()))}


def _mm(a, b, *, mode="nn", res=None, aux=None, gain=None, epi=None, out_dtype=F32, shard_out=None, name):
    whole_rows = epi in ("rms", "rms_bwd")
    b, layer = b if isinstance(b, tuple) else (b, None)
    b_shape = b.shape if layer is None else b.shape[1:]
    if mode == "nn":
        (M, K), (K2, N) = a.shape, b_shape
    elif mode == "nt":
        (M, K), (N, K2) = a.shape, b_shape
    else:
        (K, M), (K2, N) = a.shape, b_shape
    assert K == K2, (a.shape, b_shape, mode)
    n_lim = k_lim = None
    m_lim = M // N_CHIPS if shard_out == 0 else None
    if shard_out == 1:
        n_lim = N // N_CHIPS
    extra = [t for t in (res, aux) if t is not None]
    out_bytes = 6 if epi in ("relu2", "rms", "rms_bwd") else jnp.dtype(out_dtype).itemsize
    if whole_rows:
        assert (n_lim or N) == N and N <= MM_TN and shard_out is None and gain is not None
        m_lim = min(M, MM_TM // 2)

    def footprint(tm, tn, tk):
        blocks = tm * tk * a.dtype.itemsize + tk * tn * 2 + tm * tn * (out_bytes + 4 * len(extra))
        return 2 * blocks + (tm * tn * 4 if tk < K else 0)

    tm, tn, tk = _tile(m_lim or M, MM_TM), _tile(n_lim or N, MM_TN), _tile(k_lim or K, MM_TK)
    while footprint(tm, tn, tk) > MM_VMEM_BUDGET and tk > MM_TK_MIN:
        tk = _tile(K, tk // 2)
    while footprint(tm, tn, tk) > MM_VMEM_BUDGET and tm > MM_TM_MIN:
        tm = _tile(M, tm // 2)
    nk = K // tk
    a_spec = pl.BlockSpec((tk, tm), lambda i, j, k: (k, i)) if mode == "tn" else pl.BlockSpec((tm, tk), lambda i, j, k: (i, k))
    if mode == "nt":
        b_block, b_rc = (tn, tk), (lambda i, j, k: (j, k))
    else:
        b_block, b_rc = (tk, tn), (lambda i, j, k: (k, j))
    if layer is None:
        b_spec = pl.BlockSpec(b_block, b_rc)
    else:
        b_spec = pl.BlockSpec((None,) + b_block, lambda i, j, k: (layer,) + b_rc(i, j, k))
    mn_spec = pl.BlockSpec((tm, tn), lambda i, j, k: (i, j))
    dims = _DIMS[mode]
    n_out = {"relu2": 2, "rms": 2, "rms_bwd": 3}.get(epi, 1)

    def body(*refs):
        a_ref, b_ref = refs[0], refs[1]
        extra_refs = refs[2:2 + len(extra)]
        gain_ref = refs[2 + len(extra)] if whole_rows else None
        out_refs = refs[2 + len(extra) + whole_rows:][:n_out]
        prod = lax.dot_general(a_ref[...].astype(BF16), b_ref[...].astype(BF16), dims, preferred_element_type=F32)

        def finish(r):
            if epi == "rms_bwd":
                dx, dgc = _norm_bwd(extra_refs[1][...], r, gain_ref[...], N)
                dx = dx + extra_refs[0][...]
                out_refs[0][...] = dx
                out_refs[1][...] = dx.astype(BF16)
                _accumulate(out_refs[2], dgc)
                return
            if res is not None:
                r = r + extra_refs[0][...]
            if epi == "rms":
                out_refs[0][...] = r
                out_refs[1][...] = (r * _rinv(r, N) * gain_ref[...]).astype(BF16)
            elif epi == "relu2":
                out_refs[0][...] = r.astype(out_dtype)
                t = jnp.maximum(r, 0.0)
                out_refs[1][...] = (t * t).astype(BF16)
            elif epi == "drelu2":
                out_refs[0][...] = (r * (2.0 * jnp.maximum(extra_refs[-1][...].astype(F32), 0.0))).astype(out_dtype)
            else:
                out_refs[0][...] = r.astype(out_dtype)

        if nk == 1:
            finish(prod)
        else:
            acc = refs[-1]
            k = pl.program_id(2)

            @pl.when(k == 0)
            def _():
                acc[...] = prod

            @pl.when(k > 0)
            def _():
                acc[...] += prod

            @pl.when(k == nk - 1)
            def _():
                finish(acc[...])

    if epi in ("relu2", "rms", "rms_bwd"):
        out_shape = [jax.ShapeDtypeStruct((M, N), out_dtype if epi == "relu2" else F32), jax.ShapeDtypeStruct((M, N), BF16)]
        out_specs = [mn_spec, mn_spec]
        if epi == "rms_bwd":
            out_shape.append(jax.ShapeDtypeStruct((SUBLANES, N), F32))
            out_specs.append(pl.BlockSpec((SUBLANES, tn), lambda i, j, k: (0, j)))
    elif shard_out == 0:
        per = (M // N_CHIPS) // tm
        out_shape = jax.ShapeDtypeStruct((N_CHIPS, M // N_CHIPS, N), out_dtype)
        out_specs = pl.BlockSpec((None, tm, tn), lambda i, j, k: (i // per, i % per, j))
    elif shard_out == 1:
        per = (N // N_CHIPS) // tn
        out_shape = jax.ShapeDtypeStruct((N_CHIPS, M, N // N_CHIPS), out_dtype)
        out_specs = pl.BlockSpec((None, tm, tn), lambda i, j, k: (j // per, i, j % per))
    else:
        out_shape = jax.ShapeDtypeStruct((M, N), out_dtype)
        out_specs = mn_spec
    gains = [gain] if whole_rows else []
    return _pallas(
        body, name=name, grid=(M // tm, N // tn, nk),
        in_specs=[a_spec, b_spec] + [mn_spec] * len(extra) + [pl.BlockSpec((1, tn), lambda i, j, k: (0, j))] * len(gains),
        out_specs=out_specs, out_shape=out_shape,
        scratch_shapes=[pltpu.VMEM((tm, tn), F32)] if nk > 1 else [],
        compiler_params=_cparams(*(["arbitrary"] * 3 if epi == "rms_bwd" else ["parallel", "parallel", "arbitrary"])),
    )(a, b, *extra, *gains)


def _rows(ts, d, col=0):
    return pl.BlockSpec((ts, d), lambda i: (i, col))


def _gain(d):
    return pl.BlockSpec((1, d), lambda i: (0, 0))


def _accum(d):
    return pl.BlockSpec((SUBLANES, d), lambda i: (0, 0))


def _accumulate(ref, val):
    i = pl.program_id(0)

    @pl.when(i == 0)
    def _():
        ref[...] = jnp.zeros_like(ref)

    ref[...] += jnp.broadcast_to(jnp.sum(val, axis=0, keepdims=True), ref.shape)


def _rinv(x, n):
    return lax.rsqrt(jnp.sum(x * x, axis=-1, keepdims=True) / n + EPS)


def _norm_bwd(x, dy, g, n):
    r = _rinv(x, n)
    xhat = x * r
    dyg = dy * g
    dx = r * (dyg - xhat * (jnp.sum(dyg * xhat, axis=-1, keepdims=True) / n))
    return dx, dy * xhat


def _rms_fwd(x, g, name):
    S, D = x.shape
    ts = _tile(S, ROW_TILE, SUBLANES)

    def body(x_ref, g_ref, h_ref):
        xv = x_ref[...]
        h_ref[...] = (xv * _rinv(xv, D) * g_ref[...]).astype(BF16)

    return _pallas(body, name=name, grid=(S // ts,), in_specs=[_rows(ts, D), _gain(D)], out_specs=_rows(ts, D),
                   out_shape=jax.ShapeDtypeStruct((S, D), BF16), compiler_params=_cparams("parallel"))(x, g)


def _lat_fwd(z, gq, gkv):
    S = z.shape[0]
    ts = _tile(S, ROW_TILE, SUBLANES)

    def body(z_ref, gq_ref, gkv_ref, q_ref, kv_ref):
        zq = z_ref[:, Z_Q:Z_KV]
        zkv = z_ref[:, Z_KV:Z_KPE]
        q_ref[...] = (zq * _rinv(zq, Q_LORA) * gq_ref[...]).astype(BF16)
        kv_ref[...] = (zkv * _rinv(zkv, KV_LORA) * gkv_ref[...]).astype(BF16)

    return _pallas(body, name="lat_fwd", grid=(S // ts,), in_specs=[_rows(ts, Z_LAT), _gain(Q_LORA), _gain(KV_LORA)],
                   out_specs=[_rows(ts, Q_LORA), _rows(ts, KV_LORA)],
                   out_shape=[jax.ShapeDtypeStruct((S, Q_LORA), BF16), jax.ShapeDtypeStruct((S, KV_LORA), BF16)],
                   compiler_params=_cparams("parallel"))(z, gq, gkv)


def _lat_bwd(dq, dkv, z, gq, gkv):
    S = z.shape[0]
    ts = _tile(S, ROW_TILE, SUBLANES)

    def body(dq_ref, dkv_ref, z_ref, gq_ref, gkv_ref, dlat_ref, dgq_ref, dgkv_ref):
        dxq, cq = _norm_bwd(z_ref[:, Z_Q:Z_KV], dq_ref[...], gq_ref[...], Q_LORA)
        dxkv, ckv = _norm_bwd(z_ref[:, Z_KV:Z_KPE], dkv_ref[...], gkv_ref[...], KV_LORA)
        dlat_ref[:, Z_Q:Z_KV] = dxq.astype(BF16)
        dlat_ref[:, Z_KV:Z_KPE] = dxkv.astype(BF16)
        _accumulate(dgq_ref, cq)
        _accumulate(dgkv_ref, ckv)

    return _pallas(body, name="lat_bwd", grid=(S // ts,),
                   in_specs=[_rows(ts, Q_LORA), _rows(ts, KV_LORA), _rows(ts, Z_LAT), _gain(Q_LORA), _gain(KV_LORA)],
                   out_specs=[_rows(ts, Z_LAT), _accum(Q_LORA), _accum(KV_LORA)],
                   out_shape=[jax.ShapeDtypeStruct((S, Z_LAT), BF16), jax.ShapeDtypeStruct((SUBLANES, Q_LORA), F32),
                              jax.ShapeDtypeStruct((SUBLANES, KV_LORA), F32)],
                   compiler_params=_cparams("arbitrary"))(dq, dkv, z, gq, gkv)


def _head_masks():
    lane = lax.broadcasted_iota(jnp.int32, (1, HEAD_PAD), 1)
    return lane < ROPE_LO, (lane >= ROPE_LO) & (lane < ROPE_HI)


def _head_matrices():
    k = jnp.arange(HEAD_PAD)[:, None]
    j = jnp.arange(HEAD_PAD)[None, :]
    nope = (k < ROPE_LO) & (j < ROPE_LO)
    rope = (k >= ROPE_LO) & (k < ROPE_HI) & (j >= ROPE_LO) & (j < ROPE_HI)
    seg = jnp.where(nope, 1.0 / QK_NOPE, jnp.where(rope, 1.0 / QK_ROPE, 0.0))
    half = QK_ROPE // 2
    rot = jnp.where((j >= ROPE_LO) & (j < ROPE_MID) & (k == j + half), -1.0,
                    jnp.where((j >= ROPE_MID) & (j < ROPE_HI) & (k == j - half), 1.0, 0.0))
    return seg.astype(BF16), rot.astype(BF16)


_HEAD_MAT = pl.BlockSpec((HEAD_PAD, HEAD_PAD), lambda i: (0, 0))


def _split_dot(t, mat):
    hi = t.astype(BF16)
    lo = (t - hi.astype(F32)).astype(BF16)
    return jnp.dot(hi, mat, preferred_element_type=F32) + jnp.dot(lo, mat, preferred_element_type=F32)


def _qk_fwd(q_raw, kv_raw, z, cs, sn, gq, gk, seg, rot):
    S = q_raw.shape[0]
    ts = _tile(S, ROW_TILE, SUBLANES)

    def body(q_ref, k_ref, v_ref, kpe_ref, cs_ref, sn_ref, gq_ref, gk_ref, seg_ref, rot_ref, qf_ref, kf_ref, vb_ref):
        cos, sin, gqv, gkv = cs_ref[...], sn_ref[...], gq_ref[...], gk_ref[...]
        seg_m, rot_m = seg_ref[...], rot_ref[...]

        def norm(x, g):
            return x * lax.rsqrt(_split_dot(x * x, seg_m) + EPS) * g

        def rope(y):
            return y * cos + _split_dot(y, rot_m) * sin

        kr = rope(norm(kpe_ref[...], gkv))
        lane = lax.broadcasted_iota(jnp.int32, (1, HEAD_PAD), 1)
        for h in range(N_HEADS):
            sl = slice(h * HEAD_PAD, (h + 1) * HEAD_PAD)
            qf_ref[:, sl] = rope(norm(q_ref[:, sl], gqv)).astype(BF16)
            kf_ref[:, sl] = (norm(k_ref[:, sl], gkv) + kr).astype(BF16)
            vb_ref[:, sl] = jnp.where(lane == V_HEAD, 1.0, v_ref[:, sl]).astype(BF16)

    hd = jax.ShapeDtypeStruct((S, HP), BF16)
    return _pallas(body, name="qk_fwd", grid=(S // ts,),
                   in_specs=[_rows(ts, HP), _rows(ts, HP, 0), _rows(ts, HP, 1), _rows(ts, HEAD_PAD, Z_KPE // HEAD_PAD),
                             _rows(ts, HEAD_PAD), _rows(ts, HEAD_PAD), _gain(HEAD_PAD), _gain(HEAD_PAD), _HEAD_MAT, _HEAD_MAT],
                   out_specs=[_rows(ts, HP)] * 3, out_shape=[hd, hd, hd],
                   compiler_params=_cparams("parallel"))(q_raw, kv_raw, kv_raw, z, cs, sn, gq, gk, seg, rot)


def _qk_bwd(dqf, dkf, dv, q_raw, kv_raw, z, cs, sn, gq, gk, seg, rot):
    S = q_raw.shape[0]
    ts = _tile(S, ROW_TILE, SUBLANES)

    def body(dqf_ref, dkf_ref, dv_ref, q_ref, k_ref, kpe_ref, cs_ref, sn_ref, gq_ref, gk_ref, seg_ref, rot_ref,
             dq_ref, dkv_ref, dkpe_ref, dgq_ref, dgk_ref):
        m_n, m_r = _head_masks()
        cos, sin, gqv, gkv = cs_ref[...], sn_ref[...], gq_ref[...], gk_ref[...]
        seg_m, rot_m = seg_ref[...], rot_ref[...]

        def rope_t(w):
            return w * cos - _split_dot(w * sin, rot_m)

        def norm_bwd(x, dy, g):
            r = lax.rsqrt(_split_dot(x * x, seg_m) + EPS)
            xhat = x * r
            dyg = dy * g
            return r * (dyg - xhat * _split_dot(dyg * xhat, seg_m)), dy * xhat

        accq = jnp.zeros((ts, HEAD_PAD), F32)
        acck = jnp.zeros((ts, HEAD_PAD), F32)
        dkr = jnp.zeros((ts, HEAD_PAD), F32)
        for h in range(N_HEADS):
            sl = slice(h * HEAD_PAD, (h + 1) * HEAD_PAD)
            dx, c = norm_bwd(q_ref[:, sl], rope_t(dqf_ref[:, sl]), gqv)
            dq_ref[:, sl] = dx.astype(BF16)
            accq = accq + c
            dk = dkf_ref[:, sl]
            dkr = dkr + jnp.where(m_r, dk, 0.0)
            dx, c = norm_bwd(k_ref[:, sl], jnp.where(m_n, dk, 0.0), gkv)
            dkv_ref[:, sl] = dx.astype(BF16)
            acck = acck + c
            dkv_ref[:, HP + h * HEAD_PAD:HP + (h + 1) * HEAD_PAD] = dv_ref[:, sl].astype(BF16)
        dx, c = norm_bwd(kpe_ref[...], rope_t(dkr), gkv)
        dkpe_ref[...] = dx.astype(BF16)
        _accumulate(dgq_ref, accq)
        _accumulate(dgk_ref, acck + c)

    return _pallas(body, name="qk_bwd", grid=(S // ts,),
                   in_specs=[_rows(ts, HP)] * 4 + [_rows(ts, HP, 0), _rows(ts, HEAD_PAD, Z_KPE // HEAD_PAD),
                                                   _rows(ts, HEAD_PAD), _rows(ts, HEAD_PAD), _gain(HEAD_PAD), _gain(HEAD_PAD),
                                                   _HEAD_MAT, _HEAD_MAT],
                   out_specs=[_rows(ts, HP), _rows(ts, 2 * HP), _rows(ts, HEAD_PAD), _accum(HEAD_PAD), _accum(HEAD_PAD)],
                   out_shape=[jax.ShapeDtypeStruct((S, HP), BF16), jax.ShapeDtypeStruct((S, 2 * HP), BF16),
                              jax.ShapeDtypeStruct((S, HEAD_PAD), BF16), jax.ShapeDtypeStruct((SUBLANES, HEAD_PAD), F32),
                              jax.ShapeDtypeStruct((SUBLANES, HEAD_PAD), F32)],
                   compiler_params=_cparams("arbitrary"))(dqf, dkf, dv, q_raw, kv_raw, z, cs, sn, gq, gk, seg, rot)


def _causal(t):
    row = lax.broadcasted_iota(jnp.int32, (t, t), 0)
    col = lax.broadcasted_iota(jnp.int32, (t, t), 1)
    return col <= row


def _attn_tiles(S, chains):
    t = _tile(S, ATT_TILE, SUBLANES)
    return t, min(chains, S // t)


def _attn_fwd(qf, kf, vb, rider):
    S = qf.shape[0]
    t, nc = _attn_tiles(S, ATT_CHAINS_FWD)
    tq = nc * t
    nqt = S // tq

    def body(q_ref, k_ref, v_ref, *rest):
        (o_ref, lse_ref, m_ref, acc_ref), ride = rider.split(rest, 2, 2)
        qt = pl.program_id(1)
        ride(pl.program_id(0) * nqt + qt, N_HEADS * nqt)
        m_ref[...] = jnp.full_like(m_ref, -jnp.inf)
        acc_ref[...] = jnp.zeros_like(acc_ref)

        def against(rows, first, diagonal):
            kb, vb_ = k_ref[rows, :], v_ref[rows, :]
            subs = [slice(a * t, (a + 1) * t) for a in range(first, nc)]
            s = [lax.dot_general(q_ref[sub, :], kb, _DIMS["nt"], preferred_element_type=F32) * ATT_SCALE_LOG2 for sub in subs]
            if diagonal:
                s[0] = jnp.where(_causal(t), s[0], -jnp.inf)
            m_old = [m_ref[sub, :] for sub in subs]
            m_new = [jnp.maximum(mo, jnp.max(sa, axis=-1, keepdims=True)) for mo, sa in zip(m_old, s)]
            p = [jnp.exp2(sa - jnp.concatenate([mn] * (t // HEAD_PAD), axis=1)).astype(BF16) for sa, mn in zip(s, m_new)]
            for sub, mo, mn, pa in zip(subs, m_old, m_new, p):
                acc_ref[sub, :] = jnp.exp2(mo - mn) * acc_ref[sub, :] + jnp.dot(pa, vb_, preferred_element_type=F32)
                m_ref[sub, :] = mn

        def trip(j, carry):
            against(pl.ds(pl.multiple_of(j * t, t), t), 0, False)
            return carry

        lax.fori_loop(0, nc * qt, trip, 0)
        for d in range(nc):
            against(pl.ds(pl.multiple_of((nc * qt + d) * t, t), t), d, True)
        acc = acc_ref[...]
        l = acc[:, V_HEAD:V_HEAD + 1]
        lane = lax.broadcasted_iota(jnp.int32, (1, HEAD_PAD), 1)
        o_ref[...] = jnp.where(lane < V_HEAD, acc / l, 0.0)
        lse_ref[...] = m_ref[...] + jnp.log(l) * LOG2E

    tile = pl.BlockSpec((tq, HEAD_PAD), lambda h, i: (i, h))
    full = pl.BlockSpec((S, HEAD_PAD), lambda h, i: (0, h))
    outs = _pallas(body, name="attn_fwd", grid=(N_HEADS, nqt), in_specs=[tile, full, full] + rider.in_specs(),
                   out_specs=[tile, pl.BlockSpec((None, tq, HEAD_PAD), lambda h, i: (h, i, 0))] + rider.out_specs(),
                   out_shape=[jax.ShapeDtypeStruct((S, HP), F32), jax.ShapeDtypeStruct((N_HEADS, S, HEAD_PAD), F32)]
                   + rider.out_shape(),
                   scratch_shapes=[pltpu.VMEM((tq, HEAD_PAD), F32), pltpu.VMEM((tq, HEAD_PAD), F32)] + rider.scratch(),
                   input_output_aliases=rider.aliases(3, 2),
                   compiler_params=_cparams("arbitrary", "arbitrary"))(qf, kf, vb, *rider.args())
    return outs[0], outs[1], outs[2:]


def _attn_bwd(qf, kf, vb, o, do, lse, rider):
    S = qf.shape[0]
    t, nc = _attn_tiles(S, ATT_CHAINS_BWD)
    tkv = nc * t
    nq = S // t
    nkt = S // tkv

    def body(q_ref, k_ref, v_ref, o_ref, do_ref, lse_ref, *rest):
        (dq_ref, dk_ref, dv_ref), ride = rider.split(rest, 3, 0)
        kt = pl.program_id(1)
        ride(pl.program_id(0) * nkt + kt, N_HEADS * nkt)

        @pl.when(kt == 0)
        def _():
            dq_ref[...] = jnp.zeros_like(dq_ref)

        dk_ref[...] = jnp.zeros_like(dk_ref)
        dv_ref[...] = jnp.zeros_like(dv_ref)

        def q_block(rows):
            dof = do_ref[rows, :]
            delta = jnp.sum(dof * o_ref[rows, :], axis=-1, keepdims=True)
            return q_ref[rows, :], dof.astype(BF16), lse_ref[rows, :][:, :1], delta

        def against(rows, n_sub, diagonal):
            q, dob, lse, delta = q_block(rows)
            subs = [slice(b * t, (b + 1) * t) for b in range(n_sub)]
            kbs = [k_ref[sub, :] for sub in subs]
            s = [lax.dot_general(q, kb, _DIMS["nt"], preferred_element_type=F32) for kb in kbs]
            dp = [lax.dot_general(dob, v_ref[sub, :], _DIMS["nt"], preferred_element_type=F32) for sub in subs]
            p = [jnp.exp2(sb * ATT_SCALE_LOG2 - lse) for sb in s]
            if diagonal:
                p[-1] = jnp.where(_causal(t), p[-1], 0.0)
            ds = [(pb * (dpb - delta) * ATT_SCALE).astype(BF16) for pb, dpb in zip(p, dp)]
            for sub, pb in zip(subs, p):
                dv_ref[sub, :] += lax.dot_general(pb.astype(BF16), dob, _DIMS["tn"], preferred_element_type=F32)
            for sub, dsb in zip(subs, ds):
                dk_ref[sub, :] += lax.dot_general(dsb, q, _DIMS["tn"], preferred_element_type=F32)
            dq_ref[rows, :] += sum(jnp.dot(dsb, kb, preferred_element_type=F32) for dsb, kb in zip(ds, kbs))

        for a in range(nc):
            against(pl.ds(pl.multiple_of((nc * kt + a) * t, t), t), a + 1, True)

        def trip(i, carry):
            against(pl.ds(pl.multiple_of(i * t, t), t), nc, False)
            return carry

        lax.fori_loop(nc * (kt + 1), nq, trip, 0)

    tile = pl.BlockSpec((tkv, HEAD_PAD), lambda h, j: (j, h))
    full = pl.BlockSpec((S, HEAD_PAD), lambda h, j: (0, h))
    hd = jax.ShapeDtypeStruct((S, HP), F32)
    outs = _pallas(body, name="attn_bwd", grid=(N_HEADS, nkt),
                   in_specs=[full, tile, tile, full, full, pl.BlockSpec((None, S, HEAD_PAD), lambda h, j: (h, 0, 0))]
                   + rider.in_specs(),
                   out_specs=[full, tile, tile] + rider.out_specs(), out_shape=[hd, hd, hd] + rider.out_shape(),
                   scratch_shapes=rider.scratch(), input_output_aliases=rider.aliases(6, 3),
                   compiler_params=_cparams("arbitrary", "arbitrary"))(qf, kf, vb, o, do, lse, *rider.args())
    return outs[0], outs[1], outs[2], outs[3:]


def _shift_down(u, j, row):
    return jnp.where(row >= j, pltpu.roll(u, j, 0), 0.0)


def _shift_up(u, j, row, s):
    return jnp.where(row < s - j, pltpu.roll(u, s - j, 0), 0.0)


def _conv_cols(s, first_tile):
    return pl.BlockSpec((s, LANES), lambda cb: (0, first_tile + cb))


def _conv_fwd(z, cw8):
    S = z.shape[0]

    def body(gb_ref, gc_ref, xin_ref, w_ref, out_ref):
        row = lax.broadcasted_iota(jnp.int32, (S, LANES), 0)
        u = gc_ref[...] * xin_ref[...]
        y = w_ref[0:1, :] * u
        for j in range(1, CONV_TAPS):
            y = y + w_ref[j:j + 1, :] * _shift_down(u, j, row)
        out_ref[...] = gb_ref[...] * y

    return _pallas(body, name="conv_fwd", grid=(CONV_WIDTH // LANES,),
                   in_specs=[_conv_cols(S, Z_GB // LANES), _conv_cols(S, Z_GC // LANES), _conv_cols(S, Z_XIN // LANES),
                             pl.BlockSpec((SUBLANES, LANES), lambda cb: (0, cb))],
                   out_specs=_conv_cols(S, 0), out_shape=jax.ShapeDtypeStruct((S, CONV_WIDTH), F32),
                   compiler_params=_cparams("parallel"))(z, z, z, cw8)


def _conv_bwd(dconv, z, cw8):
    S = z.shape[0]

    def body(d_ref, gb_ref, gc_ref, xin_ref, w_ref, dgb_ref, dgc_ref, dxin_ref, dw_ref):
        row = lax.broadcasted_iota(jnp.int32, (S, LANES), 0)
        gc, xin, d = gc_ref[...], xin_ref[...], d_ref[...]
        u = gc * xin
        dy = d * gb_ref[...]
        y = w_ref[0:1, :] * u
        du = w_ref[0:1, :] * dy
        dw = [jnp.sum(dy * u, axis=0, keepdims=True)]
        for j in range(1, CONV_TAPS):
            uj = _shift_down(u, j, row)
            y = y + w_ref[j:j + 1, :] * uj
            du = du + w_ref[j:j + 1, :] * _shift_up(dy, j, row, S)
            dw.append(jnp.sum(dy * uj, axis=0, keepdims=True))
        dgb_ref[...] = (d * y).astype(BF16)
        dgc_ref[...] = (du * xin).astype(BF16)
        dxin_ref[...] = (du * gc).astype(BF16)
        tap = lax.broadcasted_iota(jnp.int32, (SUBLANES, LANES), 0)
        dw_ref[...] = sum(jnp.where(tap == j, dw[j], 0.0) for j in range(CONV_TAPS))

    col = _conv_cols(S, 0)
    sd = jax.ShapeDtypeStruct((S, CONV_WIDTH), BF16)
    return _pallas(body, name="conv_bwd", grid=(CONV_WIDTH // LANES,),
                   in_specs=[col, _conv_cols(S, Z_GB // LANES), _conv_cols(S, Z_GC // LANES), _conv_cols(S, Z_XIN // LANES),
                             pl.BlockSpec((SUBLANES, LANES), lambda cb: (0, cb))],
                   out_specs=[col, col, col, pl.BlockSpec((SUBLANES, LANES), lambda cb: (0, cb))],
                   out_shape=[sd, sd, sd, jax.ShapeDtypeStruct((SUBLANES, CONV_WIDTH), F32)],
                   compiler_params=_cparams("parallel"))(dconv, z, z, z, cw8)


def _mix_fwd(o, conv, ga, gc):
    S = o.shape[0]
    ts = _tile(S, ROW_TILE, SUBLANES)

    def body(o_ref, c_ref, ga_ref, gc_ref, out_ref):
        ov, cv = o_ref[...], c_ref[...]
        out_ref[:, :HP] = (ov * _rinv(ov, ATTN_WIDTH) * ga_ref[...]).astype(BF16)
        out_ref[:, HP:] = (cv * _rinv(cv, CONV_WIDTH) * gc_ref[...]).astype(BF16)

    return _pallas(body, name="mix_fwd", grid=(S // ts,),
                   in_specs=[_rows(ts, HP), _rows(ts, CONV_WIDTH), _gain(HP), _gain(CONV_WIDTH)],
                   out_specs=_rows(ts, HP + CONV_WIDTH), out_shape=jax.ShapeDtypeStruct((S, HP + CONV_WIDTH), BF16),
                   compiler_params=_cparams("parallel"))(o, conv, ga, gc)


def _mix_bwd(dmixed, o, conv, ga, gc):
    S = o.shape[0]
    ts = _tile(S, ROW_TILE, SUBLANES)

    def body(d_ref, o_ref, c_ref, ga_ref, gc_ref, do_ref, dc_ref, dga_ref, dgc_ref):
        dx, ca = _norm_bwd(o_ref[...], d_ref[:, :HP], ga_ref[...], ATTN_WIDTH)
        do_ref[...] = dx
        dx, cc = _norm_bwd(c_ref[...], d_ref[:, HP:], gc_ref[...], CONV_WIDTH)
        dc_ref[...] = dx
        _accumulate(dga_ref, ca)
        _accumulate(dgc_ref, cc)

    return _pallas(body, name="mix_bwd", grid=(S // ts,),
                   in_specs=[_rows(ts, HP + CONV_WIDTH), _rows(ts, HP), _rows(ts, CONV_WIDTH), _gain(HP), _gain(CONV_WIDTH)],
                   out_specs=[_rows(ts, HP), _rows(ts, CONV_WIDTH), _accum(HP), _accum(CONV_WIDTH)],
                   out_shape=[jax.ShapeDtypeStruct((S, HP), F32), jax.ShapeDtypeStruct((S, CONV_WIDTH), F32),
                              jax.ShapeDtypeStruct((SUBLANES, HP), F32), jax.ShapeDtypeStruct((SUBLANES, CONV_WIDTH), F32)],
                   compiler_params=_cparams("arbitrary"))(dmixed, o, conv, ga, gc)


def _ple_fwd(x, gl, pe, g_next):
    S, D = x.shape
    ts = _tile(S, ROW_TILE, SUBLANES)

    def body(x_ref, gl_ref, pe_ref, g_ref, out_ref, h_ref):
        y = x_ref[...] + jax.nn.sigmoid(gl_ref[...]) * pe_ref[...]
        out_ref[...] = y
        h_ref[...] = (y * _rinv(y, D) * g_ref[...]).astype(BF16)

    return _pallas(body, name="ple_fwd", grid=(S // ts,), in_specs=[_rows(ts, D)] * 3 + [_gain(D)], out_specs=[_rows(ts, D)] * 2,
                   out_shape=[jax.ShapeDtypeStruct((S, D), F32), jax.ShapeDtypeStruct((S, D), BF16)],
                   compiler_params=_cparams("parallel"))(x, gl, pe, g_next)


def _ple_bwd(dx, gl, pe):
    S, D = dx.shape
    ts = _tile(S, ROW_TILE, SUBLANES)

    def body(dx_ref, gl_ref, pe_ref, dpe_ref, dgl_ref):
        d = dx_ref[...]
        gate = jax.nn.sigmoid(gl_ref[...])
        dpe_ref[...] = (d * gate).astype(BF16)
        dgl_ref[...] = (d * pe_ref[...] * (gate * (1.0 - gate))).astype(BF16)

    sd = jax.ShapeDtypeStruct((S, D), BF16)
    return _pallas(body, name="ple_bwd", grid=(S // ts,), in_specs=[_rows(ts, D)] * 3, out_specs=[_rows(ts, D)] * 2,
                   out_shape=[sd, sd], compiler_params=_cparams("parallel"))(dx, gl, pe)


def _loss_grad(y, target):
    S, D = y.shape
    ts = _tile(S, ROW_TILE, SUBLANES)

    def body(y_ref, t_ref, dy_ref, sq_ref):
        e = y_ref[...] - t_ref[...]
        dy_ref[...] = e / D

        @pl.when(pl.program_id(0) == 0)
        def _():
            sq_ref[...] = jnp.zeros_like(sq_ref)

        sq_ref[...] += jnp.broadcast_to(jnp.sum(jnp.sum(e * e, axis=1, keepdims=True), axis=0, keepdims=True), sq_ref.shape)

    return _pallas(body, name="loss_grad", grid=(S // ts,), in_specs=[_rows(ts, D)] * 2,
                   out_specs=[_rows(ts, D), _accum(LANES)],
                   out_shape=[jax.ShapeDtypeStruct((S, D), F32), jax.ShapeDtypeStruct((SUBLANES, LANES), F32)],
                   compiler_params=_cparams("arbitrary"))(y, target)


def _adamw(w, g, m, v, name):
    L, R, C = w.shape
    tr = _tile(R, ROW_TILE, SUBLANES)

    def body(w_ref, g_ref, m_ref, v_ref, d_ref, m2_ref, v2_ref):
        gv = g_ref[...]
        m2 = ADAM_B1 * m_ref[...] + (1.0 - ADAM_B1) * gv
        v2 = ADAM_B2 * v_ref[...] + (1.0 - ADAM_B2) * (gv * gv)
        m_hat = m2 / (1.0 - ADAM_B1 ** ADAM_STEP)
        v_hat = v2 / (1.0 - ADAM_B2 ** ADAM_STEP)
        d_ref[...] = -ADAM_LR * (m_hat / (jnp.sqrt(v_hat) + ADAM_EPS) + ADAM_WD * w_ref[...])
        m2_ref[...] = m2
        v2_ref[...] = v2

    sd = jax.ShapeDtypeStruct((L, R, C), F32)
    spec = pl.BlockSpec((None, tr, C), lambda l, i: (l, i, 0))
    return _pallas(body, name=name, grid=(L, R // tr), in_specs=[spec] * 4, out_specs=[spec] * 3, out_shape=[sd, sd, sd],
                   compiler_params=_cparams("parallel", "parallel"))(w, g, m, v)


def _place():
    return lax.axis_index("x"), lax.axis_index("y"), lax.axis_index("c")


def _other_chips(x, y):
    return [(1 - x, y), (x, 1 - y), (1 - x, 1 - y)]


HBM = pl.BlockSpec(memory_space=pl.ANY)
VMEM_SPEC = pl.BlockSpec(memory_space=pltpu.VMEM)


def _remote_copy(send_sems, recv_sems, k, src, dst, to):
    return pltpu.make_async_remote_copy(src_ref=src, dst_ref=dst, send_sem=send_sems.at[k], recv_sem=recv_sems.at[k],
                                        device_id=to, device_id_type=MESH)


def _comm_call(body, name, arrays, out_shapes, n_remote, in_place=False):
    scratch = [pltpu.SemaphoreType.DMA((n_remote,)), pltpu.SemaphoreType.DMA((n_remote,))]
    aliases = {i: i for i in range(len(arrays))} if in_place else {}
    return _pallas(body, name=name, in_specs=[HBM] * len(arrays), out_specs=[HBM] * len(out_shapes), out_shape=out_shapes,
                   scratch_shapes=scratch, input_output_aliases=aliases,
                   compiler_params=pltpu.CompilerParams(has_side_effects=True))(*arrays)


def _shard_slab(w, chip_arr, name, stack_rows):
    L, R, C = w.shape
    tr = _tile(R, ROW_TILE, 16)
    nb = R // tr

    def body(j_ref, w_ref, out_ref):
        out_ref[...] = w_ref[...].astype(BF16)

    if stack_rows:
        out_shape = jax.ShapeDtypeStruct((L, N_CHIPS * R, C), BF16)
        out_spec = pl.BlockSpec((None, tr, C), lambda l, i, j_ref: (l, j_ref[0] * nb + i, 0))
    else:
        out_shape = jax.ShapeDtypeStruct((N_CHIPS, L, R, C), BF16)
        out_spec = pl.BlockSpec((None, None, tr, C), lambda l, i, j_ref: (j_ref[0], l, i, 0))
    grid_spec = pltpu.PrefetchScalarGridSpec(
        num_scalar_prefetch=1, grid=(L, nb), in_specs=[pl.BlockSpec((None, tr, C), lambda l, i, j_ref: (l, i, 0))],
        out_specs=out_spec)
    return _pallas(body, name=name, grid_spec=grid_spec, out_shape=out_shape,
                   compiler_params=_cparams("parallel", "parallel"))(chip_arr, w)


def _shard_dims(slab):
    return (slab.shape[2], slab.shape[3]) if slab.ndim == 4 else (slab.shape[1] // N_CHIPS, slab.shape[2])


def _shard_rows(ref, chip, l, which):
    ks, _ = _shard_dims(ref)
    h = ks // 2
    return ref.at[chip, l, pl.ds(which * h, h)] if len(ref.shape) == 4 else ref.at[l, pl.ds(chip * ks + which * h, h)]


class Rider:
    def __init__(self, reads, inplace, n_sems, stages):
        self.reads, self.inplace, self.n_sems, self.stages = list(reads), list(inplace), n_sems, stages

    def args(self):
        return self.reads + self.inplace

    def in_specs(self):
        return [HBM] * len(self.args())

    def out_specs(self):
        return [HBM] * len(self.inplace)

    def out_shape(self):
        return [jax.ShapeDtypeStruct(a.shape, a.dtype) for a in self.inplace]

    def scratch(self):
        return [pltpu.SemaphoreType.DMA((self.n_sems,)), pltpu.SemaphoreType.DMA((self.n_sems,))]

    def aliases(self, n_host_in, n_host_out):
        return {n_host_in + len(self.reads) + j: n_host_out + j for j in range(len(self.inplace))}

    def split(self, rest, n_host_out, n_host_scratch):
        n_r, n_io = len(self.reads), len(self.inplace)
        reads = rest[:n_r]
        host_out = rest[n_r + n_io:n_r + n_io + n_host_out]
        outs = rest[n_r + n_io + n_host_out:n_r + 2 * n_io + n_host_out]
        host_scratch = rest[n_r + 2 * n_io + n_host_out:n_r + 2 * n_io + n_host_out + n_host_scratch]
        sems = rest[n_r + 2 * n_io + n_host_out + n_host_scratch:]

        def ride(step, n_steps):
            at = [0, n_steps - 1] if len(self.stages) == 2 else [0, (3 * n_steps) // 4, n_steps - 1]
            for s, stage in zip(at, self.stages):
                pl.when(step == s)(functools.partial(stage, reads, outs, *sems))

        return tuple(host_out) + tuple(host_scratch), ride

    def run(self, name):
        n_r, n_io = len(self.reads), len(self.inplace)

        def body(*refs):
            for stage in self.stages:
                stage(refs[:n_r], refs[n_r + n_io:n_r + 2 * n_io], *refs[n_r + 2 * n_io:])

        return _pallas(body, name=name, in_specs=self.in_specs(), out_specs=self.out_specs(), out_shape=self.out_shape(),
                       scratch_shapes=self.scratch(), input_output_aliases=self.aliases(0, 0),
                       compiler_params=pltpu.CompilerParams(has_side_effects=True))(*self.args())


def _allgather_rider(slabs, items):
    def stage(which, reads, outs, send_sems, recv_sems):
        x, y, c = _place()
        me, sibling = (x, y, c), (x, y, 1 - c)
        copy = functools.partial(_remote_copy, send_sems, recv_sems)
        for n, (j, l) in enumerate(items):
            ref = outs[j]
            own = _shard_rows(ref, 2 * x + y, l, c)
            for k, (cx, cy) in enumerate(_other_chips(x, y)):
                arrived = _shard_rows(ref, 2 * cx + cy, l, c)
                if which == 0:
                    copy(6 * n + k, own, own, (cx, cy, c)).start()
                elif which == 1:
                    copy(6 * n + k, arrived, arrived, me).wait_recv()
                    copy(6 * n + 3 + k, arrived, arrived, sibling).start()
                else:
                    passed = _shard_rows(ref, 2 * cx + cy, l, 1 - c)
                    copy(6 * n + 3 + k, passed, passed, me).wait_recv()
                    copy(6 * n + k, own, own, me).wait_send()
                    copy(6 * n + 3 + k, arrived, arrived, me).wait_send()

    return Rider([], slabs, 6 * len(items), [functools.partial(stage, w) for w in range(3)])


def _exchange_rider(grads, landing, items):
    def stage(start, reads, outs, send_sems, recv_sems):
        x, y, c = _place()
        for n, (i, j, l) in enumerate(items):
            g_ref, r_ref = reads[i], outs[j]
            h = g_ref.shape[1] // 2
            for k in range(1, N_DEV):
                px, py, pc = (1 - x if k & 4 else x, 1 - y if k & 2 else y, 1 - c if k & 1 else c)
                cp = _remote_copy(send_sems, recv_sems, 7 * n + k - 1, g_ref.at[2 * px + py, pl.ds(pc * h, h)],
                                  r_ref.at[l, k - 1], (px, py, pc))
                if start:
                    cp.start()
                else:
                    cp.wait()

    return Rider(grads, landing, 7 * len(items), [functools.partial(stage, True), functools.partial(stage, False)])


def _reduce_partials(g, r, chip_arr, c_arr, name):
    L, n, R, C = g.shape
    H = R // 2
    th = _tile(H, ROW_TILE, 16)
    nb = H // th

    def body(j_ref, c_ref, g_ref, r_ref, out_ref):
        s = g_ref[...].astype(F32)
        for k in range(N_DEV - 1):
            s = s + r_ref[k].astype(F32)
        out_ref[...] = s

    grid_spec = pltpu.PrefetchScalarGridSpec(
        num_scalar_prefetch=2, grid=(L, nb),
        in_specs=[pl.BlockSpec((None, None, th, C), lambda l, i, j_ref, c_ref: (l, j_ref[0], c_ref[0] * nb + i, 0)),
                  pl.BlockSpec((None, N_DEV - 1, th, C), lambda l, i, j_ref, c_ref: (l, 0, i, 0))],
        out_specs=pl.BlockSpec((None, th, C), lambda l, i, j_ref, c_ref: (l, c_ref[0] * nb + i, 0)))
    return _pallas(body, name=name, grid_spec=grid_spec, out_shape=jax.ShapeDtypeStruct((L, R, C), F32),
                   compiler_params=_cparams("parallel", "parallel"))(chip_arr, c_arr, g, r)


def _join_halves(fs):
    n = len(fs)

    def body(*refs):
        out_refs, (send_sems, recv_sems) = refs[n:2 * n], refs[2 * n:]
        x, y, c = _place()
        cps = []
        for i, out_ref in enumerate(out_refs):
            h = out_ref.shape[1] // 2
            mine = out_ref.at[:, pl.ds(c * h, h)]
            cps.append(_remote_copy(send_sems, recv_sems, i, mine, mine, (x, y, 1 - c)))
            cps[-1].start()
        for cp in cps:
            cp.wait()

    return _comm_call(body, "grad_join_halves", fs, [jax.ShapeDtypeStruct(f.shape, f.dtype) for f in fs], n, in_place=True)


def _allgather_small(v, name):
    R, C = v.shape

    def body(v_ref, out_ref, send_sems, recv_sems):
        x, y, c = _place()
        me = 4 * x + 2 * y + c
        out_ref[me] = v_ref[...]
        cps = []
        for k in range(1, N_DEV):
            peer = (1 - x if k & 4 else x, 1 - y if k & 2 else y, 1 - c if k & 1 else c)
            cps.append(pltpu.make_async_remote_copy(src_ref=v_ref, dst_ref=out_ref.at[me], send_sem=send_sems.at[k - 1],
                                                    recv_sem=recv_sems.at[k - 1], device_id=peer, device_id_type=MESH))
        for cp in cps:
            cp.start()
        for cp in cps:
            cp.wait()

    return _pallas(body, name=name, in_specs=[VMEM_SPEC], out_specs=VMEM_SPEC,
                   out_shape=jax.ShapeDtypeStruct((N_DEV, R, C), v.dtype),
                   scratch_shapes=[pltpu.SemaphoreType.DMA((N_DEV - 1,)), pltpu.SemaphoreType.DMA((N_DEV - 1,))],
                   compiler_params=pltpu.CompilerParams(has_side_effects=True))(v)


def _sum_devices(g):
    n, R, C = g.shape

    def body(g_ref, out_ref):
        s = g_ref[0]
        for d in range(1, n):
            s = s + g_ref[d]
        out_ref[...] = s

    return _pallas(body, name="sum_devices", in_specs=[VMEM_SPEC], out_specs=VMEM_SPEC,
                   out_shape=jax.ShapeDtypeStruct((R, C), g.dtype))(g)


def _pad_heads(a, width):
    lead = a.shape[:-1]
    a = a.reshape(lead + (N_HEADS, width))
    a = jnp.pad(a, [(0, 0)] * len(lead) + [(0, 0), (0, HEAD_PAD - width)])
    return a.reshape(lead + (HP,))


def _unpad_heads(a, width):
    lead = a.shape[:-1]
    return a.reshape(lead + (N_HEADS, HEAD_PAD))[..., :width].reshape(lead + (N_HEADS * width,))


def _pre_weights(slabs, l):
    cols = lambda name: slabs[name][:, l].transpose(1, 0, 2).reshape(slabs[name].shape[2], -1)
    w_in = cols("w_in")
    kpe = jnp.pad(w_in[:, 640:672], ((0, 0), (ROPE_LO, HEAD_PAD - ROPE_HI)))
    w_ukv = cols("w_ukv").reshape(KV_LORA, N_HEADS, QK_NOPE + V_HEAD)
    return dict(
        w_in=jnp.concatenate([w_in[:, :640], kpe, w_in[:, 672:]], axis=1),
        w_uq=_pad_heads(cols("w_uq"), QK_HEAD),
        w_ukv=jnp.concatenate([_pad_heads(w_ukv[..., :QK_NOPE].reshape(KV_LORA, -1), QK_NOPE),
                               _pad_heads(w_ukv[..., QK_NOPE:].reshape(KV_LORA, -1), V_HEAD)], axis=1),
    )


def _post_weights(slabs, l):
    cols = lambda name: slabs[name][:, l].transpose(1, 0, 2).reshape(slabs[name].shape[2], -1)
    w_o = slabs["w_o"][l]
    return dict(
        w_o=jnp.concatenate([_pad_heads(w_o[:ATTN_WIDTH].T, V_HEAD).T, w_o[ATTN_WIDTH:]], axis=0),
        w_up=cols("w_up"), w_down=(slabs["w_down"], l), w_ple_gate=(slabs["w_ple_gate"], l), w_ple=cols("w_ple"),
    )


def _gains(small, l):
    row = lambda name: small[name][l].reshape(1, -1)
    headrow = lambda a, b: jnp.pad(jnp.concatenate([small[a][l], small[b][l]]), (0, HEAD_PAD - QK_HEAD)).reshape(1, HEAD_PAD)
    return dict(
        g_mix=row("g_mix"), g_q_lat=row("g_q_lat"), g_kv_lat=row("g_kv_lat"), g_mlp=row("g_mlp"), g_ple=row("g_ple"),
        g_out_conv=row("g_out_conv"), g_out_attn=_pad_heads(small["g_out_attn"][l], V_HEAD).reshape(1, HP),
        gq=headrow("g_qn_nope", "g_qn_rope"), gk=headrow("g_kn_nope", "g_kn_rope"),
        cw8=jnp.pad(small["conv_w"][l], ((0, SUBLANES - CONV_TAPS), (0, 0))),
        head_mats=_head_matrices(),
    )


def _w_o_shards(dw_o):
    return jnp.concatenate([_unpad_heads(dw_o[:HP].T, V_HEAD).T, dw_o[HP:]], axis=0).reshape(N_CHIPS, -1, D_MODEL)


def _unpad_grads(gp):
    dw_in = gp["w_in"]
    dw_ukv = gp["w_ukv"]
    k_part = dw_ukv[:, :HP].reshape(KV_LORA, N_HEADS, HEAD_PAD)[..., :QK_NOPE]
    v_part = dw_ukv[:, HP:].reshape(KV_LORA, N_HEADS, HEAD_PAD)[..., :V_HEAD]
    first = lambda name: gp[name][0]
    col_shards = lambda a: a.reshape(a.shape[0], N_CHIPS, -1).transpose(1, 0, 2)
    return dict(
        w_in=col_shards(jnp.concatenate([dw_in[:, :640], dw_in[:, Z_KPE + ROPE_LO:Z_KPE + ROPE_HI], dw_in[:, Z_GB:]], axis=1)),
        w_uq=col_shards(_unpad_heads(gp["w_uq"], QK_HEAD)),
        w_ukv=col_shards(jnp.concatenate([k_part, v_part], axis=-1).reshape(KV_LORA, -1)),
        w_o=gp["w_o"], w_up=gp["w_up"], w_down=gp["w_down"], w_ple_gate=gp["w_ple_gate"], w_ple=gp["w_ple"],
        g_mix=first("g_mix"), g_q_lat=first("g_q_lat"), g_kv_lat=first("g_kv_lat"), g_mlp=first("g_mlp"),
        g_ple=first("g_ple"), g_out_conv=first("g_out_conv"), g_out_attn=_unpad_heads(first("g_out_attn"), V_HEAD),
        g_qn_nope=gp["gq"][0, :QK_NOPE], g_qn_rope=gp["gq"][0, QK_NOPE:QK_HEAD],
        g_kn_nope=gp["gk"][0, :QK_NOPE], g_kn_rope=gp["gk"][0, QK_NOPE:QK_HEAD],
        conv_w=gp["cw8"][:CONV_TAPS],
    )


def _rope_tables(positions):
    inv_freq = 1.0 / (ROPE_THETA ** (jnp.arange(0, QK_ROPE, 2, dtype=F32) / QK_ROPE))
    ang = positions.astype(F32)[:, None] * inv_freq
    cos, sin = jnp.cos(ang), jnp.sin(ang)
    pad = lambda t, v: jnp.pad(jnp.concatenate([t, t], axis=1), ((0, 0), (ROPE_LO, HEAD_PAD - ROPE_HI)), constant_values=v)
    return pad(cos, 1.0), pad(sin, 0.0)


def _layer_fwd(x0, h, p_l, W, cs, sn, attend, g_next):
    z = _mm(h, W["w_in"], name="mm_in")
    qln, kvln = _lat_fwd(z, W["g_q_lat"], W["g_kv_lat"])
    q_raw = _mm(qln, W["w_uq"], name="mm_uq")
    kv_raw = _mm(kvln, W["w_ukv"], name="mm_ukv")
    qf, kf, vb = _qk_fwd(q_raw, kv_raw, z, cs, sn, W["gq"], W["gk"], *W["head_mats"])
    o, lse = attend(qf, kf, vb)
    conv = _conv_fwd(z, W["cw8"])
    mixed = _mix_fwd(o, conv, W["g_out_attn"], W["g_out_conv"])
    x1, h2 = _mm(mixed, W["w_o"], res=x0, gain=W["g_mlp"], epi="rms", name="mm_o")
    a, f = _mm(h2, W["w_up"], epi="relu2", out_dtype=BF16, name="mm_up")
    x2, h3 = _mm(f, W["w_down"], res=x1, gain=W["g_ple"], epi="rms", name="mm_down")
    gl = _mm(h3, W["w_ple_gate"], name="mm_ple_gate")
    pe = _mm(p_l, W["w_ple"], name="mm_ple")
    x3, h_next = _ple_fwd(x2, gl, pe, g_next)
    saved = dict(x0=x0, h=h, z=z, qln=qln, kvln=kvln, q_raw=q_raw, kv_raw=kv_raw, qf=qf, kf=kf, vb=vb, o=o, lse=lse,
                 conv=conv, mixed=mixed, x1=x1, h2=h2, a=a, f=f, x2=x2, h3=h3, gl=gl, pe=pe)
    return x3, h_next, saved


def _layer_bwd(dx3, p_l, W, cs, sn, sv, attend_bwd):
    g = {}
    dpe, dgl = _ple_bwd(dx3, sv["gl"], sv["pe"])
    g["w_ple"] = _mm(p_l, dpe, mode="tn", out_dtype=BF16, shard_out=1, name="mm_dw_ple")
    g["w_ple_gate"] = _mm(sv["h3"], dgl, mode="tn", out_dtype=BF16, shard_out=0, name="mm_dw_ple_gate")
    dx2, dx2b, g["g_ple"] = _mm(dgl, W["w_ple_gate"], mode="nt", res=dx3, aux=sv["x2"], gain=W["g_ple"], epi="rms_bwd",
                                name="mm_dh3")
    da = _mm(dx2b, W["w_down"], mode="nt", aux=sv["a"], epi="drelu2", out_dtype=BF16, name="mm_da")
    g["w_down"] = _mm(sv["f"], dx2b, mode="tn", out_dtype=BF16, shard_out=0, name="mm_dw_down")
    g["w_up"] = _mm(sv["h2"], da, mode="tn", out_dtype=BF16, shard_out=1, name="mm_dw_up")
    dx1, dx1b, g["g_mlp"] = _mm(da, W["w_up"], mode="nt", res=dx2, aux=sv["x1"], gain=W["g_mlp"], epi="rms_bwd",
                                name="mm_dh2")
    dmixed = _mm(dx1b, W["w_o"], mode="nt", name="mm_dmixed")
    g["w_o"] = _mm(sv["mixed"], dx1b, mode="tn", out_dtype=BF16, name="mm_dw_o")
    do, dconv, g["g_out_attn"], g["g_out_conv"] = _mix_bwd(dmixed, sv["o"], sv["conv"], W["g_out_attn"], W["g_out_conv"])
    dgb, dgc, dxin, g["cw8"] = _conv_bwd(dconv, sv["z"], W["cw8"])
    dqf, dkf, dv = attend_bwd(g, sv["qf"], sv["kf"], sv["vb"], sv["o"], do, sv["lse"])
    dq_raw, dkv_raw, dkpe, g["gq"], g["gk"] = _qk_bwd(dqf, dkf, dv, sv["q_raw"], sv["kv_raw"], sv["z"], cs, sn, W["gq"], W["gk"],
                                                      *W["head_mats"])
    g["w_uq"] = _mm(sv["qln"], dq_raw, mode="tn", out_dtype=BF16, name="mm_dw_uq")
    dqln = _mm(dq_raw, W["w_uq"], mode="nt", name="mm_dqln")
    g["w_ukv"] = _mm(sv["kvln"], dkv_raw, mode="tn", out_dtype=BF16, name="mm_dw_ukv")
    dkvln = _mm(dkv_raw, W["w_ukv"], mode="nt", name="mm_dkvln")
    dlat, g["g_q_lat"], g["g_kv_lat"] = _lat_bwd(dqln, dkvln, sv["z"], W["g_q_lat"], W["g_kv_lat"])
    dz = jnp.concatenate([dlat, dkpe, dgb, dgc, dxin], axis=1)
    g["w_in"] = _mm(sv["h"], dz, mode="tn", out_dtype=BF16, name="mm_dw_in")
    dx0, _, g["g_mix"] = _mm(dz, W["w_in"], mode="nt", res=dx1, aux=sv["x0"], gain=W["g_mix"], epi="rms_bwd", name="mm_dh")
    return dx0, g


def _local_step(x, p, positions, target, slabs, small):
    depth = p.shape[0]
    cs, sn = _rope_tables(positions)
    slabs = dict(slabs)

    def gather(items, host):
        touched = [n for n in PRE + POST if any(n == name for name, _ in items)]
        rider = _allgather_rider([slabs[n] for n in touched], [(touched.index(name), l) for name, l in items])
        if host is None:
            out, new = (), rider.run("allgather_first")
        else:
            *out, new = host(rider)
        slabs.update(zip(touched, new))
        return out

    gather([(name, 0) for name in PRE], None)
    Ws, saved = [], []
    gains = [_gains(small, l) for l in range(depth)]
    h = _rms_fwd(x, gains[0]["g_mix"], "rms_mix")
    for l in range(depth):
        W = dict(gains[l], **_pre_weights(slabs, l))

        def attend(qf, kf, vb, W=W, l=l):
            items = [(name, l) for name in POST] + ([(name, l + 1) for name in PRE] if l + 1 < depth else [])
            o, lse = gather(items, functools.partial(_attn_fwd, qf, kf, vb))
            W.update(_post_weights(slabs, l))
            return o, lse

        g_next = gains[l + 1]["g_mix"] if l + 1 < depth else jnp.ones_like(gains[l]["g_mix"])
        x, h, sv = _layer_fwd(x, h, p[l], W, cs, sn, attend, g_next)
        Ws.append(W)
        saved.append(sv)
    dx, sq = _loss_grad(x, target)

    landing = {name: lax.empty((depth, N_DEV - 1, _shard_dims(slabs[name])[0] // 2, _shard_dims(slabs[name])[1]), BF16)
               for name in PRE + POST}

    def exchange(sends, host):
        touched = [n for n in PRE + POST if any(n == name for name, _, _ in sends)]
        rider = _exchange_rider([g for _, _, g in sends], [landing[n] for n in touched],
                                [(i, touched.index(name), l) for i, (name, l, _) in enumerate(sends)])
        if host is None:
            out, new = (), rider.run("grad_exchange_last")
        else:
            *out, new = host(rider)
        landing.update(zip(touched, new))
        return out

    grads = [None] * depth
    for l in reversed(range(depth)):
        W = dict(Ws[l], w_down=(slabs["w_down"], l), w_ple_gate=(slabs["w_ple_gate"], l))

        def attend_bwd(g, qf, kf, vb, o, do, lse, l=l):
            g["w_o"] = _w_o_shards(g["w_o"])
            sends = [(name, l, g[name]) for name in POST]
            if l + 1 < depth:
                sends += [(name, l + 1, grads[l + 1][name]) for name in PRE]
            return exchange(sends, functools.partial(_attn_bwd, qf, kf, vb, o, do, lse))

        dx, gp = _layer_bwd(dx, p[l], W, cs, sn, saved[l], attend_bwd)
        grads[l] = _unpad_grads(gp)
    exchange([(name, 0, grads[0][name]) for name in PRE], None)
    return sq, dx, grads, landing


def _pack_rows(vals, entries):
    depth = vals[entries[0][0]].shape[0]
    return jnp.concatenate([jnp.pad(vals[name].reshape(depth, -1), ((0, 0), (0, _pad128(n) - n))) for name, n in entries], axis=1)


def _unpack_rows(packed, entries):
    out, off = {}, 0
    for name, n in entries:
        out[name] = packed[:, off:off + n]
        off += _pad128(n)
    return out


def kernel(x, p, positions, g_mix, w_in, g_q_lat, w_uq, g_kv_lat, w_ukv, g_qn_nope, g_qn_rope, g_kn_nope, g_kn_rope, conv_w, g_out_attn, g_out_conv, w_o, g_mlp, w_up, w_down, g_ple, w_ple_gate, w_ple, loss_target, m_g_mix, m_w_in, m_g_q_lat, m_w_uq, m_g_kv_lat, m_w_ukv, m_g_qn_nope, m_g_qn_rope, m_g_kn_nope, m_g_kn_rope, m_conv_w, m_g_out_attn, m_g_out_conv, m_w_o, m_g_mlp, m_w_up, m_w_down, m_g_ple, m_w_ple_gate, m_w_ple, v_g_mix, v_w_in, v_g_q_lat, v_w_uq, v_g_kv_lat, v_w_ukv, v_g_qn_nope, v_g_qn_rope, v_g_kn_nope, v_g_kn_rope, v_conv_w, v_g_out_attn, v_g_out_conv, v_w_o, v_g_mlp, v_w_up, v_w_down, v_g_ple, v_w_ple_gate, v_w_ple):
    w = dict(g_mix=g_mix, w_in=w_in, g_q_lat=g_q_lat, w_uq=w_uq, g_kv_lat=g_kv_lat, w_ukv=w_ukv, g_qn_nope=g_qn_nope,
             g_qn_rope=g_qn_rope, g_kn_nope=g_kn_nope, g_kn_rope=g_kn_rope, conv_w=conv_w, g_out_attn=g_out_attn,
             g_out_conv=g_out_conv, w_o=w_o, g_mlp=g_mlp, w_up=w_up, w_down=w_down, g_ple=g_ple, w_ple_gate=w_ple_gate,
             w_ple=w_ple)
    m = dict(g_mix=m_g_mix, w_in=m_w_in, g_q_lat=m_g_q_lat, w_uq=m_w_uq, g_kv_lat=m_g_kv_lat, w_ukv=m_w_ukv,
             g_qn_nope=m_g_qn_nope, g_qn_rope=m_g_qn_rope, g_kn_nope=m_g_kn_nope, g_kn_rope=m_g_kn_rope, conv_w=m_conv_w,
             g_out_attn=m_g_out_attn, g_out_conv=m_g_out_conv, w_o=m_w_o, g_mlp=m_g_mlp, w_up=m_w_up, w_down=m_w_down,
             g_ple=m_g_ple, w_ple_gate=m_w_ple_gate, w_ple=m_w_ple)
    v = dict(g_mix=v_g_mix, w_in=v_w_in, g_q_lat=v_g_q_lat, w_uq=v_w_uq, g_kv_lat=v_g_kv_lat, w_ukv=v_w_ukv,
             g_qn_nope=v_g_qn_nope, g_qn_rope=v_g_qn_rope, g_kn_nope=v_g_kn_nope, g_kn_rope=v_g_kn_rope, conv_w=v_conv_w,
             g_out_attn=v_g_out_attn, g_out_conv=v_g_out_conv, w_o=v_w_o, g_mlp=v_g_mlp, w_up=v_w_up, w_down=v_w_down,
             g_ple=v_g_ple, w_ple_gate=v_w_ple_gate, w_ple=v_w_ple)
    depth = p.shape[0]
    ax, ay, ac = _place()
    chip = 2 * ax + ay
    c_arr = jnp.reshape(ac, (1,)).astype(jnp.int32)
    chip_arr = jnp.reshape(chip, (1,)).astype(jnp.int32)
    conv_shard =("conv_w", CONV_TAPS * CONV_WIDTH // N_CHIPS)
    conv_full = ("conv_w", CONV_TAPS * CONV_WIDTH)

    names = PRE + POST
    slabs = {name: _shard_slab(w[name], chip_arr, "shard_slab_" + name, name in ROW_SHARDED) for name in names}
    conv_rows = -(-depth * CONV_TAPS // SUBLANES) * SUBLANES
    conv_all = _allgather_small(jnp.pad(conv_w.reshape(depth * CONV_TAPS, LANES), ((0, conv_rows - depth * CONV_TAPS), (0, 0))),
                                "allgather_conv_w")
    conv_cat = jnp.concatenate([conv_all[2 * j, :depth * CONV_TAPS] for j in range(N_CHIPS)], axis=1)
    small = {name: w[name] for name, _ in SMALL}
    small["conv_w"] = conv_cat.reshape(depth, CONV_TAPS, CONV_WIDTH)

    sq, grad_x, grads, landing = _local_step(x[0], p[:, 0], positions[0], loss_target[0], slabs, small)

    halves = [_reduce_partials(jnp.stack([grads[l][name] for l in range(depth)]), landing[name], chip_arr, c_arr,
                               "grad_reduce_" + name) for name in names]
    g_out = dict(zip(names, _join_halves(halves)))

    reduced = SMALL + (conv_full, ("loss", LANES))
    stacked = {name: jnp.stack([grads[l][name] for l in range(depth)]) for name, _ in SMALL + (conv_full,)}
    stacked["loss"] = jnp.broadcast_to(sq[:1] * (0.5 / D_MODEL), (depth, LANES))
    g_small = _unpack_rows(_sum_devices(_allgather_small(_pack_rows(stacked, reduced), "allgather_small_grads")), reduced)
    loss = g_small["loss"][0, 0]
    for name, n in SMALL:
        g_out[name] = g_small[name]
    g_conv = g_small["conv_w"].reshape(depth, CONV_TAPS, CONV_WIDTH)
    g_out["conv_w"] = lax.dynamic_slice_in_dim(g_conv, chip * LANES, LANES, axis=2)

    delta, new_m, new_v = {}, {}, {}
    for name in names:
        view = (lambda t: jnp.swapaxes(t, 1, 2)) if w[name].shape[2] % LANES else (lambda t: t)
        d, m2, v2 = _adamw(view(w[name]), view(g_out[name]), view(m[name]), view(v[name]), "adamw_" + name)
        delta[name], new_m[name], new_v[name] = view(d), view(m2), view(v2)
    local = SMALL + (conv_shard,)
    pack = lambda vals: _pack_rows(vals, local)[None]
    d, m2, v2 = _adamw(pack(w), pack(g_out), pack(m), pack(v), "adamw_small")
    for res, packed_res in ((delta, d), (new_m, m2), (new_v, v2)):
        un = _unpack_rows(packed_res[0], local)
        for name, _ in local:
            res[name] = un[name].reshape(w[name].shape)

    return (loss, grad_x[None], *[g_out[n] for n in WEIGHT_ORDER], *[delta[n] for n in WEIGHT_ORDER],
            *[new_m[n] for n in WEIGHT_ORDER], *[new_v[n] for n in WEIGHT_ORDER])
```

```python
import functools

import jax
import jax.numpy as jnp
from jax import lax
from jax.experimental import pallas as pl
from jax.experimental.pallas import tpu as pltpu

F32 = jnp.float32
BF16 = jnp.bfloat16
MESH = pl.DeviceIdType.MESH

D_MODEL = 1024
N_HEADS = 8
QK_NOPE = 64
QK_ROPE = 32
QK_HEAD = QK_NOPE + QK_ROPE
V_HEAD = 64
Q_LORA = 384
KV_LORA = 256
ATTN_WIDTH = N_HEADS * V_HEAD
CONV_WIDTH = 512
CONV_TAPS = 3
D_FF = 4096
PLE_DIM = 256
ROPE_THETA = 10000.0
EPS = 1e-6
ATT_SCALE = QK_HEAD ** -0.5
LOG2E = 1.4426950408889634
ATT_SCALE_LOG2 = ATT_SCALE * LOG2E

ADAM_LR = 0.001
ADAM_B1 = 0.9
ADAM_B2 = 0.999
ADAM_EPS = 1e-08
ADAM_WD = 0.01
ADAM_STEP = 10

LANES = 128
SUBLANES = 8
HEAD_PAD = LANES
HP = N_HEADS * HEAD_PAD
ROPE_LO = QK_NOPE
ROPE_MID = QK_NOPE + QK_ROPE // 2
ROPE_HI = QK_NOPE + QK_ROPE
VMEM_LIMIT = 56 * 1024 * 1024

Z_Q, Z_KV, Z_KPE, Z_GB, Z_GC, Z_XIN = 0, 384, 640, 768, 1280, 1792
Z_COLS = 2304
Z_LAT = Z_KPE

ROW_TILE = 512
ATT_TILE = 256
ATT_CHAINS_FWD, ATT_CHAINS_BWD = 8, 4
MM_TM, MM_TN, MM_TK = 1024, 1024, 4096
MM_TM_MIN, MM_TK_MIN = 256, 512
MM_VMEM_BUDGET = 40 * 1024 * 1024

N_CHIPS = 4
N_DEV = 8

PRE = ("w_in", "w_uq", "w_ukv")
POST = ("w_o", "w_up", "w_down", "w_ple_gate", "w_ple")
ROW_SHARDED = ("w_o", "w_down", "w_ple_gate")
SMALL = (
    ("g_mix", 1024), ("g_q_lat", 384), ("g_kv_lat", 256), ("g_qn_nope", 64), ("g_qn_rope", 32), ("g_kn_nope", 64),
    ("g_kn_rope", 32), ("g_out_attn", 512), ("g_out_conv", 512), ("g_mlp", 1024), ("g_ple", 1024),
)
WEIGHT_ORDER = ("g_mix", "w_in", "g_q_lat", "w_uq", "g_kv_lat", "w_ukv", "g_qn_nope", "g_qn_rope", "g_kn_nope",
                "g_kn_rope", "conv_w", "g_out_attn", "g_out_conv", "w_o", "g_mlp", "w_up", "w_down", "g_ple",
                "w_ple_gate", "w_ple")


def _pallas(body, **kw):
    return pl.pallas_call(body, **kw)


def _cparams(*sem):
    return pltpu.CompilerParams(dimension_semantics=sem, vmem_limit_bytes=VMEM_LIMIT)


def _tile(dim, pref, unit=LANES):
    if dim <= pref:
        return dim
    t = (pref // unit) * unit
    while t > unit and dim % t:
        t -= unit
    assert dim % t == 0, (dim, pref)
    return t


def _pad128(n):
    return -(-n // LANES) * LANES


_DIMS = {"nn": (((1,), (0,)), ((), ())), "nt": (((1,), (1,)), ((), ())), "tn": (((0,), (0,)), ((), ()))}


def _mm(a, b, *, mode="nn", res=None, aux=None, gain=None, epi=None, out_dtype=F32, shard_out=None, rider=None, name):
    whole_rows = epi in ("rms", "rms_bwd")
    b, layer = b if isinstance(b, tuple) else (b, None)
    b_shape = b.shape if layer is None else b.shape[1:]
    if mode == "nn":
        (M, K), (K2, N) = a.shape, b_shape
    elif mode == "nt":
        (M, K), (N, K2) = a.shape, b_shape
    else:
        (K, M), (K2, N) = a.shape, b_shape
    assert K == K2, (a.shape, b_shape, mode)
    n_lim = k_lim = None
    m_lim = M // N_CHIPS if shard_out == 0 else None
    if shard_out == 1:
        n_lim = N // N_CHIPS
    extra = [t for t in (res, aux) if t is not None]
    out_bytes = 6 if epi in ("relu2", "rms", "rms_bwd") else jnp.dtype(out_dtype).itemsize
    if whole_rows:
        assert (n_lim or N) == N and N <= MM_TN and shard_out is None and gain is not None
        m_lim = min(M, MM_TM // 2)

    def footprint(tm, tn, tk):
        blocks = tm * tk * a.dtype.itemsize + tk * tn * 2 + tm * tn * (out_bytes + 4 * len(extra))
        return 2 * blocks + (tm * tn * 4 if tk < K else 0)

    tm, tn, tk = _tile(m_lim or M, MM_TM), _tile(n_lim or N, MM_TN), _tile(k_lim or K, MM_TK)
    while footprint(tm, tn, tk) > MM_VMEM_BUDGET and tk > MM_TK_MIN:
        tk = _tile(K, tk // 2)
    while footprint(tm, tn, tk) > MM_VMEM_BUDGET and tm > MM_TM_MIN:
        tm = _tile(M, tm // 2)
    nk = K // tk
    a_spec = pl.BlockSpec((tk, tm), lambda i, j, k: (k, i)) if mode == "tn" else pl.BlockSpec((tm, tk), lambda i, j, k: (i, k))
    if mode == "nt":
        b_block, b_rc = (tn, tk), (lambda i, j, k: (j, k))
    else:
        b_block, b_rc = (tk, tn), (lambda i, j, k: (k, j))
    if layer is None:
        b_spec = pl.BlockSpec(b_block, b_rc)
    else:
        b_spec = pl.BlockSpec((None,) + b_block, lambda i, j, k: (layer,) + b_rc(i, j, k))
    mn_spec = pl.BlockSpec((tm, tn), lambda i, j, k: (i, j))
    dims = _DIMS[mode]
    n_out = {"relu2": 2, "rms": 2, "rms_bwd": 3}.get(epi, 1)

    n_in = 2 + len(extra) + whole_rows
    n_grid = (M // tm, N // tn, nk)

    def body(*refs):
        a_ref, b_ref = refs[0], refs[1]
        extra_refs = refs[2:2 + len(extra)]
        gain_ref = refs[2 + len(extra)] if whole_rows else None
        if rider is None:
            tail = refs[n_in:]
        else:
            tail, ride = rider.split(refs[n_in:], n_out, int(nk > 1))
            ride((pl.program_id(0) * n_grid[1] + pl.program_id(1)) * nk + pl.program_id(2), n_grid[0] * n_grid[1] * nk)
        out_refs = tail[:n_out]
        prod = lax.dot_general(a_ref[...].astype(BF16), b_ref[...].astype(BF16), dims, preferred_element_type=F32)

        def finish(r):
            if epi == "rms_bwd":
                dx, dgc = _norm_bwd(extra_refs[1][...], r, gain_ref[...], N)
                dx = dx + extra_refs[0][...]
                out_refs[0][...] = dx
                out_refs[1][...] = dx.astype(BF16)
                _accumulate(out_refs[2], dgc)
                return
            if res is not None:
                r = r + extra_refs[0][...]
            if epi == "rms":
                out_refs[0][...] = r
                out_refs[1][...] = (r * _rinv(r, N) * gain_ref[...]).astype(BF16)
            elif epi == "relu2":
                out_refs[0][...] = r.astype(out_dtype)
                t = jnp.maximum(r, 0.0)
                out_refs[1][...] = (t * t).astype(BF16)
            elif epi == "drelu2":
                out_refs[0][...] = (r * (2.0 * jnp.maximum(extra_refs[-1][...].astype(F32), 0.0))).astype(out_dtype)
            else:
                out_refs[0][...] = r.astype(out_dtype)

        if nk == 1:
            finish(prod)
        else:
            acc = tail[n_out]
            k = pl.program_id(2)

            @pl.when(k == 0)
            def _():
                acc[...] = prod

            @pl.when(k > 0)
            def _():
                acc[...] += prod

            @pl.when(k == nk - 1)
            def _():
                finish(acc[...])

    if epi in ("relu2", "rms", "rms_bwd"):
        out_shape = [jax.ShapeDtypeStruct((M, N), out_dtype if epi == "relu2" else F32), jax.ShapeDtypeStruct((M, N), BF16)]
        out_specs = [mn_spec, mn_spec]
        if epi == "rms_bwd":
            out_shape.append(jax.ShapeDtypeStruct((SUBLANES, N), F32))
            out_specs.append(pl.BlockSpec((SUBLANES, tn), lambda i, j, k: (0, j)))
    elif shard_out == 0:
        per = (M // N_CHIPS) // tm
        out_shape = jax.ShapeDtypeStruct((N_CHIPS, M // N_CHIPS, N), out_dtype)
        out_specs = pl.BlockSpec((None, tm, tn), lambda i, j, k: (i // per, i % per, j))
    elif shard_out == 1:
        per = (N // N_CHIPS) // tn
        out_shape = jax.ShapeDtypeStruct((N_CHIPS, M, N // N_CHIPS), out_dtype)
        out_specs = pl.BlockSpec((None, tm, tn), lambda i, j, k: (j // per, i, j % per))
    else:
        out_shape = jax.ShapeDtypeStruct((M, N), out_dtype)
        out_specs = mn_spec
    gains = [gain] if whole_rows else []
    in_specs = [a_spec, b_spec] + [mn_spec] * len(extra) + [pl.BlockSpec((1, tn), lambda i, j, k: (0, j))] * len(gains)
    scratch = [pltpu.VMEM((tm, tn), F32)] if nk > 1 else []
    ordered = epi == "rms_bwd" or rider is not None
    params = _cparams(*(["arbitrary"] * 3 if ordered else ["parallel", "parallel", "arbitrary"]))
    if rider is None:
        return _pallas(body, name=name, grid=n_grid, in_specs=in_specs, out_specs=out_specs, out_shape=out_shape,
                       scratch_shapes=scratch, compiler_params=params)(a, b, *extra, *gains)
    as_list = lambda t: list(t) if isinstance(t, (list, tuple)) else [t]
    outs = _pallas(body, name=name, grid=n_grid, in_specs=in_specs + rider.in_specs(),
                   out_specs=as_list(out_specs) + rider.out_specs(), out_shape=as_list(out_shape) + rider.out_shape(),
                   scratch_shapes=scratch + rider.scratch(), input_output_aliases=rider.aliases(n_in, n_out),
                   compiler_params=params)(a, b, *extra, *gains, *rider.args())
    return (*outs[:n_out], outs[n_out:])


def _rows(ts, d, col=0):
    return pl.BlockSpec((ts, d), lambda i: (i, col))


def _gain(d):
    return pl.BlockSpec((1, d), lambda i: (0, 0))


def _accum(d):
    return pl.BlockSpec((SUBLANES, d), lambda i: (0, 0))


def _accumulate(ref, val):
    i = pl.program_id(0)

    @pl.when(i == 0)
    def _():
        ref[...] = jnp.zeros_like(ref)

    ref[...] += jnp.broadcast_to(jnp.sum(val, axis=0, keepdims=True), ref.shape)


def _rinv(x, n):
    return lax.rsqrt(jnp.sum(x * x, axis=-1, keepdims=True) / n + EPS)


def _norm_bwd(x, dy, g, n):
    r = _rinv(x, n)
    xhat = x * r
    dyg = dy * g
    dx = r * (dyg - xhat * (jnp.sum(dyg * xhat, axis=-1, keepdims=True) / n))
    return dx, dy * xhat


def _rms_fwd(x, g, name):
    S, D = x.shape
    ts = _tile(S, ROW_TILE, SUBLANES)

    def body(x_ref, g_ref, h_ref):
        xv = x_ref[...]
        h_ref[...] = (xv * _rinv(xv, D) * g_ref[...]).astype(BF16)

    return _pallas(body, name=name, grid=(S // ts,), in_specs=[_rows(ts, D), _gain(D)], out_specs=_rows(ts, D),
                   out_shape=jax.ShapeDtypeStruct((S, D), BF16), compiler_params=_cparams("parallel"))(x, g)


def _lat_fwd(z, gq, gkv):
    S = z.shape[0]
    ts = _tile(S, ROW_TILE, SUBLANES)

    def body(z_ref, gq_ref, gkv_ref, q_ref, kv_ref):
        zq = z_ref[:, Z_Q:Z_KV]
        zkv = z_ref[:, Z_KV:Z_KPE]
        q_ref[...] = (zq * _rinv(zq, Q_LORA) * gq_ref[...]).astype(BF16)
        kv_ref[...] = (zkv * _rinv(zkv, KV_LORA) * gkv_ref[...]).astype(BF16)

    return _pallas(body, name="lat_fwd", grid=(S // ts,), in_specs=[_rows(ts, Z_LAT), _gain(Q_LORA), _gain(KV_LORA)],
                   out_specs=[_rows(ts, Q_LORA), _rows(ts, KV_LORA)],
                   out_shape=[jax.ShapeDtypeStruct((S, Q_LORA), BF16), jax.ShapeDtypeStruct((S, KV_LORA), BF16)],
                   compiler_params=_cparams("parallel"))(z, gq, gkv)


def _lat_bwd(dq, dkv, z, gq, gkv):
    S = z.shape[0]
    ts = _tile(S, ROW_TILE, SUBLANES)

    def body(dq_ref, dkv_ref, z_ref, gq_ref, gkv_ref, dlat_ref, dgq_ref, dgkv_ref):
        dxq, cq = _norm_bwd(z_ref[:, Z_Q:Z_KV], dq_ref[...], gq_ref[...], Q_LORA)
        dxkv, ckv = _norm_bwd(z_ref[:, Z_KV:Z_KPE], dkv_ref[...], gkv_ref[...], KV_LORA)
        dlat_ref[:, Z_Q:Z_KV] = dxq.astype(BF16)
        dlat_ref[:, Z_KV:Z_KPE] = dxkv.astype(BF16)
        _accumulate(dgq_ref, cq)
        _accumulate(dgkv_ref, ckv)

    return _pallas(body, name="lat_bwd", grid=(S // ts,),
                   in_specs=[_rows(ts, Q_LORA), _rows(ts, KV_LORA), _rows(ts, Z_LAT), _gain(Q_LORA), _gain(KV_LORA)],
                   out_specs=[_rows(ts, Z_LAT), _accum(Q_LORA), _accum(KV_LORA)],
                   out_shape=[jax.ShapeDtypeStruct((S, Z_LAT), BF16), jax.ShapeDtypeStruct((SUBLANES, Q_LORA), F32),
                              jax.ShapeDtypeStruct((SUBLANES, KV_LORA), F32)],
                   compiler_params=_cparams("arbitrary"))(dq, dkv, z, gq, gkv)


def _head_masks():
    lane = lax.broadcasted_iota(jnp.int32, (1, HEAD_PAD), 1)
    return lane < ROPE_LO, (lane >= ROPE_LO) & (lane < ROPE_HI)


def _head_matrices():
    k = jnp.arange(HEAD_PAD)[:, None]
    j = jnp.arange(HEAD_PAD)[None, :]
    nope = (k < ROPE_LO) & (j < ROPE_LO)
    rope = (k >= ROPE_LO) & (k < ROPE_HI) & (j >= ROPE_LO) & (j < ROPE_HI)
    seg = jnp.where(nope, 1.0 / QK_NOPE, jnp.where(rope, 1.0 / QK_ROPE, 0.0))
    half = QK_ROPE // 2
    rot = jnp.where((j >= ROPE_LO) & (j < ROPE_MID) & (k == j + half), -1.0,
                    jnp.where((j >= ROPE_MID) & (j < ROPE_HI) & (k == j - half), 1.0, 0.0))
    return seg.astype(BF16), rot.astype(BF16)


_HEAD_MAT = pl.BlockSpec((HEAD_PAD, HEAD_PAD), lambda i: (0, 0))


def _split_dot(t, mat):
    hi = t.astype(BF16)
    lo = (t - hi.astype(F32)).astype(BF16)
    return jnp.dot(hi, mat, preferred_element_type=F32) + jnp.dot(lo, mat, preferred_element_type=F32)


def _qk_fwd(q_raw, kv_raw, z, cs, sn, gq, gk, seg, rot):
    S = q_raw.shape[0]
    ts = _tile(S, ROW_TILE, SUBLANES)

    def body(q_ref, k_ref, v_ref, kpe_ref, cs_ref, sn_ref, gq_ref, gk_ref, seg_ref, rot_ref, qf_ref, kf_ref, vb_ref):
        cos, sin, gqv, gkv = cs_ref[...], sn_ref[...], gq_ref[...], gk_ref[...]
        seg_m, rot_m = seg_ref[...], rot_ref[...]

        def norm(x, g):
            return x * lax.rsqrt(_split_dot(x * x, seg_m) + EPS) * g

        def rope(y):
            return y * cos + _split_dot(y, rot_m) * sin

        kr = rope(norm(kpe_ref[...], gkv))
        lane = lax.broadcasted_iota(jnp.int32, (1, HEAD_PAD), 1)
        for h in range(N_HEADS):
            sl = slice(h * HEAD_PAD, (h + 1) * HEAD_PAD)
            qf_ref[:, sl] = rope(norm(q_ref[:, sl], gqv)).astype(BF16)
            kf_ref[:, sl] = (norm(k_ref[:, sl], gkv) + kr).astype(BF16)
            vb_ref[:, sl] = jnp.where(lane == V_HEAD, 1.0, v_ref[:, sl]).astype(BF16)

    hd = jax.ShapeDtypeStruct((S, HP), BF16)
    return _pallas(body, name="qk_fwd", grid=(S // ts,),
                   in_specs=[_rows(ts, HP), _rows(ts, HP, 0), _rows(ts, HP, 1), _rows(ts, HEAD_PAD, Z_KPE // HEAD_PAD),
                             _rows(ts, HEAD_PAD), _rows(ts, HEAD_PAD), _gain(HEAD_PAD), _gain(HEAD_PAD), _HEAD_MAT, _HEAD_MAT],
                   out_specs=[_rows(ts, HP)] * 3, out_shape=[hd, hd, hd],
                   compiler_params=_cparams("parallel"))(q_raw, kv_raw, kv_raw, z, cs, sn, gq, gk, seg, rot)


def _qk_bwd(dqf, dkf, dv, q_raw, kv_raw, z, cs, sn, gq, gk, seg, rot, rider):
    S = q_raw.shape[0]
    ts = _tile(S, ROW_TILE, SUBLANES)

    def body(dqf_ref, dkf_ref, dv_ref, q_ref, k_ref, kpe_ref, cs_ref, sn_ref, gq_ref, gk_ref, seg_ref, rot_ref, *rest):
        (dq_ref, dkv_ref, dkpe_ref, dgq_ref, dgk_ref), ride = rider.split(rest, 5, 0)
        ride(pl.program_id(0), S // ts)
        m_n, m_r = _head_masks()
        cos, sin, gqv, gkv = cs_ref[...], sn_ref[...], gq_ref[...], gk_ref[...]
        seg_m, rot_m = seg_ref[...], rot_ref[...]

        def rope_t(w):
            return w * cos - _split_dot(w * sin, rot_m)

        def norm_bwd(x, dy, g):
            r = lax.rsqrt(_split_dot(x * x, seg_m) + EPS)
            xhat = x * r
            dyg = dy * g
            return r * (dyg - xhat * _split_dot(dyg * xhat, seg_m)), dy * xhat

        accq = jnp.zeros((ts, HEAD_PAD), F32)
        acck = jnp.zeros((ts, HEAD_PAD), F32)
        dkr = jnp.zeros((ts, HEAD_PAD), F32)
        for h in range(N_HEADS):
            sl = slice(h * HEAD_PAD, (h + 1) * HEAD_PAD)
            dx, c = norm_bwd(q_ref[:, sl], rope_t(dqf_ref[:, sl]), gqv)
            dq_ref[:, sl] = dx.astype(BF16)
            accq = accq + c
            dk = dkf_ref[:, sl]
            dkr = dkr + jnp.where(m_r, dk, 0.0)
            dx, c = norm_bwd(k_ref[:, sl], jnp.where(m_n, dk, 0.0), gkv)
            dkv_ref[:, sl] = dx.astype(BF16)
            acck = acck + c
            dkv_ref[:, HP + h * HEAD_PAD:HP + (h + 1) * HEAD_PAD] = dv_ref[:, sl].astype(BF16)
        dx, c = norm_bwd(kpe_ref[...], rope_t(dkr), gkv)
        dkpe_ref[...] = dx.astype(BF16)
        _accumulate(dgq_ref, accq)
        _accumulate(dgk_ref, acck + c)

    outs = _pallas(body, name="qk_bwd", grid=(S // ts,),
                   in_specs=[_rows(ts, HP)] * 4 + [_rows(ts, HP, 0), _rows(ts, HEAD_PAD, Z_KPE // HEAD_PAD),
                                                   _rows(ts, HEAD_PAD), _rows(ts, HEAD_PAD), _gain(HEAD_PAD), _gain(HEAD_PAD),
                                                   _HEAD_MAT, _HEAD_MAT] + rider.in_specs(),
                   out_specs=[_rows(ts, HP), _rows(ts, 2 * HP), _rows(ts, HEAD_PAD), _accum(HEAD_PAD), _accum(HEAD_PAD)]
                   + rider.out_specs(),
                   out_shape=[jax.ShapeDtypeStruct((S, HP), BF16), jax.ShapeDtypeStruct((S, 2 * HP), BF16),
                              jax.ShapeDtypeStruct((S, HEAD_PAD), BF16), jax.ShapeDtypeStruct((SUBLANES, HEAD_PAD), F32),
                              jax.ShapeDtypeStruct((SUBLANES, HEAD_PAD), F32)] + rider.out_shape(),
                   scratch_shapes=rider.scratch(), input_output_aliases=rider.aliases(12, 5),
                   compiler_params=_cparams("arbitrary"))(dqf, dkf, dv, q_raw, kv_raw, z, cs, sn, gq, gk, seg, rot, *rider.args())
    return (*outs[:5], outs[5:])


def _causal(t):
    row = lax.broadcasted_iota(jnp.int32, (t, t), 0)
    col = lax.broadcasted_iota(jnp.int32, (t, t), 1)
    return col <= row


def _attn_tiles(S, chains):
    t = _tile(S, ATT_TILE, SUBLANES)
    return t, min(chains, S // t)


def _attn_fwd(qf, kf, vb, rider):
    S = qf.shape[0]
    t, nc = _attn_tiles(S, ATT_CHAINS_FWD)
    tq = nc * t
    nqt = S // tq

    def body(q_ref, k_ref, v_ref, *rest):
        (o_ref, lse_ref, m_ref, acc_ref), ride = rider.split(rest, 2, 2)
        qt = pl.program_id(1)
        ride(pl.program_id(0) * nqt + qt, N_HEADS * nqt)
        m_ref[...] = jnp.full_like(m_ref, -jnp.inf)
        acc_ref[...] = jnp.zeros_like(acc_ref)

        def against(rows, first, diagonal):
            kb, vb_ = k_ref[rows, :], v_ref[rows, :]
            subs = [slice(a * t, (a + 1) * t) for a in range(first, nc)]
            s = [lax.dot_general(q_ref[sub, :], kb, _DIMS["nt"], preferred_element_type=F32) * ATT_SCALE_LOG2 for sub in subs]
            if diagonal:
                s[0] = jnp.where(_causal(t), s[0], -jnp.inf)
            m_old = [m_ref[sub, :] for sub in subs]
            m_new = [jnp.maximum(mo, jnp.max(sa, axis=-1, keepdims=True)) for mo, sa in zip(m_old, s)]
            p = [jnp.exp2(sa - jnp.concatenate([mn] * (t // HEAD_PAD), axis=1)).astype(BF16) for sa, mn in zip(s, m_new)]
            for sub, mo, mn, pa in zip(subs, m_old, m_new, p):
                acc_ref[sub, :] = jnp.exp2(mo - mn) * acc_ref[sub, :] + jnp.dot(pa, vb_, preferred_element_type=F32)
                m_ref[sub, :] = mn

        def trip(j, carry):
            against(pl.ds(pl.multiple_of(j * t, t), t), 0, False)
            return carry

        lax.fori_loop(0, nc * qt, trip, 0)
        for d in range(nc):
            against(pl.ds(pl.multiple_of((nc * qt + d) * t, t), t), d, True)
        acc = acc_ref[...]
        l = acc[:, V_HEAD:V_HEAD + 1]
        lane = lax.broadcasted_iota(jnp.int32, (1, HEAD_PAD), 1)
        o_ref[...] = jnp.where(lane < V_HEAD, acc / l, 0.0)
        lse_ref[...] = m_ref[...] + jnp.log(l) * LOG2E

    tile = pl.BlockSpec((tq, HEAD_PAD), lambda h, i: (i, h))
    full = pl.BlockSpec((S, HEAD_PAD), lambda h, i: (0, h))
    outs = _pallas(body, name="attn_fwd", grid=(N_HEADS, nqt), in_specs=[tile, full, full] + rider.in_specs(),
                   out_specs=[tile, pl.BlockSpec((None, tq, HEAD_PAD), lambda h, i: (h, i, 0))] + rider.out_specs(),
                   out_shape=[jax.ShapeDtypeStruct((S, HP), F32), jax.ShapeDtypeStruct((N_HEADS, S, HEAD_PAD), F32)]
                   + rider.out_shape(),
                   scratch_shapes=[pltpu.VMEM((tq, HEAD_PAD), F32), pltpu.VMEM((tq, HEAD_PAD), F32)] + rider.scratch(),
                   input_output_aliases=rider.aliases(3, 2),
                   compiler_params=_cparams("arbitrary", "arbitrary"))(qf, kf, vb, *rider.args())
    return outs[0], outs[1], outs[2:]


def _attn_bwd(qf, kf, vb, o, do, lse, rider):
    S = qf.shape[0]
    t, nc = _attn_tiles(S, ATT_CHAINS_BWD)
    tkv = nc * t
    nq = S // t
    nkt = S // tkv

    def body(q_ref, k_ref, v_ref, o_ref, do_ref, lse_ref, *rest):
        (dq_ref, dk_ref, dv_ref), ride = rider.split(rest, 3, 0)
        kt = pl.program_id(1)
        ride(pl.program_id(0) * nkt + kt, N_HEADS * nkt)

        @pl.when(kt == 0)
        def _():
            dq_ref[...] = jnp.zeros_like(dq_ref)

        dk_ref[...] = jnp.zeros_like(dk_ref)
        dv_ref[...] = jnp.zeros_like(dv_ref)

        def q_block(rows):
            dof = do_ref[rows, :]
            delta = jnp.sum(dof * o_ref[rows, :], axis=-1, keepdims=True)
            return q_ref[rows, :], dof.astype(BF16), lse_ref[rows, :][:, :1], delta

        def against(rows, n_sub, diagonal):
            q, dob, lse, delta = q_block(rows)
            subs = [slice(b * t, (b + 1) * t) for b in range(n_sub)]
            kbs = [k_ref[sub, :] for sub in subs]
            s = [lax.dot_general(q, kb, _DIMS["nt"], preferred_element_type=F32) for kb in kbs]
            dp = [lax.dot_general(dob, v_ref[sub, :], _DIMS["nt"], preferred_element_type=F32) for sub in subs]
            p = [jnp.exp2(sb * ATT_SCALE_LOG2 - lse) for sb in s]
            if diagonal:
                p[-1] = jnp.where(_causal(t), p[-1], 0.0)
            ds = [(pb * (dpb - delta) * ATT_SCALE).astype(BF16) for pb, dpb in zip(p, dp)]
            for sub, pb in zip(subs, p):
                dv_ref[sub, :] += lax.dot_general(pb.astype(BF16), dob, _DIMS["tn"], preferred_element_type=F32)
            for sub, dsb in zip(subs, ds):
                dk_ref[sub, :] += lax.dot_general(dsb, q, _DIMS["tn"], preferred_element_type=F32)
            dq_ref[rows, :] += sum(jnp.dot(dsb, kb, preferred_element_type=F32) for dsb, kb in zip(ds, kbs))

        for a in range(nc):
            against(pl.ds(pl.multiple_of((nc * kt + a) * t, t), t), a + 1, True)

        def trip(i, carry):
            against(pl.ds(pl.multiple_of(i * t, t), t), nc, False)
            return carry

        lax.fori_loop(nc * (kt + 1), nq, trip, 0)

    tile = pl.BlockSpec((tkv, HEAD_PAD), lambda h, j: (j, h))
    full = pl.BlockSpec((S, HEAD_PAD), lambda h, j: (0, h))
    hd = jax.ShapeDtypeStruct((S, HP), F32)
    outs = _pallas(body, name="attn_bwd", grid=(N_HEADS, nkt),
                   in_specs=[full, tile, tile, full, full, pl.BlockSpec((None, S, HEAD_PAD), lambda h, j: (h, 0, 0))]
                   + rider.in_specs(),
                   out_specs=[full, tile, tile] + rider.out_specs(), out_shape=[hd, hd, hd] + rider.out_shape(),
                   scratch_shapes=rider.scratch(), input_output_aliases=rider.aliases(6, 3),
                   compiler_params=_cparams("arbitrary", "arbitrary"))(qf, kf, vb, o, do, lse, *rider.args())
    return outs[0], outs[1], outs[2], outs[3:]


def _shift_down(u, j, row):
    return jnp.where(row >= j, pltpu.roll(u, j, 0), 0.0)


def _shift_up(u, j, row, s):
    return jnp.where(row < s - j, pltpu.roll(u, s - j, 0), 0.0)


def _conv_cols(s, first_tile):
    return pl.BlockSpec((s, LANES), lambda cb: (0, first_tile + cb))


def _conv_fwd(z, cw8):
    S = z.shape[0]

    def body(gb_ref, gc_ref, xin_ref, w_ref, out_ref):
        row = lax.broadcasted_iota(jnp.int32, (S, LANES), 0)
        u = gc_ref[...] * xin_ref[...]
        y = w_ref[0:1, :] * u
        for j in range(1, CONV_TAPS):
            y = y + w_ref[j:j + 1, :] * _shift_down(u, j, row)
        out_ref[...] = gb_ref[...] * y

    return _pallas(body, name="conv_fwd", grid=(CONV_WIDTH // LANES,),
                   in_specs=[_conv_cols(S, Z_GB // LANES), _conv_cols(S, Z_GC // LANES), _conv_cols(S, Z_XIN // LANES),
                             pl.BlockSpec((SUBLANES, LANES), lambda cb: (0, cb))],
                   out_specs=_conv_cols(S, 0), out_shape=jax.ShapeDtypeStruct((S, CONV_WIDTH), F32),
                   compiler_params=_cparams("parallel"))(z, z, z, cw8)


def _conv_bwd(dconv, z, cw8):
    S = z.shape[0]

    def body(d_ref, gb_ref, gc_ref, xin_ref, w_ref, dgb_ref, dgc_ref, dxin_ref, dw_ref):
        row = lax.broadcasted_iota(jnp.int32, (S, LANES), 0)
        gc, xin, d = gc_ref[...], xin_ref[...], d_ref[...]
        u = gc * xin
        dy = d * gb_ref[...]
        y = w_ref[0:1, :] * u
        du = w_ref[0:1, :] * dy
        dw = [jnp.sum(dy * u, axis=0, keepdims=True)]
        for j in range(1, CONV_TAPS):
            uj = _shift_down(u, j, row)
            y = y + w_ref[j:j + 1, :] * uj
            du = du + w_ref[j:j + 1, :] * _shift_up(dy, j, row, S)
            dw.append(jnp.sum(dy * uj, axis=0, keepdims=True))
        dgb_ref[...] = (d * y).astype(BF16)
        dgc_ref[...] = (du * xin).astype(BF16)
        dxin_ref[...] = (du * gc).astype(BF16)
        tap = lax.broadcasted_iota(jnp.int32, (SUBLANES, LANES), 0)
        dw_ref[...] = sum(jnp.where(tap == j, dw[j], 0.0) for j in range(CONV_TAPS))

    col = _conv_cols(S, 0)
    sd = jax.ShapeDtypeStruct((S, CONV_WIDTH), BF16)
    return _pallas(body, name="conv_bwd", grid=(CONV_WIDTH // LANES,),
                   in_specs=[col, _conv_cols(S, Z_GB // LANES), _conv_cols(S, Z_GC // LANES), _conv_cols(S, Z_XIN // LANES),
                             pl.BlockSpec((SUBLANES, LANES), lambda cb: (0, cb))],
                   out_specs=[col, col, col, pl.BlockSpec((SUBLANES, LANES), lambda cb: (0, cb))],
                   out_shape=[sd, sd, sd, jax.ShapeDtypeStruct((SUBLANES, CONV_WIDTH), F32)],
                   compiler_params=_cparams("parallel"))(dconv, z, z, z, cw8)


def _mix_fwd(o, conv, ga, gc):
    S = o.shape[0]
    ts = _tile(S, ROW_TILE, SUBLANES)

    def body(o_ref, c_ref, ga_ref, gc_ref, out_ref):
        ov, cv = o_ref[...], c_ref[...]
        out_ref[:, :HP] = (ov * _rinv(ov, ATTN_WIDTH) * ga_ref[...]).astype(BF16)
        out_ref[:, HP:] = (cv * _rinv(cv, CONV_WIDTH) * gc_ref[...]).astype(BF16)

    return _pallas(body, name="mix_fwd", grid=(S // ts,),
                   in_specs=[_rows(ts, HP), _rows(ts, CONV_WIDTH), _gain(HP), _gain(CONV_WIDTH)],
                   out_specs=_rows(ts, HP + CONV_WIDTH), out_shape=jax.ShapeDtypeStruct((S, HP + CONV_WIDTH), BF16),
                   compiler_params=_cparams("parallel"))(o, conv, ga, gc)


def _mix_bwd(dmixed, o, conv, ga, gc):
    S = o.shape[0]
    ts = _tile(S, ROW_TILE, SUBLANES)

    def body(d_ref, o_ref, c_ref, ga_ref, gc_ref, do_ref, dc_ref, dga_ref, dgc_ref):
        dx, ca = _norm_bwd(o_ref[...], d_ref[:, :HP], ga_ref[...], ATTN_WIDTH)
        do_ref[...] = dx
        dx, cc = _norm_bwd(c_ref[...], d_ref[:, HP:], gc_ref[...], CONV_WIDTH)
        dc_ref[...] = dx
        _accumulate(dga_ref, ca)
        _accumulate(dgc_ref, cc)

    return _pallas(body, name="mix_bwd", grid=(S // ts,),
                   in_specs=[_rows(ts, HP + CONV_WIDTH), _rows(ts, HP), _rows(ts, CONV_WIDTH), _gain(HP), _gain(CONV_WIDTH)],
                   out_specs=[_rows(ts, HP), _rows(ts, CONV_WIDTH), _accum(HP), _accum(CONV_WIDTH)],
                   out_shape=[jax.ShapeDtypeStruct((S, HP), F32), jax.ShapeDtypeStruct((S, CONV_WIDTH), F32),
                              jax.ShapeDtypeStruct((SUBLANES, HP), F32), jax.ShapeDtypeStruct((SUBLANES, CONV_WIDTH), F32)],
                   compiler_params=_cparams("arbitrary"))(dmixed, o, conv, ga, gc)


def _ple_fwd(x, gl, pe, g_next):
    S, D = x.shape
    ts = _tile(S, ROW_TILE, SUBLANES)

    def body(x_ref, gl_ref, pe_ref, g_ref, out_ref, h_ref):
        y = x_ref[...] + jax.nn.sigmoid(gl_ref[...]) * pe_ref[...]
        out_ref[...] = y
        h_ref[...] = (y * _rinv(y, D) * g_ref[...]).astype(BF16)

    return _pallas(body, name="ple_fwd", grid=(S // ts,), in_specs=[_rows(ts, D)] * 3 + [_gain(D)], out_specs=[_rows(ts, D)] * 2,
                   out_shape=[jax.ShapeDtypeStruct((S, D), F32), jax.ShapeDtypeStruct((S, D), BF16)],
                   compiler_params=_cparams("parallel"))(x, gl, pe, g_next)


def _ple_bwd(dx, gl, pe):
    S, D = dx.shape
    ts = _tile(S, ROW_TILE, SUBLANES)

    def body(dx_ref, gl_ref, pe_ref, dpe_ref, dgl_ref):
        d = dx_ref[...]
        gate = jax.nn.sigmoid(gl_ref[...])
        dpe_ref[...] = (d * gate).astype(BF16)
        dgl_ref[...] = (d * pe_ref[...] * (gate * (1.0 - gate))).astype(BF16)

    sd = jax.ShapeDtypeStruct((S, D), BF16)
    return _pallas(body, name="ple_bwd", grid=(S // ts,), in_specs=[_rows(ts, D)] * 3, out_specs=[_rows(ts, D)] * 2,
                   out_shape=[sd, sd], compiler_params=_cparams("parallel"))(dx, gl, pe)


def _loss_grad(y, target):
    S, D = y.shape
    ts = _tile(S, ROW_TILE, SUBLANES)

    def body(y_ref, t_ref, dy_ref, sq_ref):
        e = y_ref[...] - t_ref[...]
        dy_ref[...] = e / D

        @pl.when(pl.program_id(0) == 0)
        def _():
            sq_ref[...] = jnp.zeros_like(sq_ref)

        sq_ref[...] += jnp.broadcast_to(jnp.sum(jnp.sum(e * e, axis=1, keepdims=True), axis=0, keepdims=True), sq_ref.shape)

    return _pallas(body, name="loss_grad", grid=(S // ts,), in_specs=[_rows(ts, D)] * 2,
                   out_specs=[_rows(ts, D), _accum(LANES)],
                   out_shape=[jax.ShapeDtypeStruct((S, D), F32), jax.ShapeDtypeStruct((SUBLANES, LANES), F32)],
                   compiler_params=_cparams("arbitrary"))(y, target)


def _adamw(w, g, m, v, name):
    L, R, C = w.shape
    tr = _tile(R, ROW_TILE, SUBLANES)

    def body(w_ref, g_ref, m_ref, v_ref, d_ref, m2_ref, v2_ref):
        gv = g_ref[...]
        m2 = ADAM_B1 * m_ref[...] + (1.0 - ADAM_B1) * gv
        v2 = ADAM_B2 * v_ref[...] + (1.0 - ADAM_B2) * (gv * gv)
        m_hat = m2 / (1.0 - ADAM_B1 ** ADAM_STEP)
        v_hat = v2 / (1.0 - ADAM_B2 ** ADAM_STEP)
        d_ref[...] = -ADAM_LR * (m_hat / (jnp.sqrt(v_hat) + ADAM_EPS) + ADAM_WD * w_ref[...])
        m2_ref[...] = m2
        v2_ref[...] = v2

    sd = jax.ShapeDtypeStruct((L, R, C), F32)
    spec = pl.BlockSpec((None, tr, C), lambda l, i: (l, i, 0))
    return _pallas(body, name=name, grid=(L, R // tr), in_specs=[spec] * 4, out_specs=[spec] * 3, out_shape=[sd, sd, sd],
                   compiler_params=_cparams("parallel", "parallel"))(w, g, m, v)


def _place():
    return lax.axis_index("x"), lax.axis_index("y"), lax.axis_index("c")


def _other_chips(x, y):
    return [(1 - x, y), (x, 1 - y), (1 - x, 1 - y)]


HBM = pl.BlockSpec(memory_space=pl.ANY)
VMEM_SPEC = pl.BlockSpec(memory_space=pltpu.VMEM)


def _remote_copy(send_sems, recv_sems, k, src, dst, to):
    return pltpu.make_async_remote_copy(src_ref=src, dst_ref=dst, send_sem=send_sems.at[k], recv_sem=recv_sems.at[k],
                                        device_id=to, device_id_type=MESH)


def _comm_call(body, name, arrays, out_shapes, n_remote, in_place=False):
    scratch = [pltpu.SemaphoreType.DMA((n_remote,)), pltpu.SemaphoreType.DMA((n_remote,))]
    aliases = {i: i for i in range(len(arrays))} if in_place else {}
    return _pallas(body, name=name, in_specs=[HBM] * len(arrays), out_specs=[HBM] * len(out_shapes), out_shape=out_shapes,
                   scratch_shapes=scratch, input_output_aliases=aliases,
                   compiler_params=pltpu.CompilerParams(has_side_effects=True))(*arrays)


def _shard_slab(w, chip_arr, name, stack_rows):
    L, R, C = w.shape
    tr = _tile(R, ROW_TILE, 16)
    nb = R // tr

    def body(j_ref, w_ref, out_ref):
        out_ref[...] = w_ref[...].astype(BF16)

    if stack_rows:
        out_shape = jax.ShapeDtypeStruct((L, N_CHIPS * R, C), BF16)
        out_spec = pl.BlockSpec((None, tr, C), lambda l, i, j_ref: (l, j_ref[0] * nb + i, 0))
    else:
        out_shape = jax.ShapeDtypeStruct((N_CHIPS, L, R, C), BF16)
        out_spec = pl.BlockSpec((None, None, tr, C), lambda l, i, j_ref: (j_ref[0], l, i, 0))
    grid_spec = pltpu.PrefetchScalarGridSpec(
        num_scalar_prefetch=1, grid=(L, nb), in_specs=[pl.BlockSpec((None, tr, C), lambda l, i, j_ref: (l, i, 0))],
        out_specs=out_spec)
    return _pallas(body, name=name, grid_spec=grid_spec, out_shape=out_shape,
                   compiler_params=_cparams("parallel", "parallel"))(chip_arr, w)


def _shard_dims(slab):
    return (slab.shape[2], slab.shape[3]) if slab.ndim == 4 else (slab.shape[1] // N_CHIPS, slab.shape[2])


def _shard_rows(ref, chip, l, which):
    ks, _ = _shard_dims(ref)
    h = ks // 2
    return ref.at[chip, l, pl.ds(which * h, h)] if len(ref.shape) == 4 else ref.at[l, pl.ds(chip * ks + which * h, h)]


class Rider:
    def __init__(self, reads, inplace, n_sems, stages):
        self.reads, self.inplace, self.n_sems, self.stages = list(reads), list(inplace), n_sems, stages

    def args(self):
        return self.reads + self.inplace

    def in_specs(self):
        return [HBM] * len(self.args())

    def out_specs(self):
        return [HBM] * len(self.inplace)

    def out_shape(self):
        return [jax.ShapeDtypeStruct(a.shape, a.dtype) for a in self.inplace]

    def scratch(self):
        return [pltpu.SemaphoreType.DMA((self.n_sems,)), pltpu.SemaphoreType.DMA((self.n_sems,))] if self.n_sems else []

    def aliases(self, n_host_in, n_host_out):
        return {n_host_in + len(self.reads) + j: n_host_out + j for j in range(len(self.inplace))}

    def split(self, rest, n_host_out, n_host_scratch):
        n_r, n_io = len(self.reads), len(self.inplace)
        reads = rest[:n_r]
        host_out = rest[n_r + n_io:n_r + n_io + n_host_out]
        outs = rest[n_r + n_io + n_host_out:n_r + 2 * n_io + n_host_out]
        host_scratch = rest[n_r + 2 * n_io + n_host_out:n_r + 2 * n_io + n_host_out + n_host_scratch]
        sems = rest[n_r + 2 * n_io + n_host_out + n_host_scratch:]

        def ride(step, n_steps):
            at = [0, n_steps - 1] if len(self.stages) == 2 else [0, (3 * n_steps) // 4, n_steps - 1]
            for s, stage in zip(at, self.stages):
                pl.when(step == s)(functools.partial(stage, reads, outs, *sems))

        return tuple(host_out) + tuple(host_scratch), ride

    def run(self, name):
        n_r, n_io = len(self.reads), len(self.inplace)

        def body(*refs):
            for stage in self.stages:
                stage(refs[:n_r], refs[n_r + n_io:n_r + 2 * n_io], *refs[n_r + 2 * n_io:])

        return _pallas(body, name=name, in_specs=self.in_specs(), out_specs=self.out_specs(), out_shape=self.out_shape(),
                       scratch_shapes=self.scratch(), input_output_aliases=self.aliases(0, 0),
                       compiler_params=pltpu.CompilerParams(has_side_effects=True))(*self.args())


def _allgather_rider(slabs, items):
    def stage(which, reads, outs, send_sems, recv_sems):
        x, y, c = _place()
        me, sibling = (x, y, c), (x, y, 1 - c)
        copy = functools.partial(_remote_copy, send_sems, recv_sems)
        for n, (j, l) in enumerate(items):
            ref = outs[j]
            own = _shard_rows(ref, 2 * x + y, l, c)
            for k, (cx, cy) in enumerate(_other_chips(x, y)):
                arrived = _shard_rows(ref, 2 * cx + cy, l, c)
                if which == 0:
                    copy(6 * n + k, own, own, (cx, cy, c)).start()
                elif which == 1:
                    copy(6 * n + k, arrived, arrived, me).wait_recv()
                    copy(6 * n + 3 + k, arrived, arrived, sibling).start()
                else:
                    passed = _shard_rows(ref, 2 * cx + cy, l, 1 - c)
                    copy(6 * n + 3 + k, passed, passed, me).wait_recv()
                    copy(6 * n + k, own, own, me).wait_send()
                    copy(6 * n + 3 + k, arrived, arrived, me).wait_send()

    return Rider([], slabs, 6 * len(items), [functools.partial(stage, w) for w in range(3)])


def _exchange_rider(grads, landing, items):
    def stage(start, reads, outs, send_sems, recv_sems):
        x, y, c = _place()
        for n, (i, j, l) in enumerate(items):
            g_ref, r_ref = reads[i], outs[j]
            h = g_ref.shape[1] // 2
            for k in range(1, N_DEV):
                px, py, pc = (1 - x if k & 4 else x, 1 - y if k & 2 else y, 1 - c if k & 1 else c)
                cp = _remote_copy(send_sems, recv_sems, 7 * n + k - 1, g_ref.at[2 * px + py, pl.ds(pc * h, h)],
                                  r_ref.at[l, k - 1], (px, py, pc))
                if start:
                    cp.start()
                else:
                    cp.wait()

    return Rider(grads, landing, 7 * len(items), [functools.partial(stage, True), functools.partial(stage, False)])


def _reduce_partials(g, r, chip_arr, c_arr, name):
    L, n, R, C = g.shape
    H = R // 2
    th = _tile(H, ROW_TILE, 16)
    nb = H // th

    def body(j_ref, c_ref, g_ref, r_ref, out_ref):
        s = g_ref[...].astype(F32)
        for k in range(N_DEV - 1):
            s = s + r_ref[k].astype(F32)
        out_ref[...] = s

    grid_spec = pltpu.PrefetchScalarGridSpec(
        num_scalar_prefetch=2, grid=(L, nb),
        in_specs=[pl.BlockSpec((None, None, th, C), lambda l, i, j_ref, c_ref: (l, j_ref[0], c_ref[0] * nb + i, 0)),
                  pl.BlockSpec((None, N_DEV - 1, th, C), lambda l, i, j_ref, c_ref: (l, 0, i, 0))],
        out_specs=pl.BlockSpec((None, th, C), lambda l, i, j_ref, c_ref: (l, c_ref[0] * nb + i, 0)))
    return _pallas(body, name=name, grid_spec=grid_spec, out_shape=jax.ShapeDtypeStruct((L, R, C), F32),
                   compiler_params=_cparams("parallel", "parallel"))(chip_arr, c_arr, g, r)


def _join_halves(fs):
    n = len(fs)

    def body(*refs):
        out_refs, (send_sems, recv_sems) = refs[n:2 * n], refs[2 * n:]
        x, y, c = _place()
        cps = []
        for i, out_ref in enumerate(out_refs):
            h = out_ref.shape[1] // 2
            mine = out_ref.at[:, pl.ds(c * h, h)]
            cps.append(_remote_copy(send_sems, recv_sems, i, mine, mine, (x, y, 1 - c)))
            cps[-1].start()
        for cp in cps:
            cp.wait()

    return _comm_call(body, "grad_join_halves", fs, [jax.ShapeDtypeStruct(f.shape, f.dtype) for f in fs], n, in_place=True)


def _allgather_small(v, name):
    R, C = v.shape

    def body(v_ref, out_ref, send_sems, recv_sems):
        x, y, c = _place()
        me = 4 * x + 2 * y + c
        out_ref[me] = v_ref[...]
        cps = []
        for k in range(1, N_DEV):
            peer = (1 - x if k & 4 else x, 1 - y if k & 2 else y, 1 - c if k & 1 else c)
            cps.append(pltpu.make_async_remote_copy(src_ref=v_ref, dst_ref=out_ref.at[me], send_sem=send_sems.at[k - 1],
                                                    recv_sem=recv_sems.at[k - 1], device_id=peer, device_id_type=MESH))
        for cp in cps:
            cp.start()
        for cp in cps:
            cp.wait()

    return _pallas(body, name=name, in_specs=[VMEM_SPEC], out_specs=VMEM_SPEC,
                   out_shape=jax.ShapeDtypeStruct((N_DEV, R, C), v.dtype),
                   scratch_shapes=[pltpu.SemaphoreType.DMA((N_DEV - 1,)), pltpu.SemaphoreType.DMA((N_DEV - 1,))],
                   compiler_params=pltpu.CompilerParams(has_side_effects=True))(v)


def _sum_devices(g):
    n, R, C = g.shape

    def body(g_ref, out_ref):
        s = g_ref[0]
        for d in range(1, n):
            s = s + g_ref[d]
        out_ref[...] = s

    return _pallas(body, name="sum_devices", in_specs=[VMEM_SPEC], out_specs=VMEM_SPEC,
                   out_shape=jax.ShapeDtypeStruct((R, C), g.dtype))(g)


def _pad_heads(a, width):
    lead = a.shape[:-1]
    a = a.reshape(lead + (N_HEADS, width))
    a = jnp.pad(a, [(0, 0)] * len(lead) + [(0, 0), (0, HEAD_PAD - width)])
    return a.reshape(lead + (HP,))


def _unpad_heads(a, width):
    lead = a.shape[:-1]
    return a.reshape(lead + (N_HEADS, HEAD_PAD))[..., :width].reshape(lead + (N_HEADS * width,))


def _pre_weights(slabs, l):
    cols = lambda name: slabs[name][:, l].transpose(1, 0, 2).reshape(slabs[name].shape[2], -1)
    w_in = cols("w_in")
    kpe = jnp.pad(w_in[:, 640:672], ((0, 0), (ROPE_LO, HEAD_PAD - ROPE_HI)))
    w_ukv = cols("w_ukv").reshape(KV_LORA, N_HEADS, QK_NOPE + V_HEAD)
    return dict(
        w_in=jnp.concatenate([w_in[:, :640], kpe, w_in[:, 672:]], axis=1),
        w_uq=_pad_heads(cols("w_uq"), QK_HEAD),
        w_ukv=jnp.concatenate([_pad_heads(w_ukv[..., :QK_NOPE].reshape(KV_LORA, -1), QK_NOPE),
                               _pad_heads(w_ukv[..., QK_NOPE:].reshape(KV_LORA, -1), V_HEAD)], axis=1),
    )


def _post_weights(slabs, l):
    cols = lambda name: slabs[name][:, l].transpose(1, 0, 2).reshape(slabs[name].shape[2], -1)
    w_o = slabs["w_o"][l]
    return dict(
        w_o=jnp.concatenate([_pad_heads(w_o[:ATTN_WIDTH].T, V_HEAD).T, w_o[ATTN_WIDTH:]], axis=0),
        w_up=cols("w_up"), w_down=(slabs["w_down"], l), w_ple_gate=(slabs["w_ple_gate"], l), w_ple=cols("w_ple"),
    )


def _gains(small, l):
    row = lambda name: small[name][l].reshape(1, -1)
    headrow = lambda a, b: jnp.pad(jnp.concatenate([small[a][l], small[b][l]]), (0, HEAD_PAD - QK_HEAD)).reshape(1, HEAD_PAD)
    return dict(
        g_mix=row("g_mix"), g_q_lat=row("g_q_lat"), g_kv_lat=row("g_kv_lat"), g_mlp=row("g_mlp"), g_ple=row("g_ple"),
        g_out_conv=row("g_out_conv"), g_out_attn=_pad_heads(small["g_out_attn"][l], V_HEAD).reshape(1, HP),
        gq=headrow("g_qn_nope", "g_qn_rope"), gk=headrow("g_kn_nope", "g_kn_rope"),
        cw8=jnp.pad(small["conv_w"][l], ((0, SUBLANES - CONV_TAPS), (0, 0))),
        head_mats=_head_matrices(),
    )


def _w_o_shards(dw_o):
    return jnp.concatenate([_unpad_heads(dw_o[:HP].T, V_HEAD).T, dw_o[HP:]], axis=0).reshape(N_CHIPS, -1, D_MODEL)


def _unpad_grads(gp):
    dw_in = gp["w_in"]
    dw_ukv = gp["w_ukv"]
    k_part = dw_ukv[:, :HP].reshape(KV_LORA, N_HEADS, HEAD_PAD)[..., :QK_NOPE]
    v_part = dw_ukv[:, HP:].reshape(KV_LORA, N_HEADS, HEAD_PAD)[..., :V_HEAD]
    first = lambda name: gp[name][0]
    col_shards = lambda a: a.reshape(a.shape[0], N_CHIPS, -1).transpose(1, 0, 2)
    return dict(
        w_in=col_shards(jnp.concatenate([dw_in[:, :640], dw_in[:, Z_KPE + ROPE_LO:Z_KPE + ROPE_HI], dw_in[:, Z_GB:]], axis=1)),
        w_uq=col_shards(_unpad_heads(gp["w_uq"], QK_HEAD)),
        w_ukv=col_shards(jnp.concatenate([k_part, v_part], axis=-1).reshape(KV_LORA, -1)),
        w_o=gp["w_o"], w_up=gp["w_up"], w_down=gp["w_down"], w_ple_gate=gp["w_ple_gate"], w_ple=gp["w_ple"],
        g_mix=first("g_mix"), g_q_lat=first("g_q_lat"), g_kv_lat=first("g_kv_lat"), g_mlp=first("g_mlp"),
        g_ple=first("g_ple"), g_out_conv=first("g_out_conv"), g_out_attn=_unpad_heads(first("g_out_attn"), V_HEAD),
        g_qn_nope=gp["gq"][0, :QK_NOPE], g_qn_rope=gp["gq"][0, QK_NOPE:QK_HEAD],
        g_kn_nope=gp["gk"][0, :QK_NOPE], g_kn_rope=gp["gk"][0, QK_NOPE:QK_HEAD],
        conv_w=gp["cw8"][:CONV_TAPS],
    )


def _rope_tables(positions):
    inv_freq = 1.0 / (ROPE_THETA ** (jnp.arange(0, QK_ROPE, 2, dtype=F32) / QK_ROPE))
    ang = positions.astype(F32)[:, None] * inv_freq
    cos, sin = jnp.cos(ang), jnp.sin(ang)
    pad = lambda t, v: jnp.pad(jnp.concatenate([t, t], axis=1), ((0, 0), (ROPE_LO, HEAD_PAD - ROPE_HI)), constant_values=v)
    return pad(cos, 1.0), pad(sin, 0.0)


def _layer_fwd(x0, h, p_l, W, cs, sn, attend, g_next):
    z = _mm(h, W["w_in"], name="mm_in")
    qln, kvln = _lat_fwd(z, W["g_q_lat"], W["g_kv_lat"])
    q_raw = _mm(qln, W["w_uq"], name="mm_uq")
    kv_raw = _mm(kvln, W["w_ukv"], name="mm_ukv")
    qf, kf, vb = _qk_fwd(q_raw, kv_raw, z, cs, sn, W["gq"], W["gk"], *W["head_mats"])
    o, lse = attend(qf, kf, vb)
    conv = _conv_fwd(z, W["cw8"])
    mixed = _mix_fwd(o, conv, W["g_out_attn"], W["g_out_conv"])
    x1, h2 = _mm(mixed, W["w_o"], res=x0, gain=W["g_mlp"], epi="rms", name="mm_o")
    a, f = W["host_mm_up"](lambda rider: _mm(h2, W["w_up"], epi="relu2", out_dtype=BF16, rider=rider, name="mm_up"))
    x2, h3 = _mm(f, W["w_down"], res=x1, gain=W["g_ple"], epi="rms", name="mm_down")
    gl = _mm(h3, W["w_ple_gate"], name="mm_ple_gate")
    pe = _mm(p_l, W["w_ple"], name="mm_ple")
    x3, h_next = _ple_fwd(x2, gl, pe, g_next)
    saved = dict(x0=x0, h=h, z=z, qln=qln, kvln=kvln, q_raw=q_raw, kv_raw=kv_raw, qf=qf, kf=kf, vb=vb, o=o, lse=lse,
                 conv=conv, mixed=mixed, x1=x1, h2=h2, a=a, f=f, x2=x2, h3=h3, gl=gl, pe=pe)
    return x3, h_next, saved


def _layer_bwd(dx3, p_l, W, cs, sn, sv, attend_bwd):
    g = {}
    dpe, dgl = _ple_bwd(dx3, sv["gl"], sv["pe"])
    g["w_ple"] = _mm(p_l, dpe, mode="tn", out_dtype=BF16, shard_out=1, name="mm_dw_ple")
    g["w_ple_gate"] = _mm(sv["h3"], dgl, mode="tn", out_dtype=BF16, shard_out=0, name="mm_dw_ple_gate")
    dx2, dx2b, g["g_ple"] = _mm(dgl, W["w_ple_gate"], mode="nt", res=dx3, aux=sv["x2"], gain=W["g_ple"], epi="rms_bwd",
                                name="mm_dh3")
    da = _mm(dx2b, W["w_down"], mode="nt", aux=sv["a"], epi="drelu2", out_dtype=BF16, name="mm_da")
    g["w_down"] = _mm(sv["f"], dx2b, mode="tn", out_dtype=BF16, shard_out=0, name="mm_dw_down")
    g["w_up"] = _mm(sv["h2"], da, mode="tn", out_dtype=BF16, shard_out=1, name="mm_dw_up")
    dx1, dx1b, g["g_mlp"] = _mm(da, W["w_up"], mode="nt", res=dx2, aux=sv["x1"], gain=W["g_mlp"], epi="rms_bwd",
                                name="mm_dh2")
    dmixed = _mm(dx1b, W["w_o"], mode="nt", name="mm_dmixed")
    g["w_o"] = _mm(sv["mixed"], dx1b, mode="tn", out_dtype=BF16, name="mm_dw_o")
    do, dconv, g["g_out_attn"], g["g_out_conv"] = _mix_bwd(dmixed, sv["o"], sv["conv"], W["g_out_attn"], W["g_out_conv"])
    dgb, dgc, dxin, g["cw8"] = _conv_bwd(dconv, sv["z"], W["cw8"])
    dqf, dkf, dv = attend_bwd(g, sv["qf"], sv["kf"], sv["vb"], sv["o"], do, sv["lse"])
    dq_raw, dkv_raw, dkpe, g["gq"], g["gk"] = W["host_qk_bwd"](functools.partial(
        _qk_bwd, dqf, dkf, dv, sv["q_raw"], sv["kv_raw"], sv["z"], cs, sn, W["gq"], W["gk"], *W["head_mats"]))
    g["w_uq"] = _mm(sv["qln"], dq_raw, mode="tn", out_dtype=BF16, name="mm_dw_uq")
    dqln = _mm(dq_raw, W["w_uq"], mode="nt", name="mm_dqln")
    g["w_ukv"] = _mm(sv["kvln"], dkv_raw, mode="tn", out_dtype=BF16, name="mm_dw_ukv")
    dkvln = _mm(dkv_raw, W["w_ukv"], mode="nt", name="mm_dkvln")
    dlat, g["g_q_lat"], g["g_kv_lat"] = _lat_bwd(dqln, dkvln, sv["z"], W["g_q_lat"], W["g_kv_lat"])
    dz = jnp.concatenate([dlat, dkpe, dgb, dgc, dxin], axis=1)
    g["w_in"] = _mm(sv["h"], dz, mode="tn", out_dtype=BF16, name="mm_dw_in")
    dx0, _, g["g_mix"] = _mm(dz, W["w_in"], mode="nt", res=dx1, aux=sv["x0"], gain=W["g_mix"], epi="rms_bwd", name="mm_dh")
    return dx0, g


def _local_step(x, p, positions, target, slabs, small):
    depth = p.shape[0]
    cs, sn = _rope_tables(positions)
    slabs = dict(slabs)

    def gather(items, host):
        touched = [n for n in PRE + POST if any(n == name for name, _ in items)]
        rider = _allgather_rider([slabs[n] for n in touched], [(touched.index(name), l) for name, l in items])
        if host is None:
            out, new = (), rider.run("allgather_first")
        else:
            *out, new = host(rider)
        slabs.update(zip(touched, new))
        return out

    gather([(name, 0) for name in PRE], None)
    Ws, saved = [], []
    gains = [_gains(small, l) for l in range(depth)]
    h = _rms_fwd(x, gains[0]["g_mix"], "rms_mix")
    for l in range(depth):
        W = dict(gains[l], **_pre_weights(slabs, l))

        def attend(qf, kf, vb, W=W, l=l):
            o, lse = gather([(name, l) for name in POST], functools.partial(_attn_fwd, qf, kf, vb))
            W.update(_post_weights(slabs, l))
            return o, lse

        if l + 1 < depth:
            W["host_mm_up"] = lambda run, l=l: gather([(name, l + 1) for name in PRE], run)
        else:
            W["host_mm_up"] = lambda run: run(None)

        g_next = gains[l + 1]["g_mix"] if l + 1 < depth else jnp.ones_like(gains[l]["g_mix"])
        x, h, sv = _layer_fwd(x, h, p[l], W, cs, sn, attend, g_next)
        Ws.append(W)
        saved.append(sv)
    dx, sq = _loss_grad(x, target)

    landing = {name: lax.empty((depth, N_DEV - 1, _shard_dims(slabs[name])[0] // 2, _shard_dims(slabs[name])[1]), BF16)
               for name in PRE + POST}

    def exchange(sends, host):
        touched = [n for n in PRE + POST if any(n == name for name, _, _ in sends)]
        rider = _exchange_rider([g for _, _, g in sends], [landing[n] for n in touched],
                                [(i, touched.index(name), l) for i, (name, l, _) in enumerate(sends)])
        if host is None:
            out, new = (), rider.run("grad_exchange_last")
        else:
            *out, new = host(rider)
        landing.update(zip(touched, new))
        return out

    grads = [None] * depth
    for l in reversed(range(depth)):
        W = dict(Ws[l], w_down=(slabs["w_down"], l), w_ple_gate=(slabs["w_ple_gate"], l))

        def attend_bwd(g, qf, kf, vb, o, do, lse, l=l):
            g["w_o"] = _w_o_shards(g["w_o"])
            return exchange([(name, l, g[name]) for name in POST], functools.partial(_attn_bwd, qf, kf, vb, o, do, lse))

        if l + 1 < depth:
            W["host_qk_bwd"] = lambda run, l=l: exchange([(name, l + 1, grads[l + 1][name]) for name in PRE], run)
        else:
            W["host_qk_bwd"] = lambda run: run(Rider([], [], 0, []))[:-1]

        dx, gp = _layer_bwd(dx, p[l], W, cs, sn, saved[l], attend_bwd)
        grads[l] = _unpad_grads(gp)
    exchange([(name, 0, grads[0][name]) for name in PRE], None)
    return sq, dx, grads, landing


def _pack_rows(vals, entries):
    depth = vals[entries[0][0]].shape[0]
    return jnp.concatenate([jnp.pad(vals[name].reshape(depth, -1), ((0, 0), (0, _pad128(n) - n))) for name, n in entries], axis=1)


def _unpack_rows(packed, entries):
    out, off = {}, 0
    for name, n in entries:
        out[name] = packed[:, off:off + n]
        off += _pad128(n)
    return out


def kernel(x, p, positions, g_mix, w_in, g_q_lat, w_uq, g_kv_lat, w_ukv, g_qn_nope, g_qn_rope, g_kn_nope, g_kn_rope, conv_w, g_out_attn, g_out_conv, w_o, g_mlp, w_up, w_down, g_ple, w_ple_gate, w_ple, loss_target, m_g_mix, m_w_in, m_g_q_lat, m_w_uq, m_g_kv_lat, m_w_ukv, m_g_qn_nope, m_g_qn_rope, m_g_kn_nope, m_g_kn_rope, m_conv_w, m_g_out_attn, m_g_out_conv, m_w_o, m_g_mlp, m_w_up, m_w_down, m_g_ple, m_w_ple_gate, m_w_ple, v_g_mix, v_w_in, v_g_q_lat, v_w_uq, v_g_kv_lat, v_w_ukv, v_g_qn_nope, v_g_qn_rope, v_g_kn_nope, v_g_kn_rope, v_conv_w, v_g_out_attn, v_g_out_conv, v_w_o, v_g_mlp, v_w_up, v_w_down, v_g_ple, v_w_ple_gate, v_w_ple):
    w = dict(g_mix=g_mix, w_in=w_in, g_q_lat=g_q_lat, w_uq=w_uq, g_kv_lat=g_kv_lat, w_ukv=w_ukv, g_qn_nope=g_qn_nope,
             g_qn_rope=g_qn_rope, g_kn_nope=g_kn_nope, g_kn_rope=g_kn_rope, conv_w=conv_w, g_out_attn=g_out_attn,
             g_out_conv=g_out_conv, w_o=w_o, g_mlp=g_mlp, w_up=w_up, w_down=w_down, g_ple=g_ple, w_ple_gate=w_ple_gate,
             w_ple=w_ple)
    m = dict(g_mix=m_g_mix, w_in=m_w_in, g_q_lat=m_g_q_lat, w_uq=m_w_uq, g_kv_lat=m_g_kv_lat, w_ukv=m_w_ukv,
             g_qn_nope=m_g_qn_nope, g_qn_rope=m_g_qn_rope, g_kn_nope=m_g_kn_nope, g_kn_rope=m_g_kn_rope, conv_w=m_conv_w,
             g_out_attn=m_g_out_attn, g_out_conv=m_g_out_conv, w_o=m_w_o, g_mlp=m_g_mlp, w_up=m_w_up, w_down=m_w_down,
             g_ple=m_g_ple, w_ple_gate=m_w_ple_gate, w_ple=m_w_ple)
    v = dict(g_mix=v_g_mix, w_in=v_w_in, g_q_lat=v_g_q_lat, w_uq=v_w_uq, g_kv_lat=v_g_kv_lat, w_ukv=v_w_ukv,
             g_qn_nope=v_g_qn_nope, g_qn_rope=v_g_qn_rope, g_kn_nope=v_g_kn_nope, g_kn_rope=v_g_kn_rope, conv_w=v_conv_w,
             g_out_attn=v_g_out_attn, g_out_conv=v_g_out_conv, w_o=v_w_o, g_mlp=v_g_mlp, w_up=v_w_up, w_down=v_w_down,
             g_ple=v_g_ple, w_ple_gate=v_w_ple_gate, w_ple=v_w_ple)
    depth = p.shape[0]
    ax, ay, ac = _place()
    chip = 2 * ax + ay
    c_arr = jnp.reshape(ac, (1,)).astype(jnp.int32)
    chip_arr = jnp.reshape(chip, (1,)).astype(jnp.int32)
    conv_shard =("conv_w", CONV_TAPS * CONV_WIDTH // N_CHIPS)
    conv_full = ("conv_w", CONV_TAPS * CONV_WIDTH)

    names = PRE + POST
    slabs = {name: _shard_slab(w[name], chip_arr, "shard_slab_" + name, name in ROW_SHARDED) for name in names}
    conv_rows = -(-depth * CONV_TAPS // SUBLANES) * SUBLANES
    conv_all = _allgather_small(jnp.pad(conv_w.reshape(depth * CONV_TAPS, LANES), ((0, conv_rows - depth * CONV_TAPS), (0, 0))),
                                "allgather_conv_w")
    conv_cat = jnp.concatenate([conv_all[2 * j, :depth * CONV_TAPS] for j in range(N_CHIPS)], axis=1)
    small = {name: w[name] for name, _ in SMALL}
    small["conv_w"] = conv_cat.reshape(depth, CONV_TAPS, CONV_WIDTH)

    sq, grad_x, grads, landing = _local_step(x[0], p[:, 0], positions[0], loss_target[0], slabs, small)

    halves = [_reduce_partials(jnp.stack([grads[l][name] for l in range(depth)]), landing[name], chip_arr, c_arr,
                               "grad_reduce_" + name) for name in names]
    g_out = dict(zip(names, _join_halves(halves)))

    reduced = SMALL + (conv_full, ("loss", LANES))
    stacked = {name: jnp.stack([grads[l][name] for l in range(depth)]) for name, _ in SMALL + (conv_full,)}
    stacked["loss"] = jnp.broadcast_to(sq[:1] * (0.5 / D_MODEL), (depth, LANES))
    g_small = _unpack_rows(_sum_devices(_allgather_small(_pack_rows(stacked, reduced), "allgather_small_grads")), reduced)
    loss = g_small["loss"][0, 0]
    for name, n in SMALL:
        g_out[name] = g_small[name]
    g_conv = g_small["conv_w"].reshape(depth, CONV_TAPS, CONV_WIDTH)
    g_out["conv_w"] = lax.dynamic_slice_in_dim(g_conv, chip * LANES, LANES, axis=2)

    delta, new_m, new_v = {}, {}, {}
    for name in names:
        view = (lambda t: jnp.swapaxes(t, 1, 2)) if w[name].shape[2] % LANES else (lambda t: t)
        d, m2, v2 = _adamw(view(w[name]), view(g_out[name]), view(m[name]), view(v[name]), "adamw_" + name)
        delta[name], new_m[name], new_v[name] = view(d), view(m2), view(v2)
    local = SMALL + (conv_shard,)
    pack = lambda vals: _pack_rows(vals, local)[None]
    d, m2, v2 = _adamw(pack(w), pack(g_out), pack(m), pack(v), "adamw_small")
    for res, packed_res in ((delta, d), (new_m, m2), (new_v, v2)):
        un = _unpack_rows(packed_res[0], local)
        for name, _ in local:
            res[name] = un[name].reshape(w[name].shape)

    return (loss, grad_x[None], *[g_out[n] for n in WEIGHT_ORDER], *[delta[n] for n in WEIGHT_ORDER],
            *[new_m[n] for n in WEIGHT_ORDER], *[new_v[n] for n in WEIGHT_ORDER])
```

```python
import functools

import jax
import jax.numpy as jnp
from jax import lax
from jax.experimental import pallas as pl
from jax.experimental.pallas import tpu as pltpu

F32 = jnp.float32
BF16 = jnp.bfloat16
MESH = pl.DeviceIdType.MESH

D_MODEL = 1024
N_HEADS = 8
QK_NOPE = 64
QK_ROPE = 32
QK_HEAD = QK_NOPE + QK_ROPE
V_HEAD = 64
Q_LORA = 384
KV_LORA = 256
ATTN_WIDTH = N_HEADS * V_HEAD
CONV_WIDTH = 512
CONV_TAPS = 3
D_FF = 4096
PLE_DIM = 256
ROPE_THETA = 10000.0
EPS = 1e-6
ATT_SCALE = QK_HEAD ** -0.5
LOG2E = 1.4426950408889634
ATT_SCALE_LOG2 = ATT_SCALE * LOG2E

ADAM_LR = 0.001
ADAM_B1 = 0.9
ADAM_B2 = 0.999
ADAM_EPS = 1e-08
ADAM_WD = 0.01
ADAM_STEP = 10

LANES = 128
SUBLANES = 8
HEAD_PAD = LANES
HP = N_HEADS * HEAD_PAD
ROPE_LO = QK_NOPE
ROPE_MID = QK_NOPE + QK_ROPE // 2
ROPE_HI = QK_NOPE + QK_ROPE
VMEM_LIMIT = 56 * 1024 * 1024

Z_Q, Z_KV, Z_KPE, Z_GB, Z_GC, Z_XIN = 0, 384, 640, 768, 1280, 1792
Z_COLS = 2304
Z_LAT = Z_KPE

ROW_TILE = 512
ATT_TILE = 256
ATT_CHAINS_FWD, ATT_CHAINS_BWD = 8, 4
MM_TM, MM_TN, MM_TK = 1024, 1024, 4096
MM_TM_MIN, MM_TK_MIN = 256, 512
MM_VMEM_BUDGET = 40 * 1024 * 1024

N_CHIPS = 4
N_DEV = 8

PRE = ("w_in", "w_uq", "w_ukv")
POST = ("w_o", "w_up", "w_down", "w_ple_gate", "w_ple")
ROW_SHARDED = ("w_o", "w_down", "w_ple_gate")
SMALL = (
    ("g_mix", 1024), ("g_q_lat", 384), ("g_kv_lat", 256), ("g_qn_nope", 64), ("g_qn_rope", 32), ("g_kn_nope", 64),
    ("g_kn_rope", 32), ("g_out_attn", 512), ("g_out_conv", 512), ("g_mlp", 1024), ("g_ple", 1024),
)
WEIGHT_ORDER = ("g_mix", "w_in", "g_q_lat", "w_uq", "g_kv_lat", "w_ukv", "g_qn_nope", "g_qn_rope", "g_kn_nope",
                "g_kn_rope", "conv_w", "g_out_attn", "g_out_conv", "w_o", "g_mlp", "w_up", "w_down", "g_ple",
                "w_ple_gate", "w_ple")


def _pallas(body, **kw):
    return pl.pallas_call(body, **kw)


def _cparams(*sem):
    return pltpu.CompilerParams(dimension_semantics=sem, vmem_limit_bytes=VMEM_LIMIT)


def _tile(dim, pref, unit=LANES):
    if dim <= pref:
        return dim
    t = (pref // unit) * unit
    while t > unit and dim % t:
        t -= unit
    assert dim % t == 0, (dim, pref)
    return t


def _pad128(n):
    return -(-n // LANES) * LANES


_DIMS = {"nn": (((1,), (0,)), ((), ())), "nt": (((1,), (1,)), ((), ())), "tn": (((0,), (0,)), ((), ()))}


def _mm(a, b, *, mode="nn", res=None, aux=None, gain=None, epi=None, out_dtype=F32, shard_out=None, rider=None, name):
    whole_rows = epi in ("rms", "rms_bwd")
    b, layer = b if isinstance(b, tuple) else (b, None)
    b_shape = b.shape if layer is None else b.shape[1:]
    if mode == "nn":
        (M, K), (K2, N) = a.shape, b_shape
    elif mode == "nt":
        (M, K), (N, K2) = a.shape, b_shape
    else:
        (K, M), (K2, N) = a.shape, b_shape
    assert K == K2, (a.shape, b_shape, mode)
    n_lim = k_lim = None
    m_lim = M // N_CHIPS if shard_out == 0 else None
    if shard_out == 1:
        n_lim = N // N_CHIPS
    extra = [t for t in (res, aux) if t is not None]
    out_bytes = 6 if epi in ("relu2", "rms", "rms_bwd") else jnp.dtype(out_dtype).itemsize
    if whole_rows:
        assert (n_lim or N) == N and N <= MM_TN and shard_out is None and gain is not None
        m_lim = min(M, MM_TM // 2)

    def footprint(tm, tn, tk):
        blocks = tm * tk * a.dtype.itemsize + tk * tn * 2 + tm * tn * (out_bytes + 4 * len(extra))
        return 2 * blocks + (tm * tn * 4 if tk < K else 0)

    tm, tn, tk = _tile(m_lim or M, MM_TM), _tile(n_lim or N, MM_TN), _tile(k_lim or K, MM_TK)
    while footprint(tm, tn, tk) > MM_VMEM_BUDGET and tk > MM_TK_MIN:
        tk = _tile(K, tk // 2)
    while footprint(tm, tn, tk) > MM_VMEM_BUDGET and tm > MM_TM_MIN:
        tm = _tile(M, tm // 2)
    nk = K // tk
    a_spec = pl.BlockSpec((tk, tm), lambda i, j, k: (k, i)) if mode == "tn" else pl.BlockSpec((tm, tk), lambda i, j, k: (i, k))
    if mode == "nt":
        b_block, b_rc = (tn, tk), (lambda i, j, k: (j, k))
    else:
        b_block, b_rc = (tk, tn), (lambda i, j, k: (k, j))
    if layer is None:
        b_spec = pl.BlockSpec(b_block, b_rc)
    else:
        b_spec = pl.BlockSpec((None,) + b_block, lambda i, j, k: (layer,) + b_rc(i, j, k))
    mn_spec = pl.BlockSpec((tm, tn), lambda i, j, k: (i, j))
    dims = _DIMS[mode]
    n_out = {"relu2": 2, "rms": 2, "rms_bwd": 3}.get(epi, 1)

    n_in = 2 + len(extra) + whole_rows
    n_grid = (M // tm, N // tn, nk)

    def body(*refs):
        a_ref, b_ref = refs[0], refs[1]
        extra_refs = refs[2:2 + len(extra)]
        gain_ref = refs[2 + len(extra)] if whole_rows else None
        if rider is None:
            tail = refs[n_in:]
        else:
            tail, ride = rider.split(refs[n_in:], n_out, int(nk > 1))
            ride((pl.program_id(0) * n_grid[1] + pl.program_id(1)) * nk + pl.program_id(2), n_grid[0] * n_grid[1] * nk)
        out_refs = tail[:n_out]
        prod = lax.dot_general(a_ref[...].astype(BF16), b_ref[...].astype(BF16), dims, preferred_element_type=F32)

        def finish(r):
            if epi == "rms_bwd":
                dx, dgc = _norm_bwd(extra_refs[1][...], r, gain_ref[...], N)
                dx = dx + extra_refs[0][...]
                out_refs[0][...] = dx
                out_refs[1][...] = dx.astype(BF16)
                _accumulate(out_refs[2], dgc)
                return
            if res is not None:
                r = r + extra_refs[0][...]
            if epi == "rms":
                out_refs[0][...] = r
                out_refs[1][...] = (r * _rinv(r, N) * gain_ref[...]).astype(BF16)
            elif epi == "relu2":
                out_refs[0][...] = r.astype(out_dtype)
                t = jnp.maximum(r, 0.0)
                out_refs[1][...] = (t * t).astype(BF16)
            elif epi == "drelu2":
                out_refs[0][...] = (r * (2.0 * jnp.maximum(extra_refs[-1][...].astype(F32), 0.0))).astype(out_dtype)
            else:
                out_refs[0][...] = r.astype(out_dtype)

        if nk == 1:
            finish(prod)
        else:
            acc = tail[n_out]
            k = pl.program_id(2)

            @pl.when(k == 0)
            def _():
                acc[...] = prod

            @pl.when(k > 0)
            def _():
                acc[...] += prod

            @pl.when(k == nk - 1)
            def _():
                finish(acc[...])

    if epi in ("relu2", "rms", "rms_bwd"):
        out_shape = [jax.ShapeDtypeStruct((M, N), out_dtype if epi == "relu2" else F32), jax.ShapeDtypeStruct((M, N), BF16)]
        out_specs = [mn_spec, mn_spec]
        if epi == "rms_bwd":
            out_shape.append(jax.ShapeDtypeStruct((SUBLANES, N), F32))
            out_specs.append(pl.BlockSpec((SUBLANES, tn), lambda i, j, k: (0, j)))
    elif shard_out == 0:
        per = (M // N_CHIPS) // tm
        out_shape = jax.ShapeDtypeStruct((N_CHIPS, M // N_CHIPS, N), out_dtype)
        out_specs = pl.BlockSpec((None, tm, tn), lambda i, j, k: (i // per, i % per, j))
    elif shard_out == 1:
        per = (N // N_CHIPS) // tn
        out_shape = jax.ShapeDtypeStruct((N_CHIPS, M, N // N_CHIPS), out_dtype)
        out_specs = pl.BlockSpec((None, tm, tn), lambda i, j, k: (j // per, i, j % per))
    else:
        out_shape = jax.ShapeDtypeStruct((M, N), out_dtype)
        out_specs = mn_spec
    gains = [gain] if whole_rows else []
    in_specs = [a_spec, b_spec] + [mn_spec] * len(extra) + [pl.BlockSpec((1, tn), lambda i, j, k: (0, j))] * len(gains)
    scratch = [pltpu.VMEM((tm, tn), F32)] if nk > 1 else []
    ordered = epi == "rms_bwd" or rider is not None
    params = _cparams(*(["arbitrary"] * 3 if ordered else ["parallel", "parallel", "arbitrary"]))
    if rider is None:
        return _pallas(body, name=name, grid=n_grid, in_specs=in_specs, out_specs=out_specs, out_shape=out_shape,
                       scratch_shapes=scratch, compiler_params=params)(a, b, *extra, *gains)
    as_list = lambda t: list(t) if isinstance(t, (list, tuple)) else [t]
    outs = _pallas(body, name=name, grid=n_grid, in_specs=in_specs + rider.in_specs(),
                   out_specs=as_list(out_specs) + rider.out_specs(), out_shape=as_list(out_shape) + rider.out_shape(),
                   scratch_shapes=scratch + rider.scratch(), input_output_aliases=rider.aliases(n_in, n_out),
                   compiler_params=params)(a, b, *extra, *gains, *rider.args())
    return (*outs[:n_out], outs[n_out:])


def _rows(ts, d, col=0):
    return pl.BlockSpec((ts, d), lambda i: (i, col))


def _gain(d):
    return pl.BlockSpec((1, d), lambda i: (0, 0))


def _accum(d):
    return pl.BlockSpec((SUBLANES, d), lambda i: (0, 0))


def _accumulate(ref, val):
    i = pl.program_id(0)

    @pl.when(i == 0)
    def _():
        ref[...] = jnp.zeros_like(ref)

    ref[...] += jnp.broadcast_to(jnp.sum(val, axis=0, keepdims=True), ref.shape)


def _rinv(x, n):
    return lax.rsqrt(jnp.sum(x * x, axis=-1, keepdims=True) / n + EPS)


def _norm_bwd(x, dy, g, n):
    r = _rinv(x, n)
    xhat = x * r
    dyg = dy * g
    dx = r * (dyg - xhat * (jnp.sum(dyg * xhat, axis=-1, keepdims=True) / n))
    return dx, dy * xhat


def _rms_fwd(x, g, name):
    S, D = x.shape
    ts = _tile(S, ROW_TILE, SUBLANES)

    def body(x_ref, g_ref, h_ref):
        xv = x_ref[...]
        h_ref[...] = (xv * _rinv(xv, D) * g_ref[...]).astype(BF16)

    return _pallas(body, name=name, grid=(S // ts,), in_specs=[_rows(ts, D), _gain(D)], out_specs=_rows(ts, D),
                   out_shape=jax.ShapeDtypeStruct((S, D), BF16), compiler_params=_cparams("parallel"))(x, g)


def _lat_fwd(z, gq, gkv):
    S = z.shape[0]
    ts = _tile(S, ROW_TILE, SUBLANES)

    def body(z_ref, gq_ref, gkv_ref, q_ref, kv_ref):
        zq = z_ref[:, Z_Q:Z_KV]
        zkv = z_ref[:, Z_KV:Z_KPE]
        q_ref[...] = (zq * _rinv(zq, Q_LORA) * gq_ref[...]).astype(BF16)
        kv_ref[...] = (zkv * _rinv(zkv, KV_LORA) * gkv_ref[...]).astype(BF16)

    return _pallas(body, name="lat_fwd", grid=(S // ts,), in_specs=[_rows(ts, Z_LAT), _gain(Q_LORA), _gain(KV_LORA)],
                   out_specs=[_rows(ts, Q_LORA), _rows(ts, KV_LORA)],
                   out_shape=[jax.ShapeDtypeStruct((S, Q_LORA), BF16), jax.ShapeDtypeStruct((S, KV_LORA), BF16)],
                   compiler_params=_cparams("parallel"))(z, gq, gkv)


def _lat_bwd(dq, dkv, z, gq, gkv):
    S = z.shape[0]
    ts = _tile(S, ROW_TILE, SUBLANES)

    def body(dq_ref, dkv_ref, z_ref, gq_ref, gkv_ref, dlat_ref, dgq_ref, dgkv_ref):
        dxq, cq = _norm_bwd(z_ref[:, Z_Q:Z_KV], dq_ref[...], gq_ref[...], Q_LORA)
        dxkv, ckv = _norm_bwd(z_ref[:, Z_KV:Z_KPE], dkv_ref[...], gkv_ref[...], KV_LORA)
        dlat_ref[:, Z_Q:Z_KV] = dxq.astype(BF16)
        dlat_ref[:, Z_KV:Z_KPE] = dxkv.astype(BF16)
        _accumulate(dgq_ref, cq)
        _accumulate(dgkv_ref, ckv)

    return _pallas(body, name="lat_bwd", grid=(S // ts,),
                   in_specs=[_rows(ts, Q_LORA), _rows(ts, KV_LORA), _rows(ts, Z_LAT), _gain(Q_LORA), _gain(KV_LORA)],
                   out_specs=[_rows(ts, Z_LAT), _accum(Q_LORA), _accum(KV_LORA)],
                   out_shape=[jax.ShapeDtypeStruct((S, Z_LAT), BF16), jax.ShapeDtypeStruct((SUBLANES, Q_LORA), F32),
                              jax.ShapeDtypeStruct((SUBLANES, KV_LORA), F32)],
                   compiler_params=_cparams("arbitrary"))(dq, dkv, z, gq, gkv)


def _head_masks():
    lane = lax.broadcasted_iota(jnp.int32, (1, HEAD_PAD), 1)
    return lane < ROPE_LO, (lane >= ROPE_LO) & (lane < ROPE_HI)


def _head_matrices():
    k = jnp.arange(HEAD_PAD)[:, None]
    j = jnp.arange(HEAD_PAD)[None, :]
    nope = (k < ROPE_LO) & (j < ROPE_LO)
    rope = (k >= ROPE_LO) & (k < ROPE_HI) & (j >= ROPE_LO) & (j < ROPE_HI)
    seg = jnp.where(nope, 1.0 / QK_NOPE, jnp.where(rope, 1.0 / QK_ROPE, 0.0))
    half = QK_ROPE // 2
    rot = jnp.where((j >= ROPE_LO) & (j < ROPE_MID) & (k == j + half), -1.0,
                    jnp.where((j >= ROPE_MID) & (j < ROPE_HI) & (k == j - half), 1.0, 0.0))
    return seg.astype(BF16), rot.astype(BF16)


_HEAD_MAT = pl.BlockSpec((HEAD_PAD, HEAD_PAD), lambda i: (0, 0))


def _split_dot(t, mat):
    hi = t.astype(BF16)
    lo = (t - hi.astype(F32)).astype(BF16)
    return jnp.dot(hi, mat, preferred_element_type=F32) + jnp.dot(lo, mat, preferred_element_type=F32)


def _qk_fwd(q_raw, kv_raw, z, cs, sn, gq, gk, seg, rot):
    S = q_raw.shape[0]
    ts = _tile(S, ROW_TILE, SUBLANES)

    def body(q_ref, k_ref, v_ref, kpe_ref, cs_ref, sn_ref, gq_ref, gk_ref, seg_ref, rot_ref, qf_ref, kf_ref, vb_ref):
        cos, sin, gqv, gkv = cs_ref[...], sn_ref[...], gq_ref[...], gk_ref[...]
        seg_m, rot_m = seg_ref[...], rot_ref[...]

        def norm(x, g):
            return x * lax.rsqrt(_split_dot(x * x, seg_m) + EPS) * g

        def rope(y):
            return y * cos + _split_dot(y, rot_m) * sin

        kr = rope(norm(kpe_ref[...], gkv))
        lane = lax.broadcasted_iota(jnp.int32, (1, HEAD_PAD), 1)
        for h in range(N_HEADS):
            sl = slice(h * HEAD_PAD, (h + 1) * HEAD_PAD)
            qf_ref[:, sl] = rope(norm(q_ref[:, sl], gqv)).astype(BF16)
            kf_ref[:, sl] = (norm(k_ref[:, sl], gkv) + kr).astype(BF16)
            vb_ref[:, sl] = jnp.where(lane == V_HEAD, 1.0, v_ref[:, sl]).astype(BF16)

    hd = jax.ShapeDtypeStruct((S, HP), BF16)
    return _pallas(body, name="qk_fwd", grid=(S // ts,),
                   in_specs=[_rows(ts, HP), _rows(ts, HP, 0), _rows(ts, HP, 1), _rows(ts, HEAD_PAD, Z_KPE // HEAD_PAD),
                             _rows(ts, HEAD_PAD), _rows(ts, HEAD_PAD), _gain(HEAD_PAD), _gain(HEAD_PAD), _HEAD_MAT, _HEAD_MAT],
                   out_specs=[_rows(ts, HP)] * 3, out_shape=[hd, hd, hd],
                   compiler_params=_cparams("parallel"))(q_raw, kv_raw, kv_raw, z, cs, sn, gq, gk, seg, rot)


def _qk_bwd(dqf, dkf, dv, q_raw, kv_raw, z, cs, sn, gq, gk, seg, rot):
    S = q_raw.shape[0]
    ts = _tile(S, ROW_TILE, SUBLANES)

    def body(dqf_ref, dkf_ref, dv_ref, q_ref, k_ref, kpe_ref, cs_ref, sn_ref, gq_ref, gk_ref, seg_ref, rot_ref,
             dq_ref, dkv_ref, dkpe_ref, dgq_ref, dgk_ref):
        m_n, m_r = _head_masks()
        cos, sin, gqv, gkv = cs_ref[...], sn_ref[...], gq_ref[...], gk_ref[...]
        seg_m, rot_m = seg_ref[...], rot_ref[...]

        def rope_t(w):
            return w * cos - _split_dot(w * sin, rot_m)

        def norm_bwd(x, dy, g):
            r = lax.rsqrt(_split_dot(x * x, seg_m) + EPS)
            xhat = x * r
            dyg = dy * g
            return r * (dyg - xhat * _split_dot(dyg * xhat, seg_m)), dy * xhat

        accq = jnp.zeros((ts, HEAD_PAD), F32)
        acck = jnp.zeros((ts, HEAD_PAD), F32)
        dkr = jnp.zeros((ts, HEAD_PAD), F32)
        for h in range(N_HEADS):
            sl = slice(h * HEAD_PAD, (h + 1) * HEAD_PAD)
            dx, c = norm_bwd(q_ref[:, sl], rope_t(dqf_ref[:, sl]), gqv)
            dq_ref[:, sl] = dx.astype(BF16)
            accq = accq + c
            dk = dkf_ref[:, sl]
            dkr = dkr + jnp.where(m_r, dk, 0.0)
            dx, c = norm_bwd(k_ref[:, sl], jnp.where(m_n, dk, 0.0), gkv)
            dkv_ref[:, sl] = dx.astype(BF16)
            acck = acck + c
            dkv_ref[:, HP + h * HEAD_PAD:HP + (h + 1) * HEAD_PAD] = dv_ref[:, sl].astype(BF16)
        dx, c = norm_bwd(kpe_ref[...], rope_t(dkr), gkv)
        dkpe_ref[...] = dx.astype(BF16)
        _accumulate(dgq_ref, accq)
        _accumulate(dgk_ref, acck + c)

    return _pallas(body, name="qk_bwd", grid=(S // ts,),
                   in_specs=[_rows(ts, HP)] * 4 + [_rows(ts, HP, 0), _rows(ts, HEAD_PAD, Z_KPE // HEAD_PAD),
                                                   _rows(ts, HEAD_PAD), _rows(ts, HEAD_PAD), _gain(HEAD_PAD), _gain(HEAD_PAD),
                                                   _HEAD_MAT, _HEAD_MAT],
                   out_specs=[_rows(ts, HP), _rows(ts, 2 * HP), _rows(ts, HEAD_PAD), _accum(HEAD_PAD), _accum(HEAD_PAD)],
                   out_shape=[jax.ShapeDtypeStruct((S, HP), BF16), jax.ShapeDtypeStruct((S, 2 * HP), BF16),
                              jax.ShapeDtypeStruct((S, HEAD_PAD), BF16), jax.ShapeDtypeStruct((SUBLANES, HEAD_PAD), F32),
                              jax.ShapeDtypeStruct((SUBLANES, HEAD_PAD), F32)],
                   compiler_params=_cparams("arbitrary"))(dqf, dkf, dv, q_raw, kv_raw, z, cs, sn, gq, gk, seg, rot)


def _causal(t):
    row = lax.broadcasted_iota(jnp.int32, (t, t), 0)
    col = lax.broadcasted_iota(jnp.int32, (t, t), 1)
    return col <= row


def _attn_tiles(S, chains):
    t = _tile(S, ATT_TILE, SUBLANES)
    return t, min(chains, S // t)


def _attn_fwd(qf, kf, vb, rider):
    S = qf.shape[0]
    t, nc = _attn_tiles(S, ATT_CHAINS_FWD)
    tq = nc * t
    nqt = S // tq

    def body(q_ref, k_ref, v_ref, *rest):
        (o_ref, lse_ref, m_ref, acc_ref), ride = rider.split(rest, 2, 2)
        qt = pl.program_id(1)
        ride(pl.program_id(0) * nqt + qt, N_HEADS * nqt)
        m_ref[...] = jnp.full_like(m_ref, -jnp.inf)
        acc_ref[...] = jnp.zeros_like(acc_ref)

        def against(rows, first, diagonal):
            kb, vb_ = k_ref[rows, :], v_ref[rows, :]
            subs = [slice(a * t, (a + 1) * t) for a in range(first, nc)]
            s = [lax.dot_general(q_ref[sub, :], kb, _DIMS["nt"], preferred_element_type=F32) * ATT_SCALE_LOG2 for sub in subs]
            if diagonal:
                s[0] = jnp.where(_causal(t), s[0], -jnp.inf)
            m_old = [m_ref[sub, :] for sub in subs]
            m_new = [jnp.maximum(mo, jnp.max(sa, axis=-1, keepdims=True)) for mo, sa in zip(m_old, s)]
            p = [jnp.exp2(sa - jnp.concatenate([mn] * (t // HEAD_PAD), axis=1)).astype(BF16) for sa, mn in zip(s, m_new)]
            for sub, mo, mn, pa in zip(subs, m_old, m_new, p):
                acc_ref[sub, :] = jnp.exp2(mo - mn) * acc_ref[sub, :] + jnp.dot(pa, vb_, preferred_element_type=F32)
                m_ref[sub, :] = mn

        def trip(j, carry):
            against(pl.ds(pl.multiple_of(j * t, t), t), 0, False)
            return carry

        lax.fori_loop(0, nc * qt, trip, 0)
        for d in range(nc):
            against(pl.ds(pl.multiple_of((nc * qt + d) * t, t), t), d, True)
        acc = acc_ref[...]
        l = acc[:, V_HEAD:V_HEAD + 1]
        lane = lax.broadcasted_iota(jnp.int32, (1, HEAD_PAD), 1)
        o_ref[...] = jnp.where(lane < V_HEAD, acc / l, 0.0)
        lse_ref[...] = m_ref[...] + jnp.log(l) * LOG2E

    tile = pl.BlockSpec((tq, HEAD_PAD), lambda h, i: (i, h))
    full = pl.BlockSpec((S, HEAD_PAD), lambda h, i: (0, h))
    outs = _pallas(body, name="attn_fwd", grid=(N_HEADS, nqt), in_specs=[tile, full, full] + rider.in_specs(),
                   out_specs=[tile, pl.BlockSpec((None, tq, HEAD_PAD), lambda h, i: (h, i, 0))] + rider.out_specs(),
                   out_shape=[jax.ShapeDtypeStruct((S, HP), F32), jax.ShapeDtypeStruct((N_HEADS, S, HEAD_PAD), F32)]
                   + rider.out_shape(),
                   scratch_shapes=[pltpu.VMEM((tq, HEAD_PAD), F32), pltpu.VMEM((tq, HEAD_PAD), F32)] + rider.scratch(),
                   input_output_aliases=rider.aliases(3, 2),
                   compiler_params=_cparams("arbitrary", "arbitrary"))(qf, kf, vb, *rider.args())
    return outs[0], outs[1], outs[2:]


def _attn_bwd(qf, kf, vb, o, do, lse, rider):
    S = qf.shape[0]
    t, nc = _attn_tiles(S, ATT_CHAINS_BWD)
    tkv = nc * t
    nq = S // t
    nkt = S // tkv

    def body(q_ref, k_ref, v_ref, o_ref, do_ref, lse_ref, *rest):
        (dq_ref, dk_ref, dv_ref), ride = rider.split(rest, 3, 0)
        kt = pl.program_id(1)
        ride(pl.program_id(0) * nkt + kt, N_HEADS * nkt)

        @pl.when(kt == 0)
        def _():
            dq_ref[...] = jnp.zeros_like(dq_ref)

        dk_ref[...] = jnp.zeros_like(dk_ref)
        dv_ref[...] = jnp.zeros_like(dv_ref)

        def q_block(rows):
            dof = do_ref[rows, :]
            delta = jnp.sum(dof * o_ref[rows, :], axis=-1, keepdims=True)
            return q_ref[rows, :], dof.astype(BF16), lse_ref[rows, :][:, :1], delta

        def against(rows, n_sub, diagonal):
            q, dob, lse, delta = q_block(rows)
            subs = [slice(b * t, (b + 1) * t) for b in range(n_sub)]
            kbs = [k_ref[sub, :] for sub in subs]
            s = [lax.dot_general(q, kb, _DIMS["nt"], preferred_element_type=F32) for kb in kbs]
            dp = [lax.dot_general(dob, v_ref[sub, :], _DIMS["nt"], preferred_element_type=F32) for sub in subs]
            p = [jnp.exp2(sb * ATT_SCALE_LOG2 - lse) for sb in s]
            if diagonal:
                p[-1] = jnp.where(_causal(t), p[-1], 0.0)
            ds = [(pb * (dpb - delta) * ATT_SCALE).astype(BF16) for pb, dpb in zip(p, dp)]
            for sub, pb in zip(subs, p):
                dv_ref[sub, :] += lax.dot_general(pb.astype(BF16), dob, _DIMS["tn"], preferred_element_type=F32)
            for sub, dsb in zip(subs, ds):
                dk_ref[sub, :] += lax.dot_general(dsb, q, _DIMS["tn"], preferred_element_type=F32)
            dq_ref[rows, :] += sum(jnp.dot(dsb, kb, preferred_element_type=F32) for dsb, kb in zip(ds, kbs))

        for a in range(nc):
            against(pl.ds(pl.multiple_of((nc * kt + a) * t, t), t), a + 1, True)

        def trip(i, carry):
            against(pl.ds(pl.multiple_of(i * t, t), t), nc, False)
            return carry

        lax.fori_loop(nc * (kt + 1), nq, trip, 0)

    tile = pl.BlockSpec((tkv, HEAD_PAD), lambda h, j: (j, h))
    full = pl.BlockSpec((S, HEAD_PAD), lambda h, j: (0, h))
    hd = jax.ShapeDtypeStruct((S, HP), F32)
    outs = _pallas(body, name="attn_bwd", grid=(N_HEADS, nkt),
                   in_specs=[full, tile, tile, full, full, pl.BlockSpec((None, S, HEAD_PAD), lambda h, j: (h, 0, 0))]
                   + rider.in_specs(),
                   out_specs=[full, tile, tile] + rider.out_specs(), out_shape=[hd, hd, hd] + rider.out_shape(),
                   scratch_shapes=rider.scratch(), input_output_aliases=rider.aliases(6, 3),
                   compiler_params=_cparams("arbitrary", "arbitrary"))(qf, kf, vb, o, do, lse, *rider.args())
    return outs[0], outs[1], outs[2], outs[3:]


def _shift_down(u, j, row):
    return jnp.where(row >= j, pltpu.roll(u, j, 0), 0.0)


def _shift_up(u, j, row, s):
    return jnp.where(row < s - j, pltpu.roll(u, s - j, 0), 0.0)


def _conv_cols(s, first_tile):
    return pl.BlockSpec((s, LANES), lambda cb: (0, first_tile + cb))


def _conv_fwd(z, cw8):
    S = z.shape[0]

    def body(gb_ref, gc_ref, xin_ref, w_ref, out_ref):
        row = lax.broadcasted_iota(jnp.int32, (S, LANES), 0)
        u = gc_ref[...] * xin_ref[...]
        y = w_ref[0:1, :] * u
        for j in range(1, CONV_TAPS):
            y = y + w_ref[j:j + 1, :] * _shift_down(u, j, row)
        out_ref[...] = gb_ref[...] * y

    return _pallas(body, name="conv_fwd", grid=(CONV_WIDTH // LANES,),
                   in_specs=[_conv_cols(S, Z_GB // LANES), _conv_cols(S, Z_GC // LANES), _conv_cols(S, Z_XIN // LANES),
                             pl.BlockSpec((SUBLANES, LANES), lambda cb: (0, cb))],
                   out_specs=_conv_cols(S, 0), out_shape=jax.ShapeDtypeStruct((S, CONV_WIDTH), F32),
                   compiler_params=_cparams("parallel"))(z, z, z, cw8)


def _conv_bwd(dconv, z, cw8):
    S = z.shape[0]

    def body(d_ref, gb_ref, gc_ref, xin_ref, w_ref, dgb_ref, dgc_ref, dxin_ref, dw_ref):
        row = lax.broadcasted_iota(jnp.int32, (S, LANES), 0)
        gc, xin, d = gc_ref[...], xin_ref[...], d_ref[...]
        u = gc * xin
        dy = d * gb_ref[...]
        y = w_ref[0:1, :] * u
        du = w_ref[0:1, :] * dy
        dw = [jnp.sum(dy * u, axis=0, keepdims=True)]
        for j in range(1, CONV_TAPS):
            uj = _shift_down(u, j, row)
            y = y + w_ref[j:j + 1, :] * uj
            du = du + w_ref[j:j + 1, :] * _shift_up(dy, j, row, S)
            dw.append(jnp.sum(dy * uj, axis=0, keepdims=True))
        dgb_ref[...] = (d * y).astype(BF16)
        dgc_ref[...] = (du * xin).astype(BF16)
        dxin_ref[...] = (du * gc).astype(BF16)
        tap = lax.broadcasted_iota(jnp.int32, (SUBLANES, LANES), 0)
        dw_ref[...] = sum(jnp.where(tap == j, dw[j], 0.0) for j in range(CONV_TAPS))

    col = _conv_cols(S, 0)
    sd = jax.ShapeDtypeStruct((S, CONV_WIDTH), BF16)
    return _pallas(body, name="conv_bwd", grid=(CONV_WIDTH // LANES,),
                   in_specs=[col, _conv_cols(S, Z_GB // LANES), _conv_cols(S, Z_GC // LANES), _conv_cols(S, Z_XIN // LANES),
                             pl.BlockSpec((SUBLANES, LANES), lambda cb: (0, cb))],
                   out_specs=[col, col, col, pl.BlockSpec((SUBLANES, LANES), lambda cb: (0, cb))],
                   out_shape=[sd, sd, sd, jax.ShapeDtypeStruct((SUBLANES, CONV_WIDTH), F32)],
                   compiler_params=_cparams("parallel"))(dconv, z, z, z, cw8)


def _mix_fwd(o, conv, ga, gc):
    S = o.shape[0]
    ts = _tile(S, ROW_TILE, SUBLANES)

    def body(o_ref, c_ref, ga_ref, gc_ref, out_ref):
        ov, cv = o_ref[...], c_ref[...]
        out_ref[:, :HP] = (ov * _rinv(ov, ATTN_WIDTH) * ga_ref[...]).astype(BF16)
        out_ref[:, HP:] = (cv * _rinv(cv, CONV_WIDTH) * gc_ref[...]).astype(BF16)

    return _pallas(body, name="mix_fwd", grid=(S // ts,),
                   in_specs=[_rows(ts, HP), _rows(ts, CONV_WIDTH), _gain(HP), _gain(CONV_WIDTH)],
                   out_specs=_rows(ts, HP + CONV_WIDTH), out_shape=jax.ShapeDtypeStruct((S, HP + CONV_WIDTH), BF16),
                   compiler_params=_cparams("parallel"))(o, conv, ga, gc)


def _mix_bwd(dmixed, o, conv, ga, gc):
    S = o.shape[0]
    ts = _tile(S, ROW_TILE, SUBLANES)

    def body(d_ref, o_ref, c_ref, ga_ref, gc_ref, do_ref, dc_ref, dga_ref, dgc_ref):
        dx, ca = _norm_bwd(o_ref[...], d_ref[:, :HP], ga_ref[...], ATTN_WIDTH)
        do_ref[...] = dx
        dx, cc = _norm_bwd(c_ref[...], d_ref[:, HP:], gc_ref[...], CONV_WIDTH)
        dc_ref[...] = dx
        _accumulate(dga_ref, ca)
        _accumulate(dgc_ref, cc)

    return _pallas(body, name="mix_bwd", grid=(S // ts,),
                   in_specs=[_rows(ts, HP + CONV_WIDTH), _rows(ts, HP), _rows(ts, CONV_WIDTH), _gain(HP), _gain(CONV_WIDTH)],
                   out_specs=[_rows(ts, HP), _rows(ts, CONV_WIDTH), _accum(HP), _accum(CONV_WIDTH)],
                   out_shape=[jax.ShapeDtypeStruct((S, HP), F32), jax.ShapeDtypeStruct((S, CONV_WIDTH), F32),
                              jax.ShapeDtypeStruct((SUBLANES, HP), F32), jax.ShapeDtypeStruct((SUBLANES, CONV_WIDTH), F32)],
                   compiler_params=_cparams("arbitrary"))(dmixed, o, conv, ga, gc)


def _ple_fwd(x, gl, pe, g_next):
    S, D = x.shape
    ts = _tile(S, ROW_TILE, SUBLANES)

    def body(x_ref, gl_ref, pe_ref, g_ref, out_ref, h_ref):
        y = x_ref[...] + jax.nn.sigmoid(gl_ref[...]) * pe_ref[...]
        out_ref[...] = y
        h_ref[...] = (y * _rinv(y, D) * g_ref[...]).astype(BF16)

    return _pallas(body, name="ple_fwd", grid=(S // ts,), in_specs=[_rows(ts, D)] * 3 + [_gain(D)], out_specs=[_rows(ts, D)] * 2,
                   out_shape=[jax.ShapeDtypeStruct((S, D), F32), jax.ShapeDtypeStruct((S, D), BF16)],
                   compiler_params=_cparams("parallel"))(x, gl, pe, g_next)


def _ple_bwd(dx, gl, pe):
    S, D = dx.shape
    ts = _tile(S, ROW_TILE, SUBLANES)

    def body(dx_ref, gl_ref, pe_ref, dpe_ref, dgl_ref):
        d = dx_ref[...]
        gate = jax.nn.sigmoid(gl_ref[...])
        dpe_ref[...] = (d * gate).astype(BF16)
        dgl_ref[...] = (d * pe_ref[...] * (gate * (1.0 - gate))).astype(BF16)

    sd = jax.ShapeDtypeStruct((S, D), BF16)
    return _pallas(body, name="ple_bwd", grid=(S // ts,), in_specs=[_rows(ts, D)] * 3, out_specs=[_rows(ts, D)] * 2,
                   out_shape=[sd, sd], compiler_params=_cparams("parallel"))(dx, gl, pe)


def _loss_grad(y, target):
    S, D = y.shape
    ts = _tile(S, ROW_TILE, SUBLANES)

    def body(y_ref, t_ref, dy_ref, sq_ref):
        e = y_ref[...] - t_ref[...]
        dy_ref[...] = e / D

        @pl.when(pl.program_id(0) == 0)
        def _():
            sq_ref[...] = jnp.zeros_like(sq_ref)

        sq_ref[...] += jnp.broadcast_to(jnp.sum(jnp.sum(e * e, axis=1, keepdims=True), axis=0, keepdims=True), sq_ref.shape)

    return _pallas(body, name="loss_grad", grid=(S // ts,), in_specs=[_rows(ts, D)] * 2,
                   out_specs=[_rows(ts, D), _accum(LANES)],
                   out_shape=[jax.ShapeDtypeStruct((S, D), F32), jax.ShapeDtypeStruct((SUBLANES, LANES), F32)],
                   compiler_params=_cparams("arbitrary"))(y, target)


def _adamw(w, g, m, v, name):
    L, R, C = w.shape
    tr = _tile(R, ROW_TILE, SUBLANES)

    def body(w_ref, g_ref, m_ref, v_ref, d_ref, m2_ref, v2_ref):
        gv = g_ref[...]
        m2 = ADAM_B1 * m_ref[...] + (1.0 - ADAM_B1) * gv
        v2 = ADAM_B2 * v_ref[...] + (1.0 - ADAM_B2) * (gv * gv)
        m_hat = m2 / (1.0 - ADAM_B1 ** ADAM_STEP)
        v_hat = v2 / (1.0 - ADAM_B2 ** ADAM_STEP)
        d_ref[...] = -ADAM_LR * (m_hat / (jnp.sqrt(v_hat) + ADAM_EPS) + ADAM_WD * w_ref[...])
        m2_ref[...] = m2
        v2_ref[...] = v2

    sd = jax.ShapeDtypeStruct((L, R, C), F32)
    spec = pl.BlockSpec((None, tr, C), lambda l, i: (l, i, 0))
    return _pallas(body, name=name, grid=(L, R // tr), in_specs=[spec] * 4, out_specs=[spec] * 3, out_shape=[sd, sd, sd],
                   compiler_params=_cparams("parallel", "parallel"))(w, g, m, v)


def _place():
    return lax.axis_index("x"), lax.axis_index("y"), lax.axis_index("c")


def _other_chips(x, y):
    return [(1 - x, y), (x, 1 - y), (1 - x, 1 - y)]


HBM = pl.BlockSpec(memory_space=pl.ANY)
VMEM_SPEC = pl.BlockSpec(memory_space=pltpu.VMEM)


def _remote_copy(send_sems, recv_sems, k, src, dst, to):
    return pltpu.make_async_remote_copy(src_ref=src, dst_ref=dst, send_sem=send_sems.at[k], recv_sem=recv_sems.at[k],
                                        device_id=to, device_id_type=MESH)


def _comm_call(body, name, arrays, out_shapes, n_remote, in_place=False):
    scratch = [pltpu.SemaphoreType.DMA((n_remote,)), pltpu.SemaphoreType.DMA((n_remote,))]
    aliases = {i: i for i in range(len(arrays))} if in_place else {}
    return _pallas(body, name=name, in_specs=[HBM] * len(arrays), out_specs=[HBM] * len(out_shapes), out_shape=out_shapes,
                   scratch_shapes=scratch, input_output_aliases=aliases,
                   compiler_params=pltpu.CompilerParams(has_side_effects=True))(*arrays)


def _shard_slab(w, chip_arr, name, stack_rows):
    L, R, C = w.shape
    tr = _tile(R, ROW_TILE, 16)
    nb = R // tr

    def body(j_ref, w_ref, out_ref):
        out_ref[...] = w_ref[...].astype(BF16)

    if stack_rows:
        out_shape = jax.ShapeDtypeStruct((L, N_CHIPS * R, C), BF16)
        out_spec = pl.BlockSpec((None, tr, C), lambda l, i, j_ref: (l, j_ref[0] * nb + i, 0))
    else:
        out_shape = jax.ShapeDtypeStruct((N_CHIPS, L, R, C), BF16)
        out_spec = pl.BlockSpec((None, None, tr, C), lambda l, i, j_ref: (j_ref[0], l, i, 0))
    grid_spec = pltpu.PrefetchScalarGridSpec(
        num_scalar_prefetch=1, grid=(L, nb), in_specs=[pl.BlockSpec((None, tr, C), lambda l, i, j_ref: (l, i, 0))],
        out_specs=out_spec)
    return _pallas(body, name=name, grid_spec=grid_spec, out_shape=out_shape,
                   compiler_params=_cparams("parallel", "parallel"))(chip_arr, w)


def _shard_dims(slab):
    return (slab.shape[2], slab.shape[3]) if slab.ndim == 4 else (slab.shape[1] // N_CHIPS, slab.shape[2])


def _shard_rows(ref, chip, l, which):
    ks, _ = _shard_dims(ref)
    h = ks // 2
    return ref.at[chip, l, pl.ds(which * h, h)] if len(ref.shape) == 4 else ref.at[l, pl.ds(chip * ks + which * h, h)]


class Rider:
    def __init__(self, reads, inplace, n_sems, stages):
        self.reads, self.inplace, self.n_sems, self.stages = list(reads), list(inplace), n_sems, stages

    def args(self):
        return self.reads + self.inplace

    def in_specs(self):
        return [HBM] * len(self.args())

    def out_specs(self):
        return [HBM] * len(self.inplace)

    def out_shape(self):
        return [jax.ShapeDtypeStruct(a.shape, a.dtype) for a in self.inplace]

    def scratch(self):
        return [pltpu.SemaphoreType.DMA((self.n_sems,)), pltpu.SemaphoreType.DMA((self.n_sems,))] if self.n_sems else []

    def aliases(self, n_host_in, n_host_out):
        return {n_host_in + len(self.reads) + j: n_host_out + j for j in range(len(self.inplace))}

    def split(self, rest, n_host_out, n_host_scratch):
        n_r, n_io = len(self.reads), len(self.inplace)
        reads = rest[:n_r]
        host_out = rest[n_r + n_io:n_r + n_io + n_host_out]
        outs = rest[n_r + n_io + n_host_out:n_r + 2 * n_io + n_host_out]
        host_scratch = rest[n_r + 2 * n_io + n_host_out:n_r + 2 * n_io + n_host_out + n_host_scratch]
        sems = rest[n_r + 2 * n_io + n_host_out + n_host_scratch:]

        def ride(step, n_steps):
            at = [0, n_steps - 1] if len(self.stages) == 2 else [0, (3 * n_steps) // 4, n_steps - 1]
            for s, stage in zip(at, self.stages):
                pl.when(step == s)(functools.partial(stage, reads, outs, *sems))

        return tuple(host_out) + tuple(host_scratch), ride

    def run(self, name):
        n_r, n_io = len(self.reads), len(self.inplace)

        def body(*refs):
            for stage in self.stages:
                stage(refs[:n_r], refs[n_r + n_io:n_r + 2 * n_io], *refs[n_r + 2 * n_io:])

        return _pallas(body, name=name, in_specs=self.in_specs(), out_specs=self.out_specs(), out_shape=self.out_shape(),
                       scratch_shapes=self.scratch(), input_output_aliases=self.aliases(0, 0),
                       compiler_params=pltpu.CompilerParams(has_side_effects=True))(*self.args())


def _allgather_rider(slabs, items):
    def stage(which, reads, outs, send_sems, recv_sems):
        x, y, c = _place()
        me, sibling = (x, y, c), (x, y, 1 - c)
        copy = functools.partial(_remote_copy, send_sems, recv_sems)
        for n, (j, l) in enumerate(items):
            ref = outs[j]
            own = _shard_rows(ref, 2 * x + y, l, c)
            for k, (cx, cy) in enumerate(_other_chips(x, y)):
                arrived = _shard_rows(ref, 2 * cx + cy, l, c)
                if which == 0:
                    copy(6 * n + k, own, own, (cx, cy, c)).start()
                elif which == 1:
                    copy(6 * n + k, arrived, arrived, me).wait_recv()
                    copy(6 * n + 3 + k, arrived, arrived, sibling).start()
                else:
                    passed = _shard_rows(ref, 2 * cx + cy, l, 1 - c)
                    copy(6 * n + 3 + k, passed, passed, me).wait_recv()
                    copy(6 * n + k, own, own, me).wait_send()
                    copy(6 * n + 3 + k, arrived, arrived, me).wait_send()

    return Rider([], slabs, 6 * len(items), [functools.partial(stage, w) for w in range(3)])


def _exchange_rider(grads, landing, items):
    def stage(start, reads, outs, send_sems, recv_sems):
        x, y, c = _place()
        for n, (i, j, l) in enumerate(items):
            g_ref, r_ref = reads[i], outs[j]
            h = g_ref.shape[1] // 2
            for k in range(1, N_DEV):
                px, py, pc = (1 - x if k & 4 else x, 1 - y if k & 2 else y, 1 - c if k & 1 else c)
                cp = _remote_copy(send_sems, recv_sems, 7 * n + k - 1, g_ref.at[2 * px + py, pl.ds(pc * h, h)],
                                  r_ref.at[l, k - 1], (px, py, pc))
                if start:
                    cp.start()
                else:
                    cp.wait()

    return Rider(grads, landing, 7 * len(items), [functools.partial(stage, True), functools.partial(stage, False)])


def _reduce_partials(g, r, chip_arr, c_arr, name):
    L, n, R, C = g.shape
    H = R // 2
    th = _tile(H, ROW_TILE, 16)
    nb = H // th

    def body(j_ref, c_ref, g_ref, r_ref, out_ref):
        s = g_ref[...].astype(F32)
        for k in range(N_DEV - 1):
            s = s + r_ref[k].astype(F32)
        out_ref[...] = s

    grid_spec = pltpu.PrefetchScalarGridSpec(
        num_scalar_prefetch=2, grid=(L, nb),
        in_specs=[pl.BlockSpec((None, None, th, C), lambda l, i, j_ref, c_ref: (l, j_ref[0], c_ref[0] * nb + i, 0)),
                  pl.BlockSpec((None, N_DEV - 1, th, C), lambda l, i, j_ref, c_ref: (l, 0, i, 0))],
        out_specs=pl.BlockSpec((None, th, C), lambda l, i, j_ref, c_ref: (l, c_ref[0] * nb + i, 0)))
    return _pallas(body, name=name, grid_spec=grid_spec, out_shape=jax.ShapeDtypeStruct((L, R, C), F32),
                   compiler_params=_cparams("parallel", "parallel"))(chip_arr, c_arr, g, r)


def _join_halves(fs):
    n = len(fs)

    def body(*refs):
        out_refs, (send_sems, recv_sems) = refs[n:2 * n], refs[2 * n:]
        x, y, c = _place()
        cps = []
        for i, out_ref in enumerate(out_refs):
            h = out_ref.shape[1] // 2
            mine = out_ref.at[:, pl.ds(c * h, h)]
            cps.append(_remote_copy(send_sems, recv_sems, i, mine, mine, (x, y, 1 - c)))
            cps[-1].start()
        for cp in cps:
            cp.wait()

    return _comm_call(body, "grad_join_halves", fs, [jax.ShapeDtypeStruct(f.shape, f.dtype) for f in fs], n, in_place=True)


def _allgather_small(v, name):
    R, C = v.shape

    def body(v_ref, out_ref, send_sems, recv_sems):
        x, y, c = _place()
        me = 4 * x + 2 * y + c
        out_ref[me] = v_ref[...]
        cps = []
        for k in range(1, N_DEV):
            peer = (1 - x if k & 4 else x, 1 - y if k & 2 else y, 1 - c if k & 1 else c)
            cps.append(pltpu.make_async_remote_copy(src_ref=v_ref, dst_ref=out_ref.at[me], send_sem=send_sems.at[k - 1],
                                                    recv_sem=recv_sems.at[k - 1], device_id=peer, device_id_type=MESH))
        for cp in cps:
            cp.start()
        for cp in cps:
            cp.wait()

    return _pallas(body, name=name, in_specs=[VMEM_SPEC], out_specs=VMEM_SPEC,
                   out_shape=jax.ShapeDtypeStruct((N_DEV, R, C), v.dtype),
                   scratch_shapes=[pltpu.SemaphoreType.DMA((N_DEV - 1,)), pltpu.SemaphoreType.DMA((N_DEV - 1,))],
                   compiler_params=pltpu.CompilerParams(has_side_effects=True))(v)


def _sum_devices(g):
    n, R, C = g.shape

    def body(g_ref, out_ref):
        s = g_ref[0]
        for d in range(1, n):
            s = s + g_ref[d]
        out_ref[...] = s

    return _pallas(body, name="sum_devices", in_specs=[VMEM_SPEC], out_specs=VMEM_SPEC,
                   out_shape=jax.ShapeDtypeStruct((R, C), g.dtype))(g)


def _pad_heads(a, width):
    lead = a.shape[:-1]
    a = a.reshape(lead + (N_HEADS, width))
    a = jnp.pad(a, [(0, 0)] * len(lead) + [(0, 0), (0, HEAD_PAD - width)])
    return a.reshape(lead + (HP,))


def _unpad_heads(a, width):
    lead = a.shape[:-1]
    return a.reshape(lead + (N_HEADS, HEAD_PAD))[..., :width].reshape(lead + (N_HEADS * width,))


def _pre_weights(slabs, l):
    cols = lambda name: slabs[name][:, l].transpose(1, 0, 2).reshape(slabs[name].shape[2], -1)
    w_in = cols("w_in")
    kpe = jnp.pad(w_in[:, 640:672], ((0, 0), (ROPE_LO, HEAD_PAD - ROPE_HI)))
    w_ukv = cols("w_ukv").reshape(KV_LORA, N_HEADS, QK_NOPE + V_HEAD)
    return dict(
        w_in=jnp.concatenate([w_in[:, :640], kpe, w_in[:, 672:]], axis=1),
        w_uq=_pad_heads(cols("w_uq"), QK_HEAD),
        w_ukv=jnp.concatenate([_pad_heads(w_ukv[..., :QK_NOPE].reshape(KV_LORA, -1), QK_NOPE),
                               _pad_heads(w_ukv[..., QK_NOPE:].reshape(KV_LORA, -1), V_HEAD)], axis=1),
    )


def _post_weights(slabs, l):
    cols = lambda name: slabs[name][:, l].transpose(1, 0, 2).reshape(slabs[name].shape[2], -1)
    w_o = slabs["w_o"][l]
    return dict(
        w_o=jnp.concatenate([_pad_heads(w_o[:ATTN_WIDTH].T, V_HEAD).T, w_o[ATTN_WIDTH:]], axis=0),
        w_up=cols("w_up"), w_down=(slabs["w_down"], l), w_ple_gate=(slabs["w_ple_gate"], l), w_ple=cols("w_ple"),
    )


def _gains(small, l):
    row = lambda name: small[name][l].reshape(1, -1)
    headrow = lambda a, b: jnp.pad(jnp.concatenate([small[a][l], small[b][l]]), (0, HEAD_PAD - QK_HEAD)).reshape(1, HEAD_PAD)
    return dict(
        g_mix=row("g_mix"), g_q_lat=row("g_q_lat"), g_kv_lat=row("g_kv_lat"), g_mlp=row("g_mlp"), g_ple=row("g_ple"),
        g_out_conv=row("g_out_conv"), g_out_attn=_pad_heads(small["g_out_attn"][l], V_HEAD).reshape(1, HP),
        gq=headrow("g_qn_nope", "g_qn_rope"), gk=headrow("g_kn_nope", "g_kn_rope"),
        cw8=jnp.pad(small["conv_w"][l], ((0, SUBLANES - CONV_TAPS), (0, 0))),
        head_mats=_head_matrices(),
    )


def _w_o_shards(dw_o):
    return jnp.concatenate([_unpad_heads(dw_o[:HP].T, V_HEAD).T, dw_o[HP:]], axis=0).reshape(N_CHIPS, -1, D_MODEL)


def _unpad_grads(gp):
    dw_in = gp["w_in"]
    dw_ukv = gp["w_ukv"]
    k_part = dw_ukv[:, :HP].reshape(KV_LORA, N_HEADS, HEAD_PAD)[..., :QK_NOPE]
    v_part = dw_ukv[:, HP:].reshape(KV_LORA, N_HEADS, HEAD_PAD)[..., :V_HEAD]
    first = lambda name: gp[name][0]
    col_shards = lambda a: a.reshape(a.shape[0], N_CHIPS, -1).transpose(1, 0, 2)
    return dict(
        w_in=col_shards(jnp.concatenate([dw_in[:, :640], dw_in[:, Z_KPE + ROPE_LO:Z_KPE + ROPE_HI], dw_in[:, Z_GB:]], axis=1)),
        w_uq=col_shards(_unpad_heads(gp["w_uq"], QK_HEAD)),
        w_ukv=col_shards(jnp.concatenate([k_part, v_part], axis=-1).reshape(KV_LORA, -1)),
        w_o=gp["w_o"], w_up=gp["w_up"], w_down=gp["w_down"], w_ple_gate=gp["w_ple_gate"], w_ple=gp["w_ple"],
        g_mix=first("g_mix"), g_q_lat=first("g_q_lat"), g_kv_lat=first("g_kv_lat"), g_mlp=first("g_mlp"),
        g_ple=first("g_ple"), g_out_conv=first("g_out_conv"), g_out_attn=_unpad_heads(first("g_out_attn"), V_HEAD),
        g_qn_nope=gp["gq"][0, :QK_NOPE], g_qn_rope=gp["gq"][0, QK_NOPE:QK_HEAD],
        g_kn_nope=gp["gk"][0, :QK_NOPE], g_kn_rope=gp["gk"][0, QK_NOPE:QK_HEAD],
        conv_w=gp["cw8"][:CONV_TAPS],
    )


def _rope_tables(positions):
    inv_freq = 1.0 / (ROPE_THETA ** (jnp.arange(0, QK_ROPE, 2, dtype=F32) / QK_ROPE))
    ang = positions.astype(F32)[:, None] * inv_freq
    cos, sin = jnp.cos(ang), jnp.sin(ang)
    pad = lambda t, v: jnp.pad(jnp.concatenate([t, t], axis=1), ((0, 0), (ROPE_LO, HEAD_PAD - ROPE_HI)), constant_values=v)
    return pad(cos, 1.0), pad(sin, 0.0)


def _layer_fwd(x0, h, p_l, W, cs, sn, attend, g_next):
    z = _mm(h, W["w_in"], name="mm_in")
    qln, kvln = _lat_fwd(z, W["g_q_lat"], W["g_kv_lat"])
    q_raw = _mm(qln, W["w_uq"], name="mm_uq")
    kv_raw = _mm(kvln, W["w_ukv"], name="mm_ukv")
    qf, kf, vb = _qk_fwd(q_raw, kv_raw, z, cs, sn, W["gq"], W["gk"], *W["head_mats"])
    o, lse = attend(qf, kf, vb)
    conv = _conv_fwd(z, W["cw8"])
    mixed = _mix_fwd(o, conv, W["g_out_attn"], W["g_out_conv"])
    x1, h2 = _mm(mixed, W["w_o"], res=x0, gain=W["g_mlp"], epi="rms", name="mm_o")
    a, f = W["host_mm_up"](lambda rider: _mm(h2, W["w_up"], epi="relu2", out_dtype=BF16, rider=rider, name="mm_up"))
    x2, h3 = _mm(f, W["w_down"], res=x1, gain=W["g_ple"], epi="rms", name="mm_down")
    gl = _mm(h3, W["w_ple_gate"], name="mm_ple_gate")
    pe = _mm(p_l, W["w_ple"], name="mm_ple")
    x3, h_next = _ple_fwd(x2, gl, pe, g_next)
    saved = dict(x0=x0, h=h, z=z, qln=qln, kvln=kvln, q_raw=q_raw, kv_raw=kv_raw, qf=qf, kf=kf, vb=vb, o=o, lse=lse,
                 conv=conv, mixed=mixed, x1=x1, h2=h2, a=a, f=f, x2=x2, h3=h3, gl=gl, pe=pe)
    return x3, h_next, saved


def _layer_bwd(dx3, p_l, W, cs, sn, sv, attend_bwd):
    g = {}
    dpe, dgl = _ple_bwd(dx3, sv["gl"], sv["pe"])
    g["w_ple"] = _mm(p_l, dpe, mode="tn", out_dtype=BF16, shard_out=1, name="mm_dw_ple")
    g["w_ple_gate"] = _mm(sv["h3"], dgl, mode="tn", out_dtype=BF16, shard_out=0, name="mm_dw_ple_gate")
    dx2, dx2b, g["g_ple"] = _mm(dgl, W["w_ple_gate"], mode="nt", res=dx3, aux=sv["x2"], gain=W["g_ple"], epi="rms_bwd",
                                name="mm_dh3")
    da, = W["host_mm_da"](lambda rider: _mm(dx2b, W["w_down"], mode="nt", aux=sv["a"], epi="drelu2", out_dtype=BF16,
                                            rider=rider, name="mm_da"))
    g["w_down"] = _mm(sv["f"], dx2b, mode="tn", out_dtype=BF16, shard_out=0, name="mm_dw_down")
    g["w_up"] = _mm(sv["h2"], da, mode="tn", out_dtype=BF16, shard_out=1, name="mm_dw_up")
    dx1, dx1b, g["g_mlp"] = _mm(da, W["w_up"], mode="nt", res=dx2, aux=sv["x1"], gain=W["g_mlp"], epi="rms_bwd",
                                name="mm_dh2")
    dmixed = _mm(dx1b, W["w_o"], mode="nt", name="mm_dmixed")
    g["w_o"] = _mm(sv["mixed"], dx1b, mode="tn", out_dtype=BF16, name="mm_dw_o")
    do, dconv, g["g_out_attn"], g["g_out_conv"] = _mix_bwd(dmixed, sv["o"], sv["conv"], W["g_out_attn"], W["g_out_conv"])
    dgb, dgc, dxin, g["cw8"] = _conv_bwd(dconv, sv["z"], W["cw8"])
    dqf, dkf, dv = attend_bwd(g, sv["qf"], sv["kf"], sv["vb"], sv["o"], do, sv["lse"])
    dq_raw, dkv_raw, dkpe, g["gq"], g["gk"] = _qk_bwd(dqf, dkf, dv, sv["q_raw"], sv["kv_raw"], sv["z"], cs, sn, W["gq"], W["gk"],
                                                      *W["head_mats"])
    g["w_uq"] = _mm(sv["qln"], dq_raw, mode="tn", out_dtype=BF16, name="mm_dw_uq")
    dqln = _mm(dq_raw, W["w_uq"], mode="nt", name="mm_dqln")
    g["w_ukv"] = _mm(sv["kvln"], dkv_raw, mode="tn", out_dtype=BF16, name="mm_dw_ukv")
    dkvln = _mm(dkv_raw, W["w_ukv"], mode="nt", name="mm_dkvln")
    dlat, g["g_q_lat"], g["g_kv_lat"] = _lat_bwd(dqln, dkvln, sv["z"], W["g_q_lat"], W["g_kv_lat"])
    dz = jnp.concatenate([dlat, dkpe, dgb, dgc, dxin], axis=1)
    g["w_in"] = _mm(sv["h"], dz, mode="tn", out_dtype=BF16, name="mm_dw_in")
    dx0, _, g["g_mix"] = _mm(dz, W["w_in"], mode="nt", res=dx1, aux=sv["x0"], gain=W["g_mix"], epi="rms_bwd", name="mm_dh")
    return dx0, g


def _local_step(x, p, positions, target, slabs, small):
    depth = p.shape[0]
    cs, sn = _rope_tables(positions)
    slabs = dict(slabs)

    def gather(items, host):
        touched = [n for n in PRE + POST if any(n == name for name, _ in items)]
        rider = _allgather_rider([slabs[n] for n in touched], [(touched.index(name), l) for name, l in items])
        if host is None:
            out, new = (), rider.run("allgather_first")
        else:
            *out, new = host(rider)
        slabs.update(zip(touched, new))
        return out

    gather([(name, 0) for name in PRE], None)
    Ws, saved = [], []
    gains = [_gains(small, l) for l in range(depth)]
    h = _rms_fwd(x, gains[0]["g_mix"], "rms_mix")
    for l in range(depth):
        W = dict(gains[l], **_pre_weights(slabs, l))

        def attend(qf, kf, vb, W=W, l=l):
            o, lse = gather([(name, l) for name in POST], functools.partial(_attn_fwd, qf, kf, vb))
            W.update(_post_weights(slabs, l))
            return o, lse

        if l + 1 < depth:
            W["host_mm_up"] = lambda run, l=l: gather([(name, l + 1) for name in PRE], run)
        else:
            W["host_mm_up"] = lambda run: run(None)

        g_next = gains[l + 1]["g_mix"] if l + 1 < depth else jnp.ones_like(gains[l]["g_mix"])
        x, h, sv = _layer_fwd(x, h, p[l], W, cs, sn, attend, g_next)
        Ws.append(W)
        saved.append(sv)
    dx, sq = _loss_grad(x, target)

    landing = {name: lax.empty((depth, N_DEV - 1, _shard_dims(slabs[name])[0] // 2, _shard_dims(slabs[name])[1]), BF16)
               for name in PRE + POST}

    def exchange(sends, host):
        touched = [n for n in PRE + POST if any(n == name for name, _, _ in sends)]
        rider = _exchange_rider([g for _, _, g in sends], [landing[n] for n in touched],
                                [(i, touched.index(name), l) for i, (name, l, _) in enumerate(sends)])
        if host is None:
            out, new = (), rider.run("grad_exchange_last")
        else:
            *out, new = host(rider)
        landing.update(zip(touched, new))
        return out

    grads = [None] * depth
    for l in reversed(range(depth)):
        W = dict(Ws[l], w_down=(slabs["w_down"], l), w_ple_gate=(slabs["w_ple_gate"], l))

        def attend_bwd(g, qf, kf, vb, o, do, lse, l=l):
            g["w_o"] = _w_o_shards(g["w_o"])
            return exchange([(name, l, g[name]) for name in POST], functools.partial(_attn_bwd, qf, kf, vb, o, do, lse))

        if l + 1 < depth:
            W["host_mm_da"] = lambda run, l=l: exchange([(name, l + 1, grads[l + 1][name]) for name in PRE], run)
        else:
            W["host_mm_da"] = lambda run: (run(None),)

        dx, gp = _layer_bwd(dx, p[l], W, cs, sn, saved[l], attend_bwd)
        grads[l] = _unpad_grads(gp)
    exchange([(name, 0, grads[0][name]) for name in PRE], None)
    return sq, dx, grads, landing


def _pack_rows(vals, entries):
    depth = vals[entries[0][0]].shape[0]
    return jnp.concatenate([jnp.pad(vals[name].reshape(depth, -1), ((0, 0), (0, _pad128(n) - n))) for name, n in entries], axis=1)


def _unpack_rows(packed, entries):
    out, off = {}, 0
    for name, n in entries:
        out[name] = packed[:, off:off + n]
        off += _pad128(n)
    return out


def kernel(x, p, positions, g_mix, w_in, g_q_lat, w_uq, g_kv_lat, w_ukv, g_qn_nope, g_qn_rope, g_kn_nope, g_kn_rope, conv_w, g_out_attn, g_out_conv, w_o, g_mlp, w_up, w_down, g_ple, w_ple_gate, w_ple, loss_target, m_g_mix, m_w_in, m_g_q_lat, m_w_uq, m_g_kv_lat, m_w_ukv, m_g_qn_nope, m_g_qn_rope, m_g_kn_nope, m_g_kn_rope, m_conv_w, m_g_out_attn, m_g_out_conv, m_w_o, m_g_mlp, m_w_up, m_w_down, m_g_ple, m_w_ple_gate, m_w_ple, v_g_mix, v_w_in, v_g_q_lat, v_w_uq, v_g_kv_lat, v_w_ukv, v_g_qn_nope, v_g_qn_rope, v_g_kn_nope, v_g_kn_rope, v_conv_w, v_g_out_attn, v_g_out_conv, v_w_o, v_g_mlp, v_w_up, v_w_down, v_g_ple, v_w_ple_gate, v_w_ple):
    w = dict(g_mix=g_mix, w_in=w_in, g_q_lat=g_q_lat, w_uq=w_uq, g_kv_lat=g_kv_lat, w_ukv=w_ukv, g_qn_nope=g_qn_nope,
             g_qn_rope=g_qn_rope, g_kn_nope=g_kn_nope, g_kn_rope=g_kn_rope, conv_w=conv_w, g_out_attn=g_out_attn,
             g_out_conv=g_out_conv, w_o=w_o, g_mlp=g_mlp, w_up=w_up, w_down=w_down, g_ple=g_ple, w_ple_gate=w_ple_gate,
             w_ple=w_ple)
    m = dict(g_mix=m_g_mix, w_in=m_w_in, g_q_lat=m_g_q_lat, w_uq=m_w_uq, g_kv_lat=m_g_kv_lat, w_ukv=m_w_ukv,
             g_qn_nope=m_g_qn_nope, g_qn_rope=m_g_qn_rope, g_kn_nope=m_g_kn_nope, g_kn_rope=m_g_kn_rope, conv_w=m_conv_w,
             g_out_attn=m_g_out_attn, g_out_conv=m_g_out_conv, w_o=m_w_o, g_mlp=m_g_mlp, w_up=m_w_up, w_down=m_w_down,
             g_ple=m_g_ple, w_ple_gate=m_w_ple_gate, w_ple=m_w_ple)
    v = dict(g_mix=v_g_mix, w_in=v_w_in, g_q_lat=v_g_q_lat, w_uq=v_w_uq, g_kv_lat=v_g_kv_lat, w_ukv=v_w_ukv,
             g_qn_nope=v_g_qn_nope, g_qn_rope=v_g_qn_rope, g_kn_nope=v_g_kn_nope, g_kn_rope=v_g_kn_rope, conv_w=v_conv_w,
             g_out_attn=v_g_out_attn, g_out_conv=v_g_out_conv, w_o=v_w_o, g_mlp=v_g_mlp, w_up=v_w_up, w_down=v_w_down,
             g_ple=v_g_ple, w_ple_gate=v_w_ple_gate, w_ple=v_w_ple)
    depth = p.shape[0]
    ax, ay, ac = _place()
    chip = 2 * ax + ay
    c_arr = jnp.reshape(ac, (1,)).astype(jnp.int32)
    chip_arr = jnp.reshape(chip, (1,)).astype(jnp.int32)
    conv_shard =("conv_w", CONV_TAPS * CONV_WIDTH // N_CHIPS)
    conv_full = ("conv_w", CONV_TAPS * CONV_WIDTH)

    names = PRE + POST
    slabs = {name: _shard_slab(w[name], chip_arr, "shard_slab_" + name, name in ROW_SHARDED) for name in names}
    conv_rows = -(-depth * CONV_TAPS // SUBLANES) * SUBLANES
    conv_all = _allgather_small(jnp.pad(conv_w.reshape(depth * CONV_TAPS, LANES), ((0, conv_rows - depth * CONV_TAPS), (0, 0))),
                                "allgather_conv_w")
    conv_cat = jnp.concatenate([conv_all[2 * j, :depth * CONV_TAPS] for j in range(N_CHIPS)], axis=1)
    small = {name: w[name] for name, _ in SMALL}
    small["conv_w"] = conv_cat.reshape(depth, CONV_TAPS, CONV_WIDTH)

    sq, grad_x, grads, landing = _local_step(x[0], p[:, 0], positions[0], loss_target[0], slabs, small)

    halves = [_reduce_partials(jnp.stack([grads[l][name] for l in range(depth)]), landing[name], chip_arr, c_arr,
                               "grad_reduce_" + name) for name in names]
    g_out = dict(zip(names, _join_halves(halves)))

    reduced = SMALL + (conv_full, ("loss", LANES))
    stacked = {name: jnp.stack([grads[l][name] for l in range(depth)]) for name, _ in SMALL + (conv_full,)}
    stacked["loss"] = jnp.broadcast_to(sq[:1] * (0.5 / D_MODEL), (depth, LANES))
    g_small = _unpack_rows(_sum_devices(_allgather_small(_pack_rows(stacked, reduced), "allgather_small_grads")), reduced)
    loss = g_small["loss"][0, 0]
    for name, n in SMALL:
        g_out[name] = g_small[name]
    g_conv = g_small["conv_w"].reshape(depth, CONV_TAPS, CONV_WIDTH)
    g_out["conv_w"] = lax.dynamic_slice_in_dim(g_conv, chip * LANES, LANES, axis=2)

    delta, new_m, new_v = {}, {}, {}
    for name in names:
        view = (lambda t: jnp.swapaxes(t, 1, 2)) if w[name].shape[2] % LANES else (lambda t: t)
        d, m2, v2 = _adamw(view(w[name]), view(g_out[name]), view(m[name]), view(v[name]), "adamw_" + name)
        delta[name], new_m[name], new_v[name] = view(d), view(m2), view(v2)
    local = SMALL + (conv_shard,)
    pack = lambda vals: _pack_rows(vals, local)[None]
    d, m2, v2 = _adamw(pack(w), pack(g_out), pack(m), pack(v), "adamw_small")
    for res, packed_res in ((delta, d), (new_m, m2), (new_v, v2)):
        un = _unpack_rows(packed_res[0], local)
        for name, _ in local:
            res[name] = un[name].reshape(w[name].shape)

    return (loss, grad_x[None], *[g_out[n] for n in WEIGHT_ORDER], *[delta[n] for n in WEIGHT_ORDER],
            *[new_m[n] for n in WEIGHT_ORDER], *[new_v[n] for n in WEIGHT_ORDER])
```

```python
import functools

import jax
import jax.numpy as jnp
from jax import lax
from jax.experimental import pallas as pl
from jax.experimental.pallas import tpu as pltpu

F32 = jnp.float32
BF16 = jnp.bfloat16
MESH = pl.DeviceIdType.MESH

D_MODEL = 1024
N_HEADS = 8
QK_NOPE = 64
QK_ROPE = 32
QK_HEAD = QK_NOPE + QK_ROPE
V_HEAD = 64
Q_LORA = 384
KV_LORA = 256
ATTN_WIDTH = N_HEADS * V_HEAD
CONV_WIDTH = 512
CONV_TAPS = 3
D_FF = 4096
PLE_DIM = 256
ROPE_THETA = 10000.0
EPS = 1e-6
ATT_SCALE = QK_HEAD ** -0.5
LOG2E = 1.4426950408889634
ATT_SCALE_LOG2 = ATT_SCALE * LOG2E

ADAM_LR = 0.001
ADAM_B1 = 0.9
ADAM_B2 = 0.999
ADAM_EPS = 1e-08
ADAM_WD = 0.01
ADAM_STEP = 10

LANES = 128
SUBLANES = 8
HEAD_PAD = LANES
HP = N_HEADS * HEAD_PAD
ROPE_LO = QK_NOPE
ROPE_MID = QK_NOPE + QK_ROPE // 2
ROPE_HI = QK_NOPE + QK_ROPE
VMEM_LIMIT = 56 * 1024 * 1024

Z_Q, Z_KV, Z_KPE, Z_GB, Z_GC, Z_XIN = 0, 384, 640, 768, 1280, 1792
Z_COLS = 2304
Z_LAT = Z_KPE

ROW_TILE = 512
ATT_TILE = 256
ATT_CHAINS_FWD, ATT_CHAINS_BWD = 8, 4
MM_TM, MM_TN, MM_TK = 1024, 1024, 4096
MM_TM_MIN, MM_TK_MIN = 256, 512
MM_VMEM_BUDGET = 40 * 1024 * 1024

N_CHIPS = 4
N_DEV = 8

PRE = ("w_in", "w_uq", "w_ukv")
POST = ("w_o", "w_up", "w_down", "w_ple_gate", "w_ple")
ROW_SHARDED = ("w_o", "w_down", "w_ple_gate")
SMALL = (
    ("g_mix", 1024), ("g_q_lat", 384), ("g_kv_lat", 256), ("g_qn_nope", 64), ("g_qn_rope", 32), ("g_kn_nope", 64),
    ("g_kn_rope", 32), ("g_out_attn", 512), ("g_out_conv", 512), ("g_mlp", 1024), ("g_ple", 1024),
)
WEIGHT_ORDER = ("g_mix", "w_in", "g_q_lat", "w_uq", "g_kv_lat", "w_ukv", "g_qn_nope", "g_qn_rope", "g_kn_nope",
                "g_kn_rope", "conv_w", "g_out_attn", "g_out_conv", "w_o", "g_mlp", "w_up", "w_down", "g_ple",
                "w_ple_gate", "w_ple")


def _pallas(body, **kw):
    return pl.pallas_call(body, **kw)


def _cparams(*sem):
    return pltpu.CompilerParams(dimension_semantics=sem, vmem_limit_bytes=VMEM_LIMIT)


def _tile(dim, pref, unit=LANES):
    if dim <= pref:
        return dim
    t = (pref // unit) * unit
    while t > unit and dim % t:
        t -= unit
    assert dim % t == 0, (dim, pref)
    return t


def _pad128(n):
    return -(-n // LANES) * LANES


_DIMS = {"nn": (((1,), (0,)), ((), ())), "nt": (((1,), (1,)), ((), ())), "tn": (((0,), (0,)), ((), ()))}


def _mm(a, b, *, mode="nn", res=None, aux=None, gain=None, epi=None, out_dtype=F32, shard_out=None, rider=None, name):
    whole_rows = epi in ("rms", "rms_bwd")
    b, layer = b if isinstance(b, tuple) else (b, None)
    b_shape = b.shape if layer is None else b.shape[1:]
    if mode == "nn":
        (M, K), (K2, N) = a.shape, b_shape
    elif mode == "nt":
        (M, K), (N, K2) = a.shape, b_shape
    else:
        (K, M), (K2, N) = a.shape, b_shape
    assert K == K2, (a.shape, b_shape, mode)
    n_lim = k_lim = None
    m_lim = M // N_CHIPS if shard_out == 0 else None
    if shard_out == 1:
        n_lim = N // N_CHIPS
    extra = [t for t in (res, aux) if t is not None]
    out_bytes = 6 if epi in ("relu2", "rms", "rms_bwd") else jnp.dtype(out_dtype).itemsize
    if whole_rows:
        assert (n_lim or N) == N and N <= MM_TN and shard_out is None and gain is not None
        m_lim = min(M, MM_TM // 2)

    def footprint(tm, tn, tk):
        blocks = tm * tk * a.dtype.itemsize + tk * tn * 2 + tm * tn * (out_bytes + 4 * len(extra))
        return 2 * blocks + (tm * tn * 4 if tk < K else 0)

    tm, tn, tk = _tile(m_lim or M, MM_TM), _tile(n_lim or N, MM_TN), _tile(k_lim or K, MM_TK)
    while footprint(tm, tn, tk) > MM_VMEM_BUDGET and tk > MM_TK_MIN:
        tk = _tile(K, tk // 2)
    while footprint(tm, tn, tk) > MM_VMEM_BUDGET and tm > MM_TM_MIN:
        tm = _tile(M, tm // 2)
    nk = K // tk
    a_spec = pl.BlockSpec((tk, tm), lambda i, j, k: (k, i)) if mode == "tn" else pl.BlockSpec((tm, tk), lambda i, j, k: (i, k))
    if mode == "nt":
        b_block, b_rc = (tn, tk), (lambda i, j, k: (j, k))
    else:
        b_block, b_rc = (tk, tn), (lambda i, j, k: (k, j))
    if layer is None:
        b_spec = pl.BlockSpec(b_block, b_rc)
    else:
        b_spec = pl.BlockSpec((None,) + b_block, lambda i, j, k: (layer,) + b_rc(i, j, k))
    mn_spec = pl.BlockSpec((tm, tn), lambda i, j, k: (i, j))
    dims = _DIMS[mode]
    n_out = {"relu2": 2, "rms": 2, "rms_bwd": 3}.get(epi, 1)

    n_in = 2 + len(extra) + whole_rows
    n_grid = (M // tm, N // tn, nk)

    def body(*refs):
        a_ref, b_ref = refs[0], refs[1]
        extra_refs = refs[2:2 + len(extra)]
        gain_ref = refs[2 + len(extra)] if whole_rows else None
        if rider is None:
            tail = refs[n_in:]
        else:
            tail, ride = rider.split(refs[n_in:], n_out, int(nk > 1))
            ride((pl.program_id(0) * n_grid[1] + pl.program_id(1)) * nk + pl.program_id(2), n_grid[0] * n_grid[1] * nk)
        out_refs = tail[:n_out]
        prod = lax.dot_general(a_ref[...].astype(BF16), b_ref[...].astype(BF16), dims, preferred_element_type=F32)

        def finish(r):
            if epi == "rms_bwd":
                dx, dgc = _norm_bwd(extra_refs[1][...], r, gain_ref[...], N)
                dx = dx + extra_refs[0][...]
                out_refs[0][...] = dx
                out_refs[1][...] = dx.astype(BF16)
                _accumulate(out_refs[2], dgc)
                return
            if res is not None:
                r = r + extra_refs[0][...]
            if epi == "rms":
                out_refs[0][...] = r
                out_refs[1][...] = (r * _rinv(r, N) * gain_ref[...]).astype(BF16)
            elif epi == "relu2":
                out_refs[0][...] = r.astype(out_dtype)
                t = jnp.maximum(r, 0.0)
                out_refs[1][...] = (t * t).astype(BF16)
            elif epi == "drelu2":
                out_refs[0][...] = (r * (2.0 * jnp.maximum(extra_refs[-1][...].astype(F32), 0.0))).astype(out_dtype)
            else:
                out_refs[0][...] = r.astype(out_dtype)

        if nk == 1:
            finish(prod)
        else:
            acc = tail[n_out]
            k = pl.program_id(2)

            @pl.when(k == 0)
            def _():
                acc[...] = prod

            @pl.when(k > 0)
            def _():
                acc[...] += prod

            @pl.when(k == nk - 1)
            def _():
                finish(acc[...])

    if epi in ("relu2", "rms", "rms_bwd"):
        out_shape = [jax.ShapeDtypeStruct((M, N), out_dtype if epi == "relu2" else F32), jax.ShapeDtypeStruct((M, N), BF16)]
        out_specs = [mn_spec, mn_spec]
        if epi == "rms_bwd":
            out_shape.append(jax.ShapeDtypeStruct((SUBLANES, N), F32))
            out_specs.append(pl.BlockSpec((SUBLANES, tn), lambda i, j, k: (0, j)))
    elif shard_out == 0:
        per = (M // N_CHIPS) // tm
        out_shape = jax.ShapeDtypeStruct((N_CHIPS, M // N_CHIPS, N), out_dtype)
        out_specs = pl.BlockSpec((None, tm, tn), lambda i, j, k: (i // per, i % per, j))
    elif shard_out == 1:
        per = (N // N_CHIPS) // tn
        out_shape = jax.ShapeDtypeStruct((N_CHIPS, M, N // N_CHIPS), out_dtype)
        out_specs = pl.BlockSpec((None, tm, tn), lambda i, j, k: (j // per, i, j % per))
    else:
        out_shape = jax.ShapeDtypeStruct((M, N), out_dtype)
        out_specs = mn_spec
    gains = [gain] if whole_rows else []
    in_specs = [a_spec, b_spec] + [mn_spec] * len(extra) + [pl.BlockSpec((1, tn), lambda i, j, k: (0, j))] * len(gains)
    scratch = [pltpu.VMEM((tm, tn), F32)] if nk > 1 else []
    ordered = epi == "rms_bwd" or rider is not None
    params = _cparams(*(["arbitrary"] * 3 if ordered else ["parallel", "parallel", "arbitrary"]))
    if rider is None:
        return _pallas(body, name=name, grid=n_grid, in_specs=in_specs, out_specs=out_specs, out_shape=out_shape,
                       scratch_shapes=scratch, compiler_params=params)(a, b, *extra, *gains)
    as_list = lambda t: list(t) if isinstance(t, (list, tuple)) else [t]
    outs = _pallas(body, name=name, grid=n_grid, in_specs=in_specs + rider.in_specs(),
                   out_specs=as_list(out_specs) + rider.out_specs(), out_shape=as_list(out_shape) + rider.out_shape(),
                   scratch_shapes=scratch + rider.scratch(), input_output_aliases=rider.aliases(n_in, n_out),
                   compiler_params=params)(a, b, *extra, *gains, *rider.args())
    return (*outs[:n_out], outs[n_out:])


def _rows(ts, d, col=0):
    return pl.BlockSpec((ts, d), lambda i: (i, col))


def _gain(d):
    return pl.BlockSpec((1, d), lambda i: (0, 0))


def _accum(d):
    return pl.BlockSpec((SUBLANES, d), lambda i: (0, 0))


def _accumulate(ref, val):
    i = pl.program_id(0)

    @pl.when(i == 0)
    def _():
        ref[...] = jnp.zeros_like(ref)

    ref[...] += jnp.broadcast_to(jnp.sum(val, axis=0, keepdims=True), ref.shape)


def _rinv(x, n):
    return lax.rsqrt(jnp.sum(x * x, axis=-1, keepdims=True) / n + EPS)


def _norm_bwd(x, dy, g, n):
    r = _rinv(x, n)
    xhat = x * r
    dyg = dy * g
    dx = r * (dyg - xhat * (jnp.sum(dyg * xhat, axis=-1, keepdims=True) / n))
    return dx, dy * xhat


def _rms_fwd(x, g, name):
    S, D = x.shape
    ts = _tile(S, ROW_TILE, SUBLANES)

    def body(x_ref, g_ref, h_ref):
        xv = x_ref[...]
        h_ref[...] = (xv * _rinv(xv, D) * g_ref[...]).astype(BF16)

    return _pallas(body, name=name, grid=(S // ts,), in_specs=[_rows(ts, D), _gain(D)], out_specs=_rows(ts, D),
                   out_shape=jax.ShapeDtypeStruct((S, D), BF16), compiler_params=_cparams("parallel"))(x, g)


def _lat_fwd(z, gq, gkv):
    S = z.shape[0]
    ts = _tile(S, ROW_TILE, SUBLANES)

    def body(z_ref, gq_ref, gkv_ref, q_ref, kv_ref):
        zq = z_ref[:, Z_Q:Z_KV]
        zkv = z_ref[:, Z_KV:Z_KPE]
        q_ref[...] = (zq * _rinv(zq, Q_LORA) * gq_ref[...]).astype(BF16)
        kv_ref[...] = (zkv * _rinv(zkv, KV_LORA) * gkv_ref[...]).astype(BF16)

    return _pallas(body, name="lat_fwd", grid=(S // ts,), in_specs=[_rows(ts, Z_LAT), _gain(Q_LORA), _gain(KV_LORA)],
                   out_specs=[_rows(ts, Q_LORA), _rows(ts, KV_LORA)],
                   out_shape=[jax.ShapeDtypeStruct((S, Q_LORA), BF16), jax.ShapeDtypeStruct((S, KV_LORA), BF16)],
                   compiler_params=_cparams("parallel"))(z, gq, gkv)


def _lat_bwd(dq, dkv, z, gq, gkv):
    S = z.shape[0]
    ts = _tile(S, ROW_TILE, SUBLANES)

    def body(dq_ref, dkv_ref, z_ref, gq_ref, gkv_ref, dlat_ref, dgq_ref, dgkv_ref):
        dxq, cq = _norm_bwd(z_ref[:, Z_Q:Z_KV], dq_ref[...], gq_ref[...], Q_LORA)
        dxkv, ckv = _norm_bwd(z_ref[:, Z_KV:Z_KPE], dkv_ref[...], gkv_ref[...], KV_LORA)
        dlat_ref[:, Z_Q:Z_KV] = dxq.astype(BF16)
        dlat_ref[:, Z_KV:Z_KPE] = dxkv.astype(BF16)
        _accumulate(dgq_ref, cq)
        _accumulate(dgkv_ref, ckv)

    return _pallas(body, name="lat_bwd", grid=(S // ts,),
                   in_specs=[_rows(ts, Q_LORA), _rows(ts, KV_LORA), _rows(ts, Z_LAT), _gain(Q_LORA), _gain(KV_LORA)],
                   out_specs=[_rows(ts, Z_LAT), _accum(Q_LORA), _accum(KV_LORA)],
                   out_shape=[jax.ShapeDtypeStruct((S, Z_LAT), BF16), jax.ShapeDtypeStruct((SUBLANES, Q_LORA), F32),
                              jax.ShapeDtypeStruct((SUBLANES, KV_LORA), F32)],
                   compiler_params=_cparams("arbitrary"))(dq, dkv, z, gq, gkv)


def _head_masks():
    lane = lax.broadcasted_iota(jnp.int32, (1, HEAD_PAD), 1)
    return lane < ROPE_LO, (lane >= ROPE_LO) & (lane < ROPE_HI)


def _head_matrices():
    k = jnp.arange(HEAD_PAD)[:, None]
    j = jnp.arange(HEAD_PAD)[None, :]
    nope = (k < ROPE_LO) & (j < ROPE_LO)
    rope = (k >= ROPE_LO) & (k < ROPE_HI) & (j >= ROPE_LO) & (j < ROPE_HI)
    seg = jnp.where(nope, 1.0 / QK_NOPE, jnp.where(rope, 1.0 / QK_ROPE, 0.0))
    half = QK_ROPE // 2
    rot = jnp.where((j >= ROPE_LO) & (j < ROPE_MID) & (k == j + half), -1.0,
                    jnp.where((j >= ROPE_MID) & (j < ROPE_HI) & (k == j - half), 1.0, 0.0))
    return seg.astype(BF16), rot.astype(BF16)


_HEAD_MAT = pl.BlockSpec((HEAD_PAD, HEAD_PAD), lambda i: (0, 0))


def _split_dot(t, mat):
    hi = t.astype(BF16)
    lo = (t - hi.astype(F32)).astype(BF16)
    return jnp.dot(hi, mat, preferred_element_type=F32) + jnp.dot(lo, mat, preferred_element_type=F32)


def _qk_fwd(q_raw, kv_raw, z, cs, sn, gq, gk, seg, rot):
    S = q_raw.shape[0]
    ts = _tile(S, ROW_TILE, SUBLANES)

    def body(q_ref, k_ref, v_ref, kpe_ref, cs_ref, sn_ref, gq_ref, gk_ref, seg_ref, rot_ref, qf_ref, kf_ref, vb_ref):
        cos, sin, gqv, gkv = cs_ref[...], sn_ref[...], gq_ref[...], gk_ref[...]
        seg_m, rot_m = seg_ref[...], rot_ref[...]

        def norm(x, g):
            return x * lax.rsqrt(_split_dot(x * x, seg_m) + EPS) * g

        def rope(y):
            return y * cos + _split_dot(y, rot_m) * sin

        kr = rope(norm(kpe_ref[...], gkv))
        lane = lax.broadcasted_iota(jnp.int32, (1, HEAD_PAD), 1)
        for h in range(N_HEADS):
            sl = slice(h * HEAD_PAD, (h + 1) * HEAD_PAD)
            qf_ref[:, sl] = rope(norm(q_ref[:, sl], gqv)).astype(BF16)
            kf_ref[:, sl] = (norm(k_ref[:, sl], gkv) + kr).astype(BF16)
            vb_ref[:, sl] = jnp.where(lane == V_HEAD, 1.0, v_ref[:, sl]).astype(BF16)

    hd = jax.ShapeDtypeStruct((S, HP), BF16)
    return _pallas(body, name="qk_fwd", grid=(S // ts,),
                   in_specs=[_rows(ts, HP), _rows(ts, HP, 0), _rows(ts, HP, 1), _rows(ts, HEAD_PAD, Z_KPE // HEAD_PAD),
                             _rows(ts, HEAD_PAD), _rows(ts, HEAD_PAD), _gain(HEAD_PAD), _gain(HEAD_PAD), _HEAD_MAT, _HEAD_MAT],
                   out_specs=[_rows(ts, HP)] * 3, out_shape=[hd, hd, hd],
                   compiler_params=_cparams("parallel"))(q_raw, kv_raw, kv_raw, z, cs, sn, gq, gk, seg, rot)


def _qk_bwd(dqf, dkf, dv, q_raw, kv_raw, z, cs, sn, gq, gk, seg, rot):
    S = q_raw.shape[0]
    ts = _tile(S, ROW_TILE, SUBLANES)

    def body(dqf_ref, dkf_ref, dv_ref, q_ref, k_ref, kpe_ref, cs_ref, sn_ref, gq_ref, gk_ref, seg_ref, rot_ref,
             dq_ref, dkv_ref, dkpe_ref, dgq_ref, dgk_ref):
        m_n, m_r = _head_masks()
        cos, sin, gqv, gkv = cs_ref[...], sn_ref[...], gq_ref[...], gk_ref[...]
        seg_m, rot_m = seg_ref[...], rot_ref[...]

        def rope_t(w):
            return w * cos - _split_dot(w * sin, rot_m)

        def norm_bwd(x, dy, g):
            r = lax.rsqrt(_split_dot(x * x, seg_m) + EPS)
            xhat = x * r
            dyg = dy * g
            return r * (dyg - xhat * _split_dot(dyg * xhat, seg_m)), dy * xhat

        accq = jnp.zeros((ts, HEAD_PAD), F32)
        acck = jnp.zeros((ts, HEAD_PAD), F32)
        dkr = jnp.zeros((ts, HEAD_PAD), F32)
        for h in range(N_HEADS):
            sl = slice(h * HEAD_PAD, (h + 1) * HEAD_PAD)
            dx, c = norm_bwd(q_ref[:, sl], rope_t(dqf_ref[:, sl]), gqv)
            dq_ref[:, sl] = dx.astype(BF16)
            accq = accq + c
            dk = dkf_ref[:, sl]
            dkr = dkr + jnp.where(m_r, dk, 0.0)
            dx, c = norm_bwd(k_ref[:, sl], jnp.where(m_n, dk, 0.0), gkv)
            dkv_ref[:, sl] = dx.astype(BF16)
            acck = acck + c
            dkv_ref[:, HP + h * HEAD_PAD:HP + (h + 1) * HEAD_PAD] = dv_ref[:, sl].astype(BF16)
        dx, c = norm_bwd(kpe_ref[...], rope_t(dkr), gkv)
        dkpe_ref[...] = dx.astype(BF16)
        _accumulate(dgq_ref, accq)
        _accumulate(dgk_ref, acck + c)

    return _pallas(body, name="qk_bwd", grid=(S // ts,),
                   in_specs=[_rows(ts, HP)] * 4 + [_rows(ts, HP, 0), _rows(ts, HEAD_PAD, Z_KPE // HEAD_PAD),
                                                   _rows(ts, HEAD_PAD), _rows(ts, HEAD_PAD), _gain(HEAD_PAD), _gain(HEAD_PAD),
                                                   _HEAD_MAT, _HEAD_MAT],
                   out_specs=[_rows(ts, HP), _rows(ts, 2 * HP), _rows(ts, HEAD_PAD), _accum(HEAD_PAD), _accum(HEAD_PAD)],
                   out_shape=[jax.ShapeDtypeStruct((S, HP), BF16), jax.ShapeDtypeStruct((S, 2 * HP), BF16),
                              jax.ShapeDtypeStruct((S, HEAD_PAD), BF16), jax.ShapeDtypeStruct((SUBLANES, HEAD_PAD), F32),
                              jax.ShapeDtypeStruct((SUBLANES, HEAD_PAD), F32)],
                   compiler_params=_cparams("arbitrary"))(dqf, dkf, dv, q_raw, kv_raw, z, cs, sn, gq, gk, seg, rot)


def _causal(t):
    row = lax.broadcasted_iota(jnp.int32, (t, t), 0)
    col = lax.broadcasted_iota(jnp.int32, (t, t), 1)
    return col <= row


def _attn_tiles(S, chains):
    t = _tile(S, ATT_TILE, SUBLANES)
    return t, min(chains, S // t)


def _attn_fwd(qf, kf, vb, rider):
    S = qf.shape[0]
    t, nc = _attn_tiles(S, ATT_CHAINS_FWD)
    tq = nc * t
    nqt = S // tq

    def body(q_ref, k_ref, v_ref, *rest):
        (o_ref, lse_ref, m_ref, acc_ref), ride = rider.split(rest, 2, 2)
        qt = pl.program_id(1)
        ride(pl.program_id(0) * nqt + qt, N_HEADS * nqt)
        m_ref[...] = jnp.full_like(m_ref, -jnp.inf)
        acc_ref[...] = jnp.zeros_like(acc_ref)

        def against(rows, first, diagonal):
            kb, vb_ = k_ref[rows, :], v_ref[rows, :]
            subs = [slice(a * t, (a + 1) * t) for a in range(first, nc)]
            s = [lax.dot_general(q_ref[sub, :], kb, _DIMS["nt"], preferred_element_type=F32) * ATT_SCALE_LOG2 for sub in subs]
            if diagonal:
                s[0] = jnp.where(_causal(t), s[0], -jnp.inf)
            m_old = [m_ref[sub, :] for sub in subs]
            m_new = [jnp.maximum(mo, jnp.max(sa, axis=-1, keepdims=True)) for mo, sa in zip(m_old, s)]
            p = [jnp.exp2(sa - jnp.concatenate([mn] * (t // HEAD_PAD), axis=1)).astype(BF16) for sa, mn in zip(s, m_new)]
            for sub, mo, mn, pa in zip(subs, m_old, m_new, p):
                acc_ref[sub, :] = jnp.exp2(mo - mn) * acc_ref[sub, :] + jnp.dot(pa, vb_, preferred_element_type=F32)
                m_ref[sub, :] = mn

        def trip(j, carry):
            against(pl.ds(pl.multiple_of(j * t, t), t), 0, False)
            return carry

        lax.fori_loop(0, nc * qt, trip, 0)
        for d in range(nc):
            against(pl.ds(pl.multiple_of((nc * qt + d) * t, t), t), d, True)
        acc = acc_ref[...]
        l = acc[:, V_HEAD:V_HEAD + 1]
        lane = lax.broadcasted_iota(jnp.int32, (1, HEAD_PAD), 1)
        o_ref[...] = jnp.where(lane < V_HEAD, acc / l, 0.0)
        lse_ref[...] = m_ref[...] + jnp.log(l) * LOG2E

    tile = pl.BlockSpec((tq, HEAD_PAD), lambda h, i: (i, h))
    full = pl.BlockSpec((S, HEAD_PAD), lambda h, i: (0, h))
    outs = _pallas(body, name="attn_fwd", grid=(N_HEADS, nqt), in_specs=[tile, full, full] + rider.in_specs(),
                   out_specs=[tile, pl.BlockSpec((None, tq, HEAD_PAD), lambda h, i: (h, i, 0))] + rider.out_specs(),
                   out_shape=[jax.ShapeDtypeStruct((S, HP), F32), jax.ShapeDtypeStruct((N_HEADS, S, HEAD_PAD), F32)]
                   + rider.out_shape(),
                   scratch_shapes=[pltpu.VMEM((tq, HEAD_PAD), F32), pltpu.VMEM((tq, HEAD_PAD), F32)] + rider.scratch(),
                   input_output_aliases=rider.aliases(3, 2),
                   compiler_params=_cparams("arbitrary", "arbitrary"))(qf, kf, vb, *rider.args())
    return outs[0], outs[1], outs[2:]


def _attn_bwd(qf, kf, vb, o, do, lse, rider):
    S = qf.shape[0]
    t, nc = _attn_tiles(S, ATT_CHAINS_BWD)
    tkv = nc * t
    nq = S // t
    nkt = S // tkv

    def body(q_ref, k_ref, v_ref, o_ref, do_ref, lse_ref, *rest):
        (dq_ref, dk_ref, dv_ref), ride = rider.split(rest, 3, 0)
        kt = pl.program_id(1)
        ride(pl.program_id(0) * nkt + kt, N_HEADS * nkt)

        @pl.when(kt == 0)
        def _():
            dq_ref[...] = jnp.zeros_like(dq_ref)

        dk_ref[...] = jnp.zeros_like(dk_ref)
        dv_ref[...] = jnp.zeros_like(dv_ref)

        def q_block(rows):
            dof = do_ref[rows, :]
            delta = jnp.sum(dof * o_ref[rows, :], axis=-1, keepdims=True)
            return q_ref[rows, :], dof.astype(BF16), lse_ref[rows, :][:, :1], delta

        def against(rows, n_sub, diagonal):
            q, dob, lse, delta = q_block(rows)
            subs = [slice(b * t, (b + 1) * t) for b in range(n_sub)]
            kbs = [k_ref[sub, :] for sub in subs]
            s = [lax.dot_general(q, kb, _DIMS["nt"], preferred_element_type=F32) for kb in kbs]
            dp = [lax.dot_general(dob, v_ref[sub, :], _DIMS["nt"], preferred_element_type=F32) for sub in subs]
            p = [jnp.exp2(sb * ATT_SCALE_LOG2 - lse) for sb in s]
            if diagonal:
                p[-1] = jnp.where(_causal(t), p[-1], 0.0)
            ds = [(pb * (dpb - delta) * ATT_SCALE).astype(BF16) for pb, dpb in zip(p, dp)]
            for sub, pb in zip(subs, p):
                dv_ref[sub, :] += lax.dot_general(pb.astype(BF16), dob, _DIMS["tn"], preferred_element_type=F32)
            for sub, dsb in zip(subs, ds):
                dk_ref[sub, :] += lax.dot_general(dsb, q, _DIMS["tn"], preferred_element_type=F32)
            dq_ref[rows, :] += sum(jnp.dot(dsb, kb, preferred_element_type=F32) for dsb, kb in zip(ds, kbs))

        for a in range(nc):
            against(pl.ds(pl.multiple_of((nc * kt + a) * t, t), t), a + 1, True)

        def trip(i, carry):
            against(pl.ds(pl.multiple_of(i * t, t), t), nc, False)
            return carry

        lax.fori_loop(nc * (kt + 1), nq, trip, 0)

    tile = pl.BlockSpec((tkv, HEAD_PAD), lambda h, j: (j, h))
    full = pl.BlockSpec((S, HEAD_PAD), lambda h, j: (0, h))
    hd = jax.ShapeDtypeStruct((S, HP), F32)
    outs = _pallas(body, name="attn_bwd", grid=(N_HEADS, nkt),
                   in_specs=[full, tile, tile, full, full, pl.BlockSpec((None, S, HEAD_PAD), lambda h, j: (h, 0, 0))]
                   + rider.in_specs(),
                   out_specs=[full, tile, tile] + rider.out_specs(), out_shape=[hd, hd, hd] + rider.out_shape(),
                   scratch_shapes=rider.scratch(), input_output_aliases=rider.aliases(6, 3),
                   compiler_params=_cparams("arbitrary", "arbitrary"))(qf, kf, vb, o, do, lse, *rider.args())
    return outs[0], outs[1], outs[2], outs[3:]


def _shift_down(u, j, row):
    return jnp.where(row >= j, pltpu.roll(u, j, 0), 0.0)


def _shift_up(u, j, row, s):
    return jnp.where(row < s - j, pltpu.roll(u, s - j, 0), 0.0)


def _conv_cols(s, first_tile):
    return pl.BlockSpec((s, LANES), lambda cb: (0, first_tile + cb))


def _conv_fwd(z, cw8):
    S = z.shape[0]

    def body(gb_ref, gc_ref, xin_ref, w_ref, out_ref):
        row = lax.broadcasted_iota(jnp.int32, (S, LANES), 0)
        u = gc_ref[...] * xin_ref[...]
        y = w_ref[0:1, :] * u
        for j in range(1, CONV_TAPS):
            y = y + w_ref[j:j + 1, :] * _shift_down(u, j, row)
        out_ref[...] = gb_ref[...] * y

    return _pallas(body, name="conv_fwd", grid=(CONV_WIDTH // LANES,),
                   in_specs=[_conv_cols(S, Z_GB // LANES), _conv_cols(S, Z_GC // LANES), _conv_cols(S, Z_XIN // LANES),
                             pl.BlockSpec((SUBLANES, LANES), lambda cb: (0, cb))],
                   out_specs=_conv_cols(S, 0), out_shape=jax.ShapeDtypeStruct((S, CONV_WIDTH), F32),
                   compiler_params=_cparams("parallel"))(z, z, z, cw8)


def _conv_bwd(dconv, z, cw8):
    S = z.shape[0]

    def body(d_ref, gb_ref, gc_ref, xin_ref, w_ref, dgb_ref, dgc_ref, dxin_ref, dw_ref):
        row = lax.broadcasted_iota(jnp.int32, (S, LANES), 0)
        gc, xin, d = gc_ref[...], xin_ref[...], d_ref[...]
        u = gc * xin
        dy = d * gb_ref[...]
        y = w_ref[0:1, :] * u
        du = w_ref[0:1, :] * dy
        dw = [jnp.sum(dy * u, axis=0, keepdims=True)]
        for j in range(1, CONV_TAPS):
            uj = _shift_down(u, j, row)
            y = y + w_ref[j:j + 1, :] * uj
            du = du + w_ref[j:j + 1, :] * _shift_up(dy, j, row, S)
            dw.append(jnp.sum(dy * uj, axis=0, keepdims=True))
        dgb_ref[...] = (d * y).astype(BF16)
        dgc_ref[...] = (du * xin).astype(BF16)
        dxin_ref[...] = (du * gc).astype(BF16)
        tap = lax.broadcasted_iota(jnp.int32, (SUBLANES, LANES), 0)
        dw_ref[...] = sum(jnp.where(tap == j, dw[j], 0.0) for j in range(CONV_TAPS))

    col = _conv_cols(S, 0)
    sd = jax.ShapeDtypeStruct((S, CONV_WIDTH), BF16)
    return _pallas(body, name="conv_bwd", grid=(CONV_WIDTH // LANES,),
                   in_specs=[col, _conv_cols(S, Z_GB // LANES), _conv_cols(S, Z_GC // LANES), _conv_cols(S, Z_XIN // LANES),
                             pl.BlockSpec((SUBLANES, LANES), lambda cb: (0, cb))],
                   out_specs=[col, col, col, pl.BlockSpec((SUBLANES, LANES), lambda cb: (0, cb))],
                   out_shape=[sd, sd, sd, jax.ShapeDtypeStruct((SUBLANES, CONV_WIDTH), F32)],
                   compiler_params=_cparams("parallel"))(dconv, z, z, z, cw8)


def _mix_fwd(o, conv, ga, gc):
    S = o.shape[0]
    ts = _tile(S, ROW_TILE, SUBLANES)

    def body(o_ref, c_ref, ga_ref, gc_ref, out_ref):
        ov, cv = o_ref[...], c_ref[...]
        out_ref[:, :HP] = (ov * _rinv(ov, ATTN_WIDTH) * ga_ref[...]).astype(BF16)
        out_ref[:, HP:] = (cv * _rinv(cv, CONV_WIDTH) * gc_ref[...]).astype(BF16)

    return _pallas(body, name="mix_fwd", grid=(S // ts,),
                   in_specs=[_rows(ts, HP), _rows(ts, CONV_WIDTH), _gain(HP), _gain(CONV_WIDTH)],
                   out_specs=_rows(ts, HP + CONV_WIDTH), out_shape=jax.ShapeDtypeStruct((S, HP + CONV_WIDTH), BF16),
                   compiler_params=_cparams("parallel"))(o, conv, ga, gc)


def _mix_bwd(dmixed, o, conv, ga, gc):
    S = o.shape[0]
    ts = _tile(S, ROW_TILE, SUBLANES)

    def body(d_ref, o_ref, c_ref, ga_ref, gc_ref, do_ref, dc_ref, dga_ref, dgc_ref):
        dx, ca = _norm_bwd(o_ref[...], d_ref[:, :HP], ga_ref[...], ATTN_WIDTH)
        do_ref[...] = dx
        dx, cc = _norm_bwd(c_ref[...], d_ref[:, HP:], gc_ref[...], CONV_WIDTH)
        dc_ref[...] = dx
        _accumulate(dga_ref, ca)
        _accumulate(dgc_ref, cc)

    return _pallas(body, name="mix_bwd", grid=(S // ts,),
                   in_specs=[_rows(ts, HP + CONV_WIDTH), _rows(ts, HP), _rows(ts, CONV_WIDTH), _gain(HP), _gain(CONV_WIDTH)],
                   out_specs=[_rows(ts, HP), _rows(ts, CONV_WIDTH), _accum(HP), _accum(CONV_WIDTH)],
                   out_shape=[jax.ShapeDtypeStruct((S, HP), F32), jax.ShapeDtypeStruct((S, CONV_WIDTH), F32),
                              jax.ShapeDtypeStruct((SUBLANES, HP), F32), jax.ShapeDtypeStruct((SUBLANES, CONV_WIDTH), F32)],
                   compiler_params=_cparams("arbitrary"))(dmixed, o, conv, ga, gc)


def _ple_fwd(x, gl, pe, g_next):
    S, D = x.shape
    ts = _tile(S, ROW_TILE, SUBLANES)

    def body(x_ref, gl_ref, pe_ref, g_ref, out_ref, h_ref):
        y = x_ref[...] + jax.nn.sigmoid(gl_ref[...]) * pe_ref[...]
        out_ref[...] = y
        h_ref[...] = (y * _rinv(y, D) * g_ref[...]).astype(BF16)

    return _pallas(body, name="ple_fwd", grid=(S // ts,), in_specs=[_rows(ts, D)] * 3 + [_gain(D)], out_specs=[_rows(ts, D)] * 2,
                   out_shape=[jax.ShapeDtypeStruct((S, D), F32), jax.ShapeDtypeStruct((S, D), BF16)],
                   compiler_params=_cparams("parallel"))(x, gl, pe, g_next)


def _ple_bwd(dx, gl, pe):
    S, D = dx.shape
    ts = _tile(S, ROW_TILE, SUBLANES)

    def body(dx_ref, gl_ref, pe_ref, dpe_ref, dgl_ref):
        d = dx_ref[...]
        gate = jax.nn.sigmoid(gl_ref[...])
        dpe_ref[...] = (d * gate).astype(BF16)
        dgl_ref[...] = (d * pe_ref[...] * (gate * (1.0 - gate))).astype(BF16)

    sd = jax.ShapeDtypeStruct((S, D), BF16)
    return _pallas(body, name="ple_bwd", grid=(S // ts,), in_specs=[_rows(ts, D)] * 3, out_specs=[_rows(ts, D)] * 2,
                   out_shape=[sd, sd], compiler_params=_cparams("parallel"))(dx, gl, pe)


def _loss_grad(y, target):
    S, D = y.shape
    ts = _tile(S, ROW_TILE, SUBLANES)

    def body(y_ref, t_ref, dy_ref, sq_ref):
        e = y_ref[...] - t_ref[...]
        dy_ref[...] = e / D

        @pl.when(pl.program_id(0) == 0)
        def _():
            sq_ref[...] = jnp.zeros_like(sq_ref)

        sq_ref[...] += jnp.broadcast_to(jnp.sum(jnp.sum(e * e, axis=1, keepdims=True), axis=0, keepdims=True), sq_ref.shape)

    return _pallas(body, name="loss_grad", grid=(S // ts,), in_specs=[_rows(ts, D)] * 2,
                   out_specs=[_rows(ts, D), _accum(LANES)],
                   out_shape=[jax.ShapeDtypeStruct((S, D), F32), jax.ShapeDtypeStruct((SUBLANES, LANES), F32)],
                   compiler_params=_cparams("arbitrary"))(y, target)


def _adamw(w, g, m, v, name):
    L, R, C = w.shape
    tr = _tile(R, ROW_TILE, SUBLANES)

    def body(w_ref, g_ref, m_ref, v_ref, d_ref, m2_ref, v2_ref):
        gv = g_ref[...]
        m2 = ADAM_B1 * m_ref[...] + (1.0 - ADAM_B1) * gv
        v2 = ADAM_B2 * v_ref[...] + (1.0 - ADAM_B2) * (gv * gv)
        m_hat = m2 / (1.0 - ADAM_B1 ** ADAM_STEP)
        v_hat = v2 / (1.0 - ADAM_B2 ** ADAM_STEP)
        d_ref[...] = -ADAM_LR * (m_hat / (jnp.sqrt(v_hat) + ADAM_EPS) + ADAM_WD * w_ref[...])
        m2_ref[...] = m2
        v2_ref[...] = v2

    sd = jax.ShapeDtypeStruct((L, R, C), F32)
    spec = pl.BlockSpec((None, tr, C), lambda l, i: (l, i, 0))
    return _pallas(body, name=name, grid=(L, R // tr), in_specs=[spec] * 4, out_specs=[spec] * 3, out_shape=[sd, sd, sd],
                   compiler_params=_cparams("parallel", "parallel"))(w, g, m, v)


def _place():
    return lax.axis_index("x"), lax.axis_index("y"), lax.axis_index("c")


def _other_chips(x, y):
    return [(1 - x, y), (x, 1 - y), (1 - x, 1 - y)]


HBM = pl.BlockSpec(memory_space=pl.ANY)
VMEM_SPEC = pl.BlockSpec(memory_space=pltpu.VMEM)


def _remote_copy(send_sems, recv_sems, k, src, dst, to):
    return pltpu.make_async_remote_copy(src_ref=src, dst_ref=dst, send_sem=send_sems.at[k], recv_sem=recv_sems.at[k],
                                        device_id=to, device_id_type=MESH)


def _comm_call(body, name, arrays, out_shapes, n_remote, in_place=False):
    scratch = [pltpu.SemaphoreType.DMA((n_remote,)), pltpu.SemaphoreType.DMA((n_remote,))]
    aliases = {i: i for i in range(len(arrays))} if in_place else {}
    return _pallas(body, name=name, in_specs=[HBM] * len(arrays), out_specs=[HBM] * len(out_shapes), out_shape=out_shapes,
                   scratch_shapes=scratch, input_output_aliases=aliases,
                   compiler_params=pltpu.CompilerParams(has_side_effects=True))(*arrays)


def _shard_slab(w, chip_arr, name, stack_rows):
    L, R, C = w.shape
    tr = _tile(R, ROW_TILE, 16)
    nb = R // tr

    def body(j_ref, w_ref, out_ref):
        out_ref[...] = w_ref[...].astype(BF16)

    if stack_rows:
        out_shape = jax.ShapeDtypeStruct((L, N_CHIPS * R, C), BF16)
        out_spec = pl.BlockSpec((None, tr, C), lambda l, i, j_ref: (l, j_ref[0] * nb + i, 0))
    else:
        out_shape = jax.ShapeDtypeStruct((N_CHIPS, L, R, C), BF16)
        out_spec = pl.BlockSpec((None, None, tr, C), lambda l, i, j_ref: (j_ref[0], l, i, 0))
    grid_spec = pltpu.PrefetchScalarGridSpec(
        num_scalar_prefetch=1, grid=(L, nb), in_specs=[pl.BlockSpec((None, tr, C), lambda l, i, j_ref: (l, i, 0))],
        out_specs=out_spec)
    return _pallas(body, name=name, grid_spec=grid_spec, out_shape=out_shape,
                   compiler_params=_cparams("parallel", "parallel"))(chip_arr, w)


def _shard_dims(slab):
    return (slab.shape[2], slab.shape[3]) if slab.ndim == 4 else (slab.shape[1] // N_CHIPS, slab.shape[2])


def _shard_rows(ref, chip, l, which):
    ks, _ = _shard_dims(ref)
    h = ks // 2
    return ref.at[chip, l, pl.ds(which * h, h)] if len(ref.shape) == 4 else ref.at[l, pl.ds(chip * ks + which * h, h)]


class Rider:
    def __init__(self, reads, inplace, n_sems, stages):
        self.reads, self.inplace, self.n_sems, self.stages = list(reads), list(inplace), n_sems, stages

    def args(self):
        return self.reads + self.inplace

    def in_specs(self):
        return [HBM] * len(self.args())

    def out_specs(self):
        return [HBM] * len(self.inplace)

    def out_shape(self):
        return [jax.ShapeDtypeStruct(a.shape, a.dtype) for a in self.inplace]

    def scratch(self):
        return [pltpu.SemaphoreType.DMA((self.n_sems,)), pltpu.SemaphoreType.DMA((self.n_sems,))] if self.n_sems else []

    def aliases(self, n_host_in, n_host_out):
        return {n_host_in + len(self.reads) + j: n_host_out + j for j in range(len(self.inplace))}

    def split(self, rest, n_host_out, n_host_scratch):
        n_r, n_io = len(self.reads), len(self.inplace)
        reads = rest[:n_r]
        host_out = rest[n_r + n_io:n_r + n_io + n_host_out]
        outs = rest[n_r + n_io + n_host_out:n_r + 2 * n_io + n_host_out]
        host_scratch = rest[n_r + 2 * n_io + n_host_out:n_r + 2 * n_io + n_host_out + n_host_scratch]
        sems = rest[n_r + 2 * n_io + n_host_out + n_host_scratch:]

        def ride(step, n_steps):
            at = [0, n_steps - 1] if len(self.stages) == 2 else [0, (3 * n_steps) // 4, n_steps - 1]
            for s, stage in zip(at, self.stages):
                pl.when(step == s)(functools.partial(stage, reads, outs, *sems))

        return tuple(host_out) + tuple(host_scratch), ride

    def run(self, name):
        n_r, n_io = len(self.reads), len(self.inplace)

        def body(*refs):
            for stage in self.stages:
                stage(refs[:n_r], refs[n_r + n_io:n_r + 2 * n_io], *refs[n_r + 2 * n_io:])

        return _pallas(body, name=name, in_specs=self.in_specs(), out_specs=self.out_specs(), out_shape=self.out_shape(),
                       scratch_shapes=self.scratch(), input_output_aliases=self.aliases(0, 0),
                       compiler_params=pltpu.CompilerParams(has_side_effects=True))(*self.args())


def _allgather_rider(slabs, items):
    def stage(which, reads, outs, send_sems, recv_sems):
        x, y, c = _place()
        me, sibling = (x, y, c), (x, y, 1 - c)
        copy = functools.partial(_remote_copy, send_sems, recv_sems)
        for n, (j, l) in enumerate(items):
            ref = outs[j]
            own = _shard_rows(ref, 2 * x + y, l, c)
            for k, (cx, cy) in enumerate(_other_chips(x, y)):
                arrived = _shard_rows(ref, 2 * cx + cy, l, c)
                if which == 0:
                    copy(6 * n + k, own, own, (cx, cy, c)).start()
                elif which == 1:
                    copy(6 * n + k, arrived, arrived, me).wait_recv()
                    copy(6 * n + 3 + k, arrived, arrived, sibling).start()
                else:
                    passed = _shard_rows(ref, 2 * cx + cy, l, 1 - c)
                    copy(6 * n + 3 + k, passed, passed, me).wait_recv()
                    copy(6 * n + k, own, own, me).wait_send()
                    copy(6 * n + 3 + k, arrived, arrived, me).wait_send()

    return Rider([], slabs, 6 * len(items), [functools.partial(stage, w) for w in range(3)])


def _exchange_rider(grads, landing, items):
    def stage(start, reads, outs, send_sems, recv_sems):
        x, y, c = _place()
        for n, (i, j, l) in enumerate(items):
            g_ref, r_ref = reads[i], outs[j]
            h = g_ref.shape[1] // 2
            for k in range(1, N_DEV):
                px, py, pc = (1 - x if k & 4 else x, 1 - y if k & 2 else y, 1 - c if k & 1 else c)
                cp = _remote_copy(send_sems, recv_sems, 7 * n + k - 1, g_ref.at[2 * px + py, pl.ds(pc * h, h)],
                                  r_ref.at[l, k - 1], (px, py, pc))
                if start:
                    cp.start()
                else:
                    cp.wait()

    return Rider(grads, landing, 7 * len(items), [functools.partial(stage, True), functools.partial(stage, False)])


def _reduce_partials(g, r, chip_arr, c_arr, name):
    L, n, R, C = g.shape
    H = R // 2
    th = _tile(H, ROW_TILE, 16)
    nb = H // th

    def body(j_ref, c_ref, g_ref, r_ref, out_ref):
        s = g_ref[...].astype(F32)
        for k in range(N_DEV - 1):
            s = s + r_ref[k].astype(F32)
        out_ref[...] = s

    grid_spec = pltpu.PrefetchScalarGridSpec(
        num_scalar_prefetch=2, grid=(L, nb),
        in_specs=[pl.BlockSpec((None, None, th, C), lambda l, i, j_ref, c_ref: (l, j_ref[0], c_ref[0] * nb + i, 0)),
                  pl.BlockSpec((None, N_DEV - 1, th, C), lambda l, i, j_ref, c_ref: (l, 0, i, 0))],
        out_specs=pl.BlockSpec((None, th, C), lambda l, i, j_ref, c_ref: (l, c_ref[0] * nb + i, 0)))
    return _pallas(body, name=name, grid_spec=grid_spec, out_shape=jax.ShapeDtypeStruct((L, R, C), F32),
                   compiler_params=_cparams("parallel", "parallel"))(chip_arr, c_arr, g, r)


def _join_halves(fs):
    n = len(fs)

    def body(*refs):
        out_refs, (send_sems, recv_sems) = refs[n:2 * n], refs[2 * n:]
        x, y, c = _place()
        cps = []
        for i, out_ref in enumerate(out_refs):
            h = out_ref.shape[1] // 2
            mine = out_ref.at[:, pl.ds(c * h, h)]
            cps.append(_remote_copy(send_sems, recv_sems, i, mine, mine, (x, y, 1 - c)))
            cps[-1].start()
        for cp in cps:
            cp.wait()

    return _comm_call(body, "grad_join_halves", fs, [jax.ShapeDtypeStruct(f.shape, f.dtype) for f in fs], n, in_place=True)


def _allgather_small(v, name):
    R, C = v.shape

    def body(v_ref, out_ref, send_sems, recv_sems):
        x, y, c = _place()
        me = 4 * x + 2 * y + c
        out_ref[me] = v_ref[...]
        cps = []
        for k in range(1, N_DEV):
            peer = (1 - x if k & 4 else x, 1 - y if k & 2 else y, 1 - c if k & 1 else c)
            cps.append(pltpu.make_async_remote_copy(src_ref=v_ref, dst_ref=out_ref.at[me], send_sem=send_sems.at[k - 1],
                                                    recv_sem=recv_sems.at[k - 1], device_id=peer, device_id_type=MESH))
        for cp in cps:
            cp.start()
        for cp in cps:
            cp.wait()

    return _pallas(body, name=name, in_specs=[VMEM_SPEC], out_specs=VMEM_SPEC,
                   out_shape=jax.ShapeDtypeStruct((N_DEV, R, C), v.dtype),
                   scratch_shapes=[pltpu.SemaphoreType.DMA((N_DEV - 1,)), pltpu.SemaphoreType.DMA((N_DEV - 1,))],
                   compiler_params=pltpu.CompilerParams(has_side_effects=True))(v)


def _sum_devices(g):
    n, R, C = g.shape

    def body(g_ref, out_ref):
        s = g_ref[0]
        for d in range(1, n):
            s = s + g_ref[d]
        out_ref[...] = s

    return _pallas(body, name="sum_devices", in_specs=[VMEM_SPEC], out_specs=VMEM_SPEC,
                   out_shape=jax.ShapeDtypeStruct((R, C), g.dtype))(g)


def _pad_heads(a, width):
    lead = a.shape[:-1]
    a = a.reshape(lead + (N_HEADS, width))
    a = jnp.pad(a, [(0, 0)] * len(lead) + [(0, 0), (0, HEAD_PAD - width)])
    return a.reshape(lead + (HP,))


def _unpad_heads(a, width):
    lead = a.shape[:-1]
    return a.reshape(lead + (N_HEADS, HEAD_PAD))[..., :width].reshape(lead + (N_HEADS * width,))


def _pre_weights(slabs, l):
    cols = lambda name: slabs[name][:, l].transpose(1, 0, 2).reshape(slabs[name].shape[2], -1)
    w_in = cols("w_in")
    kpe = jnp.pad(w_in[:, 640:672], ((0, 0), (ROPE_LO, HEAD_PAD - ROPE_HI)))
    w_ukv = cols("w_ukv").reshape(KV_LORA, N_HEADS, QK_NOPE + V_HEAD)
    return dict(
        w_in=jnp.concatenate([w_in[:, :640], kpe, w_in[:, 672:]], axis=1),
        w_uq=_pad_heads(cols("w_uq"), QK_HEAD),
        w_ukv=jnp.concatenate([_pad_heads(w_ukv[..., :QK_NOPE].reshape(KV_LORA, -1), QK_NOPE),
                               _pad_heads(w_ukv[..., QK_NOPE:].reshape(KV_LORA, -1), V_HEAD)], axis=1),
    )


def _post_weights(slabs, l):
    cols = lambda name: slabs[name][:, l].transpose(1, 0, 2).reshape(slabs[name].shape[2], -1)
    w_o = slabs["w_o"][l]
    return dict(
        w_o=jnp.concatenate([_pad_heads(w_o[:ATTN_WIDTH].T, V_HEAD).T, w_o[ATTN_WIDTH:]], axis=0),
        w_up=cols("w_up"), w_down=(slabs["w_down"], l), w_ple_gate=(slabs["w_ple_gate"], l), w_ple=cols("w_ple"),
    )


def _gains(small, l):
    row = lambda name: small[name][l].reshape(1, -1)
    headrow = lambda a, b: jnp.pad(jnp.concatenate([small[a][l], small[b][l]]), (0, HEAD_PAD - QK_HEAD)).reshape(1, HEAD_PAD)
    return dict(
        g_mix=row("g_mix"), g_q_lat=row("g_q_lat"), g_kv_lat=row("g_kv_lat"), g_mlp=row("g_mlp"), g_ple=row("g_ple"),
        g_out_conv=row("g_out_conv"), g_out_attn=_pad_heads(small["g_out_attn"][l], V_HEAD).reshape(1, HP),
        gq=headrow("g_qn_nope", "g_qn_rope"), gk=headrow("g_kn_nope", "g_kn_rope"),
        cw8=jnp.pad(small["conv_w"][l], ((0, SUBLANES - CONV_TAPS), (0, 0))),
        head_mats=_head_matrices(),
    )


def _w_o_shards(dw_o):
    return jnp.concatenate([_unpad_heads(dw_o[:HP].T, V_HEAD).T, dw_o[HP:]], axis=0).reshape(N_CHIPS, -1, D_MODEL)


def _unpad_grads(gp):
    dw_in = gp["w_in"]
    dw_ukv = gp["w_ukv"]
    k_part = dw_ukv[:, :HP].reshape(KV_LORA, N_HEADS, HEAD_PAD)[..., :QK_NOPE]
    v_part = dw_ukv[:, HP:].reshape(KV_LORA, N_HEADS, HEAD_PAD)[..., :V_HEAD]
    first = lambda name: gp[name][0]
    col_shards = lambda a: a.reshape(a.shape[0], N_CHIPS, -1).transpose(1, 0, 2)
    return dict(
        w_in=col_shards(jnp.concatenate([dw_in[:, :640], dw_in[:, Z_KPE + ROPE_LO:Z_KPE + ROPE_HI], dw_in[:, Z_GB:]], axis=1)),
        w_uq=col_shards(_unpad_heads(gp["w_uq"], QK_HEAD)),
        w_ukv=col_shards(jnp.concatenate([k_part, v_part], axis=-1).reshape(KV_LORA, -1)),
        w_o=gp["w_o"], w_up=gp["w_up"], w_down=gp["w_down"], w_ple_gate=gp["w_ple_gate"], w_ple=gp["w_ple"],
        g_mix=first("g_mix"), g_q_lat=first("g_q_lat"), g_kv_lat=first("g_kv_lat"), g_mlp=first("g_mlp"),
        g_ple=first("g_ple"), g_out_conv=first("g_out_conv"), g_out_attn=_unpad_heads(first("g_out_attn"), V_HEAD),
        g_qn_nope=gp["gq"][0, :QK_NOPE], g_qn_rope=gp["gq"][0, QK_NOPE:QK_HEAD],
        g_kn_nope=gp["gk"][0, :QK_NOPE], g_kn_rope=gp["gk"][0, QK_NOPE:QK_HEAD],
        conv_w=gp["cw8"][:CONV_TAPS],
    )


def _rope_tables(positions):
    inv_freq = 1.0 / (ROPE_THETA ** (jnp.arange(0, QK_ROPE, 2, dtype=F32) / QK_ROPE))
    ang = positions.astype(F32)[:, None] * inv_freq
    cos, sin = jnp.cos(ang), jnp.sin(ang)
    pad = lambda t, v: jnp.pad(jnp.concatenate([t, t], axis=1), ((0, 0), (ROPE_LO, HEAD_PAD - ROPE_HI)), constant_values=v)
    return pad(cos, 1.0), pad(sin, 0.0)


def _layer_fwd(x0, h, p_l, W, cs, sn, attend, g_next):
    z = _mm(h, W["w_in"], name="mm_in")
    qln, kvln = _lat_fwd(z, W["g_q_lat"], W["g_kv_lat"])
    q_raw = _mm(qln, W["w_uq"], name="mm_uq")
    kv_raw = _mm(kvln, W["w_ukv"], name="mm_ukv")
    qf, kf, vb = _qk_fwd(q_raw, kv_raw, z, cs, sn, W["gq"], W["gk"], *W["head_mats"])
    o, lse = attend(qf, kf, vb)
    conv = _conv_fwd(z, W["cw8"])
    mixed = _mix_fwd(o, conv, W["g_out_attn"], W["g_out_conv"])
    x1, h2 = _mm(mixed, W["w_o"], res=x0, gain=W["g_mlp"], epi="rms", name="mm_o")
    a, f = W["host_mm_up"](lambda rider: _mm(h2, W["w_up"], epi="relu2", out_dtype=BF16, rider=rider, name="mm_up"))
    x2, h3 = _mm(f, W["w_down"], res=x1, gain=W["g_ple"], epi="rms", name="mm_down")
    gl = _mm(h3, W["w_ple_gate"], name="mm_ple_gate")
    pe = _mm(p_l, W["w_ple"], name="mm_ple")
    x3, h_next = _ple_fwd(x2, gl, pe, g_next)
    saved = dict(x0=x0, h=h, z=z, qln=qln, kvln=kvln, q_raw=q_raw, kv_raw=kv_raw, qf=qf, kf=kf, vb=vb, o=o, lse=lse,
                 conv=conv, mixed=mixed, x1=x1, h2=h2, a=a, f=f, x2=x2, h3=h3, gl=gl, pe=pe)
    return x3, h_next, saved


def _layer_bwd(dx3, p_l, W, cs, sn, sv, attend_bwd):
    g = {}
    dpe, dgl = _ple_bwd(dx3, sv["gl"], sv["pe"])
    g["w_ple"] = _mm(p_l, dpe, mode="tn", out_dtype=BF16, shard_out=1, name="mm_dw_ple")
    g["w_ple_gate"] = _mm(sv["h3"], dgl, mode="tn", out_dtype=BF16, shard_out=0, name="mm_dw_ple_gate")
    dx2, dx2b, g["g_ple"] = _mm(dgl, W["w_ple_gate"], mode="nt", res=dx3, aux=sv["x2"], gain=W["g_ple"], epi="rms_bwd",
                                name="mm_dh3")
    da, = W["host_mm_da"](lambda rider: _mm(dx2b, W["w_down"], mode="nt", aux=sv["a"], epi="drelu2", out_dtype=BF16,
                                            rider=rider, name="mm_da"))
    g["w_down"] = _mm(sv["f"], dx2b, mode="tn", out_dtype=BF16, shard_out=0, name="mm_dw_down")
    g["w_up"] = _mm(sv["h2"], da, mode="tn", out_dtype=BF16, shard_out=1, name="mm_dw_up")
    dx1, dx1b, g["g_mlp"] = _mm(da, W["w_up"], mode="nt", res=dx2, aux=sv["x1"], gain=W["g_mlp"], epi="rms_bwd",
                                name="mm_dh2")
    dmixed = _mm(dx1b, W["w_o"], mode="nt", name="mm_dmixed")
    g["w_o"] = _mm(sv["mixed"], dx1b, mode="tn", out_dtype=BF16, name="mm_dw_o")
    do, dconv, g["g_out_attn"], g["g_out_conv"] = _mix_bwd(dmixed, sv["o"], sv["conv"], W["g_out_attn"], W["g_out_conv"])
    dgb, dgc, dxin, g["cw8"] = _conv_bwd(dconv, sv["z"], W["cw8"])
    dqf, dkf, dv = attend_bwd(g, sv["qf"], sv["kf"], sv["vb"], sv["o"], do, sv["lse"])
    dq_raw, dkv_raw, dkpe, g["gq"], g["gk"] = _qk_bwd(dqf, dkf, dv, sv["q_raw"], sv["kv_raw"], sv["z"], cs, sn, W["gq"], W["gk"],
                                                      *W["head_mats"])
    g["w_uq"] = _mm(sv["qln"], dq_raw, mode="tn", out_dtype=BF16, name="mm_dw_uq")
    dqln = _mm(dq_raw, W["w_uq"], mode="nt", name="mm_dqln")
    g["w_ukv"] = _mm(sv["kvln"], dkv_raw, mode="tn", out_dtype=BF16, name="mm_dw_ukv")
    dkvln = _mm(dkv_raw, W["w_ukv"], mode="nt", name="mm_dkvln")
    dlat, g["g_q_lat"], g["g_kv_lat"] = _lat_bwd(dqln, dkvln, sv["z"], W["g_q_lat"], W["g_kv_lat"])
    dz = jnp.concatenate([dlat, dkpe, dgb, dgc, dxin], axis=1)
    g["w_in"] = _mm(sv["h"], dz, mode="tn", out_dtype=BF16, name="mm_dw_in")
    dx0, _, g["g_mix"] = _mm(dz, W["w_in"], mode="nt", res=dx1, aux=sv["x0"], gain=W["g_mix"], epi="rms_bwd", name="mm_dh")
    return dx0, g


def _local_step(x, p, positions, target, slabs, small):
    depth = p.shape[0]
    cs, sn = _rope_tables(positions)
    slabs = dict(slabs)

    def gather(items, host):
        touched = [n for n in PRE + POST if any(n == name for name, _ in items)]
        rider = _allgather_rider([slabs[n] for n in touched], [(touched.index(name), l) for name, l in items])
        if host is None:
            out, new = (), rider.run("allgather_first")
        else:
            *out, new = host(rider)
        slabs.update(zip(touched, new))
        return out

    gather([(name, 0) for name in PRE], None)
    Ws, saved = [], []
    gains = [_gains(small, l) for l in range(depth)]
    h = _rms_fwd(x, gains[0]["g_mix"], "rms_mix")
    for l in range(depth):
        W = dict(gains[l], **_pre_weights(slabs, l))

        def attend(qf, kf, vb, W=W, l=l):
            o, lse = gather([(name, l) for name in POST], functools.partial(_attn_fwd, qf, kf, vb))
            W.update(_post_weights(slabs, l))
            return o, lse

        if l + 1 < depth:
            W["host_mm_up"] = lambda run, l=l: gather([(name, l + 1) for name in PRE], run)
        else:
            W["host_mm_up"] = lambda run: run(None)

        g_next = gains[l + 1]["g_mix"] if l + 1 < depth else jnp.ones_like(gains[l]["g_mix"])
        x, h, sv = _layer_fwd(x, h, p[l], W, cs, sn, attend, g_next)
        Ws.append(W)
        saved.append(sv)
    dx, sq = _loss_grad(x, target)

    landing = {name: lax.empty((depth, N_DEV - 1, _shard_dims(slabs[name])[0] // 2, _shard_dims(slabs[name])[1]), BF16)
               for name in PRE + POST}

    def exchange(sends, host):
        touched = [n for n in PRE + POST if any(n == name for name, _, _ in sends)]
        rider = _exchange_rider([g for _, _, g in sends], [landing[n] for n in touched],
                                [(i, touched.index(name), l) for i, (name, l, _) in enumerate(sends)])
        if host is None:
            out, new = (), rider.run("grad_exchange_last")
        else:
            *out, new = host(rider)
        landing.update(zip(touched, new))
        return out

    grads = [None] * depth
    for l in reversed(range(depth)):
        W = dict(Ws[l], w_down=(slabs["w_down"], l), w_ple_gate=(slabs["w_ple_gate"], l))

        def attend_bwd(g, qf, kf, vb, o, do, lse, l=l):
            g["w_o"] = _w_o_shards(g["w_o"])
            sends = [(name, l, g[name]) for name in POST]
            if l + 1 < depth:
                sends += [(name, l + 1, grads[l + 1][name]) for name in PRE[1:]]
            return exchange(sends, functools.partial(_attn_bwd, qf, kf, vb, o, do, lse))

        if l + 1 < depth:
            W["host_mm_da"] = lambda run, l=l: exchange([(PRE[0], l + 1, grads[l + 1][PRE[0]])], run)
        else:
            W["host_mm_da"] = lambda run: (run(None),)

        dx, gp = _layer_bwd(dx, p[l], W, cs, sn, saved[l], attend_bwd)
        grads[l] = _unpad_grads(gp)
    exchange([(name, 0, grads[0][name]) for name in PRE], None)
    return sq, dx, grads, landing


def _pack_rows(vals, entries):
    depth = vals[entries[0][0]].shape[0]
    return jnp.concatenate([jnp.pad(vals[name].reshape(depth, -1), ((0, 0), (0, _pad128(n) - n))) for name, n in entries], axis=1)


def _unpack_rows(packed, entries):
    out, off = {}, 0
    for name, n in entries:
        out[name] = packed[:, off:off + n]
        off += _pad128(n)
    return out


def kernel(x, p, positions, g_mix, w_in, g_q_lat, w_uq, g_kv_lat, w_ukv, g_qn_nope, g_qn_rope, g_kn_nope, g_kn_rope, conv_w, g_out_attn, g_out_conv, w_o, g_mlp, w_up, w_down, g_ple, w_ple_gate, w_ple, loss_target, m_g_mix, m_w_in, m_g_q_lat, m_w_uq, m_g_kv_lat, m_w_ukv, m_g_qn_nope, m_g_qn_rope, m_g_kn_nope, m_g_kn_rope, m_conv_w, m_g_out_attn, m_g_out_conv, m_w_o, m_g_mlp, m_w_up, m_w_down, m_g_ple, m_w_ple_gate, m_w_ple, v_g_mix, v_w_in, v_g_q_lat, v_w_uq, v_g_kv_lat, v_w_ukv, v_g_qn_nope, v_g_qn_rope, v_g_kn_nope, v_g_kn_rope, v_conv_w, v_g_out_attn, v_g_out_conv, v_w_o, v_g_mlp, v_w_up, v_w_down, v_g_ple, v_w_ple_gate, v_w_ple):
    w = dict(g_mix=g_mix, w_in=w_in, g_q_lat=g_q_lat, w_uq=w_uq, g_kv_lat=g_kv_lat, w_ukv=w_ukv, g_qn_nope=g_qn_nope,
             g_qn_rope=g_qn_rope, g_kn_nope=g_kn_nope, g_kn_rope=g_kn_rope, conv_w=conv_w, g_out_attn=g_out_attn,
             g_out_conv=g_out_conv, w_o=w_o, g_mlp=g_mlp, w_up=w_up, w_down=w_down, g_ple=g_ple, w_ple_gate=w_ple_gate,
             w_ple=w_ple)
    m = dict(g_mix=m_g_mix, w_in=m_w_in, g_q_lat=m_g_q_lat, w_uq=m_w_uq, g_kv_lat=m_g_kv_lat, w_ukv=m_w_ukv,
             g_qn_nope=m_g_qn_nope, g_qn_rope=m_g_qn_rope, g_kn_nope=m_g_kn_nope, g_kn_rope=m_g_kn_rope, conv_w=m_conv_w,
             g_out_attn=m_g_out_attn, g_out_conv=m_g_out_conv, w_o=m_w_o, g_mlp=m_g_mlp, w_up=m_w_up, w_down=m_w_down,
             g_ple=m_g_ple, w_ple_gate=m_w_ple_gate, w_ple=m_w_ple)
    v = dict(g_mix=v_g_mix, w_in=v_w_in, g_q_lat=v_g_q_lat, w_uq=v_w_uq, g_kv_lat=v_g_kv_lat, w_ukv=v_w_ukv,
             g_qn_nope=v_g_qn_nope, g_qn_rope=v_g_qn_rope, g_kn_nope=v_g_kn_nope, g_kn_rope=v_g_kn_rope, conv_w=v_conv_w,
             g_out_attn=v_g_out_attn, g_out_conv=v_g_out_conv, w_o=v_w_o, g_mlp=v_g_mlp, w_up=v_w_up, w_down=v_w_down,
             g_ple=v_g_ple, w_ple_gate=v_w_ple_gate, w_ple=v_w_ple)
    depth = p.shape[0]
    ax, ay, ac = _place()
    chip = 2 * ax + ay
    c_arr = jnp.reshape(ac, (1,)).astype(jnp.int32)
    chip_arr = jnp.reshape(chip, (1,)).astype(jnp.int32)
    conv_shard =("conv_w", CONV_TAPS * CONV_WIDTH // N_CHIPS)
    conv_full = ("conv_w", CONV_TAPS * CONV_WIDTH)

    names = PRE + POST
    slabs = {name: _shard_slab(w[name], chip_arr, "shard_slab_" + name, name in ROW_SHARDED) for name in names}
    conv_rows = -(-depth * CONV_TAPS // SUBLANES) * SUBLANES
    conv_all = _allgather_small(jnp.pad(conv_w.reshape(depth * CONV_TAPS, LANES), ((0, conv_rows - depth * CONV_TAPS), (0, 0))),
                                "allgather_conv_w")
    conv_cat = jnp.concatenate([conv_all[2 * j, :depth * CONV_TAPS] for j in range(N_CHIPS)], axis=1)
    small = {name: w[name] for name, _ in SMALL}
    small["conv_w"] = conv_cat.reshape(depth, CONV_TAPS, CONV_WIDTH)

    sq, grad_x, grads, landing = _local_step(x[0], p[:, 0], positions[0], loss_target[0], slabs, small)

    halves = [_reduce_partials(jnp.stack([grads[l][name] for l in range(depth)]), landing[name], chip_arr, c_arr,
                               "grad_reduce_" + name) for name in names]
    g_out = dict(zip(names, _join_halves(halves)))

    reduced = SMALL + (conv_full, ("loss", LANES))
    stacked = {name: jnp.stack([grads[l][name] for l in range(depth)]) for name, _ in SMALL + (conv_full,)}
    stacked["loss"] = jnp.broadcast_to(sq[:1] * (0.5 / D_MODEL), (depth, LANES))
    g_small = _unpack_rows(_sum_devices(_allgather_small(_pack_rows(stacked, reduced), "allgather_small_grads")), reduced)
    loss = g_small["loss"][0, 0]
    for name, n in SMALL:
        g_out[name] = g_small[name]
    g_conv = g_small["conv_w"].reshape(depth, CONV_TAPS, CONV_WIDTH)
    g_out["conv_w"] = lax.dynamic_slice_in_dim(g_conv, chip * LANES, LANES, axis=2)

    delta, new_m, new_v = {}, {}, {}
    for name in names:
        view = (lambda t: jnp.swapaxes(t, 1, 2)) if w[name].shape[2] % LANES else (lambda t: t)
        d, m2, v2 = _adamw(view(w[name]), view(g_out[name]), view(m[name]), view(v[name]), "adamw_" + name)
        delta[name], new_m[name], new_v[name] = view(d), view(m2), view(v2)
    local = SMALL + (conv_shard,)
    pack = lambda vals: _pack_rows(vals, local)[None]
    d, m2, v2 = _adamw(pack(w), pack(g_out), pack(m), pack(v), "adamw_small")
    for res, packed_res in ((delta, d), (new_m, m2), (new_v, v2)):
        un = _unpack_rows(packed_res[0], local)
        for name, _ in local:
            res[name] = un[name].reshape(w[name].shape)

    return (loss, grad_x[None], *[g_out[n] for n in WEIGHT_ORDER], *[delta[n] for n in WEIGHT_ORDER],
            *[new_m[n] for n in WEIGHT_ORDER], *[new_v[n] for n in WEIGHT_ORDER])
```

```python
import functools

import jax
import jax.numpy as jnp
from jax import lax
from jax.experimental import pallas as pl
from jax.experimental.pallas import tpu as pltpu

F32 = jnp.float32
BF16 = jnp.bfloat16
MESH = pl.DeviceIdType.MESH

D_MODEL = 1024
N_HEADS = 8
QK_NOPE = 64
QK_ROPE = 32
QK_HEAD = QK_NOPE + QK_ROPE
V_HEAD = 64
Q_LORA = 384
KV_LORA = 256
ATTN_WIDTH = N_HEADS * V_HEAD
CONV_WIDTH = 512
CONV_TAPS = 3
D_FF = 4096
PLE_DIM = 256
ROPE_THETA = 10000.0
EPS = 1e-6
ATT_SCALE = QK_HEAD ** -0.5
LOG2E = 1.4426950408889634
ATT_SCALE_LOG2 = ATT_SCALE * LOG2E

ADAM_LR = 0.001
ADAM_B1 = 0.9
ADAM_B2 = 0.999
ADAM_EPS = 1e-08
ADAM_WD = 0.01
ADAM_STEP = 10

LANES = 128
SUBLANES = 8
HEAD_PAD = LANES
HP = N_HEADS * HEAD_PAD
ROPE_LO = QK_NOPE
ROPE_MID = QK_NOPE + QK_ROPE // 2
ROPE_HI = QK_NOPE + QK_ROPE
VMEM_LIMIT = 56 * 1024 * 1024

Z_Q, Z_KV, Z_KPE, Z_GB, Z_GC, Z_XIN = 0, 384, 640, 768, 1280, 1792
Z_COLS = 2304
Z_LAT = Z_KPE

ROW_TILE = 512
ATT_TILE = 256
ATT_CHAINS_FWD, ATT_CHAINS_BWD = 8, 8
MM_TM, MM_TN, MM_TK = 1024, 1024, 4096
MM_TM_MIN, MM_TK_MIN = 256, 512
MM_VMEM_BUDGET = 40 * 1024 * 1024

N_CHIPS = 4
N_DEV = 8

PRE = ("w_in", "w_uq", "w_ukv")
POST = ("w_o", "w_up", "w_down", "w_ple_gate", "w_ple")
ROW_SHARDED = ("w_o", "w_down", "w_ple_gate")
SMALL = (
    ("g_mix", 1024), ("g_q_lat", 384), ("g_kv_lat", 256), ("g_qn_nope", 64), ("g_qn_rope", 32), ("g_kn_nope", 64),
    ("g_kn_rope", 32), ("g_out_attn", 512), ("g_out_conv", 512), ("g_mlp", 1024), ("g_ple", 1024),
)
WEIGHT_ORDER = ("g_mix", "w_in", "g_q_lat", "w_uq", "g_kv_lat", "w_ukv", "g_qn_nope", "g_qn_rope", "g_kn_nope",
                "g_kn_rope", "conv_w", "g_out_attn", "g_out_conv", "w_o", "g_mlp", "w_up", "w_down", "g_ple",
                "w_ple_gate", "w_ple")


def _pallas(body, **kw):
    return pl.pallas_call(body, **kw)


def _cparams(*sem):
    return pltpu.CompilerParams(dimension_semantics=sem, vmem_limit_bytes=VMEM_LIMIT)


def _tile(dim, pref, unit=LANES):
    if dim <= pref:
        return dim
    t = (pref // unit) * unit
    while t > unit and dim % t:
        t -= unit
    assert dim % t == 0, (dim, pref)
    return t


def _pad128(n):
    return -(-n // LANES) * LANES


_DIMS = {"nn": (((1,), (0,)), ((), ())), "nt": (((1,), (1,)), ((), ())), "tn": (((0,), (0,)), ((), ()))}


def _mm(a, b, *, mode="nn", res=None, aux=None, gain=None, epi=None, out_dtype=F32, shard_out=None, rider=None, name):
    whole_rows = epi in ("rms", "rms_bwd")
    b, layer = b if isinstance(b, tuple) else (b, None)
    b_shape = b.shape if layer is None else b.shape[1:]
    if mode == "nn":
        (M, K), (K2, N) = a.shape, b_shape
    elif mode == "nt":
        (M, K), (N, K2) = a.shape, b_shape
    else:
        (K, M), (K2, N) = a.shape, b_shape
    assert K == K2, (a.shape, b_shape, mode)
    n_lim = k_lim = None
    m_lim = M // N_CHIPS if shard_out == 0 else None
    if shard_out == 1:
        n_lim = N // N_CHIPS
    extra = [t for t in (res, aux) if t is not None]
    out_bytes = 6 if epi in ("relu2", "rms", "rms_bwd") else jnp.dtype(out_dtype).itemsize
    if whole_rows:
        assert (n_lim or N) == N and N <= MM_TN and shard_out is None and gain is not None
        m_lim = min(M, MM_TM // 2)

    def footprint(tm, tn, tk):
        blocks = tm * tk * a.dtype.itemsize + tk * tn * 2 + tm * tn * (out_bytes + 4 * len(extra))
        return 2 * blocks + (tm * tn * 4 if tk < K else 0)

    tm, tn, tk = _tile(m_lim or M, MM_TM), _tile(n_lim or N, MM_TN), _tile(k_lim or K, MM_TK)
    while footprint(tm, tn, tk) > MM_VMEM_BUDGET and tk > MM_TK_MIN:
        tk = _tile(K, tk // 2)
    while footprint(tm, tn, tk) > MM_VMEM_BUDGET and tm > MM_TM_MIN:
        tm = _tile(M, tm // 2)
    nk = K // tk
    a_spec = pl.BlockSpec((tk, tm), lambda i, j, k: (k, i)) if mode == "tn" else pl.BlockSpec((tm, tk), lambda i, j, k: (i, k))
    if mode == "nt":
        b_block, b_rc = (tn, tk), (lambda i, j, k: (j, k))
    else:
        b_block, b_rc = (tk, tn), (lambda i, j, k: (k, j))
    if layer is None:
        b_spec = pl.BlockSpec(b_block, b_rc)
    else:
        b_spec = pl.BlockSpec((None,) + b_block, lambda i, j, k: (layer,) + b_rc(i, j, k))
    mn_spec = pl.BlockSpec((tm, tn), lambda i, j, k: (i, j))
    dims = _DIMS[mode]
    n_out = {"relu2": 2, "rms": 2, "rms_bwd": 3}.get(epi, 1)

    n_in = 2 + len(extra) + whole_rows
    n_grid = (M // tm, N // tn, nk)

    def body(*refs):
        a_ref, b_ref = refs[0], refs[1]
        extra_refs = refs[2:2 + len(extra)]
        gain_ref = refs[2 + len(extra)] if whole_rows else None
        if rider is None:
            tail = refs[n_in:]
        else:
            tail, ride = rider.split(refs[n_in:], n_out, int(nk > 1))
            ride((pl.program_id(0) * n_grid[1] + pl.program_id(1)) * nk + pl.program_id(2), n_grid[0] * n_grid[1] * nk)
        out_refs = tail[:n_out]
        prod = lax.dot_general(a_ref[...].astype(BF16), b_ref[...].astype(BF16), dims, preferred_element_type=F32)

        def finish(r):
            if epi == "rms_bwd":
                dx, dgc = _norm_bwd(extra_refs[1][...], r, gain_ref[...], N)
                dx = dx + extra_refs[0][...]
                out_refs[0][...] = dx
                out_refs[1][...] = dx.astype(BF16)
                _accumulate(out_refs[2], dgc)
                return
            if res is not None:
                r = r + extra_refs[0][...]
            if epi == "rms":
                out_refs[0][...] = r
                out_refs[1][...] = (r * _rinv(r, N) * gain_ref[...]).astype(BF16)
            elif epi == "relu2":
                out_refs[0][...] = r.astype(out_dtype)
                t = jnp.maximum(r, 0.0)
                out_refs[1][...] = (t * t).astype(BF16)
            elif epi == "drelu2":
                out_refs[0][...] = (r * (2.0 * jnp.maximum(extra_refs[-1][...].astype(F32), 0.0))).astype(out_dtype)
            else:
                out_refs[0][...] = r.astype(out_dtype)

        if nk == 1:
            finish(prod)
        else:
            acc = tail[n_out]
            k = pl.program_id(2)

            @pl.when(k == 0)
            def _():
                acc[...] = prod

            @pl.when(k > 0)
            def _():
                acc[...] += prod

            @pl.when(k == nk - 1)
            def _():
                finish(acc[...])

    if epi in ("relu2", "rms", "rms_bwd"):
        out_shape = [jax.ShapeDtypeStruct((M, N), out_dtype if epi == "relu2" else F32), jax.ShapeDtypeStruct((M, N), BF16)]
        out_specs = [mn_spec, mn_spec]
        if epi == "rms_bwd":
            out_shape.append(jax.ShapeDtypeStruct((SUBLANES, N), F32))
            out_specs.append(pl.BlockSpec((SUBLANES, tn), lambda i, j, k: (0, j)))
    elif shard_out == 0:
        per = (M // N_CHIPS) // tm
        out_shape = jax.ShapeDtypeStruct((N_CHIPS, M // N_CHIPS, N), out_dtype)
        out_specs = pl.BlockSpec((None, tm, tn), lambda i, j, k: (i // per, i % per, j))
    elif shard_out == 1:
        per = (N // N_CHIPS) // tn
        out_shape = jax.ShapeDtypeStruct((N_CHIPS, M, N // N_CHIPS), out_dtype)
        out_specs = pl.BlockSpec((None, tm, tn), lambda i, j, k: (j // per, i, j % per))
    else:
        out_shape = jax.ShapeDtypeStruct((M, N), out_dtype)
        out_specs = mn_spec
    gains = [gain] if whole_rows else []
    in_specs = [a_spec, b_spec] + [mn_spec] * len(extra) + [pl.BlockSpec((1, tn), lambda i, j, k: (0, j))] * len(gains)
    scratch = [pltpu.VMEM((tm, tn), F32)] if nk > 1 else []
    ordered = epi == "rms_bwd" or rider is not None
    params = _cparams(*(["arbitrary"] * 3 if ordered else ["parallel", "parallel", "arbitrary"]))
    if rider is None:
        return _pallas(body, name=name, grid=n_grid, in_specs=in_specs, out_specs=out_specs, out_shape=out_shape,
                       scratch_shapes=scratch, compiler_params=params)(a, b, *extra, *gains)
    as_list = lambda t: list(t) if isinstance(t, (list, tuple)) else [t]
    outs = _pallas(body, name=name, grid=n_grid, in_specs=in_specs + rider.in_specs(),
                   out_specs=as_list(out_specs) + rider.out_specs(), out_shape=as_list(out_shape) + rider.out_shape(),
                   scratch_shapes=scratch + rider.scratch(), input_output_aliases=rider.aliases(n_in, n_out),
                   compiler_params=params)(a, b, *extra, *gains, *rider.args())
    return (*outs[:n_out], outs[n_out:])


def _rows(ts, d, col=0):
    return pl.BlockSpec((ts, d), lambda i: (i, col))


def _gain(d):
    return pl.BlockSpec((1, d), lambda i: (0, 0))


def _accum(d):
    return pl.BlockSpec((SUBLANES, d), lambda i: (0, 0))


def _accumulate(ref, val):
    i = pl.program_id(0)

    @pl.when(i == 0)
    def _():
        ref[...] = jnp.zeros_like(ref)

    ref[...] += jnp.broadcast_to(jnp.sum(val, axis=0, keepdims=True), ref.shape)


def _rinv(x, n):
    return lax.rsqrt(jnp.sum(x * x, axis=-1, keepdims=True) / n + EPS)


def _norm_bwd(x, dy, g, n):
    r = _rinv(x, n)
    xhat = x * r
    dyg = dy * g
    dx = r * (dyg - xhat * (jnp.sum(dyg * xhat, axis=-1, keepdims=True) / n))
    return dx, dy * xhat


def _rms_fwd(x, g, name):
    S, D = x.shape
    ts = _tile(S, ROW_TILE, SUBLANES)

    def body(x_ref, g_ref, h_ref):
        xv = x_ref[...]
        h_ref[...] = (xv * _rinv(xv, D) * g_ref[...]).astype(BF16)

    return _pallas(body, name=name, grid=(S // ts,), in_specs=[_rows(ts, D), _gain(D)], out_specs=_rows(ts, D),
                   out_shape=jax.ShapeDtypeStruct((S, D), BF16), compiler_params=_cparams("parallel"))(x, g)


def _lat_fwd(z, gq, gkv):
    S = z.shape[0]
    ts = _tile(S, ROW_TILE, SUBLANES)

    def body(z_ref, gq_ref, gkv_ref, q_ref, kv_ref):
        zq = z_ref[:, Z_Q:Z_KV]
        zkv = z_ref[:, Z_KV:Z_KPE]
        q_ref[...] = (zq * _rinv(zq, Q_LORA) * gq_ref[...]).astype(BF16)
        kv_ref[...] = (zkv * _rinv(zkv, KV_LORA) * gkv_ref[...]).astype(BF16)

    return _pallas(body, name="lat_fwd", grid=(S // ts,), in_specs=[_rows(ts, Z_LAT), _gain(Q_LORA), _gain(KV_LORA)],
                   out_specs=[_rows(ts, Q_LORA), _rows(ts, KV_LORA)],
                   out_shape=[jax.ShapeDtypeStruct((S, Q_LORA), BF16), jax.ShapeDtypeStruct((S, KV_LORA), BF16)],
                   compiler_params=_cparams("parallel"))(z, gq, gkv)


def _lat_bwd(dq, dkv, z, gq, gkv):
    S = z.shape[0]
    ts = _tile(S, ROW_TILE, SUBLANES)

    def body(dq_ref, dkv_ref, z_ref, gq_ref, gkv_ref, dlat_ref, dgq_ref, dgkv_ref):
        dxq, cq = _norm_bwd(z_ref[:, Z_Q:Z_KV], dq_ref[...], gq_ref[...], Q_LORA)
        dxkv, ckv = _norm_bwd(z_ref[:, Z_KV:Z_KPE], dkv_ref[...], gkv_ref[...], KV_LORA)
        dlat_ref[:, Z_Q:Z_KV] = dxq.astype(BF16)
        dlat_ref[:, Z_KV:Z_KPE] = dxkv.astype(BF16)
        _accumulate(dgq_ref, cq)
        _accumulate(dgkv_ref, ckv)

    return _pallas(body, name="lat_bwd", grid=(S // ts,),
                   in_specs=[_rows(ts, Q_LORA), _rows(ts, KV_LORA), _rows(ts, Z_LAT), _gain(Q_LORA), _gain(KV_LORA)],
                   out_specs=[_rows(ts, Z_LAT), _accum(Q_LORA), _accum(KV_LORA)],
                   out_shape=[jax.ShapeDtypeStruct((S, Z_LAT), BF16), jax.ShapeDtypeStruct((SUBLANES, Q_LORA), F32),
                              jax.ShapeDtypeStruct((SUBLANES, KV_LORA), F32)],
                   compiler_params=_cparams("arbitrary"))(dq, dkv, z, gq, gkv)


def _head_masks():
    lane = lax.broadcasted_iota(jnp.int32, (1, HEAD_PAD), 1)
    return lane < ROPE_LO, (lane >= ROPE_LO) & (lane < ROPE_HI)


def _head_matrices():
    k = jnp.arange(HEAD_PAD)[:, None]
    j = jnp.arange(HEAD_PAD)[None, :]
    nope = (k < ROPE_LO) & (j < ROPE_LO)
    rope = (k >= ROPE_LO) & (k < ROPE_HI) & (j >= ROPE_LO) & (j < ROPE_HI)
    seg = jnp.where(nope, 1.0 / QK_NOPE, jnp.where(rope, 1.0 / QK_ROPE, 0.0))
    half = QK_ROPE // 2
    rot = jnp.where((j >= ROPE_LO) & (j < ROPE_MID) & (k == j + half), -1.0,
                    jnp.where((j >= ROPE_MID) & (j < ROPE_HI) & (k == j - half), 1.0, 0.0))
    return seg.astype(BF16), rot.astype(BF16)


_HEAD_MAT = pl.BlockSpec((HEAD_PAD, HEAD_PAD), lambda i: (0, 0))


def _split_dot(t, mat):
    hi = t.astype(BF16)
    lo = (t - hi.astype(F32)).astype(BF16)
    return jnp.dot(hi, mat, preferred_element_type=F32) + jnp.dot(lo, mat, preferred_element_type=F32)


def _qk_fwd(q_raw, kv_raw, z, cs, sn, gq, gk, seg, rot):
    S = q_raw.shape[0]
    ts = _tile(S, ROW_TILE, SUBLANES)

    def body(q_ref, k_ref, v_ref, kpe_ref, cs_ref, sn_ref, gq_ref, gk_ref, seg_ref, rot_ref, qf_ref, kf_ref, vb_ref):
        cos, sin, gqv, gkv = cs_ref[...], sn_ref[...], gq_ref[...], gk_ref[...]
        seg_m, rot_m = seg_ref[...], rot_ref[...]

        def norm(x, g):
            return x * lax.rsqrt(_split_dot(x * x, seg_m) + EPS) * g

        def rope(y):
            return y * cos + _split_dot(y, rot_m) * sin

        kr = rope(norm(kpe_ref[...], gkv))
        lane = lax.broadcasted_iota(jnp.int32, (1, HEAD_PAD), 1)
        for h in range(N_HEADS):
            sl = slice(h * HEAD_PAD, (h + 1) * HEAD_PAD)
            qf_ref[:, sl] = rope(norm(q_ref[:, sl], gqv)).astype(BF16)
            kf_ref[:, sl] = (norm(k_ref[:, sl], gkv) + kr).astype(BF16)
            vb_ref[:, sl] = jnp.where(lane == V_HEAD, 1.0, v_ref[:, sl]).astype(BF16)

    hd = jax.ShapeDtypeStruct((S, HP), BF16)
    return _pallas(body, name="qk_fwd", grid=(S // ts,),
                   in_specs=[_rows(ts, HP), _rows(ts, HP, 0), _rows(ts, HP, 1), _rows(ts, HEAD_PAD, Z_KPE // HEAD_PAD),
                             _rows(ts, HEAD_PAD), _rows(ts, HEAD_PAD), _gain(HEAD_PAD), _gain(HEAD_PAD), _HEAD_MAT, _HEAD_MAT],
                   out_specs=[_rows(ts, HP)] * 3, out_shape=[hd, hd, hd],
                   compiler_params=_cparams("parallel"))(q_raw, kv_raw, kv_raw, z, cs, sn, gq, gk, seg, rot)


def _qk_bwd(dqf, dkf, dv, q_raw, kv_raw, z, cs, sn, gq, gk, seg, rot):
    S = q_raw.shape[0]
    ts = _tile(S, ROW_TILE, SUBLANES)

    def body(dqf_ref, dkf_ref, dv_ref, q_ref, k_ref, kpe_ref, cs_ref, sn_ref, gq_ref, gk_ref, seg_ref, rot_ref,
             dq_ref, dkv_ref, dkpe_ref, dgq_ref, dgk_ref):
        m_n, m_r = _head_masks()
        cos, sin, gqv, gkv = cs_ref[...], sn_ref[...], gq_ref[...], gk_ref[...]
        seg_m, rot_m = seg_ref[...], rot_ref[...]

        def rope_t(w):
            return w * cos - _split_dot(w * sin, rot_m)

        def norm_bwd(x, dy, g):
            r = lax.rsqrt(_split_dot(x * x, seg_m) + EPS)
            xhat = x * r
            dyg = dy * g
            return r * (dyg - xhat * _split_dot(dyg * xhat, seg_m)), dy * xhat

        accq = jnp.zeros((ts, HEAD_PAD), F32)
        acck = jnp.zeros((ts, HEAD_PAD), F32)
        dkr = jnp.zeros((ts, HEAD_PAD), F32)
        for h in range(N_HEADS):
            sl = slice(h * HEAD_PAD, (h + 1) * HEAD_PAD)
            dx, c = norm_bwd(q_ref[:, sl], rope_t(dqf_ref[:, sl]), gqv)
            dq_ref[:, sl] = dx.astype(BF16)
            accq = accq + c
            dk = dkf_ref[:, sl]
            dkr = dkr + jnp.where(m_r, dk, 0.0)
            dx, c = norm_bwd(k_ref[:, sl], jnp.where(m_n, dk, 0.0), gkv)
            dkv_ref[:, sl] = dx.astype(BF16)
            acck = acck + c
            dkv_ref[:, HP + h * HEAD_PAD:HP + (h + 1) * HEAD_PAD] = dv_ref[:, sl].astype(BF16)
        dx, c = norm_bwd(kpe_ref[...], rope_t(dkr), gkv)
        dkpe_ref[...] = dx.astype(BF16)
        _accumulate(dgq_ref, accq)
        _accumulate(dgk_ref, acck + c)

    return _pallas(body, name="qk_bwd", grid=(S // ts,),
                   in_specs=[_rows(ts, HP)] * 4 + [_rows(ts, HP, 0), _rows(ts, HEAD_PAD, Z_KPE // HEAD_PAD),
                                                   _rows(ts, HEAD_PAD), _rows(ts, HEAD_PAD), _gain(HEAD_PAD), _gain(HEAD_PAD),
                                                   _HEAD_MAT, _HEAD_MAT],
                   out_specs=[_rows(ts, HP), _rows(ts, 2 * HP), _rows(ts, HEAD_PAD), _accum(HEAD_PAD), _accum(HEAD_PAD)],
                   out_shape=[jax.ShapeDtypeStruct((S, HP), BF16), jax.ShapeDtypeStruct((S, 2 * HP), BF16),
                              jax.ShapeDtypeStruct((S, HEAD_PAD), BF16), jax.ShapeDtypeStruct((SUBLANES, HEAD_PAD), F32),
                              jax.ShapeDtypeStruct((SUBLANES, HEAD_PAD), F32)],
                   compiler_params=_cparams("arbitrary"))(dqf, dkf, dv, q_raw, kv_raw, z, cs, sn, gq, gk, seg, rot)


def _causal(t):
    row = lax.broadcasted_iota(jnp.int32, (t, t), 0)
    col = lax.broadcasted_iota(jnp.int32, (t, t), 1)
    return col <= row


def _attn_tiles(S, chains):
    t = _tile(S, ATT_TILE, SUBLANES)
    return t, min(chains, S // t)


def _attn_fwd(qf, kf, vb, rider):
    S = qf.shape[0]
    t, nc = _attn_tiles(S, ATT_CHAINS_FWD)
    tq = nc * t
    nqt = S // tq

    def body(q_ref, k_ref, v_ref, *rest):
        (o_ref, lse_ref, m_ref, acc_ref), ride = rider.split(rest, 2, 2)
        qt = pl.program_id(1)
        ride(pl.program_id(0) * nqt + qt, N_HEADS * nqt)
        m_ref[...] = jnp.full_like(m_ref, -jnp.inf)
        acc_ref[...] = jnp.zeros_like(acc_ref)

        def against(rows, first, diagonal):
            kb, vb_ = k_ref[rows, :], v_ref[rows, :]
            subs = [slice(a * t, (a + 1) * t) for a in range(first, nc)]
            s = [lax.dot_general(q_ref[sub, :], kb, _DIMS["nt"], preferred_element_type=F32) * ATT_SCALE_LOG2 for sub in subs]
            if diagonal:
                s[0] = jnp.where(_causal(t), s[0], -jnp.inf)
            m_old = [m_ref[sub, :] for sub in subs]
            m_new = [jnp.maximum(mo, jnp.max(sa, axis=-1, keepdims=True)) for mo, sa in zip(m_old, s)]
            p = [jnp.exp2(sa - jnp.concatenate([mn] * (t // HEAD_PAD), axis=1)).astype(BF16) for sa, mn in zip(s, m_new)]
            for sub, mo, mn, pa in zip(subs, m_old, m_new, p):
                acc_ref[sub, :] = jnp.exp2(mo - mn) * acc_ref[sub, :] + jnp.dot(pa, vb_, preferred_element_type=F32)
                m_ref[sub, :] = mn

        def trip(j, carry):
            against(pl.ds(pl.multiple_of(j * t, t), t), 0, False)
            return carry

        lax.fori_loop(0, nc * qt, trip, 0)
        for d in range(nc):
            against(pl.ds(pl.multiple_of((nc * qt + d) * t, t), t), d, True)
        acc = acc_ref[...]
        l = acc[:, V_HEAD:V_HEAD + 1]
        lane = lax.broadcasted_iota(jnp.int32, (1, HEAD_PAD), 1)
        o_ref[...] = jnp.where(lane < V_HEAD, acc / l, 0.0)
        lse_ref[...] = m_ref[...] + jnp.log(l) * LOG2E

    tile = pl.BlockSpec((tq, HEAD_PAD), lambda h, i: (i, h))
    full = pl.BlockSpec((S, HEAD_PAD), lambda h, i: (0, h))
    outs = _pallas(body, name="attn_fwd", grid=(N_HEADS, nqt), in_specs=[tile, full, full] + rider.in_specs(),
                   out_specs=[tile, pl.BlockSpec((None, tq, HEAD_PAD), lambda h, i: (h, i, 0))] + rider.out_specs(),
                   out_shape=[jax.ShapeDtypeStruct((S, HP), F32), jax.ShapeDtypeStruct((N_HEADS, S, HEAD_PAD), F32)]
                   + rider.out_shape(),
                   scratch_shapes=[pltpu.VMEM((tq, HEAD_PAD), F32), pltpu.VMEM((tq, HEAD_PAD), F32)] + rider.scratch(),
                   input_output_aliases=rider.aliases(3, 2),
                   compiler_params=_cparams("arbitrary", "arbitrary"))(qf, kf, vb, *rider.args())
    return outs[0], outs[1], outs[2:]


def _attn_bwd(qf, kf, vb, o, do, lse, rider):
    S = qf.shape[0]
    t, nc = _attn_tiles(S, ATT_CHAINS_BWD)
    tkv = nc * t
    nq = S // t
    nkt = S // tkv

    def body(q_ref, k_ref, v_ref, o_ref, do_ref, lse_ref, *rest):
        (dq_ref, dk_ref, dv_ref), ride = rider.split(rest, 3, 0)
        kt = pl.program_id(1)
        ride(pl.program_id(0) * nkt + kt, N_HEADS * nkt)

        @pl.when(kt == 0)
        def _():
            dq_ref[...] = jnp.zeros_like(dq_ref)

        dk_ref[...] = jnp.zeros_like(dk_ref)
        dv_ref[...] = jnp.zeros_like(dv_ref)

        def q_block(rows):
            dof = do_ref[rows, :]
            delta = jnp.sum(dof * o_ref[rows, :], axis=-1, keepdims=True)
            return q_ref[rows, :], dof.astype(BF16), lse_ref[rows, :][:, :1], delta

        def against(rows, n_sub, diagonal):
            q, dob, lse, delta = q_block(rows)
            subs = [slice(b * t, (b + 1) * t) for b in range(n_sub)]
            kbs = [k_ref[sub, :] for sub in subs]
            s = [lax.dot_general(q, kb, _DIMS["nt"], preferred_element_type=F32) for kb in kbs]
            dp = [lax.dot_general(dob, v_ref[sub, :], _DIMS["nt"], preferred_element_type=F32) for sub in subs]
            p = [jnp.exp2(sb * ATT_SCALE_LOG2 - lse) for sb in s]
            if diagonal:
                p[-1] = jnp.where(_causal(t), p[-1], 0.0)
            ds = [(pb * (dpb - delta) * ATT_SCALE).astype(BF16) for pb, dpb in zip(p, dp)]
            for sub, pb in zip(subs, p):
                dv_ref[sub, :] += lax.dot_general(pb.astype(BF16), dob, _DIMS["tn"], preferred_element_type=F32)
            for sub, dsb in zip(subs, ds):
                dk_ref[sub, :] += lax.dot_general(dsb, q, _DIMS["tn"], preferred_element_type=F32)
            dq_ref[rows, :] += sum(jnp.dot(dsb, kb, preferred_element_type=F32) for dsb, kb in zip(ds, kbs))

        for a in range(nc):
            against(pl.ds(pl.multiple_of((nc * kt + a) * t, t), t), a + 1, True)

        def trip(i, carry):
            against(pl.ds(pl.multiple_of(i * t, t), t), nc, False)
            return carry

        lax.fori_loop(nc * (kt + 1), nq, trip, 0)

    tile = pl.BlockSpec((tkv, HEAD_PAD), lambda h, j: (j, h))
    full = pl.BlockSpec((S, HEAD_PAD), lambda h, j: (0, h))
    hd = jax.ShapeDtypeStruct((S, HP), F32)
    outs = _pallas(body, name="attn_bwd", grid=(N_HEADS, nkt),
                   in_specs=[full, tile, tile, full, full, pl.BlockSpec((None, S, HEAD_PAD), lambda h, j: (h, 0, 0))]
                   + rider.in_specs(),
                   out_specs=[full, tile, tile] + rider.out_specs(), out_shape=[hd, hd, hd] + rider.out_shape(),
                   scratch_shapes=rider.scratch(), input_output_aliases=rider.aliases(6, 3),
                   compiler_params=_cparams("arbitrary", "arbitrary"))(qf, kf, vb, o, do, lse, *rider.args())
    return outs[0], outs[1], outs[2], outs[3:]


def _shift_down(u, j, row):
    return jnp.where(row >= j, pltpu.roll(u, j, 0), 0.0)


def _shift_up(u, j, row, s):
    return jnp.where(row < s - j, pltpu.roll(u, s - j, 0), 0.0)


def _conv_cols(s, first_tile):
    return pl.BlockSpec((s, LANES), lambda cb: (0, first_tile + cb))


def _conv_fwd(z, cw8):
    S = z.shape[0]

    def body(gb_ref, gc_ref, xin_ref, w_ref, out_ref):
        row = lax.broadcasted_iota(jnp.int32, (S, LANES), 0)
        u = gc_ref[...] * xin_ref[...]
        y = w_ref[0:1, :] * u
        for j in range(1, CONV_TAPS):
            y = y + w_ref[j:j + 1, :] * _shift_down(u, j, row)
        out_ref[...] = gb_ref[...] * y

    return _pallas(body, name="conv_fwd", grid=(CONV_WIDTH // LANES,),
                   in_specs=[_conv_cols(S, Z_GB // LANES), _conv_cols(S, Z_GC // LANES), _conv_cols(S, Z_XIN // LANES),
                             pl.BlockSpec((SUBLANES, LANES), lambda cb: (0, cb))],
                   out_specs=_conv_cols(S, 0), out_shape=jax.ShapeDtypeStruct((S, CONV_WIDTH), F32),
                   compiler_params=_cparams("parallel"))(z, z, z, cw8)


def _conv_bwd(dconv, z, cw8):
    S = z.shape[0]

    def body(d_ref, gb_ref, gc_ref, xin_ref, w_ref, dgb_ref, dgc_ref, dxin_ref, dw_ref):
        row = lax.broadcasted_iota(jnp.int32, (S, LANES), 0)
        gc, xin, d = gc_ref[...], xin_ref[...], d_ref[...]
        u = gc * xin
        dy = d * gb_ref[...]
        y = w_ref[0:1, :] * u
        du = w_ref[0:1, :] * dy
        dw = [jnp.sum(dy * u, axis=0, keepdims=True)]
        for j in range(1, CONV_TAPS):
            uj = _shift_down(u, j, row)
            y = y + w_ref[j:j + 1, :] * uj
            du = du + w_ref[j:j + 1, :] * _shift_up(dy, j, row, S)
            dw.append(jnp.sum(dy * uj, axis=0, keepdims=True))
        dgb_ref[...] = (d * y).astype(BF16)
        dgc_ref[...] = (du * xin).astype(BF16)
        dxin_ref[...] = (du * gc).astype(BF16)
        tap = lax.broadcasted_iota(jnp.int32, (SUBLANES, LANES), 0)
        dw_ref[...] = sum(jnp.where(tap == j, dw[j], 0.0) for j in range(CONV_TAPS))

    col = _conv_cols(S, 0)
    sd = jax.ShapeDtypeStruct((S, CONV_WIDTH), BF16)
    return _pallas(body, name="conv_bwd", grid=(CONV_WIDTH // LANES,),
                   in_specs=[col, _conv_cols(S, Z_GB // LANES), _conv_cols(S, Z_GC // LANES), _conv_cols(S, Z_XIN // LANES),
                             pl.BlockSpec((SUBLANES, LANES), lambda cb: (0, cb))],
                   out_specs=[col, col, col, pl.BlockSpec((SUBLANES, LANES), lambda cb: (0, cb))],
                   out_shape=[sd, sd, sd, jax.ShapeDtypeStruct((SUBLANES, CONV_WIDTH), F32)],
                   compiler_params=_cparams("parallel"))(dconv, z, z, z, cw8)


def _mix_fwd(o, conv, ga, gc):
    S = o.shape[0]
    ts = _tile(S, ROW_TILE, SUBLANES)

    def body(o_ref, c_ref, ga_ref, gc_ref, out_ref):
        ov, cv = o_ref[...], c_ref[...]
        out_ref[:, :HP] = (ov * _rinv(ov, ATTN_WIDTH) * ga_ref[...]).astype(BF16)
        out_ref[:, HP:] = (cv * _rinv(cv, CONV_WIDTH) * gc_ref[...]).astype(BF16)

    return _pallas(body, name="mix_fwd", grid=(S // ts,),
                   in_specs=[_rows(ts, HP), _rows(ts, CONV_WIDTH), _gain(HP), _gain(CONV_WIDTH)],
                   out_specs=_rows(ts, HP + CONV_WIDTH), out_shape=jax.ShapeDtypeStruct((S, HP + CONV_WIDTH), BF16),
                   compiler_params=_cparams("parallel"))(o, conv, ga, gc)


def _mix_bwd(dmixed, o, conv, ga, gc):
    S = o.shape[0]
    ts = _tile(S, ROW_TILE, SUBLANES)

    def body(d_ref, o_ref, c_ref, ga_ref, gc_ref, do_ref, dc_ref, dga_ref, dgc_ref):
        dx, ca = _norm_bwd(o_ref[...], d_ref[:, :HP], ga_ref[...], ATTN_WIDTH)
        do_ref[...] = dx
        dx, cc = _norm_bwd(c_ref[...], d_ref[:, HP:], gc_ref[...], CONV_WIDTH)
        dc_ref[...] = dx
        _accumulate(dga_ref, ca)
        _accumulate(dgc_ref, cc)

    return _pallas(body, name="mix_bwd", grid=(S // ts,),
                   in_specs=[_rows(ts, HP + CONV_WIDTH), _rows(ts, HP), _rows(ts, CONV_WIDTH), _gain(HP), _gain(CONV_WIDTH)],
                   out_specs=[_rows(ts, HP), _rows(ts, CONV_WIDTH), _accum(HP), _accum(CONV_WIDTH)],
                   out_shape=[jax.ShapeDtypeStruct((S, HP), F32), jax.ShapeDtypeStruct((S, CONV_WIDTH), F32),
                              jax.ShapeDtypeStruct((SUBLANES, HP), F32), jax.ShapeDtypeStruct((SUBLANES, CONV_WIDTH), F32)],
                   compiler_params=_cparams("arbitrary"))(dmixed, o, conv, ga, gc)


def _ple_fwd(x, gl, pe, g_next):
    S, D = x.shape
    ts = _tile(S, ROW_TILE, SUBLANES)

    def body(x_ref, gl_ref, pe_ref, g_ref, out_ref, h_ref):
        y = x_ref[...] + jax.nn.sigmoid(gl_ref[...]) * pe_ref[...]
        out_ref[...] = y
        h_ref[...] = (y * _rinv(y, D) * g_ref[...]).astype(BF16)

    return _pallas(body, name="ple_fwd", grid=(S // ts,), in_specs=[_rows(ts, D)] * 3 + [_gain(D)], out_specs=[_rows(ts, D)] * 2,
                   out_shape=[jax.ShapeDtypeStruct((S, D), F32), jax.ShapeDtypeStruct((S, D), BF16)],
                   compiler_params=_cparams("parallel"))(x, gl, pe, g_next)


def _ple_bwd(dx, gl, pe):
    S, D = dx.shape
    ts = _tile(S, ROW_TILE, SUBLANES)

    def body(dx_ref, gl_ref, pe_ref, dpe_ref, dgl_ref):
        d = dx_ref[...]
        gate = jax.nn.sigmoid(gl_ref[...])
        dpe_ref[...] = (d * gate).astype(BF16)
        dgl_ref[...] = (d * pe_ref[...] * (gate * (1.0 - gate))).astype(BF16)

    sd = jax.ShapeDtypeStruct((S, D), BF16)
    return _pallas(body, name="ple_bwd", grid=(S // ts,), in_specs=[_rows(ts, D)] * 3, out_specs=[_rows(ts, D)] * 2,
                   out_shape=[sd, sd], compiler_params=_cparams("parallel"))(dx, gl, pe)


def _loss_grad(y, target):
    S, D = y.shape
    ts = _tile(S, ROW_TILE, SUBLANES)

    def body(y_ref, t_ref, dy_ref, sq_ref):
        e = y_ref[...] - t_ref[...]
        dy_ref[...] = e / D

        @pl.when(pl.program_id(0) == 0)
        def _():
            sq_ref[...] = jnp.zeros_like(sq_ref)

        sq_ref[...] += jnp.broadcast_to(jnp.sum(jnp.sum(e * e, axis=1, keepdims=True), axis=0, keepdims=True), sq_ref.shape)

    return _pallas(body, name="loss_grad", grid=(S // ts,), in_specs=[_rows(ts, D)] * 2,
                   out_specs=[_rows(ts, D), _accum(LANES)],
                   out_shape=[jax.ShapeDtypeStruct((S, D), F32), jax.ShapeDtypeStruct((SUBLANES, LANES), F32)],
                   compiler_params=_cparams("arbitrary"))(y, target)


def _adamw(w, g, m, v, name):
    L, R, C = w.shape
    tr = _tile(R, ROW_TILE, SUBLANES)

    def body(w_ref, g_ref, m_ref, v_ref, d_ref, m2_ref, v2_ref):
        gv = g_ref[...]
        m2 = ADAM_B1 * m_ref[...] + (1.0 - ADAM_B1) * gv
        v2 = ADAM_B2 * v_ref[...] + (1.0 - ADAM_B2) * (gv * gv)
        m_hat = m2 / (1.0 - ADAM_B1 ** ADAM_STEP)
        v_hat = v2 / (1.0 - ADAM_B2 ** ADAM_STEP)
        d_ref[...] = -ADAM_LR * (m_hat / (jnp.sqrt(v_hat) + ADAM_EPS) + ADAM_WD * w_ref[...])
        m2_ref[...] = m2
        v2_ref[...] = v2

    sd = jax.ShapeDtypeStruct((L, R, C), F32)
    spec = pl.BlockSpec((None, tr, C), lambda l, i: (l, i, 0))
    return _pallas(body, name=name, grid=(L, R // tr), in_specs=[spec] * 4, out_specs=[spec] * 3, out_shape=[sd, sd, sd],
                   compiler_params=_cparams("parallel", "parallel"))(w, g, m, v)


def _place():
    return lax.axis_index("x"), lax.axis_index("y"), lax.axis_index("c")


def _other_chips(x, y):
    return [(1 - x, y), (x, 1 - y), (1 - x, 1 - y)]


HBM = pl.BlockSpec(memory_space=pl.ANY)
VMEM_SPEC = pl.BlockSpec(memory_space=pltpu.VMEM)


def _remote_copy(send_sems, recv_sems, k, src, dst, to):
    return pltpu.make_async_remote_copy(src_ref=src, dst_ref=dst, send_sem=send_sems.at[k], recv_sem=recv_sems.at[k],
                                        device_id=to, device_id_type=MESH)


def _comm_call(body, name, arrays, out_shapes, n_remote, in_place=False):
    scratch = [pltpu.SemaphoreType.DMA((n_remote,)), pltpu.SemaphoreType.DMA((n_remote,))]
    aliases = {i: i for i in range(len(arrays))} if in_place else {}
    return _pallas(body, name=name, in_specs=[HBM] * len(arrays), out_specs=[HBM] * len(out_shapes), out_shape=out_shapes,
                   scratch_shapes=scratch, input_output_aliases=aliases,
                   compiler_params=pltpu.CompilerParams(has_side_effects=True))(*arrays)


def _shard_slab(w, chip_arr, name, stack_rows):
    L, R, C = w.shape
    tr = _tile(R, ROW_TILE, 16)
    nb = R // tr

    def body(j_ref, w_ref, out_ref):
        out_ref[...] = w_ref[...].astype(BF16)

    if stack_rows:
        out_shape = jax.ShapeDtypeStruct((L, N_CHIPS * R, C), BF16)
        out_spec = pl.BlockSpec((None, tr, C), lambda l, i, j_ref: (l, j_ref[0] * nb + i, 0))
    else:
        out_shape = jax.ShapeDtypeStruct((N_CHIPS, L, R, C), BF16)
        out_spec = pl.BlockSpec((None, None, tr, C), lambda l, i, j_ref: (j_ref[0], l, i, 0))
    grid_spec = pltpu.PrefetchScalarGridSpec(
        num_scalar_prefetch=1, grid=(L, nb), in_specs=[pl.BlockSpec((None, tr, C), lambda l, i, j_ref: (l, i, 0))],
        out_specs=out_spec)
    return _pallas(body, name=name, grid_spec=grid_spec, out_shape=out_shape,
                   compiler_params=_cparams("parallel", "parallel"))(chip_arr, w)


def _shard_dims(slab):
    return (slab.shape[2], slab.shape[3]) if slab.ndim == 4 else (slab.shape[1] // N_CHIPS, slab.shape[2])


def _shard_rows(ref, chip, l, which):
    ks, _ = _shard_dims(ref)
    h = ks // 2
    return ref.at[chip, l, pl.ds(which * h, h)] if len(ref.shape) == 4 else ref.at[l, pl.ds(chip * ks + which * h, h)]


class Rider:
    def __init__(self, reads, inplace, n_sems, stages):
        self.reads, self.inplace, self.n_sems, self.stages = list(reads), list(inplace), n_sems, stages

    def args(self):
        return self.reads + self.inplace

    def in_specs(self):
        return [HBM] * len(self.args())

    def out_specs(self):
        return [HBM] * len(self.inplace)

    def out_shape(self):
        return [jax.ShapeDtypeStruct(a.shape, a.dtype) for a in self.inplace]

    def scratch(self):
        return [pltpu.SemaphoreType.DMA((self.n_sems,)), pltpu.SemaphoreType.DMA((self.n_sems,))] if self.n_sems else []

    def aliases(self, n_host_in, n_host_out):
        return {n_host_in + len(self.reads) + j: n_host_out + j for j in range(len(self.inplace))}

    def split(self, rest, n_host_out, n_host_scratch):
        n_r, n_io = len(self.reads), len(self.inplace)
        reads = rest[:n_r]
        host_out = rest[n_r + n_io:n_r + n_io + n_host_out]
        outs = rest[n_r + n_io + n_host_out:n_r + 2 * n_io + n_host_out]
        host_scratch = rest[n_r + 2 * n_io + n_host_out:n_r + 2 * n_io + n_host_out + n_host_scratch]
        sems = rest[n_r + 2 * n_io + n_host_out + n_host_scratch:]

        def ride(step, n_steps):
            at = [0, n_steps - 1] if len(self.stages) == 2 else [0, (3 * n_steps) // 4, n_steps - 1]
            for s, stage in zip(at, self.stages):
                pl.when(step == s)(functools.partial(stage, reads, outs, *sems))

        return tuple(host_out) + tuple(host_scratch), ride

    def run(self, name):
        n_r, n_io = len(self.reads), len(self.inplace)

        def body(*refs):
            for stage in self.stages:
                stage(refs[:n_r], refs[n_r + n_io:n_r + 2 * n_io], *refs[n_r + 2 * n_io:])

        return _pallas(body, name=name, in_specs=self.in_specs(), out_specs=self.out_specs(), out_shape=self.out_shape(),
                       scratch_shapes=self.scratch(), input_output_aliases=self.aliases(0, 0),
                       compiler_params=pltpu.CompilerParams(has_side_effects=True))(*self.args())


def _allgather_rider(slabs, items):
    def stage(which, reads, outs, send_sems, recv_sems):
        x, y, c = _place()
        me, sibling = (x, y, c), (x, y, 1 - c)
        copy = functools.partial(_remote_copy, send_sems, recv_sems)
        for n, (j, l) in enumerate(items):
            ref = outs[j]
            own = _shard_rows(ref, 2 * x + y, l, c)
            for k, (cx, cy) in enumerate(_other_chips(x, y)):
                arrived = _shard_rows(ref, 2 * cx + cy, l, c)
                if which == 0:
                    copy(6 * n + k, own, own, (cx, cy, c)).start()
                elif which == 1:
                    copy(6 * n + k, arrived, arrived, me).wait_recv()
                    copy(6 * n + 3 + k, arrived, arrived, sibling).start()
                else:
                    passed = _shard_rows(ref, 2 * cx + cy, l, 1 - c)
                    copy(6 * n + 3 + k, passed, passed, me).wait_recv()
                    copy(6 * n + k, own, own, me).wait_send()
                    copy(6 * n + 3 + k, arrived, arrived, me).wait_send()

    return Rider([], slabs, 6 * len(items), [functools.partial(stage, w) for w in range(3)])


def _exchange_rider(grads, landing, items):
    def stage(start, reads, outs, send_sems, recv_sems):
        x, y, c = _place()
        for n, (i, j, l) in enumerate(items):
            g_ref, r_ref = reads[i], outs[j]
            h = g_ref.shape[1] // 2
            for k in range(1, N_DEV):
                px, py, pc = (1 - x if k & 4 else x, 1 - y if k & 2 else y, 1 - c if k & 1 else c)
                cp = _remote_copy(send_sems, recv_sems, 7 * n + k - 1, g_ref.at[2 * px + py, pl.ds(pc * h, h)],
                                  r_ref.at[l, k - 1], (px, py, pc))
                if start:
                    cp.start()
                else:
                    cp.wait()

    return Rider(grads, landing, 7 * len(items), [functools.partial(stage, True), functools.partial(stage, False)])


def _reduce_partials(g, r, chip_arr, c_arr, name):
    L, n, R, C = g.shape
    H = R // 2
    th = _tile(H, ROW_TILE, 16)
    nb = H // th

    def body(j_ref, c_ref, g_ref, r_ref, out_ref):
        s = g_ref[...].astype(F32)
        for k in range(N_DEV - 1):
            s = s + r_ref[k].astype(F32)
        out_ref[...] = s

    grid_spec = pltpu.PrefetchScalarGridSpec(
        num_scalar_prefetch=2, grid=(L, nb),
        in_specs=[pl.BlockSpec((None, None, th, C), lambda l, i, j_ref, c_ref: (l, j_ref[0], c_ref[0] * nb + i, 0)),
                  pl.BlockSpec((None, N_DEV - 1, th, C), lambda l, i, j_ref, c_ref: (l, 0, i, 0))],
        out_specs=pl.BlockSpec((None, th, C), lambda l, i, j_ref, c_ref: (l, c_ref[0] * nb + i, 0)))
    return _pallas(body, name=name, grid_spec=grid_spec, out_shape=jax.ShapeDtypeStruct((L, R, C), F32),
                   compiler_params=_cparams("parallel", "parallel"))(chip_arr, c_arr, g, r)


def _join_halves(fs):
    n = len(fs)

    def body(*refs):
        out_refs, (send_sems, recv_sems) = refs[n:2 * n], refs[2 * n:]
        x, y, c = _place()
        cps = []
        for i, out_ref in enumerate(out_refs):
            h = out_ref.shape[1] // 2
            mine = out_ref.at[:, pl.ds(c * h, h)]
            cps.append(_remote_copy(send_sems, recv_sems, i, mine, mine, (x, y, 1 - c)))
            cps[-1].start()
        for cp in cps:
            cp.wait()

    return _comm_call(body, "grad_join_halves", fs, [jax.ShapeDtypeStruct(f.shape, f.dtype) for f in fs], n, in_place=True)


def _allgather_small(v, name):
    R, C = v.shape

    def body(v_ref, out_ref, send_sems, recv_sems):
        x, y, c = _place()
        me = 4 * x + 2 * y + c
        out_ref[me] = v_ref[...]
        cps = []
        for k in range(1, N_DEV):
            peer = (1 - x if k & 4 else x, 1 - y if k & 2 else y, 1 - c if k & 1 else c)
            cps.append(pltpu.make_async_remote_copy(src_ref=v_ref, dst_ref=out_ref.at[me], send_sem=send_sems.at[k - 1],
                                                    recv_sem=recv_sems.at[k - 1], device_id=peer, device_id_type=MESH))
        for cp in cps:
            cp.start()
        for cp in cps:
            cp.wait()

    return _pallas(body, name=name, in_specs=[VMEM_SPEC], out_specs=VMEM_SPEC,
                   out_shape=jax.ShapeDtypeStruct((N_DEV, R, C), v.dtype),
                   scratch_shapes=[pltpu.SemaphoreType.DMA((N_DEV - 1,)), pltpu.SemaphoreType.DMA((N_DEV - 1,))],
                   compiler_params=pltpu.CompilerParams(has_side_effects=True))(v)


def _sum_devices(g):
    n, R, C = g.shape

    def body(g_ref, out_ref):
        s = g_ref[0]
        for d in range(1, n):
            s = s + g_ref[d]
        out_ref[...] = s

    return _pallas(body, name="sum_devices", in_specs=[VMEM_SPEC], out_specs=VMEM_SPEC,
                   out_shape=jax.ShapeDtypeStruct((R, C), g.dtype))(g)


def _pad_heads(a, width):
    lead = a.shape[:-1]
    a = a.reshape(lead + (N_HEADS, width))
    a = jnp.pad(a, [(0, 0)] * len(lead) + [(0, 0), (0, HEAD_PAD - width)])
    return a.reshape(lead + (HP,))


def _unpad_heads(a, width):
    lead = a.shape[:-1]
    return a.reshape(lead + (N_HEADS, HEAD_PAD))[..., :width].reshape(lead + (N_HEADS * width,))


def _pre_weights(slabs, l):
    cols = lambda name: slabs[name][:, l].transpose(1, 0, 2).reshape(slabs[name].shape[2], -1)
    w_in = cols("w_in")
    kpe = jnp.pad(w_in[:, 640:672], ((0, 0), (ROPE_LO, HEAD_PAD - ROPE_HI)))
    w_ukv = cols("w_ukv").reshape(KV_LORA, N_HEADS, QK_NOPE + V_HEAD)
    return dict(
        w_in=jnp.concatenate([w_in[:, :640], kpe, w_in[:, 672:]], axis=1),
        w_uq=_pad_heads(cols("w_uq"), QK_HEAD),
        w_ukv=jnp.concatenate([_pad_heads(w_ukv[..., :QK_NOPE].reshape(KV_LORA, -1), QK_NOPE),
                               _pad_heads(w_ukv[..., QK_NOPE:].reshape(KV_LORA, -1), V_HEAD)], axis=1),
    )


def _post_weights(slabs, l):
    cols = lambda name: slabs[name][:, l].transpose(1, 0, 2).reshape(slabs[name].shape[2], -1)
    w_o = slabs["w_o"][l]
    return dict(
        w_o=jnp.concatenate([_pad_heads(w_o[:ATTN_WIDTH].T, V_HEAD).T, w_o[ATTN_WIDTH:]], axis=0),
        w_up=cols("w_up"), w_down=(slabs["w_down"], l), w_ple_gate=(slabs["w_ple_gate"], l), w_ple=cols("w_ple"),
    )


def _gains(small, l):
    row = lambda name: small[name][l].reshape(1, -1)
    headrow = lambda a, b: jnp.pad(jnp.concatenate([small[a][l], small[b][l]]), (0, HEAD_PAD - QK_HEAD)).reshape(1, HEAD_PAD)
    return dict(
        g_mix=row("g_mix"), g_q_lat=row("g_q_lat"), g_kv_lat=row("g_kv_lat"), g_mlp=row("g_mlp"), g_ple=row("g_ple"),
        g_out_conv=row("g_out_conv"), g_out_attn=_pad_heads(small["g_out_attn"][l], V_HEAD).reshape(1, HP),
        gq=headrow("g_qn_nope", "g_qn_rope"), gk=headrow("g_kn_nope", "g_kn_rope"),
        cw8=jnp.pad(small["conv_w"][l], ((0, SUBLANES - CONV_TAPS), (0, 0))),
        head_mats=_head_matrices(),
    )


def _w_o_shards(dw_o):
    return jnp.concatenate([_unpad_heads(dw_o[:HP].T, V_HEAD).T, dw_o[HP:]], axis=0).reshape(N_CHIPS, -1, D_MODEL)


def _unpad_grads(gp):
    dw_in = gp["w_in"]
    dw_ukv = gp["w_ukv"]
    k_part = dw_ukv[:, :HP].reshape(KV_LORA, N_HEADS, HEAD_PAD)[..., :QK_NOPE]
    v_part = dw_ukv[:, HP:].reshape(KV_LORA, N_HEADS, HEAD_PAD)[..., :V_HEAD]
    first = lambda name: gp[name][0]
    col_shards = lambda a: a.reshape(a.shape[0], N_CHIPS, -1).transpose(1, 0, 2)
    return dict(
        w_in=col_shards(jnp.concatenate([dw_in[:, :640], dw_in[:, Z_KPE + ROPE_LO:Z_KPE + ROPE_HI], dw_in[:, Z_GB:]], axis=1)),
        w_uq=col_shards(_unpad_heads(gp["w_uq"], QK_HEAD)),
        w_ukv=col_shards(jnp.concatenate([k_part, v_part], axis=-1).reshape(KV_LORA, -1)),
        w_o=gp["w_o"], w_up=gp["w_up"], w_down=gp["w_down"], w_ple_gate=gp["w_ple_gate"], w_ple=gp["w_ple"],
        g_mix=first("g_mix"), g_q_lat=first("g_q_lat"), g_kv_lat=first("g_kv_lat"), g_mlp=first("g_mlp"),
        g_ple=first("g_ple"), g_out_conv=first("g_out_conv"), g_out_attn=_unpad_heads(first("g_out_attn"), V_HEAD),
        g_qn_nope=gp["gq"][0, :QK_NOPE], g_qn_rope=gp["gq"][0, QK_NOPE:QK_HEAD],
        g_kn_nope=gp["gk"][0, :QK_NOPE], g_kn_rope=gp["gk"][0, QK_NOPE:QK_HEAD],
        conv_w=gp["cw8"][:CONV_TAPS],
    )


def _rope_tables(positions):
    inv_freq = 1.0 / (ROPE_THETA ** (jnp.arange(0, QK_ROPE, 2, dtype=F32) / QK_ROPE))
    ang = positions.astype(F32)[:, None] * inv_freq
    cos, sin = jnp.cos(ang), jnp.sin(ang)
    pad = lambda t, v: jnp.pad(jnp.concatenate([t, t], axis=1), ((0, 0), (ROPE_LO, HEAD_PAD - ROPE_HI)), constant_values=v)
    return pad(cos, 1.0), pad(sin, 0.0)


def _layer_fwd(x0, h, p_l, W, cs, sn, attend, g_next):
    z = _mm(h, W["w_in"], name="mm_in")
    qln, kvln = _lat_fwd(z, W["g_q_lat"], W["g_kv_lat"])
    q_raw = _mm(qln, W["w_uq"], name="mm_uq")
    kv_raw = _mm(kvln, W["w_ukv"], name="mm_ukv")
    qf, kf, vb = _qk_fwd(q_raw, kv_raw, z, cs, sn, W["gq"], W["gk"], *W["head_mats"])
    o, lse = attend(qf, kf, vb)
    conv = _conv_fwd(z, W["cw8"])
    mixed = _mix_fwd(o, conv, W["g_out_attn"], W["g_out_conv"])
    x1, h2 = _mm(mixed, W["w_o"], res=x0, gain=W["g_mlp"], epi="rms", name="mm_o")
    a, f = W["host_mm_up"](lambda rider: _mm(h2, W["w_up"], epi="relu2", out_dtype=BF16, rider=rider, name="mm_up"))
    x2, h3 = _mm(f, W["w_down"], res=x1, gain=W["g_ple"], epi="rms", name="mm_down")
    gl = _mm(h3, W["w_ple_gate"], name="mm_ple_gate")
    pe = _mm(p_l, W["w_ple"], name="mm_ple")
    x3, h_next = _ple_fwd(x2, gl, pe, g_next)
    saved = dict(x0=x0, h=h, z=z, qln=qln, kvln=kvln, q_raw=q_raw, kv_raw=kv_raw, qf=qf, kf=kf, vb=vb, o=o, lse=lse,
                 conv=conv, mixed=mixed, x1=x1, h2=h2, a=a, f=f, x2=x2, h3=h3, gl=gl, pe=pe)
    return x3, h_next, saved


def _layer_bwd(dx3, p_l, W, cs, sn, sv, attend_bwd):
    g = {}
    dpe, dgl = _ple_bwd(dx3, sv["gl"], sv["pe"])
    g["w_ple"] = _mm(p_l, dpe, mode="tn", out_dtype=BF16, shard_out=1, name="mm_dw_ple")
    g["w_ple_gate"] = _mm(sv["h3"], dgl, mode="tn", out_dtype=BF16, shard_out=0, name="mm_dw_ple_gate")
    dx2, dx2b, g["g_ple"] = _mm(dgl, W["w_ple_gate"], mode="nt", res=dx3, aux=sv["x2"], gain=W["g_ple"], epi="rms_bwd",
                                name="mm_dh3")
    da, = W["host_mm_da"](lambda rider: _mm(dx2b, W["w_down"], mode="nt", aux=sv["a"], epi="drelu2", out_dtype=BF16,
                                            rider=rider, name="mm_da"))
    g["w_down"] = _mm(sv["f"], dx2b, mode="tn", out_dtype=BF16, shard_out=0, name="mm_dw_down")
    g["w_up"] = _mm(sv["h2"], da, mode="tn", out_dtype=BF16, shard_out=1, name="mm_dw_up")
    dx1, dx1b, g["g_mlp"] = _mm(da, W["w_up"], mode="nt", res=dx2, aux=sv["x1"], gain=W["g_mlp"], epi="rms_bwd",
                                name="mm_dh2")
    dmixed = _mm(dx1b, W["w_o"], mode="nt", name="mm_dmixed")
    g["w_o"] = _mm(sv["mixed"], dx1b, mode="tn", out_dtype=BF16, name="mm_dw_o")
    do, dconv, g["g_out_attn"], g["g_out_conv"] = _mix_bwd(dmixed, sv["o"], sv["conv"], W["g_out_attn"], W["g_out_conv"])
    dgb, dgc, dxin, g["cw8"] = _conv_bwd(dconv, sv["z"], W["cw8"])
    dqf, dkf, dv = attend_bwd(g, sv["qf"], sv["kf"], sv["vb"], sv["o"], do, sv["lse"])
    dq_raw, dkv_raw, dkpe, g["gq"], g["gk"] = _qk_bwd(dqf, dkf, dv, sv["q_raw"], sv["kv_raw"], sv["z"], cs, sn, W["gq"], W["gk"],
                                                      *W["head_mats"])
    g["w_uq"] = _mm(sv["qln"], dq_raw, mode="tn", out_dtype=BF16, name="mm_dw_uq")
    dqln = _mm(dq_raw, W["w_uq"], mode="nt", name="mm_dqln")
    g["w_ukv"] = _mm(sv["kvln"], dkv_raw, mode="tn", out_dtype=BF16, name="mm_dw_ukv")
    dkvln = _mm(dkv_raw, W["w_ukv"], mode="nt", name="mm_dkvln")
    dlat, g["g_q_lat"], g["g_kv_lat"] = _lat_bwd(dqln, dkvln, sv["z"], W["g_q_lat"], W["g_kv_lat"])
    dz = jnp.concatenate([dlat, dkpe, dgb, dgc, dxin], axis=1)
    g["w_in"] = _mm(sv["h"], dz, mode="tn", out_dtype=BF16, name="mm_dw_in")
    dx0, _, g["g_mix"] = _mm(dz, W["w_in"], mode="nt", res=dx1, aux=sv["x0"], gain=W["g_mix"], epi="rms_bwd", name="mm_dh")
    return dx0, g


def _local_step(x, p, positions, target, slabs, small):
    depth = p.shape[0]
    cs, sn = _rope_tables(positions)
    slabs = dict(slabs)

    def gather(items, host):
        touched = [n for n in PRE + POST if any(n == name for name, _ in items)]
        rider = _allgather_rider([slabs[n] for n in touched], [(touched.index(name), l) for name, l in items])
        if host is None:
            out, new = (), rider.run("allgather_first")
        else:
            *out, new = host(rider)
        slabs.update(zip(touched, new))
        return out

    gather([(name, 0) for name in PRE], None)
    Ws, saved = [], []
    gains = [_gains(small, l) for l in range(depth)]
    h = _rms_fwd(x, gains[0]["g_mix"], "rms_mix")
    for l in range(depth):
        W = dict(gains[l], **_pre_weights(slabs, l))

        def attend(qf, kf, vb, W=W, l=l):
            o, lse = gather([(name, l) for name in POST], functools.partial(_attn_fwd, qf, kf, vb))
            W.update(_post_weights(slabs, l))
            return o, lse

        if l + 1 < depth:
            W["host_mm_up"] = lambda run, l=l: gather([(name, l + 1) for name in PRE], run)
        else:
            W["host_mm_up"] = lambda run: run(None)

        g_next = gains[l + 1]["g_mix"] if l + 1 < depth else jnp.ones_like(gains[l]["g_mix"])
        x, h, sv = _layer_fwd(x, h, p[l], W, cs, sn, attend, g_next)
        Ws.append(W)
        saved.append(sv)
    dx, sq = _loss_grad(x, target)

    landing = {name: lax.empty((depth, N_DEV - 1, _shard_dims(slabs[name])[0] // 2, _shard_dims(slabs[name])[1]), BF16)
               for name in PRE + POST}

    def exchange(sends, host):
        touched = [n for n in PRE + POST if any(n == name for name, _, _ in sends)]
        rider = _exchange_rider([g for _, _, g in sends], [landing[n] for n in touched],
                                [(i, touched.index(name), l) for i, (name, l, _) in enumerate(sends)])
        if host is None:
            out, new = (), rider.run("grad_exchange_last")
        else:
            *out, new = host(rider)
        landing.update(zip(touched, new))
        return out

    grads = [None] * depth
    for l in reversed(range(depth)):
        W = dict(Ws[l], w_down=(slabs["w_down"], l), w_ple_gate=(slabs["w_ple_gate"], l))

        def attend_bwd(g, qf, kf, vb, o, do, lse, l=l):
            g["w_o"] = _w_o_shards(g["w_o"])
            sends = [(name, l, g[name]) for name in POST]
            if l + 1 < depth:
                sends += [(name, l + 1, grads[l + 1][name]) for name in PRE[1:]]
            return exchange(sends, functools.partial(_attn_bwd, qf, kf, vb, o, do, lse))

        if l + 1 < depth:
            W["host_mm_da"] = lambda run, l=l: exchange([(PRE[0], l + 1, grads[l + 1][PRE[0]])], run)
        else:
            W["host_mm_da"] = lambda run: (run(None),)

        dx, gp = _layer_bwd(dx, p[l], W, cs, sn, saved[l], attend_bwd)
        grads[l] = _unpad_grads(gp)
    exchange([(name, 0, grads[0][name]) for name in PRE], None)
    return sq, dx, grads, landing


def _pack_rows(vals, entries):
    depth = vals[entries[0][0]].shape[0]
    return jnp.concatenate([jnp.pad(vals[name].reshape(depth, -1), ((0, 0), (0, _pad128(n) - n))) for name, n in entries], axis=1)


def _unpack_rows(packed, entries):
    out, off = {}, 0
    for name, n in entries:
        out[name] = packed[:, off:off + n]
        off += _pad128(n)
    return out


def kernel(x, p, positions, g_mix, w_in, g_q_lat, w_uq, g_kv_lat, w_ukv, g_qn_nope, g_qn_rope, g_kn_nope, g_kn_rope, conv_w, g_out_attn, g_out_conv, w_o, g_mlp, w_up, w_down, g_ple, w_ple_gate, w_ple, loss_target, m_g_mix, m_w_in, m_g_q_lat, m_w_uq, m_g_kv_lat, m_w_ukv, m_g_qn_nope, m_g_qn_rope, m_g_kn_nope, m_g_kn_rope, m_conv_w, m_g_out_attn, m_g_out_conv, m_w_o, m_g_mlp, m_w_up, m_w_down, m_g_ple, m_w_ple_gate, m_w_ple, v_g_mix, v_w_in, v_g_q_lat, v_w_uq, v_g_kv_lat, v_w_ukv, v_g_qn_nope, v_g_qn_rope, v_g_kn_nope, v_g_kn_rope, v_conv_w, v_g_out_attn, v_g_out_conv, v_w_o, v_g_mlp, v_w_up, v_w_down, v_g_ple, v_w_ple_gate, v_w_ple):
    w = dict(g_mix=g_mix, w_in=w_in, g_q_lat=g_q_lat, w_uq=w_uq, g_kv_lat=g_kv_lat, w_ukv=w_ukv, g_qn_nope=g_qn_nope,
             g_qn_rope=g_qn_rope, g_kn_nope=g_kn_nope, g_kn_rope=g_kn_rope, conv_w=conv_w, g_out_attn=g_out_attn,
             g_out_conv=g_out_conv, w_o=w_o, g_mlp=g_mlp, w_up=w_up, w_down=w_down, g_ple=g_ple, w_ple_gate=w_ple_gate,
             w_ple=w_ple)
    m = dict(g_mix=m_g_mix, w_in=m_w_in, g_q_lat=m_g_q_lat, w_uq=m_w_uq, g_kv_lat=m_g_kv_lat, w_ukv=m_w_ukv,
             g_qn_nope=m_g_qn_nope, g_qn_rope=m_g_qn_rope, g_kn_nope=m_g_kn_nope, g_kn_rope=m_g_kn_rope, conv_w=m_conv_w,
             g_out_attn=m_g_out_attn, g_out_conv=m_g_out_conv, w_o=m_w_o, g_mlp=m_g_mlp, w_up=m_w_up, w_down=m_w_down,
             g_ple=m_g_ple, w_ple_gate=m_w_ple_gate, w_ple=m_w_ple)
    v = dict(g_mix=v_g_mix, w_in=v_w_in, g_q_lat=v_g_q_lat, w_uq=v_w_uq, g_kv_lat=v_g_kv_lat, w_ukv=v_w_ukv,
             g_qn_nope=v_g_qn_nope, g_qn_rope=v_g_qn_rope, g_kn_nope=v_g_kn_nope, g_kn_rope=v_g_kn_rope, conv_w=v_conv_w,
             g_out_attn=v_g_out_attn, g_out_conv=v_g_out_conv, w_o=v_w_o, g_mlp=v_g_mlp, w_up=v_w_up, w_down=v_w_down,
             g_ple=v_g_ple, w_ple_gate=v_w_ple_gate, w_ple=v_w_ple)
    depth = p.shape[0]
    ax, ay, ac = _place()
    chip = 2 * ax + ay
    c_arr = jnp.reshape(ac, (1,)).astype(jnp.int32)
    chip_arr = jnp.reshape(chip, (1,)).astype(jnp.int32)
    conv_shard =("conv_w", CONV_TAPS * CONV_WIDTH // N_CHIPS)
    conv_full = ("conv_w", CONV_TAPS * CONV_WIDTH)

    names = PRE + POST
    slabs = {name: _shard_slab(w[name], chip_arr, "shard_slab_" + name, name in ROW_SHARDED) for name in names}
    conv_rows = -(-depth * CONV_TAPS // SUBLANES) * SUBLANES
    conv_all = _allgather_small(jnp.pad(conv_w.reshape(depth * CONV_TAPS, LANES), ((0, conv_rows - depth * CONV_TAPS), (0, 0))),
                                "allgather_conv_w")
    conv_cat = jnp.concatenate([conv_all[2 * j, :depth * CONV_TAPS] for j in range(N_CHIPS)], axis=1)
    small = {name: w[name] for name, _ in SMALL}
    small["conv_w"] = conv_cat.reshape(depth, CONV_TAPS, CONV_WIDTH)

    sq, grad_x, grads, landing = _local_step(x[0], p[:, 0], positions[0], loss_target[0], slabs, small)

    halves = [_reduce_partials(jnp.stack([grads[l][name] for l in range(depth)]), landing[name], chip_arr, c_arr,
                               "grad_reduce_" + name) for name in names]
    g_out = dict(zip(names, _join_halves(halves)))

    reduced = SMALL + (conv_full, ("loss", LANES))
    stacked = {name: jnp.stack([grads[l][name] for l in range(depth)]) for name, _ in SMALL + (conv_full,)}
    stacked["loss"] = jnp.broadcast_to(sq[:1] * (0.5 / D_MODEL), (depth, LANES))
    g_small = _unpack_rows(_sum_devices(_allgather_small(_pack_rows(stacked, reduced), "allgather_small_grads")), reduced)
    loss = g_small["loss"][0, 0]
    for name, n in SMALL:
        g_out[name] = g_small[name]
    g_conv = g_small["conv_w"].reshape(depth, CONV_TAPS, CONV_WIDTH)
    g_out["conv_w"] = lax.dynamic_slice_in_dim(g_conv, chip * LANES, LANES, axis=2)

    delta, new_m, new_v = {}, {}, {}
    for name in names:
        view = (lambda t: jnp.swapaxes(t, 1, 2)) if w[name].shape[2] % LANES else (lambda t: t)
        d, m2, v2 = _adamw(view(w[name]), view(g_out[name]), view(m[name]), view(v[name]), "adamw_" + name)
        delta[name], new_m[name], new_v[name] = view(d), view(m2), view(v2)
    local = SMALL + (conv_shard,)
    pack = lambda vals: _pack_rows(vals, local)[None]
    d, m2, v2 = _adamw(pack(w), pack(g_out), pack(m), pack(v), "adamw_small")
    for res, packed_res in ((delta, d), (new_m, m2), (new_v, v2)):
        un = _unpack_rows(packed_res[0], local)
        for name, _ in local:
            res[name] = un[name].reshape(w[name].shape)

    return (loss, grad_x[None], *[g_out[n] for n in WEIGHT_ORDER], *[delta[n] for n in WEIGHT_ORDER],
            *[new_m[n] for n in WEIGHT_ORDER], *[new_v[n] for n in WEIGHT_ORDER])
```

```python
import functools

import jax
import jax.numpy as jnp
from jax import lax
from jax.experimental import pallas as pl
from jax.experimental.pallas import tpu as pltpu

F32 = jnp.float32
BF16 = jnp.bfloat16
MESH = pl.DeviceIdType.MESH

D_MODEL = 1024
N_HEADS = 8
QK_NOPE = 64
QK_ROPE = 32
QK_HEAD = QK_NOPE + QK_ROPE
V_HEAD = 64
Q_LORA = 384
KV_LORA = 256
ATTN_WIDTH = N_HEADS * V_HEAD
CONV_WIDTH = 512
CONV_TAPS = 3
D_FF = 4096
PLE_DIM = 256
ROPE_THETA = 10000.0
EPS = 1e-6
ATT_SCALE = QK_HEAD ** -0.5
LOG2E = 1.4426950408889634
ATT_SCALE_LOG2 = ATT_SCALE * LOG2E

ADAM_LR = 0.001
ADAM_B1 = 0.9
ADAM_B2 = 0.999
ADAM_EPS = 1e-08
ADAM_WD = 0.01
ADAM_STEP = 10

LANES = 128
SUBLANES = 8
HEAD_PAD = LANES
HP = N_HEADS * HEAD_PAD
ROPE_LO = QK_NOPE
ROPE_MID = QK_NOPE + QK_ROPE // 2
ROPE_HI = QK_NOPE + QK_ROPE
VMEM_LIMIT = 56 * 1024 * 1024

Z_Q, Z_KV, Z_KPE, Z_GB, Z_GC, Z_XIN = 0, 384, 640, 768, 1280, 1792
Z_COLS = 2304
Z_LAT = Z_KPE

ROW_TILE = 512
ATT_TILE = 256
ATT_CHAINS_FWD, ATT_CHAINS_BWD = 8, 8
MM_TM, MM_TN, MM_TK = 1024, 1024, 4096
MM_TM_MIN, MM_TK_MIN = 256, 512
MM_VMEM_BUDGET = 40 * 1024 * 1024

N_CHIPS = 4
N_DEV = 8

PRE = ("w_in", "w_uq", "w_ukv")
POST = ("w_o", "w_up", "w_down", "w_ple_gate", "w_ple")
ROW_SHARDED = ("w_o", "w_down", "w_ple_gate")
SMALL = (
    ("g_mix", 1024), ("g_q_lat", 384), ("g_kv_lat", 256), ("g_qn_nope", 64), ("g_qn_rope", 32), ("g_kn_nope", 64),
    ("g_kn_rope", 32), ("g_out_attn", 512), ("g_out_conv", 512), ("g_mlp", 1024), ("g_ple", 1024),
)
WEIGHT_ORDER = ("g_mix", "w_in", "g_q_lat", "w_uq", "g_kv_lat", "w_ukv", "g_qn_nope", "g_qn_rope", "g_kn_nope",
                "g_kn_rope", "conv_w", "g_out_attn", "g_out_conv", "w_o", "g_mlp", "w_up", "w_down", "g_ple",
                "w_ple_gate", "w_ple")


def _pallas(body, **kw):
    return pl.pallas_call(body, **kw)


def _cparams(*sem):
    return pltpu.CompilerParams(dimension_semantics=sem, vmem_limit_bytes=VMEM_LIMIT)


def _tile(dim, pref, unit=LANES):
    if dim <= pref:
        return dim
    t = (pref // unit) * unit
    while t > unit and dim % t:
        t -= unit
    assert dim % t == 0, (dim, pref)
    return t


def _pad128(n):
    return -(-n // LANES) * LANES


_DIMS = {"nn": (((1,), (0,)), ((), ())), "nt": (((1,), (1,)), ((), ())), "tn": (((0,), (0,)), ((), ()))}


def _mm(a, b, *, mode="nn", res=None, aux=None, gain=None, epi=None, out_dtype=F32, shard_out=None, rider=None, name):
    whole_rows = epi in ("rms", "rms_bwd")
    b, layer = b if isinstance(b, tuple) else (b, None)
    b_shape = b.shape if layer is None else b.shape[1:]
    if mode == "nn":
        (M, K), (K2, N) = a.shape, b_shape
    elif mode == "nt":
        (M, K), (N, K2) = a.shape, b_shape
    else:
        (K, M), (K2, N) = a.shape, b_shape
    assert K == K2, (a.shape, b_shape, mode)
    n_lim = k_lim = None
    m_lim = M // N_CHIPS if shard_out == 0 else None
    if shard_out == 1:
        n_lim = N // N_CHIPS
    extra = [t for t in (res, aux) if t is not None]
    out_bytes = 6 if epi in ("relu2", "rms", "rms_bwd") else jnp.dtype(out_dtype).itemsize
    if whole_rows:
        assert (n_lim or N) == N and N <= MM_TN and shard_out is None and gain is not None
        m_lim = min(M, MM_TM // 2)

    def footprint(tm, tn, tk):
        blocks = tm * tk * a.dtype.itemsize + tk * tn * 2 + tm * tn * (out_bytes + 4 * len(extra))
        return 2 * blocks + (tm * tn * 4 if tk < K else 0)

    tm, tn, tk = _tile(m_lim or M, MM_TM), _tile(n_lim or N, MM_TN), _tile(k_lim or K, MM_TK)
    while footprint(tm, tn, tk) > MM_VMEM_BUDGET and tk > MM_TK_MIN:
        tk = _tile(K, tk // 2)
    while footprint(tm, tn, tk) > MM_VMEM_BUDGET and tm > MM_TM_MIN:
        tm = _tile(M, tm // 2)
    nk = K // tk
    a_spec = pl.BlockSpec((tk, tm), lambda i, j, k: (k, i)) if mode == "tn" else pl.BlockSpec((tm, tk), lambda i, j, k: (i, k))
    if mode == "nt":
        b_block, b_rc = (tn, tk), (lambda i, j, k: (j, k))
    else:
        b_block, b_rc = (tk, tn), (lambda i, j, k: (k, j))
    if layer is None:
        b_spec = pl.BlockSpec(b_block, b_rc)
    else:
        b_spec = pl.BlockSpec((None,) + b_block, lambda i, j, k: (layer,) + b_rc(i, j, k))
    mn_spec = pl.BlockSpec((tm, tn), lambda i, j, k: (i, j))
    dims = _DIMS[mode]
    n_out = {"relu2": 2, "rms": 2, "rms_bwd": 3}.get(epi, 1)

    n_in = 2 + len(extra) + whole_rows
    n_grid = (M // tm, N // tn, nk)

    def body(*refs):
        a_ref, b_ref = refs[0], refs[1]
        extra_refs = refs[2:2 + len(extra)]
        gain_ref = refs[2 + len(extra)] if whole_rows else None
        if rider is None:
            tail = refs[n_in:]
        else:
            tail, ride = rider.split(refs[n_in:], n_out, int(nk > 1))
            ride((pl.program_id(0) * n_grid[1] + pl.program_id(1)) * nk + pl.program_id(2), n_grid[0] * n_grid[1] * nk)
        out_refs = tail[:n_out]
        prod = lax.dot_general(a_ref[...].astype(BF16), b_ref[...].astype(BF16), dims, preferred_element_type=F32)

        def finish(r):
            if epi == "rms_bwd":
                dx, dgc = _norm_bwd(extra_refs[1][...], r, gain_ref[...], N)
                dx = dx + extra_refs[0][...]
                out_refs[0][...] = dx
                out_refs[1][...] = dx.astype(BF16)
                _accumulate(out_refs[2], dgc)
                return
            if res is not None:
                r = r + extra_refs[0][...]
            if epi == "rms":
                out_refs[0][...] = r
                out_refs[1][...] = (r * _rinv(r, N) * gain_ref[...]).astype(BF16)
            elif epi == "relu2":
                out_refs[0][...] = r.astype(out_dtype)
                t = jnp.maximum(r, 0.0)
                out_refs[1][...] = (t * t).astype(BF16)
            elif epi == "drelu2":
                out_refs[0][...] = (r * (2.0 * jnp.maximum(extra_refs[-1][...].astype(F32), 0.0))).astype(out_dtype)
            else:
                out_refs[0][...] = r.astype(out_dtype)

        if nk == 1:
            finish(prod)
        else:
            acc = tail[n_out]
            k = pl.program_id(2)

            @pl.when(k == 0)
            def _():
                acc[...] = prod

            @pl.when(k > 0)
            def _():
                acc[...] += prod

            @pl.when(k == nk - 1)
            def _():
                finish(acc[...])

    if epi in ("relu2", "rms", "rms_bwd"):
        out_shape = [jax.ShapeDtypeStruct((M, N), out_dtype if epi == "relu2" else F32), jax.ShapeDtypeStruct((M, N), BF16)]
        out_specs = [mn_spec, mn_spec]
        if epi == "rms_bwd":
            out_shape.append(jax.ShapeDtypeStruct((SUBLANES, N), F32))
            out_specs.append(pl.BlockSpec((SUBLANES, tn), lambda i, j, k: (0, j)))
    elif shard_out == 0:
        per = (M // N_CHIPS) // tm
        out_shape = jax.ShapeDtypeStruct((N_CHIPS, M // N_CHIPS, N), out_dtype)
        out_specs = pl.BlockSpec((None, tm, tn), lambda i, j, k: (i // per, i % per, j))
    elif shard_out == 1:
        per = (N // N_CHIPS) // tn
        out_shape = jax.ShapeDtypeStruct((N_CHIPS, M, N // N_CHIPS), out_dtype)
        out_specs = pl.BlockSpec((None, tm, tn), lambda i, j, k: (j // per, i, j % per))
    else:
        out_shape = jax.ShapeDtypeStruct((M, N), out_dtype)
        out_specs = mn_spec
    gains = [gain] if whole_rows else []
    in_specs = [a_spec, b_spec] + [mn_spec] * len(extra) + [pl.BlockSpec((1, tn), lambda i, j, k: (0, j))] * len(gains)
    scratch = [pltpu.VMEM((tm, tn), F32)] if nk > 1 else []
    ordered = epi == "rms_bwd" or rider is not None
    params = _cparams(*(["arbitrary"] * 3 if ordered else ["parallel", "parallel", "arbitrary"]))
    if rider is None:
        return _pallas(body, name=name, grid=n_grid, in_specs=in_specs, out_specs=out_specs, out_shape=out_shape,
                       scratch_shapes=scratch, compiler_params=params)(a, b, *extra, *gains)
    as_list = lambda t: list(t) if isinstance(t, (list, tuple)) else [t]
    outs = _pallas(body, name=name, grid=n_grid, in_specs=in_specs + rider.in_specs(),
                   out_specs=as_list(out_specs) + rider.out_specs(), out_shape=as_list(out_shape) + rider.out_shape(),
                   scratch_shapes=scratch + rider.scratch(), input_output_aliases=rider.aliases(n_in, n_out),
                   compiler_params=params)(a, b, *extra, *gains, *rider.args())
    return (*outs[:n_out], outs[n_out:])


def _rows(ts, d, col=0):
    return pl.BlockSpec((ts, d), lambda i: (i, col))


def _gain(d):
    return pl.BlockSpec((1, d), lambda i: (0, 0))


def _accum(d):
    return pl.BlockSpec((SUBLANES, d), lambda i: (0, 0))


def _accumulate(ref, val):
    i = pl.program_id(0)

    @pl.when(i == 0)
    def _():
        ref[...] = jnp.zeros_like(ref)

    ref[...] += jnp.broadcast_to(jnp.sum(val, axis=0, keepdims=True), ref.shape)


def _rinv(x, n):
    return lax.rsqrt(jnp.sum(x * x, axis=-1, keepdims=True) / n + EPS)


def _norm_bwd(x, dy, g, n):
    r = _rinv(x, n)
    xhat = x * r
    dyg = dy * g
    dx = r * (dyg - xhat * (jnp.sum(dyg * xhat, axis=-1, keepdims=True) / n))
    return dx, dy * xhat


def _rms_fwd(x, g, name):
    S, D = x.shape
    ts = _tile(S, ROW_TILE, SUBLANES)

    def body(x_ref, g_ref, h_ref):
        xv = x_ref[...]
        h_ref[...] = (xv * _rinv(xv, D) * g_ref[...]).astype(BF16)

    return _pallas(body, name=name, grid=(S // ts,), in_specs=[_rows(ts, D), _gain(D)], out_specs=_rows(ts, D),
                   out_shape=jax.ShapeDtypeStruct((S, D), BF16), compiler_params=_cparams("parallel"))(x, g)


def _lat_fwd(z, gq, gkv):
    S = z.shape[0]
    ts = _tile(S, ROW_TILE, SUBLANES)

    def body(z_ref, gq_ref, gkv_ref, q_ref, kv_ref):
        zq = z_ref[:, Z_Q:Z_KV]
        zkv = z_ref[:, Z_KV:Z_KPE]
        q_ref[...] = (zq * _rinv(zq, Q_LORA) * gq_ref[...]).astype(BF16)
        kv_ref[...] = (zkv * _rinv(zkv, KV_LORA) * gkv_ref[...]).astype(BF16)

    return _pallas(body, name="lat_fwd", grid=(S // ts,), in_specs=[_rows(ts, Z_LAT), _gain(Q_LORA), _gain(KV_LORA)],
                   out_specs=[_rows(ts, Q_LORA), _rows(ts, KV_LORA)],
                   out_shape=[jax.ShapeDtypeStruct((S, Q_LORA), BF16), jax.ShapeDtypeStruct((S, KV_LORA), BF16)],
                   compiler_params=_cparams("parallel"))(z, gq, gkv)


def _lat_bwd(dq, dkv, z, gq, gkv):
    S = z.shape[0]
    ts = _tile(S, ROW_TILE, SUBLANES)

    def body(dq_ref, dkv_ref, z_ref, gq_ref, gkv_ref, dlat_ref, dgq_ref, dgkv_ref):
        dxq, cq = _norm_bwd(z_ref[:, Z_Q:Z_KV], dq_ref[...], gq_ref[...], Q_LORA)
        dxkv, ckv = _norm_bwd(z_ref[:, Z_KV:Z_KPE], dkv_ref[...], gkv_ref[...], KV_LORA)
        dlat_ref[:, Z_Q:Z_KV] = dxq.astype(BF16)
        dlat_ref[:, Z_KV:Z_KPE] = dxkv.astype(BF16)
        _accumulate(dgq_ref, cq)
        _accumulate(dgkv_ref, ckv)

    return _pallas(body, name="lat_bwd", grid=(S // ts,),
                   in_specs=[_rows(ts, Q_LORA), _rows(ts, KV_LORA), _rows(ts, Z_LAT), _gain(Q_LORA), _gain(KV_LORA)],
                   out_specs=[_rows(ts, Z_LAT), _accum(Q_LORA), _accum(KV_LORA)],
                   out_shape=[jax.ShapeDtypeStruct((S, Z_LAT), BF16), jax.ShapeDtypeStruct((SUBLANES, Q_LORA), F32),
                              jax.ShapeDtypeStruct((SUBLANES, KV_LORA), F32)],
                   compiler_params=_cparams("arbitrary"))(dq, dkv, z, gq, gkv)


def _head_masks():
    lane = lax.broadcasted_iota(jnp.int32, (1, HEAD_PAD), 1)
    return lane < ROPE_LO, (lane >= ROPE_LO) & (lane < ROPE_HI)


def _head_matrices():
    k = jnp.arange(HEAD_PAD)[:, None]
    j = jnp.arange(HEAD_PAD)[None, :]
    nope = (k < ROPE_LO) & (j < ROPE_LO)
    rope = (k >= ROPE_LO) & (k < ROPE_HI) & (j >= ROPE_LO) & (j < ROPE_HI)
    seg = jnp.where(nope, 1.0 / QK_NOPE, jnp.where(rope, 1.0 / QK_ROPE, 0.0))
    half = QK_ROPE // 2
    rot = jnp.where((j >= ROPE_LO) & (j < ROPE_MID) & (k == j + half), -1.0,
                    jnp.where((j >= ROPE_MID) & (j < ROPE_HI) & (k == j - half), 1.0, 0.0))
    return seg.astype(BF16), rot.astype(BF16)


_HEAD_MAT = pl.BlockSpec((HEAD_PAD, HEAD_PAD), lambda i: (0, 0))


def _split_dot(t, mat):
    hi = t.astype(BF16)
    lo = (t - hi.astype(F32)).astype(BF16)
    return jnp.dot(hi, mat, preferred_element_type=F32) + jnp.dot(lo, mat, preferred_element_type=F32)


def _qk_fwd(q_raw, kv_raw, z, cs, sn, gq, gk, seg, rot):
    S = q_raw.shape[0]
    ts = _tile(S, ROW_TILE, SUBLANES)

    def body(q_ref, k_ref, v_ref, kpe_ref, cs_ref, sn_ref, gq_ref, gk_ref, seg_ref, rot_ref, qf_ref, kf_ref, vb_ref):
        cos, sin, gqv, gkv = cs_ref[...], sn_ref[...], gq_ref[...], gk_ref[...]
        seg_m, rot_m = seg_ref[...], rot_ref[...]

        def norm(x, g):
            return x * lax.rsqrt(_split_dot(x * x, seg_m) + EPS) * g

        def rope(y):
            return y * cos + _split_dot(y, rot_m) * sin

        kr = rope(norm(kpe_ref[...], gkv))
        lane = lax.broadcasted_iota(jnp.int32, (1, HEAD_PAD), 1)
        for h in range(N_HEADS):
            sl = slice(h * HEAD_PAD, (h + 1) * HEAD_PAD)
            qf_ref[:, sl] = rope(norm(q_ref[:, sl], gqv)).astype(BF16)
            kf_ref[:, sl] = (norm(k_ref[:, sl], gkv) + kr).astype(BF16)
            vb_ref[:, sl] = jnp.where(lane == V_HEAD, 1.0, v_ref[:, sl]).astype(BF16)

    hd = jax.ShapeDtypeStruct((S, HP), BF16)
    return _pallas(body, name="qk_fwd", grid=(S // ts,),
                   in_specs=[_rows(ts, HP), _rows(ts, HP, 0), _rows(ts, HP, 1), _rows(ts, HEAD_PAD, Z_KPE // HEAD_PAD),
                             _rows(ts, HEAD_PAD), _rows(ts, HEAD_PAD), _gain(HEAD_PAD), _gain(HEAD_PAD), _HEAD_MAT, _HEAD_MAT],
                   out_specs=[_rows(ts, HP)] * 3, out_shape=[hd, hd, hd],
                   compiler_params=_cparams("parallel"))(q_raw, kv_raw, kv_raw, z, cs, sn, gq, gk, seg, rot)


def _qk_bwd(dqf, dkf, dv, q_raw, kv_raw, z, cs, sn, gq, gk, seg, rot):
    S = q_raw.shape[0]
    ts = _tile(S, ROW_TILE, SUBLANES)

    def body(dqf_ref, dkf_ref, dv_ref, q_ref, k_ref, kpe_ref, cs_ref, sn_ref, gq_ref, gk_ref, seg_ref, rot_ref,
             dq_ref, dkv_ref, dkpe_ref, dgq_ref, dgk_ref):
        m_n, m_r = _head_masks()
        cos, sin, gqv, gkv = cs_ref[...], sn_ref[...], gq_ref[...], gk_ref[...]
        seg_m, rot_m = seg_ref[...], rot_ref[...]

        def rope_t(w):
            return w * cos - _split_dot(w * sin, rot_m)

        def norm_bwd(x, dy, g):
            r = lax.rsqrt(_split_dot(x * x, seg_m) + EPS)
            xhat = x * r
            dyg = dy * g
            return r * (dyg - xhat * _split_dot(dyg * xhat, seg_m)), dy * xhat

        accq = jnp.zeros((ts, HEAD_PAD), F32)
        acck = jnp.zeros((ts, HEAD_PAD), F32)
        dkr = jnp.zeros((ts, HEAD_PAD), F32)
        for h in range(N_HEADS):
            sl = slice(h * HEAD_PAD, (h + 1) * HEAD_PAD)
            dx, c = norm_bwd(q_ref[:, sl], rope_t(dqf_ref[:, sl]), gqv)
            dq_ref[:, sl] = dx.astype(BF16)
            accq = accq + c
            dk = dkf_ref[:, sl]
            dkr = dkr + jnp.where(m_r, dk, 0.0)
            dx, c = norm_bwd(k_ref[:, sl], jnp.where(m_n, dk, 0.0), gkv)
            dkv_ref[:, sl] = dx.astype(BF16)
            acck = acck + c
            dkv_ref[:, HP + h * HEAD_PAD:HP + (h + 1) * HEAD_PAD] = dv_ref[:, sl].astype(BF16)
        dx, c = norm_bwd(kpe_ref[...], rope_t(dkr), gkv)
        dkpe_ref[...] = dx.astype(BF16)
        _accumulate(dgq_ref, accq)
        _accumulate(dgk_ref, acck + c)

    return _pallas(body, name="qk_bwd", grid=(S // ts,),
                   in_specs=[_rows(ts, HP)] * 4 + [_rows(ts, HP, 0), _rows(ts, HEAD_PAD, Z_KPE // HEAD_PAD),
                                                   _rows(ts, HEAD_PAD), _rows(ts, HEAD_PAD), _gain(HEAD_PAD), _gain(HEAD_PAD),
                                                   _HEAD_MAT, _HEAD_MAT],
                   out_specs=[_rows(ts, HP), _rows(ts, 2 * HP), _rows(ts, HEAD_PAD), _accum(HEAD_PAD), _accum(HEAD_PAD)],
                   out_shape=[jax.ShapeDtypeStruct((S, HP), BF16), jax.ShapeDtypeStruct((S, 2 * HP), BF16),
                              jax.ShapeDtypeStruct((S, HEAD_PAD), BF16), jax.ShapeDtypeStruct((SUBLANES, HEAD_PAD), F32),
                              jax.ShapeDtypeStruct((SUBLANES, HEAD_PAD), F32)],
                   compiler_params=_cparams("arbitrary"))(dqf, dkf, dv, q_raw, kv_raw, z, cs, sn, gq, gk, seg, rot)


def _causal(t):
    row = lax.broadcasted_iota(jnp.int32, (t, t), 0)
    col = lax.broadcasted_iota(jnp.int32, (t, t), 1)
    return col <= row


def _attn_tiles(S, chains):
    t = _tile(S, ATT_TILE, SUBLANES)
    return t, min(chains, S // t)


def _attn_fwd(qf, kf, vb, rider):
    S = qf.shape[0]
    t, nc = _attn_tiles(S, ATT_CHAINS_FWD)
    tq = nc * t
    nqt = S // tq

    def body(q_ref, k_ref, v_ref, *rest):
        (o_ref, lse_ref, m_ref, acc_ref), ride = rider.split(rest, 2, 2)
        qt = pl.program_id(1)
        ride(pl.program_id(0) * nqt + qt, N_HEADS * nqt)
        m_ref[...] = jnp.full_like(m_ref, -jnp.inf)
        acc_ref[...] = jnp.zeros_like(acc_ref)

        def against(rows, first, diagonal):
            kb, vb_ = k_ref[rows, :], v_ref[rows, :]
            subs = [slice(a * t, (a + 1) * t) for a in range(first, nc)]
            s = [lax.dot_general(q_ref[sub, :], kb, _DIMS["nt"], preferred_element_type=F32) * ATT_SCALE_LOG2 for sub in subs]
            if diagonal:
                s[0] = jnp.where(_causal(t), s[0], -jnp.inf)
            m_old = [m_ref[sub, :] for sub in subs]
            m_new = [jnp.maximum(mo, jnp.max(sa, axis=-1, keepdims=True)) for mo, sa in zip(m_old, s)]
            p = [jnp.exp2(sa - jnp.concatenate([mn] * (t // HEAD_PAD), axis=1)).astype(BF16) for sa, mn in zip(s, m_new)]
            for sub, mo, mn, pa in zip(subs, m_old, m_new, p):
                acc_ref[sub, :] = jnp.exp2(mo - mn) * acc_ref[sub, :] + jnp.dot(pa, vb_, preferred_element_type=F32)
                m_ref[sub, :] = mn

        def trip(j, carry):
            against(pl.ds(pl.multiple_of(j * t, t), t), 0, False)
            return carry

        lax.fori_loop(0, nc * qt, trip, 0)
        for d in range(nc):
            against(pl.ds(pl.multiple_of((nc * qt + d) * t, t), t), d, True)
        acc = acc_ref[...]
        l = acc[:, V_HEAD:V_HEAD + 1]
        lane = lax.broadcasted_iota(jnp.int32, (1, HEAD_PAD), 1)
        o_ref[...] = jnp.where(lane < V_HEAD, acc / l, 0.0)
        lse_ref[...] = m_ref[...] + jnp.log(l) * LOG2E

    tile = pl.BlockSpec((tq, HEAD_PAD), lambda h, i: (i, h))
    full = pl.BlockSpec((S, HEAD_PAD), lambda h, i: (0, h))
    outs = _pallas(body, name="attn_fwd", grid=(N_HEADS, nqt), in_specs=[tile, full, full] + rider.in_specs(),
                   out_specs=[tile, pl.BlockSpec((None, tq, HEAD_PAD), lambda h, i: (h, i, 0))] + rider.out_specs(),
                   out_shape=[jax.ShapeDtypeStruct((S, HP), F32), jax.ShapeDtypeStruct((N_HEADS, S, HEAD_PAD), F32)]
                   + rider.out_shape(),
                   scratch_shapes=[pltpu.VMEM((tq, HEAD_PAD), F32), pltpu.VMEM((tq, HEAD_PAD), F32)] + rider.scratch(),
                   input_output_aliases=rider.aliases(3, 2),
                   compiler_params=_cparams("arbitrary", "arbitrary"))(qf, kf, vb, *rider.args())
    return outs[0], outs[1], outs[2:]


def _attn_bwd(qf, kf, vb, o, do, lse, rider):
    S = qf.shape[0]
    t, nc = _attn_tiles(S, ATT_CHAINS_BWD)
    tkv = nc * t
    nq = S // t
    nkt = S // tkv

    def body(q_ref, k_ref, v_ref, o_ref, do_ref, lse_ref, *rest):
        (dq_ref, dk_ref, dv_ref), ride = rider.split(rest, 3, 0)
        kt = pl.program_id(1)
        ride(pl.program_id(0) * nkt + kt, N_HEADS * nkt)

        @pl.when(kt == 0)
        def _():
            dq_ref[...] = jnp.zeros_like(dq_ref)

        dk_ref[...] = jnp.zeros_like(dk_ref)
        dv_ref[...] = jnp.zeros_like(dv_ref)

        def q_block(rows):
            dof = do_ref[rows, :]
            delta = jnp.sum(dof * o_ref[rows, :], axis=-1, keepdims=True)
            return q_ref[rows, :], dof.astype(BF16), lse_ref[rows, :][:, :1], delta

        def against(rows, n_sub, diagonal):
            q, dob, lse, delta = q_block(rows)
            subs = [slice(b * t, (b + 1) * t) for b in range(n_sub)]
            kbs = [k_ref[sub, :] for sub in subs]
            s = [lax.dot_general(q, kb, _DIMS["nt"], preferred_element_type=F32) for kb in kbs]
            dp = [lax.dot_general(dob, v_ref[sub, :], _DIMS["nt"], preferred_element_type=F32) for sub in subs]
            p = [jnp.exp2(sb * ATT_SCALE_LOG2 - lse) for sb in s]
            if diagonal:
                p[-1] = jnp.where(_causal(t), p[-1], 0.0)
            ds = [(pb * (dpb - delta) * ATT_SCALE).astype(BF16) for pb, dpb in zip(p, dp)]
            for sub, pb in zip(subs, p):
                dv_ref[sub, :] += lax.dot_general(pb.astype(BF16), dob, _DIMS["tn"], preferred_element_type=F32)
            for sub, dsb in zip(subs, ds):
                dk_ref[sub, :] += lax.dot_general(dsb, q, _DIMS["tn"], preferred_element_type=F32)
            dq_ref[rows, :] += sum(jnp.dot(dsb, kb, preferred_element_type=F32) for dsb, kb in zip(ds, kbs))

        for a in range(nc):
            against(pl.ds(pl.multiple_of((nc * kt + a) * t, t), t), a + 1, True)

        def trip(i, carry):
            against(pl.ds(pl.multiple_of(i * t, t), t), nc, False)
            return carry

        lax.fori_loop(nc * (kt + 1), nq, trip, 0)

    tile = pl.BlockSpec((tkv, HEAD_PAD), lambda h, j: (j, h))
    full = pl.BlockSpec((S, HEAD_PAD), lambda h, j: (0, h))
    hd = jax.ShapeDtypeStruct((S, HP), F32)
    outs = _pallas(body, name="attn_bwd", grid=(N_HEADS, nkt),
                   in_specs=[full, tile, tile, full, full, pl.BlockSpec((None, S, HEAD_PAD), lambda h, j: (h, 0, 0))]
                   + rider.in_specs(),
                   out_specs=[full, tile, tile] + rider.out_specs(), out_shape=[hd, hd, hd] + rider.out_shape(),
                   scratch_shapes=rider.scratch(), input_output_aliases=rider.aliases(6, 3),
                   compiler_params=_cparams("arbitrary", "arbitrary"))(qf, kf, vb, o, do, lse, *rider.args())
    return outs[0], outs[1], outs[2], outs[3:]


def _shift_down(u, j, row):
    return jnp.where(row >= j, pltpu.roll(u, j, 0), 0.0)


def _shift_up(u, j, row, s):
    return jnp.where(row < s - j, pltpu.roll(u, s - j, 0), 0.0)


def _conv_cols(s, first_tile):
    return pl.BlockSpec((s, LANES), lambda cb: (0, first_tile + cb))


def _conv_fwd(z, cw8):
    S = z.shape[0]

    def body(gb_ref, gc_ref, xin_ref, w_ref, out_ref):
        row = lax.broadcasted_iota(jnp.int32, (S, LANES), 0)
        u = gc_ref[...] * xin_ref[...]
        y = w_ref[0:1, :] * u
        for j in range(1, CONV_TAPS):
            y = y + w_ref[j:j + 1, :] * _shift_down(u, j, row)
        out_ref[...] = gb_ref[...] * y

    return _pallas(body, name="conv_fwd", grid=(CONV_WIDTH // LANES,),
                   in_specs=[_conv_cols(S, Z_GB // LANES), _conv_cols(S, Z_GC // LANES), _conv_cols(S, Z_XIN // LANES),
                             pl.BlockSpec((SUBLANES, LANES), lambda cb: (0, cb))],
                   out_specs=_conv_cols(S, 0), out_shape=jax.ShapeDtypeStruct((S, CONV_WIDTH), F32),
                   compiler_params=_cparams("parallel"))(z, z, z, cw8)


def _conv_bwd(dconv, z, cw8):
    S = z.shape[0]

    def body(d_ref, gb_ref, gc_ref, xin_ref, w_ref, dgb_ref, dgc_ref, dxin_ref, dw_ref):
        row = lax.broadcasted_iota(jnp.int32, (S, LANES), 0)
        gc, xin, d = gc_ref[...], xin_ref[...], d_ref[...]
        u = gc * xin
        dy = d * gb_ref[...]
        y = w_ref[0:1, :] * u
        du = w_ref[0:1, :] * dy
        dw = [jnp.sum(dy * u, axis=0, keepdims=True)]
        for j in range(1, CONV_TAPS):
            uj = _shift_down(u, j, row)
            y = y + w_ref[j:j + 1, :] * uj
            du = du + w_ref[j:j + 1, :] * _shift_up(dy, j, row, S)
            dw.append(jnp.sum(dy * uj, axis=0, keepdims=True))
        dgb_ref[...] = (d * y).astype(BF16)
        dgc_ref[...] = (du * xin).astype(BF16)
        dxin_ref[...] = (du * gc).astype(BF16)
        tap = lax.broadcasted_iota(jnp.int32, (SUBLANES, LANES), 0)
        dw_ref[...] = sum(jnp.where(tap == j, dw[j], 0.0) for j in range(CONV_TAPS))

    col = _conv_cols(S, 0)
    sd = jax.ShapeDtypeStruct((S, CONV_WIDTH), BF16)
    return _pallas(body, name="conv_bwd", grid=(CONV_WIDTH // LANES,),
                   in_specs=[col, _conv_cols(S, Z_GB // LANES), _conv_cols(S, Z_GC // LANES), _conv_cols(S, Z_XIN // LANES),
                             pl.BlockSpec((SUBLANES, LANES), lambda cb: (0, cb))],
                   out_specs=[col, col, col, pl.BlockSpec((SUBLANES, LANES), lambda cb: (0, cb))],
                   out_shape=[sd, sd, sd, jax.ShapeDtypeStruct((SUBLANES, CONV_WIDTH), F32)],
                   compiler_params=_cparams("parallel"))(dconv, z, z, z, cw8)


def _mix_fwd(o, conv, ga, gc):
    S = o.shape[0]
    ts = _tile(S, ROW_TILE, SUBLANES)

    def body(o_ref, c_ref, ga_ref, gc_ref, out_ref):
        ov, cv = o_ref[...], c_ref[...]
        out_ref[:, :HP] = (ov * _rinv(ov, ATTN_WIDTH) * ga_ref[...]).astype(BF16)
        out_ref[:, HP:] = (cv * _rinv(cv, CONV_WIDTH) * gc_ref[...]).astype(BF16)

    return _pallas(body, name="mix_fwd", grid=(S // ts,),
                   in_specs=[_rows(ts, HP), _rows(ts, CONV_WIDTH), _gain(HP), _gain(CONV_WIDTH)],
                   out_specs=_rows(ts, HP + CONV_WIDTH), out_shape=jax.ShapeDtypeStruct((S, HP + CONV_WIDTH), BF16),
                   compiler_params=_cparams("parallel"))(o, conv, ga, gc)


def _mix_bwd(dmixed, o, conv, ga, gc):
    S = o.shape[0]
    ts = _tile(S, ROW_TILE, SUBLANES)

    def body(d_ref, o_ref, c_ref, ga_ref, gc_ref, do_ref, dc_ref, dga_ref, dgc_ref):
        dx, ca = _norm_bwd(o_ref[...], d_ref[:, :HP], ga_ref[...], ATTN_WIDTH)
        do_ref[...] = dx
        dx, cc = _norm_bwd(c_ref[...], d_ref[:, HP:], gc_ref[...], CONV_WIDTH)
        dc_ref[...] = dx
        _accumulate(dga_ref, ca)
        _accumulate(dgc_ref, cc)

    return _pallas(body, name="mix_bwd", grid=(S // ts,),
                   in_specs=[_rows(ts, HP + CONV_WIDTH), _rows(ts, HP), _rows(ts, CONV_WIDTH), _gain(HP), _gain(CONV_WIDTH)],
                   out_specs=[_rows(ts, HP), _rows(ts, CONV_WIDTH), _accum(HP), _accum(CONV_WIDTH)],
                   out_shape=[jax.ShapeDtypeStruct((S, HP), F32), jax.ShapeDtypeStruct((S, CONV_WIDTH), F32),
                              jax.ShapeDtypeStruct((SUBLANES, HP), F32), jax.ShapeDtypeStruct((SUBLANES, CONV_WIDTH), F32)],
                   compiler_params=_cparams("arbitrary"))(dmixed, o, conv, ga, gc)


def _ple_fwd(x, gl, pe, g_next):
    S, D = x.shape
    ts = _tile(S, ROW_TILE, SUBLANES)

    def body(x_ref, gl_ref, pe_ref, g_ref, out_ref, h_ref):
        y = x_ref[...] + jax.nn.sigmoid(gl_ref[...]) * pe_ref[...]
        out_ref[...] = y
        h_ref[...] = (y * _rinv(y, D) * g_ref[...]).astype(BF16)

    return _pallas(body, name="ple_fwd", grid=(S // ts,), in_specs=[_rows(ts, D)] * 3 + [_gain(D)], out_specs=[_rows(ts, D)] * 2,
                   out_shape=[jax.ShapeDtypeStruct((S, D), F32), jax.ShapeDtypeStruct((S, D), BF16)],
                   compiler_params=_cparams("parallel"))(x, gl, pe, g_next)


def _ple_bwd(dx, gl, pe):
    S, D = dx.shape
    ts = _tile(S, ROW_TILE, SUBLANES)

    def body(dx_ref, gl_ref, pe_ref, dpe_ref, dgl_ref):
        d = dx_ref[...]
        gate = jax.nn.sigmoid(gl_ref[...])
        dpe_ref[...] = (d * gate).astype(BF16)
        dgl_ref[...] = (d * pe_ref[...] * (gate * (1.0 - gate))).astype(BF16)

    sd = jax.ShapeDtypeStruct((S, D), BF16)
    return _pallas(body, name="ple_bwd", grid=(S // ts,), in_specs=[_rows(ts, D)] * 3, out_specs=[_rows(ts, D)] * 2,
                   out_shape=[sd, sd], compiler_params=_cparams("parallel"))(dx, gl, pe)


def _loss_grad(y, target):
    S, D = y.shape
    ts = _tile(S, ROW_TILE, SUBLANES)

    def body(y_ref, t_ref, dy_ref, sq_ref):
        e = y_ref[...] - t_ref[...]
        dy_ref[...] = e / D

        @pl.when(pl.program_id(0) == 0)
        def _():
            sq_ref[...] = jnp.zeros_like(sq_ref)

        sq_ref[...] += jnp.broadcast_to(jnp.sum(jnp.sum(e * e, axis=1, keepdims=True), axis=0, keepdims=True), sq_ref.shape)

    return _pallas(body, name="loss_grad", grid=(S // ts,), in_specs=[_rows(ts, D)] * 2,
                   out_specs=[_rows(ts, D), _accum(LANES)],
                   out_shape=[jax.ShapeDtypeStruct((S, D), F32), jax.ShapeDtypeStruct((SUBLANES, LANES), F32)],
                   compiler_params=_cparams("arbitrary"))(y, target)


def _adamw(w, g, m, v, name):
    L, R, C = w.shape
    tr = _tile(R, ROW_TILE, SUBLANES)

    def body(w_ref, g_ref, m_ref, v_ref, d_ref, m2_ref, v2_ref):
        gv = g_ref[...]
        m2 = ADAM_B1 * m_ref[...] + (1.0 - ADAM_B1) * gv
        v2 = ADAM_B2 * v_ref[...] + (1.0 - ADAM_B2) * (gv * gv)
        m_hat = m2 / (1.0 - ADAM_B1 ** ADAM_STEP)
        v_hat = v2 / (1.0 - ADAM_B2 ** ADAM_STEP)
        d_ref[...] = -ADAM_LR * (m_hat / (jnp.sqrt(v_hat) + ADAM_EPS) + ADAM_WD * w_ref[...])
        m2_ref[...] = m2
        v2_ref[...] = v2

    sd = jax.ShapeDtypeStruct((L, R, C), F32)
    spec = pl.BlockSpec((None, tr, C), lambda l, i: (l, i, 0))
    return _pallas(body, name=name, grid=(L, R // tr), in_specs=[spec] * 4, out_specs=[spec] * 3, out_shape=[sd, sd, sd],
                   compiler_params=_cparams("parallel", "parallel"))(w, g, m, v)


def _place():
    return lax.axis_index("x"), lax.axis_index("y"), lax.axis_index("c")


def _other_chips(x, y):
    return [(1 - x, y), (x, 1 - y), (1 - x, 1 - y)]


HBM = pl.BlockSpec(memory_space=pl.ANY)
VMEM_SPEC = pl.BlockSpec(memory_space=pltpu.VMEM)


def _remote_copy(send_sems, recv_sems, k, src, dst, to):
    return pltpu.make_async_remote_copy(src_ref=src, dst_ref=dst, send_sem=send_sems.at[k], recv_sem=recv_sems.at[k],
                                        device_id=to, device_id_type=MESH)


def _comm_call(body, name, arrays, out_shapes, n_remote, in_place=False):
    scratch = [pltpu.SemaphoreType.DMA((n_remote,)), pltpu.SemaphoreType.DMA((n_remote,))]
    aliases = {i: i for i in range(len(arrays))} if in_place else {}
    return _pallas(body, name=name, in_specs=[HBM] * len(arrays), out_specs=[HBM] * len(out_shapes), out_shape=out_shapes,
                   scratch_shapes=scratch, input_output_aliases=aliases,
                   compiler_params=pltpu.CompilerParams(has_side_effects=True))(*arrays)


def _shard_slab(w, chip_arr, name, stack_rows):
    L, R, C = w.shape
    tr = _tile(R, ROW_TILE, 16)
    nb = R // tr

    def body(j_ref, w_ref, out_ref):
        out_ref[...] = w_ref[...].astype(BF16)

    if stack_rows:
        out_shape = jax.ShapeDtypeStruct((L, N_CHIPS * R, C), BF16)
        out_spec = pl.BlockSpec((None, tr, C), lambda l, i, j_ref: (l, j_ref[0] * nb + i, 0))
    else:
        out_shape = jax.ShapeDtypeStruct((N_CHIPS, L, R, C), BF16)
        out_spec = pl.BlockSpec((None, None, tr, C), lambda l, i, j_ref: (j_ref[0], l, i, 0))
    grid_spec = pltpu.PrefetchScalarGridSpec(
        num_scalar_prefetch=1, grid=(L, nb), in_specs=[pl.BlockSpec((None, tr, C), lambda l, i, j_ref: (l, i, 0))],
        out_specs=out_spec)
    return _pallas(body, name=name, grid_spec=grid_spec, out_shape=out_shape,
                   compiler_params=_cparams("parallel", "parallel"))(chip_arr, w)


def _shard_dims(slab):
    return (slab.shape[2], slab.shape[3]) if slab.ndim == 4 else (slab.shape[1] // N_CHIPS, slab.shape[2])


def _shard_rows(ref, chip, l, which):
    ks, _ = _shard_dims(ref)
    h = ks // 2
    return ref.at[chip, l, pl.ds(which * h, h)] if len(ref.shape) == 4 else ref.at[l, pl.ds(chip * ks + which * h, h)]


class Rider:
    def __init__(self, reads, inplace, n_sems, stages):
        self.reads, self.inplace, self.n_sems, self.stages = list(reads), list(inplace), n_sems, stages

    def args(self):
        return self.reads + self.inplace

    def in_specs(self):
        return [HBM] * len(self.args())

    def out_specs(self):
        return [HBM] * len(self.inplace)

    def out_shape(self):
        return [jax.ShapeDtypeStruct(a.shape, a.dtype) for a in self.inplace]

    def scratch(self):
        return [pltpu.SemaphoreType.DMA((self.n_sems,)), pltpu.SemaphoreType.DMA((self.n_sems,))] if self.n_sems else []

    def aliases(self, n_host_in, n_host_out):
        return {n_host_in + len(self.reads) + j: n_host_out + j for j in range(len(self.inplace))}

    def split(self, rest, n_host_out, n_host_scratch):
        n_r, n_io = len(self.reads), len(self.inplace)
        reads = rest[:n_r]
        host_out = rest[n_r + n_io:n_r + n_io + n_host_out]
        outs = rest[n_r + n_io + n_host_out:n_r + 2 * n_io + n_host_out]
        host_scratch = rest[n_r + 2 * n_io + n_host_out:n_r + 2 * n_io + n_host_out + n_host_scratch]
        sems = rest[n_r + 2 * n_io + n_host_out + n_host_scratch:]

        def ride(step, n_steps):
            at = [0, n_steps - 1] if len(self.stages) == 2 else [0, (3 * n_steps) // 4, n_steps - 1]
            for s, stage in zip(at, self.stages):
                pl.when(step == s)(functools.partial(stage, reads, outs, *sems))

        return tuple(host_out) + tuple(host_scratch), ride

    def run(self, name):
        n_r, n_io = len(self.reads), len(self.inplace)

        def body(*refs):
            for stage in self.stages:
                stage(refs[:n_r], refs[n_r + n_io:n_r + 2 * n_io], *refs[n_r + 2 * n_io:])

        return _pallas(body, name=name, in_specs=self.in_specs(), out_specs=self.out_specs(), out_shape=self.out_shape(),
                       scratch_shapes=self.scratch(), input_output_aliases=self.aliases(0, 0),
                       compiler_params=pltpu.CompilerParams(has_side_effects=True))(*self.args())


def _allgather_rider(slabs, items):
    def stage(which, reads, outs, send_sems, recv_sems):
        x, y, c = _place()
        me, sibling = (x, y, c), (x, y, 1 - c)
        copy = functools.partial(_remote_copy, send_sems, recv_sems)
        for n, (j, l) in enumerate(items):
            ref = outs[j]
            own = _shard_rows(ref, 2 * x + y, l, c)
            for k, (cx, cy) in enumerate(_other_chips(x, y)):
                arrived = _shard_rows(ref, 2 * cx + cy, l, c)
                if which == 0:
                    copy(6 * n + k, own, own, (cx, cy, c)).start()
                elif which == 1:
                    copy(6 * n + k, arrived, arrived, me).wait_recv()
                    copy(6 * n + 3 + k, arrived, arrived, sibling).start()
                else:
                    passed = _shard_rows(ref, 2 * cx + cy, l, 1 - c)
                    copy(6 * n + 3 + k, passed, passed, me).wait_recv()
                    copy(6 * n + k, own, own, me).wait_send()
                    copy(6 * n + 3 + k, arrived, arrived, me).wait_send()

    return Rider([], slabs, 6 * len(items), [functools.partial(stage, w) for w in range(3)])


def _exchange_rider(grads, landing, items):
    def stage(start, reads, outs, send_sems, recv_sems):
        x, y, c = _place()
        for n, (i, j, l) in enumerate(items):
            g_ref, r_ref = reads[i], outs[j]
            h = g_ref.shape[1] // 2
            for k in range(1, N_DEV):
                px, py, pc = (1 - x if k & 4 else x, 1 - y if k & 2 else y, 1 - c if k & 1 else c)
                cp = _remote_copy(send_sems, recv_sems, 7 * n + k - 1, g_ref.at[2 * px + py, pl.ds(pc * h, h)],
                                  r_ref.at[l, k - 1], (px, py, pc))
                if start:
                    cp.start()
                else:
                    cp.wait()

    return Rider(grads, landing, 7 * len(items), [functools.partial(stage, True), functools.partial(stage, False)])


def _reduce_partials(g, r, chip_arr, c_arr, name):
    L, n, R, C = g.shape
    H = R // 2
    th = _tile(H, ROW_TILE, 16)
    nb = H // th

    def body(j_ref, c_ref, g_ref, r_ref, out_ref):
        s = g_ref[...].astype(F32)
        for k in range(N_DEV - 1):
            s = s + r_ref[k].astype(F32)
        out_ref[...] = s

    grid_spec = pltpu.PrefetchScalarGridSpec(
        num_scalar_prefetch=2, grid=(L, nb),
        in_specs=[pl.BlockSpec((None, None, th, C), lambda l, i, j_ref, c_ref: (l, j_ref[0], c_ref[0] * nb + i, 0)),
                  pl.BlockSpec((None, N_DEV - 1, th, C), lambda l, i, j_ref, c_ref: (l, 0, i, 0))],
        out_specs=pl.BlockSpec((None, th, C), lambda l, i, j_ref, c_ref: (l, c_ref[0] * nb + i, 0)))
    return _pallas(body, name=name, grid_spec=grid_spec, out_shape=jax.ShapeDtypeStruct((L, R, C), F32),
                   compiler_params=_cparams("parallel", "parallel"))(chip_arr, c_arr, g, r)


def _join_halves(fs):
    n = len(fs)

    def body(*refs):
        out_refs, (send_sems, recv_sems) = refs[n:2 * n], refs[2 * n:]
        x, y, c = _place()
        cps = []
        for i, out_ref in enumerate(out_refs):
            h = out_ref.shape[1] // 2
            mine = out_ref.at[:, pl.ds(c * h, h)]
            cps.append(_remote_copy(send_sems, recv_sems, i, mine, mine, (x, y, 1 - c)))
            cps[-1].start()
        for cp in cps:
            cp.wait()

    return _comm_call(body, "grad_join_halves", fs, [jax.ShapeDtypeStruct(f.shape, f.dtype) for f in fs], n, in_place=True)


def _allgather_small(v, name):
    R, C = v.shape

    def body(v_ref, out_ref, send_sems, recv_sems):
        x, y, c = _place()
        me = 4 * x + 2 * y + c
        out_ref[me] = v_ref[...]
        cps = []
        for k in range(1, N_DEV):
            peer = (1 - x if k & 4 else x, 1 - y if k & 2 else y, 1 - c if k & 1 else c)
            cps.append(pltpu.make_async_remote_copy(src_ref=v_ref, dst_ref=out_ref.at[me], send_sem=send_sems.at[k - 1],
                                                    recv_sem=recv_sems.at[k - 1], device_id=peer, device_id_type=MESH))
        for cp in cps:
            cp.start()
        for cp in cps:
            cp.wait()

    return _pallas(body, name=name, in_specs=[VMEM_SPEC], out_specs=VMEM_SPEC,
                   out_shape=jax.ShapeDtypeStruct((N_DEV, R, C), v.dtype),
                   scratch_shapes=[pltpu.SemaphoreType.DMA((N_DEV - 1,)), pltpu.SemaphoreType.DMA((N_DEV - 1,))],
                   compiler_params=pltpu.CompilerParams(has_side_effects=True))(v)


def _sum_devices(g):
    n, R, C = g.shape

    def body(g_ref, out_ref):
        s = g_ref[0]
        for d in range(1, n):
            s = s + g_ref[d]
        out_ref[...] = s

    return _pallas(body, name="sum_devices", in_specs=[VMEM_SPEC], out_specs=VMEM_SPEC,
                   out_shape=jax.ShapeDtypeStruct((R, C), g.dtype))(g)


def _pad_heads(a, width):
    lead = a.shape[:-1]
    a = a.reshape(lead + (N_HEADS, width))
    a = jnp.pad(a, [(0, 0)] * len(lead) + [(0, 0), (0, HEAD_PAD - width)])
    return a.reshape(lead + (HP,))


def _unpad_heads(a, width):
    lead = a.shape[:-1]
    return a.reshape(lead + (N_HEADS, HEAD_PAD))[..., :width].reshape(lead + (N_HEADS * width,))


def _pre_weights(slabs, l):
    cols = lambda name: slabs[name][:, l].transpose(1, 0, 2).reshape(slabs[name].shape[2], -1)
    w_in = cols("w_in")
    kpe = jnp.pad(w_in[:, 640:672], ((0, 0), (ROPE_LO, HEAD_PAD - ROPE_HI)))
    w_ukv = cols("w_ukv").reshape(KV_LORA, N_HEADS, QK_NOPE + V_HEAD)
    return dict(
        w_in=jnp.concatenate([w_in[:, :640], kpe, w_in[:, 672:]], axis=1),
        w_uq=_pad_heads(cols("w_uq"), QK_HEAD),
        w_ukv=jnp.concatenate([_pad_heads(w_ukv[..., :QK_NOPE].reshape(KV_LORA, -1), QK_NOPE),
                               _pad_heads(w_ukv[..., QK_NOPE:].reshape(KV_LORA, -1), V_HEAD)], axis=1),
    )


def _post_weights(slabs, l):
    cols = lambda name: slabs[name][:, l].transpose(1, 0, 2).reshape(slabs[name].shape[2], -1)
    w_o = slabs["w_o"][l]
    return dict(
        w_o=jnp.concatenate([_pad_heads(w_o[:ATTN_WIDTH].T, V_HEAD).T, w_o[ATTN_WIDTH:]], axis=0),
        w_up=cols("w_up"), w_down=(slabs["w_down"], l), w_ple_gate=(slabs["w_ple_gate"], l), w_ple=cols("w_ple"),
    )


def _gains(small, l):
    row = lambda name: small[name][l].reshape(1, -1)
    headrow = lambda a, b: jnp.pad(jnp.concatenate([small[a][l], small[b][l]]), (0, HEAD_PAD - QK_HEAD)).reshape(1, HEAD_PAD)
    return dict(
        g_mix=row("g_mix"), g_q_lat=row("g_q_lat"), g_kv_lat=row("g_kv_lat"), g_mlp=row("g_mlp"), g_ple=row("g_ple"),
        g_out_conv=row("g_out_conv"), g_out_attn=_pad_heads(small["g_out_attn"][l], V_HEAD).reshape(1, HP),
        gq=headrow("g_qn_nope", "g_qn_rope"), gk=headrow("g_kn_nope", "g_kn_rope"),
        cw8=jnp.pad(small["conv_w"][l], ((0, SUBLANES - CONV_TAPS), (0, 0))),
        head_mats=_head_matrices(),
    )


def _w_o_shards(dw_o):
    return jnp.concatenate([_unpad_heads(dw_o[:HP].T, V_HEAD).T, dw_o[HP:]], axis=0).reshape(N_CHIPS, -1, D_MODEL)


def _unpad_grads(gp):
    dw_in = gp["w_in"]
    dw_ukv = gp["w_ukv"]
    k_part = dw_ukv[:, :HP].reshape(KV_LORA, N_HEADS, HEAD_PAD)[..., :QK_NOPE]
    v_part = dw_ukv[:, HP:].reshape(KV_LORA, N_HEADS, HEAD_PAD)[..., :V_HEAD]
    first = lambda name: gp[name][0]
    col_shards = lambda a: a.reshape(a.shape[0], N_CHIPS, -1).transpose(1, 0, 2)
    return dict(
        w_in=col_shards(jnp.concatenate([dw_in[:, :640], dw_in[:, Z_KPE + ROPE_LO:Z_KPE + ROPE_HI], dw_in[:, Z_GB:]], axis=1)),
        w_uq=col_shards(_unpad_heads(gp["w_uq"], QK_HEAD)),
        w_ukv=col_shards(jnp.concatenate([k_part, v_part], axis=-1).reshape(KV_LORA, -1)),
        w_o=gp["w_o"], w_up=gp["w_up"], w_down=gp["w_down"], w_ple_gate=gp["w_ple_gate"], w_ple=gp["w_ple"],
        g_mix=first("g_mix"), g_q_lat=first("g_q_lat"), g_kv_lat=first("g_kv_lat"), g_mlp=first("g_mlp"),
        g_ple=first("g_ple"), g_out_conv=first("g_out_conv"), g_out_attn=_unpad_heads(first("g_out_attn"), V_HEAD),
        g_qn_nope=gp["gq"][0, :QK_NOPE], g_qn_rope=gp["gq"][0, QK_NOPE:QK_HEAD],
        g_kn_nope=gp["gk"][0, :QK_NOPE], g_kn_rope=gp["gk"][0, QK_NOPE:QK_HEAD],
        conv_w=gp["cw8"][:CONV_TAPS],
    )


def _rope_tables(positions):
    inv_freq = 1.0 / (ROPE_THETA ** (jnp.arange(0, QK_ROPE, 2, dtype=F32) / QK_ROPE))
    ang = positions.astype(F32)[:, None] * inv_freq
    cos, sin = jnp.cos(ang), jnp.sin(ang)
    pad = lambda t, v: jnp.pad(jnp.concatenate([t, t], axis=1), ((0, 0), (ROPE_LO, HEAD_PAD - ROPE_HI)), constant_values=v)
    return pad(cos, 1.0), pad(sin, 0.0)


def _layer_fwd(x0, h, p_l, W, cs, sn, attend, g_next):
    z = _mm(h, W["w_in"], name="mm_in")
    qln, kvln = _lat_fwd(z, W["g_q_lat"], W["g_kv_lat"])
    q_raw = _mm(qln, W["w_uq"], name="mm_uq")
    kv_raw = _mm(kvln, W["w_ukv"], name="mm_ukv")
    qf, kf, vb = _qk_fwd(q_raw, kv_raw, z, cs, sn, W["gq"], W["gk"], *W["head_mats"])
    o, lse = attend(qf, kf, vb)
    conv = _conv_fwd(z, W["cw8"])
    mixed = _mix_fwd(o, conv, W["g_out_attn"], W["g_out_conv"])
    x1, h2 = _mm(mixed, W["w_o"], res=x0, gain=W["g_mlp"], epi="rms", name="mm_o")
    a, f = W["host_mm_up"](lambda rider: _mm(h2, W["w_up"], epi="relu2", out_dtype=BF16, rider=rider, name="mm_up"))
    x2, h3 = _mm(f, W["w_down"], res=x1, gain=W["g_ple"], epi="rms", name="mm_down")
    gl = _mm(h3, W["w_ple_gate"], name="mm_ple_gate")
    pe = _mm(p_l, W["w_ple"], name="mm_ple")
    x3, h_next = _ple_fwd(x2, gl, pe, g_next)
    saved = dict(x0=x0, h=h, z=z, qln=qln, kvln=kvln, q_raw=q_raw, kv_raw=kv_raw, qf=qf, kf=kf, vb=vb, o=o, lse=lse,
                 conv=conv, mixed=mixed, x1=x1, h2=h2, a=a, f=f, x2=x2, h3=h3, gl=gl, pe=pe)
    return x3, h_next, saved


def _layer_bwd(dx3, p_l, W, cs, sn, sv, attend_bwd):
    g = {}
    dpe, dgl = _ple_bwd(dx3, sv["gl"], sv["pe"])
    g["w_ple"] = _mm(p_l, dpe, mode="tn", out_dtype=BF16, shard_out=1, name="mm_dw_ple")
    g["w_ple_gate"] = _mm(sv["h3"], dgl, mode="tn", out_dtype=BF16, shard_out=0, name="mm_dw_ple_gate")
    dx2, dx2b, g["g_ple"] = _mm(dgl, W["w_ple_gate"], mode="nt", res=dx3, aux=sv["x2"], gain=W["g_ple"], epi="rms_bwd",
                                name="mm_dh3")
    da, = W["host_mm_da"](lambda rider: _mm(dx2b, W["w_down"], mode="nt", aux=sv["a"], epi="drelu2", out_dtype=BF16,
                                            rider=rider, name="mm_da"))
    g["w_down"] = _mm(sv["f"], dx2b, mode="tn", out_dtype=BF16, shard_out=0, name="mm_dw_down")
    g["w_up"] = _mm(sv["h2"], da, mode="tn", out_dtype=BF16, shard_out=1, name="mm_dw_up")
    dx1, dx1b, g["g_mlp"] = W["host_mm_dh2"](g, lambda rider: _mm(da, W["w_up"], mode="nt", res=dx2, aux=sv["x1"], gain=W["g_mlp"],
                                                                  epi="rms_bwd", rider=rider, name="mm_dh2"))
    dmixed = _mm(dx1b, W["w_o"], mode="nt", name="mm_dmixed")
    g["w_o"] = _mm(sv["mixed"], dx1b, mode="tn", out_dtype=BF16, name="mm_dw_o")
    do, dconv, g["g_out_attn"], g["g_out_conv"] = _mix_bwd(dmixed, sv["o"], sv["conv"], W["g_out_attn"], W["g_out_conv"])
    dgb, dgc, dxin, g["cw8"] = _conv_bwd(dconv, sv["z"], W["cw8"])
    dqf, dkf, dv = attend_bwd(g, sv["qf"], sv["kf"], sv["vb"], sv["o"], do, sv["lse"])
    dq_raw, dkv_raw, dkpe, g["gq"], g["gk"] = _qk_bwd(dqf, dkf, dv, sv["q_raw"], sv["kv_raw"], sv["z"], cs, sn, W["gq"], W["gk"],
                                                      *W["head_mats"])
    g["w_uq"] = _mm(sv["qln"], dq_raw, mode="tn", out_dtype=BF16, name="mm_dw_uq")
    dqln = _mm(dq_raw, W["w_uq"], mode="nt", name="mm_dqln")
    g["w_ukv"] = _mm(sv["kvln"], dkv_raw, mode="tn", out_dtype=BF16, name="mm_dw_ukv")
    dkvln = _mm(dkv_raw, W["w_ukv"], mode="nt", name="mm_dkvln")
    dlat, g["g_q_lat"], g["g_kv_lat"] = _lat_bwd(dqln, dkvln, sv["z"], W["g_q_lat"], W["g_kv_lat"])
    dz = jnp.concatenate([dlat, dkpe, dgb, dgc, dxin], axis=1)
    g["w_in"] = _mm(sv["h"], dz, mode="tn", out_dtype=BF16, name="mm_dw_in")
    dx0, _, g["g_mix"] = _mm(dz, W["w_in"], mode="nt", res=dx1, aux=sv["x0"], gain=W["g_mix"], epi="rms_bwd", name="mm_dh")
    return dx0, g


def _local_step(x, p, positions, target, slabs, small):
    depth = p.shape[0]
    cs, sn = _rope_tables(positions)
    slabs = dict(slabs)

    def gather(items, host):
        touched = [n for n in PRE + POST if any(n == name for name, _ in items)]
        rider = _allgather_rider([slabs[n] for n in touched], [(touched.index(name), l) for name, l in items])
        if host is None:
            out, new = (), rider.run("allgather_first")
        else:
            *out, new = host(rider)
        slabs.update(zip(touched, new))
        return out

    gather([(name, 0) for name in PRE], None)
    Ws, saved = [], []
    gains = [_gains(small, l) for l in range(depth)]
    h = _rms_fwd(x, gains[0]["g_mix"], "rms_mix")
    for l in range(depth):
        W = dict(gains[l], **_pre_weights(slabs, l))

        def attend(qf, kf, vb, W=W, l=l):
            o, lse = gather([(name, l) for name in POST], functools.partial(_attn_fwd, qf, kf, vb))
            W.update(_post_weights(slabs, l))
            return o, lse

        if l + 1 < depth:
            W["host_mm_up"] = lambda run, l=l: gather([(name, l + 1) for name in PRE], run)
        else:
            W["host_mm_up"] = lambda run: run(None)

        g_next = gains[l + 1]["g_mix"] if l + 1 < depth else jnp.ones_like(gains[l]["g_mix"])
        x, h, sv = _layer_fwd(x, h, p[l], W, cs, sn, attend, g_next)
        Ws.append(W)
        saved.append(sv)
    dx, sq = _loss_grad(x, target)

    landing = {name: lax.empty((depth, N_DEV - 1, _shard_dims(slabs[name])[0] // 2, _shard_dims(slabs[name])[1]), BF16)
               for name in PRE + POST}

    def exchange(sends, host):
        touched = [n for n in PRE + POST if any(n == name for name, _, _ in sends)]
        rider = _exchange_rider([g for _, _, g in sends], [landing[n] for n in touched],
                                [(i, touched.index(name), l) for i, (name, l, _) in enumerate(sends)])
        if host is None:
            out, new = (), rider.run("grad_exchange_last")
        else:
            *out, new = host(rider)
        landing.update(zip(touched, new))
        return out

    grads = [None] * depth
    for l in reversed(range(depth)):
        W = dict(Ws[l], w_down=(slabs["w_down"], l), w_ple_gate=(slabs["w_ple_gate"], l))

        def attend_bwd(g, qf, kf, vb, o, do, lse, l=l):
            g["w_o"] = _w_o_shards(g["w_o"])
            sends = [(name, l, g[name]) for name in POST[:3]]
            if l + 1 < depth:
                sends += [(name, l + 1, grads[l + 1][name]) for name in PRE[1:]]
            return exchange(sends, functools.partial(_attn_bwd, qf, kf, vb, o, do, lse))

        W["host_mm_dh2"] = lambda g, run, l=l: exchange([(name, l, g[name]) for name in POST[3:]], run)
        if l + 1 < depth:
            W["host_mm_da"] = lambda run, l=l: exchange([(PRE[0], l + 1, grads[l + 1][PRE[0]])], run)
        else:
            W["host_mm_da"] = lambda run: (run(None),)

        dx, gp = _layer_bwd(dx, p[l], W, cs, sn, saved[l], attend_bwd)
        grads[l] = _unpad_grads(gp)
    exchange([(name, 0, grads[0][name]) for name in PRE], None)
    return sq, dx, grads, landing


def _pack_rows(vals, entries):
    depth = vals[entries[0][0]].shape[0]
    return jnp.concatenate([jnp.pad(vals[name].reshape(depth, -1), ((0, 0), (0, _pad128(n) - n))) for name, n in entries], axis=1)


def _unpack_rows(packed, entries):
    out, off = {}, 0
    for name, n in entries:
        out[name] = packed[:, off:off + n]
        off += _pad128(n)
    return out


def kernel(x, p, positions, g_mix, w_in, g_q_lat, w_uq, g_kv_lat, w_ukv, g_qn_nope, g_qn_rope, g_kn_nope, g_kn_rope, conv_w, g_out_attn, g_out_conv, w_o, g_mlp, w_up, w_down, g_ple, w_ple_gate, w_ple, loss_target, m_g_mix, m_w_in, m_g_q_lat, m_w_uq, m_g_kv_lat, m_w_ukv, m_g_qn_nope, m_g_qn_rope, m_g_kn_nope, m_g_kn_rope, m_conv_w, m_g_out_attn, m_g_out_conv, m_w_o, m_g_mlp, m_w_up, m_w_down, m_g_ple, m_w_ple_gate, m_w_ple, v_g_mix, v_w_in, v_g_q_lat, v_w_uq, v_g_kv_lat, v_w_ukv, v_g_qn_nope, v_g_qn_rope, v_g_kn_nope, v_g_kn_rope, v_conv_w, v_g_out_attn, v_g_out_conv, v_w_o, v_g_mlp, v_w_up, v_w_down, v_g_ple, v_w_ple_gate, v_w_ple):
    w = dict(g_mix=g_mix, w_in=w_in, g_q_lat=g_q_lat, w_uq=w_uq, g_kv_lat=g_kv_lat, w_ukv=w_ukv, g_qn_nope=g_qn_nope,
             g_qn_rope=g_qn_rope, g_kn_nope=g_kn_nope, g_kn_rope=g_kn_rope, conv_w=conv_w, g_out_attn=g_out_attn,
             g_out_conv=g_out_conv, w_o=w_o, g_mlp=g_mlp, w_up=w_up, w_down=w_down, g_ple=g_ple, w_ple_gate=w_ple_gate,
             w_ple=w_ple)
    m = dict(g_mix=m_g_mix, w_in=m_w_in, g_q_lat=m_g_q_lat, w_uq=m_w_uq, g_kv_lat=m_g_kv_lat, w_ukv=m_w_ukv,
             g_qn_nope=m_g_qn_nope, g_qn_rope=m_g_qn_rope, g_kn_nope=m_g_kn_nope, g_kn_rope=m_g_kn_rope, conv_w=m_conv_w,
             g_out_attn=m_g_out_attn, g_out_conv=m_g_out_conv, w_o=m_w_o, g_mlp=m_g_mlp, w_up=m_w_up, w_down=m_w_down,
             g_ple=m_g_ple, w_ple_gate=m_w_ple_gate, w_ple=m_w_ple)
    v = dict(g_mix=v_g_mix, w_in=v_w_in, g_q_lat=v_g_q_lat, w_uq=v_w_uq, g_kv_lat=v_g_kv_lat, w_ukv=v_w_ukv,
             g_qn_nope=v_g_qn_nope, g_qn_rope=v_g_qn_rope, g_kn_nope=v_g_kn_nope, g_kn_rope=v_g_kn_rope, conv_w=v_conv_w,
             g_out_attn=v_g_out_attn, g_out_conv=v_g_out_conv, w_o=v_w_o, g_mlp=v_g_mlp, w_up=v_w_up, w_down=v_w_down,
             g_ple=v_g_ple, w_ple_gate=v_w_ple_gate, w_ple=v_w_ple)
    depth = p.shape[0]
    ax, ay, ac = _place()
    chip = 2 * ax + ay
    c_arr = jnp.reshape(ac, (1,)).astype(jnp.int32)
    chip_arr = jnp.reshape(chip, (1,)).astype(jnp.int32)
    conv_shard =("conv_w", CONV_TAPS * CONV_WIDTH // N_CHIPS)
    conv_full = ("conv_w", CONV_TAPS * CONV_WIDTH)

    names = PRE + POST
    slabs = {name: _shard_slab(w[name], chip_arr, "shard_slab_" + name, name in ROW_SHARDED) for name in names}
    conv_rows = -(-depth * CONV_TAPS // SUBLANES) * SUBLANES
    conv_all = _allgather_small(jnp.pad(conv_w.reshape(depth * CONV_TAPS, LANES), ((0, conv_rows - depth * CONV_TAPS), (0, 0))),
                                "allgather_conv_w")
    conv_cat = jnp.concatenate([conv_all[2 * j, :depth * CONV_TAPS] for j in range(N_CHIPS)], axis=1)
    small = {name: w[name] for name, _ in SMALL}
    small["conv_w"] = conv_cat.reshape(depth, CONV_TAPS, CONV_WIDTH)

    sq, grad_x, grads, landing = _local_step(x[0], p[:, 0], positions[0], loss_target[0], slabs, small)

    halves = [_reduce_partials(jnp.stack([grads[l][name] for l in range(depth)]), landing[name], chip_arr, c_arr,
                               "grad_reduce_" + name) for name in names]
    g_out = dict(zip(names, _join_halves(halves)))

    reduced = SMALL + (conv_full, ("loss", LANES))
    stacked = {name: jnp.stack([grads[l][name] for l in range(depth)]) for name, _ in SMALL + (conv_full,)}
    stacked["loss"] = jnp.broadcast_to(sq[:1] * (0.5 / D_MODEL), (depth, LANES))
    g_small = _unpack_rows(_sum_devices(_allgather_small(_pack_rows(stacked, reduced), "allgather_small_grads")), reduced)
    loss = g_small["loss"][0, 0]
    for name, n in SMALL:
        g_out[name] = g_small[name]
    g_conv = g_small["conv_w"].reshape(depth, CONV_TAPS, CONV_WIDTH)
    g_out["conv_w"] = lax.dynamic_slice_in_dim(g_conv, chip * LANES, LANES, axis=2)

    delta, new_m, new_v = {}, {}, {}
    for name in names:
        view = (lambda t: jnp.swapaxes(t, 1, 2)) if w[name].shape[2] % LANES else (lambda t: t)
        d, m2, v2 = _adamw(view(w[name]), view(g_out[name]), view(m[name]), view(v[name]), "adamw_" + name)
        delta[name], new_m[name], new_v[name] = view(d), view(m2), view(v2)
    local = SMALL + (conv_shard,)
    pack = lambda vals: _pack_rows(vals, local)[None]
    d, m2, v2 = _adamw(pack(w), pack(g_out), pack(m), pack(v), "adamw_small")
    for res, packed_res in ((delta, d), (new_m, m2), (new_v, v2)):
        un = _unpack_rows(packed_res[0], local)
        for name, _ in local:
            res[name] = un[name].reshape(w[name].shape)

    return (loss, grad_x[None], *[g_out[n] for n in WEIGHT_ORDER], *[delta[n] for n in WEIGHT_ORDER],
            *[new_m[n] for n in WEIGHT_ORDER], *[new_v[n] for n in WEIGHT_ORDER])
```
